```python
import math
import jax, jax.numpy as jnp
from jax import lax
import numpy as np

D_MODEL = 1024
BATCH = 8
SEQ = 4096
DEPTH = 2

N_MIXERS = 2
N_A = (DEPTH + 1) // 2
N_B = DEPTH // 2
HG_HEAD_DIM = 128
HG_HEADS = D_MODEL // HG_HEAD_DIM
HG_DIM = HG_HEADS * HG_HEAD_DIM
HG_CHUNK = 64
D_INNER = 2 * D_MODEL
SSM_HEAD_DIM = 64
SSM_HEADS = D_INNER // SSM_HEAD_DIM
SSM_GROUPS = 8
SSM_HPG = SSM_HEADS // SSM_GROUPS
SSM_STATE = 128
SSM_CONV = 5
SSD_CHUNK = 128
GN = SSM_GROUPS * SSM_STATE
CONV_DIM = D_INNER + 2 * GN
B_PROJ = 2 * D_INNER + 2 * GN + 2 * SSM_HEADS
D_FF = ((8 * D_MODEL // 3 + 255) // 256) * 256
FFN_CONV = 3
EPS = 1e-6

kernel_name = "hgrn2_mamba2_convglu_bidir_hybrid"


def rmsnorm(x, w):
    xf = x.astype(jnp.float32)
    y = xf * lax.rsqrt(jnp.mean(xf * xf, axis=-1, keepdims=True) + EPS)
    return (y * w.astype(jnp.float32)).astype(x.dtype)


def group_rmsnorm(x, w, groups):
    shp = x.shape
    xf = x.astype(jnp.float32).reshape(*shp[:-1], groups, shp[-1] // groups)
    y = xf * lax.rsqrt(jnp.mean(xf * xf, axis=-1, keepdims=True) + EPS)
    y = y.reshape(shp) * w.astype(jnp.float32)
    return y.astype(x.dtype)


def dwconv_centred(x, w, b):
    ch = x.shape[-1]
    y = lax.conv_general_dilated(x, w[:, None, :].astype(x.dtype), window_strides=(1,), padding='SAME',
                                 dimension_numbers=('NWC', 'WIO', 'NWC'), feature_group_count=ch)
    return y + b.astype(x.dtype)


def flip(t):
    return jnp.flip(t, axis=1)


def gla_chunked(q, k, v, logf):
    bsz, seq, h, dk = q.shape
    dv = v.shape[-1]
    nc = seq // HG_CHUNK

    def chunks(t):
        return t.reshape(bsz, nc, HG_CHUNK, h, t.shape[-1]).transpose(1, 0, 3, 2, 4)

    lower = jnp.tril(jnp.ones((HG_CHUNK, HG_CHUNK), dtype=bool))

    def step(state, inp):
        qc, kc, vc, gc = inp
        b = jnp.cumsum(gc, axis=2)
        o_inter = jnp.einsum('bhtk,bhkv->bhtv', qc * jnp.exp(b), state)
        rel = jnp.where(lower[:, :, None], b[:, :, :, None, :] - b[:, :, None, :, :], -jnp.inf)
        att = jnp.einsum('bhtsk,bhsk->bhts', qc[:, :, :, None, :] * jnp.exp(rel), kc)
        o = o_inter + jnp.einsum('bhts,bhsv->bhtv', att, vc)
        b_last = b[:, :, -1:, :]
        state = (jnp.exp(b_last)[:, :, 0, :, None] * state
                 + jnp.einsum('bhsk,bhsv->bhkv', kc * jnp.exp(b_last - b), vc))
        return state, o

    state0 = jnp.zeros((bsz, h, dk, dv), q.dtype)
    _, o = lax.scan(step, state0, (chunks(q), chunks(k), chunks(v), chunks(logf)))
    return o.transpose(1, 0, 3, 2, 4).reshape(bsz, seq, h, dv)


def hgrn2_mixer(u, w_in, lb, norm_w, w_out):
    bsz, seq, _ = u.shape
    q, f_fw, f_bw, iv, g = jnp.split(u @ w_in, 5, axis=-1)

    def heads(t):
        return t.reshape(bsz, seq, HG_HEADS, HG_HEAD_DIM)

    def gate(fr):
        f = lb + (1.0 - lb) * jax.nn.sigmoid(fr.astype(jnp.float32))
        return heads(jnp.log(f)).astype(u.dtype), heads(1.0 - f).astype(u.dtype)

    q = heads(jax.nn.silu(q))
    iv = heads(iv)
    logf_fw, k_fw = gate(f_fw)
    logf_bw, k_bw = gate(f_bw)
    o = gla_chunked(q, k_fw, iv, logf_fw) + flip(gla_chunked(flip(q), flip(k_bw), flip(iv), flip(logf_bw)))
    o = rmsnorm(o, norm_w) * jax.nn.silu(heads(g))
    return o.reshape(bsz, seq, HG_DIM) @ w_out


def ssd_chunked(x, dt, a, bm, cm):
    bsz, seq = x.shape[:2]
    nc = seq // SSD_CHUNK
    c = SSD_CHUNK
    xd = (x * dt[..., None]).reshape(bsz, nc, c, SSM_GROUPS, SSM_HPG, SSM_HEAD_DIM)
    la = (dt * a).reshape(bsz, nc, c, SSM_GROUPS, SSM_HPG).transpose(0, 3, 4, 1, 2)
    a_cum = jnp.cumsum(la, axis=-1)
    bc = bm.reshape(bsz, nc, c, SSM_GROUPS, SSM_STATE)
    cc = cm.reshape(bsz, nc, c, SSM_GROUPS, SSM_STATE)
    lower = jnp.tril(jnp.ones((c, c), dtype=bool))
    lmat = jnp.exp(jnp.where(lower, a_cum[..., :, None] - a_cum[..., None, :], -jnp.inf))
    cb = jnp.einsum('bclgn,bcsgn->bgcls', cc, bc)
    y_diag = jnp.einsum('bgjcls,bcsgjp->bclgjp', cb[:, :, None] * lmat, xd)
    decay_s = jnp.exp(a_cum[..., -1:] - a_cum).transpose(0, 3, 4, 1, 2)[..., None]
    states = jnp.einsum('bcsgn,bcsgjp->bcgjpn', bc, xd * decay_s)
    a_last = a_cum[..., -1]
    a_cs = jnp.cumsum(a_last, axis=-1)
    a_excl = a_cs - a_last
    before = jnp.tril(jnp.ones((nc, nc), dtype=bool), k=-1)
    w = jnp.exp(jnp.where(before, a_excl[..., :, None] - a_cs[..., None, :], -jnp.inf))
    h_in = jnp.einsum('bgjzc,bcgjpn->bzgjpn', w, states)
    y_off = (jnp.einsum('bzlgn,bzgjpn->bzlgjp', cc, h_in)
             * jnp.exp(a_cum).transpose(0, 3, 4, 1, 2)[..., None])
    return (y_diag + y_off).reshape(bsz, seq, SSM_GROUPS, SSM_HPG, SSM_HEAD_DIM)


def mamba2_mixer(u, w_in, conv_w, conv_b, dt_bias, a_log, d_skip, norm_w, w_out):
    bsz, seq, _ = u.shape
    z, xbc, dt_raw = jnp.split(u @ w_in, [D_INNER, D_INNER + CONV_DIM], axis=-1)
    xbc = jax.nn.silu(dwconv_centred(xbc, conv_w, conv_b))
    xs, bm, cm = jnp.split(xbc, [D_INNER, D_INNER + GN], axis=-1)
    xs = xs.reshape(bsz, seq, SSM_GROUPS, SSM_HPG, SSM_HEAD_DIM)
    bm = bm.reshape(bsz, seq, SSM_GROUPS, SSM_STATE)
    cm = cm.reshape(bsz, seq, SSM_GROUPS, SSM_STATE)
    dt = jax.nn.softplus(dt_raw.reshape(bsz, seq, 2, SSM_HEADS) + dt_bias)
    a = -jnp.exp(a_log).reshape(2, SSM_GROUPS, SSM_HPG)
    dt_fw = dt[:, :, 0].reshape(bsz, seq, SSM_GROUPS, SSM_HPG)
    dt_bw = dt[:, :, 1].reshape(bsz, seq, SSM_GROUPS, SSM_HPG)
    y = (ssd_chunked(xs, dt_fw, a[0], bm, cm)
         + flip(ssd_chunked(flip(xs), flip(dt_bw), a[1], flip(bm), flip(cm)))
         + xs * d_skip.reshape(SSM_GROUPS, SSM_HPG, 1))
    y = y.reshape(bsz, seq, D_INNER) * jax.nn.silu(z)
    y = group_rmsnorm(y, norm_w, SSM_GROUPS)
    return y @ w_out


def conv_glu(u, w_in, conv_w, conv_b, w_out):
    gate, val = jnp.split(u @ w_in, 2, axis=-1)
    return (jax.nn.silu(dwconv_centred(gate, conv_w, conv_b)) * val) @ w_out


def _fwd_setup_inputs(seed: int = 0) -> dict:
    key = jax.random.key(seed)
    ks = jax.random.split(key, 20)

    def nrm(k, shape, scale):
        return scale * jax.random.normal(k, shape, jnp.float32)

    dt = jnp.exp(jax.random.uniform(ks[10], (N_B, 2, SSM_HEADS), jnp.float32)
                 * (math.log(0.1) - math.log(1e-3)) + math.log(1e-3))
    return {
        "x": nrm(ks[0], (BATCH, SEQ, D_MODEL), 1.0),
        "norm1_w": 1.0 + nrm(ks[1], (DEPTH, D_MODEL), 0.02),
        "norm2_w": 1.0 + nrm(ks[2], (DEPTH, D_MODEL), 0.02),
        "a_w_in": nrm(ks[3], (N_A, D_MODEL, 5 * HG_DIM), D_MODEL ** -0.5),
        "a_lb_logits": nrm(ks[4], (DEPTH + 1, HG_DIM), 0.1),
        "a_norm_w": 1.0 + nrm(ks[5], (N_A, HG_HEAD_DIM), 0.02),
        "a_w_out": nrm(ks[6], (N_A, HG_DIM, D_MODEL), HG_DIM ** -0.5),
        "b_w_in": nrm(ks[7], (N_B, D_MODEL, B_PROJ), D_MODEL ** -0.5),
        "b_conv_w": nrm(ks[8], (N_B, SSM_CONV, CONV_DIM), SSM_CONV ** -0.5),
        "b_conv_b": nrm(ks[9], (N_B, CONV_DIM), 0.02),
        "b_dt_bias": dt + jnp.log(-jnp.expm1(-dt)),
        "b_a_log": jnp.log(jax.random.uniform(ks[11], (N_B, 2, SSM_HEADS), jnp.float32, 1.0, 16.0)),
        "b_d_skip": 1.0 + nrm(ks[12], (N_B, SSM_HEADS), 0.02),
        "b_norm_w": 1.0 + nrm(ks[13], (N_B, D_INNER), 0.02),
        "b_w_out": nrm(ks[14], (N_B, D_INNER, D_MODEL), D_INNER ** -0.5),
        "ffn_w_in": nrm(ks[15], (DEPTH, D_MODEL, 2 * D_FF), D_MODEL ** -0.5),
        "ffn_conv_w": nrm(ks[16], (DEPTH, FFN_CONV, D_FF), FFN_CONV ** -0.5),
        "ffn_conv_b": nrm(ks[17], (DEPTH, D_FF), 0.02),
        "ffn_w_out": nrm(ks[18], (DEPTH, D_FF, D_MODEL), D_FF ** -0.5),
        "final_norm_w": 1.0 + nrm(ks[19], (D_MODEL,), 0.02),
    }


def _fwd_reference(x, norm1_w, norm2_w, a_w_in, a_lb_logits, a_norm_w, a_w_out, b_w_in, b_conv_w, b_conv_b,
              b_dt_bias, b_a_log, b_d_skip, b_norm_w, b_w_out, ffn_w_in, ffn_conv_w, ffn_conv_b, ffn_w_out,
              final_norm_w):
    lower_bounds = jnp.cumsum(jax.nn.softmax(a_lb_logits.astype(jnp.float32), axis=0), axis=0)
    h = x
    for i in range(DEPTH):
        u = rmsnorm(h, norm1_w[i])
        j = i // N_MIXERS
        if i % N_MIXERS == 0:
            h = h + hgrn2_mixer(u, a_w_in[j], lower_bounds[i], a_norm_w[j], a_w_out[j])
        else:
            h = h + mamba2_mixer(u, b_w_in[j], b_conv_w[j], b_conv_b[j], b_dt_bias[j], b_a_log[j],
                                 b_d_skip[j], b_norm_w[j], b_w_out[j])
        h = h + conv_glu(rmsnorm(h, norm2_w[i]), ffn_w_in[i], ffn_conv_w[i], ffn_conv_b[i], ffn_w_out[i])
    return rmsnorm(h, final_norm_w)


import jax as _jax
import jax.numpy as _jnp

TWIN_FORMAT = 'train_step'
FWD_PARAMS = ['x', 'norm1_w', 'norm2_w', 'a_w_in', 'a_lb_logits', 'a_norm_w', 'a_w_out', 'b_w_in', 'b_conv_w', 'b_conv_b', 'b_dt_bias', 'b_a_log', 'b_d_skip', 'b_norm_w', 'b_w_out', 'ffn_w_in', 'ffn_conv_w', 'ffn_conv_b', 'ffn_w_out', 'final_norm_w']
TWIN_WEIGHTS = ['norm1_w', 'norm2_w', 'a_w_in', 'a_lb_logits', 'a_norm_w', 'a_w_out', 'b_w_in', 'b_conv_w', 'b_conv_b', 'b_dt_bias', 'b_a_log', 'b_d_skip', 'b_norm_w', 'b_w_out', 'ffn_w_in', 'ffn_conv_w', 'ffn_conv_b', 'ffn_w_out', 'final_norm_w']
TWIN_DIFF_INPUT = 'x'
TWIN_INPUTS = ['x', 'norm1_w', 'norm2_w', 'a_w_in', 'a_lb_logits', 'a_norm_w', 'a_w_out', 'b_w_in', 'b_conv_w', 'b_conv_b', 'b_dt_bias', 'b_a_log', 'b_d_skip', 'b_norm_w', 'b_w_out', 'ffn_w_in', 'ffn_conv_w', 'ffn_conv_b', 'ffn_w_out', 'final_norm_w', 'loss_target', 'm_norm1_w', 'm_norm2_w', 'm_a_w_in', 'm_a_lb_logits', 'm_a_norm_w', 'm_a_w_out', 'm_b_w_in', 'm_b_conv_w', 'm_b_conv_b', 'm_b_dt_bias', 'm_b_a_log', 'm_b_d_skip', 'm_b_norm_w', 'm_b_w_out', 'm_ffn_w_in', 'm_ffn_conv_w', 'm_ffn_conv_b', 'm_ffn_w_out', 'm_final_norm_w', 'v_norm1_w', 'v_norm2_w', 'v_a_w_in', 'v_a_lb_logits', 'v_a_norm_w', 'v_a_w_out', 'v_b_w_in', 'v_b_conv_w', 'v_b_conv_b', 'v_b_dt_bias', 'v_b_a_log', 'v_b_d_skip', 'v_b_norm_w', 'v_b_w_out', 'v_ffn_w_in', 'v_ffn_conv_w', 'v_ffn_conv_b', 'v_ffn_w_out', 'v_final_norm_w']
TWIN_OUTPUTS = ['loss', 'grad_x', 'grad_norm1_w', 'grad_norm2_w', 'grad_a_w_in', 'grad_a_lb_logits', 'grad_a_norm_w', 'grad_a_w_out', 'grad_b_w_in', 'grad_b_conv_w', 'grad_b_conv_b', 'grad_b_dt_bias', 'grad_b_a_log', 'grad_b_d_skip', 'grad_b_norm_w', 'grad_b_w_out', 'grad_ffn_w_in', 'grad_ffn_conv_w', 'grad_ffn_conv_b', 'grad_ffn_w_out', 'grad_final_norm_w', 'delta_norm1_w', 'delta_norm2_w', 'delta_a_w_in', 'delta_a_lb_logits', 'delta_a_norm_w', 'delta_a_w_out', 'delta_b_w_in', 'delta_b_conv_w', 'delta_b_conv_b', 'delta_b_dt_bias', 'delta_b_a_log', 'delta_b_d_skip', 'delta_b_norm_w', 'delta_b_w_out', 'delta_ffn_w_in', 'delta_ffn_conv_w', 'delta_ffn_conv_b', 'delta_ffn_w_out', 'delta_final_norm_w', 'new_m_norm1_w', 'new_m_norm2_w', 'new_m_a_w_in', 'new_m_a_lb_logits', 'new_m_a_norm_w', 'new_m_a_w_out', 'new_m_b_w_in', 'new_m_b_conv_w', 'new_m_b_conv_b', 'new_m_b_dt_bias', 'new_m_b_a_log', 'new_m_b_d_skip', 'new_m_b_norm_w', 'new_m_b_w_out', 'new_m_ffn_w_in', 'new_m_ffn_conv_w', 'new_m_ffn_conv_b', 'new_m_ffn_w_out', 'new_m_final_norm_w', 'new_v_norm1_w', 'new_v_norm2_w', 'new_v_a_w_in', 'new_v_a_lb_logits', 'new_v_a_norm_w', 'new_v_a_w_out', 'new_v_b_w_in', 'new_v_b_conv_w', 'new_v_b_conv_b', 'new_v_b_dt_bias', 'new_v_b_a_log', 'new_v_b_d_skip', 'new_v_b_norm_w', 'new_v_b_w_out', 'new_v_ffn_w_in', 'new_v_ffn_conv_w', 'new_v_ffn_conv_b', 'new_v_ffn_w_out', 'new_v_final_norm_w']
TWIN_LEAF_KINDS = {'loss': 'loss', 'grad_x': 'grad_x', 'grad_norm1_w': 'grad_w', 'grad_norm2_w': 'grad_w', 'grad_a_w_in': 'grad_w', 'grad_a_lb_logits': 'grad_w', 'grad_a_norm_w': 'grad_w', 'grad_a_w_out': 'grad_w', 'grad_b_w_in': 'grad_w', 'grad_b_conv_w': 'grad_w', 'grad_b_conv_b': 'grad_w', 'grad_b_dt_bias': 'grad_w', 'grad_b_a_log': 'grad_w', 'grad_b_d_skip': 'grad_w', 'grad_b_norm_w': 'grad_w', 'grad_b_w_out': 'grad_w', 'grad_ffn_w_in': 'grad_w', 'grad_ffn_conv_w': 'grad_w', 'grad_ffn_conv_b': 'grad_w', 'grad_ffn_w_out': 'grad_w', 'grad_final_norm_w': 'grad_w', 'delta_norm1_w': 'delta_w', 'delta_norm2_w': 'delta_w', 'delta_a_w_in': 'delta_w', 'delta_a_lb_logits': 'delta_w', 'delta_a_norm_w': 'delta_w', 'delta_a_w_out': 'delta_w', 'delta_b_w_in': 'delta_w', 'delta_b_conv_w': 'delta_w', 'delta_b_conv_b': 'delta_w', 'delta_b_dt_bias': 'delta_w', 'delta_b_a_log': 'delta_w', 'delta_b_d_skip': 'delta_w', 'delta_b_norm_w': 'delta_w', 'delta_b_w_out': 'delta_w', 'delta_ffn_w_in': 'delta_w', 'delta_ffn_conv_w': 'delta_w', 'delta_ffn_conv_b': 'delta_w', 'delta_ffn_w_out': 'delta_w', 'delta_final_norm_w': 'delta_w', 'new_m_norm1_w': 'new_m', 'new_m_norm2_w': 'new_m', 'new_m_a_w_in': 'new_m', 'new_m_a_lb_logits': 'new_m', 'new_m_a_norm_w': 'new_m', 'new_m_a_w_out': 'new_m', 'new_m_b_w_in': 'new_m', 'new_m_b_conv_w': 'new_m', 'new_m_b_conv_b': 'new_m', 'new_m_b_dt_bias': 'new_m', 'new_m_b_a_log': 'new_m', 'new_m_b_d_skip': 'new_m', 'new_m_b_norm_w': 'new_m', 'new_m_b_w_out': 'new_m', 'new_m_ffn_w_in': 'new_m', 'new_m_ffn_conv_w': 'new_m', 'new_m_ffn_conv_b': 'new_m', 'new_m_ffn_w_out': 'new_m', 'new_m_final_norm_w': 'new_m', 'new_v_norm1_w': 'new_v', 'new_v_norm2_w': 'new_v', 'new_v_a_w_in': 'new_v', 'new_v_a_lb_logits': 'new_v', 'new_v_a_norm_w': 'new_v', 'new_v_a_w_out': 'new_v', 'new_v_b_w_in': 'new_v', 'new_v_b_conv_w': 'new_v', 'new_v_b_conv_b': 'new_v', 'new_v_b_dt_bias': 'new_v', 'new_v_b_a_log': 'new_v', 'new_v_b_d_skip': 'new_v', 'new_v_b_norm_w': 'new_v', 'new_v_b_w_out': 'new_v', 'new_v_ffn_w_in': 'new_v', 'new_v_ffn_conv_w': 'new_v', 'new_v_ffn_conv_b': 'new_v', 'new_v_ffn_w_out': 'new_v', 'new_v_final_norm_w': 'new_v'}


def _forward(args):
    return _fwd_reference(*[args[k] for k in FWD_PARAMS])


def _output_shape():
    out = _jax.eval_shape(lambda: _forward(_fwd_setup_inputs(0)))
    return out.shape, out.dtype

N_MICROBATCH = 1
ADAM_LR = 0.001
ADAM_B1 = 0.9
ADAM_B2 = 0.999
ADAM_EPS = 1e-08
ADAM_WD = 0.01
ADAM_STEP = 10
PER_EXAMPLE_BATCH_AXIS = {'x': 0, 'loss_target': 0}
SHARED_INPUTS = []
_WEIGHT_DTYPES = {'norm1_w': _jnp.float32, 'norm2_w': _jnp.float32, 'a_w_in': _jnp.float32, 'a_lb_logits': _jnp.float32, 'a_norm_w': _jnp.float32, 'a_w_out': _jnp.float32, 'b_w_in': _jnp.float32, 'b_conv_w': _jnp.float32, 'b_conv_b': _jnp.float32, 'b_dt_bias': _jnp.float32, 'b_a_log': _jnp.float32, 'b_d_skip': _jnp.float32, 'b_norm_w': _jnp.float32, 'b_w_out': _jnp.float32, 'ffn_w_in': _jnp.float32, 'ffn_conv_w': _jnp.float32, 'ffn_conv_b': _jnp.float32, 'ffn_w_out': _jnp.float32, 'final_norm_w': _jnp.float32}
MOMENT_SCALE = {'norm1_w': 1.777250e-01, 'norm2_w': 1.252842e-01, 'a_w_in': 8.356249e-02, 'a_lb_logits': 5.201023e-03, 'a_norm_w': 3.739153e-01, 'a_w_out': 1.290856e-01, 'b_w_in': 6.918134e-02, 'b_conv_w': 5.965678e-02, 'b_conv_b': 1.008657e-01, 'b_dt_bias': 1.340237e-01, 'b_a_log': 1.810957e-01, 'b_d_skip': 3.033220e-01, 'b_norm_w': 8.312851e-02, 'b_w_out': 1.141220e-01, 'ffn_w_in': 5.295477e-02, 'ffn_conv_w': 5.443578e-02, 'ffn_conv_b': 5.066098e-02, 'ffn_w_out': 8.615333e-02, 'final_norm_w': 3.197507e+01}


def _to_microbatches(a, axis):
    t = _jnp.moveaxis(a, axis, 0)
    t = t.reshape((N_MICROBATCH, t.shape[0] // N_MICROBATCH) + t.shape[1:])
    return _jnp.moveaxis(t, 1, axis + 1)


def setup_inputs(seed: int = 0) -> dict:
    inp = _fwd_setup_inputs(seed)
    key = _jax.random.fold_in(_jax.random.key(seed), 7919)
    shape, _ = _output_shape()
    out = dict(inp)
    out["loss_target"] = _jax.random.normal(_jax.random.fold_in(key, 0), shape, _jnp.float32)
    for i, name in enumerate(TWIN_WEIGHTS):
        w = inp[name].astype(_jnp.float32)
        if MOMENT_SCALE is None:
            s = _jnp.sqrt(_jnp.mean(_jnp.square(w)) + 1e-30)
        else:
            s = MOMENT_SCALE[name]
        km, kv = _jax.random.split(_jax.random.fold_in(key, i + 1))
        out[name] = w
        out["m_" + name] = s * _jax.random.normal(km, w.shape, _jnp.float32)
        out["v_" + name] = (s * s) * _jax.random.uniform(kv, w.shape, _jnp.float32, 0.5, 1.5)
    if N_MICROBATCH > 1:
        for name, axis in PER_EXAMPLE_BATCH_AXIS.items():
            out[name] = _to_microbatches(out[name], axis)
    return {'x': out['x'], 'norm1_w': out['norm1_w'], 'norm2_w': out['norm2_w'], 'a_w_in': out['a_w_in'], 'a_lb_logits': out['a_lb_logits'], 'a_norm_w': out['a_norm_w'], 'a_w_out': out['a_w_out'], 'b_w_in': out['b_w_in'], 'b_conv_w': out['b_conv_w'], 'b_conv_b': out['b_conv_b'], 'b_dt_bias': out['b_dt_bias'], 'b_a_log': out['b_a_log'], 'b_d_skip': out['b_d_skip'], 'b_norm_w': out['b_norm_w'], 'b_w_out': out['b_w_out'], 'ffn_w_in': out['ffn_w_in'], 'ffn_conv_w': out['ffn_conv_w'], 'ffn_conv_b': out['ffn_conv_b'], 'ffn_w_out': out['ffn_w_out'], 'final_norm_w': out['final_norm_w'], 'loss_target': out['loss_target'], 'm_norm1_w': out['m_norm1_w'], 'm_norm2_w': out['m_norm2_w'], 'm_a_w_in': out['m_a_w_in'], 'm_a_lb_logits': out['m_a_lb_logits'], 'm_a_norm_w': out['m_a_norm_w'], 'm_a_w_out': out['m_a_w_out'], 'm_b_w_in': out['m_b_w_in'], 'm_b_conv_w': out['m_b_conv_w'], 'm_b_conv_b': out['m_b_conv_b'], 'm_b_dt_bias': out['m_b_dt_bias'], 'm_b_a_log': out['m_b_a_log'], 'm_b_d_skip': out['m_b_d_skip'], 'm_b_norm_w': out['m_b_norm_w'], 'm_b_w_out': out['m_b_w_out'], 'm_ffn_w_in': out['m_ffn_w_in'], 'm_ffn_conv_w': out['m_ffn_conv_w'], 'm_ffn_conv_b': out['m_ffn_conv_b'], 'm_ffn_w_out': out['m_ffn_w_out'], 'm_final_norm_w': out['m_final_norm_w'], 'v_norm1_w': out['v_norm1_w'], 'v_norm2_w': out['v_norm2_w'], 'v_a_w_in': out['v_a_w_in'], 'v_a_lb_logits': out['v_a_lb_logits'], 'v_a_norm_w': out['v_a_norm_w'], 'v_a_w_out': out['v_a_w_out'], 'v_b_w_in': out['v_b_w_in'], 'v_b_conv_w': out['v_b_conv_w'], 'v_b_conv_b': out['v_b_conv_b'], 'v_b_dt_bias': out['v_b_dt_bias'], 'v_b_a_log': out['v_b_a_log'], 'v_b_d_skip': out['v_b_d_skip'], 'v_b_norm_w': out['v_b_norm_w'], 'v_b_w_out': out['v_b_w_out'], 'v_ffn_w_in': out['v_ffn_w_in'], 'v_ffn_conv_w': out['v_ffn_conv_w'], 'v_ffn_conv_b': out['v_ffn_conv_b'], 'v_ffn_w_out': out['v_ffn_w_out'], 'v_final_norm_w': out['v_final_norm_w']}


def _loss(weights, diff, rest, loss_target):
    with _jax.named_scope("forward"):
        args = {**rest, TWIN_DIFF_INPUT: diff, **{k: w.astype(_WEIGHT_DTYPES[k]) for k, w in weights.items()}}
        y = _forward(args)
    with _jax.named_scope("loss_head"):
        err = _jnp.square(y.astype(_jnp.float32) - loss_target)
        return 0.5 * _jnp.sum(_jnp.mean(err, axis=-1)) if err.ndim else 0.5 * err


def _adamw(w, g, m, v):
    m = ADAM_B1 * m + (1.0 - ADAM_B1) * g
    v = ADAM_B2 * v + (1.0 - ADAM_B2) * _jnp.square(g)
    m_hat = m / (1.0 - ADAM_B1 ** ADAM_STEP)
    v_hat = v / (1.0 - ADAM_B2 ** ADAM_STEP)
    delta = -ADAM_LR * (m_hat / (_jnp.sqrt(v_hat) + ADAM_EPS) + ADAM_WD * w)
    return delta, m, v


def reference(x, norm1_w, norm2_w, a_w_in, a_lb_logits, a_norm_w, a_w_out, b_w_in, b_conv_w, b_conv_b, b_dt_bias, b_a_log, b_d_skip, b_norm_w, b_w_out, ffn_w_in, ffn_conv_w, ffn_conv_b, ffn_w_out, final_norm_w, loss_target, m_norm1_w, m_norm2_w, m_a_w_in, m_a_lb_logits, m_a_norm_w, m_a_w_out, m_b_w_in, m_b_conv_w, m_b_conv_b, m_b_dt_bias, m_b_a_log, m_b_d_skip, m_b_norm_w, m_b_w_out, m_ffn_w_in, m_ffn_conv_w, m_ffn_conv_b, m_ffn_w_out, m_final_norm_w, v_norm1_w, v_norm2_w, v_a_w_in, v_a_lb_logits, v_a_norm_w, v_a_w_out, v_b_w_in, v_b_conv_w, v_b_conv_b, v_b_dt_bias, v_b_a_log, v_b_d_skip, v_b_norm_w, v_b_w_out, v_ffn_w_in, v_ffn_conv_w, v_ffn_conv_b, v_ffn_w_out, v_final_norm_w):
    given = dict(x=x, norm1_w=norm1_w, norm2_w=norm2_w, a_w_in=a_w_in, a_lb_logits=a_lb_logits, a_norm_w=a_norm_w, a_w_out=a_w_out, b_w_in=b_w_in, b_conv_w=b_conv_w, b_conv_b=b_conv_b, b_dt_bias=b_dt_bias, b_a_log=b_a_log, b_d_skip=b_d_skip, b_norm_w=b_norm_w, b_w_out=b_w_out, ffn_w_in=ffn_w_in, ffn_conv_w=ffn_conv_w, ffn_conv_b=ffn_conv_b, ffn_w_out=ffn_w_out, final_norm_w=final_norm_w, loss_target=loss_target, m_norm1_w=m_norm1_w, m_norm2_w=m_norm2_w, m_a_w_in=m_a_w_in, m_a_lb_logits=m_a_lb_logits, m_a_norm_w=m_a_norm_w, m_a_w_out=m_a_w_out, m_b_w_in=m_b_w_in, m_b_conv_w=m_b_conv_w, m_b_conv_b=m_b_conv_b, m_b_dt_bias=m_b_dt_bias, m_b_a_log=m_b_a_log, m_b_d_skip=m_b_d_skip, m_b_norm_w=m_b_norm_w, m_b_w_out=m_b_w_out, m_ffn_w_in=m_ffn_w_in, m_ffn_conv_w=m_ffn_conv_w, m_ffn_conv_b=m_ffn_conv_b, m_ffn_w_out=m_ffn_w_out, m_final_norm_w=m_final_norm_w, v_norm1_w=v_norm1_w, v_norm2_w=v_norm2_w, v_a_w_in=v_a_w_in, v_a_lb_logits=v_a_lb_logits, v_a_norm_w=v_a_norm_w, v_a_w_out=v_a_w_out, v_b_w_in=v_b_w_in, v_b_conv_w=v_b_conv_w, v_b_conv_b=v_b_conv_b, v_b_dt_bias=v_b_dt_bias, v_b_a_log=v_b_a_log, v_b_d_skip=v_b_d_skip, v_b_norm_w=v_b_norm_w, v_b_w_out=v_b_w_out, v_ffn_w_in=v_ffn_w_in, v_ffn_conv_w=v_ffn_conv_w, v_ffn_conv_b=v_ffn_conv_b, v_ffn_w_out=v_ffn_w_out, v_final_norm_w=v_final_norm_w)
    weights = {n: given[n] for n in TWIN_WEIGHTS}
    shared = {n: given[n] for n in SHARED_INPUTS}
    per_example = {n: given[n] for n in ['x']}
    grad_fn = _jax.value_and_grad(_loss, argnums=(0, 1))

    def one_microbatch(ex, loss_target):
        ex = dict(ex)
        diff = ex.pop(TWIN_DIFF_INPUT)
        return grad_fn(weights, diff, {**shared, **ex}, loss_target)

    if N_MICROBATCH == 1:
        loss, (grad_w, grad_x) = one_microbatch(per_example, given["loss_target"])
    else:
        def body(carry, xs):
            loss_sum, grad_sum = carry
            l_k, (gw_k, gx_k) = one_microbatch(xs[0], xs[1])
            with _jax.named_scope("update"):
                return (loss_sum + l_k, _jax.tree.map(_jnp.add, grad_sum, gw_k)), gx_k

        init = (_jnp.zeros((), _jnp.float32), _jax.tree.map(_jnp.zeros_like, weights))
        (loss, grad_w), grad_x = _jax.lax.scan(body, init, (per_example, given["loss_target"]))
    with _jax.named_scope("update"):
        delta_w, new_m, new_v = {}, {}, {}
        for n in TWIN_WEIGHTS:
            delta_w[n], new_m[n], new_v[n] = _adamw(weights[n], grad_w[n], given["m_" + n], given["v_" + n])
    return (loss, grad_x, *[grad_w[n] for n in TWIN_WEIGHTS], *[delta_w[n] for n in TWIN_WEIGHTS],
            *[new_m[n] for n in TWIN_WEIGHTS], *[new_v[n] for n in TWIN_WEIGHTS])
```

```python
import functools

import jax
import jax.numpy as jnp
from jax import lax
from jax.experimental import pallas as pl
from jax.experimental.pallas import tpu as pltpu

F32, BF = jnp.float32, jnp.bfloat16
HI = lax.Precision.HIGHEST

D = 1024
EPS = 1e-6
HG_HEADS, HG_HD, HG_CHUNK, HG_SUB = 8, 128, 64, 16
D_INNER, SSM_HEADS, SSM_HD, SSM_GROUPS, SSM_HPG, SSM_N, SSD_CHUNK = 2048, 32, 64, 8, 4, 128, 128
CONV_DIM = D_INNER + 2 * SSM_GROUPS * SSM_N
B_PROJ = 2 * D_INNER + 2 * SSM_GROUPS * SSM_N + 2 * SSM_HEADS
B_PROJ_PAD = 6272
D_FF = 2816
NEG = -1e30
ROWS = 256
VMEM_LIMIT = 56 * 1024 * 1024

ADAM_LR, ADAM_B1, ADAM_B2, ADAM_EPS, ADAM_WD, ADAM_STEP = 0.001, 0.9, 0.999, 1e-08, 0.01, 10

MESH = pl.DeviceIdType.MESH


def _pick(n, cands):
    for c in cands:
        if n % c == 0:
            return c
    return n


def _vcall(name, fn, grid, ins, outs, acc=None, scratch=()):
    acc = acc or {}
    n_in, n_out, nd = len(ins), len(outs), len(grid)

    def body(*refs):
        in_refs, out_refs, scr = refs[:n_in], refs[n_in:n_in + n_out], refs[n_in + n_out:]
        res = fn(*[r[...] for r in in_refs], *scr)
        if not isinstance(res, (tuple, list)):
            res = (res,)
        for j, (o_ref, r) in enumerate(zip(out_refs, res)):
            mode = acc.get(j)
            if mode is None:
                o_ref[...] = r.astype(o_ref.dtype)
                continue
            first = pl.program_id(nd - 1) == 0
            if mode == "all":
                for ax in range(nd - 1):
                    first = jnp.logical_and(first, pl.program_id(ax) == 0)

            @pl.when(first)
            def _():
                o_ref[...] = r.astype(o_ref.dtype)

            @pl.when(jnp.logical_not(first))
            def _():
                o_ref[...] += r.astype(o_ref.dtype)

    out = pl.pallas_call(
        body,
        name=name,
        grid=grid,
        in_specs=[pl.BlockSpec(bs, im) for _, bs, im in ins],
        out_specs=[pl.BlockSpec(bs, im) for _, _, bs, im in outs],
        out_shape=[jax.ShapeDtypeStruct(s, dt) for s, dt, _, _ in outs],
        scratch_shapes=list(scratch),
        compiler_params=pltpu.CompilerParams(
            dimension_semantics=("arbitrary",) * nd, vmem_limit_bytes=VMEM_LIMIT),
    )(*[a for a, _, _ in ins])
    return out[0] if n_out == 1 else out


def _mm(name, a, b, kind, out_dtype=F32, add=None):
    if kind == "tn":
        m, k = a.shape
        _, n = b.shape
        tm = _pick(m, (1024, 512, 256))
        tk = _pick(k, (1024, 1408, 512, 256, 128))
        tn = _pick(n, (512, 896, 256, 128))

        def fn(av, bv):
            return lax.dot_general(av.astype(BF), bv.astype(BF), (((0,), (0,)), ((), ())),
                                   preferred_element_type=F32)

        return _vcall(name, fn, (k // tk, n // tn, m // tm),
                      [(a, (tm, tk), lambda i, j, s: (s, i)), (b, (tm, tn), lambda i, j, s: (s, j))],
                      [((k, n), F32, (tk, tn), lambda i, j, s: (i, j))], acc={0: "last"})
    m, k = a.shape
    n = b.shape[1] if kind == "nn" else b.shape[0]
    tm = _pick(m, (512, 256))
    tn = _pick(n, (512, 896, 256, 128))
    dims = (((1,), (0,)), ((), ())) if kind == "nn" else (((1,), (1,)), ((), ()))

    def fn(av, bv, *rest):
        r = lax.dot_general(av.astype(BF), bv.astype(BF), dims, preferred_element_type=F32)
        return r + rest[0] if rest else r

    ins = [(a, (tm, k), lambda i, j: (i, 0)),
           (b, (k, tn), lambda i, j: (0, j)) if kind == "nn" else (b, (tn, k), lambda i, j: (j, 0))]
    if add is not None:
        ins.append((add, (tm, tn), lambda i, j: (i, j)))
    return _vcall(name, fn, (m // tm, n // tn), ins, [((m, n), out_dtype, (tm, tn), lambda i, j: (i, j))])


def _rms(h, w):
    return h * lax.rsqrt(jnp.mean(h * h, axis=-1, keepdims=True) + EPS) * w


def _rms_fwd(name, h, w):
    L = h.shape[0]
    tb = _pick(L, (ROWS,))
    return _vcall(name, _rms, (L // tb,),
                  [(h, (tb, D), lambda i: (i, 0)), (w.reshape(1, D), (1, D), lambda i: (0, 0))],
                  [((L, D), BF, (tb, D), lambda i: (i, 0))])


def _rms_bwd(name, du, h, w, dh_next):
    L = h.shape[0]
    tb = _pick(L, (ROWS,))

    def fn(duv, hv, wv, dnv):
        _, vjp = jax.vjp(_rms, hv, wv)
        dh, dw = vjp(duv)
        return dh + dnv, dw

    row = lambda i: (i, 0)
    return _vcall(name, fn, (L // tb,),
                  [(du, (tb, D), row), (h, (tb, D), row), (w.reshape(1, D), (1, D), lambda i: (0, 0)),
                   (dh_next, (tb, D), row)],
                  [((L, D), F32, (tb, D), row), ((1, D), F32, (1, D), lambda i: (0, 0))], acc={1: "all"})


def _loss_head(name, h, tgt, w):
    L = h.shape[0]
    tb = _pick(L, (ROWS,))

    def lossf(hv, wv, tv):
        err = _rms(hv, wv) - tv
        return 0.5 * jnp.sum(err * err) * (1.0 / D)

    def fn(hv, wv, tv):
        val, vjp = jax.vjp(lambda a, b: lossf(a, b, tv), hv, wv)
        dh, dw = vjp(jnp.ones((), F32))
        return jnp.full((1, 128), val, F32), dh, dw

    row = lambda i: (i, 0)
    zero = lambda i: (0, 0)
    return _vcall(name, fn, (L // tb,),
                  [(h, (tb, D), row), (w.reshape(1, D), (1, D), zero), (tgt, (tb, D), row)],
                  [((1, 128), F32, (1, 128), zero), ((L, D), F32, (tb, D), row), ((1, D), F32, (1, D), zero)],
                  acc={0: "all", 2: "all"})


def _bf(x):
    return x.astype(BF)


def _dot(a, b, dims, precision=None):
    return lax.dot_general(a, b, (dims, ((), ())), preferred_element_type=F32, precision=precision)


def _tri(n, reverse):
    r = lax.broadcasted_iota(jnp.int32, (n, n), 0)
    c = lax.broadcasted_iota(jnp.int32, (n, n), 1)
    return (r <= c) if reverse else (r >= c)


def _gla_chunk(q_raw, f_raw, v, lb3, S, reverse):
    C, SB = HG_CHUNK, HG_SUB
    row3 = lax.broadcasted_iota(jnp.int32, (3, HG_HD), 0)
    e = jnp.exp(lb3 - jnp.max(lb3, axis=0, keepdims=True))
    lb = jnp.sum(jnp.where(row3 == 0, e, 0.0), axis=0, keepdims=True) / jnp.sum(e, axis=0, keepdims=True)
    q = q_raw * jax.nn.sigmoid(q_raw)
    f = lb + (1.0 - lb) * jax.nn.sigmoid(f_raw)
    g = jnp.log(f)
    k = 1.0 - f
    b = _dot(_tri(C, reverse).astype(F32), g, ((1,), (0,)), precision=HI)
    row = lax.broadcasted_iota(jnp.int32, (C, 1), 0)

    def rowof(x, t):
        return jnp.sum(jnp.where(row == t, x, 0.0), axis=0, keepdims=True)

    o = _dot(_bf(q * jnp.exp(b)), _bf(S), ((1,), (0,)))
    att = None
    for i in range(C // SB):
        lo = SB * i
        if (not reverse and i == 0) or (reverse and i == C // SB - 1):
            continue
        first = lo + SB - 1 if reverse else lo
        r = rowof(b, first) - rowof(g, first)
        in_blk = jnp.logical_and(row >= lo, row < lo + SB)
        before = (row >= lo + SB) if reverse else (row < lo)
        qi = q * jnp.exp(jnp.where(in_blk, b - r, NEG))
        kk = k * jnp.exp(jnp.where(before, r - b, NEG))
        a_i = _dot(_bf(qi), _bf(kk), ((1,), (1,)))
        att = a_i if att is None else att + a_i
    o = o + _dot(_bf(att), _bf(v), ((1,), (0,)))
    t_i = lax.broadcasted_iota(jnp.int32, (SB, SB, HG_HD), 0)
    s_i = lax.broadcasted_iota(jnp.int32, (SB, SB, HG_HD), 1)
    pair = (t_i <= s_i) if reverse else (t_i >= s_i)
    diag = []
    for i in range(C // SB):
        lo = SB * i
        qb, kb, bb, vb = q[lo:lo + SB], k[lo:lo + SB], b[lo:lo + SB], v[lo:lo + SB]
        shp = (SB, SB, HG_HD)
        dif = lax.broadcast_in_dim(bb, shp, (0, 2)) - lax.broadcast_in_dim(bb, shp, (1, 2))
        w = lax.broadcast_in_dim(qb, shp, (0, 2)) * jnp.exp(jnp.where(pair, dif, NEG)) * lax.broadcast_in_dim(kb, shp, (1, 2))
        diag.append(_dot(_bf(jnp.sum(w, axis=2)), _bf(vb), ((1,), (0,))))
    o = o + jnp.concatenate(diag, axis=0)
    btot = rowof(b, 0 if reverse else C - 1)
    eye = lax.broadcasted_iota(jnp.int32, (HG_HD, HG_HD), 0) == lax.broadcasted_iota(jnp.int32, (HG_HD, HG_HD), 1)
    btot_col = jnp.sum(jnp.where(eye, btot, 0.0), axis=1, keepdims=True)
    s_new = jnp.exp(btot_col) * S + _dot(_bf(k * jnp.exp(btot - b)), _bf(v), ((0,), (0,)))
    return o, s_new


def _gla_fwd(name, pa, lbl, reverse):
    L = pa.shape[0]
    C = HG_CHUNK
    nc = L // C
    cidx = (lambda i: nc - 1 - i) if reverse else (lambda i: i)
    sec = 16 if reverse else 8

    def fn(qr, fr, v, lb3, s_ref):
        @pl.when(pl.program_id(1) == 0)
        def _():
            s_ref[...] = jnp.zeros_like(s_ref)

        s = s_ref[...]
        o, s_new = _gla_chunk(qr, fr, v, lb3, s, reverse)
        s_ref[...] = s_new
        return o, s[None, None]

    blk = (C, HG_HD)
    return _vcall(name, fn, (HG_HEADS, nc),
                  [(pa, blk, lambda h, i: (cidx(i), h)), (pa, blk, lambda h, i: (cidx(i), sec + h)),
                   (pa, blk, lambda h, i: (cidx(i), 24 + h)), (lbl, (3, HG_HD), lambda h, i: (0, h))],
                  [((L, D), F32, blk, lambda h, i: (cidx(i), h)),
                   ((nc, HG_HEADS, HG_HD, HG_HD), F32, (1, 1, HG_HD, HG_HD), lambda h, i: (cidx(i), h, 0, 0))],
                  scratch=[pltpu.VMEM((HG_HD, HG_HD), F32)])


def _gla_bwd(name, pa, lbl, s_in, do, reverse, prev=None):
    L = pa.shape[0]
    C = HG_CHUNK
    nc = L // C
    cidx = (lambda i: i) if reverse else (lambda i: nc - 1 - i)
    sec = 16 if reverse else 8
    n_prev = 0 if prev is None else 2

    def fn(qr, fr, v, lb3, s, dov, *rest):
        ds_ref = rest[n_prev]

        @pl.when(pl.program_id(1) == 0)
        def _():
            ds_ref[...] = jnp.zeros_like(ds_ref)

        _, vjp = jax.vjp(lambda *a: _gla_chunk(*a, reverse), qr, fr, v, lb3, s[0, 0])
        dq, df, dv, dlb, ds = vjp((dov, ds_ref[...]))
        ds_ref[...] = ds
        if n_prev:
            dq, dv = dq + rest[0], dv + rest[1]
        return dq, df, dv, dlb

    blk = (C, HG_HD)
    at = lambda h, i: (cidx(i), h)
    ins = [(pa, blk, at), (pa, blk, lambda h, i: (cidx(i), sec + h)), (pa, blk, lambda h, i: (cidx(i), 24 + h)),
           (lbl, (3, HG_HD), lambda h, i: (0, h)),
           (s_in, (1, 1, HG_HD, HG_HD), lambda h, i: (cidx(i), h, 0, 0)), (do, blk, at)]
    if prev is not None:
        ins += [(prev[0], blk, at), (prev[1], blk, at)]
    sum_dt = F32 if prev is None else BF
    return _vcall(name, fn, (HG_HEADS, nc), ins,
                  [((L, D), sum_dt, blk, at), ((L, D), BF, blk, at), ((L, D), sum_dt, blk, at),
                   ((3, D), F32, (3, HG_HD), lambda h, i: (0, h))],
                  acc={3: "last"}, scratch=[pltpu.VMEM((HG_HD, HG_HD), F32)])


def _hgout(o_f, o_b, g, nw):
    o = o_f + o_b
    return _rms(o, nw) * (g * jax.nn.sigmoid(g))


def _hgout_fwd(name, o_f, o_b, pa, nw):
    L = o_f.shape[0]
    tb = _pick(L, (ROWS,))
    blk = (tb, HG_HD)
    at = lambda h, i: (i, h)
    return _vcall(name, _hgout, (HG_HEADS, L // tb),
                  [(o_f, blk, at), (o_b, blk, at), (pa, blk, lambda h, i: (i, 32 + h)),
                   (nw.reshape(1, HG_HD), (1, HG_HD), lambda h, i: (0, 0))],
                  [((L, D), BF, blk, at)])


def _hgout_bwd(name, o_f, o_b, pa, nw, dy):
    L = o_f.shape[0]
    tb = _pick(L, (ROWS,))

    def fn(ofv, obv, gv, nwv, dyv):
        _, vjp = jax.vjp(_hgout, ofv, obv, gv, nwv)
        do, _, dg, dnw = vjp(dyv)
        return do, dg, dnw

    blk = (tb, HG_HD)
    at = lambda h, i: (i, h)
    zero = lambda h, i: (0, 0)
    return _vcall(name, fn, (HG_HEADS, L // tb),
                  [(o_f, blk, at), (o_b, blk, at), (pa, blk, lambda h, i: (i, 32 + h)),
                   (nw.reshape(1, HG_HD), (1, HG_HD), zero), (dy, blk, at)],
                  [((L, D), F32, blk, at), ((L, D), BF, blk, at), ((1, HG_HD), F32, (1, HG_HD), zero)],
                  acc={2: "all"})


def _shift(x, s):
    if s == 0:
        return x
    n = x.shape[0]
    t = lax.broadcasted_iota(jnp.int32, (n, 1), 0)
    if s > 0:
        return jnp.where(t >= s, pltpu.roll(x, s, 0), 0.0)
    return jnp.where(t < n + s, pltpu.roll(x, n + s, 0), 0.0)


def _conv(x, w, b):
    kk = w.shape[0]
    p = (kk - 1) // 2
    y = b
    for j in range(kk):
        y = y + w[j:j + 1] * _shift(x, p - j)
    return y


def _conv_bwd(x, w, dc):
    kk = w.shape[0]
    p = (kk - 1) // 2
    dx = None
    dws = []
    for j in range(kk):
        t = w[j:j + 1] * _shift(dc, j - p)
        dx = t if dx is None else dx + t
        dws.append(jnp.sum(dc * _shift(x, p - j), axis=0, keepdims=True))
    rows = lax.broadcasted_iota(jnp.int32, (kk, 1), 0)
    dw = None
    for j in range(kk):
        t = jnp.where(rows == j, dws[j], 0.0)
        dw = t if dw is None else dw + t
    return dx, dw, jnp.sum(dc, axis=0, keepdims=True)


def _silu_grad(c):
    s = jax.nn.sigmoid(c)
    return s * (1.0 + c * (1.0 - s))


def _glu_fwd(name, pf, cw, cb):
    L = pf.shape[0]
    tc = 128
    nt = D_FF // tc

    def fn(gate, val, w, b):
        c = _conv(gate, w, b)
        return c * jax.nn.sigmoid(c) * val

    return _vcall(name, fn, (nt,),
                  [(pf, (L, tc), lambda j: (0, j)), (pf, (L, tc), lambda j: (0, nt + j)),
                   (cw, (3, tc), lambda j: (0, j)), (cb.reshape(1, D_FF), (1, tc), lambda j: (0, j))],
                  [((L, D_FF), BF, (L, tc), lambda j: (0, j))])


def _glu_bwd(name, pf, cw, cb, dy):
    L = pf.shape[0]
    tc = 128
    nt = D_FF // tc

    def fn(gate, val, w, b, dyv):
        c = _conv(gate, w, b)
        sc = c * jax.nn.sigmoid(c)
        dc = dyv * val * _silu_grad(c)
        dgate, dw, db = _conv_bwd(gate, w, dc)
        return dgate, dyv * sc, dw, db

    col = lambda j: (0, j)
    return _vcall(name, fn, (nt,),
                  [(pf, (L, tc), col), (pf, (L, tc), lambda j: (0, nt + j)), (cw, (3, tc), col),
                   (cb.reshape(1, D_FF), (1, tc), col), (dy, (L, tc), col)],
                  [((L, D_FF), BF, (L, tc), col), ((L, D_FF), BF, (L, tc), col),
                   ((3, D_FF), F32, (3, tc), col), ((1, D_FF), F32, (1, tc), col)])


def _perm_tile(j):
    return jnp.where(j < 16, 4 * (j // 2) + j % 2, jnp.where(j < 24, 4 * (j - 16) + 2, 4 * (j - 24) + 3))


def _mpre_fwd(name, pb, cw, cb):
    L = pb.shape[0]
    tc = 128

    def fn(x, w, b):
        c = _conv(x, w, b)
        return c * jax.nn.sigmoid(c)

    return _vcall(name, fn, (CONV_DIM // tc,),
                  [(pb, (L, tc), lambda j: (0, 16 + j)), (cw, (5, tc), lambda j: (0, j)),
                   (cb.reshape(1, CONV_DIM), (1, tc), lambda j: (0, j))],
                  [((L, CONV_DIM), F32, (L, tc), lambda j: (0, _perm_tile(j)))])


def _mpre_bwd(name, pb, cw, cb, dact):
    L = pb.shape[0]
    tc = 128

    def fn(x, w, b, da):
        c = _conv(x, w, b)
        return _conv_bwd(x, w, da * _silu_grad(c))

    col = lambda j: (0, j)
    return _vcall(name, fn, (CONV_DIM // tc,),
                  [(pb, (L, tc), lambda j: (0, 16 + j)), (cw, (5, tc), col), (cb.reshape(1, CONV_DIM), (1, tc), col),
                   (dact, (L, tc), lambda j: (0, _perm_tile(j)))],
                  [((L, CONV_DIM), BF, (L, tc), col), ((5, CONV_DIM), F32, (5, tc), col),
                   ((1, CONV_DIM), F32, (1, tc), col)])


def _softplus(x):
    return jnp.maximum(x, 0.0) + jnp.log(1.0 + jnp.exp(-jnp.abs(x)))


def _dt_fwd(name, pb, dtb, alog):
    L = pb.shape[0]
    tb = _pick(L, (1024, ROWS))

    def fn(x, bias, al):
        dt = _softplus(x + bias)
        return dt, dt * (-jnp.exp(al))

    row = lambda i: (i, 0)
    zero = lambda i: (0, 0)
    return _vcall(name, fn, (L // tb,),
                  [(pb, (tb, 128), lambda i: (i, 48)), (dtb, (1, 128), zero), (alog, (1, 128), zero)],
                  [((L, 128), F32, (tb, 128), row), ((L, 128), F32, (tb, 128), row)])


def _dt_bwd(name, pb, dtb, alog, ddt_f, dla_f, ddt_b, dla_b):
    L = pb.shape[0]
    tb = _pick(L, (1024, ROWS))

    def fn(x, bias, al, a1, b1, a2, b2):
        ddt = jnp.sum(a1, axis=0) + jnp.sum(a2, axis=0)
        dla = jnp.sum(b1, axis=0) + jnp.sum(b2, axis=0)
        z = x + bias
        dt = _softplus(z)
        a = -jnp.exp(al)
        dz = (ddt + dla * a) * jax.nn.sigmoid(z)
        return dz, jnp.sum(dz, axis=0, keepdims=True), jnp.sum(dla * dt, axis=0, keepdims=True) * a

    zero = lambda i: (0, 0)
    g3 = (SSM_GROUPS, tb, 128)
    at3 = lambda i: (0, i, 0)
    return _vcall(name, fn, (L // tb,),
                  [(pb, (tb, 128), lambda i: (i, 48)), (dtb, (1, 128), zero), (alog, (1, 128), zero),
                   (ddt_f, g3, at3), (dla_f, g3, at3), (ddt_b, g3, at3), (dla_b, g3, at3)],
                  [((L, 128), BF, (tb, 128), lambda i: (i, 0)), ((1, 128), F32, (1, 128), zero),
                   ((1, 128), F32, (1, 128), zero)], acc={1: "all", 2: "all"})


def _ssd_chunk(xa, dt, la, hs, head0, reverse):
    C = SSD_CHUNK
    P4 = SSM_HPG * SSM_HD
    xs, bm, cm = xa[:, :P4], xa[:, P4:P4 + SSM_N], xa[:, P4 + SSM_N:]
    lane = lax.broadcasted_iota(jnp.int32, (1, 128), 1)
    col_head = lax.broadcasted_iota(jnp.int32, (1, P4), 1) // SSM_HD
    row_head = lax.broadcasted_iota(jnp.int32, (P4, 1), 0) // SSM_HD
    row = lax.broadcasted_iota(jnp.int32, (C, 1), 0)
    eye = lax.broadcasted_iota(jnp.int32, (C, C), 0) == lax.broadcasted_iota(jnp.int32, (C, C), 1)
    tri = _tri(C, reverse)
    acum = _dot(tri.astype(F32), la, ((1,), (0,)), precision=HI)
    cb = _dot(_bf(cm), _bf(bm), ((1,), (1,)))
    last = 0 if reverse else C - 1
    dt_x, ea_x, dec_x, y = 0.0, 0.0, 0.0, 0.0
    atot_rows = 0.0
    lmats = []
    for j in range(SSM_HPG):
        sel = lane == head0 + j
        dt_j = jnp.sum(jnp.where(sel, dt, 0.0), axis=1, keepdims=True)
        ac_j = jnp.sum(jnp.where(sel, acum, 0.0), axis=1, keepdims=True)
        ac_row = jnp.sum(jnp.where(eye, ac_j, 0.0), axis=0, keepdims=True)
        atot = jnp.sum(jnp.where(row == last, ac_j, 0.0), axis=0, keepdims=True)
        lmats.append(jnp.exp(jnp.where(tri, ac_j - ac_row, NEG)))
        mine = col_head == j
        dt_x = dt_x + jnp.where(mine, dt_j, 0.0)
        ea_x = ea_x + jnp.where(mine, jnp.exp(ac_j), 0.0)
        dec_x = dec_x + jnp.where(mine, jnp.exp(atot - ac_j), 0.0)
        atot_rows = atot_rows + jnp.where(row_head == j, jnp.exp(atot), 0.0)
    xd = xs * dt_x
    xdb = _bf(xd)
    for j in range(SSM_HPG):
        y = y + jnp.where(col_head == j, _dot(_bf(cb * lmats[j]), xdb, ((1,), (0,))), 0.0)
    y = y + _dot(_bf(cm), _bf(hs), ((1,), (1,))) * ea_x
    hs_new = atot_rows * hs + _dot(_bf(xd * dec_x), _bf(bm), ((0,), (0,)))
    return y, hs_new


def _ssd_fwd(name, xact, dt, la, reverse):
    L = xact.shape[0]
    C = SSD_CHUNK
    nc = L // C
    cidx = (lambda i: nc - 1 - i) if reverse else (lambda i: i)
    base = SSM_HEADS if reverse else 0
    P4 = SSM_HPG * SSM_HD

    def fn(xa, dtv, lav, h_ref):
        @pl.when(pl.program_id(1) == 0)
        def _():
            h_ref[...] = jnp.zeros_like(h_ref)

        hs = h_ref[...]
        y, hs_new = _ssd_chunk(xa, dtv, lav, hs, base + SSM_HPG * pl.program_id(0), reverse)
        h_ref[...] = hs_new
        return y, hs[None, None]

    return _vcall(name, fn, (SSM_GROUPS, nc),
                  [(xact, (C, 512), lambda g, i: (cidx(i), g)), (dt, (C, 128), lambda g, i: (cidx(i), 0)),
                   (la, (C, 128), lambda g, i: (cidx(i), 0))],
                  [((L, D_INNER), F32, (C, P4), lambda g, i: (cidx(i), g)),
                   ((nc, SSM_GROUPS, P4, SSM_N), F32, (1, 1, P4, SSM_N), lambda g, i: (cidx(i), g, 0, 0))],
                  scratch=[pltpu.VMEM((P4, SSM_N), F32)])


def _ssd_bwd(name, xact, dt, la, h_in, dy, reverse, prev_xs=None, prev_all=None):
    L = xact.shape[0]
    C = SSD_CHUNK
    nc = L // C
    cidx = (lambda i: i) if reverse else (lambda i: nc - 1 - i)
    base = SSM_HEADS if reverse else 0
    P4 = SSM_HPG * SSM_HD

    def fn(xa, dtv, lav, hs, dyv, pv, dh_ref):
        @pl.when(pl.program_id(1) == 0)
        def _():
            dh_ref[...] = jnp.zeros_like(dh_ref)

        head0 = base + SSM_HPG * pl.program_id(0)
        _, vjp = jax.vjp(lambda a, b, c, d: _ssd_chunk(a, b, c, d, head0, reverse), xa, dtv, lav, hs[0, 0])
        dxa, ddt, dla, dh = vjp((dyv, dh_ref[...]))
        dh_ref[...] = dh
        if prev_all is not None:
            dxa = dxa + pv
        else:
            dxa = dxa + jnp.concatenate([pv, jnp.zeros((C, 2 * SSM_N), F32)], axis=1)
        return dxa, ddt[None], dla[None]

    at = lambda g, i: (cidx(i), g)
    at0 = lambda g, i: (cidx(i), 0)
    pv = (prev_all, (C, 512), at) if prev_all is not None else (prev_xs, (C, P4), at)
    return _vcall(name, fn, (SSM_GROUPS, nc),
                  [(xact, (C, 512), at), (dt, (C, 128), at0), (la, (C, 128), at0),
                   (h_in, (1, 1, P4, SSM_N), lambda g, i: (cidx(i), g, 0, 0)), (dy, (C, P4), at), pv],
                  [((L, CONV_DIM), F32, (C, 512), at),
                   ((SSM_GROUPS, L, 128), F32, (1, C, 128), lambda g, i: (g, cidx(i), 0)),
                   ((SSM_GROUPS, L, 128), F32, (1, C, 128), lambda g, i: (g, cidx(i), 0))],
                  scratch=[pltpu.VMEM((P4, SSM_N), F32)])


def _mpost(y_f, y_b, xs, z, dsk, nw):
    y = (y_f + y_b + xs * dsk) * (z * jax.nn.sigmoid(z))
    return _rms(y, nw)


def _mpost_fwd(name, y_f, y_b, xact, pb, dsk, nw):
    L = y_f.shape[0]
    tb = _pick(L, (ROWS,))
    blk = (tb, 256)
    at = lambda g, i: (i, g)
    par = lambda g, i: (0, g)
    return _vcall(name, _mpost, (SSM_GROUPS, L // tb),
                  [(y_f, blk, at), (y_b, blk, at), (xact, blk, lambda g, i: (i, 2 * g)), (pb, blk, at),
                   (dsk, (1, 256), par), (nw, (1, 256), par)],
                  [((L, D_INNER), BF, blk, at)])


def _mpost_bwd(name, y_f, y_b, xact, pb, dsk, nw, dy):
    L = y_f.shape[0]
    tb = _pick(L, (ROWS,))

    def fn(yf, yb, xs, z, dskv, nwv, dyv):
        _, vjp = jax.vjp(_mpost, yf, yb, xs, z, dskv, nwv)
        dyf, _, dxs, dz, ddsk, dnw = vjp(dyv)
        return dyf, dxs, dz, ddsk, dnw

    blk = (tb, 256)
    at = lambda g, i: (i, g)
    par = lambda g, i: (0, g)
    return _vcall(name, fn, (SSM_GROUPS, L // tb),
                  [(y_f, blk, at), (y_b, blk, at), (xact, blk, lambda g, i: (i, 2 * g)), (pb, blk, at),
                   (dsk, (1, 256), par), (nw, (1, 256), par), (dy, blk, at)],
                  [((L, D_INNER), F32, blk, at), ((L, D_INNER), F32, blk, at), ((L, D_INNER), BF, blk, at),
                   ((1, D_INNER), F32, (1, 256), par), ((1, D_INNER), F32, (1, 256), par)],
                  acc={3: "last", 4: "last"})


def _ffn_fwd(tag, h, nw, w_in, cw, cb, w_out):
    u = _rms_fwd(f"{tag}_norm", h, nw)
    pf = _mm(f"{tag}_in", u, w_in, "nn")
    yf = _glu_fwd(f"{tag}_glu", pf, cw, cb)
    return _mm(f"{tag}_out", yf, w_out, "nn", add=h), (u, pf, yf)


def _ffn_bwd(tag, h, nw, w_in, cw, cb, w_out, saved, dh):
    u, pf, yf = saved
    d_w_out = _mm(f"{tag}_dwout", yf, dh, "tn")
    dyf = _mm(f"{tag}_dy", dh, w_out, "nt")
    dgate, dval, dcw, dcb = _glu_bwd(f"{tag}_dglu", pf, cw, cb, dyf)
    dpf = jnp.concatenate([dgate, dval], axis=1)
    d_w_in = _mm(f"{tag}_dwin", u, dpf, "tn")
    du = _mm(f"{tag}_du", dpf, w_in, "nt")
    dh_in, dnw = _rms_bwd(f"{tag}_dnorm", du, h, nw, dh)
    return dh_in, dnw, d_w_in, dcw, dcb, d_w_out


def _sequence_grads(x, tgt, p):
    g = {}
    lbl = p["a_lb_logits"]
    u1 = _rms_fwd("a_norm", x, p["norm1_w"][0])
    pa = _mm("a_in", u1, p["a_w_in"], "nn")
    o_f, s_f = _gla_fwd("a_scan_f", pa, lbl, False)
    o_b, s_b = _gla_fwd("a_scan_b", pa, lbl, True)
    ya = _hgout_fwd("a_gate", o_f, o_b, pa, p["a_norm_w"])
    h1 = _mm("a_out", ya, p["a_w_out"], "nn", add=x)
    h2, ffn0 = _ffn_fwd("f0", h1, p["norm2_w"][0], p["ffn_w_in"][0], p["ffn_conv_w"][0], p["ffn_conv_b"][0], p["ffn_w_out"][0])
    u3 = _rms_fwd("b_norm", h2, p["norm1_w"][1])
    pb = _mm("b_in", u3, p["b_w_in"], "nn")
    xact = _mpre_fwd("b_conv", pb, p["b_conv_w"], p["b_conv_b"])
    dt, la = _dt_fwd("b_dt", pb, p["b_dt_bias"], p["b_a_log"])
    y_f, hs_f = _ssd_fwd("b_scan_f", xact, dt, la, False)
    y_b, hs_b = _ssd_fwd("b_scan_b", xact, dt, la, True)
    yb = _mpost_fwd("b_gate", y_f, y_b, xact, pb, p["b_d_skip"], p["b_norm_w"])
    h3 = _mm("b_out", yb, p["b_w_out"], "nn", add=h2)
    h4, ffn1 = _ffn_fwd("f1", h3, p["norm2_w"][1], p["ffn_w_in"][1], p["ffn_conv_w"][1], p["ffn_conv_b"][1], p["ffn_w_out"][1])
    loss, dh4, g["final_norm_w"] = _loss_head("head", h4, tgt, p["final_norm_w"])
    dh3, dn2_1, dwin1, dcw1, dcb1, dwout1 = _ffn_bwd("f1", h3, p["norm2_w"][1], p["ffn_w_in"][1], p["ffn_conv_w"][1],
                                                     p["ffn_conv_b"][1], p["ffn_w_out"][1], ffn1, dh4)
    g["b_w_out"] = _mm("b_dwout", yb, dh3, "tn")
    dyb = _mm("b_dy", dh3, p["b_w_out"], "nt")
    dys, dxs, dz, g["b_d_skip"], g["b_norm_w"] = _mpost_bwd("b_dgate", y_f, y_b, xact, pb, p["b_d_skip"], p["b_norm_w"], dyb)
    dxa1, ddt_f, dla_f = _ssd_bwd("b_dscan_f", xact, dt, la, hs_f, dys, False, prev_xs=dxs)
    dxa, ddt_b, dla_b = _ssd_bwd("b_dscan_b", xact, dt, la, hs_b, dys, True, prev_all=dxa1)
    dxbc, g["b_conv_w"], g["b_conv_b"] = _mpre_bwd("b_dconv", pb, p["b_conv_w"], p["b_conv_b"], dxa)
    ddtr, g["b_dt_bias"], g["b_a_log"] = _dt_bwd("b_ddt", pb, p["b_dt_bias"], p["b_a_log"], ddt_f, dla_f, ddt_b, dla_b)
    dpb = jnp.concatenate([dz, dxbc, ddtr], axis=1)
    g["b_w_in"] = _mm("b_dwin", u3, dpb, "tn")
    du3 = _mm("b_du", dpb, p["b_w_in"], "nt")
    dh2, dn1_1 = _rms_bwd("b_dnorm", du3, h2, p["norm1_w"][1], dh3)
    dh1, dn2_0, dwin0, dcw0, dcb0, dwout0 = _ffn_bwd("f0", h1, p["norm2_w"][0], p["ffn_w_in"][0], p["ffn_conv_w"][0],
                                                     p["ffn_conv_b"][0], p["ffn_w_out"][0], ffn0, dh2)
    g["a_w_out"] = _mm("a_dwout", ya, dh1, "tn")
    dya = _mm("a_dy", dh1, p["a_w_out"], "nt")
    do, dg, g["a_norm_w"] = _hgout_bwd("a_dgate", o_f, o_b, pa, p["a_norm_w"], dya)
    dq1, df1, dv1, dl1 = _gla_bwd("a_dscan_f", pa, lbl, s_f, do, False)
    dq, df2, dv, dl2 = _gla_bwd("a_dscan_b", pa, lbl, s_b, do, True, prev=(dq1, dv1))
    dpa = jnp.concatenate([dq, df1, df2, dv, dg], axis=1)
    g["a_w_in"] = _mm("a_dwin", u1, dpa, "tn")
    du1 = _mm("a_du", dpa, p["a_w_in"], "nt")
    dx, dn1_0 = _rms_bwd("a_dnorm", du1, x, p["norm1_w"][0], dh1)
    g["a_lb_logits"] = (dl1, dl2)
    g["norm1_w"] = (dn1_0, dn1_1)
    g["norm2_w"] = (dn2_0, dn2_1)
    g["ffn_w_in"] = (dwin0, dwin1)
    g["ffn_conv_w"] = (dcw0, dcw1)
    g["ffn_conv_b"] = (dcb0, dcb1)
    g["ffn_w_out"] = (dwout0, dwout1)
    return loss, dx, g


def _here():
    return lax.axis_index("x"), lax.axis_index("y"), lax.axis_index("c")


def _allgather8(name, src, by_core=False):
    blk = src.shape[1:] if by_core else src.shape

    def body(x_ref, out_ref, send_sems, recv_sems, local_sem):
        x, y, c = _here()
        me, sibling = (x, y, c), (x, y, 1 - c)
        chips = [(1 - x, y), (x, 1 - y), (1 - x, 1 - y)]
        own = x_ref.at[c] if by_core else x_ref

        def slot(px, py, pc):
            return out_ref.at[4 * px + 2 * py + pc]

        def copy(k, block, to, from_own=False):
            return pltpu.make_async_remote_copy(
                src_ref=own if from_own else slot(*block), dst_ref=slot(*block),
                send_sem=send_sems.at[k], recv_sem=recv_sems.at[k], device_id=to, device_id_type=MESH)

        mine = pltpu.make_async_copy(own, slot(*me), local_sem)
        mine.start()
        first = [copy(0, me, sibling, from_own=True)]
        first += [copy(1 + j, me, (*chip, c), from_own=True) for j, chip in enumerate(chips)]
        for cp in first:
            cp.start()
        passed = [copy(4 + j, (*chip, c), sibling) for j, chip in enumerate(chips)]
        for j, chip in enumerate(chips):
            copy(1 + j, (*chip, c), me).wait_recv()
            passed[j].start()
        copy(0, sibling, me).wait_recv()
        for j, chip in enumerate(chips):
            copy(4 + j, (*chip, 1 - c), me).wait_recv()
        for cp in first + passed:
            cp.wait_send()
        mine.wait()

    return pl.pallas_call(
        body, name=name,
        out_shape=jax.ShapeDtypeStruct((8,) + tuple(blk), src.dtype),
        in_specs=[pl.BlockSpec(memory_space=pl.ANY)],
        out_specs=pl.BlockSpec(memory_space=pl.ANY),
        scratch_shapes=[pltpu.SemaphoreType.DMA((7,)), pltpu.SemaphoreType.DMA((7,)), pltpu.SemaphoreType.DMA],
    )(src)


def _exchange(name, ins, out_shapes, plan, n_remote, n_local):
    def body(*refs):
        in_refs = refs[:len(ins)]
        out_refs = refs[len(ins):len(ins) + len(out_shapes)]
        send_sems, recv_sems, local_sems = refs[len(ins) + len(out_shapes):]
        remote, local = plan(in_refs, out_refs)
        copies = [pltpu.make_async_copy(s, d, local_sems.at[i]) for i, (s, d) in enumerate(local)]
        copies += [pltpu.make_async_remote_copy(src_ref=s, dst_ref=d, send_sem=send_sems.at[i], recv_sem=recv_sems.at[i],
                                                device_id=dev, device_id_type=MESH)
                   for i, (s, d, dev) in enumerate(remote)]
        for cp in copies:
            cp.start()
        for cp in copies:
            cp.wait()

    out = pl.pallas_call(
        body, name=name,
        out_shape=[jax.ShapeDtypeStruct(s, dt) for s, dt in out_shapes],
        in_specs=[pl.BlockSpec(memory_space=pl.ANY)] * len(ins),
        out_specs=[pl.BlockSpec(memory_space=pl.ANY)] * len(out_shapes),
        scratch_shapes=[pltpu.SemaphoreType.DMA((n_remote,)), pltpu.SemaphoreType.DMA((n_remote,)),
                        pltpu.SemaphoreType.DMA((n_local,))],
    )(*ins)
    return out


def _reduce_to_owner(gp):
    _, _, r, n = gp.shape

    def plan1(ins, outs):
        x, y, c = _here()
        (g,), (own, got) = ins, outs
        return ([(g.at[j, 1 - c], got.at[j], (x, y, 1 - c)) for j in range(4)],
                [(g.at[j, c], own.at[j]) for j in range(4)])

    own, got = _exchange("g_pair", [gp], [((4, r, n), F32)] * 2, plan1, 4, 4)
    tb = _pick(r, (560, 256, 128, 64, 16))
    row3 = lambda i: (i, 0)
    chip_sum = _vcall("g_pair_sum", lambda a, b: a + b, (4 * r // tb,),
                      [(own.reshape(4 * r, n), (tb, n), row3), (got.reshape(4 * r, n), (tb, n), row3)],
                      [((4 * r, n), BF, (tb, n), row3)]).reshape(4, r, n)

    def plan2(ins, outs):
        x, y, c = _here()
        (s,), (mine, got) = ins, outs
        m = 2 * x + y
        remote = []
        for k in (1, 2, 3):
            t = (m + k) % 4
            remote.append((s.at[t], got.at[k - 1], (t // 2, t % 2, c)))
        return remote, [(s.at[m], mine)]

    mine, got = _exchange("g_chips", [chip_sum], [((r, n), BF), ((3, r, n), BF)], plan2, 3, 1)
    half = _vcall("g_chip_sum", lambda a, b: ((a.astype(F32) + b[0].astype(F32)) + b[1].astype(F32)) + b[2].astype(F32),
                  (r // tb,), [(mine, (tb, n), lambda i: (i, 0)), (got, (3, tb, n), lambda i: (0, i, 0))],
                  [((r, n), F32, (tb, n), lambda i: (i, 0))])

    def plan3(ins, outs):
        x, y, c = _here()
        (h,), (full,) = ins, outs
        return [(h, full.at[c], (x, y, 1 - c))], [(h, full.at[c])]

    (full,) = _exchange("g_halves", [half], [((2, r, n), F32)], plan3, 1, 1)
    return full


def _adam(name, w, g, m, v):
    rows, cols = w.shape
    tb = _pick(rows, (256, 128, 64, 8))

    def fn(wv, gv, mv, vv):
        m2 = ADAM_B1 * mv + (1.0 - ADAM_B1) * gv
        v2 = ADAM_B2 * vv + (1.0 - ADAM_B2) * jnp.square(gv)
        m_hat = m2 / (1.0 - ADAM_B1 ** ADAM_STEP)
        v_hat = v2 / (1.0 - ADAM_B2 ** ADAM_STEP)
        return -ADAM_LR * (m_hat / (jnp.sqrt(v_hat) + ADAM_EPS) + ADAM_WD * wv), m2, v2

    at = lambda i: (i, 0)
    return _vcall(name, fn, (rows // tb,), [(a, (tb, cols), at) for a in (w, g, m, v)],
                  [((rows, cols), F32, (tb, cols), at)] * 3)


def _pack(arrays, width, row_multiple, dtype):
    parts, offs, at = [], [], 0
    for a in arrays:
        flat = a.reshape(-1).astype(dtype)
        rows = -(-flat.shape[0] // width)
        parts.append(jnp.pad(flat, (0, rows * width - flat.shape[0])).reshape(rows, width))
        offs.append(at)
        at += rows
    total = -(-at // row_multiple) * row_multiple
    if total > at:
        parts.append(jnp.zeros((total - at, width), dtype))
    return jnp.concatenate(parts, axis=0), offs


def _unpack(flat, shapes, offs):
    out = []
    for shp, at in zip(shapes, offs):
        n = 1
        for s in shp:
            n *= s
        rows = -(-n // flat.shape[1])
        out.append(flat[at:at + rows].reshape(-1)[:n].reshape(shp))
    return out


_BIG = ("a_w_in", "a_w_out", "b_w_in", "b_w_out", "ffn_w_in", "ffn_w_out")
_BIG_AXIS = {"a_w_in": 2, "a_w_out": 1, "b_w_in": 2, "b_w_out": 1, "ffn_w_in": 2, "ffn_w_out": 1}
_SMALL_SPLIT = ("b_conv_w", "b_conv_b", "b_norm_w", "ffn_conv_w")
_SMALL = ("norm1_w", "norm2_w", "a_lb_logits", "a_norm_w", "b_conv_w", "b_conv_b", "b_dt_bias", "b_a_log", "b_d_skip",
          "b_norm_w", "ffn_conv_w", "ffn_conv_b", "final_norm_w")
_ORDER = ("norm1_w", "norm2_w", "a_w_in", "a_lb_logits", "a_norm_w", "a_w_out", "b_w_in", "b_conv_w", "b_conv_b", "b_dt_bias",
          "b_a_log", "b_d_skip", "b_norm_w", "b_w_out", "ffn_w_in", "ffn_conv_w", "ffn_conv_b", "ffn_w_out", "final_norm_w")


def kernel(x, norm1_w, norm2_w, a_w_in, a_lb_logits, a_norm_w, a_w_out, b_w_in, b_conv_w, b_conv_b, b_dt_bias, b_a_log, b_d_skip, b_norm_w, b_w_out, ffn_w_in, ffn_conv_w, ffn_conv_b, ffn_w_out, final_norm_w, loss_target, m_norm1_w, m_norm2_w, m_a_w_in, m_a_lb_logits, m_a_norm_w, m_a_w_out, m_b_w_in, m_b_conv_w, m_b_conv_b, m_b_dt_bias, m_b_a_log, m_b_d_skip, m_b_norm_w, m_b_w_out, m_ffn_w_in, m_ffn_conv_w, m_ffn_conv_b, m_ffn_w_out, m_final_norm_w, v_norm1_w, v_norm2_w, v_a_w_in, v_a_lb_logits, v_a_norm_w, v_a_w_out, v_b_w_in, v_b_conv_w, v_b_conv_b, v_b_dt_bias, v_b_a_log, v_b_d_skip, v_b_norm_w, v_b_w_out, v_ffn_w_in, v_ffn_conv_w, v_ffn_conv_b, v_ffn_w_out, v_final_norm_w):
    w = dict(norm1_w=norm1_w, norm2_w=norm2_w, a_w_in=a_w_in, a_lb_logits=a_lb_logits, a_norm_w=a_norm_w, a_w_out=a_w_out,
             b_w_in=b_w_in, b_conv_w=b_conv_w, b_conv_b=b_conv_b, b_dt_bias=b_dt_bias, b_a_log=b_a_log, b_d_skip=b_d_skip,
             b_norm_w=b_norm_w, b_w_out=b_w_out, ffn_w_in=ffn_w_in, ffn_conv_w=ffn_conv_w, ffn_conv_b=ffn_conv_b,
             ffn_w_out=ffn_w_out, final_norm_w=final_norm_w)
    mom = dict(norm1_w=m_norm1_w, norm2_w=m_norm2_w, a_w_in=m_a_w_in, a_lb_logits=m_a_lb_logits, a_norm_w=m_a_norm_w,
               a_w_out=m_a_w_out, b_w_in=m_b_w_in, b_conv_w=m_b_conv_w, b_conv_b=m_b_conv_b, b_dt_bias=m_b_dt_bias,
               b_a_log=m_b_a_log, b_d_skip=m_b_d_skip, b_norm_w=m_b_norm_w, b_w_out=m_b_w_out, ffn_w_in=m_ffn_w_in,
               ffn_conv_w=m_ffn_conv_w, ffn_conv_b=m_ffn_conv_b, ffn_w_out=m_ffn_w_out, final_norm_w=m_final_norm_w)
    var = dict(norm1_w=v_norm1_w, norm2_w=v_norm2_w, a_w_in=v_a_w_in, a_lb_logits=v_a_lb_logits, a_norm_w=v_a_norm_w,
               a_w_out=v_a_w_out, b_w_in=v_b_w_in, b_conv_w=v_b_conv_w, b_conv_b=v_b_conv_b, b_dt_bias=v_b_dt_bias,
               b_a_log=v_b_a_log, b_d_skip=v_b_d_skip, b_norm_w=v_b_norm_w, b_w_out=v_b_w_out, ffn_w_in=v_ffn_w_in,
               ffn_conv_w=v_ffn_conv_w, ffn_conv_b=v_ffn_conv_b, ffn_w_out=v_ffn_w_out, final_norm_w=v_final_norm_w)
    chip = 2 * lax.axis_index("x") + lax.axis_index("y")

    big_shapes = [w[n].shape for n in _BIG]
    wpack, big_offs = _pack([w[n] for n in _BIG], D, 32, BF)
    rp = wpack.shape[0]
    wall = _allgather8("w_gather", wpack.reshape(2, rp // 2, D), by_core=True).reshape(4, rp, D)
    shards = [_unpack(wall[j], big_shapes, big_offs) for j in range(4)]
    full = {n: jnp.concatenate([shards[j][i] for j in range(4)], axis=_BIG_AXIS[n]) for i, n in enumerate(_BIG)}
    small_shapes = [w[n].shape for n in _SMALL_SPLIT]
    spack, small_offs = _pack([w[n] for n in _SMALL_SPLIT], 128, 8, F32)
    sall = _allgather8("s_gather", spack)
    sshards = [_unpack(sall[2 * j], small_shapes, small_offs) for j in range(4)]
    sfull = {n: jnp.concatenate([sshards[j][i] for j in range(4)], axis=-1) for i, n in enumerate(_SMALL_SPLIT)}

    p = dict(
        norm1_w=norm1_w, norm2_w=norm2_w, a_lb_logits=a_lb_logits, a_norm_w=a_norm_w[0], final_norm_w=final_norm_w,
        a_w_in=full["a_w_in"][0], a_w_out=full["a_w_out"][0],
        b_w_in=jnp.pad(full["b_w_in"][0], ((0, 0), (0, B_PROJ_PAD - B_PROJ))), b_w_out=full["b_w_out"][0],
        ffn_w_in=full["ffn_w_in"], ffn_w_out=full["ffn_w_out"],
        b_conv_w=sfull["b_conv_w"][0], b_conv_b=sfull["b_conv_b"][0], b_norm_w=sfull["b_norm_w"],
        ffn_conv_w=sfull["ffn_conv_w"], ffn_conv_b=ffn_conv_b,
        b_dt_bias=jnp.pad(b_dt_bias.reshape(1, 2 * SSM_HEADS), ((0, 0), (0, 128 - 2 * SSM_HEADS))),
        b_a_log=jnp.pad(b_a_log.reshape(1, 2 * SSM_HEADS), ((0, 0), (0, 128 - 2 * SSM_HEADS))),
        b_d_skip=jnp.repeat(b_d_skip[0], SSM_HD)[None],
    )

    loss_row, dx, g = _sequence_grads(x[0], loss_target[0], p)

    gfull = {
        "a_w_in": g["a_w_in"][None], "a_w_out": g["a_w_out"][None], "b_w_in": g["b_w_in"][None, :, :B_PROJ],
        "b_w_out": g["b_w_out"][None], "ffn_w_in": jnp.stack(g["ffn_w_in"]), "ffn_w_out": jnp.stack(g["ffn_w_out"]),
    }
    per_chip = []
    for j in range(4):
        parts = []
        for n, shp in zip(_BIG, big_shapes):
            ax = _BIG_AXIS[n]
            parts.append(lax.slice_in_dim(gfull[n], j * shp[ax], (j + 1) * shp[ax], axis=ax))
        per_chip.append(_pack(parts, D, 32, F32)[0])
    gp = jnp.stack(per_chip).reshape(4, 2, rp // 2, D)
    gmine = _reduce_to_owner(gp).reshape(rp, D)
    grads = dict(zip(_BIG, _unpack(gmine, big_shapes, big_offs)))

    gsmall = {
        "norm1_w": jnp.concatenate(g["norm1_w"], axis=0), "norm2_w": jnp.concatenate(g["norm2_w"], axis=0),
        "a_lb_logits": jnp.stack(g["a_lb_logits"]), "a_norm_w": g["a_norm_w"], "b_conv_w": g["b_conv_w"],
        "b_conv_b": g["b_conv_b"], "b_dt_bias": g["b_dt_bias"], "b_a_log": g["b_a_log"], "b_d_skip": g["b_d_skip"],
        "b_norm_w": g["b_norm_w"], "ffn_conv_w": jnp.stack(g["ffn_conv_w"]),
        "ffn_conv_b": jnp.concatenate(g["ffn_conv_b"], axis=0), "final_norm_w": g["final_norm_w"],
    }
    pieces = [gsmall[n] for n in _SMALL] + [loss_row]
    piece_shapes = [a.shape for a in pieces]
    gspack, gs_offs = _pack(pieces, 128, 8, F32)
    rows = gspack.shape[0]
    gsall = _allgather8("gs_gather", gspack)

    def sum8(a):
        r = a[0]
        for i in range(1, 8):
            r = r + a[i]
        return r

    gssum = _vcall("gs_sum", sum8, (1,), [(gsall, (8, rows, 128), lambda i: (0, 0, 0))],
                   [((rows, 128), F32, (rows, 128), lambda i: (0, 0))])
    gs = dict(zip(_SMALL + ("loss",), _unpack(gssum, piece_shapes, gs_offs)))
    loss = gs["loss"][0, 0]
    lb2 = gs["a_lb_logits"]
    small_grads = {
        "norm1_w": gs["norm1_w"], "norm2_w": gs["norm2_w"], "a_lb_logits": lb2[0] + lb2[1], "a_norm_w": gs["a_norm_w"],
        "b_dt_bias": gs["b_dt_bias"][:, :2 * SSM_HEADS].reshape(1, 2, SSM_HEADS),
        "b_a_log": gs["b_a_log"][:, :2 * SSM_HEADS].reshape(1, 2, SSM_HEADS),
        "b_d_skip": gs["b_d_skip"].reshape(1, SSM_HEADS, SSM_HD).sum(axis=-1),
        "ffn_conv_b": gs["ffn_conv_b"], "final_norm_w": gs["final_norm_w"][0],
        "b_conv_w": gs["b_conv_w"][None], "b_conv_b": gs["b_conv_b"], "b_norm_w": gs["b_norm_w"], "ffn_conv_w": gs["ffn_conv_w"],
    }
    for n in _SMALL_SPLIT:
        width = w[n].shape[-1]
        small_grads[n] = lax.dynamic_slice_in_dim(small_grads[n], chip * width, width, axis=small_grads[n].ndim - 1)
    grads.update(small_grads)

    delta, new_m, new_v = {}, {}, {}
    for n in _BIG:
        shp = w[n].shape
        two_d = (shp[0] * shp[1], shp[2])
        d_, m_, v_ = _adam(f"adam_{n}", w[n].reshape(two_d), grads[n].reshape(two_d), mom[n].reshape(two_d), var[n].reshape(two_d))
        delta[n], new_m[n], new_v[n] = d_.reshape(shp), m_.reshape(shp), v_.reshape(shp)
    s_shapes = [w[n].shape for n in _SMALL]
    packs = [_pack([src[n] for n in _SMALL], 128, 8, F32) for src in (w, grads, mom, var)]
    outs = _adam("adam_small", *[pk[0] for pk in packs])
    for res, dst in zip(outs, (delta, new_m, new_v)):
        dst.update(dict(zip(_SMALL, _unpack(res, s_shapes, packs[0][1]))))

    return (loss, dx[None], *[grads[n] for n in _ORDER], *[delta[n] for n in _ORDER],
            *[new_m[n] for n in _ORDER], *[new_v[n] for n in _ORDER])
```

```python
import functools

import jax
import jax.numpy as jnp
from jax import lax
from jax.experimental import pallas as pl
from jax.experimental.pallas import tpu as pltpu

F32, BF = jnp.float32, jnp.bfloat16
HI = lax.Precision.HIGHEST

D = 1024
EPS = 1e-6
HG_HEADS, HG_HD, HG_CHUNK, HG_SUB = 8, 128, 64, 16
HG_HB = 4
SSM_GB = 2
D_INNER, SSM_HEADS, SSM_HD, SSM_GROUPS, SSM_HPG, SSM_N, SSD_CHUNK = 2048, 32, 64, 8, 4, 128, 128
CONV_DIM = D_INNER + 2 * SSM_GROUPS * SSM_N
B_PROJ = 2 * D_INNER + 2 * SSM_GROUPS * SSM_N + 2 * SSM_HEADS
B_PROJ_PAD = 6272
D_FF = 2816
NEG = -1e30
ROWS = 256
VMEM_LIMIT = 56 * 1024 * 1024

ADAM_LR, ADAM_B1, ADAM_B2, ADAM_EPS, ADAM_WD, ADAM_STEP = 0.001, 0.9, 0.999, 1e-08, 0.01, 10

MESH = pl.DeviceIdType.MESH


def _pick(n, cands):
    for c in cands:
        if n % c == 0:
            return c
    return n


def _vcall(name, fn, grid, ins, outs, acc=None, scratch=(), place=None):
    acc = acc or {}
    n_in, n_out, nd = len(ins), len(outs), len(grid)
    n_pre = 0 if place is None else 1

    def body(*refs):
        refs = refs[n_pre:]
        in_refs, out_refs, scr = refs[:n_in], refs[n_in:n_in + n_out], refs[n_in + n_out:]
        res = fn(*[r[...] for r in in_refs], *scr)
        if not isinstance(res, (tuple, list)):
            res = (res,)
        for j, (o_ref, r) in enumerate(zip(out_refs, res)):
            mode = acc.get(j)
            if mode is None:
                o_ref[...] = r.astype(o_ref.dtype)
                continue
            first = pl.program_id(nd - 1) == 0
            if mode == "all":
                for ax in range(nd - 1):
                    first = jnp.logical_and(first, pl.program_id(ax) == 0)

            @pl.when(first)
            def _():
                o_ref[...] = r.astype(o_ref.dtype)

            @pl.when(jnp.logical_not(first))
            def _():
                o_ref[...] += r.astype(o_ref.dtype)

    in_specs = [pl.BlockSpec(bs, im) for _, bs, im in ins]
    out_specs = [pl.BlockSpec(bs, im) for _, _, bs, im in outs]
    params = pltpu.CompilerParams(dimension_semantics=("arbitrary",) * nd, vmem_limit_bytes=VMEM_LIMIT)
    out_shape = [jax.ShapeDtypeStruct(s, dt) for s, dt, _, _ in outs]
    if place is None:
        out = pl.pallas_call(body, name=name, grid=grid, in_specs=in_specs, out_specs=out_specs, out_shape=out_shape,
                             scratch_shapes=list(scratch), compiler_params=params)(*[a for a, _, _ in ins])
    else:
        spec = pltpu.PrefetchScalarGridSpec(num_scalar_prefetch=1, grid=grid, in_specs=in_specs, out_specs=out_specs,
                                            scratch_shapes=list(scratch))
        out = pl.pallas_call(body, name=name, grid_spec=spec, out_shape=out_shape,
                             compiler_params=params)(place, *[a for a, _, _ in ins])
    return out[0] if n_out == 1 else out


def _mm(name, a, b, kind, out_dtype=F32, add=None):
    if kind == "tn":
        m, k = a.shape
        _, n = b.shape
        tm = _pick(m, (1024, 512, 256))
        tk = _pick(k, (1024, 1408, 512, 256, 128))
        tn = _pick(n, (1024, 1408, 896, 512, 256, 128))

        def fn(av, bv):
            return lax.dot_general(av.astype(BF), bv.astype(BF), (((0,), (0,)), ((), ())),
                                   preferred_element_type=F32)

        return _vcall(name, fn, (k // tk, n // tn, m // tm),
                      [(a, (tm, tk), lambda i, j, s: (s, i)), (b, (tm, tn), lambda i, j, s: (s, j))],
                      [((k, n), F32, (tk, tn), lambda i, j, s: (i, j))], acc={0: "last"})
    m, k = a.shape
    n = b.shape[1] if kind == "nn" else b.shape[0]
    long_k = k > 4096
    tm = _pick(m, (512, 256)) if long_k else _pick(m, (1024, 512, 256))
    tn = _pick(n, (512, 896, 256, 128)) if long_k else _pick(n, (1024, 1408, 896, 512, 256, 128))
    dims =(((1,), (0,)), ((), ())) if kind == "nn" else (((1,), (1,)), ((), ()))

    def fn(av, bv, *rest):
        r = lax.dot_general(av.astype(BF), bv.astype(BF), dims, preferred_element_type=F32)
        return r + rest[0] if rest else r

    ins = [(a, (tm, k), lambda i, j: (i, 0)),
           (b, (k, tn), lambda i, j: (0, j)) if kind == "nn" else (b, (tn, k), lambda i, j: (j, 0))]
    if add is not None:
        ins.append((add, (tm, tn), lambda i, j: (i, j)))
    return _vcall(name, fn, (m // tm, n // tn), ins, [((m, n), out_dtype, (tm, tn), lambda i, j: (i, j))])


def _rms(h, w):
    return h * lax.rsqrt(jnp.mean(h * h, axis=-1, keepdims=True) + EPS) * w


def _rms_fwd(name, h, w):
    L = h.shape[0]
    tb = _pick(L, (ROWS,))
    return _vcall(name, _rms, (L // tb,),
                  [(h, (tb, D), lambda i: (i, 0)), (w.reshape(1, D), (1, D), lambda i: (0, 0))],
                  [((L, D), BF, (tb, D), lambda i: (i, 0))])


def _rms_bwd(name, du, h, w, dh_next):
    L = h.shape[0]
    tb = _pick(L, (ROWS,))

    def fn(duv, hv, wv, dnv):
        _, vjp = jax.vjp(_rms, hv, wv)
        dh, dw = vjp(duv)
        return dh + dnv, dw

    row = lambda i: (i, 0)
    return _vcall(name, fn, (L // tb,),
                  [(du, (tb, D), row), (h, (tb, D), row), (w.reshape(1, D), (1, D), lambda i: (0, 0)),
                   (dh_next, (tb, D), row)],
                  [((L, D), F32, (tb, D), row), ((1, D), F32, (1, D), lambda i: (0, 0))], acc={1: "all"})


def _loss_head(name, h, tgt, w):
    L = h.shape[0]
    tb = _pick(L, (ROWS,))

    def lossf(hv, wv, tv):
        err = _rms(hv, wv) - tv
        return 0.5 * jnp.sum(err * err) * (1.0 / D)

    def fn(hv, wv, tv):
        val, vjp = jax.vjp(lambda a, b: lossf(a, b, tv), hv, wv)
        dh, dw = vjp(jnp.ones((), F32))
        return jnp.full((1, 128), val, F32), dh, dw

    row = lambda i: (i, 0)
    zero = lambda i: (0, 0)
    return _vcall(name, fn, (L // tb,),
                  [(h, (tb, D), row), (w.reshape(1, D), (1, D), zero), (tgt, (tb, D), row)],
                  [((1, 128), F32, (1, 128), zero), ((L, D), F32, (tb, D), row), ((1, D), F32, (1, D), zero)],
                  acc={0: "all", 2: "all"})


def _bf(x):
    return x.astype(BF)


def _dot(a, b, dims, precision=None):
    return lax.dot_general(a, b, (dims, ((), ())), preferred_element_type=F32, precision=precision)


def _tri(n, reverse):
    r = lax.broadcasted_iota(jnp.int32, (n, n), 0)
    c = lax.broadcasted_iota(jnp.int32, (n, n), 1)
    return (r <= c) if reverse else (r >= c)


def _gla_chunk(q_raw, f_raw, v, lb3, S, reverse):
    C, SB = HG_CHUNK, HG_SUB
    row3 = lax.broadcasted_iota(jnp.int32, (3, HG_HD), 0)
    e = jnp.exp(lb3 - jnp.max(lb3, axis=0, keepdims=True))
    lb = jnp.sum(jnp.where(row3 == 0, e, 0.0), axis=0, keepdims=True) / jnp.sum(e, axis=0, keepdims=True)
    q = q_raw * jax.nn.sigmoid(q_raw)
    f = lb + (1.0 - lb) * jax.nn.sigmoid(f_raw)
    g = jnp.log(f)
    k = 1.0 - f
    b = _dot(_tri(C, reverse).astype(F32), g, ((1,), (0,)), precision=HI)
    row = lax.broadcasted_iota(jnp.int32, (C, 1), 0)

    def rowof(x, t):
        return jnp.sum(jnp.where(row == t, x, 0.0), axis=0, keepdims=True)

    o = _dot(_bf(q * jnp.exp(b)), _bf(S), ((1,), (0,)))
    att = None
    for i in range(C // SB):
        lo = SB * i
        if (not reverse and i == 0) or (reverse and i == C // SB - 1):
            continue
        first = lo + SB - 1 if reverse else lo
        r = rowof(b, first) - rowof(g, first)
        in_blk = jnp.logical_and(row >= lo, row < lo + SB)
        before = (row >= lo + SB) if reverse else (row < lo)
        qi = q * jnp.exp(jnp.where(in_blk, b - r, NEG))
        kk = k * jnp.exp(jnp.where(before, r - b, NEG))
        a_i = _dot(_bf(qi), _bf(kk), ((1,), (1,)))
        att = a_i if att is None else att + a_i
    o = o + _dot(_bf(att), _bf(v), ((1,), (0,)))
    t_i = lax.broadcasted_iota(jnp.int32, (SB, SB, HG_HD), 0)
    s_i = lax.broadcasted_iota(jnp.int32, (SB, SB, HG_HD), 1)
    pair = (t_i <= s_i) if reverse else (t_i >= s_i)
    diag = []
    for i in range(C // SB):
        lo = SB * i
        qb, kb, bb, vb = q[lo:lo + SB], k[lo:lo + SB], b[lo:lo + SB], v[lo:lo + SB]
        shp = (SB, SB, HG_HD)
        dif = lax.broadcast_in_dim(bb, shp, (0, 2)) - lax.broadcast_in_dim(bb, shp, (1, 2))
        w = lax.broadcast_in_dim(qb, shp, (0, 2)) * jnp.exp(jnp.where(pair, dif, NEG)) * lax.broadcast_in_dim(kb, shp, (1, 2))
        diag.append(_dot(_bf(jnp.sum(w, axis=2)), _bf(vb), ((1,), (0,))))
    o = o + jnp.concatenate(diag, axis=0)
    btot = rowof(b, 0 if reverse else C - 1)
    eye = lax.broadcasted_iota(jnp.int32, (HG_HD, HG_HD), 0) == lax.broadcasted_iota(jnp.int32, (HG_HD, HG_HD), 1)
    btot_col = jnp.sum(jnp.where(eye, btot, 0.0), axis=1, keepdims=True)
    s_new = jnp.exp(btot_col) * S + _dot(_bf(k * jnp.exp(btot - b)), _bf(v), ((0,), (0,)))
    return o, s_new


def _gla_fwd(name, pa, lbl, reverse):
    L = pa.shape[0]
    C = HG_CHUNK
    nc = L // C
    cidx = (lambda i: nc - 1 - i) if reverse else (lambda i: i)
    sec = 2 if reverse else 1
    hb_n, nh = HG_HB, HG_HEADS // HG_HB

    def fn(qr, fr, v, lb3, s_ref):
        @pl.when(pl.program_id(1) == 0)
        def _():
            s_ref[...] = jnp.zeros_like(s_ref)

        outs, olds = [], []
        for hb in range(hb_n):
            sl = slice(HG_HD * hb, HG_HD * (hb + 1))
            s = s_ref[hb]
            o, s_new = _gla_chunk(qr[:, sl], fr[:, sl], v[:, sl], lb3[:, sl], s, reverse)
            s_ref[hb] = s_new
            outs.append(o)
            olds.append(s)
        return jnp.concatenate(outs, axis=1), jnp.stack(olds)[None]

    blk = (C, HG_HD * hb_n)
    return _vcall(name, fn, (nh, nc),
                  [(pa, blk, lambda h, i: (cidx(i), h)), (pa, blk, lambda h, i: (cidx(i), sec * nh + h)),
                   (pa, blk, lambda h, i: (cidx(i), 3 * nh + h)), (lbl, (3, HG_HD * hb_n), lambda h, i: (0, h))],
                  [((L, D), F32, blk, lambda h, i: (cidx(i), h)),
                   ((nc, HG_HEADS, HG_HD, HG_HD), F32, (1, hb_n, HG_HD, HG_HD), lambda h, i: (cidx(i), h, 0, 0))],
                  scratch=[pltpu.VMEM((hb_n, HG_HD, HG_HD), F32)])


def _gla_bwd(name, pa, lbl, s_in, do, reverse, prev=None):
    L = pa.shape[0]
    C = HG_CHUNK
    nc = L // C
    cidx = (lambda i: i) if reverse else (lambda i: nc - 1 - i)
    sec = 2 if reverse else 1
    n_prev = 0 if prev is None else 2
    hb_n, nh = HG_HB, HG_HEADS // HG_HB

    def fn(qr, fr, v, lb3, s, dov, *rest):
        ds_ref = rest[n_prev]

        @pl.when(pl.program_id(1) == 0)
        def _():
            ds_ref[...] = jnp.zeros_like(ds_ref)

        res = []
        for hb in range(hb_n):
            sl = slice(HG_HD * hb, HG_HD * (hb + 1))
            _, vjp = jax.vjp(lambda *a: _gla_chunk(*a, reverse), qr[:, sl], fr[:, sl], v[:, sl], lb3[:, sl], s[0, hb])
            dq, df, dv, dlb, ds = vjp((dov[:, sl], ds_ref[hb]))
            ds_ref[hb] = ds
            if n_prev:
                dq, dv = dq + rest[0][:, sl], dv + rest[1][:, sl]
            res.append((dq, df, dv, dlb))
        return tuple(jnp.concatenate([r[j] for r in res], axis=1) for j in range(4))

    blk = (C, HG_HD * hb_n)
    at = lambda h, i: (cidx(i), h)
    ins = [(pa, blk, at), (pa, blk, lambda h, i: (cidx(i), sec * nh + h)), (pa, blk, lambda h, i: (cidx(i), 3 * nh + h)),
           (lbl, (3, HG_HD * hb_n), lambda h, i: (0, h)),
           (s_in, (1, hb_n, HG_HD, HG_HD), lambda h, i: (cidx(i), h, 0, 0)), (do, blk, at)]
    if prev is not None:
        ins += [(prev[0], blk, at), (prev[1], blk, at)]
    sum_dt = F32 if prev is None else BF
    return _vcall(name, fn, (nh, nc), ins,
                  [((L, D), sum_dt, blk, at), ((L, D), BF, blk, at), ((L, D), sum_dt, blk, at),
                   ((3, D), F32, (3, HG_HD * hb_n), lambda h, i: (0, h))],
                  acc={3: "last"}, scratch=[pltpu.VMEM((hb_n, HG_HD, HG_HD), F32)])


def _hgout(o_f, o_b, g, nw):
    o = o_f + o_b
    return _rms(o, nw) * (g * jax.nn.sigmoid(g))


def _hgout_fwd(name, o_f, o_b, pa, nw):
    L = o_f.shape[0]
    tb = _pick(L, (ROWS,))
    blk = (tb, HG_HD)
    at = lambda h, i: (i, h)
    return _vcall(name, _hgout, (HG_HEADS, L // tb),
                  [(o_f, blk, at), (o_b, blk, at), (pa, blk, lambda h, i: (i, 32 + h)),
                   (nw.reshape(1, HG_HD), (1, HG_HD), lambda h, i: (0, 0))],
                  [((L, D), BF, blk, at)])


def _hgout_bwd(name, o_f, o_b, pa, nw, dy):
    L = o_f.shape[0]
    tb = _pick(L, (ROWS,))

    def fn(ofv, obv, gv, nwv, dyv):
        _, vjp = jax.vjp(_hgout, ofv, obv, gv, nwv)
        do, _, dg, dnw = vjp(dyv)
        return do, dg, dnw

    blk = (tb, HG_HD)
    at = lambda h, i: (i, h)
    zero = lambda h, i: (0, 0)
    return _vcall(name, fn, (HG_HEADS, L // tb),
                  [(o_f, blk, at), (o_b, blk, at), (pa, blk, lambda h, i: (i, 32 + h)),
                   (nw.reshape(1, HG_HD), (1, HG_HD), zero), (dy, blk, at)],
                  [((L, D), F32, blk, at), ((L, D), BF, blk, at), ((1, HG_HD), F32, (1, HG_HD), zero)],
                  acc={2: "all"})


def _shift(x, s):
    if s == 0:
        return x
    n = x.shape[0]
    t = lax.broadcasted_iota(jnp.int32, (n, 1), 0)
    if s > 0:
        return jnp.where(t >= s, pltpu.roll(x, s, 0), 0.0)
    return jnp.where(t < n + s, pltpu.roll(x, n + s, 0), 0.0)


def _conv(x, w, b):
    kk = w.shape[0]
    p = (kk - 1) // 2
    y = b
    for j in range(kk):
        y = y + w[j:j + 1] * _shift(x, p - j)
    return y


def _conv_bwd(x, w, dc):
    kk = w.shape[0]
    p = (kk - 1) // 2
    dx = None
    dws = []
    for j in range(kk):
        t = w[j:j + 1] * _shift(dc, j - p)
        dx = t if dx is None else dx + t
        dws.append(jnp.sum(dc * _shift(x, p - j), axis=0, keepdims=True))
    rows = lax.broadcasted_iota(jnp.int32, (kk, 1), 0)
    dw = None
    for j in range(kk):
        t = jnp.where(rows == j, dws[j], 0.0)
        dw = t if dw is None else dw + t
    return dx, dw, jnp.sum(dc, axis=0, keepdims=True)


def _silu_grad(c):
    s = jax.nn.sigmoid(c)
    return s * (1.0 + c * (1.0 - s))


def _glu_fwd(name, pf, cw, cb):
    L = pf.shape[0]
    tc = 128
    nt = D_FF // tc

    def fn(gate, val, w, b):
        c = _conv(gate, w, b)
        return c * jax.nn.sigmoid(c) * val

    return _vcall(name, fn, (nt,),
                  [(pf, (L, tc), lambda j: (0, j)), (pf, (L, tc), lambda j: (0, nt + j)),
                   (cw, (3, tc), lambda j: (0, j)), (cb.reshape(1, D_FF), (1, tc), lambda j: (0, j))],
                  [((L, D_FF), BF, (L, tc), lambda j: (0, j))])


def _glu_bwd(name, pf, cw, cb, dy):
    L = pf.shape[0]
    tc = 128
    nt = D_FF // tc

    def fn(gate, val, w, b, dyv):
        c = _conv(gate, w, b)
        sc = c * jax.nn.sigmoid(c)
        dc = dyv * val * _silu_grad(c)
        dgate, dw, db = _conv_bwd(gate, w, dc)
        return dgate, dyv * sc, dw, db

    col = lambda j: (0, j)
    return _vcall(name, fn, (nt,),
                  [(pf, (L, tc), col), (pf, (L, tc), lambda j: (0, nt + j)), (cw, (3, tc), col),
                   (cb.reshape(1, D_FF), (1, tc), col), (dy, (L, tc), col)],
                  [((L, D_FF), BF, (L, tc), col), ((L, D_FF), BF, (L, tc), col),
                   ((3, D_FF), F32, (3, tc), col), ((1, D_FF), F32, (1, tc), col)])


def _perm_tile(j):
    return jnp.where(j < 16, 4 * (j // 2) + j % 2, jnp.where(j < 24, 4 * (j - 16) + 2, 4 * (j - 24) + 3))


def _mpre_fwd(name, pb, cw, cb):
    L = pb.shape[0]
    tc = 128

    def fn(x, w, b):
        c = _conv(x, w, b)
        return c * jax.nn.sigmoid(c)

    return _vcall(name, fn, (CONV_DIM // tc,),
                  [(pb, (L, tc), lambda j: (0, 16 + j)), (cw, (5, tc), lambda j: (0, j)),
                   (cb.reshape(1, CONV_DIM), (1, tc), lambda j: (0, j))],
                  [((L, CONV_DIM), F32, (L, tc), lambda j: (0, _perm_tile(j)))])


def _mpre_bwd(name, pb, cw, cb, dact):
    L = pb.shape[0]
    tc = 128

    def fn(x, w, b, da):
        c = _conv(x, w, b)
        return _conv_bwd(x, w, da * _silu_grad(c))

    col = lambda j: (0, j)
    return _vcall(name, fn, (CONV_DIM // tc,),
                  [(pb, (L, tc), lambda j: (0, 16 + j)), (cw, (5, tc), col), (cb.reshape(1, CONV_DIM), (1, tc), col),
                   (dact, (L, tc), lambda j: (0, _perm_tile(j)))],
                  [((L, CONV_DIM), BF, (L, tc), col), ((5, CONV_DIM), F32, (5, tc), col),
                   ((1, CONV_DIM), F32, (1, tc), col)])


def _softplus(x):
    return jnp.maximum(x, 0.0) + jnp.log(1.0 + jnp.exp(-jnp.abs(x)))


def _dt_fwd(name, pb, dtb, alog):
    L = pb.shape[0]
    tb = _pick(L, (1024, ROWS))

    def fn(x, bias, al):
        dt = _softplus(x + bias)
        return dt, dt * (-jnp.exp(al))

    row = lambda i: (i, 0)
    zero = lambda i: (0, 0)
    return _vcall(name, fn, (L // tb,),
                  [(pb, (tb, 128), lambda i: (i, 48)), (dtb, (1, 128), zero), (alog, (1, 128), zero)],
                  [((L, 128), F32, (tb, 128), row), ((L, 128), F32, (tb, 128), row)])


def _dt_bwd(name, pb, dtb, alog, ddt_f, dla_f, ddt_b, dla_b):
    L = pb.shape[0]
    tb = _pick(L, (1024, ROWS))

    def fn(x, bias, al, a1, b1, a2, b2):
        ddt = jnp.sum(a1, axis=0) + jnp.sum(a2, axis=0)
        dla = jnp.sum(b1, axis=0) + jnp.sum(b2, axis=0)
        z = x + bias
        dt = _softplus(z)
        a = -jnp.exp(al)
        dz = (ddt + dla * a) * jax.nn.sigmoid(z)
        return dz, jnp.sum(dz, axis=0, keepdims=True), jnp.sum(dla * dt, axis=0, keepdims=True) * a

    zero = lambda i: (0, 0)
    g3 = (SSM_GROUPS, tb, 128)
    at3 = lambda i: (0, i, 0)
    return _vcall(name, fn, (L // tb,),
                  [(pb, (tb, 128), lambda i: (i, 48)), (dtb, (1, 128), zero), (alog, (1, 128), zero),
                   (ddt_f, g3, at3), (dla_f, g3, at3), (ddt_b, g3, at3), (dla_b, g3, at3)],
                  [((L, 128), BF, (tb, 128), lambda i: (i, 0)), ((1, 128), F32, (1, 128), zero),
                   ((1, 128), F32, (1, 128), zero)], acc={1: "all", 2: "all"})


def _ssd_chunk(xa, dt, la, hs, head0, reverse):
    C = SSD_CHUNK
    P4 = SSM_HPG * SSM_HD
    xs, bm, cm = xa[:, :P4], xa[:, P4:P4 + SSM_N], xa[:, P4 + SSM_N:]
    lane = lax.broadcasted_iota(jnp.int32, (1, 128), 1)
    col_head = lax.broadcasted_iota(jnp.int32, (1, P4), 1) // SSM_HD
    row_head = lax.broadcasted_iota(jnp.int32, (P4, 1), 0) // SSM_HD
    row = lax.broadcasted_iota(jnp.int32, (C, 1), 0)
    eye = lax.broadcasted_iota(jnp.int32, (C, C), 0) == lax.broadcasted_iota(jnp.int32, (C, C), 1)
    tri = _tri(C, reverse)
    acum = _dot(tri.astype(F32), la, ((1,), (0,)), precision=HI)
    cb = _dot(_bf(cm), _bf(bm), ((1,), (1,)))
    last = 0 if reverse else C - 1
    dt_x, ea_x, dec_x, y = 0.0, 0.0, 0.0, 0.0
    atot_rows = 0.0
    lmats = []
    for j in range(SSM_HPG):
        sel = lane == head0 + j
        dt_j = jnp.sum(jnp.where(sel, dt, 0.0), axis=1, keepdims=True)
        ac_j = jnp.sum(jnp.where(sel, acum, 0.0), axis=1, keepdims=True)
        ac_row = jnp.sum(jnp.where(eye, ac_j, 0.0), axis=0, keepdims=True)
        atot = jnp.sum(jnp.where(row == last, ac_j, 0.0), axis=0, keepdims=True)
        lmats.append(jnp.exp(jnp.where(tri, ac_j - ac_row, NEG)))
        mine = col_head == j
        dt_x = dt_x + jnp.where(mine, dt_j, 0.0)
        ea_x = ea_x + jnp.where(mine, jnp.exp(ac_j), 0.0)
        dec_x = dec_x + jnp.where(mine, jnp.exp(atot - ac_j), 0.0)
        atot_rows = atot_rows + jnp.where(row_head == j, jnp.exp(atot), 0.0)
    xd = xs * dt_x
    xdb = _bf(xd)
    for j in range(SSM_HPG):
        y = y + jnp.where(col_head == j, _dot(_bf(cb * lmats[j]), xdb, ((1,), (0,))), 0.0)
    y = y + _dot(_bf(cm), _bf(hs), ((1,), (1,))) * ea_x
    hs_new = atot_rows * hs + _dot(_bf(xd * dec_x), _bf(bm), ((0,), (0,)))
    return y, hs_new


def _ssd_fwd(name, xact, dt, la, reverse):
    L = xact.shape[0]
    C = SSD_CHUNK
    nc = L // C
    cidx = (lambda i: nc - 1 - i) if reverse else (lambda i: i)
    base = SSM_HEADS if reverse else 0
    P4 = SSM_HPG * SSM_HD

    gb_n = SSM_GB

    def fn(xa, dtv, lav, h_ref):
        @pl.when(pl.program_id(1) == 0)
        def _():
            h_ref[...] = jnp.zeros_like(h_ref)

        ys, olds = [], []
        for gb in range(gb_n):
            hs = h_ref[gb]
            head0 = base + SSM_HPG * (gb_n * pl.program_id(0) + gb)
            y, hs_new = _ssd_chunk(xa[:, 512 * gb:512 * (gb + 1)], dtv, lav, hs, head0, reverse)
            h_ref[gb] = hs_new
            ys.append(y)
            olds.append(hs)
        return jnp.concatenate(ys, axis=1), jnp.stack(olds)[None]

    return _vcall(name, fn, (SSM_GROUPS // gb_n, nc),
                  [(xact, (C, 512 * gb_n), lambda g, i: (cidx(i), g)), (dt, (C, 128), lambda g, i: (cidx(i), 0)),
                   (la, (C, 128), lambda g, i: (cidx(i), 0))],
                  [((L, D_INNER), F32, (C, P4 * gb_n), lambda g, i: (cidx(i), g)),
                   ((nc, SSM_GROUPS, P4, SSM_N), F32, (1, gb_n, P4, SSM_N), lambda g, i: (cidx(i), g, 0, 0))],
                  scratch=[pltpu.VMEM((gb_n, P4, SSM_N), F32)])


def _ssd_bwd(name, xact, dt, la, h_in, dy, reverse, prev_xs=None, prev_all=None):
    L = xact.shape[0]
    C = SSD_CHUNK
    nc = L // C
    cidx = (lambda i: i) if reverse else (lambda i: nc - 1 - i)
    base = SSM_HEADS if reverse else 0
    P4 = SSM_HPG * SSM_HD

    gb_n = SSM_GB

    def fn(xa, dtv, lav, hs, dyv, pv, dh_ref):
        @pl.when(pl.program_id(1) == 0)
        def _():
            dh_ref[...] = jnp.zeros_like(dh_ref)

        res = []
        for gb in range(gb_n):
            head0 = base + SSM_HPG * (gb_n * pl.program_id(0) + gb)
            _, vjp = jax.vjp(lambda a, b, c, d: _ssd_chunk(a, b, c, d, head0, reverse),
                             xa[:, 512 * gb:512 * (gb + 1)], dtv, lav, hs[0, gb])
            dxa, ddt, dla, dh = vjp((dyv[:, P4 * gb:P4 * (gb + 1)], dh_ref[gb]))
            dh_ref[gb] = dh
            if prev_all is not None:
                dxa = dxa + pv[:, 512 * gb:512 * (gb + 1)]
            else:
                dxa = dxa + jnp.concatenate([pv[:, P4 * gb:P4 * (gb + 1)], jnp.zeros((C, 2 * SSM_N), F32)], axis=1)
            res.append((dxa, ddt, dla))
        return (jnp.concatenate([r[0] for r in res], axis=1), jnp.stack([r[1] for r in res]),
                jnp.stack([r[2] for r in res]))

    at = lambda g, i: (cidx(i), g)
    at0 = lambda g, i: (cidx(i), 0)
    pv = (prev_all, (C, 512 * gb_n), at) if prev_all is not None else (prev_xs, (C, P4 * gb_n), at)
    return _vcall(name, fn, (SSM_GROUPS // gb_n, nc),
                  [(xact, (C, 512 * gb_n), at), (dt, (C, 128), at0), (la, (C, 128), at0),
                   (h_in, (1, gb_n, P4, SSM_N), lambda g, i: (cidx(i), g, 0, 0)), (dy, (C, P4 * gb_n), at), pv],
                  [((L, CONV_DIM), F32, (C, 512 * gb_n), at),
                   ((SSM_GROUPS, L, 128), F32, (gb_n, C, 128), lambda g, i: (g, cidx(i), 0)),
                   ((SSM_GROUPS, L, 128), F32, (gb_n, C, 128), lambda g, i: (g, cidx(i), 0))],
                  scratch=[pltpu.VMEM((gb_n, P4, SSM_N), F32)])


def _mpost(y_f, y_b, xs, z, dsk, nw):
    y = (y_f + y_b + xs * dsk) * (z * jax.nn.sigmoid(z))
    return _rms(y, nw)


def _mpost_fwd(name, y_f, y_b, xact, pb, dsk, nw):
    L = y_f.shape[0]
    tb = _pick(L, (ROWS,))
    blk = (tb, 256)
    at = lambda g, i: (i, g)
    par = lambda g, i: (0, g)
    return _vcall(name, _mpost, (SSM_GROUPS, L // tb),
                  [(y_f, blk, at), (y_b, blk, at), (xact, blk, lambda g, i: (i, 2 * g)), (pb, blk, at),
                   (dsk, (1, 256), par), (nw, (1, 256), par)],
                  [((L, D_INNER), BF, blk, at)])


def _mpost_bwd(name, y_f, y_b, xact, pb, dsk, nw, dy):
    L = y_f.shape[0]
    tb = _pick(L, (ROWS,))

    def fn(yf, yb, xs, z, dskv, nwv, dyv):
        _, vjp = jax.vjp(_mpost, yf, yb, xs, z, dskv, nwv)
        dyf, _, dxs, dz, ddsk, dnw = vjp(dyv)
        return dyf, dxs, dz, ddsk, dnw

    blk = (tb, 256)
    at = lambda g, i: (i, g)
    par = lambda g, i: (0, g)
    return _vcall(name, fn, (SSM_GROUPS, L // tb),
                  [(y_f, blk, at), (y_b, blk, at), (xact, blk, lambda g, i: (i, 2 * g)), (pb, blk, at),
                   (dsk, (1, 256), par), (nw, (1, 256), par), (dy, blk, at)],
                  [((L, D_INNER), F32, blk, at), ((L, D_INNER), F32, blk, at), ((L, D_INNER), BF, blk, at),
                   ((1, D_INNER), F32, (1, 256), par), ((1, D_INNER), F32, (1, 256), par)],
                  acc={3: "last", 4: "last"})


def _ffn_fwd(tag, h, nw, w_in, cw, cb, w_out):
    u = _rms_fwd(f"{tag}_norm", h, nw)
    pf = _mm(f"{tag}_in", u, w_in, "nn")
    yf = _glu_fwd(f"{tag}_glu", pf, cw, cb)
    return _mm(f"{tag}_out", yf, w_out, "nn", add=h), (u, pf, yf)


def _ffn_bwd(tag, h, nw, w_in, cw, cb, w_out, saved, dh):
    u, pf, yf = saved
    d_w_out = _mm(f"{tag}_dwout", yf, dh, "tn")
    dyf = _mm(f"{tag}_dy", dh, w_out, "nt")
    dgate, dval, dcw, dcb = _glu_bwd(f"{tag}_dglu", pf, cw, cb, dyf)
    dpf = jnp.concatenate([dgate, dval], axis=1)
    d_w_in = _mm(f"{tag}_dwin", u, dpf, "tn")
    du = _mm(f"{tag}_du", dpf, w_in, "nt")
    dh_in, dnw = _rms_bwd(f"{tag}_dnorm", du, h, nw, dh)
    return dh_in, dnw, d_w_in, dcw, dcb, d_w_out


def _sequence_grads(x, tgt, p):
    g = {}
    lbl = p["a_lb_logits"]
    u1 = _rms_fwd("a_norm", x, p["norm1_w"][0])
    pa = _mm("a_in", u1, p["a_w_in"], "nn")
    o_f, s_f = _gla_fwd("a_scan_f", pa, lbl, False)
    o_b, s_b = _gla_fwd("a_scan_b", pa, lbl, True)
    ya = _hgout_fwd("a_gate", o_f, o_b, pa, p["a_norm_w"])
    h1 = _mm("a_out", ya, p["a_w_out"], "nn", add=x)
    h2, ffn0 = _ffn_fwd("f0", h1, p["norm2_w"][0], p["ffn_w_in"][0], p["ffn_conv_w"][0], p["ffn_conv_b"][0], p["ffn_w_out"][0])
    u3 = _rms_fwd("b_norm", h2, p["norm1_w"][1])
    pb = _mm("b_in", u3, p["b_w_in"], "nn")
    xact = _mpre_fwd("b_conv", pb, p["b_conv_w"], p["b_conv_b"])
    dt, la = _dt_fwd("b_dt", pb, p["b_dt_bias"], p["b_a_log"])
    y_f, hs_f = _ssd_fwd("b_scan_f", xact, dt, la, False)
    y_b, hs_b = _ssd_fwd("b_scan_b", xact, dt, la, True)
    yb = _mpost_fwd("b_gate", y_f, y_b, xact, pb, p["b_d_skip"], p["b_norm_w"])
    h3 = _mm("b_out", yb, p["b_w_out"], "nn", add=h2)
    h4, ffn1 = _ffn_fwd("f1", h3, p["norm2_w"][1], p["ffn_w_in"][1], p["ffn_conv_w"][1], p["ffn_conv_b"][1], p["ffn_w_out"][1])
    loss, dh4, g["final_norm_w"] = _loss_head("head", h4, tgt, p["final_norm_w"])
    dh3, dn2_1, dwin1, dcw1, dcb1, dwout1 = _ffn_bwd("f1", h3, p["norm2_w"][1], p["ffn_w_in"][1], p["ffn_conv_w"][1],
                                                     p["ffn_conv_b"][1], p["ffn_w_out"][1], ffn1, dh4)
    g["b_w_out"] = _mm("b_dwout", yb, dh3, "tn")
    dyb = _mm("b_dy", dh3, p["b_w_out"], "nt")
    dys, dxs, dz, g["b_d_skip"], g["b_norm_w"] = _mpost_bwd("b_dgate", y_f, y_b, xact, pb, p["b_d_skip"], p["b_norm_w"], dyb)
    dxa1, ddt_f, dla_f = _ssd_bwd("b_dscan_f", xact, dt, la, hs_f, dys, False, prev_xs=dxs)
    dxa, ddt_b, dla_b = _ssd_bwd("b_dscan_b", xact, dt, la, hs_b, dys, True, prev_all=dxa1)
    dxbc, g["b_conv_w"], g["b_conv_b"] = _mpre_bwd("b_dconv", pb, p["b_conv_w"], p["b_conv_b"], dxa)
    ddtr, g["b_dt_bias"], g["b_a_log"] = _dt_bwd("b_ddt", pb, p["b_dt_bias"], p["b_a_log"], ddt_f, dla_f, ddt_b, dla_b)
    dpb = jnp.concatenate([dz, dxbc, ddtr], axis=1)
    g["b_w_in"] = _mm("b_dwin", u3, dpb, "tn")
    du3 = _mm("b_du", dpb, p["b_w_in"], "nt")
    dh2, dn1_1 = _rms_bwd("b_dnorm", du3, h2, p["norm1_w"][1], dh3)
    dh1, dn2_0, dwin0, dcw0, dcb0, dwout0 = _ffn_bwd("f0", h1, p["norm2_w"][0], p["ffn_w_in"][0], p["ffn_conv_w"][0],
                                                     p["ffn_conv_b"][0], p["ffn_w_out"][0], ffn0, dh2)
    g["a_w_out"] = _mm("a_dwout", ya, dh1, "tn")
    dya = _mm("a_dy", dh1, p["a_w_out"], "nt")
    do, dg, g["a_norm_w"] = _hgout_bwd("a_dgate", o_f, o_b, pa, p["a_norm_w"], dya)
    dq1, df1, dv1, dl1 = _gla_bwd("a_dscan_f", pa, lbl, s_f, do, False)
    dq, df2, dv, dl2 = _gla_bwd("a_dscan_b", pa, lbl, s_b, do, True, prev=(dq1, dv1))
    dpa = jnp.concatenate([dq, df1, df2, dv, dg], axis=1)
    g["a_w_in"] = _mm("a_dwin", u1, dpa, "tn")
    du1 = _mm("a_du", dpa, p["a_w_in"], "nt")
    dx, dn1_0 = _rms_bwd("a_dnorm", du1, x, p["norm1_w"][0], dh1)
    g["a_lb_logits"] = (dl1, dl2)
    g["norm1_w"] = (dn1_0, dn1_1)
    g["norm2_w"] = (dn2_0, dn2_1)
    g["ffn_w_in"] = (dwin0, dwin1)
    g["ffn_conv_w"] = (dcw0, dcw1)
    g["ffn_conv_b"] = (dcb0, dcb1)
    g["ffn_w_out"] = (dwout0, dwout1)
    return loss, dx, g


def _here():
    return lax.axis_index("x"), lax.axis_index("y"), lax.axis_index("c")


def _allgather8(name, src, by_core=False):
    blk = src.shape[1:] if by_core else src.shape

    def body(x_ref, out_ref, send_sems, recv_sems, local_sem):
        x, y, c = _here()
        me, sibling = (x, y, c), (x, y, 1 - c)
        chips = [(1 - x, y), (x, 1 - y), (1 - x, 1 - y)]
        own = x_ref.at[c] if by_core else x_ref

        def slot(px, py, pc):
            return out_ref.at[4 * px + 2 * py + pc]

        def copy(k, block, to, from_own=False):
            return pltpu.make_async_remote_copy(
                src_ref=own if from_own else slot(*block), dst_ref=slot(*block),
                send_sem=send_sems.at[k], recv_sem=recv_sems.at[k], device_id=to, device_id_type=MESH)

        mine = pltpu.make_async_copy(own, slot(*me), local_sem)
        mine.start()
        first = [copy(0, me, sibling, from_own=True)]
        first += [copy(1 + j, me, (*chip, c), from_own=True) for j, chip in enumerate(chips)]
        for cp in first:
            cp.start()
        passed = [copy(4 + j, (*chip, c), sibling) for j, chip in enumerate(chips)]
        for j, chip in enumerate(chips):
            copy(1 + j, (*chip, c), me).wait_recv()
            passed[j].start()
        copy(0, sibling, me).wait_recv()
        for j, chip in enumerate(chips):
            copy(4 + j, (*chip, 1 - c), me).wait_recv()
        for cp in first + passed:
            cp.wait_send()
        mine.wait()

    return pl.pallas_call(
        body, name=name,
        out_shape=jax.ShapeDtypeStruct((8,) + tuple(blk), src.dtype),
        in_specs=[pl.BlockSpec(memory_space=pl.ANY)],
        out_specs=pl.BlockSpec(memory_space=pl.ANY),
        scratch_shapes=[pltpu.SemaphoreType.DMA((7,)), pltpu.SemaphoreType.DMA((7,)), pltpu.SemaphoreType.DMA],
    )(src)


def _exchange(name, ins, out_shapes, plan, n_remote, n_local, alias=False):
    def body(*refs):
        in_refs = refs[:len(ins)]
        out_refs = refs[len(ins):len(ins) + len(out_shapes)]
        send_sems, recv_sems, local_sems = refs[len(ins) + len(out_shapes):]
        remote, local = plan(in_refs, out_refs)
        copies = [pltpu.make_async_copy(s, d, local_sems.at[i]) for i, (s, d) in enumerate(local)]
        copies += [pltpu.make_async_remote_copy(src_ref=s, dst_ref=d, send_sem=send_sems.at[i], recv_sem=recv_sems.at[i],
                                                device_id=dev, device_id_type=MESH)
                   for i, (s, d, dev) in enumerate(remote)]
        for cp in copies:
            cp.start()
        for cp in copies:
            cp.wait()

    out = pl.pallas_call(
        body, name=name,
        out_shape=[jax.ShapeDtypeStruct(s, dt) for s, dt in out_shapes],
        in_specs=[pl.BlockSpec(memory_space=pl.ANY)] * len(ins),
        out_specs=[pl.BlockSpec(memory_space=pl.ANY)] * len(out_shapes),
        scratch_shapes=[pltpu.SemaphoreType.DMA((n_remote,)), pltpu.SemaphoreType.DMA((n_remote,)),
                        pltpu.SemaphoreType.DMA((max(n_local, 1),))],
        input_output_aliases={0: 0} if alias else {},
    )(*ins)
    return out


def _reduce_to_owner(gp):
    _, _, r, n = gp.shape
    x, y, c = _here()
    place = jnp.stack([2 * x + y, c]).astype(jnp.int32)

    def plan1(ins, outs):
        x, y, c = _here()
        (g,), (got,) = ins, outs
        return [(g.at[j, 1 - c], got.at[j], (x, y, 1 - c)) for j in range(4)], []

    (got,) = _exchange("g_pair", [gp], [((4, r, n), F32)], plan1, 4, 0)
    tb = _pick(r, (560, 256, 128, 64, 16))
    chip_sum = _vcall("g_pair_sum", lambda a, b: a[0] + b, (4, r // tb),
                      [(gp, (1, 1, tb, n), lambda j, i, pl_: (j, pl_[1], i, 0)), (got, (1, tb, n), lambda j, i, pl_: (j, i, 0))],
                      [((4, r, n), BF, (1, tb, n), lambda j, i, pl_: (j, i, 0))], place=place)

    def plan2(ins, outs):
        x, y, c = _here()
        (s,), (got,) = ins, outs
        m = 2 * x + y
        remote = []
        for k in (1, 2, 3):
            t = (m + k) % 4
            remote.append((s.at[t], got.at[k - 1], (t // 2, t % 2, c)))
        return remote, []

    (got2,) = _exchange("g_chips", [chip_sum], [((3, r, n), BF)], plan2, 3, 0)
    full = _vcall("g_chip_sum",
                  lambda a, b: (((a[0].astype(F32) + b[0].astype(F32)) + b[1].astype(F32)) + b[2].astype(F32))[None],
                  (r // tb,), [(chip_sum, (1, tb, n), lambda i, pl_: (pl_[0], i, 0)), (got2, (3, tb, n), lambda i, pl_: (0, i, 0))],
                  [((2, r, n), F32, (1, tb, n), lambda i, pl_: (pl_[1], i, 0))], place=place)

    def plan3(ins, outs):
        x, y, c = _here()
        (full_ref,) = outs
        return [(full_ref.at[c], full_ref.at[c], (x, y, 1 - c))], []

    (full,) = _exchange("g_halves", [full], [((2, r, n), F32)], plan3, 1, 0, alias=True)
    return full


def _adam(name, w, g, m, v):
    rows, cols = w.shape
    tb = _pick(rows, (256, 128, 64, 8))

    def fn(wv, gv, mv, vv):
        m2 = ADAM_B1 * mv + (1.0 - ADAM_B1) * gv
        v2 = ADAM_B2 * vv + (1.0 - ADAM_B2) * jnp.square(gv)
        m_hat = m2 / (1.0 - ADAM_B1 ** ADAM_STEP)
        v_hat = v2 / (1.0 - ADAM_B2 ** ADAM_STEP)
        return -ADAM_LR * (m_hat / (jnp.sqrt(v_hat) + ADAM_EPS) + ADAM_WD * wv), m2, v2

    at = lambda i: (i, 0)
    return _vcall(name, fn, (rows // tb,), [(a, (tb, cols), at) for a in (w, g, m, v)],
                  [((rows, cols), F32, (tb, cols), at)] * 3)


def _pack(arrays, width, row_multiple, dtype):
    parts, offs, at = [], [], 0
    for a in arrays:
        flat = a.reshape(-1).astype(dtype)
        rows = -(-flat.shape[0] // (width * row_multiple)) * row_multiple
        parts.append(jnp.pad(flat, (0, rows * width - flat.shape[0])).reshape(rows, width))
        offs.append(at)
        at += rows
    return jnp.concatenate(parts, axis=0), offs


def _unpack(flat, shapes, offs):
    out = []
    for shp, at in zip(shapes, offs):
        n = 1
        for s in shp:
            n *= s
        rows = -(-n // flat.shape[1])
        out.append(flat[at:at + rows].reshape(-1)[:n].reshape(shp))
    return out


_BIG = ("a_w_in", "a_w_out", "b_w_in", "b_w_out", "ffn_w_in", "ffn_w_out")
_BIG_AXIS = {"a_w_in": 2, "a_w_out": 1, "b_w_in": 2, "b_w_out": 1, "ffn_w_in": 2, "ffn_w_out": 1}
_SMALL_SPLIT = ("b_conv_w", "b_conv_b", "b_norm_w", "ffn_conv_w")
_SMALL = ("norm1_w", "norm2_w", "a_lb_logits", "a_norm_w", "b_conv_w", "b_conv_b", "b_dt_bias", "b_a_log", "b_d_skip",
          "b_norm_w", "ffn_conv_w", "ffn_conv_b", "final_norm_w")
_ORDER = ("norm1_w", "norm2_w", "a_w_in", "a_lb_logits", "a_norm_w", "a_w_out", "b_w_in", "b_conv_w", "b_conv_b", "b_dt_bias",
          "b_a_log", "b_d_skip", "b_norm_w", "b_w_out", "ffn_w_in", "ffn_conv_w", "ffn_conv_b", "ffn_w_out", "final_norm_w")


def kernel(x, norm1_w, norm2_w, a_w_in, a_lb_logits, a_norm_w, a_w_out, b_w_in, b_conv_w, b_conv_b, b_dt_bias, b_a_log, b_d_skip, b_norm_w, b_w_out, ffn_w_in, ffn_conv_w, ffn_conv_b, ffn_w_out, final_norm_w, loss_target, m_norm1_w, m_norm2_w, m_a_w_in, m_a_lb_logits, m_a_norm_w, m_a_w_out, m_b_w_in, m_b_conv_w, m_b_conv_b, m_b_dt_bias, m_b_a_log, m_b_d_skip, m_b_norm_w, m_b_w_out, m_ffn_w_in, m_ffn_conv_w, m_ffn_conv_b, m_ffn_w_out, m_final_norm_w, v_norm1_w, v_norm2_w, v_a_w_in, v_a_lb_logits, v_a_norm_w, v_a_w_out, v_b_w_in, v_b_conv_w, v_b_conv_b, v_b_dt_bias, v_b_a_log, v_b_d_skip, v_b_norm_w, v_b_w_out, v_ffn_w_in, v_ffn_conv_w, v_ffn_conv_b, v_ffn_w_out, v_final_norm_w):
    w = dict(norm1_w=norm1_w, norm2_w=norm2_w, a_w_in=a_w_in, a_lb_logits=a_lb_logits, a_norm_w=a_norm_w, a_w_out=a_w_out,
             b_w_in=b_w_in, b_conv_w=b_conv_w, b_conv_b=b_conv_b, b_dt_bias=b_dt_bias, b_a_log=b_a_log, b_d_skip=b_d_skip,
             b_norm_w=b_norm_w, b_w_out=b_w_out, ffn_w_in=ffn_w_in, ffn_conv_w=ffn_conv_w, ffn_conv_b=ffn_conv_b,
             ffn_w_out=ffn_w_out, final_norm_w=final_norm_w)
    mom = dict(norm1_w=m_norm1_w, norm2_w=m_norm2_w, a_w_in=m_a_w_in, a_lb_logits=m_a_lb_logits, a_norm_w=m_a_norm_w,
               a_w_out=m_a_w_out, b_w_in=m_b_w_in, b_conv_w=m_b_conv_w, b_conv_b=m_b_conv_b, b_dt_bias=m_b_dt_bias,
               b_a_log=m_b_a_log, b_d_skip=m_b_d_skip, b_norm_w=m_b_norm_w, b_w_out=m_b_w_out, ffn_w_in=m_ffn_w_in,
               ffn_conv_w=m_ffn_conv_w, ffn_conv_b=m_ffn_conv_b, ffn_w_out=m_ffn_w_out, final_norm_w=m_final_norm_w)
    var = dict(norm1_w=v_norm1_w, norm2_w=v_norm2_w, a_w_in=v_a_w_in, a_lb_logits=v_a_lb_logits, a_norm_w=v_a_norm_w,
               a_w_out=v_a_w_out, b_w_in=v_b_w_in, b_conv_w=v_b_conv_w, b_conv_b=v_b_conv_b, b_dt_bias=v_b_dt_bias,
               b_a_log=v_b_a_log, b_d_skip=v_b_d_skip, b_norm_w=v_b_norm_w, b_w_out=v_b_w_out, ffn_w_in=v_ffn_w_in,
               ffn_conv_w=v_ffn_conv_w, ffn_conv_b=v_ffn_conv_b, ffn_w_out=v_ffn_w_out, final_norm_w=v_final_norm_w)
    chip = 2 * lax.axis_index("x") + lax.axis_index("y")

    big_shapes = [w[n].shape for n in _BIG]
    wpack, big_offs = _pack([w[n] for n in _BIG], D, 32, BF)
    rp = wpack.shape[0]
    wall = _allgather8("w_gather", wpack.reshape(2, rp // 2, D), by_core=True).reshape(4, rp, D)
    shards = [_unpack(wall[j], big_shapes, big_offs) for j in range(4)]
    full = {n: jnp.concatenate([shards[j][i] for j in range(4)], axis=_BIG_AXIS[n]) for i, n in enumerate(_BIG)}
    small_shapes = [w[n].shape for n in _SMALL_SPLIT]
    spack, small_offs = _pack([w[n] for n in _SMALL_SPLIT], 128, 8, F32)
    sall = _allgather8("s_gather", spack)
    sshards = [_unpack(sall[2 * j], small_shapes, small_offs) for j in range(4)]
    sfull = {n: jnp.concatenate([sshards[j][i] for j in range(4)], axis=-1) for i, n in enumerate(_SMALL_SPLIT)}

    p = dict(
        norm1_w=norm1_w, norm2_w=norm2_w, a_lb_logits=a_lb_logits, a_norm_w=a_norm_w[0], final_norm_w=final_norm_w,
        a_w_in=full["a_w_in"][0], a_w_out=full["a_w_out"][0],
        b_w_in=jnp.pad(full["b_w_in"][0], ((0, 0), (0, B_PROJ_PAD - B_PROJ))), b_w_out=full["b_w_out"][0],
        ffn_w_in=full["ffn_w_in"], ffn_w_out=full["ffn_w_out"],
        b_conv_w=sfull["b_conv_w"][0], b_conv_b=sfull["b_conv_b"][0], b_norm_w=sfull["b_norm_w"],
        ffn_conv_w=sfull["ffn_conv_w"], ffn_conv_b=ffn_conv_b,
        b_dt_bias=jnp.pad(b_dt_bias.reshape(1, 2 * SSM_HEADS), ((0, 0), (0, 128 - 2 * SSM_HEADS))),
        b_a_log=jnp.pad(b_a_log.reshape(1, 2 * SSM_HEADS), ((0, 0), (0, 128 - 2 * SSM_HEADS))),
        b_d_skip=jnp.repeat(b_d_skip[0], SSM_HD)[None],
    )

    loss_row, dx, g = _sequence_grads(x[0], loss_target[0], p)

    gfull = {
        "a_w_in": g["a_w_in"][None], "a_w_out": g["a_w_out"][None], "b_w_in": g["b_w_in"][None, :, :B_PROJ],
        "b_w_out": g["b_w_out"][None], "ffn_w_in": jnp.stack(g["ffn_w_in"]), "ffn_w_out": jnp.stack(g["ffn_w_out"]),
    }
    per_chip = []
    for j in range(4):
        parts = []
        for n, shp in zip(_BIG, big_shapes):
            ax = _BIG_AXIS[n]
            parts.append(lax.slice_in_dim(gfull[n], j * shp[ax], (j + 1) * shp[ax], axis=ax))
        per_chip.append(_pack(parts, D, 32, F32)[0])
    gp = jnp.stack(per_chip).reshape(4, 2, rp // 2, D)
    gmine = _reduce_to_owner(gp).reshape(rp, D)
    grads = dict(zip(_BIG, _unpack(gmine, big_shapes, big_offs)))

    gsmall = {
        "norm1_w": jnp.concatenate(g["norm1_w"], axis=0), "norm2_w": jnp.concatenate(g["norm2_w"], axis=0),
        "a_lb_logits": jnp.stack(g["a_lb_logits"]), "a_norm_w": g["a_norm_w"], "b_conv_w": g["b_conv_w"],
        "b_conv_b": g["b_conv_b"], "b_dt_bias": g["b_dt_bias"], "b_a_log": g["b_a_log"], "b_d_skip": g["b_d_skip"],
        "b_norm_w": g["b_norm_w"], "ffn_conv_w": jnp.stack(g["ffn_conv_w"]),
        "ffn_conv_b": jnp.concatenate(g["ffn_conv_b"], axis=0), "final_norm_w": g["final_norm_w"],
    }
    pieces = [gsmall[n] for n in _SMALL] + [loss_row]
    piece_shapes = [a.shape for a in pieces]
    gspack, gs_offs = _pack(pieces, 128, 8, F32)
    rows = gspack.shape[0]
    gsall = _allgather8("gs_gather", gspack)

    def sum8(a):
        r = a[0]
        for i in range(1, 8):
            r = r + a[i]
        return r

    gssum = _vcall("gs_sum", sum8, (1,), [(gsall, (8, rows, 128), lambda i: (0, 0, 0))],
                   [((rows, 128), F32, (rows, 128), lambda i: (0, 0))])
    gs = dict(zip(_SMALL + ("loss",), _unpack(gssum, piece_shapes, gs_offs)))
    loss = gs["loss"][0, 0]
    lb2 = gs["a_lb_logits"]
    small_grads = {
        "norm1_w": gs["norm1_w"], "norm2_w": gs["norm2_w"], "a_lb_logits": lb2[0] + lb2[1], "a_norm_w": gs["a_norm_w"],
        "b_dt_bias": gs["b_dt_bias"][:, :2 * SSM_HEADS].reshape(1, 2, SSM_HEADS),
        "b_a_log": gs["b_a_log"][:, :2 * SSM_HEADS].reshape(1, 2, SSM_HEADS),
        "b_d_skip": gs["b_d_skip"].reshape(1, SSM_HEADS, SSM_HD).sum(axis=-1),
        "ffn_conv_b": gs["ffn_conv_b"], "final_norm_w": gs["final_norm_w"][0],
        "b_conv_w": gs["b_conv_w"][None], "b_conv_b": gs["b_conv_b"], "b_norm_w": gs["b_norm_w"], "ffn_conv_w": gs["ffn_conv_w"],
    }
    for n in _SMALL_SPLIT:
        width = w[n].shape[-1]
        small_grads[n] = lax.dynamic_slice_in_dim(small_grads[n], chip * width, width, axis=small_grads[n].ndim - 1)
    grads.update(small_grads)

    delta, new_m, new_v = {}, {}, {}
    for n in _BIG:
        shp = w[n].shape
        two_d = (shp[0] * shp[1], shp[2])
        d_, m_, v_ = _adam(f"adam_{n}", w[n].reshape(two_d), grads[n].reshape(two_d), mom[n].reshape(two_d), var[n].reshape(two_d))
        delta[n], new_m[n], new_v[n] = d_.reshape(shp), m_.reshape(shp), v_.reshape(shp)
    s_shapes = [w[n].shape for n in _SMALL]
    packs = [_pack([src[n] for n in _SMALL], 128, 8, F32) for src in (w, grads, mom, var)]
    outs = _adam("adam_small", *[pk[0] for pk in packs])
    for res, dst in zip(outs, (delta, new_m, new_v)):
        dst.update(dict(zip(_SMALL, _unpack(res, s_shapes, packs[0][1]))))

    return (loss, dx[None], *[grads[n] for n in _ORDER], *[delta[n] for n in _ORDER],
            *[new_m[n] for n in _ORDER], *[new_v[n] for n in _ORDER])
```

```python
import functools

import jax
import jax.numpy as jnp
from jax import lax
from jax.experimental import pallas as pl
from jax.experimental.pallas import tpu as pltpu

F32, BF = jnp.float32, jnp.bfloat16
HI = lax.Precision.HIGHEST

D = 1024
EPS = 1e-6
HG_HEADS, HG_HD, HG_CHUNK, HG_SUB = 8, 128, 64, 16
HG_HB = 4
SSM_GB = 2
D_INNER, SSM_HEADS, SSM_HD, SSM_GROUPS, SSM_HPG, SSM_N, SSD_CHUNK = 2048, 32, 64, 8, 4, 128, 128
CONV_DIM = D_INNER + 2 * SSM_GROUPS * SSM_N
B_PROJ = 2 * D_INNER + 2 * SSM_GROUPS * SSM_N + 2 * SSM_HEADS
B_PROJ_PAD = 6272
D_FF = 2816
NEG = -1e30
ROWS = 256
VMEM_LIMIT = 56 * 1024 * 1024

ADAM_LR, ADAM_B1, ADAM_B2, ADAM_EPS, ADAM_WD, ADAM_STEP = 0.001, 0.9, 0.999, 1e-08, 0.01, 10

MESH = pl.DeviceIdType.MESH


def _pick(n, cands):
    for c in cands:
        if n % c == 0:
            return c
    return n


class _Side:
    def __init__(self, ins, outs, plan, n_remote, n_local, alias=None):
        self.ins, self.outs, self.plan, self.n_remote, self.n_local = list(ins), list(outs), plan, n_remote, n_local
        self.alias = alias or {}

    def copies(self, in_refs, out_refs, send_sems, recv_sems, local_sems):
        remote, local = self.plan(in_refs, out_refs)
        cps = [pltpu.make_async_copy(s, d, local_sems.at[i]) for i, (s, d) in enumerate(local)]
        cps += [pltpu.make_async_remote_copy(src_ref=s, dst_ref=d, send_sem=send_sems.at[i], recv_sem=recv_sems.at[i],
                                             device_id=dev, device_id_type=MESH)
                for i, (s, d, dev) in enumerate(remote)]
        return cps

    def sems(self):
        return [pltpu.SemaphoreType.DMA((self.n_remote,)), pltpu.SemaphoreType.DMA((self.n_remote,)),
                pltpu.SemaphoreType.DMA((max(self.n_local, 1),))]


def _vcall(name, fn, grid, ins, outs, acc=None, scratch=(), place=None, side=None):
    acc = acc or {}
    n_in, n_out, nd = len(ins), len(outs), len(grid)
    n_pre = 0 if place is None else 1
    n_sin = len(side.ins) if side else 0
    n_sout = len(side.outs) if side else 0
    n_scr = len(scratch)

    def body(*refs):
        refs = refs[n_pre:]
        in_refs, refs = refs[:n_in], refs[n_in:]
        sin_refs, refs = refs[:n_sin], refs[n_sin:]
        out_refs, refs = refs[:n_out], refs[n_out:]
        sout_refs, refs = refs[:n_sout], refs[n_sout:]
        scr, sems = refs[:n_scr], refs[n_scr:]
        if side:
            at_first, at_last = None, None
            for ax in range(nd):
                f, l = pl.program_id(ax) == 0, pl.program_id(ax) == grid[ax] - 1
                at_first = f if at_first is None else jnp.logical_and(at_first, f)
                at_last = l if at_last is None else jnp.logical_and(at_last, l)

            @pl.when(at_first)
            def _():
                for cp in side.copies(sin_refs, sout_refs, *sems):
                    cp.start()

        res = fn(*[r[...] for r in in_refs], *scr)
        if not isinstance(res, (tuple, list)):
            res = (res,)
        if side:
            @pl.when(at_last)
            def _():
                for cp in side.copies(sin_refs, sout_refs, *sems):
                    cp.wait()
        for j, (o_ref, r) in enumerate(zip(out_refs, res)):
            mode = acc.get(j)
            if mode is None:
                o_ref[...] = r.astype(o_ref.dtype)
                continue
            first = pl.program_id(nd - 1) == 0
            if mode == "all":
                for ax in range(nd - 1):
                    first = jnp.logical_and(first, pl.program_id(ax) == 0)

            @pl.when(first)
            def _():
                o_ref[...] = r.astype(o_ref.dtype)

            @pl.when(jnp.logical_not(first))
            def _():
                o_ref[...] += r.astype(o_ref.dtype)

    hbm = pl.BlockSpec(memory_space=pl.ANY)
    in_specs = [pl.BlockSpec(bs, im) for _, bs, im in ins] + [hbm] * n_sin
    out_specs = [pl.BlockSpec(bs, im) for _, _, bs, im in outs] + [hbm] * n_sout
    params = pltpu.CompilerParams(dimension_semantics=("arbitrary",) * nd, vmem_limit_bytes=VMEM_LIMIT)
    out_shape = [jax.ShapeDtypeStruct(s, dt) for s, dt, _, _ in outs]
    operands = [a for a, _, _ in ins]
    scratch = list(scratch)
    aliases = {}
    if side:
        out_shape += [jax.ShapeDtypeStruct(s, dt) for s, dt in side.outs]
        operands += side.ins
        scratch += side.sems()
        aliases = {n_pre + n_in + i: n_out + o for i, o in side.alias.items()}
    if place is None:
        out = pl.pallas_call(body, name=name, grid=grid, in_specs=in_specs, out_specs=out_specs, out_shape=out_shape,
                             scratch_shapes=scratch, compiler_params=params, input_output_aliases=aliases)(*operands)
    else:
        spec = pltpu.PrefetchScalarGridSpec(num_scalar_prefetch=1, grid=grid, in_specs=in_specs, out_specs=out_specs,
                                            scratch_shapes=scratch)
        out = pl.pallas_call(body, name=name, grid_spec=spec, out_shape=out_shape, compiler_params=params,
                             input_output_aliases=aliases)(place, *operands)
    if side:
        return tuple(out[:n_out]), tuple(out[n_out:])
    return out[0] if n_out == 1 else out


def _mm(name, a, b, kind, out_dtype=F32, add=None):
    if kind == "tn":
        m, k = a.shape
        _, n = b.shape
        tm = _pick(m, (1024, 512, 256))
        tk = _pick(k, (1024, 1408, 896, 512, 256, 128))
        tn = _pick(n, (1024, 1408, 896, 512, 256, 128))

        def fn(av, bv):
            return lax.dot_general(av.astype(BF), bv.astype(BF), (((0,), (0,)), ((), ())),
                                   preferred_element_type=F32)

        return _vcall(name, fn, (k // tk, n // tn, m // tm),
                      [(a, (tm, tk), lambda i, j, s: (s, i)), (b, (tm, tn), lambda i, j, s: (s, j))],
                      [((k, n), F32, (tk, tn), lambda i, j, s: (i, j))], acc={0: "last"})
    m, k = a.shape
    n = b.shape[1] if kind == "nn" else b.shape[0]
    long_k = k > 4096
    tm = _pick(m, (512, 256)) if long_k else _pick(m, (1024, 512, 256))
    tn = _pick(n, (512, 896, 256, 128)) if long_k else _pick(n, (1024, 1408, 896, 512, 256, 128))
    dims =(((1,), (0,)), ((), ())) if kind == "nn" else (((1,), (1,)), ((), ()))

    def fn(av, bv, *rest):
        r = lax.dot_general(av.astype(BF), bv.astype(BF), dims, preferred_element_type=F32)
        return r + rest[0] if rest else r

    ins = [(a, (tm, k), lambda i, j: (i, 0)),
           (b, (k, tn), lambda i, j: (0, j)) if kind == "nn" else (b, (tn, k), lambda i, j: (j, 0))]
    if add is not None:
        ins.append((add, (tm, tn), lambda i, j: (i, j)))
    return _vcall(name, fn, (m // tm, n // tn), ins, [((m, n), out_dtype, (tm, tn), lambda i, j: (i, j))])


def _rms(h, w):
    return h * lax.rsqrt(jnp.mean(h * h, axis=-1, keepdims=True) + EPS) * w


def _rms_fwd(name, h, w):
    L = h.shape[0]
    tb = _pick(L, (ROWS,))
    return _vcall(name, _rms, (L // tb,),
                  [(h, (tb, D), lambda i: (i, 0)), (w.reshape(1, D), (1, D), lambda i: (0, 0))],
                  [((L, D), BF, (tb, D), lambda i: (i, 0))])


def _rms_bwd(name, du, h, w, dh_next):
    L = h.shape[0]
    tb = _pick(L, (ROWS,))

    def fn(duv, hv, wv, dnv):
        _, vjp = jax.vjp(_rms, hv, wv)
        dh, dw = vjp(duv)
        return dh + dnv, dw

    row = lambda i: (i, 0)
    return _vcall(name, fn, (L // tb,),
                  [(du, (tb, D), row), (h, (tb, D), row), (w.reshape(1, D), (1, D), lambda i: (0, 0)),
                   (dh_next, (tb, D), row)],
                  [((L, D), F32, (tb, D), row), ((1, D), F32, (1, D), lambda i: (0, 0))], acc={1: "all"})


def _loss_head(name, h, tgt, w):
    L = h.shape[0]
    tb = _pick(L, (ROWS,))

    def lossf(hv, wv, tv):
        err = _rms(hv, wv) - tv
        return 0.5 * jnp.sum(err * err) * (1.0 / D)

    def fn(hv, wv, tv):
        val, vjp = jax.vjp(lambda a, b: lossf(a, b, tv), hv, wv)
        dh, dw = vjp(jnp.ones((), F32))
        return jnp.full((1, 128), val, F32), dh, dw

    row = lambda i: (i, 0)
    zero = lambda i: (0, 0)
    return _vcall(name, fn, (L // tb,),
                  [(h, (tb, D), row), (w.reshape(1, D), (1, D), zero), (tgt, (tb, D), row)],
                  [((1, 128), F32, (1, 128), zero), ((L, D), F32, (tb, D), row), ((1, D), F32, (1, D), zero)],
                  acc={0: "all", 2: "all"})


def _bf(x):
    return x.astype(BF)


def _dot(a, b, dims, precision=None):
    return lax.dot_general(a, b, (dims, ((), ())), preferred_element_type=F32, precision=precision)


def _tri(n, reverse):
    r = lax.broadcasted_iota(jnp.int32, (n, n), 0)
    c = lax.broadcasted_iota(jnp.int32, (n, n), 1)
    return (r <= c) if reverse else (r >= c)


def _gla_chunk(q_raw, f_raw, v, lb3, S, reverse):
    C, SB = HG_CHUNK, HG_SUB
    row3 = lax.broadcasted_iota(jnp.int32, (3, HG_HD), 0)
    e = jnp.exp(lb3 - jnp.max(lb3, axis=0, keepdims=True))
    lb = jnp.sum(jnp.where(row3 == 0, e, 0.0), axis=0, keepdims=True) / jnp.sum(e, axis=0, keepdims=True)
    q = q_raw * jax.nn.sigmoid(q_raw)
    f = lb + (1.0 - lb) * jax.nn.sigmoid(f_raw)
    g = jnp.log(f)
    k = 1.0 - f
    b = _dot(_tri(C, reverse).astype(F32), g, ((1,), (0,)), precision=HI)
    row = lax.broadcasted_iota(jnp.int32, (C, 1), 0)

    def rowof(x, t):
        return jnp.sum(jnp.where(row == t, x, 0.0), axis=0, keepdims=True)

    o = _dot(_bf(q * jnp.exp(b)), _bf(S), ((1,), (0,)))
    att = None
    for i in range(C // SB):
        lo = SB * i
        if (not reverse and i == 0) or (reverse and i == C // SB - 1):
            continue
        first = lo + SB - 1 if reverse else lo
        r = rowof(b, first) - rowof(g, first)
        in_blk = jnp.logical_and(row >= lo, row < lo + SB)
        before = (row >= lo + SB) if reverse else (row < lo)
        qi = q * jnp.exp(jnp.where(in_blk, b - r, NEG))
        kk = k * jnp.exp(jnp.where(before, r - b, NEG))
        a_i = _dot(_bf(qi), _bf(kk), ((1,), (1,)))
        att = a_i if att is None else att + a_i
    o = o + _dot(_bf(att), _bf(v), ((1,), (0,)))
    t_i = lax.broadcasted_iota(jnp.int32, (SB, SB, HG_HD), 0)
    s_i = lax.broadcasted_iota(jnp.int32, (SB, SB, HG_HD), 1)
    pair = (t_i <= s_i) if reverse else (t_i >= s_i)
    diag = []
    for i in range(C // SB):
        lo = SB * i
        qb, kb, bb, vb = q[lo:lo + SB], k[lo:lo + SB], b[lo:lo + SB], v[lo:lo + SB]
        shp = (SB, SB, HG_HD)
        dif = lax.broadcast_in_dim(bb, shp, (0, 2)) - lax.broadcast_in_dim(bb, shp, (1, 2))
        w = lax.broadcast_in_dim(qb, shp, (0, 2)) * jnp.exp(jnp.where(pair, dif, NEG)) * lax.broadcast_in_dim(kb, shp, (1, 2))
        diag.append(_dot(_bf(jnp.sum(w, axis=2)), _bf(vb), ((1,), (0,))))
    o = o + jnp.concatenate(diag, axis=0)
    btot = rowof(b, 0 if reverse else C - 1)
    eye = lax.broadcasted_iota(jnp.int32, (HG_HD, HG_HD), 0) == lax.broadcasted_iota(jnp.int32, (HG_HD, HG_HD), 1)
    btot_col = jnp.sum(jnp.where(eye, btot, 0.0), axis=1, keepdims=True)
    s_new = jnp.exp(btot_col) * S + _dot(_bf(k * jnp.exp(btot - b)), _bf(v), ((0,), (0,)))
    return o, s_new


def _gla_fwd(name, pa, lbl, reverse, side=None):
    L = pa.shape[0]
    C = HG_CHUNK
    nc = L // C
    cidx = (lambda i: nc - 1 - i) if reverse else (lambda i: i)
    sec = 2 if reverse else 1
    hb_n, nh = HG_HB, HG_HEADS // HG_HB

    def fn(qr, fr, v, lb3, s_ref):
        @pl.when(pl.program_id(1) == 0)
        def _():
            s_ref[...] = jnp.zeros_like(s_ref)

        outs, olds = [], []
        for hb in range(hb_n):
            sl = slice(HG_HD * hb, HG_HD * (hb + 1))
            s = s_ref[hb]
            o, s_new = _gla_chunk(qr[:, sl], fr[:, sl], v[:, sl], lb3[:, sl], s, reverse)
            s_ref[hb] = s_new
            outs.append(o)
            olds.append(s)
        return jnp.concatenate(outs, axis=1), jnp.stack(olds)[None]

    blk = (C, HG_HD * hb_n)
    return _vcall(name, fn, (nh, nc),
                  [(pa, blk, lambda h, i: (cidx(i), h)), (pa, blk, lambda h, i: (cidx(i), sec * nh + h)),
                   (pa, blk, lambda h, i: (cidx(i), 3 * nh + h)), (lbl, (3, HG_HD * hb_n), lambda h, i: (0, h))],
                  [((L, D), F32, blk, lambda h, i: (cidx(i), h)),
                   ((nc, HG_HEADS, HG_HD, HG_HD), F32, (1, hb_n, HG_HD, HG_HD), lambda h, i: (cidx(i), h, 0, 0))],
                  scratch=[pltpu.VMEM((hb_n, HG_HD, HG_HD), F32)], side=side)


def _gla_bwd(name, pa, lbl, s_in, do, reverse, prev=None, side=None):
    L = pa.shape[0]
    C = HG_CHUNK
    nc = L // C
    cidx = (lambda i: i) if reverse else (lambda i: nc - 1 - i)
    sec = 2 if reverse else 1
    n_prev = 0 if prev is None else 2
    hb_n, nh = HG_HB, HG_HEADS // HG_HB

    def fn(qr, fr, v, lb3, s, dov, *rest):
        ds_ref = rest[n_prev]

        @pl.when(pl.program_id(1) == 0)
        def _():
            ds_ref[...] = jnp.zeros_like(ds_ref)

        res = []
        for hb in range(hb_n):
            sl = slice(HG_HD * hb, HG_HD * (hb + 1))
            _, vjp = jax.vjp(lambda *a: _gla_chunk(*a, reverse), qr[:, sl], fr[:, sl], v[:, sl], lb3[:, sl], s[0, hb])
            dq, df, dv, dlb, ds = vjp((dov[:, sl], ds_ref[hb]))
            ds_ref[hb] = ds
            if n_prev:
                dq, dv = dq + rest[0][:, sl], dv + rest[1][:, sl]
            res.append((dq, df, dv, dlb))
        return tuple(jnp.concatenate([r[j] for r in res], axis=1) for j in range(4))

    blk = (C, HG_HD * hb_n)
    at = lambda h, i: (cidx(i), h)
    ins = [(pa, blk, at), (pa, blk, lambda h, i: (cidx(i), sec * nh + h)), (pa, blk, lambda h, i: (cidx(i), 3 * nh + h)),
           (lbl, (3, HG_HD * hb_n), lambda h, i: (0, h)),
           (s_in, (1, hb_n, HG_HD, HG_HD), lambda h, i: (cidx(i), h, 0, 0)), (do, blk, at)]
    if prev is not None:
        ins += [(prev[0], blk, at), (prev[1], blk, at)]
    sum_dt = F32 if prev is None else BF
    return _vcall(name, fn, (nh, nc), ins,
                  [((L, D), sum_dt, blk, at), ((L, D), BF, blk, at), ((L, D), sum_dt, blk, at),
                   ((3, D), F32, (3, HG_HD * hb_n), lambda h, i: (0, h))],
                  acc={3: "last"}, scratch=[pltpu.VMEM((hb_n, HG_HD, HG_HD), F32)], side=side)


def _hgout(o_f, o_b, g, nw):
    o = o_f + o_b
    return _rms(o, nw) * (g * jax.nn.sigmoid(g))


def _hgout_fwd(name, o_f, o_b, pa, nw, side=None):
    L = o_f.shape[0]
    tb = _pick(L, (ROWS,))
    blk = (tb, HG_HD)
    at = lambda h, i: (i, h)
    return _vcall(name, _hgout, (HG_HEADS, L // tb),
                  [(o_f, blk, at), (o_b, blk, at), (pa, blk, lambda h, i: (i, 32 + h)),
                   (nw.reshape(1, HG_HD), (1, HG_HD), lambda h, i: (0, 0))],
                  [((L, D), BF, blk, at)], side=side)


def _hgout_bwd(name, o_f, o_b, pa, nw, dy):
    L = o_f.shape[0]
    tb = _pick(L, (ROWS,))

    def fn(ofv, obv, gv, nwv, dyv):
        _, vjp = jax.vjp(_hgout, ofv, obv, gv, nwv)
        do, _, dg, dnw = vjp(dyv)
        return do, dg, dnw

    blk = (tb, HG_HD)
    at = lambda h, i: (i, h)
    zero = lambda h, i: (0, 0)
    return _vcall(name, fn, (HG_HEADS, L // tb),
                  [(o_f, blk, at), (o_b, blk, at), (pa, blk, lambda h, i: (i, 32 + h)),
                   (nw.reshape(1, HG_HD), (1, HG_HD), zero), (dy, blk, at)],
                  [((L, D), F32, blk, at), ((L, D), BF, blk, at), ((1, HG_HD), F32, (1, HG_HD), zero)],
                  acc={2: "all"})


def _shift(x, s):
    if s == 0:
        return x
    n = x.shape[0]
    t = lax.broadcasted_iota(jnp.int32, (n, 1), 0)
    if s > 0:
        return jnp.where(t >= s, pltpu.roll(x, s, 0), 0.0)
    return jnp.where(t < n + s, pltpu.roll(x, n + s, 0), 0.0)


def _conv(x, w, b):
    kk = w.shape[0]
    p = (kk - 1) // 2
    y = b
    for j in range(kk):
        y = y + w[j:j + 1] * _shift(x, p - j)
    return y


def _conv_bwd(x, w, dc):
    kk = w.shape[0]
    p = (kk - 1) // 2
    dx = None
    dws = []
    for j in range(kk):
        t = w[j:j + 1] * _shift(dc, j - p)
        dx = t if dx is None else dx + t
        dws.append(jnp.sum(dc * _shift(x, p - j), axis=0, keepdims=True))
    rows = lax.broadcasted_iota(jnp.int32, (kk, 1), 0)
    dw = None
    for j in range(kk):
        t = jnp.where(rows == j, dws[j], 0.0)
        dw = t if dw is None else dw + t
    return dx, dw, jnp.sum(dc, axis=0, keepdims=True)


def _silu_grad(c):
    s = jax.nn.sigmoid(c)
    return s * (1.0 + c * (1.0 - s))


def _glu_fwd(name, pf, cw, cb):
    L = pf.shape[0]
    tc = 128
    nt = D_FF // tc

    def fn(gate, val, w, b):
        c = _conv(gate, w, b)
        return c * jax.nn.sigmoid(c) * val

    return _vcall(name, fn, (nt,),
                  [(pf, (L, tc), lambda j: (0, j)), (pf, (L, tc), lambda j: (0, nt + j)),
                   (cw, (3, tc), lambda j: (0, j)), (cb.reshape(1, D_FF), (1, tc), lambda j: (0, j))],
                  [((L, D_FF), BF, (L, tc), lambda j: (0, j))])


def _glu_bwd(name, pf, cw, cb, dy):
    L = pf.shape[0]
    tc = 128
    nt = D_FF // tc

    def fn(gate, val, w, b, dyv):
        c = _conv(gate, w, b)
        sc = c * jax.nn.sigmoid(c)
        dc = dyv * val * _silu_grad(c)
        dgate, dw, db = _conv_bwd(gate, w, dc)
        return dgate, dyv * sc, dw, db

    col = lambda j: (0, j)
    return _vcall(name, fn, (nt,),
                  [(pf, (L, tc), col), (pf, (L, tc), lambda j: (0, nt + j)), (cw, (3, tc), col),
                   (cb.reshape(1, D_FF), (1, tc), col), (dy, (L, tc), col)],
                  [((L, D_FF), BF, (L, tc), col), ((L, D_FF), BF, (L, tc), col),
                   ((3, D_FF), F32, (3, tc), col), ((1, D_FF), F32, (1, tc), col)])


def _perm_tile(j):
    return jnp.where(j < 16, 4 * (j // 2) + j % 2, jnp.where(j < 24, 4 * (j - 16) + 2, 4 * (j - 24) + 3))


def _mpre_fwd(name, pb, cw, cb):
    L = pb.shape[0]
    tc = 128

    def fn(x, w, b):
        c = _conv(x, w, b)
        return c * jax.nn.sigmoid(c)

    return _vcall(name, fn, (CONV_DIM // tc,),
                  [(pb, (L, tc), lambda j: (0, 16 + j)), (cw, (5, tc), lambda j: (0, j)),
                   (cb.reshape(1, CONV_DIM), (1, tc), lambda j: (0, j))],
                  [((L, CONV_DIM), F32, (L, tc), lambda j: (0, _perm_tile(j)))])


def _mpre_bwd(name, pb, cw, cb, dact):
    L = pb.shape[0]
    tc = 128

    def fn(x, w, b, da):
        c = _conv(x, w, b)
        return _conv_bwd(x, w, da * _silu_grad(c))

    col = lambda j: (0, j)
    return _vcall(name, fn, (CONV_DIM // tc,),
                  [(pb, (L, tc), lambda j: (0, 16 + j)), (cw, (5, tc), col), (cb.reshape(1, CONV_DIM), (1, tc), col),
                   (dact, (L, tc), lambda j: (0, _perm_tile(j)))],
                  [((L, CONV_DIM), BF, (L, tc), col), ((5, CONV_DIM), F32, (5, tc), col),
                   ((1, CONV_DIM), F32, (1, tc), col)])


def _softplus(x):
    return jnp.maximum(x, 0.0) + jnp.log(1.0 + jnp.exp(-jnp.abs(x)))


def _dt_fwd(name, pb, dtb, alog):
    L = pb.shape[0]
    tb = _pick(L, (1024, ROWS))

    def fn(x, bias, al):
        dt = _softplus(x + bias)
        return dt, dt * (-jnp.exp(al))

    row = lambda i: (i, 0)
    zero = lambda i: (0, 0)
    return _vcall(name, fn, (L // tb,),
                  [(pb, (tb, 128), lambda i: (i, 48)), (dtb, (1, 128), zero), (alog, (1, 128), zero)],
                  [((L, 128), F32, (tb, 128), row), ((L, 128), F32, (tb, 128), row)])


def _dt_bwd(name, pb, dtb, alog, ddt_f, dla_f, ddt_b, dla_b):
    L = pb.shape[0]
    tb = _pick(L, (1024, ROWS))

    def fn(x, bias, al, a1, b1, a2, b2):
        ddt = jnp.sum(a1, axis=0) + jnp.sum(a2, axis=0)
        dla = jnp.sum(b1, axis=0) + jnp.sum(b2, axis=0)
        z = x + bias
        dt = _softplus(z)
        a = -jnp.exp(al)
        dz = (ddt + dla * a) * jax.nn.sigmoid(z)
        return dz, jnp.sum(dz, axis=0, keepdims=True), jnp.sum(dla * dt, axis=0, keepdims=True) * a

    zero = lambda i: (0, 0)
    g3 = (SSM_GROUPS, tb, 128)
    at3 = lambda i: (0, i, 0)
    return _vcall(name, fn, (L // tb,),
                  [(pb, (tb, 128), lambda i: (i, 48)), (dtb, (1, 128), zero), (alog, (1, 128), zero),
                   (ddt_f, g3, at3), (dla_f, g3, at3), (ddt_b, g3, at3), (dla_b, g3, at3)],
                  [((L, 128), BF, (tb, 128), lambda i: (i, 0)), ((1, 128), F32, (1, 128), zero),
                   ((1, 128), F32, (1, 128), zero)], acc={1: "all", 2: "all"})


def _ssd_chunk(xa, dt, la, hs, head0, reverse):
    C = SSD_CHUNK
    P4 = SSM_HPG * SSM_HD
    xs, bm, cm = xa[:, :P4], xa[:, P4:P4 + SSM_N], xa[:, P4 + SSM_N:]
    lane = lax.broadcasted_iota(jnp.int32, (1, 128), 1)
    col_head = lax.broadcasted_iota(jnp.int32, (1, P4), 1) // SSM_HD
    row_head = lax.broadcasted_iota(jnp.int32, (P4, 1), 0) // SSM_HD
    row = lax.broadcasted_iota(jnp.int32, (C, 1), 0)
    eye = lax.broadcasted_iota(jnp.int32, (C, C), 0) == lax.broadcasted_iota(jnp.int32, (C, C), 1)
    tri = _tri(C, reverse)
    acum = _dot(tri.astype(F32), la, ((1,), (0,)), precision=HI)
    cb = _dot(_bf(cm), _bf(bm), ((1,), (1,)))
    last = 0 if reverse else C - 1
    dt_x, ea_x, dec_x, y = 0.0, 0.0, 0.0, 0.0
    atot_rows = 0.0
    lmats = []
    for j in range(SSM_HPG):
        sel = lane == head0 + j
        dt_j = jnp.sum(jnp.where(sel, dt, 0.0), axis=1, keepdims=True)
        ac_j = jnp.sum(jnp.where(sel, acum, 0.0), axis=1, keepdims=True)
        ac_row = jnp.sum(jnp.where(eye, ac_j, 0.0), axis=0, keepdims=True)
        atot = jnp.sum(jnp.where(row == last, ac_j, 0.0), axis=0, keepdims=True)
        lmats.append(jnp.exp(jnp.where(tri, ac_j - ac_row, NEG)))
        mine = col_head == j
        dt_x = dt_x + jnp.where(mine, dt_j, 0.0)
        ea_x = ea_x + jnp.where(mine, jnp.exp(ac_j), 0.0)
        dec_x = dec_x + jnp.where(mine, jnp.exp(atot - ac_j), 0.0)
        atot_rows = atot_rows + jnp.where(row_head == j, jnp.exp(atot), 0.0)
    xd = xs * dt_x
    xdb = _bf(xd)
    for j in range(SSM_HPG):
        y = y + jnp.where(col_head == j, _dot(_bf(cb * lmats[j]), xdb, ((1,), (0,))), 0.0)
    y = y + _dot(_bf(cm), _bf(hs), ((1,), (1,))) * ea_x
    hs_new = atot_rows * hs + _dot(_bf(xd * dec_x), _bf(bm), ((0,), (0,)))
    return y, hs_new


def _ssd_fwd(name, xact, dt, la, reverse):
    L = xact.shape[0]
    C = SSD_CHUNK
    nc = L // C
    cidx = (lambda i: nc - 1 - i) if reverse else (lambda i: i)
    base = SSM_HEADS if reverse else 0
    P4 = SSM_HPG * SSM_HD

    gb_n = SSM_GB

    def fn(xa, dtv, lav, h_ref):
        @pl.when(pl.program_id(1) == 0)
        def _():
            h_ref[...] = jnp.zeros_like(h_ref)

        ys, olds = [], []
        for gb in range(gb_n):
            hs = h_ref[gb]
            head0 = base + SSM_HPG * (gb_n * pl.program_id(0) + gb)
            y, hs_new = _ssd_chunk(xa[:, 512 * gb:512 * (gb + 1)], dtv, lav, hs, head0, reverse)
            h_ref[gb] = hs_new
            ys.append(y)
            olds.append(hs)
        return jnp.concatenate(ys, axis=1), jnp.stack(olds)[None]

    return _vcall(name, fn, (SSM_GROUPS // gb_n, nc),
                  [(xact, (C, 512 * gb_n), lambda g, i: (cidx(i), g)), (dt, (C, 128), lambda g, i: (cidx(i), 0)),
                   (la, (C, 128), lambda g, i: (cidx(i), 0))],
                  [((L, D_INNER), F32, (C, P4 * gb_n), lambda g, i: (cidx(i), g)),
                   ((nc, SSM_GROUPS, P4, SSM_N), F32, (1, gb_n, P4, SSM_N), lambda g, i: (cidx(i), g, 0, 0))],
                  scratch=[pltpu.VMEM((gb_n, P4, SSM_N), F32)])


def _ssd_bwd(name, xact, dt, la, h_in, dy, reverse, prev_xs=None, prev_all=None):
    L = xact.shape[0]
    C = SSD_CHUNK
    nc = L // C
    cidx = (lambda i: i) if reverse else (lambda i: nc - 1 - i)
    base = SSM_HEADS if reverse else 0
    P4 = SSM_HPG * SSM_HD

    gb_n = SSM_GB

    def fn(xa, dtv, lav, hs, dyv, pv, dh_ref):
        @pl.when(pl.program_id(1) == 0)
        def _():
            dh_ref[...] = jnp.zeros_like(dh_ref)

        res = []
        for gb in range(gb_n):
            head0 = base + SSM_HPG * (gb_n * pl.program_id(0) + gb)
            _, vjp = jax.vjp(lambda a, b, c, d: _ssd_chunk(a, b, c, d, head0, reverse),
                             xa[:, 512 * gb:512 * (gb + 1)], dtv, lav, hs[0, gb])
            dxa, ddt, dla, dh = vjp((dyv[:, P4 * gb:P4 * (gb + 1)], dh_ref[gb]))
            dh_ref[gb] = dh
            if prev_all is not None:
                dxa = dxa + pv[:, 512 * gb:512 * (gb + 1)]
            else:
                dxa = dxa + jnp.concatenate([pv[:, P4 * gb:P4 * (gb + 1)], jnp.zeros((C, 2 * SSM_N), F32)], axis=1)
            res.append((dxa, ddt, dla))
        return (jnp.concatenate([r[0] for r in res], axis=1), jnp.stack([r[1] for r in res]),
                jnp.stack([r[2] for r in res]))

    at = lambda g, i: (cidx(i), g)
    at0 = lambda g, i: (cidx(i), 0)
    pv = (prev_all, (C, 512 * gb_n), at) if prev_all is not None else (prev_xs, (C, P4 * gb_n), at)
    return _vcall(name, fn, (SSM_GROUPS // gb_n, nc),
                  [(xact, (C, 512 * gb_n), at), (dt, (C, 128), at0), (la, (C, 128), at0),
                   (h_in, (1, gb_n, P4, SSM_N), lambda g, i: (cidx(i), g, 0, 0)), (dy, (C, P4 * gb_n), at), pv],
                  [((L, CONV_DIM), F32, (C, 512 * gb_n), at),
                   ((SSM_GROUPS, L, 128), F32, (gb_n, C, 128), lambda g, i: (g, cidx(i), 0)),
                   ((SSM_GROUPS, L, 128), F32, (gb_n, C, 128), lambda g, i: (g, cidx(i), 0))],
                  scratch=[pltpu.VMEM((gb_n, P4, SSM_N), F32)])


def _mpost(y_f, y_b, xs, z, dsk, nw):
    y = (y_f + y_b + xs * dsk) * (z * jax.nn.sigmoid(z))
    return _rms(y, nw)


def _mpost_fwd(name, y_f, y_b, xact, pb, dsk, nw):
    L = y_f.shape[0]
    tb = _pick(L, (ROWS,))
    blk = (tb, 256)
    at = lambda g, i: (i, g)
    par = lambda g, i: (0, g)
    return _vcall(name, _mpost, (SSM_GROUPS, L // tb),
                  [(y_f, blk, at), (y_b, blk, at), (xact, blk, lambda g, i: (i, 2 * g)), (pb, blk, at),
                   (dsk, (1, 256), par), (nw, (1, 256), par)],
                  [((L, D_INNER), BF, blk, at)])


def _mpost_bwd(name, y_f, y_b, xact, pb, dsk, nw, dy):
    L = y_f.shape[0]
    tb = _pick(L, (ROWS,))

    def fn(yf, yb, xs, z, dskv, nwv, dyv):
        _, vjp = jax.vjp(_mpost, yf, yb, xs, z, dskv, nwv)
        dyf, _, dxs, dz, ddsk, dnw = vjp(dyv)
        return dyf, dxs, dz, ddsk, dnw

    blk = (tb, 256)
    at = lambda g, i: (i, g)
    par = lambda g, i: (0, g)
    return _vcall(name, fn, (SSM_GROUPS, L // tb),
                  [(y_f, blk, at), (y_b, blk, at), (xact, blk, lambda g, i: (i, 2 * g)), (pb, blk, at),
                   (dsk, (1, 256), par), (nw, (1, 256), par), (dy, blk, at)],
                  [((L, D_INNER), F32, blk, at), ((L, D_INNER), F32, blk, at), ((L, D_INNER), BF, blk, at),
                   ((1, D_INNER), F32, (1, 256), par), ((1, D_INNER), F32, (1, 256), par)],
                  acc={3: "last", 4: "last"})


def _ffn_fwd(tag, h, nw, w_in, cw, cb, w_out):
    u = _rms_fwd(f"{tag}_norm", h, nw)
    pf = _mm(f"{tag}_in", u, w_in, "nn")
    yf = _glu_fwd(f"{tag}_glu", pf, cw, cb)
    return _mm(f"{tag}_out", yf, w_out, "nn", add=h), (u, pf, yf)


def _ffn_bwd(tag, h, nw, w_in, cw, cb, w_out, saved, dh):
    u, pf, yf = saved
    d_w_out = _mm(f"{tag}_dwout", yf, dh, "tn")
    dyf = _mm(f"{tag}_dy", dh, w_out, "nt")
    dgate, dval, dcw, dcb = _glu_bwd(f"{tag}_dglu", pf, cw, cb, dyf)
    dpf = jnp.concatenate([dgate, dval], axis=1)
    d_w_in = _mm(f"{tag}_dwin", u, dpf, "tn")
    du = _mm(f"{tag}_du", dpf, w_in, "nt")
    dh_in, dnw = _rms_bwd(f"{tag}_dnorm", du, h, nw, dh)
    return dh_in, dnw, d_w_in, dcw, dcb, d_w_out


def _sequence_grads(x, tgt, p, sh, place):
    g = {}
    lbl = p["a_lb_logits"]
    first, mid, last = ("a_in", "a_out"), ("f0_in", "f0_out", "b_in", "b_out"), ("f1_in", "f1_out")
    W = {}
    got = _exchange("w_first", _gather_side(first, (), sh, W))
    W.update(zip(first, got))
    got = _exchange("w_first_pass", _gather_side((), first, sh, W))
    W.update(zip(first, got))
    u1 = _rms_fwd("a_norm", x, p["norm1_w"][0])
    pa = _mm("a_in", u1, W["a_in"], "nn")
    (o_f, s_f), got = _gla_fwd("a_scan_f", pa, lbl, False, side=_gather_side(mid, (), sh, W))
    W.update(zip(mid, got))
    (o_b, s_b), got = _gla_fwd("a_scan_b", pa, lbl, True, side=_gather_side(last, mid, sh, W))
    W.update(zip(last + mid, got))
    (ya,), got = _hgout_fwd("a_gate", o_f, o_b, pa, p["a_norm_w"], side=_gather_side((), last, sh, W))
    W.update(zip(last, got))
    wb4 = W["b_in"].reshape(4, D, B_PROJ // 4)
    p = dict(p, a_w_in=W["a_in"], a_w_out=W["a_out"], b_w_out=W["b_out"], ffn_w_in=(W["f0_in"], W["f1_in"]),
             ffn_w_out=(W["f0_out"], W["f1_out"]),
             b_w_in=jnp.pad(jnp.concatenate([wb4[j] for j in range(4)], axis=1), ((0, 0), (0, B_PROJ_PAD - B_PROJ))))
    h1 = _mm("a_out", ya, p["a_w_out"], "nn", add=x)
    h2, ffn0 = _ffn_fwd("f0", h1, p["norm2_w"][0], p["ffn_w_in"][0], p["ffn_conv_w"][0], p["ffn_conv_b"][0], p["ffn_w_out"][0])
    u3 = _rms_fwd("b_norm", h2, p["norm1_w"][1])
    pb = _mm("b_in", u3, p["b_w_in"], "nn")
    xact = _mpre_fwd("b_conv", pb, p["b_conv_w"], p["b_conv_b"])
    dt, la = _dt_fwd("b_dt", pb, p["b_dt_bias"], p["b_a_log"])
    y_f, hs_f = _ssd_fwd("b_scan_f", xact, dt, la, False)
    y_b, hs_b = _ssd_fwd("b_scan_b", xact, dt, la, True)
    yb = _mpost_fwd("b_gate", y_f, y_b, xact, pb, p["b_d_skip"], p["b_norm_w"])
    h3 = _mm("b_out", yb, p["b_w_out"], "nn", add=h2)
    h4, ffn1 = _ffn_fwd("f1", h3, p["norm2_w"][1], p["ffn_w_in"][1], p["ffn_conv_w"][1], p["ffn_conv_b"][1], p["ffn_w_out"][1])
    loss, dh4, g["final_norm_w"] = _loss_head("head", h4, tgt, p["final_norm_w"])
    dh3, dn2_1, dwin1, dcw1, dcb1, dwout1 = _ffn_bwd("f1", h3, p["norm2_w"][1], p["ffn_w_in"][1], p["ffn_conv_w"][1],
                                                     p["ffn_conv_b"][1], p["ffn_w_out"][1], ffn1, dh4)
    G = {"f1_in": dwin1, "f1_out": dwout1}
    G["b_out"] = _mm("b_dwout", yb, dh3, "tn")
    dyb = _mm("b_dy", dh3, p["b_w_out"], "nt")
    dys, dxs, dz, g["b_d_skip"], g["b_norm_w"] = _mpost_bwd("b_dgate", y_f, y_b, xact, pb, p["b_d_skip"], p["b_norm_w"], dyb)
    dxa1, ddt_f, dla_f = _ssd_bwd("b_dscan_f", xact, dt, la, hs_f, dys, False, prev_xs=dxs)
    dxa, ddt_b, dla_b = _ssd_bwd("b_dscan_b", xact, dt, la, hs_b, dys, True, prev_all=dxa1)
    dxbc, g["b_conv_w"], g["b_conv_b"] = _mpre_bwd("b_dconv", pb, p["b_conv_w"], p["b_conv_b"], dxa)
    ddtr, g["b_dt_bias"], g["b_a_log"] = _dt_bwd("b_ddt", pb, p["b_dt_bias"], p["b_a_log"], ddt_f, dla_f, ddt_b, dla_b)
    dpb = jnp.concatenate([dz, dxbc, ddtr], axis=1)
    G["b_in"] = _mm("b_dwin", dpb, u3, "tn")
    du3 = _mm("b_du", dpb, p["b_w_in"], "nt")
    dh2, dn1_1 = _rms_bwd("b_dnorm", du3, h2, p["norm1_w"][1], dh3)
    dh1, dn2_0, G["f0_in"], dcw0, dcb0, G["f0_out"] = _ffn_bwd("f0", h1, p["norm2_w"][0], p["ffn_w_in"][0], p["ffn_conv_w"][0],
                                                              p["ffn_conv_b"][0], p["ffn_w_out"][0], ffn0, dh2)
    late = last + mid
    chip_sums = _pair_reduce("gl", late, G, place)
    G["a_out"] = _mm("a_dwout", ya, dh1, "tn")
    dya = _mm("a_dy", dh1, p["a_w_out"], "nt")
    do, dg, g["a_norm_w"] = _hgout_bwd("a_dgate", o_f, o_b, pa, p["a_norm_w"], dya)
    (dq1, df1, dv1, dl1), got = _gla_bwd("a_dscan_f", pa, lbl, s_f, do, False, side=_chips_side(late, chip_sums))
    shards = {u: _chip_sum(f"gl_sum_{u}", _GGEO[u], chip_sums[u], r, place) for u, r in zip(late, got)}
    (dq, df2, dv, dl2), got = _gla_bwd("a_dscan_b", pa, lbl, s_b, do, True, prev=(dq1, dv1), side=_halves_side(late, shards))
    shards = dict(zip(late, got))
    dpa = jnp.concatenate([dq, df1, df2, dv, dg], axis=1)
    G["a_in"] = _mm("a_dwin", u1, dpa, "tn")
    du1 = _mm("a_du", dpa, p["a_w_in"], "nt")
    dx, dn1_0 = _rms_bwd("a_dnorm", du1, x, p["norm1_w"][0], dh1)
    chip_sums = _pair_reduce("ga", first, G, place)
    got = _exchange("ga_chips", _chips_side(first, chip_sums))
    mine = {u: _chip_sum(f"ga_sum_{u}", _GGEO[u], chip_sums[u], r, place) for u, r in zip(first, got)}
    shards.update(zip(first, _exchange("ga_halves", _halves_side(first, mine))))
    g["a_lb_logits"] = (dl1, dl2)
    g["norm1_w"] = (dn1_0, dn1_1)
    g["norm2_w"] = (dn2_0, dn2_1)
    g["ffn_conv_w"] = (dcw0, dcw1)
    g["ffn_conv_b"] = (dcb0, dcb1)
    return loss, dx, g, shards


def _here():
    return lax.axis_index("x"), lax.axis_index("y"), lax.axis_index("c")


def _allgather8(name, src, by_core=False):
    blk = src.shape[1:] if by_core else src.shape

    def body(x_ref, out_ref, send_sems, recv_sems, local_sem):
        x, y, c = _here()
        me, sibling = (x, y, c), (x, y, 1 - c)
        chips = [(1 - x, y), (x, 1 - y), (1 - x, 1 - y)]
        own = x_ref.at[c] if by_core else x_ref

        def slot(px, py, pc):
            return out_ref.at[4 * px + 2 * py + pc]

        def copy(k, block, to, from_own=False):
            return pltpu.make_async_remote_copy(
                src_ref=own if from_own else slot(*block), dst_ref=slot(*block),
                send_sem=send_sems.at[k], recv_sem=recv_sems.at[k], device_id=to, device_id_type=MESH)

        mine = pltpu.make_async_copy(own, slot(*me), local_sem)
        mine.start()
        first = [copy(0, me, sibling, from_own=True)]
        first += [copy(1 + j, me, (*chip, c), from_own=True) for j, chip in enumerate(chips)]
        for cp in first:
            cp.start()
        passed = [copy(4 + j, (*chip, c), sibling) for j, chip in enumerate(chips)]
        for j, chip in enumerate(chips):
            copy(1 + j, (*chip, c), me).wait_recv()
            passed[j].start()
        copy(0, sibling, me).wait_recv()
        for j, chip in enumerate(chips):
            copy(4 + j, (*chip, 1 - c), me).wait_recv()
        for cp in first + passed:
            cp.wait_send()
        mine.wait()

    return pl.pallas_call(
        body, name=name,
        out_shape=jax.ShapeDtypeStruct((8,) + tuple(blk), src.dtype),
        in_specs=[pl.BlockSpec(memory_space=pl.ANY)],
        out_specs=pl.BlockSpec(memory_space=pl.ANY),
        scratch_shapes=[pltpu.SemaphoreType.DMA((7,)), pltpu.SemaphoreType.DMA((7,)), pltpu.SemaphoreType.DMA],
    )(src)


def _exchange(name, side):
    n_i, n_o = len(side.ins), len(side.outs)

    def body(*refs):
        copies = side.copies(refs[:n_i], refs[n_i:n_i + n_o], *refs[n_i + n_o:])
        for cp in copies:
            cp.start()
        for cp in copies:
            cp.wait()

    return pl.pallas_call(
        body, name=name,
        out_shape=[jax.ShapeDtypeStruct(s, dt) for s, dt in side.outs],
        in_specs=[pl.BlockSpec(memory_space=pl.ANY)] * n_i,
        out_specs=[pl.BlockSpec(memory_space=pl.ANY)] * n_o,
        scratch_shapes=side.sems(),
        input_output_aliases=dict(side.alias),
    )(*side.ins)


_WGEO = {"a_in": ("col", 1024, 1280), "a_out": ("row", 256, 1024), "b_in": ("row", 1024, 1552), "b_out": ("row", 512, 1024),
         "f0_in": ("col", 1024, 1408), "f1_in": ("col", 1024, 1408), "f0_out": ("row", 704, 1024), "f1_out": ("row", 704, 1024)}
_GGEO = dict(_WGEO, b_in=("row", 1552, 1024))


def _full_shape(geo):
    kind, r, cw = geo
    return (r, 4 * cw) if kind == "col" else (4 * r, cw)


def _times(i, step):
    return i * step if isinstance(i, int) else pl.multiple_of(i * step, step & -step)


def _win(ref, geo, j, h):
    kind, r, cw = geo
    hr = r // 2
    if kind == "col":
        return ref.at[pl.ds(_times(h, hr), hr), pl.ds(_times(j, cw), cw)]
    return ref.at[pl.ds(_times(2 * j + h, hr), hr), :]


def _half(ref, geo, h):
    hr = geo[1] // 2
    return ref.at[pl.ds(_times(h, hr), hr), :]


def _gather_side(first, second, sh, full):
    n1 = len(first)

    def plan(ins, outs):
        x, y, c = _here()
        m = 2 * x + y
        remote, local = [], []
        for u, src, dst_full in zip(first, ins[:n1], outs[:n1]):
            mine, dst = _half(src, _WGEO[u], c), _win(dst_full, _WGEO[u], m, c)
            local.append((mine, dst))
            remote.append((mine, dst, (x, y, 1 - c)))
            for k in (1, 2, 3):
                t = (m + k) % 4
                remote.append((mine, dst, (t // 2, t % 2, c)))
        for u, buf in zip(second, outs[n1:]):
            for k in (1, 2, 3):
                w_ = _win(buf, _WGEO[u], (m + k) % 4, c)
                remote.append((w_, w_, (x, y, 1 - c)))
        return remote, local

    return _Side([sh[u] for u in first] + [full[u] for u in second],
                 [(_full_shape(_WGEO[u]), BF) for u in first + second], plan, 4 * n1 + 3 * len(second), n1,
                 alias={n1 + i: n1 + i for i in range(len(second))})


def _pair_reduce(tag, units, G, place):
    def plan(ins, outs):
        x, y, c = _here()
        return [(_win(gr, _GGEO[u], j, 1 - c), got.at[j], (x, y, 1 - c))
                for u, gr, got in zip(units, ins, outs) for j in range(4)], []

    halves = [(4, _GGEO[u][1] // 2, _GGEO[u][2]) for u in units]
    gots = _exchange(f"{tag}_pair", _Side([G[u] for u in units], [(s, F32) for s in halves], plan, 4 * len(units), 0))
    out = {}
    for u, got, shp in zip(units, gots, halves):
        blk = shp[1:]
        at = (lambda j, pl_: (pl_[1], j)) if _GGEO[u][0] == "col" else (lambda j, pl_: (2 * j + pl_[1], 0))
        slab = lambda j, pl_: (j, 0, 0)
        out[u] = _vcall(f"{tag}_pair_sum_{u}", lambda a, b: (a + b[0])[None], (4,),
                        [(G[u], blk, at), (got, (1,) + blk, slab)], [(shp, BF, (1,) + blk, slab)], place=place)
    return out


def _chips_side(units, chip_sums):
    def plan(ins, outs):
        x, y, c = _here()
        m = 2 * x + y
        remote = []
        for s, got in zip(ins, outs):
            for k in (1, 2, 3):
                t = (m + k) % 4
                remote.append((s.at[t], got.at[k - 1], (t // 2, t % 2, c)))
        return remote, []

    return _Side([chip_sums[u] for u in units], [((3,) + chip_sums[u].shape[1:], BF) for u in units], plan,
                 3 * len(units), 0)


def _chip_sum(name, geo, chip_sums, got, place):
    _, r, cw = geo
    blk = (r // 2, cw)
    return _vcall(name, lambda a, b: ((a[0].astype(F32) + b[0].astype(F32)) + b[1].astype(F32)) + b[2].astype(F32), (1,),
                  [(chip_sums, (1,) + blk, lambda i, pl_: (pl_[0], 0, 0)), (got, (3,) + blk, lambda i, pl_: (0, 0, 0))],
                  [((r, cw), F32, blk, lambda i, pl_: (pl_[1], 0))], place=place)


def _halves_side(units, shards):
    def plan(ins, outs):
        x, y, c = _here()
        return [(_half(o, _GGEO[u], c), _half(o, _GGEO[u], c), (x, y, 1 - c)) for u, o in zip(units, outs)], []

    return _Side([shards[u] for u in units], [(shards[u].shape, F32) for u in units], plan, len(units), 0,
                 alias={i: i for i in range(len(units))})


def _adam(name, w, g, m, v):
    rows, cols = w.shape
    tb = _pick(rows, (256, 128, 64, 8))

    def fn(wv, gv, mv, vv):
        m2 = ADAM_B1 * mv + (1.0 - ADAM_B1) * gv
        v2 = ADAM_B2 * vv + (1.0 - ADAM_B2) * jnp.square(gv)
        m_hat = m2 / (1.0 - ADAM_B1 ** ADAM_STEP)
        v_hat = v2 / (1.0 - ADAM_B2 ** ADAM_STEP)
        return -ADAM_LR * (m_hat / (jnp.sqrt(v_hat) + ADAM_EPS) + ADAM_WD * wv), m2, v2

    at = lambda i: (i, 0)
    return _vcall(name, fn, (rows // tb,), [(a, (tb, cols), at) for a in (w, g, m, v)],
                  [((rows, cols), F32, (tb, cols), at)] * 3)


def _pack(arrays, width, row_multiple, dtype):
    parts, offs, at = [], [], 0
    for a in arrays:
        flat = a.reshape(-1).astype(dtype)
        rows = -(-flat.shape[0] // (width * row_multiple)) * row_multiple
        parts.append(jnp.pad(flat, (0, rows * width - flat.shape[0])).reshape(rows, width))
        offs.append(at)
        at += rows
    return jnp.concatenate(parts, axis=0), offs


def _unpack(flat, shapes, offs):
    out = []
    for shp, at in zip(shapes, offs):
        n = 1
        for s in shp:
            n *= s
        rows = -(-n // flat.shape[1])
        out.append(flat[at:at + rows].reshape(-1)[:n].reshape(shp))
    return out


_BIG = ("a_w_in", "a_w_out", "b_w_in", "b_w_out", "ffn_w_in", "ffn_w_out")
_BIG_AXIS = {"a_w_in": 2, "a_w_out": 1, "b_w_in": 2, "b_w_out": 1, "ffn_w_in": 2, "ffn_w_out": 1}
_SMALL_SPLIT = ("b_conv_w", "b_conv_b", "b_norm_w", "ffn_conv_w")
_SMALL = ("norm1_w", "norm2_w", "a_lb_logits", "a_norm_w", "b_conv_w", "b_conv_b", "b_dt_bias", "b_a_log", "b_d_skip",
          "b_norm_w", "ffn_conv_w", "ffn_conv_b", "final_norm_w")
_ORDER = ("norm1_w", "norm2_w", "a_w_in", "a_lb_logits", "a_norm_w", "a_w_out", "b_w_in", "b_conv_w", "b_conv_b", "b_dt_bias",
          "b_a_log", "b_d_skip", "b_norm_w", "b_w_out", "ffn_w_in", "ffn_conv_w", "ffn_conv_b", "ffn_w_out", "final_norm_w")


def kernel(x, norm1_w, norm2_w, a_w_in, a_lb_logits, a_norm_w, a_w_out, b_w_in, b_conv_w, b_conv_b, b_dt_bias, b_a_log, b_d_skip, b_norm_w, b_w_out, ffn_w_in, ffn_conv_w, ffn_conv_b, ffn_w_out, final_norm_w, loss_target, m_norm1_w, m_norm2_w, m_a_w_in, m_a_lb_logits, m_a_norm_w, m_a_w_out, m_b_w_in, m_b_conv_w, m_b_conv_b, m_b_dt_bias, m_b_a_log, m_b_d_skip, m_b_norm_w, m_b_w_out, m_ffn_w_in, m_ffn_conv_w, m_ffn_conv_b, m_ffn_w_out, m_final_norm_w, v_norm1_w, v_norm2_w, v_a_w_in, v_a_lb_logits, v_a_norm_w, v_a_w_out, v_b_w_in, v_b_conv_w, v_b_conv_b, v_b_dt_bias, v_b_a_log, v_b_d_skip, v_b_norm_w, v_b_w_out, v_ffn_w_in, v_ffn_conv_w, v_ffn_conv_b, v_ffn_w_out, v_final_norm_w):
    w = dict(norm1_w=norm1_w, norm2_w=norm2_w, a_w_in=a_w_in, a_lb_logits=a_lb_logits, a_norm_w=a_norm_w, a_w_out=a_w_out,
             b_w_in=b_w_in, b_conv_w=b_conv_w, b_conv_b=b_conv_b, b_dt_bias=b_dt_bias, b_a_log=b_a_log, b_d_skip=b_d_skip,
             b_norm_w=b_norm_w, b_w_out=b_w_out, ffn_w_in=ffn_w_in, ffn_conv_w=ffn_conv_w, ffn_conv_b=ffn_conv_b,
             ffn_w_out=ffn_w_out, final_norm_w=final_norm_w)
    mom = dict(norm1_w=m_norm1_w, norm2_w=m_norm2_w, a_w_in=m_a_w_in, a_lb_logits=m_a_lb_logits, a_norm_w=m_a_norm_w,
               a_w_out=m_a_w_out, b_w_in=m_b_w_in, b_conv_w=m_b_conv_w, b_conv_b=m_b_conv_b, b_dt_bias=m_b_dt_bias,
               b_a_log=m_b_a_log, b_d_skip=m_b_d_skip, b_norm_w=m_b_norm_w, b_w_out=m_b_w_out, ffn_w_in=m_ffn_w_in,
               ffn_conv_w=m_ffn_conv_w, ffn_conv_b=m_ffn_conv_b, ffn_w_out=m_ffn_w_out, final_norm_w=m_final_norm_w)
    var = dict(norm1_w=v_norm1_w, norm2_w=v_norm2_w, a_w_in=v_a_w_in, a_lb_logits=v_a_lb_logits, a_norm_w=v_a_norm_w,
               a_w_out=v_a_w_out, b_w_in=v_b_w_in, b_conv_w=v_b_conv_w, b_conv_b=v_b_conv_b, b_dt_bias=v_b_dt_bias,
               b_a_log=v_b_a_log, b_d_skip=v_b_d_skip, b_norm_w=v_b_norm_w, b_w_out=v_b_w_out, ffn_w_in=v_ffn_w_in,
               ffn_conv_w=v_ffn_conv_w, ffn_conv_b=v_ffn_conv_b, ffn_w_out=v_ffn_w_out, final_norm_w=v_final_norm_w)
    chip = 2 * lax.axis_index("x") + lax.axis_index("y")
    place = jnp.stack([chip, lax.axis_index("c")]).astype(jnp.int32)

    sh = {"a_in": a_w_in[0], "a_out": a_w_out[0], "b_in": b_w_in[0], "b_out": b_w_out[0], "f0_in": ffn_w_in[0],
          "f1_in": ffn_w_in[1], "f0_out": ffn_w_out[0], "f1_out": ffn_w_out[1]}
    sh = {u: a.astype(BF) for u, a in sh.items()}
    small_shapes = [w[n].shape for n in _SMALL_SPLIT]
    spack, small_offs = _pack([w[n] for n in _SMALL_SPLIT], 128, 8, F32)
    sall = _allgather8("s_gather", spack)
    sshards = [_unpack(sall[2 * j], small_shapes, small_offs) for j in range(4)]
    sfull = {n: jnp.concatenate([sshards[j][i] for j in range(4)], axis=-1) for i, n in enumerate(_SMALL_SPLIT)}

    p = dict(
        norm1_w=norm1_w, norm2_w=norm2_w, a_lb_logits=a_lb_logits, a_norm_w=a_norm_w[0], final_norm_w=final_norm_w,
        b_conv_w=sfull["b_conv_w"][0], b_conv_b=sfull["b_conv_b"][0], b_norm_w=sfull["b_norm_w"],
        ffn_conv_w=sfull["ffn_conv_w"], ffn_conv_b=ffn_conv_b,
        b_dt_bias=jnp.pad(b_dt_bias.reshape(1, 2 * SSM_HEADS), ((0, 0), (0, 128 - 2 * SSM_HEADS))),
        b_a_log=jnp.pad(b_a_log.reshape(1, 2 * SSM_HEADS), ((0, 0), (0, 128 - 2 * SSM_HEADS))),
        b_d_skip=jnp.repeat(b_d_skip[0], SSM_HD)[None],
    )

    loss_row, dx, g, gs_ = _sequence_grads(x[0], loss_target[0], p, sh, place)
    grads = {"a_w_in": gs_["a_in"][None], "a_w_out": gs_["a_out"][None], "b_w_in": gs_["b_in"].T[None],
             "b_w_out": gs_["b_out"][None], "ffn_w_in": jnp.stack([gs_["f0_in"], gs_["f1_in"]]),
             "ffn_w_out": jnp.stack([gs_["f0_out"], gs_["f1_out"]])}

    gsmall = {
        "norm1_w": jnp.concatenate(g["norm1_w"], axis=0), "norm2_w": jnp.concatenate(g["norm2_w"], axis=0),
        "a_lb_logits": jnp.stack(g["a_lb_logits"]), "a_norm_w": g["a_norm_w"], "b_conv_w": g["b_conv_w"],
        "b_conv_b": g["b_conv_b"], "b_dt_bias": g["b_dt_bias"], "b_a_log": g["b_a_log"], "b_d_skip": g["b_d_skip"],
        "b_norm_w": g["b_norm_w"], "ffn_conv_w": jnp.stack(g["ffn_conv_w"]),
        "ffn_conv_b": jnp.concatenate(g["ffn_conv_b"], axis=0), "final_norm_w": g["final_norm_w"],
    }
    pieces = [gsmall[n] for n in _SMALL] + [loss_row]
    piece_shapes = [a.shape for a in pieces]
    gspack, gs_offs = _pack(pieces, 128, 8, F32)
    rows = gspack.shape[0]
    gsall = _allgather8("gs_gather", gspack)

    def sum8(a):
        r = a[0]
        for i in range(1, 8):
            r = r + a[i]
        return r

    gssum = _vcall("gs_sum", sum8, (1,), [(gsall, (8, rows, 128), lambda i: (0, 0, 0))],
                   [((rows, 128), F32, (rows, 128), lambda i: (0, 0))])
    gs = dict(zip(_SMALL + ("loss",), _unpack(gssum, piece_shapes, gs_offs)))
    loss = gs["loss"][0, 0]
    lb2 = gs["a_lb_logits"]
    small_grads = {
        "norm1_w": gs["norm1_w"], "norm2_w": gs["norm2_w"], "a_lb_logits": lb2[0] + lb2[1], "a_norm_w": gs["a_norm_w"],
        "b_dt_bias": gs["b_dt_bias"][:, :2 * SSM_HEADS].reshape(1, 2, SSM_HEADS),
        "b_a_log": gs["b_a_log"][:, :2 * SSM_HEADS].reshape(1, 2, SSM_HEADS),
        "b_d_skip": gs["b_d_skip"].reshape(1, SSM_HEADS, SSM_HD).sum(axis=-1),
        "ffn_conv_b": gs["ffn_conv_b"], "final_norm_w": gs["final_norm_w"][0],
        "b_conv_w": gs["b_conv_w"][None], "b_conv_b": gs["b_conv_b"], "b_norm_w": gs["b_norm_w"], "ffn_conv_w": gs["ffn_conv_w"],
    }
    for n in _SMALL_SPLIT:
        width = w[n].shape[-1]
        small_grads[n] = lax.dynamic_slice_in_dim(small_grads[n], chip * width, width, axis=small_grads[n].ndim - 1)
    grads.update(small_grads)

    delta, new_m, new_v = {}, {}, {}
    for n in _BIG:
        shp = w[n].shape
        two_d = (shp[0] * shp[1], shp[2])
        d_, m_, v_ = _adam(f"adam_{n}", w[n].reshape(two_d), grads[n].reshape(two_d), mom[n].reshape(two_d), var[n].reshape(two_d))
        delta[n], new_m[n], new_v[n] = d_.reshape(shp), m_.reshape(shp), v_.reshape(shp)
    s_shapes = [w[n].shape for n in _SMALL]
    packs = [_pack([src[n] for n in _SMALL], 128, 8, F32) for src in (w, grads, mom, var)]
    outs = _adam("adam_small", *[pk[0] for pk in packs])
    for res, dst in zip(outs, (delta, new_m, new_v)):
        dst.update(dict(zip(_SMALL, _unpack(res, s_shapes, packs[0][1]))))

    return (loss, dx[None], *[grads[n] for n in _ORDER], *[delta[n] for n in _ORDER],
            *[new_m[n] for n in _ORDER], *[new_v[n] for n in _ORDER])
```

```python
import functools

import jax
import jax.numpy as jnp
from jax import lax
from jax.experimental import pallas as pl
from jax.experimental.pallas import tpu as pltpu

F32, BF = jnp.float32, jnp.bfloat16
HI = lax.Precision.HIGHEST

D = 1024
EPS = 1e-6
HG_HEADS, HG_HD, HG_CHUNK, HG_SUB = 8, 128, 64, 16
HG_HB = 4
SSM_GB = 2
D_INNER, SSM_HEADS, SSM_HD, SSM_GROUPS, SSM_HPG, SSM_N, SSD_CHUNK = 2048, 32, 64, 8, 4, 128, 128
CONV_DIM = D_INNER + 2 * SSM_GROUPS * SSM_N
B_PROJ = 2 * D_INNER + 2 * SSM_GROUPS * SSM_N + 2 * SSM_HEADS
B_PROJ_PAD = 6272
D_FF = 2816
NEG = -1e30
ROWS = 256
VMEM_LIMIT = 56 * 1024 * 1024

ADAM_LR, ADAM_B1, ADAM_B2, ADAM_EPS, ADAM_WD, ADAM_STEP = 0.001, 0.9, 0.999, 1e-08, 0.01, 10

MESH = pl.DeviceIdType.MESH


def _pick(n, cands):
    for c in cands:
        if n % c == 0:
            return c
    return n


class _Side:
    def __init__(self, ins, outs, plan, n_remote, n_local, alias=None):
        self.ins, self.outs, self.plan, self.n_remote, self.n_local = list(ins), list(outs), plan, n_remote, n_local
        self.alias = alias or {}

    def copies(self, in_refs, out_refs, send_sems, recv_sems, local_sems):
        remote, local = self.plan(in_refs, out_refs)
        cps = [pltpu.make_async_copy(s, d, local_sems.at[i]) for i, (s, d) in enumerate(local)]
        cps += [pltpu.make_async_remote_copy(src_ref=s, dst_ref=d, send_sem=send_sems.at[i], recv_sem=recv_sems.at[i],
                                             device_id=dev, device_id_type=MESH)
                for i, (s, d, dev) in enumerate(remote)]
        return cps

    def sems(self):
        return [pltpu.SemaphoreType.DMA((self.n_remote,)), pltpu.SemaphoreType.DMA((self.n_remote,)),
                pltpu.SemaphoreType.DMA((max(self.n_local, 1),))]


def _vcall(name, fn, grid, ins, outs, acc=None, scratch=(), place=None, side=None):
    acc = acc or {}
    n_in, n_out, nd = len(ins), len(outs), len(grid)
    n_pre = 0 if place is None else 1
    n_sin = len(side.ins) if side else 0
    n_sout = len(side.outs) if side else 0
    n_scr = len(scratch)

    def body(*refs):
        refs = refs[n_pre:]
        in_refs, refs = refs[:n_in], refs[n_in:]
        sin_refs, refs = refs[:n_sin], refs[n_sin:]
        out_refs, refs = refs[:n_out], refs[n_out:]
        sout_refs, refs = refs[:n_sout], refs[n_sout:]
        scr, sems = refs[:n_scr], refs[n_scr:]
        if side:
            at_first, at_last = None, None
            for ax in range(nd):
                f, l = pl.program_id(ax) == 0, pl.program_id(ax) == grid[ax] - 1
                at_first = f if at_first is None else jnp.logical_and(at_first, f)
                at_last = l if at_last is None else jnp.logical_and(at_last, l)

            @pl.when(at_first)
            def _():
                for cp in side.copies(sin_refs, sout_refs, *sems):
                    cp.start()

        res = fn(*[r[...] for r in in_refs], *scr)
        if not isinstance(res, (tuple, list)):
            res = (res,)
        if side:
            @pl.when(at_last)
            def _():
                for cp in side.copies(sin_refs, sout_refs, *sems):
                    cp.wait()
        for j, (o_ref, r) in enumerate(zip(out_refs, res)):
            mode = acc.get(j)
            if mode is None:
                o_ref[...] = r.astype(o_ref.dtype)
                continue
            first = pl.program_id(nd - 1) == 0
            if mode == "all":
                for ax in range(nd - 1):
                    first = jnp.logical_and(first, pl.program_id(ax) == 0)

            @pl.when(first)
            def _():
                o_ref[...] = r.astype(o_ref.dtype)

            @pl.when(jnp.logical_not(first))
            def _():
                o_ref[...] += r.astype(o_ref.dtype)

    hbm = pl.BlockSpec(memory_space=pl.ANY)
    in_specs = [pl.BlockSpec(bs, im) for _, bs, im in ins] + [hbm] * n_sin
    out_specs = [pl.BlockSpec(bs, im) for _, _, bs, im in outs] + [hbm] * n_sout
    params = pltpu.CompilerParams(dimension_semantics=("arbitrary",) * nd, vmem_limit_bytes=VMEM_LIMIT)
    out_shape = [jax.ShapeDtypeStruct(s, dt) for s, dt, _, _ in outs]
    operands = [a for a, _, _ in ins]
    scratch = list(scratch)
    aliases = {}
    if side:
        out_shape += [jax.ShapeDtypeStruct(s, dt) for s, dt in side.outs]
        operands += side.ins
        scratch += side.sems()
        aliases = {n_pre + n_in + i: n_out + o for i, o in side.alias.items()}
    if place is None:
        out = pl.pallas_call(body, name=name, grid=grid, in_specs=in_specs, out_specs=out_specs, out_shape=out_shape,
                             scratch_shapes=scratch, compiler_params=params, input_output_aliases=aliases)(*operands)
    else:
        spec = pltpu.PrefetchScalarGridSpec(num_scalar_prefetch=1, grid=grid, in_specs=in_specs, out_specs=out_specs,
                                            scratch_shapes=scratch)
        out = pl.pallas_call(body, name=name, grid_spec=spec, out_shape=out_shape, compiler_params=params,
                             input_output_aliases=aliases)(place, *operands)
    if side:
        return tuple(out[:n_out]), tuple(out[n_out:])
    return out[0] if n_out == 1 else out


def _mm(name, a, b, kind, out_dtype=F32, add=None):
    if kind == "tn":
        m, k = a.shape
        _, n = b.shape
        tm = _pick(m, (1024, 512, 256))
        tk = _pick(k, (1024, 1408, 896, 512, 256, 128))
        tn = _pick(n, (1024, 1408, 896, 512, 256, 128))

        def fn(av, bv):
            return lax.dot_general(av.astype(BF), bv.astype(BF), (((0,), (0,)), ((), ())),
                                   preferred_element_type=F32)

        return _vcall(name, fn, (k // tk, n // tn, m // tm),
                      [(a, (tm, tk), lambda i, j, s: (s, i)), (b, (tm, tn), lambda i, j, s: (s, j))],
                      [((k, n), F32, (tk, tn), lambda i, j, s: (i, j))], acc={0: "last"})
    m, k = a.shape
    n = b.shape[1] if kind == "nn" else b.shape[0]
    long_k = k > 4096
    tm = _pick(m, (512, 256)) if long_k else _pick(m, (1024, 512, 256))
    tn = _pick(n, (512, 896, 256, 128)) if long_k else _pick(n, (1024, 1408, 896, 512, 256, 128))
    dims =(((1,), (0,)), ((), ())) if kind == "nn" else (((1,), (1,)), ((), ()))

    def fn(av, bv, *rest):
        r = lax.dot_general(av.astype(BF), bv.astype(BF), dims, preferred_element_type=F32)
        return r + rest[0] if rest else r

    ins = [(a, (tm, k), lambda i, j: (i, 0)),
           (b, (k, tn), lambda i, j: (0, j)) if kind == "nn" else (b, (tn, k), lambda i, j: (j, 0))]
    if add is not None:
        ins.append((add, (tm, tn), lambda i, j: (i, j)))
    return _vcall(name, fn, (m // tm, n // tn), ins, [((m, n), out_dtype, (tm, tn), lambda i, j: (i, j))])


def _rms(h, w):
    return h * lax.rsqrt(jnp.mean(h * h, axis=-1, keepdims=True) + EPS) * w


def _rms_fwd(name, h, w):
    L = h.shape[0]
    tb = _pick(L, (ROWS,))
    return _vcall(name, _rms, (L // tb,),
                  [(h, (tb, D), lambda i: (i, 0)), (w.reshape(1, D), (1, D), lambda i: (0, 0))],
                  [((L, D), BF, (tb, D), lambda i: (i, 0))])


def _rms_bwd(name, du, h, w, dh_next):
    L = h.shape[0]
    tb = _pick(L, (ROWS,))

    def fn(duv, hv, wv, dnv):
        _, vjp = jax.vjp(_rms, hv, wv)
        dh, dw = vjp(duv)
        return dh + dnv, dw

    row = lambda i: (i, 0)
    return _vcall(name, fn, (L // tb,),
                  [(du, (tb, D), row), (h, (tb, D), row), (w.reshape(1, D), (1, D), lambda i: (0, 0)),
                   (dh_next, (tb, D), row)],
                  [((L, D), F32, (tb, D), row), ((1, D), F32, (1, D), lambda i: (0, 0))], acc={1: "all"})


def _loss_head(name, h, tgt, w):
    L = h.shape[0]
    tb = _pick(L, (ROWS,))

    def lossf(hv, wv, tv):
        err = _rms(hv, wv) - tv
        return 0.5 * jnp.sum(err * err) * (1.0 / D)

    def fn(hv, wv, tv):
        val, vjp = jax.vjp(lambda a, b: lossf(a, b, tv), hv, wv)
        dh, dw = vjp(jnp.ones((), F32))
        return jnp.full((1, 128), val, F32), dh, dw

    row = lambda i: (i, 0)
    zero = lambda i: (0, 0)
    return _vcall(name, fn, (L // tb,),
                  [(h, (tb, D), row), (w.reshape(1, D), (1, D), zero), (tgt, (tb, D), row)],
                  [((1, 128), F32, (1, 128), zero), ((L, D), F32, (tb, D), row), ((1, D), F32, (1, D), zero)],
                  acc={0: "all", 2: "all"})


def _bf(x):
    return x.astype(BF)


def _dot(a, b, dims, precision=None):
    return lax.dot_general(a, b, (dims, ((), ())), preferred_element_type=F32, precision=precision)


def _tri(n, reverse):
    r = lax.broadcasted_iota(jnp.int32, (n, n), 0)
    c = lax.broadcasted_iota(jnp.int32, (n, n), 1)
    return (r <= c) if reverse else (r >= c)


def _tri_matmul(n, reverse, x):
    hi = x.astype(BF)
    r1 = x - hi.astype(F32)
    mid = r1.astype(BF)
    lo = (r1 - mid.astype(F32)).astype(BF)
    y = _dot(_tri(n, reverse).astype(BF), jnp.concatenate([hi, mid, lo], axis=1), ((1,), (0,)))
    w = x.shape[1]
    return (y[:, :w] + y[:, w:2 * w]) + y[:, 2 * w:]


@functools.partial(jax.custom_vjp, nondiff_argnums=(0, 1))
def _running_sum(n, reverse, x):
    return _tri_matmul(n, reverse, x)


def _running_sum_fwd(n, reverse, x):
    return _tri_matmul(n, reverse, x), None


def _running_sum_bwd(n, reverse, _, ct):
    return (_tri_matmul(n, not reverse, ct),)


_running_sum.defvjp(_running_sum_fwd, _running_sum_bwd)


def _gla_chunk(q_raw, f_raw, v, lb3, S, reverse):
    C, SB, HD = HG_CHUNK, HG_SUB, HG_HD
    H = S.shape[0]
    heads = [slice(HD * h, HD * (h + 1)) for h in range(H)]
    row3 = lax.broadcasted_iota(jnp.int32, (3, 1), 0)
    e = jnp.exp(lb3 - jnp.max(lb3, axis=0, keepdims=True))
    lb = jnp.sum(jnp.where(row3 == 0, e, 0.0), axis=0, keepdims=True) / jnp.sum(e, axis=0, keepdims=True)
    q = q_raw * jax.nn.sigmoid(q_raw)
    f = lb + (1.0 - lb) * jax.nn.sigmoid(f_raw)
    g = jnp.log(f)
    k = 1.0 - f
    b = _running_sum(C, reverse, g)
    row = lax.broadcasted_iota(jnp.int32, (C, 1), 0)
    vb = _bf(v)

    def rowof(x, t):
        return jnp.sum(jnp.where(row == t, x, 0.0), axis=0, keepdims=True)

    qe = _bf(q * jnp.exp(b))
    o = [_dot(qe[:, hs], _bf(S[h]), ((1,), (0,))) for h, hs in enumerate(heads)]
    att = [None] * H
    for i in range(C // SB):
        lo = SB * i
        if (not reverse and i == 0) or (reverse and i == C // SB - 1):
            continue
        first = lo + SB - 1 if reverse else lo
        r = rowof(b, first) - rowof(g, first)
        in_blk = jnp.logical_and(row >= lo, row < lo + SB)
        before = (row >= lo + SB) if reverse else (row < lo)
        qi = _bf(q * jnp.exp(jnp.where(in_blk, b - r, NEG)))
        kk = _bf(k * jnp.exp(jnp.where(before, r - b, NEG)))
        for h, hs in enumerate(heads):
            a_i = _dot(qi[:, hs], kk[:, hs], ((1,), (1,)))
            att[h] = a_i if att[h] is None else att[h] + a_i
    o = [o[h] + _dot(_bf(att[h]), vb[:, hs], ((1,), (0,))) for h, hs in enumerate(heads)]
    s_i = lax.broadcasted_iota(jnp.int32, (SB, SB, HD), 0)
    t_i = lax.broadcasted_iota(jnp.int32, (SB, SB, HD), 1)
    pair = (t_i <= s_i) if reverse else (t_i >= s_i)
    shp = (SB, SB, HD)
    diag = [[] for _ in range(H)]
    for i in range(C // SB):
        rows = slice(SB * i, SB * (i + 1))
        for h, hs in enumerate(heads):
            qb, kb, bb = q[rows, hs], k[rows, hs], b[rows, hs]
            dif = lax.broadcast_in_dim(bb, shp, (1, 2)) - lax.broadcast_in_dim(bb, shp, (0, 2))
            w = lax.broadcast_in_dim(qb, shp, (1, 2)) * jnp.exp(jnp.where(pair, dif, NEG)) * lax.broadcast_in_dim(kb, shp, (0, 2))
            d = jnp.sum(w, axis=2, keepdims=True)
            diag[h].append(jnp.sum(d * lax.broadcast_in_dim(v[rows, hs], shp, (0, 2)), axis=0))
    o = jnp.concatenate([o[h] + jnp.concatenate(diag[h], axis=0) for h in range(H)], axis=1)
    btot = rowof(b, 0 if reverse else C - 1)
    kd = _bf(k * jnp.exp(btot - b))
    eye = lax.broadcasted_iota(jnp.int32, (HD, HD), 0) == lax.broadcasted_iota(jnp.int32, (HD, HD), 1)
    s_new = []
    for h, hs in enumerate(heads):
        btot_col = jnp.sum(jnp.where(eye, btot[:, hs], 0.0), axis=1, keepdims=True)
        s_new.append((jnp.exp(btot_col) * S[h] + _dot(kd[:, hs], vb[:, hs], ((0,), (0,))))[None])
    return o, jnp.concatenate(s_new, axis=0)


def _gla_fwd(name, pa, lbl, reverse, side=None):
    L = pa.shape[0]
    C = HG_CHUNK
    nc = L // C
    cidx = (lambda i: nc - 1 - i) if reverse else (lambda i: i)
    sec = 2 if reverse else 1
    hb_n, nh = HG_HB, HG_HEADS // HG_HB

    def fn(qr, fr, v, lb3, s_ref):
        @pl.when(pl.program_id(1) == 0)
        def _():
            s_ref[...] = jnp.zeros_like(s_ref)

        s_all = s_ref[...]
        o, s_new = _gla_chunk(qr, fr, v, lb3, s_all, reverse)
        s_ref[...] = s_new
        return o, s_all[None]

    blk = (C, HG_HD * hb_n)
    return _vcall(name, fn, (nh, nc),
                  [(pa, blk, lambda h, i: (cidx(i), h)), (pa, blk, lambda h, i: (cidx(i), sec * nh + h)),
                   (pa, blk, lambda h, i: (cidx(i), 3 * nh + h)), (lbl, (3, HG_HD * hb_n), lambda h, i: (0, h))],
                  [((L, D), F32, blk, lambda h, i: (cidx(i), h)),
                   ((nc, HG_HEADS, HG_HD, HG_HD), F32, (1, hb_n, HG_HD, HG_HD), lambda h, i: (cidx(i), h, 0, 0))],
                  scratch=[pltpu.VMEM((hb_n, HG_HD, HG_HD), F32)], side=side)


def _gla_bwd(name, pa, lbl, s_in, do, reverse, prev=None, side=None):
    L = pa.shape[0]
    C = HG_CHUNK
    nc = L // C
    cidx = (lambda i: i) if reverse else (lambda i: nc - 1 - i)
    sec = 2 if reverse else 1
    n_prev = 0 if prev is None else 2
    hb_n, nh = HG_HB, HG_HEADS // HG_HB

    def fn(qr, fr, v, lb3, s, dov, *rest):
        ds_ref = rest[n_prev]

        @pl.when(pl.program_id(1) == 0)
        def _():
            ds_ref[...] = jnp.zeros_like(ds_ref)

        _, vjp = jax.vjp(lambda *a: _gla_chunk(*a, reverse), qr, fr, v, lb3, s[0])
        dq, df, dv, dlb, ds = vjp((dov, ds_ref[...]))
        ds_ref[...] = ds
        if n_prev:
            dq, dv = dq + rest[0], dv + rest[1]
        return dq, df, dv, dlb

    blk = (C, HG_HD * hb_n)
    at = lambda h, i: (cidx(i), h)
    ins = [(pa, blk, at), (pa, blk, lambda h, i: (cidx(i), sec * nh + h)), (pa, blk, lambda h, i: (cidx(i), 3 * nh + h)),
           (lbl, (3, HG_HD * hb_n), lambda h, i: (0, h)),
           (s_in, (1, hb_n, HG_HD, HG_HD), lambda h, i: (cidx(i), h, 0, 0)), (do, blk, at)]
    if prev is not None:
        ins += [(prev[0], blk, at), (prev[1], blk, at)]
    sum_dt = F32 if prev is None else BF
    return _vcall(name, fn, (nh, nc), ins,
                  [((L, D), sum_dt, blk, at), ((L, D), BF, blk, at), ((L, D), sum_dt, blk, at),
                   ((3, D), F32, (3, HG_HD * hb_n), lambda h, i: (0, h))],
                  acc={3: "last"}, scratch=[pltpu.VMEM((hb_n, HG_HD, HG_HD), F32)], side=side)


def _hgout(o_f, o_b, g, nw):
    o = o_f + o_b
    return _rms(o, nw) * (g * jax.nn.sigmoid(g))


def _hgout_fwd(name, o_f, o_b, pa, nw, side=None):
    L = o_f.shape[0]
    tb = _pick(L, (ROWS,))
    blk = (tb, HG_HD)
    at = lambda h, i: (i, h)
    return _vcall(name, _hgout, (HG_HEADS, L // tb),
                  [(o_f, blk, at), (o_b, blk, at), (pa, blk, lambda h, i: (i, 32 + h)),
                   (nw.reshape(1, HG_HD), (1, HG_HD), lambda h, i: (0, 0))],
                  [((L, D), BF, blk, at)], side=side)


def _hgout_bwd(name, o_f, o_b, pa, nw, dy):
    L = o_f.shape[0]
    tb = _pick(L, (ROWS,))

    def fn(ofv, obv, gv, nwv, dyv):
        _, vjp = jax.vjp(_hgout, ofv, obv, gv, nwv)
        do, _, dg, dnw = vjp(dyv)
        return do, dg, dnw

    blk = (tb, HG_HD)
    at = lambda h, i: (i, h)
    zero = lambda h, i: (0, 0)
    return _vcall(name, fn, (HG_HEADS, L // tb),
                  [(o_f, blk, at), (o_b, blk, at), (pa, blk, lambda h, i: (i, 32 + h)),
                   (nw.reshape(1, HG_HD), (1, HG_HD), zero), (dy, blk, at)],
                  [((L, D), F32, blk, at), ((L, D), BF, blk, at), ((1, HG_HD), F32, (1, HG_HD), zero)],
                  acc={2: "all"})


def _shift(x, s):
    if s == 0:
        return x
    n = x.shape[0]
    t = lax.broadcasted_iota(jnp.int32, (n, 1), 0)
    if s > 0:
        return jnp.where(t >= s, pltpu.roll(x, s, 0), 0.0)
    return jnp.where(t < n + s, pltpu.roll(x, n + s, 0), 0.0)


def _conv(x, w, b):
    kk = w.shape[0]
    p = (kk - 1) // 2
    y = b
    for j in range(kk):
        y = y + w[j:j + 1] * _shift(x, p - j)
    return y


def _conv_bwd(x, w, dc):
    kk = w.shape[0]
    p = (kk - 1) // 2
    dx = None
    dws = []
    for j in range(kk):
        t = w[j:j + 1] * _shift(dc, j - p)
        dx = t if dx is None else dx + t
        dws.append(jnp.sum(dc * _shift(x, p - j), axis=0, keepdims=True))
    rows = lax.broadcasted_iota(jnp.int32, (kk, 1), 0)
    dw = None
    for j in range(kk):
        t = jnp.where(rows == j, dws[j], 0.0)
        dw = t if dw is None else dw + t
    return dx, dw, jnp.sum(dc, axis=0, keepdims=True)


def _silu_grad(c):
    s = jax.nn.sigmoid(c)
    return s * (1.0 + c * (1.0 - s))


def _glu_fwd(name, pf, cw, cb):
    L = pf.shape[0]
    tc = 128
    nt = D_FF // tc

    def fn(gate, val, w, b):
        c = _conv(gate, w, b)
        return c * jax.nn.sigmoid(c) * val

    return _vcall(name, fn, (nt,),
                  [(pf, (L, tc), lambda j: (0, j)), (pf, (L, tc), lambda j: (0, nt + j)),
                   (cw, (3, tc), lambda j: (0, j)), (cb.reshape(1, D_FF), (1, tc), lambda j: (0, j))],
                  [((L, D_FF), BF, (L, tc), lambda j: (0, j))])


def _glu_bwd(name, pf, cw, cb, dy):
    L = pf.shape[0]
    tc = 128
    nt = D_FF // tc

    def fn(gate, val, w, b, dyv):
        c = _conv(gate, w, b)
        sc = c * jax.nn.sigmoid(c)
        dc = dyv * val * _silu_grad(c)
        dgate, dw, db = _conv_bwd(gate, w, dc)
        return dgate, dyv * sc, dw, db

    col = lambda j: (0, j)
    return _vcall(name, fn, (nt,),
                  [(pf, (L, tc), col), (pf, (L, tc), lambda j: (0, nt + j)), (cw, (3, tc), col),
                   (cb.reshape(1, D_FF), (1, tc), col), (dy, (L, tc), col)],
                  [((L, D_FF), BF, (L, tc), col), ((L, D_FF), BF, (L, tc), col),
                   ((3, D_FF), F32, (3, tc), col), ((1, D_FF), F32, (1, tc), col)])


def _perm_tile(j):
    return jnp.where(j < 16, 4 * (j // 2) + j % 2, jnp.where(j < 24, 4 * (j - 16) + 2, 4 * (j - 24) + 3))


def _mpre_fwd(name, pb, cw, cb):
    L = pb.shape[0]
    tc = 128

    def fn(x, w, b):
        c = _conv(x, w, b)
        return c * jax.nn.sigmoid(c)

    return _vcall(name, fn, (CONV_DIM // tc,),
                  [(pb, (L, tc), lambda j: (0, 16 + j)), (cw, (5, tc), lambda j: (0, j)),
                   (cb.reshape(1, CONV_DIM), (1, tc), lambda j: (0, j))],
                  [((L, CONV_DIM), F32, (L, tc), lambda j: (0, _perm_tile(j)))])


def _mpre_bwd(name, pb, cw, cb, dact):
    L = pb.shape[0]
    tc = 128

    def fn(x, w, b, da):
        c = _conv(x, w, b)
        return _conv_bwd(x, w, da * _silu_grad(c))

    col = lambda j: (0, j)
    return _vcall(name, fn, (CONV_DIM // tc,),
                  [(pb, (L, tc), lambda j: (0, 16 + j)), (cw, (5, tc), col), (cb.reshape(1, CONV_DIM), (1, tc), col),
                   (dact, (L, tc), lambda j: (0, _perm_tile(j)))],
                  [((L, CONV_DIM), BF, (L, tc), col), ((5, CONV_DIM), F32, (5, tc), col),
                   ((1, CONV_DIM), F32, (1, tc), col)])


def _softplus(x):
    return jnp.maximum(x, 0.0) + jnp.log(1.0 + jnp.exp(-jnp.abs(x)))


def _dt_fwd(name, pb, dtb, alog):
    L = pb.shape[0]
    tb = _pick(L, (1024, ROWS))

    def fn(x, bias, al):
        dt = _softplus(x + bias)
        return dt, dt * (-jnp.exp(al))

    row = lambda i: (i, 0)
    zero = lambda i: (0, 0)
    return _vcall(name, fn, (L // tb,),
                  [(pb, (tb, 128), lambda i: (i, 48)), (dtb, (1, 128), zero), (alog, (1, 128), zero)],
                  [((L, 128), F32, (tb, 128), row), ((L, 128), F32, (tb, 128), row)])


def _dt_bwd(name, pb, dtb, alog, ddt_f, dla_f, ddt_b, dla_b):
    L = pb.shape[0]
    tb = _pick(L, (1024, ROWS))

    def fn(x, bias, al, a1, b1, a2, b2):
        ddt = jnp.sum(a1, axis=0) + jnp.sum(a2, axis=0)
        dla = jnp.sum(b1, axis=0) + jnp.sum(b2, axis=0)
        z = x + bias
        dt = _softplus(z)
        a = -jnp.exp(al)
        dz = (ddt + dla * a) * jax.nn.sigmoid(z)
        return dz, jnp.sum(dz, axis=0, keepdims=True), jnp.sum(dla * dt, axis=0, keepdims=True) * a

    zero = lambda i: (0, 0)
    g3 = (SSM_GROUPS, tb, 128)
    at3 = lambda i: (0, i, 0)
    return _vcall(name, fn, (L // tb,),
                  [(pb, (tb, 128), lambda i: (i, 48)), (dtb, (1, 128), zero), (alog, (1, 128), zero),
                   (ddt_f, g3, at3), (dla_f, g3, at3), (ddt_b, g3, at3), (dla_b, g3, at3)],
                  [((L, 128), BF, (tb, 128), lambda i: (i, 0)), ((1, 128), F32, (1, 128), zero),
                   ((1, 128), F32, (1, 128), zero)], acc={1: "all", 2: "all"})


def _ssd_chunk(xa, dt, la, hs, head0, reverse):
    C = SSD_CHUNK
    P4 = SSM_HPG * SSM_HD
    xs, bm, cm = xa[:, :P4], xa[:, P4:P4 + SSM_N], xa[:, P4 + SSM_N:]
    lane = lax.broadcasted_iota(jnp.int32, (1, 128), 1)
    col_head = lax.broadcasted_iota(jnp.int32, (1, P4), 1) // SSM_HD
    row_head = lax.broadcasted_iota(jnp.int32, (P4, 1), 0) // SSM_HD
    row = lax.broadcasted_iota(jnp.int32, (C, 1), 0)
    eye = lax.broadcasted_iota(jnp.int32, (C, C), 0) == lax.broadcasted_iota(jnp.int32, (C, C), 1)
    tri = _tri(C, reverse)
    acum = _running_sum(C, reverse, la)
    cb = _dot(_bf(cm), _bf(bm), ((1,), (1,)))
    last = 0 if reverse else C - 1
    dt_x, ea_x, dec_x, y = 0.0, 0.0, 0.0, 0.0
    atot_rows = 0.0
    lmats = []
    for j in range(SSM_HPG):
        sel = lane == head0 + j
        dt_j = jnp.sum(jnp.where(sel, dt, 0.0), axis=1, keepdims=True)
        ac_j = jnp.sum(jnp.where(sel, acum, 0.0), axis=1, keepdims=True)
        ac_row = jnp.sum(jnp.where(eye, ac_j, 0.0), axis=0, keepdims=True)
        atot = jnp.sum(jnp.where(row == last, ac_j, 0.0), axis=0, keepdims=True)
        lmats.append(jnp.exp(jnp.where(tri, ac_j - ac_row, NEG)))
        mine = col_head == j
        dt_x = dt_x + jnp.where(mine, dt_j, 0.0)
        ea_x = ea_x + jnp.where(mine, jnp.exp(ac_j), 0.0)
        dec_x = dec_x + jnp.where(mine, jnp.exp(atot - ac_j), 0.0)
        atot_rows = atot_rows + jnp.where(row_head == j, jnp.exp(atot), 0.0)
    xd = xs * dt_x
    xdb = _bf(xd)
    for j in range(SSM_HPG):
        y = y + jnp.where(col_head == j, _dot(_bf(cb * lmats[j]), xdb, ((1,), (0,))), 0.0)
    y = y + _dot(_bf(cm), _bf(hs), ((1,), (1,))) * ea_x
    hs_new = atot_rows * hs + _dot(_bf(xd * dec_x), _bf(bm), ((0,), (0,)))
    return y, hs_new


def _ssd_fwd(name, xact, dt, la, reverse):
    L = xact.shape[0]
    C = SSD_CHUNK
    nc = L // C
    cidx = (lambda i: nc - 1 - i) if reverse else (lambda i: i)
    base = SSM_HEADS if reverse else 0
    P4 = SSM_HPG * SSM_HD

    gb_n = SSM_GB

    def fn(xa, dtv, lav, h_ref):
        @pl.when(pl.program_id(1) == 0)
        def _():
            h_ref[...] = jnp.zeros_like(h_ref)

        h_all = h_ref[...]
        ys, news = [], []
        for gb in range(gb_n):
            head0 = base + SSM_HPG * (gb_n * pl.program_id(0) + gb)
            y, hs_new = _ssd_chunk(xa[:, 512 * gb:512 * (gb + 1)], dtv, lav, h_all[gb], head0, reverse)
            ys.append(y)
            news.append(hs_new)
        h_ref[...] = jnp.stack(news)
        return jnp.concatenate(ys, axis=1), h_all[None]

    return _vcall(name, fn, (SSM_GROUPS // gb_n, nc),
                  [(xact, (C, 512 * gb_n), lambda g, i: (cidx(i), g)), (dt, (C, 128), lambda g, i: (cidx(i), 0)),
                   (la, (C, 128), lambda g, i: (cidx(i), 0))],
                  [((L, D_INNER), F32, (C, P4 * gb_n), lambda g, i: (cidx(i), g)),
                   ((nc, SSM_GROUPS, P4, SSM_N), F32, (1, gb_n, P4, SSM_N), lambda g, i: (cidx(i), g, 0, 0))],
                  scratch=[pltpu.VMEM((gb_n, P4, SSM_N), F32)])


def _ssd_bwd(name, xact, dt, la, h_in, dy, reverse, prev_xs=None, prev_all=None):
    L = xact.shape[0]
    C = SSD_CHUNK
    nc = L // C
    cidx = (lambda i: i) if reverse else (lambda i: nc - 1 - i)
    base = SSM_HEADS if reverse else 0
    P4 = SSM_HPG * SSM_HD

    gb_n = SSM_GB

    def fn(xa, dtv, lav, hs, dyv, pv, dh_ref):
        @pl.when(pl.program_id(1) == 0)
        def _():
            dh_ref[...] = jnp.zeros_like(dh_ref)

        dh_all = dh_ref[...]
        res, news = [], []
        for gb in range(gb_n):
            head0 = base + SSM_HPG * (gb_n * pl.program_id(0) + gb)
            _, vjp = jax.vjp(lambda a, b, c, d: _ssd_chunk(a, b, c, d, head0, reverse),
                             xa[:, 512 * gb:512 * (gb + 1)], dtv, lav, hs[0, gb])
            dxa, ddt, dla, dh = vjp((dyv[:, P4 * gb:P4 * (gb + 1)], dh_all[gb]))
            news.append(dh)
            if prev_all is not None:
                dxa = dxa + pv[:, 512 * gb:512 * (gb + 1)]
            else:
                dxa = dxa + jnp.concatenate([pv[:, P4 * gb:P4 * (gb + 1)], jnp.zeros((C, 2 * SSM_N), F32)], axis=1)
            res.append((dxa, ddt, dla))
        dh_ref[...] = jnp.stack(news)
        return (jnp.concatenate([r[0] for r in res], axis=1), jnp.stack([r[1] for r in res]),
                jnp.stack([r[2] for r in res]))

    at = lambda g, i: (cidx(i), g)
    at0 = lambda g, i: (cidx(i), 0)
    pv = (prev_all, (C, 512 * gb_n), at) if prev_all is not None else (prev_xs, (C, P4 * gb_n), at)
    return _vcall(name, fn, (SSM_GROUPS // gb_n, nc),
                  [(xact, (C, 512 * gb_n), at), (dt, (C, 128), at0), (la, (C, 128), at0),
                   (h_in, (1, gb_n, P4, SSM_N), lambda g, i: (cidx(i), g, 0, 0)), (dy, (C, P4 * gb_n), at), pv],
                  [((L, CONV_DIM), F32, (C, 512 * gb_n), at),
                   ((SSM_GROUPS, L, 128), F32, (gb_n, C, 128), lambda g, i: (g, cidx(i), 0)),
                   ((SSM_GROUPS, L, 128), F32, (gb_n, C, 128), lambda g, i: (g, cidx(i), 0))],
                  scratch=[pltpu.VMEM((gb_n, P4, SSM_N), F32)])


def _mpost(y_f, y_b, xs, z, dsk, nw):
    y = (y_f + y_b + xs * dsk) * (z * jax.nn.sigmoid(z))
    return _rms(y, nw)


def _mpost_fwd(name, y_f, y_b, xact, pb, dsk, nw):
    L = y_f.shape[0]
    tb = _pick(L, (ROWS,))
    blk = (tb, 256)
    at = lambda g, i: (i, g)
    par = lambda g, i: (0, g)
    return _vcall(name, _mpost, (SSM_GROUPS, L // tb),
                  [(y_f, blk, at), (y_b, blk, at), (xact, blk, lambda g, i: (i, 2 * g)), (pb, blk, at),
                   (dsk, (1, 256), par), (nw, (1, 256), par)],
                  [((L, D_INNER), BF, blk, at)])


def _mpost_bwd(name, y_f, y_b, xact, pb, dsk, nw, dy):
    L = y_f.shape[0]
    tb = _pick(L, (ROWS,))

    def fn(yf, yb, xs, z, dskv, nwv, dyv):
        _, vjp = jax.vjp(_mpost, yf, yb, xs, z, dskv, nwv)
        dyf, _, dxs, dz, ddsk, dnw = vjp(dyv)
        return dyf, dxs, dz, ddsk, dnw

    blk = (tb, 256)
    at = lambda g, i: (i, g)
    par = lambda g, i: (0, g)
    return _vcall(name, fn, (SSM_GROUPS, L // tb),
                  [(y_f, blk, at), (y_b, blk, at), (xact, blk, lambda g, i: (i, 2 * g)), (pb, blk, at),
                   (dsk, (1, 256), par), (nw, (1, 256), par), (dy, blk, at)],
                  [((L, D_INNER), F32, blk, at), ((L, D_INNER), F32, blk, at), ((L, D_INNER), BF, blk, at),
                   ((1, D_INNER), F32, (1, 256), par), ((1, D_INNER), F32, (1, 256), par)],
                  acc={3: "last", 4: "last"})


def _ffn_fwd(tag, h, nw, w_in, cw, cb, w_out):
    u = _rms_fwd(f"{tag}_norm", h, nw)
    pf = _mm(f"{tag}_in", u, w_in, "nn")
    yf = _glu_fwd(f"{tag}_glu", pf, cw, cb)
    return _mm(f"{tag}_out", yf, w_out, "nn", add=h), (u, pf, yf)


def _ffn_bwd(tag, h, nw, w_in, cw, cb, w_out, saved, dh):
    u, pf, yf = saved
    d_w_out = _mm(f"{tag}_dwout", yf, dh, "tn")
    dyf = _mm(f"{tag}_dy", dh, w_out, "nt")
    dgate, dval, dcw, dcb = _glu_bwd(f"{tag}_dglu", pf, cw, cb, dyf)
    dpf = jnp.concatenate([dgate, dval], axis=1)
    d_w_in = _mm(f"{tag}_dwin", u, dpf, "tn")
    du = _mm(f"{tag}_du", dpf, w_in, "nt")
    dh_in, dnw = _rms_bwd(f"{tag}_dnorm", du, h, nw, dh)
    return dh_in, dnw, d_w_in, dcw, dcb, d_w_out


def _sequence_grads(x, tgt, p, sh, place):
    g = {}
    lbl = p["a_lb_logits"]
    first, mid, last = ("a_in", "a_out"), ("f0_in", "f0_out", "b_in", "b_out"), ("f1_in", "f1_out")
    W = {}
    got = _exchange("w_first", _gather_side(first, (), sh, W))
    W.update(zip(first, got))
    got = _exchange("w_first_pass", _gather_side((), first, sh, W))
    W.update(zip(first, got))
    u1 = _rms_fwd("a_norm", x, p["norm1_w"][0])
    pa = _mm("a_in", u1, W["a_in"], "nn")
    (o_f, s_f), got = _gla_fwd("a_scan_f", pa, lbl, False, side=_gather_side(mid, (), sh, W))
    W.update(zip(mid, got))
    (o_b, s_b), got = _gla_fwd("a_scan_b", pa, lbl, True, side=_gather_side(last, mid, sh, W))
    W.update(zip(last + mid, got))
    (ya,), got = _hgout_fwd("a_gate", o_f, o_b, pa, p["a_norm_w"], side=_gather_side((), last, sh, W))
    W.update(zip(last, got))
    wb4 = W["b_in"].reshape(4, D, B_PROJ // 4)
    p = dict(p, a_w_in=W["a_in"], a_w_out=W["a_out"], b_w_out=W["b_out"], ffn_w_in=(W["f0_in"], W["f1_in"]),
             ffn_w_out=(W["f0_out"], W["f1_out"]),
             b_w_in=jnp.pad(jnp.concatenate([wb4[j] for j in range(4)], axis=1), ((0, 0), (0, B_PROJ_PAD - B_PROJ))))
    h1 = _mm("a_out", ya, p["a_w_out"], "nn", add=x)
    h2, ffn0 = _ffn_fwd("f0", h1, p["norm2_w"][0], p["ffn_w_in"][0], p["ffn_conv_w"][0], p["ffn_conv_b"][0], p["ffn_w_out"][0])
    u3 = _rms_fwd("b_norm", h2, p["norm1_w"][1])
    pb = _mm("b_in", u3, p["b_w_in"], "nn")
    xact = _mpre_fwd("b_conv", pb, p["b_conv_w"], p["b_conv_b"])
    dt, la = _dt_fwd("b_dt", pb, p["b_dt_bias"], p["b_a_log"])
    y_f, hs_f = _ssd_fwd("b_scan_f", xact, dt, la, False)
    y_b, hs_b = _ssd_fwd("b_scan_b", xact, dt, la, True)
    yb = _mpost_fwd("b_gate", y_f, y_b, xact, pb, p["b_d_skip"], p["b_norm_w"])
    h3 = _mm("b_out", yb, p["b_w_out"], "nn", add=h2)
    h4, ffn1 = _ffn_fwd("f1", h3, p["norm2_w"][1], p["ffn_w_in"][1], p["ffn_conv_w"][1], p["ffn_conv_b"][1], p["ffn_w_out"][1])
    loss, dh4, g["final_norm_w"] = _loss_head("head", h4, tgt, p["final_norm_w"])
    dh3, dn2_1, dwin1, dcw1, dcb1, dwout1 = _ffn_bwd("f1", h3, p["norm2_w"][1], p["ffn_w_in"][1], p["ffn_conv_w"][1],
                                                     p["ffn_conv_b"][1], p["ffn_w_out"][1], ffn1, dh4)
    G = {"f1_in": dwin1, "f1_out": dwout1}
    G["b_out"] = _mm("b_dwout", yb, dh3, "tn")
    dyb = _mm("b_dy", dh3, p["b_w_out"], "nt")
    dys, dxs, dz, g["b_d_skip"], g["b_norm_w"] = _mpost_bwd("b_dgate", y_f, y_b, xact, pb, p["b_d_skip"], p["b_norm_w"], dyb)
    dxa1, ddt_f, dla_f = _ssd_bwd("b_dscan_f", xact, dt, la, hs_f, dys, False, prev_xs=dxs)
    dxa, ddt_b, dla_b = _ssd_bwd("b_dscan_b", xact, dt, la, hs_b, dys, True, prev_all=dxa1)
    dxbc, g["b_conv_w"], g["b_conv_b"] = _mpre_bwd("b_dconv", pb, p["b_conv_w"], p["b_conv_b"], dxa)
    ddtr, g["b_dt_bias"], g["b_a_log"] = _dt_bwd("b_ddt", pb, p["b_dt_bias"], p["b_a_log"], ddt_f, dla_f, ddt_b, dla_b)
    dpb = jnp.concatenate([dz, dxbc, ddtr], axis=1)
    G["b_in"] = _mm("b_dwin", dpb, u3, "tn")
    du3 = _mm("b_du", dpb, p["b_w_in"], "nt")
    dh2, dn1_1 = _rms_bwd("b_dnorm", du3, h2, p["norm1_w"][1], dh3)
    dh1, dn2_0, G["f0_in"], dcw0, dcb0, G["f0_out"] = _ffn_bwd("f0", h1, p["norm2_w"][0], p["ffn_w_in"][0], p["ffn_conv_w"][0],
                                                              p["ffn_conv_b"][0], p["ffn_w_out"][0], ffn0, dh2)
    late = last + mid
    chip_sums = _pair_reduce("gl", late, G, place)
    G["a_out"] = _mm("a_dwout", ya, dh1, "tn")
    dya = _mm("a_dy", dh1, p["a_w_out"], "nt")
    do, dg, g["a_norm_w"] = _hgout_bwd("a_dgate", o_f, o_b, pa, p["a_norm_w"], dya)
    (dq1, df1, dv1, dl1), got = _gla_bwd("a_dscan_f", pa, lbl, s_f, do, False, side=_chips_side(late, chip_sums))
    shards = {u: _chip_sum(f"gl_sum_{u}", _GGEO[u], chip_sums[u], r, place) for u, r in zip(late, got)}
    (dq, df2, dv, dl2), got = _gla_bwd("a_dscan_b", pa, lbl, s_b, do, True, prev=(dq1, dv1), side=_halves_side(late, shards))
    shards = dict(zip(late, got))
    dpa = jnp.concatenate([dq, df1, df2, dv, dg], axis=1)
    G["a_in"] = _mm("a_dwin", u1, dpa, "tn")
    du1 = _mm("a_du", dpa, p["a_w_in"], "nt")
    dx, dn1_0 = _rms_bwd("a_dnorm", du1, x, p["norm1_w"][0], dh1)
    chip_sums = _pair_reduce("ga", first, G, place)
    got = _exchange("ga_chips", _chips_side(first, chip_sums))
    mine = {u: _chip_sum(f"ga_sum_{u}", _GGEO[u], chip_sums[u], r, place) for u, r in zip(first, got)}
    shards.update(zip(first, _exchange("ga_halves", _halves_side(first, mine))))
    g["a_lb_logits"] = (dl1, dl2)
    g["norm1_w"] = (dn1_0, dn1_1)
    g["norm2_w"] = (dn2_0, dn2_1)
    g["ffn_conv_w"] = (dcw0, dcw1)
    g["ffn_conv_b"] = (dcb0, dcb1)
    return loss, dx, g, shards


def _here():
    return lax.axis_index("x"), lax.axis_index("y"), lax.axis_index("c")


def _allgather8(name, src, by_core=False):
    blk = src.shape[1:] if by_core else src.shape

    def body(x_ref, out_ref, send_sems, recv_sems, local_sem):
        x, y, c = _here()
        me, sibling = (x, y, c), (x, y, 1 - c)
        chips = [(1 - x, y), (x, 1 - y), (1 - x, 1 - y)]
        own = x_ref.at[c] if by_core else x_ref

        def slot(px, py, pc):
            return out_ref.at[4 * px + 2 * py + pc]

        def copy(k, block, to, from_own=False):
            return pltpu.make_async_remote_copy(
                src_ref=own if from_own else slot(*block), dst_ref=slot(*block),
                send_sem=send_sems.at[k], recv_sem=recv_sems.at[k], device_id=to, device_id_type=MESH)

        mine = pltpu.make_async_copy(own, slot(*me), local_sem)
        mine.start()
        first = [copy(0, me, sibling, from_own=True)]
        first += [copy(1 + j, me, (*chip, c), from_own=True) for j, chip in enumerate(chips)]
        for cp in first:
            cp.start()
        passed = [copy(4 + j, (*chip, c), sibling) for j, chip in enumerate(chips)]
        for j, chip in enumerate(chips):
            copy(1 + j, (*chip, c), me).wait_recv()
            passed[j].start()
        copy(0, sibling, me).wait_recv()
        for j, chip in enumerate(chips):
            copy(4 + j, (*chip, 1 - c), me).wait_recv()
        for cp in first + passed:
            cp.wait_send()
        mine.wait()

    return pl.pallas_call(
        body, name=name,
        out_shape=jax.ShapeDtypeStruct((8,) + tuple(blk), src.dtype),
        in_specs=[pl.BlockSpec(memory_space=pl.ANY)],
        out_specs=pl.BlockSpec(memory_space=pl.ANY),
        scratch_shapes=[pltpu.SemaphoreType.DMA((7,)), pltpu.SemaphoreType.DMA((7,)), pltpu.SemaphoreType.DMA],
    )(src)


def _exchange(name, side):
    n_i, n_o = len(side.ins), len(side.outs)

    def body(*refs):
        copies = side.copies(refs[:n_i], refs[n_i:n_i + n_o], *refs[n_i + n_o:])
        for cp in copies:
            cp.start()
        for cp in copies:
            cp.wait()

    return pl.pallas_call(
        body, name=name,
        out_shape=[jax.ShapeDtypeStruct(s, dt) for s, dt in side.outs],
        in_specs=[pl.BlockSpec(memory_space=pl.ANY)] * n_i,
        out_specs=[pl.BlockSpec(memory_space=pl.ANY)] * n_o,
        scratch_shapes=side.sems(),
        input_output_aliases=dict(side.alias),
    )(*side.ins)


_WGEO = {"a_in": ("col", 1024, 1280), "a_out": ("row", 256, 1024), "b_in": ("row", 1024, 1552), "b_out": ("row", 512, 1024),
         "f0_in": ("col", 1024, 1408), "f1_in": ("col", 1024, 1408), "f0_out": ("row", 704, 1024), "f1_out": ("row", 704, 1024)}
_GGEO = dict(_WGEO, b_in=("row", 1552, 1024))


def _full_shape(geo):
    kind, r, cw = geo
    return (r, 4 * cw) if kind == "col" else (4 * r, cw)


def _times(i, step):
    return i * step if isinstance(i, int) else pl.multiple_of(i * step, step & -step)


def _win(ref, geo, j, h):
    kind, r, cw = geo
    hr = r // 2
    if kind == "col":
        return ref.at[pl.ds(_times(h, hr), hr), pl.ds(_times(j, cw), cw)]
    return ref.at[pl.ds(_times(2 * j + h, hr), hr), :]


def _half(ref, geo, h):
    hr = geo[1] // 2
    return ref.at[pl.ds(_times(h, hr), hr), :]


def _gather_side(first, second, sh, full):
    n1 = len(first)

    def plan(ins, outs):
        x, y, c = _here()
        m = 2 * x + y
        remote, local = [], []
        for u, src, dst_full in zip(first, ins[:n1], outs[:n1]):
            mine, dst = _half(src, _WGEO[u], c), _win(dst_full, _WGEO[u], m, c)
            local.append((mine, dst))
            remote.append((mine, dst, (x, y, 1 - c)))
            for k in (1, 2, 3):
                t = (m + k) % 4
                remote.append((mine, dst, (t // 2, t % 2, c)))
        for u, buf in zip(second, outs[n1:]):
            for k in (1, 2, 3):
                w_ = _win(buf, _WGEO[u], (m + k) % 4, c)
                remote.append((w_, w_, (x, y, 1 - c)))
        return remote, local

    return _Side([sh[u] for u in first] + [full[u] for u in second],
                 [(_full_shape(_WGEO[u]), BF) for u in first + second], plan, 4 * n1 + 3 * len(second), n1,
                 alias={n1 + i: n1 + i for i in range(len(second))})


def _pair_reduce(tag, units, G, place):
    def plan(ins, outs):
        x, y, c = _here()
        return [(_win(gr, _GGEO[u], j, 1 - c), got.at[j], (x, y, 1 - c))
                for u, gr, got in zip(units, ins, outs) for j in range(4)], []

    halves = [(4, _GGEO[u][1] // 2, _GGEO[u][2]) for u in units]
    gots = _exchange(f"{tag}_pair", _Side([G[u] for u in units], [(s, F32) for s in halves], plan, 4 * len(units), 0))
    out = {}
    for u, got, shp in zip(units, gots, halves):
        blk = shp[1:]
        at = (lambda j: (lax.axis_index("c"), j)) if _GGEO[u][0] == "col" else (lambda j: (2 * j + lax.axis_index("c"), 0))
        slab = lambda j: (j, 0, 0)
        out[u] = _vcall(f"{tag}_pair_sum_{u}", lambda a, b: (a + b[0])[None], (4,),
                        [(G[u], blk, at), (got, (1,) + blk, slab)], [(shp, BF, (1,) + blk, slab)])
    return out


def _chips_side(units, chip_sums):
    def plan(ins, outs):
        x, y, c = _here()
        m = 2 * x + y
        remote = []
        for s, got in zip(ins, outs):
            for k in (1, 2, 3):
                t = (m + k) % 4
                remote.append((s.at[t], got.at[k - 1], (t // 2, t % 2, c)))
        return remote, []

    return _Side([chip_sums[u] for u in units], [((3,) + chip_sums[u].shape[1:], BF) for u in units], plan,
                 3 * len(units), 0)


def _chip_sum(name, geo, chip_sums, got, place):
    _, r, cw = geo
    blk = (r // 2, cw)
    return _vcall(name, lambda a, b: ((a[0].astype(F32) + b[0].astype(F32)) + b[1].astype(F32)) + b[2].astype(F32), (1,),
                  [(chip_sums, (1,) + blk, lambda i: (2 * lax.axis_index("x") + lax.axis_index("y"), 0, 0)),
                   (got, (3,) + blk, lambda i: (0, 0, 0))],
                  [((r, cw), F32, blk, lambda i: (lax.axis_index("c"), 0))])


def _halves_side(units, shards):
    def plan(ins, outs):
        x, y, c = _here()
        return [(_half(o, _GGEO[u], c), _half(o, _GGEO[u], c), (x, y, 1 - c)) for u, o in zip(units, outs)], []

    return _Side([shards[u] for u in units], [(shards[u].shape, F32) for u in units], plan, len(units), 0,
                 alias={i: i for i in range(len(units))})


def _adam(name, w, g, m, v):
    rows, cols = w.shape
    tb = _pick(rows, (256, 128, 64, 8))

    def fn(wv, gv, mv, vv):
        m2 = ADAM_B1 * mv + (1.0 - ADAM_B1) * gv
        v2 = ADAM_B2 * vv + (1.0 - ADAM_B2) * jnp.square(gv)
        m_hat = m2 / (1.0 - ADAM_B1 ** ADAM_STEP)
        v_hat = v2 / (1.0 - ADAM_B2 ** ADAM_STEP)
        return -ADAM_LR * (m_hat / (jnp.sqrt(v_hat) + ADAM_EPS) + ADAM_WD * wv), m2, v2

    at = lambda i: (i, 0)
    return _vcall(name, fn, (rows // tb,), [(a, (tb, cols), at) for a in (w, g, m, v)],
                  [((rows, cols), F32, (tb, cols), at)] * 3)


def _pack(arrays, width, row_multiple, dtype):
    parts, offs, at = [], [], 0
    for a in arrays:
        flat = a.reshape(-1).astype(dtype)
        rows = -(-flat.shape[0] // (width * row_multiple)) * row_multiple
        parts.append(jnp.pad(flat, (0, rows * width - flat.shape[0])).reshape(rows, width))
        offs.append(at)
        at += rows
    return jnp.concatenate(parts, axis=0), offs


def _unpack(flat, shapes, offs):
    out = []
    for shp, at in zip(shapes, offs):
        n = 1
        for s in shp:
            n *= s
        rows = -(-n // flat.shape[1])
        out.append(flat[at:at + rows].reshape(-1)[:n].reshape(shp))
    return out


_BIG = ("a_w_in", "a_w_out", "b_w_in", "b_w_out", "ffn_w_in", "ffn_w_out")
_BIG_AXIS = {"a_w_in": 2, "a_w_out": 1, "b_w_in": 2, "b_w_out": 1, "ffn_w_in": 2, "ffn_w_out": 1}
_SMALL_SPLIT = ("b_conv_w", "b_conv_b", "b_norm_w", "ffn_conv_w")
_SMALL = ("norm1_w", "norm2_w", "a_lb_logits", "a_norm_w", "b_conv_w", "b_conv_b", "b_dt_bias", "b_a_log", "b_d_skip",
          "b_norm_w", "ffn_conv_w", "ffn_conv_b", "final_norm_w")
_ORDER = ("norm1_w", "norm2_w", "a_w_in", "a_lb_logits", "a_norm_w", "a_w_out", "b_w_in", "b_conv_w", "b_conv_b", "b_dt_bias",
          "b_a_log", "b_d_skip", "b_norm_w", "b_w_out", "ffn_w_in", "ffn_conv_w", "ffn_conv_b", "ffn_w_out", "final_norm_w")


def kernel(x, norm1_w, norm2_w, a_w_in, a_lb_logits, a_norm_w, a_w_out, b_w_in, b_conv_w, b_conv_b, b_dt_bias, b_a_log, b_d_skip, b_norm_w, b_w_out, ffn_w_in, ffn_conv_w, ffn_conv_b, ffn_w_out, final_norm_w, loss_target, m_norm1_w, m_norm2_w, m_a_w_in, m_a_lb_logits, m_a_norm_w, m_a_w_out, m_b_w_in, m_b_conv_w, m_b_conv_b, m_b_dt_bias, m_b_a_log, m_b_d_skip, m_b_norm_w, m_b_w_out, m_ffn_w_in, m_ffn_conv_w, m_ffn_conv_b, m_ffn_w_out, m_final_norm_w, v_norm1_w, v_norm2_w, v_a_w_in, v_a_lb_logits, v_a_norm_w, v_a_w_out, v_b_w_in, v_b_conv_w, v_b_conv_b, v_b_dt_bias, v_b_a_log, v_b_d_skip, v_b_norm_w, v_b_w_out, v_ffn_w_in, v_ffn_conv_w, v_ffn_conv_b, v_ffn_w_out, v_final_norm_w):
    w = dict(norm1_w=norm1_w, norm2_w=norm2_w, a_w_in=a_w_in, a_lb_logits=a_lb_logits, a_norm_w=a_norm_w, a_w_out=a_w_out,
             b_w_in=b_w_in, b_conv_w=b_conv_w, b_conv_b=b_conv_b, b_dt_bias=b_dt_bias, b_a_log=b_a_log, b_d_skip=b_d_skip,
             b_norm_w=b_norm_w, b_w_out=b_w_out, ffn_w_in=ffn_w_in, ffn_conv_w=ffn_conv_w, ffn_conv_b=ffn_conv_b,
             ffn_w_out=ffn_w_out, final_norm_w=final_norm_w)
    mom = dict(norm1_w=m_norm1_w, norm2_w=m_norm2_w, a_w_in=m_a_w_in, a_lb_logits=m_a_lb_logits, a_norm_w=m_a_norm_w,
               a_w_out=m_a_w_out, b_w_in=m_b_w_in, b_conv_w=m_b_conv_w, b_conv_b=m_b_conv_b, b_dt_bias=m_b_dt_bias,
               b_a_log=m_b_a_log, b_d_skip=m_b_d_skip, b_norm_w=m_b_norm_w, b_w_out=m_b_w_out, ffn_w_in=m_ffn_w_in,
               ffn_conv_w=m_ffn_conv_w, ffn_conv_b=m_ffn_conv_b, ffn_w_out=m_ffn_w_out, final_norm_w=m_final_norm_w)
    var = dict(norm1_w=v_norm1_w, norm2_w=v_norm2_w, a_w_in=v_a_w_in, a_lb_logits=v_a_lb_logits, a_norm_w=v_a_norm_w,
               a_w_out=v_a_w_out, b_w_in=v_b_w_in, b_conv_w=v_b_conv_w, b_conv_b=v_b_conv_b, b_dt_bias=v_b_dt_bias,
               b_a_log=v_b_a_log, b_d_skip=v_b_d_skip, b_norm_w=v_b_norm_w, b_w_out=v_b_w_out, ffn_w_in=v_ffn_w_in,
               ffn_conv_w=v_ffn_conv_w, ffn_conv_b=v_ffn_conv_b, ffn_w_out=v_ffn_w_out, final_norm_w=v_final_norm_w)
    chip = 2 * lax.axis_index("x") + lax.axis_index("y")
    place = jnp.stack([chip, lax.axis_index("c")]).astype(jnp.int32)

    sh = {"a_in": a_w_in[0], "a_out": a_w_out[0], "b_in": b_w_in[0], "b_out": b_w_out[0], "f0_in": ffn_w_in[0],
          "f1_in": ffn_w_in[1], "f0_out": ffn_w_out[0], "f1_out": ffn_w_out[1]}
    sh = {u: a.astype(BF) for u, a in sh.items()}
    small_shapes = [w[n].shape for n in _SMALL_SPLIT]
    spack, small_offs = _pack([w[n] for n in _SMALL_SPLIT], 128, 8, F32)
    sall = _allgather8("s_gather", spack)
    sshards = [_unpack(sall[2 * j], small_shapes, small_offs) for j in range(4)]
    sfull = {n: jnp.concatenate([sshards[j][i] for j in range(4)], axis=-1) for i, n in enumerate(_SMALL_SPLIT)}

    p = dict(
        norm1_w=norm1_w, norm2_w=norm2_w, a_lb_logits=a_lb_logits, a_norm_w=a_norm_w[0], final_norm_w=final_norm_w,
        b_conv_w=sfull["b_conv_w"][0], b_conv_b=sfull["b_conv_b"][0], b_norm_w=sfull["b_norm_w"],
        ffn_conv_w=sfull["ffn_conv_w"], ffn_conv_b=ffn_conv_b,
        b_dt_bias=jnp.pad(b_dt_bias.reshape(1, 2 * SSM_HEADS), ((0, 0), (0, 128 - 2 * SSM_HEADS))),
        b_a_log=jnp.pad(b_a_log.reshape(1, 2 * SSM_HEADS), ((0, 0), (0, 128 - 2 * SSM_HEADS))),
        b_d_skip=jnp.repeat(b_d_skip[0], SSM_HD)[None],
    )

    loss_row, dx, g, gs_ = _sequence_grads(x[0], loss_target[0], p, sh, place)
    grads = {"a_w_in": gs_["a_in"][None], "a_w_out": gs_["a_out"][None], "b_w_in": gs_["b_in"].T[None],
             "b_w_out": gs_["b_out"][None], "ffn_w_in": jnp.stack([gs_["f0_in"], gs_["f1_in"]]),
             "ffn_w_out": jnp.stack([gs_["f0_out"], gs_["f1_out"]])}

    gsmall = {
        "norm1_w": jnp.concatenate(g["norm1_w"], axis=0), "norm2_w": jnp.concatenate(g["norm2_w"], axis=0),
        "a_lb_logits": jnp.stack(g["a_lb_logits"]), "a_norm_w": g["a_norm_w"], "b_conv_w": g["b_conv_w"],
        "b_conv_b": g["b_conv_b"], "b_dt_bias": g["b_dt_bias"], "b_a_log": g["b_a_log"], "b_d_skip": g["b_d_skip"],
        "b_norm_w": g["b_norm_w"], "ffn_conv_w": jnp.stack(g["ffn_conv_w"]),
        "ffn_conv_b": jnp.concatenate(g["ffn_conv_b"], axis=0), "final_norm_w": g["final_norm_w"],
    }
    pieces = [gsmall[n] for n in _SMALL] + [loss_row]
    piece_shapes = [a.shape for a in pieces]
    gspack, gs_offs = _pack(pieces, 128, 8, F32)
    rows = gspack.shape[0]
    gsall = _allgather8("gs_gather", gspack)

    def sum8(a):
        r = a[0]
        for i in range(1, 8):
            r = r + a[i]
        return r

    gssum = _vcall("gs_sum", sum8, (1,), [(gsall, (8, rows, 128), lambda i: (0, 0, 0))],
                   [((rows, 128), F32, (rows, 128), lambda i: (0, 0))])
    gs = dict(zip(_SMALL + ("loss",), _unpack(gssum, piece_shapes, gs_offs)))
    loss = gs["loss"][0, 0]
    lb2 = gs["a_lb_logits"]
    small_grads = {
        "norm1_w": gs["norm1_w"], "norm2_w": gs["norm2_w"], "a_lb_logits": lb2[0] + lb2[1], "a_norm_w": gs["a_norm_w"],
        "b_dt_bias": gs["b_dt_bias"][:, :2 * SSM_HEADS].reshape(1, 2, SSM_HEADS),
        "b_a_log": gs["b_a_log"][:, :2 * SSM_HEADS].reshape(1, 2, SSM_HEADS),
        "b_d_skip": gs["b_d_skip"].reshape(1, SSM_HEADS, SSM_HD).sum(axis=-1),
        "ffn_conv_b": gs["ffn_conv_b"], "final_norm_w": gs["final_norm_w"][0],
        "b_conv_w": gs["b_conv_w"][None], "b_conv_b": gs["b_conv_b"], "b_norm_w": gs["b_norm_w"], "ffn_conv_w": gs["ffn_conv_w"],
    }
    for n in _SMALL_SPLIT:
        width = w[n].shape[-1]
        small_grads[n] = lax.dynamic_slice_in_dim(small_grads[n], chip * width, width, axis=small_grads[n].ndim - 1)
    grads.update(small_grads)

    delta, new_m, new_v = {}, {}, {}
    for n in _BIG:
        shp = w[n].shape
        two_d = (shp[0] * shp[1], shp[2])
        d_, m_, v_ = _adam(f"adam_{n}", w[n].reshape(two_d), grads[n].reshape(two_d), mom[n].reshape(two_d), var[n].reshape(two_d))
        delta[n], new_m[n], new_v[n] = d_.reshape(shp), m_.reshape(shp), v_.reshape(shp)
    s_shapes = [w[n].shape for n in _SMALL]
    packs = [_pack([src[n] for n in _SMALL], 128, 8, F32) for src in (w, grads, mom, var)]
    outs = _adam("adam_small", *[pk[0] for pk in packs])
    for res, dst in zip(outs, (delta, new_m, new_v)):
        dst.update(dict(zip(_SMALL, _unpack(res, s_shapes, packs[0][1]))))

    return (loss, dx[None], *[grads[n] for n in _ORDER], *[delta[n] for n in _ORDER],
            *[new_m[n] for n in _ORDER], *[new_v[n] for n in _ORDER])
```

```python
import functools

import jax
import jax.numpy as jnp
from jax import lax
from jax.experimental import pallas as pl
from jax.experimental.pallas import tpu as pltpu

F32, BF = jnp.float32, jnp.bfloat16
HI = lax.Precision.HIGHEST

D = 1024
EPS = 1e-6
HG_HEADS, HG_HD, HG_CHUNK, HG_SUB = 8, 128, 64, 16
HG_HB = 8
SSM_GB = 4
D_INNER, SSM_HEADS, SSM_HD, SSM_GROUPS, SSM_HPG, SSM_N, SSD_CHUNK = 2048, 32, 64, 8, 4, 128, 128
CONV_DIM = D_INNER + 2 * SSM_GROUPS * SSM_N
B_PROJ = 2 * D_INNER + 2 * SSM_GROUPS * SSM_N + 2 * SSM_HEADS
B_PROJ_PAD = 6272
D_FF = 2816
NEG = -1e30
ROWS = 256
VMEM_LIMIT = 56 * 1024 * 1024

ADAM_LR, ADAM_B1, ADAM_B2, ADAM_EPS, ADAM_WD, ADAM_STEP = 0.001, 0.9, 0.999, 1e-08, 0.01, 10

MESH = pl.DeviceIdType.MESH


def _pick(n, cands):
    for c in cands:
        if n % c == 0:
            return c
    return n


class _Side:
    def __init__(self, ins, outs, plan, n_remote, n_local, alias=None):
        self.ins, self.outs, self.plan, self.n_remote, self.n_local = list(ins), list(outs), plan, n_remote, n_local
        self.alias = alias or {}

    def copies(self, in_refs, out_refs, send_sems, recv_sems, local_sems):
        remote, local = self.plan(in_refs, out_refs)
        cps = [pltpu.make_async_copy(s, d, local_sems.at[i]) for i, (s, d) in enumerate(local)]
        cps += [pltpu.make_async_remote_copy(src_ref=s, dst_ref=d, send_sem=send_sems.at[i], recv_sem=recv_sems.at[i],
                                             device_id=dev, device_id_type=MESH)
                for i, (s, d, dev) in enumerate(remote)]
        return cps

    def sems(self):
        return [pltpu.SemaphoreType.DMA((self.n_remote,)), pltpu.SemaphoreType.DMA((self.n_remote,)),
                pltpu.SemaphoreType.DMA((max(self.n_local, 1),))]


def _vcall(name, fn, grid, ins, outs, acc=None, scratch=(), place=None, side=None):
    acc = acc or {}
    n_in, n_out, nd = len(ins), len(outs), len(grid)
    n_pre = 0 if place is None else 1
    n_sin = len(side.ins) if side else 0
    n_sout = len(side.outs) if side else 0
    n_scr = len(scratch)

    def body(*refs):
        refs = refs[n_pre:]
        in_refs, refs = refs[:n_in], refs[n_in:]
        sin_refs, refs = refs[:n_sin], refs[n_sin:]
        out_refs, refs = refs[:n_out], refs[n_out:]
        sout_refs, refs = refs[:n_sout], refs[n_sout:]
        scr, sems = refs[:n_scr], refs[n_scr:]
        if side:
            at_first, at_last = None, None
            for ax in range(nd):
                f, l = pl.program_id(ax) == 0, pl.program_id(ax) == grid[ax] - 1
                at_first = f if at_first is None else jnp.logical_and(at_first, f)
                at_last = l if at_last is None else jnp.logical_and(at_last, l)

            @pl.when(at_first)
            def _():
                for cp in side.copies(sin_refs, sout_refs, *sems):
                    cp.start()

        res = fn(*[r[...] for r in in_refs], *scr)
        if not isinstance(res, (tuple, list)):
            res = (res,)
        if side:
            @pl.when(at_last)
            def _():
                for cp in side.copies(sin_refs, sout_refs, *sems):
                    cp.wait()
        for j, (o_ref, r) in enumerate(zip(out_refs, res)):
            mode = acc.get(j)
            if mode is None:
                o_ref[...] = r.astype(o_ref.dtype)
                continue
            first = pl.program_id(nd - 1) == 0
            if mode == "all":
                for ax in range(nd - 1):
                    first = jnp.logical_and(first, pl.program_id(ax) == 0)

            @pl.when(first)
            def _():
                o_ref[...] = r.astype(o_ref.dtype)

            @pl.when(jnp.logical_not(first))
            def _():
                o_ref[...] += r.astype(o_ref.dtype)

    hbm = pl.BlockSpec(memory_space=pl.ANY)
    in_specs = [pl.BlockSpec(bs, im) for _, bs, im in ins] + [hbm] * n_sin
    out_specs = [pl.BlockSpec(bs, im) for _, _, bs, im in outs] + [hbm] * n_sout
    params = pltpu.CompilerParams(dimension_semantics=("arbitrary",) * nd, vmem_limit_bytes=VMEM_LIMIT)
    out_shape = [jax.ShapeDtypeStruct(s, dt) for s, dt, _, _ in outs]
    operands = [a for a, _, _ in ins]
    scratch = list(scratch)
    aliases = {}
    if side:
        out_shape += [jax.ShapeDtypeStruct(s, dt) for s, dt in side.outs]
        operands += side.ins
        scratch += side.sems()
        aliases = {n_pre + n_in + i: n_out + o for i, o in side.alias.items()}
    if place is None:
        out = pl.pallas_call(body, name=name, grid=grid, in_specs=in_specs, out_specs=out_specs, out_shape=out_shape,
                             scratch_shapes=scratch, compiler_params=params, input_output_aliases=aliases)(*operands)
    else:
        spec = pltpu.PrefetchScalarGridSpec(num_scalar_prefetch=1, grid=grid, in_specs=in_specs, out_specs=out_specs,
                                            scratch_shapes=scratch)
        out = pl.pallas_call(body, name=name, grid_spec=spec, out_shape=out_shape, compiler_params=params,
                             input_output_aliases=aliases)(place, *operands)
    if side:
        return tuple(out[:n_out]), tuple(out[n_out:])
    return out[0] if n_out == 1 else out


def _mm(name, a, b, kind, out_dtype=F32, add=None):
    if kind == "tn":
        m, k = a.shape
        _, n = b.shape
        tm = _pick(m, (1024, 512, 256))
        tk = _pick(k, (1024, 1408, 896, 512, 256, 128))
        tn = _pick(n, (1024, 1408, 896, 512, 256, 128))

        def fn(av, bv):
            return lax.dot_general(av.astype(BF), bv.astype(BF), (((0,), (0,)), ((), ())),
                                   preferred_element_type=F32)

        return _vcall(name, fn, (k // tk, n // tn, m // tm),
                      [(a, (tm, tk), lambda i, j, s: (s, i)), (b, (tm, tn), lambda i, j, s: (s, j))],
                      [((k, n), F32, (tk, tn), lambda i, j, s: (i, j))], acc={0: "last"})
    m, k = a.shape
    n = b.shape[1] if kind == "nn" else b.shape[0]
    long_k = k > 4096
    tm = _pick(m, (512, 256)) if long_k else _pick(m, (1024, 512, 256))
    tn = _pick(n, (512, 896, 256, 128)) if long_k else _pick(n, (1024, 1408, 896, 512, 256, 128))
    dims =(((1,), (0,)), ((), ())) if kind == "nn" else (((1,), (1,)), ((), ()))

    def fn(av, bv, *rest):
        r = lax.dot_general(av.astype(BF), bv.astype(BF), dims, preferred_element_type=F32)
        return r + rest[0] if rest else r

    ins = [(a, (tm, k), lambda i, j: (i, 0)),
           (b, (k, tn), lambda i, j: (0, j)) if kind == "nn" else (b, (tn, k), lambda i, j: (j, 0))]
    if add is not None:
        ins.append((add, (tm, tn), lambda i, j: (i, j)))
    return _vcall(name, fn, (m // tm, n // tn), ins, [((m, n), out_dtype, (tm, tn), lambda i, j: (i, j))])


def _rms(h, w):
    return h * lax.rsqrt(jnp.mean(h * h, axis=-1, keepdims=True) + EPS) * w


def _rms_fwd(name, h, w):
    L = h.shape[0]
    tb = _pick(L, (ROWS,))
    return _vcall(name, _rms, (L // tb,),
                  [(h, (tb, D), lambda i: (i, 0)), (w.reshape(1, D), (1, D), lambda i: (0, 0))],
                  [((L, D), BF, (tb, D), lambda i: (i, 0))])


def _rms_bwd(name, du, h, w, dh_next):
    L = h.shape[0]
    tb = _pick(L, (ROWS,))

    def fn(duv, hv, wv, dnv):
        _, vjp = jax.vjp(_rms, hv, wv)
        dh, dw = vjp(duv)
        return dh + dnv, dw

    row = lambda i: (i, 0)
    return _vcall(name, fn, (L // tb,),
                  [(du, (tb, D), row), (h, (tb, D), row), (w.reshape(1, D), (1, D), lambda i: (0, 0)),
                   (dh_next, (tb, D), row)],
                  [((L, D), F32, (tb, D), row), ((1, D), F32, (1, D), lambda i: (0, 0))], acc={1: "all"})


def _loss_head(name, h, tgt, w):
    L = h.shape[0]
    tb = _pick(L, (ROWS,))

    def lossf(hv, wv, tv):
        err = _rms(hv, wv) - tv
        return 0.5 * jnp.sum(err * err) * (1.0 / D)

    def fn(hv, wv, tv):
        val, vjp = jax.vjp(lambda a, b: lossf(a, b, tv), hv, wv)
        dh, dw = vjp(jnp.ones((), F32))
        return jnp.full((1, 128), val, F32), dh, dw

    row = lambda i: (i, 0)
    zero = lambda i: (0, 0)
    return _vcall(name, fn, (L // tb,),
                  [(h, (tb, D), row), (w.reshape(1, D), (1, D), zero), (tgt, (tb, D), row)],
                  [((1, 128), F32, (1, 128), zero), ((L, D), F32, (tb, D), row), ((1, D), F32, (1, D), zero)],
                  acc={0: "all", 2: "all"})


def _bf(x):
    return x.astype(BF)


def _dot(a, b, dims, precision=None):
    return lax.dot_general(a, b, (dims, ((), ())), preferred_element_type=F32, precision=precision)


def _tri(n, reverse):
    r = lax.broadcasted_iota(jnp.int32, (n, n), 0)
    c = lax.broadcasted_iota(jnp.int32, (n, n), 1)
    return (r <= c) if reverse else (r >= c)


def _tri_matmul(n, reverse, x):
    hi = x.astype(BF)
    r1 = x - hi.astype(F32)
    mid = r1.astype(BF)
    lo = (r1 - mid.astype(F32)).astype(BF)
    y = _dot(_tri(n, reverse).astype(BF), jnp.concatenate([hi, mid, lo], axis=1), ((1,), (0,)))
    w = x.shape[1]
    return (y[:, :w] + y[:, w:2 * w]) + y[:, 2 * w:]


@functools.partial(jax.custom_vjp, nondiff_argnums=(0, 1))
def _running_sum(n, reverse, x):
    return _tri_matmul(n, reverse, x)


def _running_sum_fwd(n, reverse, x):
    return _tri_matmul(n, reverse, x), None


def _running_sum_bwd(n, reverse, _, ct):
    return (_tri_matmul(n, not reverse, ct),)


_running_sum.defvjp(_running_sum_fwd, _running_sum_bwd)


def _gla_chunk(q_raw, f_raw, v, lb3, S, reverse):
    C, SB, HD = HG_CHUNK, HG_SUB, HG_HD
    H = S.shape[0]
    heads = [slice(HD * h, HD * (h + 1)) for h in range(H)]
    row3 = lax.broadcasted_iota(jnp.int32, (3, 1), 0)
    e = jnp.exp(lb3 - jnp.max(lb3, axis=0, keepdims=True))
    lb = jnp.sum(jnp.where(row3 == 0, e, 0.0), axis=0, keepdims=True) / jnp.sum(e, axis=0, keepdims=True)
    q = q_raw * jax.nn.sigmoid(q_raw)
    f = lb + (1.0 - lb) * jax.nn.sigmoid(f_raw)
    g = jnp.log(f)
    k = 1.0 - f
    b = _running_sum(C, reverse, g)
    row = lax.broadcasted_iota(jnp.int32, (C, 1), 0)
    vb = _bf(v)

    def rowof(x, t):
        return jnp.sum(jnp.where(row == t, x, 0.0), axis=0, keepdims=True)

    qe = _bf(q * jnp.exp(b))
    o = [_dot(qe[:, hs], _bf(S[h]), ((1,), (0,))) for h, hs in enumerate(heads)]
    att = [None] * H
    for i in range(C // SB):
        lo = SB * i
        if (not reverse and i == 0) or (reverse and i == C // SB - 1):
            continue
        first = lo + SB - 1 if reverse else lo
        r = rowof(b, first) - rowof(g, first)
        in_blk = jnp.logical_and(row >= lo, row < lo + SB)
        before = (row >= lo + SB) if reverse else (row < lo)
        qi = _bf(q * jnp.exp(jnp.where(in_blk, b - r, NEG)))
        kk = _bf(k * jnp.exp(jnp.where(before, r - b, NEG)))
        for h, hs in enumerate(heads):
            a_i = _dot(qi[:, hs], kk[:, hs], ((1,), (1,)))
            att[h] = a_i if att[h] is None else att[h] + a_i
    o = [o[h] + _dot(_bf(att[h]), vb[:, hs], ((1,), (0,))) for h, hs in enumerate(heads)]
    s_i = lax.broadcasted_iota(jnp.int32, (SB, SB, HD), 0)
    t_i = lax.broadcasted_iota(jnp.int32, (SB, SB, HD), 1)
    pair = (t_i <= s_i) if reverse else (t_i >= s_i)
    shp = (SB, SB, HD)
    diag = [[] for _ in range(H)]
    for i in range(C // SB):
        rows = slice(SB * i, SB * (i + 1))
        for h, hs in enumerate(heads):
            qb, kb, bb = q[rows, hs], k[rows, hs], b[rows, hs]
            dif = lax.broadcast_in_dim(bb, shp, (1, 2)) - lax.broadcast_in_dim(bb, shp, (0, 2))
            w = lax.broadcast_in_dim(qb, shp, (1, 2)) * jnp.exp(jnp.where(pair, dif, NEG)) * lax.broadcast_in_dim(kb, shp, (0, 2))
            d = jnp.sum(w, axis=2, keepdims=True)
            diag[h].append(jnp.sum(d * lax.broadcast_in_dim(v[rows, hs], shp, (0, 2)), axis=0))
    o = jnp.concatenate([o[h] + jnp.concatenate(diag[h], axis=0) for h in range(H)], axis=1)
    btot = rowof(b, 0 if reverse else C - 1)
    kd = _bf(k * jnp.exp(btot - b))
    eye = lax.broadcasted_iota(jnp.int32, (HD, HD), 0) == lax.broadcasted_iota(jnp.int32, (HD, HD), 1)
    s_new = []
    for h, hs in enumerate(heads):
        btot_col = jnp.sum(jnp.where(eye, btot[:, hs], 0.0), axis=1, keepdims=True)
        s_new.append((jnp.exp(btot_col) * S[h] + _dot(kd[:, hs], vb[:, hs], ((0,), (0,))))[None])
    return o, jnp.concatenate(s_new, axis=0)


def _gla_fwd(name, pa, lbl, reverse, side=None):
    L = pa.shape[0]
    C = HG_CHUNK
    nc = L // C
    cidx = (lambda i: nc - 1 - i) if reverse else (lambda i: i)
    sec = 2 if reverse else 1
    hb_n, nh = HG_HB, HG_HEADS // HG_HB

    def fn(qr, fr, v, lb3, s_ref):
        @pl.when(pl.program_id(1) == 0)
        def _():
            s_ref[...] = jnp.zeros_like(s_ref)

        s_all = s_ref[...]
        o, s_new = _gla_chunk(qr, fr, v, lb3, s_all, reverse)
        s_ref[...] = s_new
        return o, s_all[None]

    blk = (C, HG_HD * hb_n)
    return _vcall(name, fn, (nh, nc),
                  [(pa, blk, lambda h, i: (cidx(i), h)), (pa, blk, lambda h, i: (cidx(i), sec * nh + h)),
                   (pa, blk, lambda h, i: (cidx(i), 3 * nh + h)), (lbl, (3, HG_HD * hb_n), lambda h, i: (0, h))],
                  [((L, D), F32, blk, lambda h, i: (cidx(i), h)),
                   ((nc, HG_HEADS, HG_HD, HG_HD), F32, (1, hb_n, HG_HD, HG_HD), lambda h, i: (cidx(i), h, 0, 0))],
                  scratch=[pltpu.VMEM((hb_n, HG_HD, HG_HD), F32)], side=side)


def _gla_bwd(name, pa, lbl, s_in, do, reverse, prev=None, side=None):
    L = pa.shape[0]
    C = HG_CHUNK
    nc = L // C
    cidx = (lambda i: i) if reverse else (lambda i: nc - 1 - i)
    sec = 2 if reverse else 1
    n_prev = 0 if prev is None else 2
    hb_n, nh = HG_HB, HG_HEADS // HG_HB

    def fn(qr, fr, v, lb3, s, dov, *rest):
        ds_ref = rest[n_prev]

        @pl.when(pl.program_id(1) == 0)
        def _():
            ds_ref[...] = jnp.zeros_like(ds_ref)

        _, vjp = jax.vjp(lambda *a: _gla_chunk(*a, reverse), qr, fr, v, lb3, s[0])
        dq, df, dv, dlb, ds = vjp((dov, ds_ref[...]))
        ds_ref[...] = ds
        if n_prev:
            dq, dv = dq + rest[0], dv + rest[1]
        return dq, df, dv, dlb

    blk = (C, HG_HD * hb_n)
    at = lambda h, i: (cidx(i), h)
    ins = [(pa, blk, at), (pa, blk, lambda h, i: (cidx(i), sec * nh + h)), (pa, blk, lambda h, i: (cidx(i), 3 * nh + h)),
           (lbl, (3, HG_HD * hb_n), lambda h, i: (0, h)),
           (s_in, (1, hb_n, HG_HD, HG_HD), lambda h, i: (cidx(i), h, 0, 0)), (do, blk, at)]
    if prev is not None:
        ins += [(prev[0], blk, at), (prev[1], blk, at)]
    sum_dt = F32 if prev is None else BF
    return _vcall(name, fn, (nh, nc), ins,
                  [((L, D), sum_dt, blk, at), ((L, D), BF, blk, at), ((L, D), sum_dt, blk, at),
                   ((3, D), F32, (3, HG_HD * hb_n), lambda h, i: (0, h))],
                  acc={3: "last"}, scratch=[pltpu.VMEM((hb_n, HG_HD, HG_HD), F32)], side=side)


def _hgout(o_f, o_b, g, nw):
    o = o_f + o_b
    return _rms(o, nw) * (g * jax.nn.sigmoid(g))


def _hgout_fwd(name, o_f, o_b, pa, nw, side=None):
    L = o_f.shape[0]
    tb = _pick(L, (ROWS,))
    blk = (tb, HG_HD)
    at = lambda h, i: (i, h)
    return _vcall(name, _hgout, (HG_HEADS, L // tb),
                  [(o_f, blk, at), (o_b, blk, at), (pa, blk, lambda h, i: (i, 32 + h)),
                   (nw.reshape(1, HG_HD), (1, HG_HD), lambda h, i: (0, 0))],
                  [((L, D), BF, blk, at)], side=side)


def _hgout_bwd(name, o_f, o_b, pa, nw, dy):
    L = o_f.shape[0]
    tb = _pick(L, (ROWS,))

    def fn(ofv, obv, gv, nwv, dyv):
        _, vjp = jax.vjp(_hgout, ofv, obv, gv, nwv)
        do, _, dg, dnw = vjp(dyv)
        return do, dg, dnw

    blk = (tb, HG_HD)
    at = lambda h, i: (i, h)
    zero = lambda h, i: (0, 0)
    return _vcall(name, fn, (HG_HEADS, L // tb),
                  [(o_f, blk, at), (o_b, blk, at), (pa, blk, lambda h, i: (i, 32 + h)),
                   (nw.reshape(1, HG_HD), (1, HG_HD), zero), (dy, blk, at)],
                  [((L, D), F32, blk, at), ((L, D), BF, blk, at), ((1, HG_HD), F32, (1, HG_HD), zero)],
                  acc={2: "all"})


def _shift(x, s):
    if s == 0:
        return x
    n = x.shape[0]
    t = lax.broadcasted_iota(jnp.int32, (n, 1), 0)
    if s > 0:
        return jnp.where(t >= s, pltpu.roll(x, s, 0), 0.0)
    return jnp.where(t < n + s, pltpu.roll(x, n + s, 0), 0.0)


def _conv(x, w, b):
    kk = w.shape[0]
    p = (kk - 1) // 2
    y = b
    for j in range(kk):
        y = y + w[j:j + 1] * _shift(x, p - j)
    return y


def _conv_bwd(x, w, dc):
    kk = w.shape[0]
    p = (kk - 1) // 2
    dx = None
    dws = []
    for j in range(kk):
        t = w[j:j + 1] * _shift(dc, j - p)
        dx = t if dx is None else dx + t
        dws.append(jnp.sum(dc * _shift(x, p - j), axis=0, keepdims=True))
    rows = lax.broadcasted_iota(jnp.int32, (kk, 1), 0)
    dw = None
    for j in range(kk):
        t = jnp.where(rows == j, dws[j], 0.0)
        dw = t if dw is None else dw + t
    return dx, dw, jnp.sum(dc, axis=0, keepdims=True)


def _silu_grad(c):
    s = jax.nn.sigmoid(c)
    return s * (1.0 + c * (1.0 - s))


def _glu_fwd(name, pf, cw, cb):
    L = pf.shape[0]
    tc = 128
    nt = D_FF // tc

    def fn(gate, val, w, b):
        c = _conv(gate, w, b)
        return c * jax.nn.sigmoid(c) * val

    return _vcall(name, fn, (nt,),
                  [(pf, (L, tc), lambda j: (0, j)), (pf, (L, tc), lambda j: (0, nt + j)),
                   (cw, (3, tc), lambda j: (0, j)), (cb.reshape(1, D_FF), (1, tc), lambda j: (0, j))],
                  [((L, D_FF), BF, (L, tc), lambda j: (0, j))])


def _glu_bwd(name, pf, cw, cb, dy):
    L = pf.shape[0]
    tc = 128
    nt = D_FF // tc

    def fn(gate, val, w, b, dyv):
        c = _conv(gate, w, b)
        sc = c * jax.nn.sigmoid(c)
        dc = dyv * val * _silu_grad(c)
        dgate, dw, db = _conv_bwd(gate, w, dc)
        return dgate, dyv * sc, dw, db

    col = lambda j: (0, j)
    return _vcall(name, fn, (nt,),
                  [(pf, (L, tc), col), (pf, (L, tc), lambda j: (0, nt + j)), (cw, (3, tc), col),
                   (cb.reshape(1, D_FF), (1, tc), col), (dy, (L, tc), col)],
                  [((L, D_FF), BF, (L, tc), col), ((L, D_FF), BF, (L, tc), col),
                   ((3, D_FF), F32, (3, tc), col), ((1, D_FF), F32, (1, tc), col)])


def _perm_tile(j):
    return jnp.where(j < 16, 4 * (j // 2) + j % 2, jnp.where(j < 24, 4 * (j - 16) + 2, 4 * (j - 24) + 3))


def _mpre_fwd(name, pb, cw, cb):
    L = pb.shape[0]
    tc = 128

    def fn(x, w, b):
        c = _conv(x, w, b)
        return c * jax.nn.sigmoid(c)

    return _vcall(name, fn, (CONV_DIM // tc,),
                  [(pb, (L, tc), lambda j: (0, 16 + j)), (cw, (5, tc), lambda j: (0, j)),
                   (cb.reshape(1, CONV_DIM), (1, tc), lambda j: (0, j))],
                  [((L, CONV_DIM), F32, (L, tc), lambda j: (0, _perm_tile(j)))])


def _mpre_bwd(name, pb, cw, cb, dact):
    L = pb.shape[0]
    tc = 128

    def fn(x, w, b, da):
        c = _conv(x, w, b)
        return _conv_bwd(x, w, da * _silu_grad(c))

    col = lambda j: (0, j)
    return _vcall(name, fn, (CONV_DIM // tc,),
                  [(pb, (L, tc), lambda j: (0, 16 + j)), (cw, (5, tc), col), (cb.reshape(1, CONV_DIM), (1, tc), col),
                   (dact, (L, tc), lambda j: (0, _perm_tile(j)))],
                  [((L, CONV_DIM), BF, (L, tc), col), ((5, CONV_DIM), F32, (5, tc), col),
                   ((1, CONV_DIM), F32, (1, tc), col)])


def _softplus(x):
    return jnp.maximum(x, 0.0) + jnp.log(1.0 + jnp.exp(-jnp.abs(x)))


def _dt_fwd(name, pb, dtb, alog):
    L = pb.shape[0]
    tb = _pick(L, (1024, ROWS))

    def fn(x, bias, al):
        dt = _softplus(x + bias)
        return dt, dt * (-jnp.exp(al))

    row = lambda i: (i, 0)
    zero = lambda i: (0, 0)
    return _vcall(name, fn, (L // tb,),
                  [(pb, (tb, 128), lambda i: (i, 48)), (dtb, (1, 128), zero), (alog, (1, 128), zero)],
                  [((L, 128), F32, (tb, 128), row), ((L, 128), F32, (tb, 128), row)])


def _dt_bwd(name, pb, dtb, alog, ddt_f, dla_f, ddt_b, dla_b):
    L = pb.shape[0]
    tb = _pick(L, (1024, ROWS))

    def fn(x, bias, al, a1, b1, a2, b2):
        ddt = jnp.sum(a1, axis=0) + jnp.sum(a2, axis=0)
        dla = jnp.sum(b1, axis=0) + jnp.sum(b2, axis=0)
        z = x + bias
        dt = _softplus(z)
        a = -jnp.exp(al)
        dz = (ddt + dla * a) * jax.nn.sigmoid(z)
        return dz, jnp.sum(dz, axis=0, keepdims=True), jnp.sum(dla * dt, axis=0, keepdims=True) * a

    zero = lambda i: (0, 0)
    g3 = (ddt_f.shape[0], tb, 128)
    at3 = lambda i: (0, i, 0)
    return _vcall(name, fn, (L // tb,),
                  [(pb, (tb, 128), lambda i: (i, 48)), (dtb, (1, 128), zero), (alog, (1, 128), zero),
                   (ddt_f, g3, at3), (dla_f, g3, at3), (ddt_b, g3, at3), (dla_b, g3, at3)],
                  [((L, 128), BF, (tb, 128), lambda i: (i, 0)), ((1, 128), F32, (1, 128), zero),
                   ((1, 128), F32, (1, 128), zero)], acc={1: "all", 2: "all"})


def _split_dot(x, e, dims, pieces):
    hi = x.astype(BF)
    r1 = x - hi.astype(F32)
    mid = r1.astype(BF)
    y = _dot(hi, e, dims) + _dot(mid, e, dims)
    if pieces == 3:
        y = y + _dot((r1 - mid.astype(F32)).astype(BF), e, dims)
    return y


@functools.partial(jax.custom_vjp, nondiff_argnums=(2,))
def _spread(x, e, pieces):
    return _split_dot(x, e, ((1,), (0,)), pieces)


def _spread_fwd(x, e, pieces):
    return _split_dot(x, e, ((1,), (0,)), pieces), e


def _spread_bwd(pieces, e, ct):
    return _split_dot(ct, e, ((1,), (1,)), pieces), jnp.zeros_like(e)


_spread.defvjp(_spread_fwd, _spread_bwd)


def _ssd_chunk(xa, dt, la, hs, head0, reverse):
    C, P4, HD, N = SSD_CHUNK, SSM_HPG * SSM_HD, SSM_HD, SSM_N
    G = hs.shape[0]
    nh = SSM_HPG * G
    row = lax.broadcasted_iota(jnp.int32, (C, 1), 0)
    lane = lax.broadcasted_iota(jnp.int32, (1, 128), 1)
    eye = lax.broadcasted_iota(jnp.int32, (C, C), 0) == lax.broadcasted_iota(jnp.int32, (C, C), 1)
    tri = _tri(C, reverse)
    last = 0 if reverse else C - 1
    acum = _running_sum(C, reverse, la)
    atot = jnp.sum(jnp.where(row == last, acum, 0.0), axis=0, keepdims=True)
    src = lax.broadcasted_iota(jnp.int32, (128, 1), 0) - head0
    to_x = (src == lax.broadcasted_iota(jnp.int32, (1, nh * HD), 1) // HD).astype(BF)
    to_c = (src == lax.broadcasted_iota(jnp.int32, (1, nh * C), 1) // C).astype(BF)
    dt_x = _spread(dt, to_x, 2)
    ea_x = _spread(jnp.exp(acum), to_x, 2)
    dec_x = _spread(jnp.exp(atot - acum), to_x, 2)
    ac_c = _spread(acum, to_c, 3)
    col_head = lax.broadcasted_iota(jnp.int32, (1, P4), 1) // HD
    row_head = lax.broadcasted_iota(jnp.int32, (P4, 1), 0) // HD
    ys, news = [], []
    for gi in range(G):
        xs = xa[:, 512 * gi:512 * gi + P4]
        bm = _bf(xa[:, 512 * gi + P4:512 * gi + P4 + N])
        cm = _bf(xa[:, 512 * gi + P4 + N:512 * (gi + 1)])
        gx = slice(P4 * gi, P4 * (gi + 1))
        cb = _dot(cm, bm, ((1,), (1,)))
        xd = xs * dt_x[:, gx]
        ms, xds, scale = [], [], 0.0
        for j in range(SSM_HPG):
            i = SSM_HPG * gi + j
            ac = ac_c[:, C * i:C * (i + 1)]
            ac_row = jnp.sum(jnp.where(eye, ac, 0.0), axis=0, keepdims=True)
            ms.append(_bf(cb * jnp.exp(jnp.where(tri, ac - ac_row, NEG))))
            xds.append(_bf(jnp.where(col_head == j, xd, 0.0)))
            a_i = jnp.sum(jnp.where(lane == head0 + i, atot, 0.0), axis=1, keepdims=True)
            scale = scale + jnp.where(row_head == j, jnp.exp(a_i), 0.0)
        y = _dot(jnp.concatenate(ms, axis=1), jnp.concatenate(xds, axis=0), ((1,), (0,)))
        y = y + _dot(cm, _bf(hs[gi]), ((1,), (1,))) * ea_x[:, gx]
        ys.append(y)
        news.append((scale * hs[gi] + _dot(_bf(xd * dec_x[:, gx]), bm, ((0,), (0,))))[None])
    return jnp.concatenate(ys, axis=1), jnp.concatenate(news, axis=0)


def _ssd_fwd(name, xact, dt, la, reverse):
    L = xact.shape[0]
    C = SSD_CHUNK
    nc = L // C
    cidx = (lambda i: nc - 1 - i) if reverse else (lambda i: i)
    base = SSM_HEADS if reverse else 0
    P4 = SSM_HPG * SSM_HD

    gb_n = SSM_GB

    def fn(xa, dtv, lav, h_ref):
        @pl.when(pl.program_id(1) == 0)
        def _():
            h_ref[...] = jnp.zeros_like(h_ref)

        h_all = h_ref[...]
        y, h_new = _ssd_chunk(xa, dtv, lav, h_all, base + SSM_HPG * gb_n * pl.program_id(0), reverse)
        h_ref[...] = h_new
        return y, h_all[None]

    return _vcall(name, fn, (SSM_GROUPS // gb_n, nc),
                  [(xact, (C, 512 * gb_n), lambda g, i: (cidx(i), g)), (dt, (C, 128), lambda g, i: (cidx(i), 0)),
                   (la, (C, 128), lambda g, i: (cidx(i), 0))],
                  [((L, D_INNER), F32, (C, P4 * gb_n), lambda g, i: (cidx(i), g)),
                   ((nc, SSM_GROUPS, P4, SSM_N), F32, (1, gb_n, P4, SSM_N), lambda g, i: (cidx(i), g, 0, 0))],
                  scratch=[pltpu.VMEM((gb_n, P4, SSM_N), F32)])


def _ssd_bwd(name, xact, dt, la, h_in, dy, reverse, prev_xs=None, prev_all=None):
    L = xact.shape[0]
    C = SSD_CHUNK
    nc = L // C
    cidx = (lambda i: i) if reverse else (lambda i: nc - 1 - i)
    base = SSM_HEADS if reverse else 0
    P4 = SSM_HPG * SSM_HD

    gb_n = SSM_GB

    def fn(xa, dtv, lav, hs, dyv, pv, dh_ref):
        @pl.when(pl.program_id(1) == 0)
        def _():
            dh_ref[...] = jnp.zeros_like(dh_ref)

        head0 = base + SSM_HPG * gb_n * pl.program_id(0)
        _, vjp = jax.vjp(lambda a, b, c, d: _ssd_chunk(a, b, c, d, head0, reverse), xa, dtv, lav, hs[0])
        dxa, ddt, dla, dh = vjp((dyv, dh_ref[...]))
        dh_ref[...] = dh
        if prev_all is not None:
            dxa = dxa + pv
        else:
            zeros = jnp.zeros((C, 2 * SSM_N), F32)
            dxa = dxa + jnp.concatenate([t for gb in range(gb_n) for t in (pv[:, P4 * gb:P4 * (gb + 1)], zeros)], axis=1)
        return dxa, ddt[None], dla[None]

    at = lambda g, i: (cidx(i), g)
    at0 = lambda g, i: (cidx(i), 0)
    pv = (prev_all, (C, 512 * gb_n), at) if prev_all is not None else (prev_xs, (C, P4 * gb_n), at)
    steps = SSM_GROUPS // gb_n
    return _vcall(name, fn, (steps, nc),
                  [(xact, (C, 512 * gb_n), at), (dt, (C, 128), at0), (la, (C, 128), at0),
                   (h_in, (1, gb_n, P4, SSM_N), lambda g, i: (cidx(i), g, 0, 0)), (dy, (C, P4 * gb_n), at), pv],
                  [((L, CONV_DIM), F32, (C, 512 * gb_n), at),
                   ((steps, L, 128), F32, (1, C, 128), lambda g, i: (g, cidx(i), 0)),
                   ((steps, L, 128), F32, (1, C, 128), lambda g, i: (g, cidx(i), 0))],
                  scratch=[pltpu.VMEM((gb_n, P4, SSM_N), F32)])


def _mpost(y_f, y_b, xs, z, dsk, nw):
    y = (y_f + y_b + xs * dsk) * (z * jax.nn.sigmoid(z))
    return _rms(y, nw)


def _mpost_fwd(name, y_f, y_b, xact, pb, dsk, nw):
    L = y_f.shape[0]
    tb = _pick(L, (ROWS,))
    blk = (tb, 256)
    at = lambda g, i: (i, g)
    par = lambda g, i: (0, g)
    return _vcall(name, _mpost, (SSM_GROUPS, L // tb),
                  [(y_f, blk, at), (y_b, blk, at), (xact, blk, lambda g, i: (i, 2 * g)), (pb, blk, at),
                   (dsk, (1, 256), par), (nw, (1, 256), par)],
                  [((L, D_INNER), BF, blk, at)])


def _mpost_bwd(name, y_f, y_b, xact, pb, dsk, nw, dy):
    L = y_f.shape[0]
    tb = _pick(L, (ROWS,))

    def fn(yf, yb, xs, z, dskv, nwv, dyv):
        _, vjp = jax.vjp(_mpost, yf, yb, xs, z, dskv, nwv)
        dyf, _, dxs, dz, ddsk, dnw = vjp(dyv)
        return dyf, dxs, dz, ddsk, dnw

    blk = (tb, 256)
    at = lambda g, i: (i, g)
    par = lambda g, i: (0, g)
    return _vcall(name, fn, (SSM_GROUPS, L // tb),
                  [(y_f, blk, at), (y_b, blk, at), (xact, blk, lambda g, i: (i, 2 * g)), (pb, blk, at),
                   (dsk, (1, 256), par), (nw, (1, 256), par), (dy, blk, at)],
                  [((L, D_INNER), F32, blk, at), ((L, D_INNER), F32, blk, at), ((L, D_INNER), BF, blk, at),
                   ((1, D_INNER), F32, (1, 256), par), ((1, D_INNER), F32, (1, 256), par)],
                  acc={3: "last", 4: "last"})


def _ffn_fwd(tag, h, nw, w_in, cw, cb, w_out):
    u = _rms_fwd(f"{tag}_norm", h, nw)
    pf = _mm(f"{tag}_in", u, w_in, "nn")
    yf = _glu_fwd(f"{tag}_glu", pf, cw, cb)
    return _mm(f"{tag}_out", yf, w_out, "nn", add=h), (u, pf, yf)


def _ffn_bwd(tag, h, nw, w_in, cw, cb, w_out, saved, dh):
    u, pf, yf = saved
    d_w_out = _mm(f"{tag}_dwout", yf, dh, "tn")
    dyf = _mm(f"{tag}_dy", dh, w_out, "nt")
    dgate, dval, dcw, dcb = _glu_bwd(f"{tag}_dglu", pf, cw, cb, dyf)
    dpf = jnp.concatenate([dgate, dval], axis=1)
    d_w_in = _mm(f"{tag}_dwin", u, dpf, "tn")
    du = _mm(f"{tag}_du", dpf, w_in, "nt")
    dh_in, dnw = _rms_bwd(f"{tag}_dnorm", du, h, nw, dh)
    return dh_in, dnw, d_w_in, dcw, dcb, d_w_out


def _sequence_grads(x, tgt, p, sh, place):
    g = {}
    lbl = p["a_lb_logits"]
    first, mid, last = ("a_in", "a_out"), ("f0_in", "f0_out", "b_in", "b_out"), ("f1_in", "f1_out")
    W = {}
    got = _exchange("w_first", _gather_side(first, (), sh, W))
    W.update(zip(first, got))
    got = _exchange("w_first_pass", _gather_side((), first, sh, W))
    W.update(zip(first, got))
    u1 = _rms_fwd("a_norm", x, p["norm1_w"][0])
    pa = _mm("a_in", u1, W["a_in"], "nn")
    (o_f, s_f), got = _gla_fwd("a_scan_f", pa, lbl, False, side=_gather_side(mid, (), sh, W))
    W.update(zip(mid, got))
    (o_b, s_b), got = _gla_fwd("a_scan_b", pa, lbl, True, side=_gather_side(last, mid, sh, W))
    W.update(zip(last + mid, got))
    (ya,), got = _hgout_fwd("a_gate", o_f, o_b, pa, p["a_norm_w"], side=_gather_side((), last, sh, W))
    W.update(zip(last, got))
    wb4 = W["b_in"].reshape(4, D, B_PROJ // 4)
    p = dict(p, a_w_in=W["a_in"], a_w_out=W["a_out"], b_w_out=W["b_out"], ffn_w_in=(W["f0_in"], W["f1_in"]),
             ffn_w_out=(W["f0_out"], W["f1_out"]),
             b_w_in=jnp.pad(jnp.concatenate([wb4[j] for j in range(4)], axis=1), ((0, 0), (0, B_PROJ_PAD - B_PROJ))))
    h1 = _mm("a_out", ya, p["a_w_out"], "nn", add=x)
    h2, ffn0 = _ffn_fwd("f0", h1, p["norm2_w"][0], p["ffn_w_in"][0], p["ffn_conv_w"][0], p["ffn_conv_b"][0], p["ffn_w_out"][0])
    u3 = _rms_fwd("b_norm", h2, p["norm1_w"][1])
    pb = _mm("b_in", u3, p["b_w_in"], "nn")
    xact = _mpre_fwd("b_conv", pb, p["b_conv_w"], p["b_conv_b"])
    dt, la = _dt_fwd("b_dt", pb, p["b_dt_bias"], p["b_a_log"])
    y_f, hs_f = _ssd_fwd("b_scan_f", xact, dt, la, False)
    y_b, hs_b = _ssd_fwd("b_scan_b", xact, dt, la, True)
    yb = _mpost_fwd("b_gate", y_f, y_b, xact, pb, p["b_d_skip"], p["b_norm_w"])
    h3 = _mm("b_out", yb, p["b_w_out"], "nn", add=h2)
    h4, ffn1 = _ffn_fwd("f1", h3, p["norm2_w"][1], p["ffn_w_in"][1], p["ffn_conv_w"][1], p["ffn_conv_b"][1], p["ffn_w_out"][1])
    loss, dh4, g["final_norm_w"] = _loss_head("head", h4, tgt, p["final_norm_w"])
    dh3, dn2_1, dwin1, dcw1, dcb1, dwout1 = _ffn_bwd("f1", h3, p["norm2_w"][1], p["ffn_w_in"][1], p["ffn_conv_w"][1],
                                                     p["ffn_conv_b"][1], p["ffn_w_out"][1], ffn1, dh4)
    G = {"f1_in": dwin1, "f1_out": dwout1}
    G["b_out"] = _mm("b_dwout", yb, dh3, "tn")
    dyb = _mm("b_dy", dh3, p["b_w_out"], "nt")
    dys, dxs, dz, g["b_d_skip"], g["b_norm_w"] = _mpost_bwd("b_dgate", y_f, y_b, xact, pb, p["b_d_skip"], p["b_norm_w"], dyb)
    dxa1, ddt_f, dla_f = _ssd_bwd("b_dscan_f", xact, dt, la, hs_f, dys, False, prev_xs=dxs)
    dxa, ddt_b, dla_b = _ssd_bwd("b_dscan_b", xact, dt, la, hs_b, dys, True, prev_all=dxa1)
    dxbc, g["b_conv_w"], g["b_conv_b"] = _mpre_bwd("b_dconv", pb, p["b_conv_w"], p["b_conv_b"], dxa)
    ddtr, g["b_dt_bias"], g["b_a_log"] = _dt_bwd("b_ddt", pb, p["b_dt_bias"], p["b_a_log"], ddt_f, dla_f, ddt_b, dla_b)
    dpb = jnp.concatenate([dz, dxbc, ddtr], axis=1)
    G["b_in"] = _mm("b_dwin", dpb, u3, "tn")
    du3 = _mm("b_du", dpb, p["b_w_in"], "nt")
    dh2, dn1_1 = _rms_bwd("b_dnorm", du3, h2, p["norm1_w"][1], dh3)
    dh1, dn2_0, G["f0_in"], dcw0, dcb0, G["f0_out"] = _ffn_bwd("f0", h1, p["norm2_w"][0], p["ffn_w_in"][0], p["ffn_conv_w"][0],
                                                              p["ffn_conv_b"][0], p["ffn_w_out"][0], ffn0, dh2)
    late = last + mid
    chip_sums = _pair_reduce("gl", late, G, place)
    G["a_out"] = _mm("a_dwout", ya, dh1, "tn")
    dya = _mm("a_dy", dh1, p["a_w_out"], "nt")
    do, dg, g["a_norm_w"] = _hgout_bwd("a_dgate", o_f, o_b, pa, p["a_norm_w"], dya)
    (dq1, df1, dv1, dl1), got = _gla_bwd("a_dscan_f", pa, lbl, s_f, do, False, side=_chips_side(late, chip_sums))
    shards = {u: _chip_sum(f"gl_sum_{u}", _GGEO[u], chip_sums[u], r, place) for u, r in zip(late, got)}
    (dq, df2, dv, dl2), got = _gla_bwd("a_dscan_b", pa, lbl, s_b, do, True, prev=(dq1, dv1), side=_halves_side(late, shards))
    shards = dict(zip(late, got))
    dpa = jnp.concatenate([dq, df1, df2, dv, dg], axis=1)
    G["a_in"] = _mm("a_dwin", u1, dpa, "tn")
    du1 = _mm("a_du", dpa, p["a_w_in"], "nt")
    dx, dn1_0 = _rms_bwd("a_dnorm", du1, x, p["norm1_w"][0], dh1)
    chip_sums = _pair_reduce("ga", first, G, place)
    got = _exchange("ga_chips", _chips_side(first, chip_sums))
    mine = {u: _chip_sum(f"ga_sum_{u}", _GGEO[u], chip_sums[u], r, place) for u, r in zip(first, got)}
    shards.update(zip(first, _exchange("ga_halves", _halves_side(first, mine))))
    g["a_lb_logits"] = (dl1, dl2)
    g["norm1_w"] = (dn1_0, dn1_1)
    g["norm2_w"] = (dn2_0, dn2_1)
    g["ffn_conv_w"] = (dcw0, dcw1)
    g["ffn_conv_b"] = (dcb0, dcb1)
    return loss, dx, g, shards


def _here():
    return lax.axis_index("x"), lax.axis_index("y"), lax.axis_index("c")


def _allgather8(name, src, by_core=False):
    blk = src.shape[1:] if by_core else src.shape

    def body(x_ref, out_ref, send_sems, recv_sems, local_sem):
        x, y, c = _here()
        me, sibling = (x, y, c), (x, y, 1 - c)
        chips = [(1 - x, y), (x, 1 - y), (1 - x, 1 - y)]
        own = x_ref.at[c] if by_core else x_ref

        def slot(px, py, pc):
            return out_ref.at[4 * px + 2 * py + pc]

        def copy(k, block, to, from_own=False):
            return pltpu.make_async_remote_copy(
                src_ref=own if from_own else slot(*block), dst_ref=slot(*block),
                send_sem=send_sems.at[k], recv_sem=recv_sems.at[k], device_id=to, device_id_type=MESH)

        mine = pltpu.make_async_copy(own, slot(*me), local_sem)
        mine.start()
        first = [copy(0, me, sibling, from_own=True)]
        first += [copy(1 + j, me, (*chip, c), from_own=True) for j, chip in enumerate(chips)]
        for cp in first:
            cp.start()
        passed = [copy(4 + j, (*chip, c), sibling) for j, chip in enumerate(chips)]
        for j, chip in enumerate(chips):
            copy(1 + j, (*chip, c), me).wait_recv()
            passed[j].start()
        copy(0, sibling, me).wait_recv()
        for j, chip in enumerate(chips):
            copy(4 + j, (*chip, 1 - c), me).wait_recv()
        for cp in first + passed:
            cp.wait_send()
        mine.wait()

    return pl.pallas_call(
        body, name=name,
        out_shape=jax.ShapeDtypeStruct((8,) + tuple(blk), src.dtype),
        in_specs=[pl.BlockSpec(memory_space=pl.ANY)],
        out_specs=pl.BlockSpec(memory_space=pl.ANY),
        scratch_shapes=[pltpu.SemaphoreType.DMA((7,)), pltpu.SemaphoreType.DMA((7,)), pltpu.SemaphoreType.DMA],
    )(src)


def _exchange(name, side):
    n_i, n_o = len(side.ins), len(side.outs)

    def body(*refs):
        copies = side.copies(refs[:n_i], refs[n_i:n_i + n_o], *refs[n_i + n_o:])
        for cp in copies:
            cp.start()
        for cp in copies:
            cp.wait()

    return pl.pallas_call(
        body, name=name,
        out_shape=[jax.ShapeDtypeStruct(s, dt) for s, dt in side.outs],
        in_specs=[pl.BlockSpec(memory_space=pl.ANY)] * n_i,
        out_specs=[pl.BlockSpec(memory_space=pl.ANY)] * n_o,
        scratch_shapes=side.sems(),
        input_output_aliases=dict(side.alias),
    )(*side.ins)


_WGEO = {"a_in": ("col", 1024, 1280), "a_out": ("row", 256, 1024), "b_in": ("row", 1024, 1552), "b_out": ("row", 512, 1024),
         "f0_in": ("col", 1024, 1408), "f1_in": ("col", 1024, 1408), "f0_out": ("row", 704, 1024), "f1_out": ("row", 704, 1024)}
_GGEO = dict(_WGEO, b_in=("row", 1552, 1024))


def _full_shape(geo):
    kind, r, cw = geo
    return (r, 4 * cw) if kind == "col" else (4 * r, cw)


def _times(i, step):
    return i * step if isinstance(i, int) else pl.multiple_of(i * step, step & -step)


def _win(ref, geo, j, h):
    kind, r, cw = geo
    hr = r // 2
    if kind == "col":
        return ref.at[pl.ds(_times(h, hr), hr), pl.ds(_times(j, cw), cw)]
    return ref.at[pl.ds(_times(2 * j + h, hr), hr), :]


def _half(ref, geo, h):
    hr = geo[1] // 2
    return ref.at[pl.ds(_times(h, hr), hr), :]


def _gather_side(first, second, sh, full):
    n1 = len(first)

    def plan(ins, outs):
        x, y, c = _here()
        m = 2 * x + y
        remote, local = [], []
        for u, src, dst_full in zip(first, ins[:n1], outs[:n1]):
            mine, dst = _half(src, _WGEO[u], c), _win(dst_full, _WGEO[u], m, c)
            local.append((mine, dst))
            remote.append((mine, dst, (x, y, 1 - c)))
            for k in (1, 2, 3):
                t = (m + k) % 4
                remote.append((mine, dst, (t // 2, t % 2, c)))
        for u, buf in zip(second, outs[n1:]):
            for k in (1, 2, 3):
                w_ = _win(buf, _WGEO[u], (m + k) % 4, c)
                remote.append((w_, w_, (x, y, 1 - c)))
        return remote, local

    return _Side([sh[u] for u in first] + [full[u] for u in second],
                 [(_full_shape(_WGEO[u]), BF) for u in first + second], plan, 4 * n1 + 3 * len(second), n1,
                 alias={n1 + i: n1 + i for i in range(len(second))})


def _pair_reduce(tag, units, G, place):
    def plan(ins, outs):
        x, y, c = _here()
        return [(_win(gr, _GGEO[u], j, 1 - c), got.at[j], (x, y, 1 - c))
                for u, gr, got in zip(units, ins, outs) for j in range(4)], []

    halves = [(4, _GGEO[u][1] // 2, _GGEO[u][2]) for u in units]
    gots = _exchange(f"{tag}_pair", _Side([G[u] for u in units], [(s, F32) for s in halves], plan, 4 * len(units), 0))
    out = {}
    for u, got, shp in zip(units, gots, halves):
        blk = shp[1:]
        at = (lambda j: (lax.axis_index("c"), j)) if _GGEO[u][0] == "col" else (lambda j: (2 * j + lax.axis_index("c"), 0))
        slab = lambda j: (j, 0, 0)
        out[u] = _vcall(f"{tag}_pair_sum_{u}", lambda a, b: (a + b[0])[None], (4,),
                        [(G[u], blk, at), (got, (1,) + blk, slab)], [(shp, BF, (1,) + blk, slab)])
    return out


def _chips_side(units, chip_sums):
    def plan(ins, outs):
        x, y, c = _here()
        m = 2 * x + y
        remote = []
        for s, got in zip(ins, outs):
            for k in (1, 2, 3):
                t = (m + k) % 4
                remote.append((s.at[t], got.at[k - 1], (t // 2, t % 2, c)))
        return remote, []

    return _Side([chip_sums[u] for u in units], [((3,) + chip_sums[u].shape[1:], BF) for u in units], plan,
                 3 * len(units), 0)


def _chip_sum(name, geo, chip_sums, got, place):
    _, r, cw = geo
    blk = (r // 2, cw)
    return _vcall(name, lambda a, b: ((a[0].astype(F32) + b[0].astype(F32)) + b[1].astype(F32)) + b[2].astype(F32), (1,),
                  [(chip_sums, (1,) + blk, lambda i: (2 * lax.axis_index("x") + lax.axis_index("y"), 0, 0)),
                   (got, (3,) + blk, lambda i: (0, 0, 0))],
                  [((r, cw), F32, blk, lambda i: (lax.axis_index("c"), 0))])


def _halves_side(units, shards):
    def plan(ins, outs):
        x, y, c = _here()
        return [(_half(o, _GGEO[u], c), _half(o, _GGEO[u], c), (x, y, 1 - c)) for u, o in zip(units, outs)], []

    return _Side([shards[u] for u in units], [(shards[u].shape, F32) for u in units], plan, len(units), 0,
                 alias={i: i for i in range(len(units))})


def _adam(name, w, g, m, v):
    rows, cols = w.shape
    tb = _pick(rows, (256, 128, 64, 8))

    def fn(wv, gv, mv, vv):
        m2 = ADAM_B1 * mv + (1.0 - ADAM_B1) * gv
        v2 = ADAM_B2 * vv + (1.0 - ADAM_B2) * jnp.square(gv)
        m_hat = m2 / (1.0 - ADAM_B1 ** ADAM_STEP)
        v_hat = v2 / (1.0 - ADAM_B2 ** ADAM_STEP)
        return -ADAM_LR * (m_hat / (jnp.sqrt(v_hat) + ADAM_EPS) + ADAM_WD * wv), m2, v2

    at = lambda i: (i, 0)
    return _vcall(name, fn, (rows // tb,), [(a, (tb, cols), at) for a in (w, g, m, v)],
                  [((rows, cols), F32, (tb, cols), at)] * 3)


def _pack(arrays, width, row_multiple, dtype):
    parts, offs, at = [], [], 0
    for a in arrays:
        flat = a.reshape(-1).astype(dtype)
        rows = -(-flat.shape[0] // (width * row_multiple)) * row_multiple
        parts.append(jnp.pad(flat, (0, rows * width - flat.shape[0])).reshape(rows, width))
        offs.append(at)
        at += rows
    return jnp.concatenate(parts, axis=0), offs


def _unpack(flat, shapes, offs):
    out = []
    for shp, at in zip(shapes, offs):
        n = 1
        for s in shp:
            n *= s
        rows = -(-n // flat.shape[1])
        out.append(flat[at:at + rows].reshape(-1)[:n].reshape(shp))
    return out


_BIG = ("a_w_in", "a_w_out", "b_w_in", "b_w_out", "ffn_w_in", "ffn_w_out")
_BIG_AXIS = {"a_w_in": 2, "a_w_out": 1, "b_w_in": 2, "b_w_out": 1, "ffn_w_in": 2, "ffn_w_out": 1}
_SMALL_SPLIT = ("b_conv_w", "b_conv_b", "b_norm_w", "ffn_conv_w")
_SMALL = ("norm1_w", "norm2_w", "a_lb_logits", "a_norm_w", "b_conv_w", "b_conv_b", "b_dt_bias", "b_a_log", "b_d_skip",
          "b_norm_w", "ffn_conv_w", "ffn_conv_b", "final_norm_w")
_ORDER = ("norm1_w", "norm2_w", "a_w_in", "a_lb_logits", "a_norm_w", "a_w_out", "b_w_in", "b_conv_w", "b_conv_b", "b_dt_bias",
          "b_a_log", "b_d_skip", "b_norm_w", "b_w_out", "ffn_w_in", "ffn_conv_w", "ffn_conv_b", "ffn_w_out", "final_norm_w")


def kernel(x, norm1_w, norm2_w, a_w_in, a_lb_logits, a_norm_w, a_w_out, b_w_in, b_conv_w, b_conv_b, b_dt_bias, b_a_log, b_d_skip, b_norm_w, b_w_out, ffn_w_in, ffn_conv_w, ffn_conv_b, ffn_w_out, final_norm_w, loss_target, m_norm1_w, m_norm2_w, m_a_w_in, m_a_lb_logits, m_a_norm_w, m_a_w_out, m_b_w_in, m_b_conv_w, m_b_conv_b, m_b_dt_bias, m_b_a_log, m_b_d_skip, m_b_norm_w, m_b_w_out, m_ffn_w_in, m_ffn_conv_w, m_ffn_conv_b, m_ffn_w_out, m_final_norm_w, v_norm1_w, v_norm2_w, v_a_w_in, v_a_lb_logits, v_a_norm_w, v_a_w_out, v_b_w_in, v_b_conv_w, v_b_conv_b, v_b_dt_bias, v_b_a_log, v_b_d_skip, v_b_norm_w, v_b_w_out, v_ffn_w_in, v_ffn_conv_w, v_ffn_conv_b, v_ffn_w_out, v_final_norm_w):
    w = dict(norm1_w=norm1_w, norm2_w=norm2_w, a_w_in=a_w_in, a_lb_logits=a_lb_logits, a_norm_w=a_norm_w, a_w_out=a_w_out,
             b_w_in=b_w_in, b_conv_w=b_conv_w, b_conv_b=b_conv_b, b_dt_bias=b_dt_bias, b_a_log=b_a_log, b_d_skip=b_d_skip,
             b_norm_w=b_norm_w, b_w_out=b_w_out, ffn_w_in=ffn_w_in, ffn_conv_w=ffn_conv_w, ffn_conv_b=ffn_conv_b,
             ffn_w_out=ffn_w_out, final_norm_w=final_norm_w)
    mom = dict(norm1_w=m_norm1_w, norm2_w=m_norm2_w, a_w_in=m_a_w_in, a_lb_logits=m_a_lb_logits, a_norm_w=m_a_norm_w,
               a_w_out=m_a_w_out, b_w_in=m_b_w_in, b_conv_w=m_b_conv_w, b_conv_b=m_b_conv_b, b_dt_bias=m_b_dt_bias,
               b_a_log=m_b_a_log, b_d_skip=m_b_d_skip, b_norm_w=m_b_norm_w, b_w_out=m_b_w_out, ffn_w_in=m_ffn_w_in,
               ffn_conv_w=m_ffn_conv_w, ffn_conv_b=m_ffn_conv_b, ffn_w_out=m_ffn_w_out, final_norm_w=m_final_norm_w)
    var = dict(norm1_w=v_norm1_w, norm2_w=v_norm2_w, a_w_in=v_a_w_in, a_lb_logits=v_a_lb_logits, a_norm_w=v_a_norm_w,
               a_w_out=v_a_w_out, b_w_in=v_b_w_in, b_conv_w=v_b_conv_w, b_conv_b=v_b_conv_b, b_dt_bias=v_b_dt_bias,
               b_a_log=v_b_a_log, b_d_skip=v_b_d_skip, b_norm_w=v_b_norm_w, b_w_out=v_b_w_out, ffn_w_in=v_ffn_w_in,
               ffn_conv_w=v_ffn_conv_w, ffn_conv_b=v_ffn_conv_b, ffn_w_out=v_ffn_w_out, final_norm_w=v_final_norm_w)
    chip = 2 * lax.axis_index("x") + lax.axis_index("y")
    place = jnp.stack([chip, lax.axis_index("c")]).astype(jnp.int32)

    sh = {"a_in": a_w_in[0], "a_out": a_w_out[0], "b_in": b_w_in[0], "b_out": b_w_out[0], "f0_in": ffn_w_in[0],
          "f1_in": ffn_w_in[1], "f0_out": ffn_w_out[0], "f1_out": ffn_w_out[1]}
    sh = {u: a.astype(BF) for u, a in sh.items()}
    small_shapes = [w[n].shape for n in _SMALL_SPLIT]
    spack, small_offs = _pack([w[n] for n in _SMALL_SPLIT], 128, 8, F32)
    sall = _allgather8("s_gather", spack)
    sshards = [_unpack(sall[2 * j], small_shapes, small_offs) for j in range(4)]
    sfull = {n: jnp.concatenate([sshards[j][i] for j in range(4)], axis=-1) for i, n in enumerate(_SMALL_SPLIT)}

    p = dict(
        norm1_w=norm1_w, norm2_w=norm2_w, a_lb_logits=a_lb_logits, a_norm_w=a_norm_w[0], final_norm_w=final_norm_w,
        b_conv_w=sfull["b_conv_w"][0], b_conv_b=sfull["b_conv_b"][0], b_norm_w=sfull["b_norm_w"],
        ffn_conv_w=sfull["ffn_conv_w"], ffn_conv_b=ffn_conv_b,
        b_dt_bias=jnp.pad(b_dt_bias.reshape(1, 2 * SSM_HEADS), ((0, 0), (0, 128 - 2 * SSM_HEADS))),
        b_a_log=jnp.pad(b_a_log.reshape(1, 2 * SSM_HEADS), ((0, 0), (0, 128 - 2 * SSM_HEADS))),
        b_d_skip=jnp.repeat(b_d_skip[0], SSM_HD)[None],
    )

    loss_row, dx, g, gs_ = _sequence_grads(x[0], loss_target[0], p, sh, place)
    grads = {"a_w_in": gs_["a_in"][None], "a_w_out": gs_["a_out"][None], "b_w_in": gs_["b_in"].T[None],
             "b_w_out": gs_["b_out"][None], "ffn_w_in": jnp.stack([gs_["f0_in"], gs_["f1_in"]]),
             "ffn_w_out": jnp.stack([gs_["f0_out"], gs_["f1_out"]])}

    gsmall = {
        "norm1_w": jnp.concatenate(g["norm1_w"], axis=0), "norm2_w": jnp.concatenate(g["norm2_w"], axis=0),
        "a_lb_logits": jnp.stack(g["a_lb_logits"]), "a_norm_w": g["a_norm_w"], "b_conv_w": g["b_conv_w"],
        "b_conv_b": g["b_conv_b"], "b_dt_bias": g["b_dt_bias"], "b_a_log": g["b_a_log"], "b_d_skip": g["b_d_skip"],
        "b_norm_w": g["b_norm_w"], "ffn_conv_w": jnp.stack(g["ffn_conv_w"]),
        "ffn_conv_b": jnp.concatenate(g["ffn_conv_b"], axis=0), "final_norm_w": g["final_norm_w"],
    }
    pieces = [gsmall[n] for n in _SMALL] + [loss_row]
    piece_shapes = [a.shape for a in pieces]
    gspack, gs_offs = _pack(pieces, 128, 8, F32)
    rows = gspack.shape[0]
    gsall = _allgather8("gs_gather", gspack)

    def sum8(a):
        r = a[0]
        for i in range(1, 8):
            r = r + a[i]
        return r

    gssum = _vcall("gs_sum", sum8, (1,), [(gsall, (8, rows, 128), lambda i: (0, 0, 0))],
                   [((rows, 128), F32, (rows, 128), lambda i: (0, 0))])
    gs = dict(zip(_SMALL + ("loss",), _unpack(gssum, piece_shapes, gs_offs)))
    loss = gs["loss"][0, 0]
    lb2 = gs["a_lb_logits"]
    small_grads = {
        "norm1_w": gs["norm1_w"], "norm2_w": gs["norm2_w"], "a_lb_logits": lb2[0] + lb2[1], "a_norm_w": gs["a_norm_w"],
        "b_dt_bias": gs["b_dt_bias"][:, :2 * SSM_HEADS].reshape(1, 2, SSM_HEADS),
        "b_a_log": gs["b_a_log"][:, :2 * SSM_HEADS].reshape(1, 2, SSM_HEADS),
        "b_d_skip": gs["b_d_skip"].reshape(1, SSM_HEADS, SSM_HD).sum(axis=-1),
        "ffn_conv_b": gs["ffn_conv_b"], "final_norm_w": gs["final_norm_w"][0],
        "b_conv_w": gs["b_conv_w"][None], "b_conv_b": gs["b_conv_b"], "b_norm_w": gs["b_norm_w"], "ffn_conv_w": gs["ffn_conv_w"],
    }
    for n in _SMALL_SPLIT:
        width = w[n].shape[-1]
        small_grads[n] = lax.dynamic_slice_in_dim(small_grads[n], chip * width, width, axis=small_grads[n].ndim - 1)
    grads.update(small_grads)

    delta, new_m, new_v = {}, {}, {}
    for n in _BIG:
        shp = w[n].shape
        two_d = (shp[0] * shp[1], shp[2])
        d_, m_, v_ = _adam(f"adam_{n}", w[n].reshape(two_d), grads[n].reshape(two_d), mom[n].reshape(two_d), var[n].reshape(two_d))
        delta[n], new_m[n], new_v[n] = d_.reshape(shp), m_.reshape(shp), v_.reshape(shp)
    s_shapes = [w[n].shape for n in _SMALL]
    packs = [_pack([src[n] for n in _SMALL], 128, 8, F32) for src in (w, grads, mom, var)]
    outs = _adam("adam_small", *[pk[0] for pk in packs])
    for res, dst in zip(outs, (delta, new_m, new_v)):
        dst.update(dict(zip(_SMALL, _unpack(res, s_shapes, packs[0][1]))))

    return (loss, dx[None], *[grads[n] for n in _ORDER], *[delta[n] for n in _ORDER],
            *[new_m[n] for n in _ORDER], *[new_v[n] for n in _ORDER])
```

```python
import functools

import jax
import jax.numpy as jnp
from jax import lax
from jax.experimental import pallas as pl
from jax.experimental.pallas import tpu as pltpu

F32, BF = jnp.float32, jnp.bfloat16
HI = lax.Precision.HIGHEST

D = 1024
EPS = 1e-6
HG_HEADS, HG_HD, HG_CHUNK, HG_SUB = 8, 128, 64, 16
HG_HB = 8
SSM_GB = 4
D_INNER, SSM_HEADS, SSM_HD, SSM_GROUPS, SSM_HPG, SSM_N, SSD_CHUNK = 2048, 32, 64, 8, 4, 128, 128
CONV_DIM = D_INNER + 2 * SSM_GROUPS * SSM_N
B_PROJ = 2 * D_INNER + 2 * SSM_GROUPS * SSM_N + 2 * SSM_HEADS
B_PROJ_PAD = 6272
D_FF = 2816
NEG = -1e30
ROWS = 256
GATE_ROWS = 1024
VMEM_LIMIT = 56 * 1024 * 1024

ADAM_LR, ADAM_B1, ADAM_B2, ADAM_EPS, ADAM_WD, ADAM_STEP = 0.001, 0.9, 0.999, 1e-08, 0.01, 10

MESH = pl.DeviceIdType.MESH


def _pick(n, cands):
    for c in cands:
        if n % c == 0:
            return c
    return n


class _Side:
    def __init__(self, ins, outs, plan, n_remote, n_local, alias=None):
        self.ins, self.outs, self.plan, self.n_remote, self.n_local = list(ins), list(outs), plan, n_remote, n_local
        self.alias = alias or {}

    def copies(self, in_refs, out_refs, send_sems, recv_sems, local_sems):
        remote, local = self.plan(in_refs, out_refs)
        cps = [pltpu.make_async_copy(s, d, local_sems.at[i]) for i, (s, d) in enumerate(local)]
        cps += [pltpu.make_async_remote_copy(src_ref=s, dst_ref=d, send_sem=send_sems.at[i], recv_sem=recv_sems.at[i],
                                             device_id=dev, device_id_type=MESH)
                for i, (s, d, dev) in enumerate(remote)]
        return cps

    def sems(self):
        return [pltpu.SemaphoreType.DMA((self.n_remote,)), pltpu.SemaphoreType.DMA((self.n_remote,)),
                pltpu.SemaphoreType.DMA((max(self.n_local, 1),))]


def _vcall(name, fn, grid, ins, outs, acc=None, scratch=(), place=None, side=None):
    acc = acc or {}
    n_in, n_out, nd = len(ins), len(outs), len(grid)
    n_pre = 0 if place is None else 1
    n_sin = len(side.ins) if side else 0
    n_sout = len(side.outs) if side else 0
    n_scr = len(scratch)

    def body(*refs):
        refs = refs[n_pre:]
        in_refs, refs = refs[:n_in], refs[n_in:]
        sin_refs, refs = refs[:n_sin], refs[n_sin:]
        out_refs, refs = refs[:n_out], refs[n_out:]
        sout_refs, refs = refs[:n_sout], refs[n_sout:]
        scr, sems = refs[:n_scr], refs[n_scr:]
        if side:
            at_first, at_last = None, None
            for ax in range(nd):
                f, l = pl.program_id(ax) == 0, pl.program_id(ax) == grid[ax] - 1
                at_first = f if at_first is None else jnp.logical_and(at_first, f)
                at_last = l if at_last is None else jnp.logical_and(at_last, l)

            @pl.when(at_first)
            def _():
                for cp in side.copies(sin_refs, sout_refs, *sems):
                    cp.start()

        res = fn(*[r[...] for r in in_refs], *scr)
        if not isinstance(res, (tuple, list)):
            res = (res,)
        if side:
            @pl.when(at_last)
            def _():
                for cp in side.copies(sin_refs, sout_refs, *sems):
                    cp.wait()
        for j, (o_ref, r) in enumerate(zip(out_refs, res)):
            mode = acc.get(j)
            if mode is None:
                o_ref[...] = r.astype(o_ref.dtype)
                continue
            first = pl.program_id(nd - 1) == 0
            if mode == "all":
                for ax in range(nd - 1):
                    first = jnp.logical_and(first, pl.program_id(ax) == 0)

            @pl.when(first)
            def _():
                o_ref[...] = r.astype(o_ref.dtype)

            @pl.when(jnp.logical_not(first))
            def _():
                o_ref[...] += r.astype(o_ref.dtype)

    hbm = pl.BlockSpec(memory_space=pl.ANY)
    in_specs = [pl.BlockSpec(bs, im) for _, bs, im in ins] + [hbm] * n_sin
    out_specs = [pl.BlockSpec(bs, im) for _, _, bs, im in outs] + [hbm] * n_sout
    params = pltpu.CompilerParams(dimension_semantics=("arbitrary",) * nd, vmem_limit_bytes=VMEM_LIMIT)
    out_shape = [jax.ShapeDtypeStruct(s, dt) for s, dt, _, _ in outs]
    operands = [a for a, _, _ in ins]
    scratch = list(scratch)
    aliases = {}
    if side:
        out_shape += [jax.ShapeDtypeStruct(s, dt) for s, dt in side.outs]
        operands += side.ins
        scratch += side.sems()
        aliases = {n_pre + n_in + i: n_out + o for i, o in side.alias.items()}
    if place is None:
        out = pl.pallas_call(body, name=name, grid=grid, in_specs=in_specs, out_specs=out_specs, out_shape=out_shape,
                             scratch_shapes=scratch, compiler_params=params, input_output_aliases=aliases)(*operands)
    else:
        spec = pltpu.PrefetchScalarGridSpec(num_scalar_prefetch=1, grid=grid, in_specs=in_specs, out_specs=out_specs,
                                            scratch_shapes=scratch)
        out = pl.pallas_call(body, name=name, grid_spec=spec, out_shape=out_shape, compiler_params=params,
                             input_output_aliases=aliases)(place, *operands)
    if side:
        return tuple(out[:n_out]), tuple(out[n_out:])
    return out[0] if n_out == 1 else out


def _mm(name, a, b, kind, out_dtype=F32, add=None):
    if kind == "tn":
        m, k = a.shape
        _, n = b.shape
        tm = _pick(m, (1024, 512, 256))
        tk = _pick(k, (1024, 1408, 896, 512, 256, 128))
        tn = _pick(n, (1024, 1408, 896, 512, 256, 128))

        def fn(av, bv):
            return lax.dot_general(av.astype(BF), bv.astype(BF), (((0,), (0,)), ((), ())),
                                   preferred_element_type=F32)

        return _vcall(name, fn, (k // tk, n // tn, m // tm),
                      [(a, (tm, tk), lambda i, j, s: (s, i)), (b, (tm, tn), lambda i, j, s: (s, j))],
                      [((k, n), F32, (tk, tn), lambda i, j, s: (i, j))], acc={0: "last"})
    m, k = a.shape
    n = b.shape[1] if kind == "nn" else b.shape[0]
    long_k = k > 4096
    tm = _pick(m, (512, 256)) if long_k else _pick(m, (1024, 512, 256))
    tn = _pick(n, (512, 896, 256, 128)) if long_k else _pick(n, (1024, 1408, 896, 512, 256, 128))
    dims =(((1,), (0,)), ((), ())) if kind == "nn" else (((1,), (1,)), ((), ()))

    def fn(av, bv, *rest):
        r = lax.dot_general(av.astype(BF), bv.astype(BF), dims, preferred_element_type=F32)
        return r + rest[0] if rest else r

    ins = [(a, (tm, k), lambda i, j: (i, 0)),
           (b, (k, tn), lambda i, j: (0, j)) if kind == "nn" else (b, (tn, k), lambda i, j: (j, 0))]
    if add is not None:
        ins.append((add, (tm, tn), lambda i, j: (i, j)))
    return _vcall(name, fn, (m // tm, n // tn), ins, [((m, n), out_dtype, (tm, tn), lambda i, j: (i, j))])


def _rms(h, w):
    return h * lax.rsqrt(jnp.mean(h * h, axis=-1, keepdims=True) + EPS) * w


def _rms_fwd(name, h, w):
    L = h.shape[0]
    tb = _pick(L, (ROWS,))
    return _vcall(name, _rms, (L // tb,),
                  [(h, (tb, D), lambda i: (i, 0)), (w.reshape(1, D), (1, D), lambda i: (0, 0))],
                  [((L, D), BF, (tb, D), lambda i: (i, 0))])


def _rms_bwd(name, du, h, w, dh_next):
    L = h.shape[0]
    tb = _pick(L, (ROWS,))

    def fn(duv, hv, wv, dnv):
        _, vjp = jax.vjp(_rms, hv, wv)
        dh, dw = vjp(duv)
        return dh + dnv, dw

    row = lambda i: (i, 0)
    return _vcall(name, fn, (L // tb,),
                  [(du, (tb, D), row), (h, (tb, D), row), (w.reshape(1, D), (1, D), lambda i: (0, 0)),
                   (dh_next, (tb, D), row)],
                  [((L, D), F32, (tb, D), row), ((1, D), F32, (1, D), lambda i: (0, 0))], acc={1: "all"})


def _loss_head(name, h, tgt, w):
    L = h.shape[0]
    tb = _pick(L, (ROWS,))

    def lossf(hv, wv, tv):
        err = _rms(hv, wv) - tv
        return 0.5 * jnp.sum(err * err) * (1.0 / D)

    def fn(hv, wv, tv):
        val, vjp = jax.vjp(lambda a, b: lossf(a, b, tv), hv, wv)
        dh, dw = vjp(jnp.ones((), F32))
        return jnp.full((1, 128), val, F32), dh, dw

    row = lambda i: (i, 0)
    zero = lambda i: (0, 0)
    return _vcall(name, fn, (L // tb,),
                  [(h, (tb, D), row), (w.reshape(1, D), (1, D), zero), (tgt, (tb, D), row)],
                  [((1, 128), F32, (1, 128), zero), ((L, D), F32, (tb, D), row), ((1, D), F32, (1, D), zero)],
                  acc={0: "all", 2: "all"})


def _bf(x):
    return x.astype(BF)


def _dot(a, b, dims, precision=None):
    return lax.dot_general(a, b, (dims, ((), ())), preferred_element_type=F32, precision=precision)


def _tri(n, reverse):
    r = lax.broadcasted_iota(jnp.int32, (n, n), 0)
    c = lax.broadcasted_iota(jnp.int32, (n, n), 1)
    return (r <= c) if reverse else (r >= c)


def _tri_matmul(n, reverse, x):
    hi = x.astype(BF)
    r1 = x - hi.astype(F32)
    mid = r1.astype(BF)
    lo = (r1 - mid.astype(F32)).astype(BF)
    y = _dot(_tri(n, reverse).astype(BF), jnp.concatenate([hi, mid, lo], axis=1), ((1,), (0,)))
    w = x.shape[1]
    return (y[:, :w] + y[:, w:2 * w]) + y[:, 2 * w:]


@functools.partial(jax.custom_vjp, nondiff_argnums=(0, 1))
def _running_sum(n, reverse, x):
    return _tri_matmul(n, reverse, x)


def _running_sum_fwd(n, reverse, x):
    return _tri_matmul(n, reverse, x), None


def _running_sum_bwd(n, reverse, _, ct):
    return (_tri_matmul(n, not reverse, ct),)


_running_sum.defvjp(_running_sum_fwd, _running_sum_bwd)


def _gla_chunk(q_raw, f_raw, v, lb3, S, reverse):
    C, SB, HD = HG_CHUNK, HG_SUB, HG_HD
    H = S.shape[0]
    heads = [slice(HD * h, HD * (h + 1)) for h in range(H)]
    row3 = lax.broadcasted_iota(jnp.int32, (3, 1), 0)
    e = jnp.exp(lb3 - jnp.max(lb3, axis=0, keepdims=True))
    lb = jnp.sum(jnp.where(row3 == 0, e, 0.0), axis=0, keepdims=True) / jnp.sum(e, axis=0, keepdims=True)
    q = q_raw * jax.nn.sigmoid(q_raw)
    f = lb + (1.0 - lb) * jax.nn.sigmoid(f_raw)
    g = jnp.log(f)
    k = 1.0 - f
    b = _running_sum(C, reverse, g)
    row = lax.broadcasted_iota(jnp.int32, (C, 1), 0)
    vb = _bf(v)

    def rowof(x, t):
        return jnp.sum(jnp.where(row == t, x, 0.0), axis=0, keepdims=True)

    qe = _bf(q * jnp.exp(b))
    o = [_dot(qe[:, hs], _bf(S[h]), ((1,), (0,))) for h, hs in enumerate(heads)]
    att = [None] * H
    for i in range(C // SB):
        lo = SB * i
        if (not reverse and i == 0) or (reverse and i == C // SB - 1):
            continue
        first = lo + SB - 1 if reverse else lo
        r = rowof(b, first) - rowof(g, first)
        in_blk = jnp.logical_and(row >= lo, row < lo + SB)
        before = (row >= lo + SB) if reverse else (row < lo)
        qi = _bf(q * jnp.exp(jnp.where(in_blk, b - r, NEG)))
        kk = _bf(k * jnp.exp(jnp.where(before, r - b, NEG)))
        for h, hs in enumerate(heads):
            a_i = _dot(qi[:, hs], kk[:, hs], ((1,), (1,)))
            att[h] = a_i if att[h] is None else att[h] + a_i
    o = [o[h] + _dot(_bf(att[h]), vb[:, hs], ((1,), (0,))) for h, hs in enumerate(heads)]
    s_i = lax.broadcasted_iota(jnp.int32, (SB, SB, HD), 0)
    t_i = lax.broadcasted_iota(jnp.int32, (SB, SB, HD), 1)
    pair = (t_i <= s_i) if reverse else (t_i >= s_i)
    shp = (SB, SB, HD)
    diag = [[] for _ in range(H)]
    for i in range(C // SB):
        rows = slice(SB * i, SB * (i + 1))
        for h, hs in enumerate(heads):
            qb, kb, bb = q[rows, hs], k[rows, hs], b[rows, hs]
            dif = lax.broadcast_in_dim(bb, shp, (1, 2)) - lax.broadcast_in_dim(bb, shp, (0, 2))
            w = lax.broadcast_in_dim(qb, shp, (1, 2)) * jnp.exp(jnp.where(pair, dif, NEG)) * lax.broadcast_in_dim(kb, shp, (0, 2))
            d = jnp.sum(w, axis=2, keepdims=True)
            diag[h].append(jnp.sum(d * lax.broadcast_in_dim(v[rows, hs], shp, (0, 2)), axis=0))
    o = jnp.concatenate([o[h] + jnp.concatenate(diag[h], axis=0) for h in range(H)], axis=1)
    btot = rowof(b, 0 if reverse else C - 1)
    kd = _bf(k * jnp.exp(btot - b))
    eye = lax.broadcasted_iota(jnp.int32, (HD, HD), 0) == lax.broadcasted_iota(jnp.int32, (HD, HD), 1)
    s_new = []
    for h, hs in enumerate(heads):
        btot_col = jnp.sum(jnp.where(eye, btot[:, hs], 0.0), axis=1, keepdims=True)
        s_new.append((jnp.exp(btot_col) * S[h] + _dot(kd[:, hs], vb[:, hs], ((0,), (0,))))[None])
    return o, jnp.concatenate(s_new, axis=0)


def _gla_fwd(name, pa, lbl, reverse, side=None):
    L = pa.shape[0]
    C = HG_CHUNK
    nc = L // C
    cidx = (lambda i: nc - 1 - i) if reverse else (lambda i: i)
    sec = 2 if reverse else 1
    hb_n, nh = HG_HB, HG_HEADS // HG_HB

    def fn(qr, fr, v, lb3, s_ref):
        @pl.when(pl.program_id(1) == 0)
        def _():
            s_ref[...] = jnp.zeros_like(s_ref)

        s_all = s_ref[...]
        o, s_new = _gla_chunk(qr, fr, v, lb3, s_all, reverse)
        s_ref[...] = s_new
        return o, s_all[None]

    blk = (C, HG_HD * hb_n)
    return _vcall(name, fn, (nh, nc),
                  [(pa, blk, lambda h, i: (cidx(i), h)), (pa, blk, lambda h, i: (cidx(i), sec * nh + h)),
                   (pa, blk, lambda h, i: (cidx(i), 3 * nh + h)), (lbl, (3, HG_HD * hb_n), lambda h, i: (0, h))],
                  [((L, D), F32, blk, lambda h, i: (cidx(i), h)),
                   ((nc, HG_HEADS, HG_HD, HG_HD), F32, (1, hb_n, HG_HD, HG_HD), lambda h, i: (cidx(i), h, 0, 0))],
                  scratch=[pltpu.VMEM((hb_n, HG_HD, HG_HD), F32)], side=side)


def _gla_bwd(name, pa, lbl, s_in, do, reverse, prev=None, side=None):
    L = pa.shape[0]
    C = HG_CHUNK
    nc = L // C
    cidx = (lambda i: i) if reverse else (lambda i: nc - 1 - i)
    sec = 2 if reverse else 1
    n_prev = 0 if prev is None else 2
    hb_n, nh = HG_HB, HG_HEADS // HG_HB

    def fn(qr, fr, v, lb3, s, dov, *rest):
        ds_ref = rest[n_prev]

        @pl.when(pl.program_id(1) == 0)
        def _():
            ds_ref[...] = jnp.zeros_like(ds_ref)

        _, vjp = jax.vjp(lambda *a: _gla_chunk(*a, reverse), qr, fr, v, lb3, s[0])
        dq, df, dv, dlb, ds = vjp((dov, ds_ref[...]))
        ds_ref[...] = ds
        if n_prev:
            dq, dv = dq + rest[0], dv + rest[1]
        return dq, df, dv, dlb

    blk = (C, HG_HD * hb_n)
    at = lambda h, i: (cidx(i), h)
    ins = [(pa, blk, at), (pa, blk, lambda h, i: (cidx(i), sec * nh + h)), (pa, blk, lambda h, i: (cidx(i), 3 * nh + h)),
           (lbl, (3, HG_HD * hb_n), lambda h, i: (0, h)),
           (s_in, (1, hb_n, HG_HD, HG_HD), lambda h, i: (cidx(i), h, 0, 0)), (do, blk, at)]
    if prev is not None:
        ins += [(prev[0], blk, at), (prev[1], blk, at)]
    sum_dt = F32 if prev is None else BF
    return _vcall(name, fn, (nh, nc), ins,
                  [((L, D), sum_dt, blk, at), ((L, D), BF, blk, at), ((L, D), sum_dt, blk, at),
                   ((3, D), F32, (3, HG_HD * hb_n), lambda h, i: (0, h))],
                  acc={3: "last"}, scratch=[pltpu.VMEM((hb_n, HG_HD, HG_HD), F32)], side=side)


def _hgout(o_f, o_b, g, nw):
    o = o_f + o_b
    return _rms(o, nw) * (g * jax.nn.sigmoid(g))


def _hgout_fwd(name, o_f, o_b, pa, nw, side=None):
    L = o_f.shape[0]
    tb = _pick(L, (GATE_ROWS, ROWS))
    blk = (tb, HG_HD)
    at = lambda h, i: (i, h)
    return _vcall(name, _hgout, (HG_HEADS, L // tb),
                  [(o_f, blk, at), (o_b, blk, at), (pa, blk, lambda h, i: (i, 32 + h)),
                   (nw.reshape(1, HG_HD), (1, HG_HD), lambda h, i: (0, 0))],
                  [((L, D), BF, blk, at)], side=side)


def _hgout_bwd(name, o_f, o_b, pa, nw, dy, side=None):
    L = o_f.shape[0]
    tb = _pick(L, (GATE_ROWS, ROWS))

    def fn(ofv, obv, gv, nwv, dyv):
        _, vjp = jax.vjp(_hgout, ofv, obv, gv, nwv)
        do, _, dg, dnw = vjp(dyv)
        return do, dg, dnw

    blk = (tb, HG_HD)
    at = lambda h, i: (i, h)
    zero = lambda h, i: (0, 0)
    return _vcall(name, fn, (HG_HEADS, L // tb),
                  [(o_f, blk, at), (o_b, blk, at), (pa, blk, lambda h, i: (i, 32 + h)),
                   (nw.reshape(1, HG_HD), (1, HG_HD), zero), (dy, blk, at)],
                  [((L, D), F32, blk, at), ((L, D), BF, blk, at), ((1, HG_HD), F32, (1, HG_HD), zero)],
                  acc={2: "all"}, side=side)


def _shift(x, s):
    if s == 0:
        return x
    n = x.shape[0]
    t = lax.broadcasted_iota(jnp.int32, (n, 1), 0)
    if s > 0:
        return jnp.where(t >= s, pltpu.roll(x, s, 0), 0.0)
    return jnp.where(t < n + s, pltpu.roll(x, n + s, 0), 0.0)


def _conv(x, w, b):
    kk = w.shape[0]
    p = (kk - 1) // 2
    y = b
    for j in range(kk):
        y = y + w[j:j + 1] * _shift(x, p - j)
    return y


def _conv_bwd(x, w, dc):
    kk = w.shape[0]
    p = (kk - 1) // 2
    dx = None
    dws = []
    for j in range(kk):
        t = w[j:j + 1] * _shift(dc, j - p)
        dx = t if dx is None else dx + t
        dws.append(jnp.sum(dc * _shift(x, p - j), axis=0, keepdims=True))
    rows = lax.broadcasted_iota(jnp.int32, (kk, 1), 0)
    dw = None
    for j in range(kk):
        t = jnp.where(rows == j, dws[j], 0.0)
        dw = t if dw is None else dw + t
    return dx, dw, jnp.sum(dc, axis=0, keepdims=True)


def _silu_grad(c):
    s = jax.nn.sigmoid(c)
    return s * (1.0 + c * (1.0 - s))


def _glu_fwd(name, pf, cw, cb):
    L = pf.shape[0]
    tc = 128
    nt = D_FF // tc

    def fn(gate, val, w, b):
        c = _conv(gate, w, b)
        return c * jax.nn.sigmoid(c) * val

    return _vcall(name, fn, (nt,),
                  [(pf, (L, tc), lambda j: (0, j)), (pf, (L, tc), lambda j: (0, nt + j)),
                   (cw, (3, tc), lambda j: (0, j)), (cb.reshape(1, D_FF), (1, tc), lambda j: (0, j))],
                  [((L, D_FF), BF, (L, tc), lambda j: (0, j))])


def _glu_bwd(name, pf, cw, cb, dy, side=None):
    L = pf.shape[0]
    tc = 128
    nt = D_FF // tc

    def fn(gate, val, w, b, dyv):
        c = _conv(gate, w, b)
        sc = c * jax.nn.sigmoid(c)
        dc = dyv * val * _silu_grad(c)
        dgate, dw, db = _conv_bwd(gate, w, dc)
        return dgate, dyv * sc, dw, db

    col = lambda j: (0, j)
    return _vcall(name, fn, (nt,),
                  [(pf, (L, tc), col), (pf, (L, tc), lambda j: (0, nt + j)), (cw, (3, tc), col),
                   (cb.reshape(1, D_FF), (1, tc), col), (dy, (L, tc), col)],
                  [((L, D_FF), BF, (L, tc), col), ((L, D_FF), BF, (L, tc), col),
                   ((3, D_FF), F32, (3, tc), col), ((1, D_FF), F32, (1, tc), col)], side=side)


def _perm_tile(j):
    return jnp.where(j < 16, 4 * (j // 2) + j % 2, jnp.where(j < 24, 4 * (j - 16) + 2, 4 * (j - 24) + 3))


def _mpre_fwd(name, pb, cw, cb):
    L = pb.shape[0]
    tc = 128

    def fn(x, w, b):
        c = _conv(x, w, b)
        return c * jax.nn.sigmoid(c)

    return _vcall(name, fn, (CONV_DIM // tc,),
                  [(pb, (L, tc), lambda j: (0, 16 + j)), (cw, (5, tc), lambda j: (0, j)),
                   (cb.reshape(1, CONV_DIM), (1, tc), lambda j: (0, j))],
                  [((L, CONV_DIM), F32, (L, tc), lambda j: (0, _perm_tile(j)))])


def _mpre_bwd(name, pb, cw, cb, dact):
    L = pb.shape[0]
    tc = 128

    def fn(x, w, b, da):
        c = _conv(x, w, b)
        return _conv_bwd(x, w, da * _silu_grad(c))

    col = lambda j: (0, j)
    return _vcall(name, fn, (CONV_DIM // tc,),
                  [(pb, (L, tc), lambda j: (0, 16 + j)), (cw, (5, tc), col), (cb.reshape(1, CONV_DIM), (1, tc), col),
                   (dact, (L, tc), lambda j: (0, _perm_tile(j)))],
                  [((L, CONV_DIM), BF, (L, tc), col), ((5, CONV_DIM), F32, (5, tc), col),
                   ((1, CONV_DIM), F32, (1, tc), col)])


def _softplus(x):
    return jnp.maximum(x, 0.0) + jnp.log(1.0 + jnp.exp(-jnp.abs(x)))


def _dt_fwd(name, pb, dtb, alog):
    L = pb.shape[0]
    tb = _pick(L, (1024, ROWS))

    def fn(x, bias, al):
        dt = _softplus(x + bias)
        return dt, dt * (-jnp.exp(al))

    row = lambda i: (i, 0)
    zero = lambda i: (0, 0)
    return _vcall(name, fn, (L // tb,),
                  [(pb, (tb, 128), lambda i: (i, 48)), (dtb, (1, 128), zero), (alog, (1, 128), zero)],
                  [((L, 128), F32, (tb, 128), row), ((L, 128), F32, (tb, 128), row)])


def _dt_bwd(name, pb, dtb, alog, ddt_f, dla_f, ddt_b, dla_b):
    L = pb.shape[0]
    tb = _pick(L, (1024, ROWS))

    def fn(x, bias, al, a1, b1, a2, b2):
        ddt = jnp.sum(a1, axis=0) + jnp.sum(a2, axis=0)
        dla = jnp.sum(b1, axis=0) + jnp.sum(b2, axis=0)
        z = x + bias
        dt = _softplus(z)
        a = -jnp.exp(al)
        dz = (ddt + dla * a) * jax.nn.sigmoid(z)
        return dz, jnp.sum(dz, axis=0, keepdims=True), jnp.sum(dla * dt, axis=0, keepdims=True) * a

    zero = lambda i: (0, 0)
    g3 = (ddt_f.shape[0], tb, 128)
    at3 = lambda i: (0, i, 0)
    return _vcall(name, fn, (L // tb,),
                  [(pb, (tb, 128), lambda i: (i, 48)), (dtb, (1, 128), zero), (alog, (1, 128), zero),
                   (ddt_f, g3, at3), (dla_f, g3, at3), (ddt_b, g3, at3), (dla_b, g3, at3)],
                  [((L, 128), BF, (tb, 128), lambda i: (i, 0)), ((1, 128), F32, (1, 128), zero),
                   ((1, 128), F32, (1, 128), zero)], acc={1: "all", 2: "all"})


def _split_dot(x, e, dims, pieces):
    hi = x.astype(BF)
    r1 = x - hi.astype(F32)
    mid = r1.astype(BF)
    y = _dot(hi, e, dims) + _dot(mid, e, dims)
    if pieces == 3:
        y = y + _dot((r1 - mid.astype(F32)).astype(BF), e, dims)
    return y


@functools.partial(jax.custom_vjp, nondiff_argnums=(2,))
def _spread(x, e, pieces):
    return _split_dot(x, e, ((1,), (0,)), pieces)


def _spread_fwd(x, e, pieces):
    return _split_dot(x, e, ((1,), (0,)), pieces), e


def _spread_bwd(pieces, e, ct):
    return _split_dot(ct, e, ((1,), (1,)), pieces), jnp.zeros_like(e)


_spread.defvjp(_spread_fwd, _spread_bwd)


def _ssd_chunk(xa, dt, la, hs, head0, reverse):
    C, P4, HD, N = SSD_CHUNK, SSM_HPG * SSM_HD, SSM_HD, SSM_N
    G = hs.shape[0]
    nh = SSM_HPG * G
    row = lax.broadcasted_iota(jnp.int32, (C, 1), 0)
    lane = lax.broadcasted_iota(jnp.int32, (1, 128), 1)
    eye = lax.broadcasted_iota(jnp.int32, (C, C), 0) == lax.broadcasted_iota(jnp.int32, (C, C), 1)
    tri = _tri(C, reverse)
    last = 0 if reverse else C - 1
    acum = _running_sum(C, reverse, la)
    atot = jnp.sum(jnp.where(row == last, acum, 0.0), axis=0, keepdims=True)
    src = lax.broadcasted_iota(jnp.int32, (128, 1), 0) - head0
    to_x = (src == lax.broadcasted_iota(jnp.int32, (1, nh * HD), 1) // HD).astype(BF)
    to_c = (src == lax.broadcasted_iota(jnp.int32, (1, nh * C), 1) // C).astype(BF)
    dt_x = _spread(dt, to_x, 2)
    ea_x = _spread(jnp.exp(acum), to_x, 2)
    dec_x = _spread(jnp.exp(atot - acum), to_x, 2)
    ac_c = _spread(acum, to_c, 3)
    col_head = lax.broadcasted_iota(jnp.int32, (1, P4), 1) // HD
    row_head = lax.broadcasted_iota(jnp.int32, (P4, 1), 0) // HD
    ys, news = [], []
    for gi in range(G):
        xs = xa[:, 512 * gi:512 * gi + P4]
        bm = _bf(xa[:, 512 * gi + P4:512 * gi + P4 + N])
        cm = _bf(xa[:, 512 * gi + P4 + N:512 * (gi + 1)])
        gx = slice(P4 * gi, P4 * (gi + 1))
        cb = _dot(cm, bm, ((1,), (1,)))
        xd = xs * dt_x[:, gx]
        ms, xds, scale = [], [], 0.0
        for j in range(SSM_HPG):
            i = SSM_HPG * gi + j
            ac = ac_c[:, C * i:C * (i + 1)]
            ac_row = jnp.sum(jnp.where(eye, ac, 0.0), axis=0, keepdims=True)
            ms.append(_bf(cb * jnp.exp(jnp.where(tri, ac - ac_row, NEG))))
            xds.append(_bf(jnp.where(col_head == j, xd, 0.0)))
            a_i = jnp.sum(jnp.where(lane == head0 + i, atot, 0.0), axis=1, keepdims=True)
            scale = scale + jnp.where(row_head == j, jnp.exp(a_i), 0.0)
        y = _dot(jnp.concatenate(ms, axis=1), jnp.concatenate(xds, axis=0), ((1,), (0,)))
        y = y + _dot(cm, _bf(hs[gi]), ((1,), (1,))) * ea_x[:, gx]
        ys.append(y)
        news.append((scale * hs[gi] + _dot(_bf(xd * dec_x[:, gx]), bm, ((0,), (0,))))[None])
    return jnp.concatenate(ys, axis=1), jnp.concatenate(news, axis=0)


def _ssd_fwd(name, xact, dt, la, reverse):
    L = xact.shape[0]
    C = SSD_CHUNK
    nc = L // C
    cidx = (lambda i: nc - 1 - i) if reverse else (lambda i: i)
    base = SSM_HEADS if reverse else 0
    P4 = SSM_HPG * SSM_HD

    gb_n = SSM_GB

    def fn(xa, dtv, lav, h_ref):
        @pl.when(pl.program_id(1) == 0)
        def _():
            h_ref[...] = jnp.zeros_like(h_ref)

        h_all = h_ref[...]
        y, h_new = _ssd_chunk(xa, dtv, lav, h_all, base + SSM_HPG * gb_n * pl.program_id(0), reverse)
        h_ref[...] = h_new
        return y, h_all[None]

    return _vcall(name, fn, (SSM_GROUPS // gb_n, nc),
                  [(xact, (C, 512 * gb_n), lambda g, i: (cidx(i), g)), (dt, (C, 128), lambda g, i: (cidx(i), 0)),
                   (la, (C, 128), lambda g, i: (cidx(i), 0))],
                  [((L, D_INNER), F32, (C, P4 * gb_n), lambda g, i: (cidx(i), g)),
                   ((nc, SSM_GROUPS, P4, SSM_N), F32, (1, gb_n, P4, SSM_N), lambda g, i: (cidx(i), g, 0, 0))],
                  scratch=[pltpu.VMEM((gb_n, P4, SSM_N), F32)])


def _ssd_bwd(name, xact, dt, la, h_in, dy, reverse, prev_xs=None, prev_all=None, side=None):
    L = xact.shape[0]
    C = SSD_CHUNK
    nc = L // C
    cidx = (lambda i: i) if reverse else (lambda i: nc - 1 - i)
    base = SSM_HEADS if reverse else 0
    P4 = SSM_HPG * SSM_HD

    gb_n = SSM_GB

    def fn(xa, dtv, lav, hs, dyv, pv, dh_ref):
        @pl.when(pl.program_id(1) == 0)
        def _():
            dh_ref[...] = jnp.zeros_like(dh_ref)

        head0 = base + SSM_HPG * gb_n * pl.program_id(0)
        _, vjp = jax.vjp(lambda a, b, c, d: _ssd_chunk(a, b, c, d, head0, reverse), xa, dtv, lav, hs[0])
        dxa, ddt, dla, dh = vjp((dyv, dh_ref[...]))
        dh_ref[...] = dh
        if prev_all is not None:
            dxa = dxa + pv
        else:
            zeros = jnp.zeros((C, 2 * SSM_N), F32)
            dxa = dxa + jnp.concatenate([t for gb in range(gb_n) for t in (pv[:, P4 * gb:P4 * (gb + 1)], zeros)], axis=1)
        return dxa, ddt[None], dla[None]

    at = lambda g, i: (cidx(i), g)
    at0 = lambda g, i: (cidx(i), 0)
    pv = (prev_all, (C, 512 * gb_n), at) if prev_all is not None else (prev_xs, (C, P4 * gb_n), at)
    steps = SSM_GROUPS // gb_n
    return _vcall(name, fn, (steps, nc),
                  [(xact, (C, 512 * gb_n), at), (dt, (C, 128), at0), (la, (C, 128), at0),
                   (h_in, (1, gb_n, P4, SSM_N), lambda g, i: (cidx(i), g, 0, 0)), (dy, (C, P4 * gb_n), at), pv],
                  [((L, CONV_DIM), F32, (C, 512 * gb_n), at),
                   ((steps, L, 128), F32, (1, C, 128), lambda g, i: (g, cidx(i), 0)),
                   ((steps, L, 128), F32, (1, C, 128), lambda g, i: (g, cidx(i), 0))],
                  scratch=[pltpu.VMEM((gb_n, P4, SSM_N), F32)], side=side)


def _mpost(y_f, y_b, xs, z, dsk, nw):
    y = (y_f + y_b + xs * dsk) * (z * jax.nn.sigmoid(z))
    return _rms(y, nw)


def _mpost_fwd(name, y_f, y_b, xact, pb, dsk, nw):
    L = y_f.shape[0]
    tb = _pick(L, (GATE_ROWS, ROWS))
    blk = (tb, 256)
    at = lambda g, i: (i, g)
    par = lambda g, i: (0, g)
    return _vcall(name, _mpost, (SSM_GROUPS, L // tb),
                  [(y_f, blk, at), (y_b, blk, at), (xact, blk, lambda g, i: (i, 2 * g)), (pb, blk, at),
                   (dsk, (1, 256), par), (nw, (1, 256), par)],
                  [((L, D_INNER), BF, blk, at)])


def _mpost_bwd(name, y_f, y_b, xact, pb, dsk, nw, dy):
    L = y_f.shape[0]
    tb = _pick(L, (GATE_ROWS, ROWS))

    def fn(yf, yb, xs, z, dskv, nwv, dyv):
        _, vjp = jax.vjp(_mpost, yf, yb, xs, z, dskv, nwv)
        dyf, _, dxs, dz, ddsk, dnw = vjp(dyv)
        return dyf, dxs, dz, ddsk, dnw

    blk = (tb, 256)
    at = lambda g, i: (i, g)
    par = lambda g, i: (0, g)
    return _vcall(name, fn, (SSM_GROUPS, L // tb),
                  [(y_f, blk, at), (y_b, blk, at), (xact, blk, lambda g, i: (i, 2 * g)), (pb, blk, at),
                   (dsk, (1, 256), par), (nw, (1, 256), par), (dy, blk, at)],
                  [((L, D_INNER), F32, blk, at), ((L, D_INNER), F32, blk, at), ((L, D_INNER), BF, blk, at),
                   ((1, D_INNER), F32, (1, 256), par), ((1, D_INNER), F32, (1, 256), par)],
                  acc={3: "last", 4: "last"})


def _ffn_fwd(tag, h, nw, w_in, cw, cb, w_out):
    u = _rms_fwd(f"{tag}_norm", h, nw)
    pf = _mm(f"{tag}_in", u, w_in, "nn")
    yf = _glu_fwd(f"{tag}_glu", pf, cw, cb)
    return _mm(f"{tag}_out", yf, w_out, "nn", add=h), (u, pf, yf)


def _ffn_bwd(tag, h, nw, w_in, cw, cb, w_out, saved, dh, side=None):
    u, pf, yf = saved
    d_w_out = _mm(f"{tag}_dwout", yf, dh, "tn")
    dyf = _mm(f"{tag}_dy", dh, w_out, "nt")
    got = ()
    if side is None:
        dgate, dval, dcw, dcb = _glu_bwd(f"{tag}_dglu", pf, cw, cb, dyf)
    else:
        (dgate, dval, dcw, dcb), got = _glu_bwd(f"{tag}_dglu", pf, cw, cb, dyf, side=side)
    dpf = jnp.concatenate([dgate, dval], axis=1)
    d_w_in = _mm(f"{tag}_dwin", u, dpf, "tn")
    du = _mm(f"{tag}_du", dpf, w_in, "nt")
    dh_in, dnw = _rms_bwd(f"{tag}_dnorm", du, h, nw, dh)
    return dh_in, dnw, d_w_in, dcw, dcb, d_w_out, got


def _sequence_grads(x, tgt, p, sh, place):
    g = {}
    lbl = p["a_lb_logits"]
    first, mid, last = ("a_in", "a_out"), ("f0_in", "f0_out", "b_in", "b_out"), ("f1_in", "f1_out")
    W = {}
    got = _exchange("w_first", _gather_side(first, (), sh, W))
    W.update(zip(first, got))
    got = _exchange("w_first_pass", _gather_side((), first, sh, W))
    W.update(zip(first, got))
    u1 = _rms_fwd("a_norm", x, p["norm1_w"][0])
    pa = _mm("a_in", u1, W["a_in"], "nn")
    (o_f, s_f), got = _gla_fwd("a_scan_f", pa, lbl, False, side=_gather_side(mid, (), sh, W))
    W.update(zip(mid, got))
    (o_b, s_b), got = _gla_fwd("a_scan_b", pa, lbl, True, side=_gather_side(last, mid, sh, W))
    W.update(zip(last + mid, got))
    (ya,), got = _hgout_fwd("a_gate", o_f, o_b, pa, p["a_norm_w"], side=_gather_side((), last, sh, W))
    W.update(zip(last, got))
    wb4 = W["b_in"].reshape(4, D, B_PROJ // 4)
    p = dict(p, a_w_in=W["a_in"], a_w_out=W["a_out"], b_w_out=W["b_out"], ffn_w_in=(W["f0_in"], W["f1_in"]),
             ffn_w_out=(W["f0_out"], W["f1_out"]),
             b_w_in=jnp.pad(jnp.concatenate([wb4[j] for j in range(4)], axis=1), ((0, 0), (0, B_PROJ_PAD - B_PROJ))))
    h1 = _mm("a_out", ya, p["a_w_out"], "nn", add=x)
    h2, ffn0 = _ffn_fwd("f0", h1, p["norm2_w"][0], p["ffn_w_in"][0], p["ffn_conv_w"][0], p["ffn_conv_b"][0], p["ffn_w_out"][0])
    u3 = _rms_fwd("b_norm", h2, p["norm1_w"][1])
    pb = _mm("b_in", u3, p["b_w_in"], "nn")
    xact = _mpre_fwd("b_conv", pb, p["b_conv_w"], p["b_conv_b"])
    dt, la = _dt_fwd("b_dt", pb, p["b_dt_bias"], p["b_a_log"])
    y_f, hs_f = _ssd_fwd("b_scan_f", xact, dt, la, False)
    y_b, hs_b = _ssd_fwd("b_scan_b", xact, dt, la, True)
    yb = _mpost_fwd("b_gate", y_f, y_b, xact, pb, p["b_d_skip"], p["b_norm_w"])
    h3 = _mm("b_out", yb, p["b_w_out"], "nn", add=h2)
    h4, ffn1 = _ffn_fwd("f1", h3, p["norm2_w"][1], p["ffn_w_in"][1], p["ffn_conv_w"][1], p["ffn_conv_b"][1], p["ffn_w_out"][1])
    loss, dh4, g["final_norm_w"] = _loss_head("head", h4, tgt, p["final_norm_w"])
    dh3, dn2_1, dwin1, dcw1, dcb1, dwout1, _ = _ffn_bwd("f1", h3, p["norm2_w"][1], p["ffn_w_in"][1], p["ffn_conv_w"][1],
                                                        p["ffn_conv_b"][1], p["ffn_w_out"][1], ffn1, dh4)
    G = {"f1_in": dwin1, "f1_out": dwout1}
    G["b_out"] = _mm("b_dwout", yb, dh3, "tn")
    dyb = _mm("b_dy", dh3, p["b_w_out"], "nt")
    dys, dxs, dz, g["b_d_skip"], g["b_norm_w"] = _mpost_bwd("b_dgate", y_f, y_b, xact, pb, p["b_d_skip"], p["b_norm_w"], dyb)
    wave1 = ("f1_in", "f1_out", "b_out")
    (dxa1, ddt_f, dla_f), got = _ssd_bwd("b_dscan_f", xact, dt, la, hs_f, dys, False, prev_xs=dxs, side=_pair_side(wave1, G))
    chip_sums = _pair_sums(wave1, G, got)
    dxa, ddt_b, dla_b = _ssd_bwd("b_dscan_b", xact, dt, la, hs_b, dys, True, prev_all=dxa1)
    dxbc, g["b_conv_w"], g["b_conv_b"] = _mpre_bwd("b_dconv", pb, p["b_conv_w"], p["b_conv_b"], dxa)
    ddtr, g["b_dt_bias"], g["b_a_log"] = _dt_bwd("b_ddt", pb, p["b_dt_bias"], p["b_a_log"], ddt_f, dla_f, ddt_b, dla_b)
    dpb = jnp.concatenate([dz, dxbc, ddtr], axis=1)
    G["b_in"] = _mm("b_dwin", dpb, u3, "tn")
    du3 = _mm("b_du", dpb, p["b_w_in"], "nt")
    dh2, dn1_1 = _rms_bwd("b_dnorm", du3, h2, p["norm1_w"][1], dh3)
    dh1, dn2_0, G["f0_in"], dcw0, dcb0, G["f0_out"], got = _ffn_bwd("f0", h1, p["norm2_w"][0], p["ffn_w_in"][0], p["ffn_conv_w"][0],
                                                                   p["ffn_conv_b"][0], p["ffn_w_out"][0], ffn0, dh2,
                                                                   side=_pair_side(("b_in",), G))
    chip_sums.update(_pair_sums(("b_in",), G, got))
    wave3 = ("f0_in", "f0_out")
    G["a_out"] = _mm("a_dwout", ya, dh1, "tn")
    dya = _mm("a_dy", dh1, p["a_w_out"], "nt")
    (do, dg, g["a_norm_w"]), got = _hgout_bwd("a_dgate", o_f, o_b, pa, p["a_norm_w"], dya, side=_pair_side(wave3, G))
    chip_sums.update(_pair_sums(wave3, G, got))
    late = last + mid
    (dq1, df1, dv1, dl1), got = _gla_bwd("a_dscan_f", pa, lbl, s_f, do, False, side=_chips_side(late, chip_sums))
    shards = {u: _chip_sum(f"gl_sum_{u}", _GGEO[u], chip_sums[u], r, place) for u, r in zip(late, got)}
    (dq, df2, dv, dl2), got = _gla_bwd("a_dscan_b", pa, lbl, s_b, do, True, prev=(dq1, dv1), side=_halves_side(late, shards))
    shards = dict(zip(late, got))
    dpa = jnp.concatenate([dq, df1, df2, dv, dg], axis=1)
    G["a_in"] = _mm("a_dwin", u1, dpa, "tn")
    du1 = _mm("a_du", dpa, p["a_w_in"], "nt")
    dx, dn1_0 = _rms_bwd("a_dnorm", du1, x, p["norm1_w"][0], dh1)
    chip_sums = _pair_sums(first, G, _exchange("ga_pair", _pair_side(first, G)))
    got = _exchange("ga_chips", _chips_side(first, chip_sums))
    mine = {u: _chip_sum(f"ga_sum_{u}", _GGEO[u], chip_sums[u], r, place) for u, r in zip(first, got)}
    shards.update(zip(first, _exchange("ga_halves", _halves_side(first, mine))))
    g["a_lb_logits"] = (dl1, dl2)
    g["norm1_w"] = (dn1_0, dn1_1)
    g["norm2_w"] = (dn2_0, dn2_1)
    g["ffn_conv_w"] = (dcw0, dcw1)
    g["ffn_conv_b"] = (dcb0, dcb1)
    return loss, dx, g, shards


def _here():
    return lax.axis_index("x"), lax.axis_index("y"), lax.axis_index("c")


def _allgather8(name, src, by_core=False):
    blk = src.shape[1:] if by_core else src.shape

    def body(x_ref, out_ref, send_sems, recv_sems, local_sem):
        x, y, c = _here()
        me, sibling = (x, y, c), (x, y, 1 - c)
        chips = [(1 - x, y), (x, 1 - y), (1 - x, 1 - y)]
        own = x_ref.at[c] if by_core else x_ref

        def slot(px, py, pc):
            return out_ref.at[4 * px + 2 * py + pc]

        def copy(k, block, to, from_own=False):
            return pltpu.make_async_remote_copy(
                src_ref=own if from_own else slot(*block), dst_ref=slot(*block),
                send_sem=send_sems.at[k], recv_sem=recv_sems.at[k], device_id=to, device_id_type=MESH)

        mine = pltpu.make_async_copy(own, slot(*me), local_sem)
        mine.start()
        first = [copy(0, me, sibling, from_own=True)]
        first += [copy(1 + j, me, (*chip, c), from_own=True) for j, chip in enumerate(chips)]
        for cp in first:
            cp.start()
        passed = [copy(4 + j, (*chip, c), sibling) for j, chip in enumerate(chips)]
        for j, chip in enumerate(chips):
            copy(1 + j, (*chip, c), me).wait_recv()
            passed[j].start()
        copy(0, sibling, me).wait_recv()
        for j, chip in enumerate(chips):
            copy(4 + j, (*chip, 1 - c), me).wait_recv()
        for cp in first + passed:
            cp.wait_send()
        mine.wait()

    return pl.pallas_call(
        body, name=name,
        out_shape=jax.ShapeDtypeStruct((8,) + tuple(blk), src.dtype),
        in_specs=[pl.BlockSpec(memory_space=pl.ANY)],
        out_specs=pl.BlockSpec(memory_space=pl.ANY),
        scratch_shapes=[pltpu.SemaphoreType.DMA((7,)), pltpu.SemaphoreType.DMA((7,)), pltpu.SemaphoreType.DMA],
    )(src)


def _exchange(name, side):
    n_i, n_o = len(side.ins), len(side.outs)

    def body(*refs):
        copies = side.copies(refs[:n_i], refs[n_i:n_i + n_o], *refs[n_i + n_o:])
        for cp in copies:
            cp.start()
        for cp in copies:
            cp.wait()

    return pl.pallas_call(
        body, name=name,
        out_shape=[jax.ShapeDtypeStruct(s, dt) for s, dt in side.outs],
        in_specs=[pl.BlockSpec(memory_space=pl.ANY)] * n_i,
        out_specs=[pl.BlockSpec(memory_space=pl.ANY)] * n_o,
        scratch_shapes=side.sems(),
        input_output_aliases=dict(side.alias),
    )(*side.ins)


_WGEO = {"a_in": ("col", 1024, 1280), "a_out": ("row", 256, 1024), "b_in": ("row", 1024, 1552), "b_out": ("row", 512, 1024),
         "f0_in": ("col", 1024, 1408), "f1_in": ("col", 1024, 1408), "f0_out": ("row", 704, 1024), "f1_out": ("row", 704, 1024)}
_GGEO = dict(_WGEO, b_in=("row", 1552, 1024))


def _full_shape(geo):
    kind, r, cw = geo
    return (r, 4 * cw) if kind == "col" else (4 * r, cw)


def _times(i, step):
    return i * step if isinstance(i, int) else pl.multiple_of(i * step, step & -step)


def _win(ref, geo, j, h):
    kind, r, cw = geo
    hr = r // 2
    if kind == "col":
        return ref.at[pl.ds(_times(h, hr), hr), pl.ds(_times(j, cw), cw)]
    return ref.at[pl.ds(_times(2 * j + h, hr), hr), :]


def _half(ref, geo, h):
    hr = geo[1] // 2
    return ref.at[pl.ds(_times(h, hr), hr), :]


def _gather_side(first, second, sh, full):
    n1 = len(first)

    def plan(ins, outs):
        x, y, c = _here()
        m = 2 * x + y
        remote, local = [], []
        for u, src, dst_full in zip(first, ins[:n1], outs[:n1]):
            mine, dst = _half(src, _WGEO[u], c), _win(dst_full, _WGEO[u], m, c)
            local.append((mine, dst))
            remote.append((mine, dst, (x, y, 1 - c)))
            for k in (1, 2, 3):
                t = (m + k) % 4
                remote.append((mine, dst, (t // 2, t % 2, c)))
        for u, buf in zip(second, outs[n1:]):
            for k in (1, 2, 3):
                w_ = _win(buf, _WGEO[u], (m + k) % 4, c)
                remote.append((w_, w_, (x, y, 1 - c)))
        return remote, local

    return _Side([sh[u] for u in first] + [full[u] for u in second],
                 [(_full_shape(_WGEO[u]), BF) for u in first + second], plan, 4 * n1 + 3 * len(second), n1,
                 alias={n1 + i: n1 + i for i in range(len(second))})


def _pair_side(units, G):
    def plan(ins, outs):
        x, y, c = _here()
        return [(_win(gr, _GGEO[u], j, 1 - c), got.at[j], (x, y, 1 - c))
                for u, gr, got in zip(units, ins, outs) for j in range(4)], []

    return _Side([G[u] for u in units], [((4, _GGEO[u][1] // 2, _GGEO[u][2]), F32) for u in units], plan, 4 * len(units), 0)


def _pair_sums(units, G, gots):
    out = {}
    for u, got in zip(units, gots):
        blk = got.shape[1:]
        at = (lambda j: (lax.axis_index("c"), j)) if _GGEO[u][0] == "col" else (lambda j: (2 * j + lax.axis_index("c"), 0))
        slab = lambda j: (j, 0, 0)
        out[u] = _vcall(f"g_pair_sum_{u}", lambda a, b: (a + b[0])[None], (4,),
                        [(G[u], blk, at), (got, (1,) + blk, slab)], [(got.shape, BF, (1,) + blk, slab)])
    return out


def _chips_side(units, chip_sums):
    def plan(ins, outs):
        x, y, c = _here()
        m = 2 * x + y
        remote = []
        for s, got in zip(ins, outs):
            for k in (1, 2, 3):
                t = (m + k) % 4
                remote.append((s.at[t], got.at[k - 1], (t // 2, t % 2, c)))
        return remote, []

    return _Side([chip_sums[u] for u in units], [((3,) + chip_sums[u].shape[1:], BF) for u in units], plan,
                 3 * len(units), 0)


def _chip_sum(name, geo, chip_sums, got, place):
    _, r, cw = geo
    blk = (r // 2, cw)
    return _vcall(name, lambda a, b: ((a[0].astype(F32) + b[0].astype(F32)) + b[1].astype(F32)) + b[2].astype(F32), (1,),
                  [(chip_sums, (1,) + blk, lambda i: (2 * lax.axis_index("x") + lax.axis_index("y"), 0, 0)),
                   (got, (3,) + blk, lambda i: (0, 0, 0))],
                  [((r, cw), F32, blk, lambda i: (lax.axis_index("c"), 0))])


def _halves_side(units, shards):
    def plan(ins, outs):
        x, y, c = _here()
        return [(_half(o, _GGEO[u], c), _half(o, _GGEO[u], c), (x, y, 1 - c)) for u, o in zip(units, outs)], []

    return _Side([shards[u] for u in units], [(shards[u].shape, F32) for u in units], plan, len(units), 0,
                 alias={i: i for i in range(len(units))})


def _adam(name, w, g, m, v):
    rows, cols = w.shape
    tb = _pick(rows, (256, 128, 64, 8))

    def fn(wv, gv, mv, vv):
        m2 = ADAM_B1 * mv + (1.0 - ADAM_B1) * gv
        v2 = ADAM_B2 * vv + (1.0 - ADAM_B2) * jnp.square(gv)
        m_hat = m2 / (1.0 - ADAM_B1 ** ADAM_STEP)
        v_hat = v2 / (1.0 - ADAM_B2 ** ADAM_STEP)
        return -ADAM_LR * (m_hat / (jnp.sqrt(v_hat) + ADAM_EPS) + ADAM_WD * wv), m2, v2

    at = lambda i: (i, 0)
    return _vcall(name, fn, (rows // tb,), [(a, (tb, cols), at) for a in (w, g, m, v)],
                  [((rows, cols), F32, (tb, cols), at)] * 3)


def _pack(arrays, width, row_multiple, dtype):
    parts, offs, at = [], [], 0
    for a in arrays:
        flat = a.reshape(-1).astype(dtype)
        rows = -(-flat.shape[0] // (width * row_multiple)) * row_multiple
        parts.append(jnp.pad(flat, (0, rows * width - flat.shape[0])).reshape(rows, width))
        offs.append(at)
        at += rows
    return jnp.concatenate(parts, axis=0), offs


def _unpack(flat, shapes, offs):
    out = []
    for shp, at in zip(shapes, offs):
        n = 1
        for s in shp:
            n *= s
        rows = -(-n // flat.shape[1])
        out.append(flat[at:at + rows].reshape(-1)[:n].reshape(shp))
    return out


_BIG = ("a_w_in", "a_w_out", "b_w_in", "b_w_out", "ffn_w_in", "ffn_w_out")
_BIG_AXIS = {"a_w_in": 2, "a_w_out": 1, "b_w_in": 2, "b_w_out": 1, "ffn_w_in": 2, "ffn_w_out": 1}
_SMALL_SPLIT = ("b_conv_w", "b_conv_b", "b_norm_w", "ffn_conv_w")
_SMALL = ("norm1_w", "norm2_w", "a_lb_logits", "a_norm_w", "b_conv_w", "b_conv_b", "b_dt_bias", "b_a_log", "b_d_skip",
          "b_norm_w", "ffn_conv_w", "ffn_conv_b", "final_norm_w")
_ORDER = ("norm1_w", "norm2_w", "a_w_in", "a_lb_logits", "a_norm_w", "a_w_out", "b_w_in", "b_conv_w", "b_conv_b", "b_dt_bias",
          "b_a_log", "b_d_skip", "b_norm_w", "b_w_out", "ffn_w_in", "ffn_conv_w", "ffn_conv_b", "ffn_w_out", "final_norm_w")


def kernel(x, norm1_w, norm2_w, a_w_in, a_lb_logits, a_norm_w, a_w_out, b_w_in, b_conv_w, b_conv_b, b_dt_bias, b_a_log, b_d_skip, b_norm_w, b_w_out, ffn_w_in, ffn_conv_w, ffn_conv_b, ffn_w_out, final_norm_w, loss_target, m_norm1_w, m_norm2_w, m_a_w_in, m_a_lb_logits, m_a_norm_w, m_a_w_out, m_b_w_in, m_b_conv_w, m_b_conv_b, m_b_dt_bias, m_b_a_log, m_b_d_skip, m_b_norm_w, m_b_w_out, m_ffn_w_in, m_ffn_conv_w, m_ffn_conv_b, m_ffn_w_out, m_final_norm_w, v_norm1_w, v_norm2_w, v_a_w_in, v_a_lb_logits, v_a_norm_w, v_a_w_out, v_b_w_in, v_b_conv_w, v_b_conv_b, v_b_dt_bias, v_b_a_log, v_b_d_skip, v_b_norm_w, v_b_w_out, v_ffn_w_in, v_ffn_conv_w, v_ffn_conv_b, v_ffn_w_out, v_final_norm_w):
    w = dict(norm1_w=norm1_w, norm2_w=norm2_w, a_w_in=a_w_in, a_lb_logits=a_lb_logits, a_norm_w=a_norm_w, a_w_out=a_w_out,
             b_w_in=b_w_in, b_conv_w=b_conv_w, b_conv_b=b_conv_b, b_dt_bias=b_dt_bias, b_a_log=b_a_log, b_d_skip=b_d_skip,
             b_norm_w=b_norm_w, b_w_out=b_w_out, ffn_w_in=ffn_w_in, ffn_conv_w=ffn_conv_w, ffn_conv_b=ffn_conv_b,
             ffn_w_out=ffn_w_out, final_norm_w=final_norm_w)
    mom = dict(norm1_w=m_norm1_w, norm2_w=m_norm2_w, a_w_in=m_a_w_in, a_lb_logits=m_a_lb_logits, a_norm_w=m_a_norm_w,
               a_w_out=m_a_w_out, b_w_in=m_b_w_in, b_conv_w=m_b_conv_w, b_conv_b=m_b_conv_b, b_dt_bias=m_b_dt_bias,
               b_a_log=m_b_a_log, b_d_skip=m_b_d_skip, b_norm_w=m_b_norm_w, b_w_out=m_b_w_out, ffn_w_in=m_ffn_w_in,
               ffn_conv_w=m_ffn_conv_w, ffn_conv_b=m_ffn_conv_b, ffn_w_out=m_ffn_w_out, final_norm_w=m_final_norm_w)
    var = dict(norm1_w=v_norm1_w, norm2_w=v_norm2_w, a_w_in=v_a_w_in, a_lb_logits=v_a_lb_logits, a_norm_w=v_a_norm_w,
               a_w_out=v_a_w_out, b_w_in=v_b_w_in, b_conv_w=v_b_conv_w, b_conv_b=v_b_conv_b, b_dt_bias=v_b_dt_bias,
               b_a_log=v_b_a_log, b_d_skip=v_b_d_skip, b_norm_w=v_b_norm_w, b_w_out=v_b_w_out, ffn_w_in=v_ffn_w_in,
               ffn_conv_w=v_ffn_conv_w, ffn_conv_b=v_ffn_conv_b, ffn_w_out=v_ffn_w_out, final_norm_w=v_final_norm_w)
    chip = 2 * lax.axis_index("x") + lax.axis_index("y")
    place = jnp.stack([chip, lax.axis_index("c")]).astype(jnp.int32)

    sh = {"a_in": a_w_in[0], "a_out": a_w_out[0], "b_in": b_w_in[0], "b_out": b_w_out[0], "f0_in": ffn_w_in[0],
          "f1_in": ffn_w_in[1], "f0_out": ffn_w_out[0], "f1_out": ffn_w_out[1]}
    sh = {u: a.astype(BF) for u, a in sh.items()}
    small_shapes = [w[n].shape for n in _SMALL_SPLIT]
    spack, small_offs = _pack([w[n] for n in _SMALL_SPLIT], 128, 8, F32)
    sall = _allgather8("s_gather", spack)
    sshards = [_unpack(sall[2 * j], small_shapes, small_offs) for j in range(4)]
    sfull = {n: jnp.concatenate([sshards[j][i] for j in range(4)], axis=-1) for i, n in enumerate(_SMALL_SPLIT)}

    p = dict(
        norm1_w=norm1_w, norm2_w=norm2_w, a_lb_logits=a_lb_logits, a_norm_w=a_norm_w[0], final_norm_w=final_norm_w,
        b_conv_w=sfull["b_conv_w"][0], b_conv_b=sfull["b_conv_b"][0], b_norm_w=sfull["b_norm_w"],
        ffn_conv_w=sfull["ffn_conv_w"], ffn_conv_b=ffn_conv_b,
        b_dt_bias=jnp.pad(b_dt_bias.reshape(1, 2 * SSM_HEADS), ((0, 0), (0, 128 - 2 * SSM_HEADS))),
        b_a_log=jnp.pad(b_a_log.reshape(1, 2 * SSM_HEADS), ((0, 0), (0, 128 - 2 * SSM_HEADS))),
        b_d_skip=jnp.repeat(b_d_skip[0], SSM_HD)[None],
    )

    loss_row, dx, g, gs_ = _sequence_grads(x[0], loss_target[0], p, sh, place)
    grads = {"a_w_in": gs_["a_in"][None], "a_w_out": gs_["a_out"][None], "b_w_in": gs_["b_in"].T[None],
             "b_w_out": gs_["b_out"][None], "ffn_w_in": jnp.stack([gs_["f0_in"], gs_["f1_in"]]),
             "ffn_w_out": jnp.stack([gs_["f0_out"], gs_["f1_out"]])}

    gsmall = {
        "norm1_w": jnp.concatenate(g["norm1_w"], axis=0), "norm2_w": jnp.concatenate(g["norm2_w"], axis=0),
        "a_lb_logits": jnp.stack(g["a_lb_logits"]), "a_norm_w": g["a_norm_w"], "b_conv_w": g["b_conv_w"],
        "b_conv_b": g["b_conv_b"], "b_dt_bias": g["b_dt_bias"], "b_a_log": g["b_a_log"], "b_d_skip": g["b_d_skip"],
        "b_norm_w": g["b_norm_w"], "ffn_conv_w": jnp.stack(g["ffn_conv_w"]),
        "ffn_conv_b": jnp.concatenate(g["ffn_conv_b"], axis=0), "final_norm_w": g["final_norm_w"],
    }
    pieces = [gsmall[n] for n in _SMALL] + [loss_row]
    piece_shapes = [a.shape for a in pieces]
    gspack, gs_offs = _pack(pieces, 128, 8, F32)
    rows = gspack.shape[0]
    gsall = _allgather8("gs_gather", gspack)

    def sum8(a):
        r = a[0]
        for i in range(1, 8):
            r = r + a[i]
        return r

    gssum = _vcall("gs_sum", sum8, (1,), [(gsall, (8, rows, 128), lambda i: (0, 0, 0))],
                   [((rows, 128), F32, (rows, 128), lambda i: (0, 0))])
    gs = dict(zip(_SMALL + ("loss",), _unpack(gssum, piece_shapes, gs_offs)))
    loss = gs["loss"][0, 0]
    lb2 = gs["a_lb_logits"]
    small_grads = {
        "norm1_w": gs["norm1_w"], "norm2_w": gs["norm2_w"], "a_lb_logits": lb2[0] + lb2[1], "a_norm_w": gs["a_norm_w"],
        "b_dt_bias": gs["b_dt_bias"][:, :2 * SSM_HEADS].reshape(1, 2, SSM_HEADS),
        "b_a_log": gs["b_a_log"][:, :2 * SSM_HEADS].reshape(1, 2, SSM_HEADS),
        "b_d_skip": gs["b_d_skip"].reshape(1, SSM_HEADS, SSM_HD).sum(axis=-1),
        "ffn_conv_b": gs["ffn_conv_b"], "final_norm_w": gs["final_norm_w"][0],
        "b_conv_w": gs["b_conv_w"][None], "b_conv_b": gs["b_conv_b"], "b_norm_w": gs["b_norm_w"], "ffn_conv_w": gs["ffn_conv_w"],
    }
    for n in _SMALL_SPLIT:
        width = w[n].shape[-1]
        small_grads[n] = lax.dynamic_slice_in_dim(small_grads[n], chip * width, width, axis=small_grads[n].ndim - 1)
    grads.update(small_grads)

    delta, new_m, new_v = {}, {}, {}
    for n in _BIG:
        shp = w[n].shape
        two_d = (shp[0] * shp[1], shp[2])
        d_, m_, v_ = _adam(f"adam_{n}", w[n].reshape(two_d), grads[n].reshape(two_d), mom[n].reshape(two_d), var[n].reshape(two_d))
        delta[n], new_m[n], new_v[n] = d_.reshape(shp), m_.reshape(shp), v_.reshape(shp)
    s_shapes = [w[n].shape for n in _SMALL]
    packs = [_pack([src[n] for n in _SMALL], 128, 8, F32) for src in (w, grads, mom, var)]
    outs = _adam("adam_small", *[pk[0] for pk in packs])
    for res, dst in zip(outs, (delta, new_m, new_v)):
        dst.update(dict(zip(_SMALL, _unpack(res, s_shapes, packs[0][1]))))

    return (loss, dx[None], *[grads[n] for n in _ORDER], *[delta[n] for n in _ORDER],
            *[new_m[n] for n in _ORDER], *[new_v[n] for n in _ORDER])
```

```python
import functools

import jax
import jax.numpy as jnp
from jax import lax
from jax.experimental import pallas as pl
from jax.experimental.pallas import tpu as pltpu

F32, BF = jnp.float32, jnp.bfloat16
HI = lax.Precision.HIGHEST

D = 1024
EPS = 1e-6
HG_HEADS, HG_HD, HG_CHUNK, HG_SUB = 8, 128, 64, 16
HG_HB = 8
SSM_GB = 4
D_INNER, SSM_HEADS, SSM_HD, SSM_GROUPS, SSM_HPG, SSM_N, SSD_CHUNK = 2048, 32, 64, 8, 4, 128, 128
CONV_DIM = D_INNER + 2 * SSM_GROUPS * SSM_N
B_PROJ = 2 * D_INNER + 2 * SSM_GROUPS * SSM_N + 2 * SSM_HEADS
B_PROJ_PAD = 6272
D_FF = 2816
NEG = -1e30
ROWS = 256
GATE_ROWS = 1024
VMEM_LIMIT = 56 * 1024 * 1024

ADAM_LR, ADAM_B1, ADAM_B2, ADAM_EPS, ADAM_WD, ADAM_STEP = 0.001, 0.9, 0.999, 1e-08, 0.01, 10

MESH = pl.DeviceIdType.MESH


def _pick(n, cands):
    for c in cands:
        if n % c == 0:
            return c
    return n


class _Side:
    def __init__(self, ins, outs, plan, n_remote, n_local, alias=None):
        self.ins, self.outs, self.plan, self.n_remote, self.n_local = list(ins), list(outs), plan, n_remote, n_local
        self.alias = alias or {}

    def copies(self, in_refs, out_refs, send_sems, recv_sems, local_sems):
        remote, local = self.plan(in_refs, out_refs)
        cps = [pltpu.make_async_copy(s, d, local_sems.at[i]) for i, (s, d) in enumerate(local)]
        cps += [pltpu.make_async_remote_copy(src_ref=s, dst_ref=d, send_sem=send_sems.at[i], recv_sem=recv_sems.at[i],
                                             device_id=dev, device_id_type=MESH)
                for i, (s, d, dev) in enumerate(remote)]
        return cps

    def sems(self):
        return [pltpu.SemaphoreType.DMA((self.n_remote,)), pltpu.SemaphoreType.DMA((self.n_remote,)),
                pltpu.SemaphoreType.DMA((max(self.n_local, 1),))]


def _vcall(name, fn, grid, ins, outs, acc=None, scratch=(), place=None, side=None, raw=False):
    acc = acc or {}
    n_in, n_out, nd = len(ins), len(outs), len(grid)
    n_pre = 0 if place is None else 1
    n_sin = len(side.ins) if side else 0
    n_sout = len(side.outs) if side else 0
    n_scr = len(scratch)

    def body(*refs):
        refs = refs[n_pre:]
        in_refs, refs = refs[:n_in], refs[n_in:]
        sin_refs, refs = refs[:n_sin], refs[n_sin:]
        out_refs, refs = refs[:n_out], refs[n_out:]
        sout_refs, refs = refs[:n_sout], refs[n_sout:]
        scr, sems = refs[:n_scr], refs[n_scr:]
        if side:
            at_first, at_last = None, None
            for ax in range(nd):
                f, l = pl.program_id(ax) == 0, pl.program_id(ax) == grid[ax] - 1
                at_first = f if at_first is None else jnp.logical_and(at_first, f)
                at_last = l if at_last is None else jnp.logical_and(at_last, l)

            @pl.when(at_first)
            def _():
                for cp in side.copies(sin_refs, sout_refs, *sems):
                    cp.start()

        if raw:
            fn(in_refs, out_refs, scr)
            res = ()
        else:
            res = fn(*[r[...] for r in in_refs], *scr)
            if not isinstance(res, (tuple, list)):
                res = (res,)
        if side:
            @pl.when(at_last)
            def _():
                for cp in side.copies(sin_refs, sout_refs, *sems):
                    cp.wait()
        for j, (o_ref, r) in enumerate(zip(out_refs, res)):
            mode = acc.get(j)
            if mode is None:
                o_ref[...] = r.astype(o_ref.dtype)
                continue
            first = pl.program_id(nd - 1) == 0
            if mode == "all":
                for ax in range(nd - 1):
                    first = jnp.logical_and(first, pl.program_id(ax) == 0)

            @pl.when(first)
            def _():
                o_ref[...] = r.astype(o_ref.dtype)

            @pl.when(jnp.logical_not(first))
            def _():
                o_ref[...] += r.astype(o_ref.dtype)

    hbm = pl.BlockSpec(memory_space=pl.ANY)
    in_specs = [pl.BlockSpec(bs, im) for _, bs, im in ins] + [hbm] * n_sin
    out_specs = [pl.BlockSpec(bs, im) for _, _, bs, im in outs] + [hbm] * n_sout
    params = pltpu.CompilerParams(dimension_semantics=("arbitrary",) * nd, vmem_limit_bytes=VMEM_LIMIT)
    out_shape = [jax.ShapeDtypeStruct(s, dt) for s, dt, _, _ in outs]
    operands = [a for a, _, _ in ins]
    scratch = list(scratch)
    aliases = {}
    if side:
        out_shape += [jax.ShapeDtypeStruct(s, dt) for s, dt in side.outs]
        operands += side.ins
        scratch += side.sems()
        aliases = {n_pre + n_in + i: n_out + o for i, o in side.alias.items()}
    if place is None:
        out = pl.pallas_call(body, name=name, grid=grid, in_specs=in_specs, out_specs=out_specs, out_shape=out_shape,
                             scratch_shapes=scratch, compiler_params=params, input_output_aliases=aliases)(*operands)
    else:
        spec = pltpu.PrefetchScalarGridSpec(num_scalar_prefetch=1, grid=grid, in_specs=in_specs, out_specs=out_specs,
                                            scratch_shapes=scratch)
        out = pl.pallas_call(body, name=name, grid_spec=spec, out_shape=out_shape, compiler_params=params,
                             input_output_aliases=aliases)(place, *operands)
    if side:
        return tuple(out[:n_out]), tuple(out[n_out:])
    return out[0] if n_out == 1 else out


def _mm(name, a, b, kind, out_dtype=F32, add=None, side=None):
    if kind == "tn":
        m, k = a.shape
        _, n = b.shape
        tm = _pick(m, (1024, 512, 256))
        tk = _pick(k, (1024, 1408, 896, 512, 256, 128))
        tn = _pick(n, (1024, 1408, 896, 512, 256, 128))

        def fn(av, bv):
            return lax.dot_general(av.astype(BF), bv.astype(BF), (((0,), (0,)), ((), ())),
                                   preferred_element_type=F32)

        return _vcall(name, fn, (k // tk, n // tn, m // tm),
                      [(a, (tm, tk), lambda i, j, s: (s, i)), (b, (tm, tn), lambda i, j, s: (s, j))],
                      [((k, n), F32, (tk, tn), lambda i, j, s: (i, j))], acc={0: "last"})
    m, k = a.shape
    n = b.shape[1] if kind == "nn" else b.shape[0]
    long_k = k > 4096
    tm = _pick(m, (512, 256)) if long_k else _pick(m, (1024, 512, 256))
    tn = _pick(n, (512, 896, 256, 128)) if long_k else _pick(n, (1024, 1408, 896, 512, 256, 128))
    dims =(((1,), (0,)), ((), ())) if kind == "nn" else (((1,), (1,)), ((), ()))

    def fn(av, bv, *rest):
        r = lax.dot_general(av.astype(BF), bv.astype(BF), dims, preferred_element_type=F32)
        return r + rest[0] if rest else r

    ins = [(a, (tm, k), lambda i, j: (i, 0)),
           (b, (k, tn), lambda i, j: (0, j)) if kind == "nn" else (b, (tn, k), lambda i, j: (j, 0))]
    if add is not None:
        ins.append((add, (tm, tn), lambda i, j: (i, j)))
    return _vcall(name, fn, (m // tm, n // tn), ins, [((m, n), out_dtype, (tm, tn), lambda i, j: (i, j))], side=side)


def _rms(h, w):
    return h * lax.rsqrt(jnp.mean(h * h, axis=-1, keepdims=True) + EPS) * w


def _rms_fwd(name, h, w):
    L = h.shape[0]
    tb = _pick(L, (ROWS,))
    return _vcall(name, _rms, (L // tb,),
                  [(h, (tb, D), lambda i: (i, 0)), (w.reshape(1, D), (1, D), lambda i: (0, 0))],
                  [((L, D), BF, (tb, D), lambda i: (i, 0))])


def _rms_bwd(name, du, h, w, dh_next, side=None):
    L = h.shape[0]
    tb = _pick(L, (ROWS,))

    def fn(duv, hv, wv, dnv):
        _, vjp = jax.vjp(_rms, hv, wv)
        dh, dw = vjp(duv)
        return dh + dnv, dw

    row = lambda i: (i, 0)
    return _vcall(name, fn, (L // tb,),
                  [(du, (tb, D), row), (h, (tb, D), row), (w.reshape(1, D), (1, D), lambda i: (0, 0)),
                   (dh_next, (tb, D), row)],
                  [((L, D), F32, (tb, D), row), ((1, D), F32, (1, D), lambda i: (0, 0))], acc={1: "all"}, side=side)


def _loss_head(name, h, tgt, w):
    L = h.shape[0]
    tb = _pick(L, (ROWS,))

    def lossf(hv, wv, tv):
        err = _rms(hv, wv) - tv
        return 0.5 * jnp.sum(err * err) * (1.0 / D)

    def fn(hv, wv, tv):
        val, vjp = jax.vjp(lambda a, b: lossf(a, b, tv), hv, wv)
        dh, dw = vjp(jnp.ones((), F32))
        return jnp.full((1, 128), val, F32), dh, dw

    row = lambda i: (i, 0)
    zero = lambda i: (0, 0)
    return _vcall(name, fn, (L // tb,),
                  [(h, (tb, D), row), (w.reshape(1, D), (1, D), zero), (tgt, (tb, D), row)],
                  [((1, 128), F32, (1, 128), zero), ((L, D), F32, (tb, D), row), ((1, D), F32, (1, D), zero)],
                  acc={0: "all", 2: "all"})


def _bf(x):
    return x.astype(BF)


def _dot(a, b, dims, precision=None):
    return lax.dot_general(a, b, (dims, ((), ())), preferred_element_type=F32, precision=precision)


def _tri(n, reverse):
    r = lax.broadcasted_iota(jnp.int32, (n, n), 0)
    c = lax.broadcasted_iota(jnp.int32, (n, n), 1)
    return (r <= c) if reverse else (r >= c)


def _tri_matmul(n, reverse, x):
    hi = x.astype(BF)
    r1 = x - hi.astype(F32)
    mid = r1.astype(BF)
    lo = (r1 - mid.astype(F32)).astype(BF)
    y = _dot(_tri(n, reverse).astype(BF), jnp.concatenate([hi, mid, lo], axis=1), ((1,), (0,)))
    w = x.shape[1]
    return (y[:, :w] + y[:, w:2 * w]) + y[:, 2 * w:]


@functools.partial(jax.custom_vjp, nondiff_argnums=(0, 1))
def _running_sum(n, reverse, x):
    return _tri_matmul(n, reverse, x)


def _running_sum_fwd(n, reverse, x):
    return _tri_matmul(n, reverse, x), None


def _running_sum_bwd(n, reverse, _, ct):
    return (_tri_matmul(n, not reverse, ct),)


_running_sum.defvjp(_running_sum_fwd, _running_sum_bwd)


def _gla_chunk(q_raw, f_raw, v, lb3, S, reverse):
    C, SB, HD = HG_CHUNK, HG_SUB, HG_HD
    H = S.shape[0]
    heads = [slice(HD * h, HD * (h + 1)) for h in range(H)]
    row3 = lax.broadcasted_iota(jnp.int32, (3, 1), 0)
    e = jnp.exp(lb3 - jnp.max(lb3, axis=0, keepdims=True))
    lb = jnp.sum(jnp.where(row3 == 0, e, 0.0), axis=0, keepdims=True) / jnp.sum(e, axis=0, keepdims=True)
    q = q_raw * jax.nn.sigmoid(q_raw)
    f = lb + (1.0 - lb) * jax.nn.sigmoid(f_raw)
    g = jnp.log(f)
    k = 1.0 - f
    b = _running_sum(C, reverse, g)
    row = lax.broadcasted_iota(jnp.int32, (C, 1), 0)
    vb = _bf(v)

    def rowof(x, t):
        return jnp.sum(jnp.where(row == t, x, 0.0), axis=0, keepdims=True)

    qe = _bf(q * jnp.exp(b))
    o = [_dot(qe[:, hs], _bf(S[h]), ((1,), (0,))) for h, hs in enumerate(heads)]
    att = [None] * H
    for i in range(C // SB):
        lo = SB * i
        if (not reverse and i == 0) or (reverse and i == C // SB - 1):
            continue
        first = lo + SB - 1 if reverse else lo
        r = rowof(b, first) - rowof(g, first)
        in_blk = jnp.logical_and(row >= lo, row < lo + SB)
        before = (row >= lo + SB) if reverse else (row < lo)
        qi = _bf(q * jnp.exp(jnp.where(in_blk, b - r, NEG)))
        kk = _bf(k * jnp.exp(jnp.where(before, r - b, NEG)))
        for h, hs in enumerate(heads):
            a_i = _dot(qi[:, hs], kk[:, hs], ((1,), (1,)))
            att[h] = a_i if att[h] is None else att[h] + a_i
    o = [o[h] + _dot(_bf(att[h]), vb[:, hs], ((1,), (0,))) for h, hs in enumerate(heads)]
    s_i = lax.broadcasted_iota(jnp.int32, (SB, SB, HD), 0)
    t_i = lax.broadcasted_iota(jnp.int32, (SB, SB, HD), 1)
    pair = (t_i <= s_i) if reverse else (t_i >= s_i)
    shp = (SB, SB, HD)
    diag = [[] for _ in range(H)]
    for i in range(C // SB):
        rows = slice(SB * i, SB * (i + 1))
        for h, hs in enumerate(heads):
            qb, kb, bb = q[rows, hs], k[rows, hs], b[rows, hs]
            dif = lax.broadcast_in_dim(bb, shp, (1, 2)) - lax.broadcast_in_dim(bb, shp, (0, 2))
            w = lax.broadcast_in_dim(qb, shp, (1, 2)) * jnp.exp(jnp.where(pair, dif, NEG)) * lax.broadcast_in_dim(kb, shp, (0, 2))
            d = jnp.sum(w, axis=2, keepdims=True)
            diag[h].append(jnp.sum(d * lax.broadcast_in_dim(v[rows, hs], shp, (0, 2)), axis=0))
    o = jnp.concatenate([o[h] + jnp.concatenate(diag[h], axis=0) for h in range(H)], axis=1)
    btot = rowof(b, 0 if reverse else C - 1)
    kd = _bf(k * jnp.exp(btot - b))
    eye = lax.broadcasted_iota(jnp.int32, (HD, HD), 0) == lax.broadcasted_iota(jnp.int32, (HD, HD), 1)
    s_new = []
    for h, hs in enumerate(heads):
        btot_col = jnp.sum(jnp.where(eye, btot[:, hs], 0.0), axis=1, keepdims=True)
        s_new.append((jnp.exp(btot_col) * S[h] + _dot(kd[:, hs], vb[:, hs], ((0,), (0,))))[None])
    return o, jnp.concatenate(s_new, axis=0)


def _gla_fwd(name, pa, lbl, reverse, side=None):
    L = pa.shape[0]
    C = HG_CHUNK
    nc = L // C
    cidx = (lambda i: nc - 1 - i) if reverse else (lambda i: i)
    sec = 2 if reverse else 1
    hb_n, nh = HG_HB, HG_HEADS // HG_HB

    def fn(qr, fr, v, lb3, s_ref):
        @pl.when(pl.program_id(1) == 0)
        def _():
            s_ref[...] = jnp.zeros_like(s_ref)

        s_all = s_ref[...]
        o, s_new = _gla_chunk(qr, fr, v, lb3, s_all, reverse)
        s_ref[...] = s_new
        return o, s_all[None]

    blk = (C, HG_HD * hb_n)
    return _vcall(name, fn, (nh, nc),
                  [(pa, blk, lambda h, i: (cidx(i), h)), (pa, blk, lambda h, i: (cidx(i), sec * nh + h)),
                   (pa, blk, lambda h, i: (cidx(i), 3 * nh + h)), (lbl, (3, HG_HD * hb_n), lambda h, i: (0, h))],
                  [((L, D), F32, blk, lambda h, i: (cidx(i), h)),
                   ((nc, HG_HEADS, HG_HD, HG_HD), F32, (1, hb_n, HG_HD, HG_HD), lambda h, i: (cidx(i), h, 0, 0))],
                  scratch=[pltpu.VMEM((hb_n, HG_HD, HG_HD), F32)], side=side)


def _gla_bwd(name, pa, lbl, s_in, do, reverse, prev=None, side=None):
    L = pa.shape[0]
    C = HG_CHUNK
    nc = L // C
    cidx = (lambda i: i) if reverse else (lambda i: nc - 1 - i)
    sec = 2 if reverse else 1
    n_prev = 0 if prev is None else 2
    hb_n, nh = HG_HB, HG_HEADS // HG_HB

    def fn(qr, fr, v, lb3, s, dov, *rest):
        ds_ref = rest[n_prev]

        @pl.when(pl.program_id(1) == 0)
        def _():
            ds_ref[...] = jnp.zeros_like(ds_ref)

        _, vjp = jax.vjp(lambda *a: _gla_chunk(*a, reverse), qr, fr, v, lb3, s[0])
        dq, df, dv, dlb, ds = vjp((dov, ds_ref[...]))
        ds_ref[...] = ds
        if n_prev:
            dq, dv = dq + rest[0], dv + rest[1]
        return dq, df, dv, dlb

    blk = (C, HG_HD * hb_n)
    at = lambda h, i: (cidx(i), h)
    ins = [(pa, blk, at), (pa, blk, lambda h, i: (cidx(i), sec * nh + h)), (pa, blk, lambda h, i: (cidx(i), 3 * nh + h)),
           (lbl, (3, HG_HD * hb_n), lambda h, i: (0, h)),
           (s_in, (1, hb_n, HG_HD, HG_HD), lambda h, i: (cidx(i), h, 0, 0)), (do, blk, at)]
    if prev is not None:
        ins += [(prev[0], blk, at), (prev[1], blk, at)]
    sum_dt = F32 if prev is None else BF
    return _vcall(name, fn, (nh, nc), ins,
                  [((L, D), sum_dt, blk, at), ((L, D), BF, blk, at), ((L, D), sum_dt, blk, at),
                   ((3, D), F32, (3, HG_HD * hb_n), lambda h, i: (0, h))],
                  acc={3: "last"}, scratch=[pltpu.VMEM((hb_n, HG_HD, HG_HD), F32)], side=side)


def _hgout(o_f, o_b, g, nw):
    o = o_f + o_b
    return _rms(o, nw) * (g * jax.nn.sigmoid(g))


def _hgout_fwd(name, o_f, o_b, pa, nw, side=None):
    L = o_f.shape[0]
    tb = _pick(L, (GATE_ROWS, ROWS))
    blk = (tb, HG_HD)
    at = lambda h, i: (i, h)
    return _vcall(name, _hgout, (HG_HEADS, L // tb),
                  [(o_f, blk, at), (o_b, blk, at), (pa, blk, lambda h, i: (i, 32 + h)),
                   (nw.reshape(1, HG_HD), (1, HG_HD), lambda h, i: (0, 0))],
                  [((L, D), BF, blk, at)], side=side)


def _hgout_bwd(name, o_f, o_b, pa, nw, dy, side=None):
    L = o_f.shape[0]
    tb = _pick(L, (GATE_ROWS, ROWS))

    def fn(ofv, obv, gv, nwv, dyv):
        _, vjp = jax.vjp(_hgout, ofv, obv, gv, nwv)
        do, _, dg, dnw = vjp(dyv)
        return do, dg, dnw

    blk = (tb, HG_HD)
    at = lambda h, i: (i, h)
    zero = lambda h, i: (0, 0)
    return _vcall(name, fn, (HG_HEADS, L // tb),
                  [(o_f, blk, at), (o_b, blk, at), (pa, blk, lambda h, i: (i, 32 + h)),
                   (nw.reshape(1, HG_HD), (1, HG_HD), zero), (dy, blk, at)],
                  [((L, D), F32, blk, at), ((L, D), BF, blk, at), ((1, HG_HD), F32, (1, HG_HD), zero)],
                  acc={2: "all"}, side=side)


CONV_TC = 128
CONV_HALO = 8


def _conv_rows(L):
    return _pick(L, (256, 64, 16, 8))


def _pad_copy(dst_ref, src_ref, L, R):
    tc = src_ref.shape[1]
    dst_ref[pl.ds(0, CONV_HALO)] = jnp.zeros((CONV_HALO, tc), F32)
    dst_ref[pl.ds(L + CONV_HALO, CONV_HALO)] = jnp.zeros((CONV_HALO, tc), F32)

    def cp(r, carry):
        dst_ref[pl.ds(pl.multiple_of(CONV_HALO + r * R, 8), R)] = src_ref[pl.ds(pl.multiple_of(r * R, 8), R)].astype(F32)
        return carry

    lax.fori_loop(0, L // R, cp, 0)


def _taps(pad_ref, r, R, offsets):
    win = pad_ref[pl.ds(pl.multiple_of(r * R, 8), R + 2 * CONV_HALO)]
    return [win[CONV_HALO + o:CONV_HALO + o + R] for o in offsets]


def _fold8(v):
    out = v[0:8]
    for i in range(1, v.shape[0] // 8):
        out = out + v[8 * i:8 * (i + 1)]
    return out


def _conv_fwd_call(name, x, w, b, extra, out, post):
    L, tc, kk = x[0].shape[0], CONV_TC, w.shape[0]
    p, R = (kk - 1) // 2, _conv_rows(x[0].shape[0])
    n_ex = len(extra)

    def fn(in_refs, out_refs, scr):
        x_ref, w_ref, b_ref = in_refs[:3]
        (xp_ref,), (o_ref,) = scr, out_refs
        _pad_copy(xp_ref, x_ref, L, R)
        wv, bv = w_ref[...], b_ref[...]

        def step(r, carry):
            c = bv
            for j, tap in enumerate(_taps(xp_ref, r, R, [j - p for j in range(kk)])):
                c = c + wv[j:j + 1] * tap
            rows = pl.ds(pl.multiple_of(r * R, 8), R)
            o_ref[rows] = post(c, *[e[rows] for e in in_refs[3:]]).astype(o_ref.dtype)
            return carry

        lax.fori_loop(0, L // R, step, 0)

    ncol = w.shape[1]
    col = lambda j: (0, j)
    return _vcall(name, fn, (ncol // tc,),
                  [(x[0], (L, tc), x[1]), (w, (kk, tc), col), (b.reshape(1, ncol), (1, tc), col)]
                  + [(a, (L, tc), m) for a, m in extra],
                  [(out[0], out[1], (L, tc), out[2])],
                  scratch=[pltpu.VMEM((L + 2 * CONV_HALO, tc), F32)], raw=True)


def _conv_bwd_call(name, x, w, b, extra, dx_out, others, mid, side=None):
    L, tc, kk = x[0].shape[0], CONV_TC, w.shape[0]
    p, R = (kk - 1) // 2, _conv_rows(x[0].shape[0])
    n_oth = len(others)

    def fn(in_refs, out_refs, scr):
        x_ref, w_ref, b_ref = in_refs[:3]
        xp_ref, dcp_ref = scr
        dx_ref, oth_refs, dw_ref, db_ref = out_refs[0], out_refs[1:1 + n_oth], out_refs[1 + n_oth], out_refs[2 + n_oth]
        _pad_copy(xp_ref, x_ref, L, R)
        dcp_ref[pl.ds(0, CONV_HALO)] = jnp.zeros((CONV_HALO, tc), F32)
        dcp_ref[pl.ds(L + CONV_HALO, CONV_HALO)] = jnp.zeros((CONV_HALO, tc), F32)
        wv, bv = w_ref[...], b_ref[...]

        def first(r, acc):
            taps = _taps(xp_ref, r, R, [j - p for j in range(kk)])
            c = bv
            for j, tap in enumerate(taps):
                c = c + wv[j:j + 1] * tap
            rows = pl.ds(pl.multiple_of(r * R, 8), R)
            res = mid(c, *[e[rows] for e in in_refs[3:]])
            dc = res[0]
            for o_ref, val in zip(oth_refs, res[1:]):
                o_ref[rows] = val.astype(o_ref.dtype)
            dcp_ref[pl.ds(pl.multiple_of(CONV_HALO + r * R, 8), R)] = dc
            return tuple(a + _fold8(dc * tap) for a, tap in zip(acc[:kk], taps)) + (acc[kk] + _fold8(dc),)

        acc = lax.fori_loop(0, L // R, first, tuple(jnp.zeros((8, tc), F32) for _ in range(kk + 1)))
        for j in range(kk):
            dw_ref[j:j + 1, :] = jnp.sum(acc[j], axis=0, keepdims=True)
        db_ref[...] = jnp.sum(acc[kk], axis=0, keepdims=True)

        def second(r, carry):
            dxv = None
            for j, tap in enumerate(_taps(dcp_ref, r, R, [p - j for j in range(kk)])):
                t = wv[j:j + 1] * tap
                dxv = t if dxv is None else dxv + t
            dx_ref[pl.ds(pl.multiple_of(r * R, 8), R)] = dxv.astype(dx_ref.dtype)
            return carry

        lax.fori_loop(0, L // R, second, 0)

    ncol = w.shape[1]
    col = lambda j: (0, j)
    outs = [(o[0], o[1], (L, tc), o[2]) for o in [dx_out] + list(others)]
    outs += [((kk, ncol), F32, (kk, tc), col), ((1, ncol), F32, (1, tc), col)]
    return _vcall(name, fn, (ncol // tc,),
                  [(x[0], (L, tc), x[1]), (w, (kk, tc), col), (b.reshape(1, ncol), (1, tc), col)]
                  + [(a, (L, tc), m) for a, m in extra], outs,
                  scratch=[pltpu.VMEM((L + 2 * CONV_HALO, tc), F32)] * 2, raw=True, side=side)


def _silu(c):
    return c * jax.nn.sigmoid(c)


def _silu_grad(c):
    s = jax.nn.sigmoid(c)
    return s * (1.0 + c * (1.0 - s))


def _glu_fwd(name, pf, cw, cb):
    L = pf.shape[0]
    nt = D_FF // CONV_TC
    col = lambda j: (0, j)
    return _conv_fwd_call(name, (pf, col), cw, cb, [(pf, lambda j: (0, nt + j))], ((L, D_FF), BF, col),
                          lambda c, val: _silu(c) * val)


def _glu_bwd(name, pf, cw, cb, dy, side=None):
    L = pf.shape[0]
    nt = D_FF // CONV_TC
    col = lambda j: (0, j)
    return _conv_bwd_call(name, (pf, col), cw, cb, [(pf, lambda j: (0, nt + j)), (dy, col)],
                          ((L, D_FF), BF, col), [((L, D_FF), BF, col)],
                          lambda c, val, dyv: (dyv * val * _silu_grad(c), dyv * _silu(c)), side=side)


def _perm_tile(j):
    return jnp.where(j < 16, 4 * (j // 2) + j % 2, jnp.where(j < 24, 4 * (j - 16) + 2, 4 * (j - 24) + 3))


def _mpre_fwd(name, pb, cw, cb):
    L = pb.shape[0]
    return _conv_fwd_call(name, (pb, lambda j: (0, 16 + j)), cw, cb, [],
                          ((L, CONV_DIM), F32, lambda j: (0, _perm_tile(j))), _silu)


def _mpre_bwd(name, pb, cw, cb, dact):
    L = pb.shape[0]
    return _conv_bwd_call(name, (pb, lambda j: (0, 16 + j)), cw, cb, [(dact, lambda j: (0, _perm_tile(j)))],
                          ((L, CONV_DIM), BF, lambda j: (0, j)), [], lambda c, da: (da * _silu_grad(c),))


def _softplus(x):
    return jnp.maximum(x, 0.0) + jnp.log(1.0 + jnp.exp(-jnp.abs(x)))


def _dt_fwd(name, pb, dtb, alog):
    L = pb.shape[0]
    tb = _pick(L, (1024, ROWS))

    def fn(x, bias, al):
        dt = _softplus(x + bias)
        return dt, dt * (-jnp.exp(al))

    row = lambda i: (i, 0)
    zero = lambda i: (0, 0)
    return _vcall(name, fn, (L // tb,),
                  [(pb, (tb, 128), lambda i: (i, 48)), (dtb, (1, 128), zero), (alog, (1, 128), zero)],
                  [((L, 128), F32, (tb, 128), row), ((L, 128), F32, (tb, 128), row)])


def _dt_bwd(name, pb, dtb, alog, ddt_f, dla_f, ddt_b, dla_b):
    L = pb.shape[0]
    tb = _pick(L, (1024, ROWS))

    def fn(x, bias, al, a1, b1, a2, b2):
        ddt = jnp.sum(a1, axis=0) + jnp.sum(a2, axis=0)
        dla = jnp.sum(b1, axis=0) + jnp.sum(b2, axis=0)
        z = x + bias
        dt = _softplus(z)
        a = -jnp.exp(al)
        dz = (ddt + dla * a) * jax.nn.sigmoid(z)
        return dz, jnp.sum(dz, axis=0, keepdims=True), jnp.sum(dla * dt, axis=0, keepdims=True) * a

    zero = lambda i: (0, 0)
    g3 = (ddt_f.shape[0], tb, 128)
    at3 = lambda i: (0, i, 0)
    return _vcall(name, fn, (L // tb,),
                  [(pb, (tb, 128), lambda i: (i, 48)), (dtb, (1, 128), zero), (alog, (1, 128), zero),
                   (ddt_f, g3, at3), (dla_f, g3, at3), (ddt_b, g3, at3), (dla_b, g3, at3)],
                  [((L, 128), BF, (tb, 128), lambda i: (i, 0)), ((1, 128), F32, (1, 128), zero),
                   ((1, 128), F32, (1, 128), zero)], acc={1: "all", 2: "all"})


def _split_dot(x, e, dims, pieces):
    hi = x.astype(BF)
    r1 = x - hi.astype(F32)
    mid = r1.astype(BF)
    y = _dot(hi, e, dims) + _dot(mid, e, dims)
    if pieces == 3:
        y = y + _dot((r1 - mid.astype(F32)).astype(BF), e, dims)
    return y


@functools.partial(jax.custom_vjp, nondiff_argnums=(2,))
def _spread(x, e, pieces):
    return _split_dot(x, e, ((1,), (0,)), pieces)


def _spread_fwd(x, e, pieces):
    return _split_dot(x, e, ((1,), (0,)), pieces), e


def _spread_bwd(pieces, e, ct):
    return _split_dot(ct, e, ((1,), (1,)), pieces), jnp.zeros_like(e)


_spread.defvjp(_spread_fwd, _spread_bwd)


def _ssd_chunk(xa, dt, la, hs, head0, reverse):
    C, P4, HD, N = SSD_CHUNK, SSM_HPG * SSM_HD, SSM_HD, SSM_N
    G = hs.shape[0]
    nh = SSM_HPG * G
    row = lax.broadcasted_iota(jnp.int32, (C, 1), 0)
    lane = lax.broadcasted_iota(jnp.int32, (1, 128), 1)
    eye = lax.broadcasted_iota(jnp.int32, (C, C), 0) == lax.broadcasted_iota(jnp.int32, (C, C), 1)
    tri = _tri(C, reverse)
    last = 0 if reverse else C - 1
    acum = _running_sum(C, reverse, la)
    atot = jnp.sum(jnp.where(row == last, acum, 0.0), axis=0, keepdims=True)
    src = lax.broadcasted_iota(jnp.int32, (128, 1), 0) - head0
    to_x = (src == lax.broadcasted_iota(jnp.int32, (1, nh * HD), 1) // HD).astype(BF)
    dt_x = _spread(dt, to_x, 2)
    ea_x = _spread(jnp.exp(acum), to_x, 2)
    dec_x = _spread(jnp.exp(atot - acum), to_x, 2)
    col_head = lax.broadcasted_iota(jnp.int32, (1, P4), 1) // HD
    row_head = lax.broadcasted_iota(jnp.int32, (P4, 1), 0) // HD
    ys, news = [], []
    for gi in range(G):
        xs = xa[:, 512 * gi:512 * gi + P4]
        bm = _bf(xa[:, 512 * gi + P4:512 * gi + P4 + N])
        cm = _bf(xa[:, 512 * gi + P4 + N:512 * (gi + 1)])
        gx = slice(P4 * gi, P4 * (gi + 1))
        cb = _dot(cm, bm, ((1,), (1,)))
        xd = xs * dt_x[:, gx]
        ms, xds, scale = [], [], 0.0
        for j in range(SSM_HPG):
            i = SSM_HPG * gi + j
            ac = jnp.sum(jnp.where(lane == head0 + i, acum, 0.0), axis=1, keepdims=True)
            ac_row = jnp.sum(jnp.where(eye, ac, 0.0), axis=0, keepdims=True)
            ms.append(_bf(cb * jnp.exp(jnp.where(tri, ac - ac_row, NEG))))
            xds.append(_bf(jnp.where(col_head == j, xd, 0.0)))
            a_i = jnp.sum(jnp.where(lane == head0 + i, atot, 0.0), axis=1, keepdims=True)
            scale = scale + jnp.where(row_head == j, jnp.exp(a_i), 0.0)
        y = _dot(jnp.concatenate(ms, axis=1), jnp.concatenate(xds, axis=0), ((1,), (0,)))
        y = y + _dot(cm, _bf(hs[gi]), ((1,), (1,))) * ea_x[:, gx]
        ys.append(y)
        news.append((scale * hs[gi] + _dot(_bf(xd * dec_x[:, gx]), bm, ((0,), (0,))))[None])
    return jnp.concatenate(ys, axis=1), jnp.concatenate(news, axis=0)


def _ssd_fwd(name, xact, dt, la, reverse):
    L = xact.shape[0]
    C = SSD_CHUNK
    nc = L // C
    cidx = (lambda i: nc - 1 - i) if reverse else (lambda i: i)
    base = SSM_HEADS if reverse else 0
    P4 = SSM_HPG * SSM_HD

    gb_n = SSM_GB

    def fn(xa, dtv, lav, h_ref):
        @pl.when(pl.program_id(1) == 0)
        def _():
            h_ref[...] = jnp.zeros_like(h_ref)

        h_all = h_ref[...]
        y, h_new = _ssd_chunk(xa, dtv, lav, h_all, base + SSM_HPG * gb_n * pl.program_id(0), reverse)
        h_ref[...] = h_new
        return y, h_all[None]

    return _vcall(name, fn, (SSM_GROUPS // gb_n, nc),
                  [(xact, (C, 512 * gb_n), lambda g, i: (cidx(i), g)), (dt, (C, 128), lambda g, i: (cidx(i), 0)),
                   (la, (C, 128), lambda g, i: (cidx(i), 0))],
                  [((L, D_INNER), F32, (C, P4 * gb_n), lambda g, i: (cidx(i), g)),
                   ((nc, SSM_GROUPS, P4, SSM_N), F32, (1, gb_n, P4, SSM_N), lambda g, i: (cidx(i), g, 0, 0))],
                  scratch=[pltpu.VMEM((gb_n, P4, SSM_N), F32)])


def _ssd_bwd(name, xact, dt, la, h_in, dy, reverse, prev_xs=None, prev_all=None, side=None):
    L = xact.shape[0]
    C = SSD_CHUNK
    nc = L // C
    cidx = (lambda i: i) if reverse else (lambda i: nc - 1 - i)
    base = SSM_HEADS if reverse else 0
    P4 = SSM_HPG * SSM_HD

    gb_n = SSM_GB

    def fn(xa, dtv, lav, hs, dyv, pv, dh_ref):
        @pl.when(pl.program_id(1) == 0)
        def _():
            dh_ref[...] = jnp.zeros_like(dh_ref)

        head0 = base + SSM_HPG * gb_n * pl.program_id(0)
        _, vjp = jax.vjp(lambda a, b, c, d: _ssd_chunk(a, b, c, d, head0, reverse), xa, dtv, lav, hs[0])
        dxa, ddt, dla, dh = vjp((dyv, dh_ref[...]))
        dh_ref[...] = dh
        if prev_all is not None:
            dxa = dxa + pv
        else:
            zeros = jnp.zeros((C, 2 * SSM_N), F32)
            dxa = dxa + jnp.concatenate([t for gb in range(gb_n) for t in (pv[:, P4 * gb:P4 * (gb + 1)], zeros)], axis=1)
        return dxa, ddt[None], dla[None]

    at = lambda g, i: (cidx(i), g)
    at0 = lambda g, i: (cidx(i), 0)
    pv = (prev_all, (C, 512 * gb_n), at) if prev_all is not None else (prev_xs, (C, P4 * gb_n), at)
    steps = SSM_GROUPS // gb_n
    return _vcall(name, fn, (steps, nc),
                  [(xact, (C, 512 * gb_n), at), (dt, (C, 128), at0), (la, (C, 128), at0),
                   (h_in, (1, gb_n, P4, SSM_N), lambda g, i: (cidx(i), g, 0, 0)), (dy, (C, P4 * gb_n), at), pv],
                  [((L, CONV_DIM), F32, (C, 512 * gb_n), at),
                   ((steps, L, 128), F32, (1, C, 128), lambda g, i: (g, cidx(i), 0)),
                   ((steps, L, 128), F32, (1, C, 128), lambda g, i: (g, cidx(i), 0))],
                  scratch=[pltpu.VMEM((gb_n, P4, SSM_N), F32)], side=side)


def _mpost(y_f, y_b, xs, z, dsk, nw):
    y = (y_f + y_b + xs * dsk) * (z * jax.nn.sigmoid(z))
    return _rms(y, nw)


def _mpost_fwd(name, y_f, y_b, xact, pb, dsk, nw):
    L = y_f.shape[0]
    tb = _pick(L, (GATE_ROWS, ROWS))
    blk = (tb, 256)
    at = lambda g, i: (i, g)
    par = lambda g, i: (0, g)
    return _vcall(name, _mpost, (SSM_GROUPS, L // tb),
                  [(y_f, blk, at), (y_b, blk, at), (xact, blk, lambda g, i: (i, 2 * g)), (pb, blk, at),
                   (dsk, (1, 256), par), (nw, (1, 256), par)],
                  [((L, D_INNER), BF, blk, at)])


def _mpost_bwd(name, y_f, y_b, xact, pb, dsk, nw, dy):
    L = y_f.shape[0]
    tb = _pick(L, (GATE_ROWS, ROWS))

    def fn(yf, yb, xs, z, dskv, nwv, dyv):
        _, vjp = jax.vjp(_mpost, yf, yb, xs, z, dskv, nwv)
        dyf, _, dxs, dz, ddsk, dnw = vjp(dyv)
        return dyf, dxs, dz, ddsk, dnw

    blk = (tb, 256)
    at = lambda g, i: (i, g)
    par = lambda g, i: (0, g)
    return _vcall(name, fn, (SSM_GROUPS, L // tb),
                  [(y_f, blk, at), (y_b, blk, at), (xact, blk, lambda g, i: (i, 2 * g)), (pb, blk, at),
                   (dsk, (1, 256), par), (nw, (1, 256), par), (dy, blk, at)],
                  [((L, D_INNER), F32, blk, at), ((L, D_INNER), F32, blk, at), ((L, D_INNER), BF, blk, at),
                   ((1, D_INNER), F32, (1, 256), par), ((1, D_INNER), F32, (1, 256), par)],
                  acc={3: "last", 4: "last"})


def _ffn_fwd(tag, h, nw, w_in, cw, cb, w_out):
    u = _rms_fwd(f"{tag}_norm", h, nw)
    pf = _mm(f"{tag}_in", u, w_in, "nn")
    yf = _glu_fwd(f"{tag}_glu", pf, cw, cb)
    return _mm(f"{tag}_out", yf, w_out, "nn", add=h), (u, pf, yf)


def _ffn_bwd(tag, h, nw, w_in, cw, cb, w_out, saved, dh, side=None):
    u, pf, yf = saved
    d_w_out = _mm(f"{tag}_dwout", yf, dh, "tn")
    dyf = _mm(f"{tag}_dy", dh, w_out, "nt")
    got = ()
    if side is None:
        dgate, dval, dcw, dcb = _glu_bwd(f"{tag}_dglu", pf, cw, cb, dyf)
    else:
        (dgate, dval, dcw, dcb), got = _glu_bwd(f"{tag}_dglu", pf, cw, cb, dyf, side=side)
    dpf = jnp.concatenate([dgate, dval], axis=1)
    d_w_in = _mm(f"{tag}_dwin", u, dpf, "tn")
    du = _mm(f"{tag}_du", dpf, w_in, "nt")
    dh_in, dnw = _rms_bwd(f"{tag}_dnorm", du, h, nw, dh)
    return dh_in, dnw, d_w_in, dcw, dcb, d_w_out, got


def _sequence_grads(x, tgt, p, sh, place):
    g = {}
    lbl = p["a_lb_logits"]
    first, mid, last = ("a_in", "a_out"), ("f0_in", "f0_out", "b_in", "b_out"), ("f1_in", "f1_out")
    W = {}
    got = _exchange("w_first", _gather_side(first, (), sh, W))
    W.update(zip(first, got))
    got = _exchange("w_first_pass", _gather_side((), first, sh, W))
    W.update(zip(first, got))
    u1 = _rms_fwd("a_norm", x, p["norm1_w"][0])
    pa = _mm("a_in", u1, W["a_in"], "nn")
    (o_f, s_f), got = _gla_fwd("a_scan_f", pa, lbl, False, side=_gather_side(mid, (), sh, W))
    W.update(zip(mid, got))
    (o_b, s_b), got = _gla_fwd("a_scan_b", pa, lbl, True, side=_gather_side(last, mid, sh, W))
    W.update(zip(last + mid, got))
    (ya,), got = _hgout_fwd("a_gate", o_f, o_b, pa, p["a_norm_w"], side=_gather_side((), last, sh, W))
    W.update(zip(last, got))
    wb4 = W["b_in"].reshape(4, D, B_PROJ // 4)
    p = dict(p, a_w_in=W["a_in"], a_w_out=W["a_out"], b_w_out=W["b_out"], ffn_w_in=(W["f0_in"], W["f1_in"]),
             ffn_w_out=(W["f0_out"], W["f1_out"]),
             b_w_in=jnp.pad(jnp.concatenate([wb4[j] for j in range(4)], axis=1), ((0, 0), (0, B_PROJ_PAD - B_PROJ))))
    h1 = _mm("a_out", ya, p["a_w_out"], "nn", add=x)
    h2, ffn0 = _ffn_fwd("f0", h1, p["norm2_w"][0], p["ffn_w_in"][0], p["ffn_conv_w"][0], p["ffn_conv_b"][0], p["ffn_w_out"][0])
    u3 = _rms_fwd("b_norm", h2, p["norm1_w"][1])
    pb = _mm("b_in", u3, p["b_w_in"], "nn")
    xact = _mpre_fwd("b_conv", pb, p["b_conv_w"], p["b_conv_b"])
    dt, la = _dt_fwd("b_dt", pb, p["b_dt_bias"], p["b_a_log"])
    y_f, hs_f = _ssd_fwd("b_scan_f", xact, dt, la, False)
    y_b, hs_b = _ssd_fwd("b_scan_b", xact, dt, la, True)
    yb = _mpost_fwd("b_gate", y_f, y_b, xact, pb, p["b_d_skip"], p["b_norm_w"])
    h3 = _mm("b_out", yb, p["b_w_out"], "nn", add=h2)
    h4, ffn1 = _ffn_fwd("f1", h3, p["norm2_w"][1], p["ffn_w_in"][1], p["ffn_conv_w"][1], p["ffn_conv_b"][1], p["ffn_w_out"][1])
    loss, dh4, g["final_norm_w"] = _loss_head("head", h4, tgt, p["final_norm_w"])
    dh3, dn2_1, dwin1, dcw1, dcb1, dwout1, _ = _ffn_bwd("f1", h3, p["norm2_w"][1], p["ffn_w_in"][1], p["ffn_conv_w"][1],
                                                        p["ffn_conv_b"][1], p["ffn_w_out"][1], ffn1, dh4)
    G = {"f1_in": dwin1, "f1_out": dwout1}
    G["b_out"] = _mm("b_dwout", yb, dh3, "tn")
    dyb = _mm("b_dy", dh3, p["b_w_out"], "nt")
    dys, dxs, dz, g["b_d_skip"], g["b_norm_w"] = _mpost_bwd("b_dgate", y_f, y_b, xact, pb, p["b_d_skip"], p["b_norm_w"], dyb)
    wave1 = ("f1_in", "f1_out", "b_out")
    (dxa1, ddt_f, dla_f), got = _ssd_bwd("b_dscan_f", xact, dt, la, hs_f, dys, False, prev_xs=dxs, side=_pair_side(wave1, G))
    chip_sums = _pair_sums(wave1, G, got)
    dxa, ddt_b, dla_b = _ssd_bwd("b_dscan_b", xact, dt, la, hs_b, dys, True, prev_all=dxa1)
    dxbc, g["b_conv_w"], g["b_conv_b"] = _mpre_bwd("b_dconv", pb, p["b_conv_w"], p["b_conv_b"], dxa)
    ddtr, g["b_dt_bias"], g["b_a_log"] = _dt_bwd("b_ddt", pb, p["b_dt_bias"], p["b_a_log"], ddt_f, dla_f, ddt_b, dla_b)
    dpb = jnp.concatenate([dz, dxbc, ddtr], axis=1)
    G["b_in"] = _mm("b_dwin", dpb, u3, "tn")
    du3 = _mm("b_du", dpb, p["b_w_in"], "nt")
    dh2, dn1_1 = _rms_bwd("b_dnorm", du3, h2, p["norm1_w"][1], dh3)
    dh1, dn2_0, G["f0_in"], dcw0, dcb0, G["f0_out"], got = _ffn_bwd("f0", h1, p["norm2_w"][0], p["ffn_w_in"][0], p["ffn_conv_w"][0],
                                                                   p["ffn_conv_b"][0], p["ffn_w_out"][0], ffn0, dh2,
                                                                   side=_pair_side(("b_in",), G))
    chip_sums.update(_pair_sums(("b_in",), G, got))
    wave3 = ("f0_in", "f0_out")
    G["a_out"] = _mm("a_dwout", ya, dh1, "tn")
    dya = _mm("a_dy", dh1, p["a_w_out"], "nt")
    (do, dg, g["a_norm_w"]), got = _hgout_bwd("a_dgate", o_f, o_b, pa, p["a_norm_w"], dya, side=_pair_side(wave3, G))
    chip_sums.update(_pair_sums(wave3, G, got))
    late = last + mid
    (dq1, df1, dv1, dl1), got = _gla_bwd("a_dscan_f", pa, lbl, s_f, do, False, side=_chips_side(late, chip_sums))
    shards = {u: _chip_sum(f"gl_sum_{u}", _GGEO[u], chip_sums[u], r, place) for u, r in zip(late, got)}
    (dq, df2, dv, dl2), got = _gla_bwd("a_dscan_b", pa, lbl, s_b, do, True, prev=(dq1, dv1), side=_halves_side(late, shards))
    shards = dict(zip(late, got))
    dpa = jnp.concatenate([dq, df1, df2, dv, dg], axis=1)
    G["a_in"] = _mm("a_dwin", u1, dpa, "tn")
    chip_sums = _pair_sums(first, G, _exchange("ga_pair", _pair_side(first, G)))
    (du1,), got = _mm("a_du", dpa, p["a_w_in"], "nt", side=_chips_side(first, chip_sums))
    mine = {u: _chip_sum(f"ga_sum_{u}", _GGEO[u], chip_sums[u], r, place) for u, r in zip(first, got)}
    (dx, dn1_0), got = _rms_bwd("a_dnorm", du1, x, p["norm1_w"][0], dh1, side=_halves_side(first, mine))
    shards.update(zip(first, got))
    g["a_lb_logits"] = (dl1, dl2)
    g["norm1_w"] = (dn1_0, dn1_1)
    g["norm2_w"] = (dn2_0, dn2_1)
    g["ffn_conv_w"] = (dcw0, dcw1)
    g["ffn_conv_b"] = (dcb0, dcb1)
    return loss, dx, g, shards


def _here():
    return lax.axis_index("x"), lax.axis_index("y"), lax.axis_index("c")


def _allgather8(name, src, by_core=False):
    blk = src.shape[1:] if by_core else src.shape

    def body(x_ref, out_ref, send_sems, recv_sems, local_sem):
        x, y, c = _here()
        me, sibling = (x, y, c), (x, y, 1 - c)
        chips = [(1 - x, y), (x, 1 - y), (1 - x, 1 - y)]
        own = x_ref.at[c] if by_core else x_ref

        def slot(px, py, pc):
            return out_ref.at[4 * px + 2 * py + pc]

        def copy(k, block, to, from_own=False):
            return pltpu.make_async_remote_copy(
                src_ref=own if from_own else slot(*block), dst_ref=slot(*block),
                send_sem=send_sems.at[k], recv_sem=recv_sems.at[k], device_id=to, device_id_type=MESH)

        mine = pltpu.make_async_copy(own, slot(*me), local_sem)
        mine.start()
        first = [copy(0, me, sibling, from_own=True)]
        first += [copy(1 + j, me, (*chip, c), from_own=True) for j, chip in enumerate(chips)]
        for cp in first:
            cp.start()
        passed = [copy(4 + j, (*chip, c), sibling) for j, chip in enumerate(chips)]
        for j, chip in enumerate(chips):
            copy(1 + j, (*chip, c), me).wait_recv()
            passed[j].start()
        copy(0, sibling, me).wait_recv()
        for j, chip in enumerate(chips):
            copy(4 + j, (*chip, 1 - c), me).wait_recv()
        for cp in first + passed:
            cp.wait_send()
        mine.wait()

    return pl.pallas_call(
        body, name=name,
        out_shape=jax.ShapeDtypeStruct((8,) + tuple(blk), src.dtype),
        in_specs=[pl.BlockSpec(memory_space=pl.ANY)],
        out_specs=pl.BlockSpec(memory_space=pl.ANY),
        scratch_shapes=[pltpu.SemaphoreType.DMA((7,)), pltpu.SemaphoreType.DMA((7,)), pltpu.SemaphoreType.DMA],
    )(src)


def _exchange(name, side):
    n_i, n_o = len(side.ins), len(side.outs)

    def body(*refs):
        copies = side.copies(refs[:n_i], refs[n_i:n_i + n_o], *refs[n_i + n_o:])
        for cp in copies:
            cp.start()
        for cp in copies:
            cp.wait()

    return pl.pallas_call(
        body, name=name,
        out_shape=[jax.ShapeDtypeStruct(s, dt) for s, dt in side.outs],
        in_specs=[pl.BlockSpec(memory_space=pl.ANY)] * n_i,
        out_specs=[pl.BlockSpec(memory_space=pl.ANY)] * n_o,
        scratch_shapes=side.sems(),
        input_output_aliases=dict(side.alias),
    )(*side.ins)


_WGEO = {"a_in": ("col", 1024, 1280), "a_out": ("row", 256, 1024), "b_in": ("row", 1024, 1552), "b_out": ("row", 512, 1024),
         "f0_in": ("col", 1024, 1408), "f1_in": ("col", 1024, 1408), "f0_out": ("row", 704, 1024), "f1_out": ("row", 704, 1024)}
_GGEO = dict(_WGEO, b_in=("row", 1552, 1024))


def _full_shape(geo):
    kind, r, cw = geo
    return (r, 4 * cw) if kind == "col" else (4 * r, cw)


def _times(i, step):
    return i * step if isinstance(i, int) else pl.multiple_of(i * step, step & -step)


def _win(ref, geo, j, h):
    kind, r, cw = geo
    hr = r // 2
    if kind == "col":
        return ref.at[pl.ds(_times(h, hr), hr), pl.ds(_times(j, cw), cw)]
    return ref.at[pl.ds(_times(2 * j + h, hr), hr), :]


def _half(ref, geo, h):
    hr = geo[1] // 2
    return ref.at[pl.ds(_times(h, hr), hr), :]


def _gather_side(first, second, sh, full):
    n1 = len(first)

    def plan(ins, outs):
        x, y, c = _here()
        m = 2 * x + y
        remote, local = [], []
        for u, src, dst_full in zip(first, ins[:n1], outs[:n1]):
            mine, dst = _half(src, _WGEO[u], c), _win(dst_full, _WGEO[u], m, c)
            local.append((mine, dst))
            remote.append((mine, dst, (x, y, 1 - c)))
            for k in (1, 2, 3):
                t = (m + k) % 4
                remote.append((mine, dst, (t // 2, t % 2, c)))
        for u, buf in zip(second, outs[n1:]):
            for k in (1, 2, 3):
                w_ = _win(buf, _WGEO[u], (m + k) % 4, c)
                remote.append((w_, w_, (x, y, 1 - c)))
        return remote, local

    return _Side([sh[u] for u in first] + [full[u] for u in second],
                 [(_full_shape(_WGEO[u]), BF) for u in first + second], plan, 4 * n1 + 3 * len(second), n1,
                 alias={n1 + i: n1 + i for i in range(len(second))})


def _pair_side(units, G):
    def plan(ins, outs):
        x, y, c = _here()
        return [(_win(gr, _GGEO[u], j, 1 - c), got.at[j], (x, y, 1 - c))
                for u, gr, got in zip(units, ins, outs) for j in range(4)], []

    return _Side([G[u] for u in units], [((4, _GGEO[u][1] // 2, _GGEO[u][2]), F32) for u in units], plan, 4 * len(units), 0)


def _pair_sums(units, G, gots):
    out = {}
    for u, got in zip(units, gots):
        blk = got.shape[1:]
        at = (lambda j: (lax.axis_index("c"), j)) if _GGEO[u][0] == "col" else (lambda j: (2 * j + lax.axis_index("c"), 0))
        slab = lambda j: (j, 0, 0)
        out[u] = _vcall(f"g_pair_sum_{u}", lambda a, b: (a + b[0])[None], (4,),
                        [(G[u], blk, at), (got, (1,) + blk, slab)], [(got.shape, BF, (1,) + blk, slab)])
    return out


def _chips_side(units, chip_sums):
    def plan(ins, outs):
        x, y, c = _here()
        m = 2 * x + y
        remote = []
        for s, got in zip(ins, outs):
            for k in (1, 2, 3):
                t = (m + k) % 4
                remote.append((s.at[t], got.at[k - 1], (t // 2, t % 2, c)))
        return remote, []

    return _Side([chip_sums[u] for u in units], [((3,) + chip_sums[u].shape[1:], BF) for u in units], plan,
                 3 * len(units), 0)


def _chip_sum(name, geo, chip_sums, got, place):
    _, r, cw = geo
    blk = (r // 2, cw)
    return _vcall(name, lambda a, b: ((a[0].astype(F32) + b[0].astype(F32)) + b[1].astype(F32)) + b[2].astype(F32), (1,),
                  [(chip_sums, (1,) + blk, lambda i: (2 * lax.axis_index("x") + lax.axis_index("y"), 0, 0)),
                   (got, (3,) + blk, lambda i: (0, 0, 0))],
                  [((r, cw), F32, blk, lambda i: (lax.axis_index("c"), 0))])


def _halves_side(units, shards):
    def plan(ins, outs):
        x, y, c = _here()
        return [(_half(o, _GGEO[u], c), _half(o, _GGEO[u], c), (x, y, 1 - c)) for u, o in zip(units, outs)], []

    return _Side([shards[u] for u in units], [(shards[u].shape, F32) for u in units], plan, len(units), 0,
                 alias={i: i for i in range(len(units))})


def _adam(name, w, g, m, v):
    rows, cols = w.shape
    tb = _pick(rows, (256, 128, 64, 8))

    def fn(wv, gv, mv, vv):
        m2 = ADAM_B1 * mv + (1.0 - ADAM_B1) * gv
        v2 = ADAM_B2 * vv + (1.0 - ADAM_B2) * jnp.square(gv)
        m_hat = m2 / (1.0 - ADAM_B1 ** ADAM_STEP)
        v_hat = v2 / (1.0 - ADAM_B2 ** ADAM_STEP)
        return -ADAM_LR * (m_hat / (jnp.sqrt(v_hat) + ADAM_EPS) + ADAM_WD * wv), m2, v2

    at = lambda i: (i, 0)
    return _vcall(name, fn, (rows // tb,), [(a, (tb, cols), at) for a in (w, g, m, v)],
                  [((rows, cols), F32, (tb, cols), at)] * 3)


def _pack(arrays, width, row_multiple, dtype):
    parts, offs, at = [], [], 0
    for a in arrays:
        flat = a.reshape(-1).astype(dtype)
        rows = -(-flat.shape[0] // (width * row_multiple)) * row_multiple
        parts.append(jnp.pad(flat, (0, rows * width - flat.shape[0])).reshape(rows, width))
        offs.append(at)
        at += rows
    return jnp.concatenate(parts, axis=0), offs


def _unpack(flat, shapes, offs):
    out = []
    for shp, at in zip(shapes, offs):
        n = 1
        for s in shp:
            n *= s
        rows = -(-n // flat.shape[1])
        out.append(flat[at:at + rows].reshape(-1)[:n].reshape(shp))
    return out


_BIG = ("a_w_in", "a_w_out", "b_w_in", "b_w_out", "ffn_w_in", "ffn_w_out")
_BIG_AXIS = {"a_w_in": 2, "a_w_out": 1, "b_w_in": 2, "b_w_out": 1, "ffn_w_in": 2, "ffn_w_out": 1}
_SMALL_SPLIT = ("b_conv_w", "b_conv_b", "b_norm_w", "ffn_conv_w")
_SMALL = ("norm1_w", "norm2_w", "a_lb_logits", "a_norm_w", "b_conv_w", "b_conv_b", "b_dt_bias", "b_a_log", "b_d_skip",
          "b_norm_w", "ffn_conv_w", "ffn_conv_b", "final_norm_w")
_ORDER = ("norm1_w", "norm2_w", "a_w_in", "a_lb_logits", "a_norm_w", "a_w_out", "b_w_in", "b_conv_w", "b_conv_b", "b_dt_bias",
          "b_a_log", "b_d_skip", "b_norm_w", "b_w_out", "ffn_w_in", "ffn_conv_w", "ffn_conv_b", "ffn_w_out", "final_norm_w")


def kernel(x, norm1_w, norm2_w, a_w_in, a_lb_logits, a_norm_w, a_w_out, b_w_in, b_conv_w, b_conv_b, b_dt_bias, b_a_log, b_d_skip, b_norm_w, b_w_out, ffn_w_in, ffn_conv_w, ffn_conv_b, ffn_w_out, final_norm_w, loss_target, m_norm1_w, m_norm2_w, m_a_w_in, m_a_lb_logits, m_a_norm_w, m_a_w_out, m_b_w_in, m_b_conv_w, m_b_conv_b, m_b_dt_bias, m_b_a_log, m_b_d_skip, m_b_norm_w, m_b_w_out, m_ffn_w_in, m_ffn_conv_w, m_ffn_conv_b, m_ffn_w_out, m_final_norm_w, v_norm1_w, v_norm2_w, v_a_w_in, v_a_lb_logits, v_a_norm_w, v_a_w_out, v_b_w_in, v_b_conv_w, v_b_conv_b, v_b_dt_bias, v_b_a_log, v_b_d_skip, v_b_norm_w, v_b_w_out, v_ffn_w_in, v_ffn_conv_w, v_ffn_conv_b, v_ffn_w_out, v_final_norm_w):
    w = dict(norm1_w=norm1_w, norm2_w=norm2_w, a_w_in=a_w_in, a_lb_logits=a_lb_logits, a_norm_w=a_norm_w, a_w_out=a_w_out,
             b_w_in=b_w_in, b_conv_w=b_conv_w, b_conv_b=b_conv_b, b_dt_bias=b_dt_bias, b_a_log=b_a_log, b_d_skip=b_d_skip,
             b_norm_w=b_norm_w, b_w_out=b_w_out, ffn_w_in=ffn_w_in, ffn_conv_w=ffn_conv_w, ffn_conv_b=ffn_conv_b,
             ffn_w_out=ffn_w_out, final_norm_w=final_norm_w)
    mom = dict(norm1_w=m_norm1_w, norm2_w=m_norm2_w, a_w_in=m_a_w_in, a_lb_logits=m_a_lb_logits, a_norm_w=m_a_norm_w,
               a_w_out=m_a_w_out, b_w_in=m_b_w_in, b_conv_w=m_b_conv_w, b_conv_b=m_b_conv_b, b_dt_bias=m_b_dt_bias,
               b_a_log=m_b_a_log, b_d_skip=m_b_d_skip, b_norm_w=m_b_norm_w, b_w_out=m_b_w_out, ffn_w_in=m_ffn_w_in,
               ffn_conv_w=m_ffn_conv_w, ffn_conv_b=m_ffn_conv_b, ffn_w_out=m_ffn_w_out, final_norm_w=m_final_norm_w)
    var = dict(norm1_w=v_norm1_w, norm2_w=v_norm2_w, a_w_in=v_a_w_in, a_lb_logits=v_a_lb_logits, a_norm_w=v_a_norm_w,
               a_w_out=v_a_w_out, b_w_in=v_b_w_in, b_conv_w=v_b_conv_w, b_conv_b=v_b_conv_b, b_dt_bias=v_b_dt_bias,
               b_a_log=v_b_a_log, b_d_skip=v_b_d_skip, b_norm_w=v_b_norm_w, b_w_out=v_b_w_out, ffn_w_in=v_ffn_w_in,
               ffn_conv_w=v_ffn_conv_w, ffn_conv_b=v_ffn_conv_b, ffn_w_out=v_ffn_w_out, final_norm_w=v_final_norm_w)
    chip = 2 * lax.axis_index("x") + lax.axis_index("y")
    place = jnp.stack([chip, lax.axis_index("c")]).astype(jnp.int32)

    sh = {"a_in": a_w_in[0], "a_out": a_w_out[0], "b_in": b_w_in[0], "b_out": b_w_out[0], "f0_in": ffn_w_in[0],
          "f1_in": ffn_w_in[1], "f0_out": ffn_w_out[0], "f1_out": ffn_w_out[1]}
    sh = {u: a.astype(BF) for u, a in sh.items()}
    small_shapes = [w[n].shape for n in _SMALL_SPLIT]
    spack, small_offs = _pack([w[n] for n in _SMALL_SPLIT], 128, 8, F32)
    sall = _allgather8("s_gather", spack)
    sshards = [_unpack(sall[2 * j], small_shapes, small_offs) for j in range(4)]
    sfull = {n: jnp.concatenate([sshards[j][i] for j in range(4)], axis=-1) for i, n in enumerate(_SMALL_SPLIT)}

    p = dict(
        norm1_w=norm1_w, norm2_w=norm2_w, a_lb_logits=a_lb_logits, a_norm_w=a_norm_w[0], final_norm_w=final_norm_w,
        b_conv_w=sfull["b_conv_w"][0], b_conv_b=sfull["b_conv_b"][0], b_norm_w=sfull["b_norm_w"],
        ffn_conv_w=sfull["ffn_conv_w"], ffn_conv_b=ffn_conv_b,
        b_dt_bias=jnp.pad(b_dt_bias.reshape(1, 2 * SSM_HEADS), ((0, 0), (0, 128 - 2 * SSM_HEADS))),
        b_a_log=jnp.pad(b_a_log.reshape(1, 2 * SSM_HEADS), ((0, 0), (0, 128 - 2 * SSM_HEADS))),
        b_d_skip=jnp.repeat(b_d_skip[0], SSM_HD)[None],
    )

    loss_row, dx, g, gs_ = _sequence_grads(x[0], loss_target[0], p, sh, place)
    grads = {"a_w_in": gs_["a_in"][None], "a_w_out": gs_["a_out"][None], "b_w_in": gs_["b_in"].T[None],
             "b_w_out": gs_["b_out"][None], "ffn_w_in": jnp.stack([gs_["f0_in"], gs_["f1_in"]]),
             "ffn_w_out": jnp.stack([gs_["f0_out"], gs_["f1_out"]])}

    gsmall = {
        "norm1_w": jnp.concatenate(g["norm1_w"], axis=0), "norm2_w": jnp.concatenate(g["norm2_w"], axis=0),
        "a_lb_logits": jnp.stack(g["a_lb_logits"]), "a_norm_w": g["a_norm_w"], "b_conv_w": g["b_conv_w"],
        "b_conv_b": g["b_conv_b"], "b_dt_bias": g["b_dt_bias"], "b_a_log": g["b_a_log"], "b_d_skip": g["b_d_skip"],
        "b_norm_w": g["b_norm_w"], "ffn_conv_w": jnp.stack(g["ffn_conv_w"]),
        "ffn_conv_b": jnp.concatenate(g["ffn_conv_b"], axis=0), "final_norm_w": g["final_norm_w"],
    }
    pieces = [gsmall[n] for n in _SMALL] + [loss_row]
    piece_shapes = [a.shape for a in pieces]
    gspack, gs_offs = _pack(pieces, 128, 8, F32)
    rows = gspack.shape[0]
    gsall = _allgather8("gs_gather", gspack)

    def sum8(a):
        r = a[0]
        for i in range(1, 8):
            r = r + a[i]
        return r

    gssum = _vcall("gs_sum", sum8, (1,), [(gsall, (8, rows, 128), lambda i: (0, 0, 0))],
                   [((rows, 128), F32, (rows, 128), lambda i: (0, 0))])
    gs = dict(zip(_SMALL + ("loss",), _unpack(gssum, piece_shapes, gs_offs)))
    loss = gs["loss"][0, 0]
    lb2 = gs["a_lb_logits"]
    small_grads = {
        "norm1_w": gs["norm1_w"], "norm2_w": gs["norm2_w"], "a_lb_logits": lb2[0] + lb2[1], "a_norm_w": gs["a_norm_w"],
        "b_dt_bias": gs["b_dt_bias"][:, :2 * SSM_HEADS].reshape(1, 2, SSM_HEADS),
        "b_a_log": gs["b_a_log"][:, :2 * SSM_HEADS].reshape(1, 2, SSM_HEADS),
        "b_d_skip": gs["b_d_skip"].reshape(1, SSM_HEADS, SSM_HD).sum(axis=-1),
        "ffn_conv_b": gs["ffn_conv_b"], "final_norm_w": gs["final_norm_w"][0],
        "b_conv_w": gs["b_conv_w"][None], "b_conv_b": gs["b_conv_b"], "b_norm_w": gs["b_norm_w"], "ffn_conv_w": gs["ffn_conv_w"],
    }
    for n in _SMALL_SPLIT:
        width = w[n].shape[-1]
        small_grads[n] = lax.dynamic_slice_in_dim(small_grads[n], chip * width, width, axis=small_grads[n].ndim - 1)
    grads.update(small_grads)

    delta, new_m, new_v = {}, {}, {}
    for n in _BIG:
        shp = w[n].shape
        two_d = (shp[0] * shp[1], shp[2])
        d_, m_, v_ = _adam(f"adam_{n}", w[n].reshape(two_d), grads[n].reshape(two_d), mom[n].reshape(two_d), var[n].reshape(two_d))
        delta[n], new_m[n], new_v[n] = d_.reshape(shp), m_.reshape(shp), v_.reshape(shp)
    s_shapes = [w[n].shape for n in _SMALL]
    packs = [_pack([src[n] for n in _SMALL], 128, 8, F32) for src in (w, grads, mom, var)]
    outs = _adam("adam_small", *[pk[0] for pk in packs])
    for res, dst in zip(outs, (delta, new_m, new_v)):
        dst.update(dict(zip(_SMALL, _unpack(res, s_shapes, packs[0][1]))))

    return (loss, dx[None], *[grads[n] for n in _ORDER], *[delta[n] for n in _ORDER],
            *[new_m[n] for n in _ORDER], *[new_v[n] for n in _ORDER])
```

```python
import functools

import jax
import jax.numpy as jnp
from jax import lax
from jax.experimental import pallas as pl
from jax.experimental.pallas import tpu as pltpu

F32, BF = jnp.float32, jnp.bfloat16
HI = lax.Precision.HIGHEST

D = 1024
EPS = 1e-6
HG_HEADS, HG_HD, HG_CHUNK, HG_SUB = 8, 128, 64, 8
HG_HB = 8
SSM_GB = 4
D_INNER, SSM_HEADS, SSM_HD, SSM_GROUPS, SSM_HPG, SSM_N, SSD_CHUNK = 2048, 32, 64, 8, 4, 128, 128
CONV_DIM = D_INNER + 2 * SSM_GROUPS * SSM_N
B_PROJ = 2 * D_INNER + 2 * SSM_GROUPS * SSM_N + 2 * SSM_HEADS
B_PROJ_PAD = 6272
D_FF = 2816
NEG = -1e30
ROWS = 256
GATE_ROWS = 1024
VMEM_LIMIT = 56 * 1024 * 1024

ADAM_LR, ADAM_B1, ADAM_B2, ADAM_EPS, ADAM_WD, ADAM_STEP = 0.001, 0.9, 0.999, 1e-08, 0.01, 10

MESH = pl.DeviceIdType.MESH


def _pick(n, cands):
    for c in cands:
        if n % c == 0:
            return c
    return n


class _Side:
    def __init__(self, ins, outs, plan, n_remote, n_local, alias=None):
        self.ins, self.outs, self.plan, self.n_remote, self.n_local = list(ins), list(outs), plan, n_remote, n_local
        self.alias = alias or {}

    def copies(self, in_refs, out_refs, send_sems, recv_sems, local_sems):
        remote, local = self.plan(in_refs, out_refs)
        cps = [pltpu.make_async_copy(s, d, local_sems.at[i]) for i, (s, d) in enumerate(local)]
        cps += [pltpu.make_async_remote_copy(src_ref=s, dst_ref=d, send_sem=send_sems.at[i], recv_sem=recv_sems.at[i],
                                             device_id=dev, device_id_type=MESH)
                for i, (s, d, dev) in enumerate(remote)]
        return cps

    def sems(self):
        return [pltpu.SemaphoreType.DMA((self.n_remote,)), pltpu.SemaphoreType.DMA((self.n_remote,)),
                pltpu.SemaphoreType.DMA((max(self.n_local, 1),))]


def _vcall(name, fn, grid, ins, outs, acc=None, scratch=(), place=None, side=None):
    acc = acc or {}
    n_in, n_out, nd = len(ins), len(outs), len(grid)
    n_pre = 0 if place is None else 1
    n_sin = len(side.ins) if side else 0
    n_sout = len(side.outs) if side else 0
    n_scr = len(scratch)

    def body(*refs):
        refs = refs[n_pre:]
        in_refs, refs = refs[:n_in], refs[n_in:]
        sin_refs, refs = refs[:n_sin], refs[n_sin:]
        out_refs, refs = refs[:n_out], refs[n_out:]
        sout_refs, refs = refs[:n_sout], refs[n_sout:]
        scr, sems = refs[:n_scr], refs[n_scr:]
        if side:
            at_first, at_last = None, None
            for ax in range(nd):
                f, l = pl.program_id(ax) == 0, pl.program_id(ax) == grid[ax] - 1
                at_first = f if at_first is None else jnp.logical_and(at_first, f)
                at_last = l if at_last is None else jnp.logical_and(at_last, l)

            @pl.when(at_first)
            def _():
                for cp in side.copies(sin_refs, sout_refs, *sems):
                    cp.start()

        res = fn(*[r[...] for r in in_refs], *scr)
        if not isinstance(res, (tuple, list)):
            res = (res,)
        if side:
            @pl.when(at_last)
            def _():
                for cp in side.copies(sin_refs, sout_refs, *sems):
                    cp.wait()
        for j, (o_ref, r) in enumerate(zip(out_refs, res)):
            mode = acc.get(j)
            if mode is None:
                o_ref[...] = r.astype(o_ref.dtype)
                continue
            first = pl.program_id(nd - 1) == 0
            if mode == "all":
                for ax in range(nd - 1):
                    first = jnp.logical_and(first, pl.program_id(ax) == 0)

            @pl.when(first)
            def _():
                o_ref[...] = r.astype(o_ref.dtype)

            @pl.when(jnp.logical_not(first))
            def _():
                o_ref[...] += r.astype(o_ref.dtype)

    hbm = pl.BlockSpec(memory_space=pl.ANY)
    in_specs = [pl.BlockSpec(bs, im) for _, bs, im in ins] + [hbm] * n_sin
    out_specs = [pl.BlockSpec(bs, im) for _, _, bs, im in outs] + [hbm] * n_sout
    params = pltpu.CompilerParams(dimension_semantics=("arbitrary",) * nd, vmem_limit_bytes=VMEM_LIMIT)
    out_shape = [jax.ShapeDtypeStruct(s, dt) for s, dt, _, _ in outs]
    operands = [a for a, _, _ in ins]
    scratch = list(scratch)
    aliases = {}
    if side:
        out_shape += [jax.ShapeDtypeStruct(s, dt) for s, dt in side.outs]
        operands += side.ins
        scratch += side.sems()
        aliases = {n_pre + n_in + i: n_out + o for i, o in side.alias.items()}
    if place is None:
        out = pl.pallas_call(body, name=name, grid=grid, in_specs=in_specs, out_specs=out_specs, out_shape=out_shape,
                             scratch_shapes=scratch, compiler_params=params, input_output_aliases=aliases)(*operands)
    else:
        spec = pltpu.PrefetchScalarGridSpec(num_scalar_prefetch=1, grid=grid, in_specs=in_specs, out_specs=out_specs,
                                            scratch_shapes=scratch)
        out = pl.pallas_call(body, name=name, grid_spec=spec, out_shape=out_shape, compiler_params=params,
                             input_output_aliases=aliases)(place, *operands)
    if side:
        return tuple(out[:n_out]), tuple(out[n_out:])
    return out[0] if n_out == 1 else out


def _mm(name, a, b, kind, out_dtype=F32, add=None, side=None):
    if kind == "tn":
        m, k = a.shape
        _, n = b.shape
        tm = _pick(m, (1024, 512, 256))
        tk = _pick(k, (1024, 1408, 896, 512, 256, 128))
        tn = _pick(n, (1024, 1408, 896, 512, 256, 128))

        def fn(av, bv):
            return lax.dot_general(av.astype(BF), bv.astype(BF), (((0,), (0,)), ((), ())),
                                   preferred_element_type=F32)

        return _vcall(name, fn, (k // tk, n // tn, m // tm),
                      [(a, (tm, tk), lambda i, j, s: (s, i)), (b, (tm, tn), lambda i, j, s: (s, j))],
                      [((k, n), F32, (tk, tn), lambda i, j, s: (i, j))], acc={0: "last"})
    m, k = a.shape
    n = b.shape[1] if kind == "nn" else b.shape[0]
    long_k = k > 4096
    tm = _pick(m, (512, 256)) if long_k else _pick(m, (1024, 512, 256))
    tn = _pick(n, (512, 896, 256, 128)) if long_k else _pick(n, (1024, 1408, 896, 512, 256, 128))
    dims =(((1,), (0,)), ((), ())) if kind == "nn" else (((1,), (1,)), ((), ()))

    def fn(av, bv, *rest):
        r = lax.dot_general(av.astype(BF), bv.astype(BF), dims, preferred_element_type=F32)
        return r + rest[0] if rest else r

    ins = [(a, (tm, k), lambda i, j: (i, 0)),
           (b, (k, tn), lambda i, j: (0, j)) if kind == "nn" else (b, (tn, k), lambda i, j: (j, 0))]
    if add is not None:
        ins.append((add, (tm, tn), lambda i, j: (i, j)))
    return _vcall(name, fn, (m // tm, n // tn), ins, [((m, n), out_dtype, (tm, tn), lambda i, j: (i, j))], side=side)


def _rms(h, w):
    return h * lax.rsqrt(jnp.mean(h * h, axis=-1, keepdims=True) + EPS) * w


def _rms_fwd(name, h, w):
    L = h.shape[0]
    tb = _pick(L, (ROWS,))
    return _vcall(name, _rms, (L // tb,),
                  [(h, (tb, D), lambda i: (i, 0)), (w.reshape(1, D), (1, D), lambda i: (0, 0))],
                  [((L, D), BF, (tb, D), lambda i: (i, 0))])


def _rms_bwd(name, du, h, w, dh_next, side=None):
    L = h.shape[0]
    tb = _pick(L, (ROWS,))

    def fn(duv, hv, wv, dnv):
        _, vjp = jax.vjp(_rms, hv, wv)
        dh, dw = vjp(duv)
        return dh + dnv, dw

    row = lambda i: (i, 0)
    return _vcall(name, fn, (L // tb,),
                  [(du, (tb, D), row), (h, (tb, D), row), (w.reshape(1, D), (1, D), lambda i: (0, 0)),
                   (dh_next, (tb, D), row)],
                  [((L, D), F32, (tb, D), row), ((1, D), F32, (1, D), lambda i: (0, 0))], acc={1: "all"}, side=side)


def _loss_head(name, h, tgt, w):
    L = h.shape[0]
    tb = _pick(L, (ROWS,))

    def lossf(hv, wv, tv):
        err = _rms(hv, wv) - tv
        return 0.5 * jnp.sum(err * err) * (1.0 / D)

    def fn(hv, wv, tv):
        val, vjp = jax.vjp(lambda a, b: lossf(a, b, tv), hv, wv)
        dh, dw = vjp(jnp.ones((), F32))
        return jnp.full((1, 128), val, F32), dh, dw

    row = lambda i: (i, 0)
    zero = lambda i: (0, 0)
    return _vcall(name, fn, (L // tb,),
                  [(h, (tb, D), row), (w.reshape(1, D), (1, D), zero), (tgt, (tb, D), row)],
                  [((1, 128), F32, (1, 128), zero), ((L, D), F32, (tb, D), row), ((1, D), F32, (1, D), zero)],
                  acc={0: "all", 2: "all"})


def _bf(x):
    return x.astype(BF)


def _dot(a, b, dims, precision=None):
    return lax.dot_general(a, b, (dims, ((), ())), preferred_element_type=F32, precision=precision)


def _tri(n, reverse):
    r = lax.broadcasted_iota(jnp.int32, (n, n), 0)
    c = lax.broadcasted_iota(jnp.int32, (n, n), 1)
    return (r <= c) if reverse else (r >= c)


def _tri_matmul(n, reverse, x):
    hi = x.astype(BF)
    r1 = x - hi.astype(F32)
    mid = r1.astype(BF)
    lo = (r1 - mid.astype(F32)).astype(BF)
    y = _dot(_tri(n, reverse).astype(BF), jnp.concatenate([hi, mid, lo], axis=1), ((1,), (0,)))
    w = x.shape[1]
    return (y[:, :w] + y[:, w:2 * w]) + y[:, 2 * w:]


@functools.partial(jax.custom_vjp, nondiff_argnums=(0, 1))
def _running_sum(n, reverse, x):
    return _tri_matmul(n, reverse, x)


def _running_sum_fwd(n, reverse, x):
    return _tri_matmul(n, reverse, x), None


def _running_sum_bwd(n, reverse, _, ct):
    return (_tri_matmul(n, not reverse, ct),)


_running_sum.defvjp(_running_sum_fwd, _running_sum_bwd)


def _gla_chunk(q_raw, f_raw, v, lb3, S, reverse):
    C, SB, HD = HG_CHUNK, HG_SUB, HG_HD
    H = S.shape[0]
    heads = [slice(HD * h, HD * (h + 1)) for h in range(H)]
    row3 = lax.broadcasted_iota(jnp.int32, (3, 1), 0)
    e = jnp.exp(lb3 - jnp.max(lb3, axis=0, keepdims=True))
    lb = jnp.sum(jnp.where(row3 == 0, e, 0.0), axis=0, keepdims=True) / jnp.sum(e, axis=0, keepdims=True)
    q = q_raw * jax.nn.sigmoid(q_raw)
    f = lb + (1.0 - lb) * jax.nn.sigmoid(f_raw)
    g = jnp.log(f)
    k = 1.0 - f
    b = _running_sum(C, reverse, g)
    row = lax.broadcasted_iota(jnp.int32, (C, 1), 0)
    vb = _bf(v)

    def rowof(x, t):
        return jnp.sum(jnp.where(row == t, x, 0.0), axis=0, keepdims=True)

    qe = _bf(q * jnp.exp(b))
    o = [_dot(qe[:, hs], _bf(S[h]), ((1,), (0,))) for h, hs in enumerate(heads)]
    att = [[] for _ in range(H)]
    for i in range(C // SB):
        lo = SB * i
        if (not reverse and i == 0) or (reverse and i == C // SB - 1):
            for h in range(H):
                att[h].append(jnp.zeros((SB, C), F32))
            continue
        first = lo + SB - 1 if reverse else lo
        r = rowof(b, first) - rowof(g, first)
        before = (row >= lo + SB) if reverse else (row < lo)
        qi = q[lo:lo + SB] * jnp.exp(b[lo:lo + SB] - r)
        kk = _bf(k * jnp.exp(jnp.where(before, r - b, NEG)))
        for h, hs in enumerate(heads):
            att[h].append(_dot(_bf(qi[:, hs]), kk[:, hs], ((1,), (1,))))
    o = [o[h] + _dot(_bf(jnp.concatenate(att[h], axis=0)), vb[:, hs], ((1,), (0,))) for h, hs in enumerate(heads)]
    s_i = lax.broadcasted_iota(jnp.int32, (SB, SB, HD), 0)
    t_i = lax.broadcasted_iota(jnp.int32, (SB, SB, HD), 1)
    pair = (t_i <= s_i) if reverse else (t_i >= s_i)
    shp = (SB, SB, HD)
    diag = [[] for _ in range(H)]
    for i in range(C // SB):
        rows = slice(SB * i, SB * (i + 1))
        for h, hs in enumerate(heads):
            qb, kb, bb = q[rows, hs], k[rows, hs], b[rows, hs]
            dif = lax.broadcast_in_dim(bb, shp, (1, 2)) - lax.broadcast_in_dim(bb, shp, (0, 2))
            w = lax.broadcast_in_dim(qb, shp, (1, 2)) * jnp.exp(jnp.where(pair, dif, NEG)) * lax.broadcast_in_dim(kb, shp, (0, 2))
            d = jnp.sum(w, axis=2, keepdims=True)
            diag[h].append(jnp.sum(d * lax.broadcast_in_dim(v[rows, hs], shp, (0, 2)), axis=0))
    o = jnp.concatenate([o[h] + jnp.concatenate(diag[h], axis=0) for h in range(H)], axis=1)
    btot = rowof(b, 0 if reverse else C - 1)
    kd = _bf(k * jnp.exp(btot - b))
    eye = lax.broadcasted_iota(jnp.int32, (HD, HD), 0) == lax.broadcasted_iota(jnp.int32, (HD, HD), 1)
    s_new = []
    for h, hs in enumerate(heads):
        btot_col = jnp.sum(jnp.where(eye, btot[:, hs], 0.0), axis=1, keepdims=True)
        s_new.append((jnp.exp(btot_col) * S[h] + _dot(kd[:, hs], vb[:, hs], ((0,), (0,))))[None])
    return o, jnp.concatenate(s_new, axis=0)


def _gla_fwd(name, pa, lbl, reverse, side=None):
    L = pa.shape[0]
    C = HG_CHUNK
    nc = L // C
    cidx = (lambda i: nc - 1 - i) if reverse else (lambda i: i)
    sec = 2 if reverse else 1
    hb_n, nh = HG_HB, HG_HEADS // HG_HB

    def fn(qr, fr, v, lb3, s_ref):
        @pl.when(pl.program_id(1) == 0)
        def _():
            s_ref[...] = jnp.zeros_like(s_ref)

        s_all = s_ref[...]
        o, s_new = _gla_chunk(qr, fr, v, lb3, s_all, reverse)
        s_ref[...] = s_new
        return o, s_all[None]

    blk = (C, HG_HD * hb_n)
    return _vcall(name, fn, (nh, nc),
                  [(pa, blk, lambda h, i: (cidx(i), h)), (pa, blk, lambda h, i: (cidx(i), sec * nh + h)),
                   (pa, blk, lambda h, i: (cidx(i), 3 * nh + h)), (lbl, (3, HG_HD * hb_n), lambda h, i: (0, h))],
                  [((L, D), F32, blk, lambda h, i: (cidx(i), h)),
                   ((nc, HG_HEADS, HG_HD, HG_HD), F32, (1, hb_n, HG_HD, HG_HD), lambda h, i: (cidx(i), h, 0, 0))],
                  scratch=[pltpu.VMEM((hb_n, HG_HD, HG_HD), F32)], side=side)


def _gla_bwd(name, pa, lbl, s_in, do, reverse, prev=None, side=None):
    L = pa.shape[0]
    C = HG_CHUNK
    nc = L // C
    cidx = (lambda i: i) if reverse else (lambda i: nc - 1 - i)
    sec = 2 if reverse else 1
    n_prev = 0 if prev is None else 2
    hb_n, nh = HG_HB, HG_HEADS // HG_HB

    def fn(qr, fr, v, lb3, s, dov, *rest):
        ds_ref = rest[n_prev]

        @pl.when(pl.program_id(1) == 0)
        def _():
            ds_ref[...] = jnp.zeros_like(ds_ref)

        _, vjp = jax.vjp(lambda *a: _gla_chunk(*a, reverse), qr, fr, v, lb3, s[0])
        dq, df, dv, dlb, ds = vjp((dov, ds_ref[...]))
        ds_ref[...] = ds
        if n_prev:
            dq, dv = dq + rest[0], dv + rest[1]
        return dq, df, dv, dlb

    blk = (C, HG_HD * hb_n)
    at = lambda h, i: (cidx(i), h)
    ins = [(pa, blk, at), (pa, blk, lambda h, i: (cidx(i), sec * nh + h)), (pa, blk, lambda h, i: (cidx(i), 3 * nh + h)),
           (lbl, (3, HG_HD * hb_n), lambda h, i: (0, h)),
           (s_in, (1, hb_n, HG_HD, HG_HD), lambda h, i: (cidx(i), h, 0, 0)), (do, blk, at)]
    if prev is not None:
        ins += [(prev[0], blk, at), (prev[1], blk, at)]
    sum_dt = F32 if prev is None else BF
    return _vcall(name, fn, (nh, nc), ins,
                  [((L, D), sum_dt, blk, at), ((L, D), BF, blk, at), ((L, D), sum_dt, blk, at),
                   ((3, D), F32, (3, HG_HD * hb_n), lambda h, i: (0, h))],
                  acc={3: "last"}, scratch=[pltpu.VMEM((hb_n, HG_HD, HG_HD), F32)], side=side)


def _hgout(o_f, o_b, g, nw):
    o = o_f + o_b
    return _rms(o, nw) * (g * jax.nn.sigmoid(g))


def _hgout_fwd(name, o_f, o_b, pa, nw, side=None):
    L = o_f.shape[0]
    tb = _pick(L, (GATE_ROWS, ROWS))
    blk = (tb, HG_HD)
    at = lambda h, i: (i, h)
    return _vcall(name, _hgout, (HG_HEADS, L // tb),
                  [(o_f, blk, at), (o_b, blk, at), (pa, blk, lambda h, i: (i, 32 + h)),
                   (nw.reshape(1, HG_HD), (1, HG_HD), lambda h, i: (0, 0))],
                  [((L, D), BF, blk, at)], side=side)


def _hgout_bwd(name, o_f, o_b, pa, nw, dy, side=None):
    L = o_f.shape[0]
    tb = _pick(L, (GATE_ROWS, ROWS))

    def fn(ofv, obv, gv, nwv, dyv):
        _, vjp = jax.vjp(_hgout, ofv, obv, gv, nwv)
        do, _, dg, dnw = vjp(dyv)
        return do, dg, dnw

    blk = (tb, HG_HD)
    at = lambda h, i: (i, h)
    zero = lambda h, i: (0, 0)
    return _vcall(name, fn, (HG_HEADS, L // tb),
                  [(o_f, blk, at), (o_b, blk, at), (pa, blk, lambda h, i: (i, 32 + h)),
                   (nw.reshape(1, HG_HD), (1, HG_HD), zero), (dy, blk, at)],
                  [((L, D), F32, blk, at), ((L, D), BF, blk, at), ((1, HG_HD), F32, (1, HG_HD), zero)],
                  acc={2: "all"}, side=side)


def _shift(x, s):
    if s == 0:
        return x
    n = x.shape[0]
    t = lax.broadcasted_iota(jnp.int32, (n, 1), 0)
    if s > 0:
        return jnp.where(t >= s, pltpu.roll(x, s, 0), 0.0)
    return jnp.where(t < n + s, pltpu.roll(x, n + s, 0), 0.0)


def _conv(x, w, b):
    kk = w.shape[0]
    p = (kk - 1) // 2
    y = b
    for j in range(kk):
        y = y + w[j:j + 1] * _shift(x, p - j)
    return y


def _conv_bwd(x, w, dc):
    kk = w.shape[0]
    p = (kk - 1) // 2
    dx = None
    dws = []
    for j in range(kk):
        t = w[j:j + 1] * _shift(dc, j - p)
        dx = t if dx is None else dx + t
        dws.append(jnp.sum(dc * _shift(x, p - j), axis=0, keepdims=True))
    rows = lax.broadcasted_iota(jnp.int32, (kk, 1), 0)
    dw = None
    for j in range(kk):
        t = jnp.where(rows == j, dws[j], 0.0)
        dw = t if dw is None else dw + t
    return dx, dw, jnp.sum(dc, axis=0, keepdims=True)


def _silu(c):
    return c * jax.nn.sigmoid(c)


def _silu_grad(c):
    s = jax.nn.sigmoid(c)
    return s * (1.0 + c * (1.0 - s))


def _glu_fwd(name, pf, cw, cb):
    L = pf.shape[0]
    tc = 128
    nt = D_FF // tc
    return _vcall(name, lambda gate, val, w, b: _silu(_conv(gate, w, b)) * val, (nt,),
                  [(pf, (L, tc), lambda j: (0, j)), (pf, (L, tc), lambda j: (0, nt + j)),
                   (cw, (3, tc), lambda j: (0, j)), (cb.reshape(1, D_FF), (1, tc), lambda j: (0, j))],
                  [((L, D_FF), BF, (L, tc), lambda j: (0, j))])


def _glu_bwd(name, pf, cw, cb, dy, side=None):
    L = pf.shape[0]
    tc = 128
    nt = D_FF // tc

    def fn(gate, val, w, b, dyv):
        c = _conv(gate, w, b)
        dgate, dw, db = _conv_bwd(gate, w, dyv * val * _silu_grad(c))
        return dgate, dyv * _silu(c), dw, db

    col = lambda j: (0, j)
    return _vcall(name, fn, (nt,),
                  [(pf, (L, tc), col), (pf, (L, tc), lambda j: (0, nt + j)), (cw, (3, tc), col),
                   (cb.reshape(1, D_FF), (1, tc), col), (dy, (L, tc), col)],
                  [((L, D_FF), BF, (L, tc), col), ((L, D_FF), BF, (L, tc), col),
                   ((3, D_FF), F32, (3, tc), col), ((1, D_FF), F32, (1, tc), col)], side=side)


def _perm_tile(j):
    return jnp.where(j < 16, 4 * (j // 2) + j % 2, jnp.where(j < 24, 4 * (j - 16) + 2, 4 * (j - 24) + 3))


def _mpre_fwd(name, pb, cw, cb):
    L = pb.shape[0]
    tc = 128
    return _vcall(name, lambda x, w, b: _silu(_conv(x, w, b)), (CONV_DIM // tc,),
                  [(pb, (L, tc), lambda j: (0, 16 + j)), (cw, (5, tc), lambda j: (0, j)),
                   (cb.reshape(1, CONV_DIM), (1, tc), lambda j: (0, j))],
                  [((L, CONV_DIM), F32, (L, tc), lambda j: (0, _perm_tile(j)))])


def _mpre_bwd(name, pb, cw, cb, dact):
    L = pb.shape[0]
    tc = 128
    col = lambda j: (0, j)
    return _vcall(name, lambda x, w, b, da: _conv_bwd(x, w, da * _silu_grad(_conv(x, w, b))), (CONV_DIM // tc,),
                  [(pb, (L, tc), lambda j: (0, 16 + j)), (cw, (5, tc), col), (cb.reshape(1, CONV_DIM), (1, tc), col),
                   (dact, (L, tc), lambda j: (0, _perm_tile(j)))],
                  [((L, CONV_DIM), BF, (L, tc), col), ((5, CONV_DIM), F32, (5, tc), col),
                   ((1, CONV_DIM), F32, (1, tc), col)])


def _softplus(x):
    return jnp.maximum(x, 0.0) + jnp.log(1.0 + jnp.exp(-jnp.abs(x)))


def _dt_fwd(name, pb, dtb, alog):
    L = pb.shape[0]
    tb = _pick(L, (1024, ROWS))

    def fn(x, bias, al):
        dt = _softplus(x + bias)
        return dt, dt * (-jnp.exp(al))

    row = lambda i: (i, 0)
    zero = lambda i: (0, 0)
    return _vcall(name, fn, (L // tb,),
                  [(pb, (tb, 128), lambda i: (i, 48)), (dtb, (1, 128), zero), (alog, (1, 128), zero)],
                  [((L, 128), F32, (tb, 128), row), ((L, 128), F32, (tb, 128), row)])


def _dt_bwd(name, pb, dtb, alog, ddt_f, dla_f, ddt_b, dla_b):
    L = pb.shape[0]
    tb = _pick(L, (1024, ROWS))

    def fn(x, bias, al, a1, b1, a2, b2):
        ddt = jnp.sum(a1, axis=0) + jnp.sum(a2, axis=0)
        dla = jnp.sum(b1, axis=0) + jnp.sum(b2, axis=0)
        z = x + bias
        dt = _softplus(z)
        a = -jnp.exp(al)
        dz = (ddt + dla * a) * jax.nn.sigmoid(z)
        return dz, jnp.sum(dz, axis=0, keepdims=True), jnp.sum(dla * dt, axis=0, keepdims=True) * a

    zero = lambda i: (0, 0)
    g3 = (ddt_f.shape[0], tb, 128)
    at3 = lambda i: (0, i, 0)
    return _vcall(name, fn, (L // tb,),
                  [(pb, (tb, 128), lambda i: (i, 48)), (dtb, (1, 128), zero), (alog, (1, 128), zero),
                   (ddt_f, g3, at3), (dla_f, g3, at3), (ddt_b, g3, at3), (dla_b, g3, at3)],
                  [((L, 128), BF, (tb, 128), lambda i: (i, 0)), ((1, 128), F32, (1, 128), zero),
                   ((1, 128), F32, (1, 128), zero)], acc={1: "all", 2: "all"})


def _split_dot(x, e, dims, pieces):
    hi = x.astype(BF)
    r1 = x - hi.astype(F32)
    mid = r1.astype(BF)
    y = _dot(hi, e, dims) + _dot(mid, e, dims)
    if pieces == 3:
        y = y + _dot((r1 - mid.astype(F32)).astype(BF), e, dims)
    return y


@functools.partial(jax.custom_vjp, nondiff_argnums=(2,))
def _spread(x, e, pieces):
    return _split_dot(x, e, ((1,), (0,)), pieces)


def _spread_fwd(x, e, pieces):
    return _split_dot(x, e, ((1,), (0,)), pieces), e


def _spread_bwd(pieces, e, ct):
    return _split_dot(ct, e, ((1,), (1,)), pieces), jnp.zeros_like(e)


_spread.defvjp(_spread_fwd, _spread_bwd)


def _ssd_chunk(xa, dt, la, hs, head0, reverse):
    C, P4, HD, N = SSD_CHUNK, SSM_HPG * SSM_HD, SSM_HD, SSM_N
    G = hs.shape[0]
    nh = SSM_HPG * G
    row = lax.broadcasted_iota(jnp.int32, (C, 1), 0)
    lane = lax.broadcasted_iota(jnp.int32, (1, 128), 1)
    eye = lax.broadcasted_iota(jnp.int32, (C, C), 0) == lax.broadcasted_iota(jnp.int32, (C, C), 1)
    tri = _tri(C, reverse)
    last = 0 if reverse else C - 1
    acum = _running_sum(C, reverse, la)
    atot = jnp.sum(jnp.where(row == last, acum, 0.0), axis=0, keepdims=True)
    src = lax.broadcasted_iota(jnp.int32, (128, 1), 0) - head0
    to_x = (src == lax.broadcasted_iota(jnp.int32, (1, nh * HD), 1) // HD).astype(BF)
    dt_x = _spread(dt, to_x, 2)
    ea_x = _spread(jnp.exp(acum), to_x, 2)
    dec_x = _spread(jnp.exp(atot - acum), to_x, 2)
    col_head = lax.broadcasted_iota(jnp.int32, (1, P4), 1) // HD
    row_head = lax.broadcasted_iota(jnp.int32, (P4, 1), 0) // HD
    ys, news = [], []
    for gi in range(G):
        xs = xa[:, 512 * gi:512 * gi + P4]
        bm = _bf(xa[:, 512 * gi + P4:512 * gi + P4 + N])
        cm = _bf(xa[:, 512 * gi + P4 + N:512 * (gi + 1)])
        gx = slice(P4 * gi, P4 * (gi + 1))
        cb = _dot(cm, bm, ((1,), (1,)))
        xd = xs * dt_x[:, gx]
        ms, xds, scale = [], [], 0.0
        for j in range(SSM_HPG):
            i = SSM_HPG * gi + j
            ac = jnp.sum(jnp.where(lane == head0 + i, acum, 0.0), axis=1, keepdims=True)
            ac_row = jnp.sum(jnp.where(eye, ac, 0.0), axis=0, keepdims=True)
            ms.append(_bf(cb * jnp.exp(jnp.where(tri, ac - ac_row, NEG))))
            xds.append(_bf(jnp.where(col_head == j, xd, 0.0)))
            a_i = jnp.sum(jnp.where(lane == head0 + i, atot, 0.0), axis=1, keepdims=True)
            scale = scale + jnp.where(row_head == j, jnp.exp(a_i), 0.0)
        y = _dot(jnp.concatenate(ms, axis=1), jnp.concatenate(xds, axis=0), ((1,), (0,)))
        y = y + _dot(cm, _bf(hs[gi]), ((1,), (1,))) * ea_x[:, gx]
        ys.append(y)
        news.append((scale * hs[gi] + _dot(_bf(xd * dec_x[:, gx]), bm, ((0,), (0,))))[None])
    return jnp.concatenate(ys, axis=1), jnp.concatenate(news, axis=0)


def _ssd_fwd(name, xact, dt, la, reverse):
    L = xact.shape[0]
    C = SSD_CHUNK
    nc = L // C
    cidx = (lambda i: nc - 1 - i) if reverse else (lambda i: i)
    base = SSM_HEADS if reverse else 0
    P4 = SSM_HPG * SSM_HD

    gb_n = SSM_GB

    def fn(xa, dtv, lav, h_ref):
        @pl.when(pl.program_id(1) == 0)
        def _():
            h_ref[...] = jnp.zeros_like(h_ref)

        h_all = h_ref[...]
        y, h_new = _ssd_chunk(xa, dtv, lav, h_all, base + SSM_HPG * gb_n * pl.program_id(0), reverse)
        h_ref[...] = h_new
        return y, h_all[None]

    return _vcall(name, fn, (SSM_GROUPS // gb_n, nc),
                  [(xact, (C, 512 * gb_n), lambda g, i: (cidx(i), g)), (dt, (C, 128), lambda g, i: (cidx(i), 0)),
                   (la, (C, 128), lambda g, i: (cidx(i), 0))],
                  [((L, D_INNER), F32, (C, P4 * gb_n), lambda g, i: (cidx(i), g)),
                   ((nc, SSM_GROUPS, P4, SSM_N), F32, (1, gb_n, P4, SSM_N), lambda g, i: (cidx(i), g, 0, 0))],
                  scratch=[pltpu.VMEM((gb_n, P4, SSM_N), F32)])


def _ssd_bwd(name, xact, dt, la, h_in, dy, reverse, prev_xs=None, prev_all=None, side=None):
    L = xact.shape[0]
    C = SSD_CHUNK
    nc = L // C
    cidx = (lambda i: i) if reverse else (lambda i: nc - 1 - i)
    base = SSM_HEADS if reverse else 0
    P4 = SSM_HPG * SSM_HD

    gb_n = SSM_GB

    def fn(xa, dtv, lav, hs, dyv, pv, dh_ref):
        @pl.when(pl.program_id(1) == 0)
        def _():
            dh_ref[...] = jnp.zeros_like(dh_ref)

        head0 = base + SSM_HPG * gb_n * pl.program_id(0)
        _, vjp = jax.vjp(lambda a, b, c, d: _ssd_chunk(a, b, c, d, head0, reverse), xa, dtv, lav, hs[0])
        dxa, ddt, dla, dh = vjp((dyv, dh_ref[...]))
        dh_ref[...] = dh
        if prev_all is not None:
            dxa = dxa + pv
        else:
            zeros = jnp.zeros((C, 2 * SSM_N), F32)
            dxa = dxa + jnp.concatenate([t for gb in range(gb_n) for t in (pv[:, P4 * gb:P4 * (gb + 1)], zeros)], axis=1)
        return dxa, ddt[None], dla[None]

    at = lambda g, i: (cidx(i), g)
    at0 = lambda g, i: (cidx(i), 0)
    pv = (prev_all, (C, 512 * gb_n), at) if prev_all is not None else (prev_xs, (C, P4 * gb_n), at)
    steps = SSM_GROUPS // gb_n
    return _vcall(name, fn, (steps, nc),
                  [(xact, (C, 512 * gb_n), at), (dt, (C, 128), at0), (la, (C, 128), at0),
                   (h_in, (1, gb_n, P4, SSM_N), lambda g, i: (cidx(i), g, 0, 0)), (dy, (C, P4 * gb_n), at), pv],
                  [((L, CONV_DIM), F32, (C, 512 * gb_n), at),
                   ((steps, L, 128), F32, (1, C, 128), lambda g, i: (g, cidx(i), 0)),
                   ((steps, L, 128), F32, (1, C, 128), lambda g, i: (g, cidx(i), 0))],
                  scratch=[pltpu.VMEM((gb_n, P4, SSM_N), F32)], side=side)


def _mpost(y_f, y_b, xs, z, dsk, nw):
    y = (y_f + y_b + xs * dsk) * (z * jax.nn.sigmoid(z))
    return _rms(y, nw)


def _mpost_fwd(name, y_f, y_b, xact, pb, dsk, nw):
    L = y_f.shape[0]
    tb = _pick(L, (GATE_ROWS, ROWS))
    blk = (tb, 256)
    at = lambda g, i: (i, g)
    par = lambda g, i: (0, g)
    return _vcall(name, _mpost, (SSM_GROUPS, L // tb),
                  [(y_f, blk, at), (y_b, blk, at), (xact, blk, lambda g, i: (i, 2 * g)), (pb, blk, at),
                   (dsk, (1, 256), par), (nw, (1, 256), par)],
                  [((L, D_INNER), BF, blk, at)])


def _mpost_bwd(name, y_f, y_b, xact, pb, dsk, nw, dy):
    L = y_f.shape[0]
    tb = _pick(L, (GATE_ROWS, ROWS))

    def fn(yf, yb, xs, z, dskv, nwv, dyv):
        _, vjp = jax.vjp(_mpost, yf, yb, xs, z, dskv, nwv)
        dyf, _, dxs, dz, ddsk, dnw = vjp(dyv)
        return dyf, dxs, dz, ddsk, dnw

    blk = (tb, 256)
    at = lambda g, i: (i, g)
    par = lambda g, i: (0, g)
    return _vcall(name, fn, (SSM_GROUPS, L // tb),
                  [(y_f, blk, at), (y_b, blk, at), (xact, blk, lambda g, i: (i, 2 * g)), (pb, blk, at),
                   (dsk, (1, 256), par), (nw, (1, 256), par), (dy, blk, at)],
                  [((L, D_INNER), F32, blk, at), ((L, D_INNER), F32, blk, at), ((L, D_INNER), BF, blk, at),
                   ((1, D_INNER), F32, (1, 256), par), ((1, D_INNER), F32, (1, 256), par)],
                  acc={3: "last", 4: "last"})


def _ffn_fwd(tag, h, nw, w_in, cw, cb, w_out):
    u = _rms_fwd(f"{tag}_norm", h, nw)
    pf = _mm(f"{tag}_in", u, w_in, "nn")
    yf = _glu_fwd(f"{tag}_glu", pf, cw, cb)
    return _mm(f"{tag}_out", yf, w_out, "nn", add=h), (u, pf, yf)


def _ffn_bwd(tag, h, nw, w_in, cw, cb, w_out, saved, dh, side=None):
    u, pf, yf = saved
    d_w_out = _mm(f"{tag}_dwout", yf, dh, "tn")
    dyf = _mm(f"{tag}_dy", dh, w_out, "nt")
    got = ()
    if side is None:
        dgate, dval, dcw, dcb = _glu_bwd(f"{tag}_dglu", pf, cw, cb, dyf)
    else:
        (dgate, dval, dcw, dcb), got = _glu_bwd(f"{tag}_dglu", pf, cw, cb, dyf, side=side)
    dpf = jnp.concatenate([dgate, dval], axis=1)
    d_w_in = _mm(f"{tag}_dwin", u, dpf, "tn")
    du = _mm(f"{tag}_du", dpf, w_in, "nt")
    dh_in, dnw = _rms_bwd(f"{tag}_dnorm", du, h, nw, dh)
    return dh_in, dnw, d_w_in, dcw, dcb, d_w_out, got


def _sequence_grads(x, tgt, p, sh, place):
    g = {}
    lbl = p["a_lb_logits"]
    first, mid, last = ("a_in", "a_out"), ("f0_in", "f0_out", "b_in", "b_out"), ("f1_in", "f1_out")
    W = {}
    got = _exchange("w_first", _gather_side(first, (), sh, W))
    W.update(zip(first, got))
    got = _exchange("w_first_pass", _gather_side((), first, sh, W))
    W.update(zip(first, got))
    u1 = _rms_fwd("a_norm", x, p["norm1_w"][0])
    pa = _mm("a_in", u1, W["a_in"], "nn")
    (o_f, s_f), got = _gla_fwd("a_scan_f", pa, lbl, False, side=_gather_side(mid, (), sh, W))
    W.update(zip(mid, got))
    (o_b, s_b), got = _gla_fwd("a_scan_b", pa, lbl, True, side=_gather_side(last, mid, sh, W))
    W.update(zip(last + mid, got))
    (ya,), got = _hgout_fwd("a_gate", o_f, o_b, pa, p["a_norm_w"], side=_gather_side((), last, sh, W))
    W.update(zip(last, got))
    wb4 = W["b_in"].reshape(4, D, B_PROJ // 4)
    p = dict(p, a_w_in=W["a_in"], a_w_out=W["a_out"], b_w_out=W["b_out"], ffn_w_in=(W["f0_in"], W["f1_in"]),
             ffn_w_out=(W["f0_out"], W["f1_out"]),
             b_w_in=jnp.pad(jnp.concatenate([wb4[j] for j in range(4)], axis=1), ((0, 0), (0, B_PROJ_PAD - B_PROJ))))
    h1 = _mm("a_out", ya, p["a_w_out"], "nn", add=x)
    h2, ffn0 = _ffn_fwd("f0", h1, p["norm2_w"][0], p["ffn_w_in"][0], p["ffn_conv_w"][0], p["ffn_conv_b"][0], p["ffn_w_out"][0])
    u3 = _rms_fwd("b_norm", h2, p["norm1_w"][1])
    pb = _mm("b_in", u3, p["b_w_in"], "nn")
    xact = _mpre_fwd("b_conv", pb, p["b_conv_w"], p["b_conv_b"])
    dt, la = _dt_fwd("b_dt", pb, p["b_dt_bias"], p["b_a_log"])
    y_f, hs_f = _ssd_fwd("b_scan_f", xact, dt, la, False)
    y_b, hs_b = _ssd_fwd("b_scan_b", xact, dt, la, True)
    yb = _mpost_fwd("b_gate", y_f, y_b, xact, pb, p["b_d_skip"], p["b_norm_w"])
    h3 = _mm("b_out", yb, p["b_w_out"], "nn", add=h2)
    h4, ffn1 = _ffn_fwd("f1", h3, p["norm2_w"][1], p["ffn_w_in"][1], p["ffn_conv_w"][1], p["ffn_conv_b"][1], p["ffn_w_out"][1])
    loss, dh4, g["final_norm_w"] = _loss_head("head", h4, tgt, p["final_norm_w"])
    dh3, dn2_1, dwin1, dcw1, dcb1, dwout1, _ = _ffn_bwd("f1", h3, p["norm2_w"][1], p["ffn_w_in"][1], p["ffn_conv_w"][1],
                                                        p["ffn_conv_b"][1], p["ffn_w_out"][1], ffn1, dh4)
    G = {"f1_in": dwin1, "f1_out": dwout1}
    G["b_out"] = _mm("b_dwout", yb, dh3, "tn")
    dyb = _mm("b_dy", dh3, p["b_w_out"], "nt")
    dys, dxs, dz, g["b_d_skip"], g["b_norm_w"] = _mpost_bwd("b_dgate", y_f, y_b, xact, pb, p["b_d_skip"], p["b_norm_w"], dyb)
    wave1 = ("f1_in", "f1_out", "b_out")
    (dxa1, ddt_f, dla_f), got = _ssd_bwd("b_dscan_f", xact, dt, la, hs_f, dys, False, prev_xs=dxs, side=_pair_side(wave1, G))
    chip_sums = _pair_sums(wave1, G, got)
    dxa, ddt_b, dla_b = _ssd_bwd("b_dscan_b", xact, dt, la, hs_b, dys, True, prev_all=dxa1)
    dxbc, g["b_conv_w"], g["b_conv_b"] = _mpre_bwd("b_dconv", pb, p["b_conv_w"], p["b_conv_b"], dxa)
    ddtr, g["b_dt_bias"], g["b_a_log"] = _dt_bwd("b_ddt", pb, p["b_dt_bias"], p["b_a_log"], ddt_f, dla_f, ddt_b, dla_b)
    dpb = jnp.concatenate([dz, dxbc, ddtr], axis=1)
    G["b_in"] = _mm("b_dwin", dpb, u3, "tn")
    du3 = _mm("b_du", dpb, p["b_w_in"], "nt")
    dh2, dn1_1 = _rms_bwd("b_dnorm", du3, h2, p["norm1_w"][1], dh3)
    dh1, dn2_0, G["f0_in"], dcw0, dcb0, G["f0_out"], got = _ffn_bwd("f0", h1, p["norm2_w"][0], p["ffn_w_in"][0], p["ffn_conv_w"][0],
                                                                   p["ffn_conv_b"][0], p["ffn_w_out"][0], ffn0, dh2,
                                                                   side=_pair_side(("b_in",), G))
    chip_sums.update(_pair_sums(("b_in",), G, got))
    wave3 = ("f0_in", "f0_out")
    G["a_out"] = _mm("a_dwout", ya, dh1, "tn")
    dya = _mm("a_dy", dh1, p["a_w_out"], "nt")
    (do, dg, g["a_norm_w"]), got = _hgout_bwd("a_dgate", o_f, o_b, pa, p["a_norm_w"], dya, side=_pair_side(wave3, G))
    chip_sums.update(_pair_sums(wave3, G, got))
    late = last + mid
    (dq1, df1, dv1, dl1), got = _gla_bwd("a_dscan_f", pa, lbl, s_f, do, False, side=_chips_side(late, chip_sums))
    shards = {u: _chip_sum(f"gl_sum_{u}", _GGEO[u], chip_sums[u], r, place) for u, r in zip(late, got)}
    (dq, df2, dv, dl2), got = _gla_bwd("a_dscan_b", pa, lbl, s_b, do, True, prev=(dq1, dv1), side=_halves_side(late, shards))
    shards = dict(zip(late, got))
    dpa = jnp.concatenate([dq, df1, df2, dv, dg], axis=1)
    G["a_in"] = _mm("a_dwin", u1, dpa, "tn")
    chip_sums = _pair_sums(first, G, _exchange("ga_pair", _pair_side(first, G)))
    (du1,), got = _mm("a_du", dpa, p["a_w_in"], "nt", side=_chips_side(first, chip_sums))
    mine = {u: _chip_sum(f"ga_sum_{u}", _GGEO[u], chip_sums[u], r, place) for u, r in zip(first, got)}
    (dx, dn1_0), got = _rms_bwd("a_dnorm", du1, x, p["norm1_w"][0], dh1, side=_halves_side(first, mine))
    shards.update(zip(first, got))
    g["a_lb_logits"] = (dl1, dl2)
    g["norm1_w"] = (dn1_0, dn1_1)
    g["norm2_w"] = (dn2_0, dn2_1)
    g["ffn_conv_w"] = (dcw0, dcw1)
    g["ffn_conv_b"] = (dcb0, dcb1)
    return loss, dx, g, shards


def _here():
    return lax.axis_index("x"), lax.axis_index("y"), lax.axis_index("c")


def _allgather8(name, src, by_core=False):
    blk = src.shape[1:] if by_core else src.shape

    def body(x_ref, out_ref, send_sems, recv_sems, local_sem):
        x, y, c = _here()
        me, sibling = (x, y, c), (x, y, 1 - c)
        chips = [(1 - x, y), (x, 1 - y), (1 - x, 1 - y)]
        own = x_ref.at[c] if by_core else x_ref

        def slot(px, py, pc):
            return out_ref.at[4 * px + 2 * py + pc]

        def copy(k, block, to, from_own=False):
            return pltpu.make_async_remote_copy(
                src_ref=own if from_own else slot(*block), dst_ref=slot(*block),
                send_sem=send_sems.at[k], recv_sem=recv_sems.at[k], device_id=to, device_id_type=MESH)

        mine = pltpu.make_async_copy(own, slot(*me), local_sem)
        mine.start()
        first = [copy(0, me, sibling, from_own=True)]
        first += [copy(1 + j, me, (*chip, c), from_own=True) for j, chip in enumerate(chips)]
        for cp in first:
            cp.start()
        passed = [copy(4 + j, (*chip, c), sibling) for j, chip in enumerate(chips)]
        for j, chip in enumerate(chips):
            copy(1 + j, (*chip, c), me).wait_recv()
            passed[j].start()
        copy(0, sibling, me).wait_recv()
        for j, chip in enumerate(chips):
            copy(4 + j, (*chip, 1 - c), me).wait_recv()
        for cp in first + passed:
            cp.wait_send()
        mine.wait()

    return pl.pallas_call(
        body, name=name,
        out_shape=jax.ShapeDtypeStruct((8,) + tuple(blk), src.dtype),
        in_specs=[pl.BlockSpec(memory_space=pl.ANY)],
        out_specs=pl.BlockSpec(memory_space=pl.ANY),
        scratch_shapes=[pltpu.SemaphoreType.DMA((7,)), pltpu.SemaphoreType.DMA((7,)), pltpu.SemaphoreType.DMA],
    )(src)


def _exchange(name, side):
    n_i, n_o = len(side.ins), len(side.outs)

    def body(*refs):
        copies = side.copies(refs[:n_i], refs[n_i:n_i + n_o], *refs[n_i + n_o:])
        for cp in copies:
            cp.start()
        for cp in copies:
            cp.wait()

    return pl.pallas_call(
        body, name=name,
        out_shape=[jax.ShapeDtypeStruct(s, dt) for s, dt in side.outs],
        in_specs=[pl.BlockSpec(memory_space=pl.ANY)] * n_i,
        out_specs=[pl.BlockSpec(memory_space=pl.ANY)] * n_o,
        scratch_shapes=side.sems(),
        input_output_aliases=dict(side.alias),
    )(*side.ins)


_WGEO = {"a_in": ("col", 1024, 1280), "a_out": ("row", 256, 1024), "b_in": ("row", 1024, 1552), "b_out": ("row", 512, 1024),
         "f0_in": ("col", 1024, 1408), "f1_in": ("col", 1024, 1408), "f0_out": ("row", 704, 1024), "f1_out": ("row", 704, 1024)}
_GGEO = dict(_WGEO, b_in=("row", 1552, 1024))


def _full_shape(geo):
    kind, r, cw = geo
    return (r, 4 * cw) if kind == "col" else (4 * r, cw)


def _times(i, step):
    return i * step if isinstance(i, int) else pl.multiple_of(i * step, step & -step)


def _win(ref, geo, j, h):
    kind, r, cw = geo
    hr = r // 2
    if kind == "col":
        return ref.at[pl.ds(_times(h, hr), hr), pl.ds(_times(j, cw), cw)]
    return ref.at[pl.ds(_times(2 * j + h, hr), hr), :]


def _half(ref, geo, h):
    hr = geo[1] // 2
    return ref.at[pl.ds(_times(h, hr), hr), :]


def _gather_side(first, second, sh, full):
    n1 = len(first)

    def plan(ins, outs):
        x, y, c = _here()
        m = 2 * x + y
        remote, local = [], []
        for u, src, dst_full in zip(first, ins[:n1], outs[:n1]):
            mine, dst = _half(src, _WGEO[u], c), _win(dst_full, _WGEO[u], m, c)
            local.append((mine, dst))
            remote.append((mine, dst, (x, y, 1 - c)))
            for k in (1, 2, 3):
                t = (m + k) % 4
                remote.append((mine, dst, (t // 2, t % 2, c)))
        for u, buf in zip(second, outs[n1:]):
            for k in (1, 2, 3):
                w_ = _win(buf, _WGEO[u], (m + k) % 4, c)
                remote.append((w_, w_, (x, y, 1 - c)))
        return remote, local

    return _Side([sh[u] for u in first] + [full[u] for u in second],
                 [(_full_shape(_WGEO[u]), BF) for u in first + second], plan, 4 * n1 + 3 * len(second), n1,
                 alias={n1 + i: n1 + i for i in range(len(second))})


def _pair_side(units, G):
    def plan(ins, outs):
        x, y, c = _here()
        return [(_win(gr, _GGEO[u], j, 1 - c), got.at[j], (x, y, 1 - c))
                for u, gr, got in zip(units, ins, outs) for j in range(4)], []

    return _Side([G[u] for u in units], [((4, _GGEO[u][1] // 2, _GGEO[u][2]), F32) for u in units], plan, 4 * len(units), 0)


def _pair_sums(units, G, gots):
    out = {}
    for u, got in zip(units, gots):
        blk = got.shape[1:]
        at = (lambda j: (lax.axis_index("c"), j)) if _GGEO[u][0] == "col" else (lambda j: (2 * j + lax.axis_index("c"), 0))
        slab = lambda j: (j, 0, 0)
        out[u] = _vcall(f"g_pair_sum_{u}", lambda a, b: (a + b[0])[None], (4,),
                        [(G[u], blk, at), (got, (1,) + blk, slab)], [(got.shape, BF, (1,) + blk, slab)])
    return out


def _chips_side(units, chip_sums):
    def plan(ins, outs):
        x, y, c = _here()
        m = 2 * x + y
        remote = []
        for s, got in zip(ins, outs):
            for k in (1, 2, 3):
                t = (m + k) % 4
                remote.append((s.at[t], got.at[k - 1], (t // 2, t % 2, c)))
        return remote, []

    return _Side([chip_sums[u] for u in units], [((3,) + chip_sums[u].shape[1:], BF) for u in units], plan,
                 3 * len(units), 0)


def _chip_sum(name, geo, chip_sums, got, place):
    _, r, cw = geo
    blk = (r // 2, cw)
    return _vcall(name, lambda a, b: ((a[0].astype(F32) + b[0].astype(F32)) + b[1].astype(F32)) + b[2].astype(F32), (1,),
                  [(chip_sums, (1,) + blk, lambda i: (2 * lax.axis_index("x") + lax.axis_index("y"), 0, 0)),
                   (got, (3,) + blk, lambda i: (0, 0, 0))],
                  [((r, cw), F32, blk, lambda i: (lax.axis_index("c"), 0))])


def _halves_side(units, shards):
    def plan(ins, outs):
        x, y, c = _here()
        return [(_half(o, _GGEO[u], c), _half(o, _GGEO[u], c), (x, y, 1 - c)) for u, o in zip(units, outs)], []

    return _Side([shards[u] for u in units], [(shards[u].shape, F32) for u in units], plan, len(units), 0,
                 alias={i: i for i in range(len(units))})


def _adam(name, w, g, m, v):
    rows, cols = w.shape
    tb = _pick(rows, (256, 128, 64, 8))

    def fn(wv, gv, mv, vv):
        m2 = ADAM_B1 * mv + (1.0 - ADAM_B1) * gv
        v2 = ADAM_B2 * vv + (1.0 - ADAM_B2) * jnp.square(gv)
        m_hat = m2 / (1.0 - ADAM_B1 ** ADAM_STEP)
        v_hat = v2 / (1.0 - ADAM_B2 ** ADAM_STEP)
        return -ADAM_LR * (m_hat / (jnp.sqrt(v_hat) + ADAM_EPS) + ADAM_WD * wv), m2, v2

    at = lambda i: (i, 0)
    return _vcall(name, fn, (rows // tb,), [(a, (tb, cols), at) for a in (w, g, m, v)],
                  [((rows, cols), F32, (tb, cols), at)] * 3)


def _pack(arrays, width, row_multiple, dtype):
    parts, offs, at = [], [], 0
    for a in arrays:
        flat = a.reshape(-1).astype(dtype)
        rows = -(-flat.shape[0] // (width * row_multiple)) * row_multiple
        parts.append(jnp.pad(flat, (0, rows * width - flat.shape[0])).reshape(rows, width))
        offs.append(at)
        at += rows
    return jnp.concatenate(parts, axis=0), offs


def _unpack(flat, shapes, offs):
    out = []
    for shp, at in zip(shapes, offs):
        n = 1
        for s in shp:
            n *= s
        rows = -(-n // flat.shape[1])
        out.append(flat[at:at + rows].reshape(-1)[:n].reshape(shp))
    return out


_BIG = ("a_w_in", "a_w_out", "b_w_in", "b_w_out", "ffn_w_in", "ffn_w_out")
_BIG_AXIS = {"a_w_in": 2, "a_w_out": 1, "b_w_in": 2, "b_w_out": 1, "ffn_w_in": 2, "ffn_w_out": 1}
_SMALL_SPLIT = ("b_conv_w", "b_conv_b", "b_norm_w", "ffn_conv_w")
_SMALL = ("norm1_w", "norm2_w", "a_lb_logits", "a_norm_w", "b_conv_w", "b_conv_b", "b_dt_bias", "b_a_log", "b_d_skip",
          "b_norm_w", "ffn_conv_w", "ffn_conv_b", "final_norm_w")
_ORDER = ("norm1_w", "norm2_w", "a_w_in", "a_lb_logits", "a_norm_w", "a_w_out", "b_w_in", "b_conv_w", "b_conv_b", "b_dt_bias",
          "b_a_log", "b_d_skip", "b_norm_w", "b_w_out", "ffn_w_in", "ffn_conv_w", "ffn_conv_b", "ffn_w_out", "final_norm_w")


def kernel(x, norm1_w, norm2_w, a_w_in, a_lb_logits, a_norm_w, a_w_out, b_w_in, b_conv_w, b_conv_b, b_dt_bias, b_a_log, b_d_skip, b_norm_w, b_w_out, ffn_w_in, ffn_conv_w, ffn_conv_b, ffn_w_out, final_norm_w, loss_target, m_norm1_w, m_norm2_w, m_a_w_in, m_a_lb_logits, m_a_norm_w, m_a_w_out, m_b_w_in, m_b_conv_w, m_b_conv_b, m_b_dt_bias, m_b_a_log, m_b_d_skip, m_b_norm_w, m_b_w_out, m_ffn_w_in, m_ffn_conv_w, m_ffn_conv_b, m_ffn_w_out, m_final_norm_w, v_norm1_w, v_norm2_w, v_a_w_in, v_a_lb_logits, v_a_norm_w, v_a_w_out, v_b_w_in, v_b_conv_w, v_b_conv_b, v_b_dt_bias, v_b_a_log, v_b_d_skip, v_b_norm_w, v_b_w_out, v_ffn_w_in, v_ffn_conv_w, v_ffn_conv_b, v_ffn_w_out, v_final_norm_w):
    w = dict(norm1_w=norm1_w, norm2_w=norm2_w, a_w_in=a_w_in, a_lb_logits=a_lb_logits, a_norm_w=a_norm_w, a_w_out=a_w_out,
             b_w_in=b_w_in, b_conv_w=b_conv_w, b_conv_b=b_conv_b, b_dt_bias=b_dt_bias, b_a_log=b_a_log, b_d_skip=b_d_skip,
             b_norm_w=b_norm_w, b_w_out=b_w_out, ffn_w_in=ffn_w_in, ffn_conv_w=ffn_conv_w, ffn_conv_b=ffn_conv_b,
             ffn_w_out=ffn_w_out, final_norm_w=final_norm_w)
    mom = dict(norm1_w=m_norm1_w, norm2_w=m_norm2_w, a_w_in=m_a_w_in, a_lb_logits=m_a_lb_logits, a_norm_w=m_a_norm_w,
               a_w_out=m_a_w_out, b_w_in=m_b_w_in, b_conv_w=m_b_conv_w, b_conv_b=m_b_conv_b, b_dt_bias=m_b_dt_bias,
               b_a_log=m_b_a_log, b_d_skip=m_b_d_skip, b_norm_w=m_b_norm_w, b_w_out=m_b_w_out, ffn_w_in=m_ffn_w_in,
               ffn_conv_w=m_ffn_conv_w, ffn_conv_b=m_ffn_conv_b, ffn_w_out=m_ffn_w_out, final_norm_w=m_final_norm_w)
    var = dict(norm1_w=v_norm1_w, norm2_w=v_norm2_w, a_w_in=v_a_w_in, a_lb_logits=v_a_lb_logits, a_norm_w=v_a_norm_w,
               a_w_out=v_a_w_out, b_w_in=v_b_w_in, b_conv_w=v_b_conv_w, b_conv_b=v_b_conv_b, b_dt_bias=v_b_dt_bias,
               b_a_log=v_b_a_log, b_d_skip=v_b_d_skip, b_norm_w=v_b_norm_w, b_w_out=v_b_w_out, ffn_w_in=v_ffn_w_in,
               ffn_conv_w=v_ffn_conv_w, ffn_conv_b=v_ffn_conv_b, ffn_w_out=v_ffn_w_out, final_norm_w=v_final_norm_w)
    chip = 2 * lax.axis_index("x") + lax.axis_index("y")
    place = jnp.stack([chip, lax.axis_index("c")]).astype(jnp.int32)

    sh = {"a_in": a_w_in[0], "a_out": a_w_out[0], "b_in": b_w_in[0], "b_out": b_w_out[0], "f0_in": ffn_w_in[0],
          "f1_in": ffn_w_in[1], "f0_out": ffn_w_out[0], "f1_out": ffn_w_out[1]}
    sh = {u: a.astype(BF) for u, a in sh.items()}
    small_shapes = [w[n].shape for n in _SMALL_SPLIT]
    spack, small_offs = _pack([w[n] for n in _SMALL_SPLIT], 128, 8, F32)
    sall = _allgather8("s_gather", spack)
    sshards = [_unpack(sall[2 * j], small_shapes, small_offs) for j in range(4)]
    sfull = {n: jnp.concatenate([sshards[j][i] for j in range(4)], axis=-1) for i, n in enumerate(_SMALL_SPLIT)}

    p = dict(
        norm1_w=norm1_w, norm2_w=norm2_w, a_lb_logits=a_lb_logits, a_norm_w=a_norm_w[0], final_norm_w=final_norm_w,
        b_conv_w=sfull["b_conv_w"][0], b_conv_b=sfull["b_conv_b"][0], b_norm_w=sfull["b_norm_w"],
        ffn_conv_w=sfull["ffn_conv_w"], ffn_conv_b=ffn_conv_b,
        b_dt_bias=jnp.pad(b_dt_bias.reshape(1, 2 * SSM_HEADS), ((0, 0), (0, 128 - 2 * SSM_HEADS))),
        b_a_log=jnp.pad(b_a_log.reshape(1, 2 * SSM_HEADS), ((0, 0), (0, 128 - 2 * SSM_HEADS))),
        b_d_skip=jnp.repeat(b_d_skip[0], SSM_HD)[None],
    )

    loss_row, dx, g, gs_ = _sequence_grads(x[0], loss_target[0], p, sh, place)
    grads = {"a_w_in": gs_["a_in"][None], "a_w_out": gs_["a_out"][None], "b_w_in": gs_["b_in"].T[None],
             "b_w_out": gs_["b_out"][None], "ffn_w_in": jnp.stack([gs_["f0_in"], gs_["f1_in"]]),
             "ffn_w_out": jnp.stack([gs_["f0_out"], gs_["f1_out"]])}

    gsmall = {
        "norm1_w": jnp.concatenate(g["norm1_w"], axis=0), "norm2_w": jnp.concatenate(g["norm2_w"], axis=0),
        "a_lb_logits": jnp.stack(g["a_lb_logits"]), "a_norm_w": g["a_norm_w"], "b_conv_w": g["b_conv_w"],
        "b_conv_b": g["b_conv_b"], "b_dt_bias": g["b_dt_bias"], "b_a_log": g["b_a_log"], "b_d_skip": g["b_d_skip"],
        "b_norm_w": g["b_norm_w"], "ffn_conv_w": jnp.stack(g["ffn_conv_w"]),
        "ffn_conv_b": jnp.concatenate(g["ffn_conv_b"], axis=0), "final_norm_w": g["final_norm_w"],
    }
    pieces = [gsmall[n] for n in _SMALL] + [loss_row]
    piece_shapes = [a.shape for a in pieces]
    gspack, gs_offs = _pack(pieces, 128, 8, F32)
    rows = gspack.shape[0]
    gsall = _allgather8("gs_gather", gspack)

    def sum8(a):
        r = a[0]
        for i in range(1, 8):
            r = r + a[i]
        return r

    gssum = _vcall("gs_sum", sum8, (1,), [(gsall, (8, rows, 128), lambda i: (0, 0, 0))],
                   [((rows, 128), F32, (rows, 128), lambda i: (0, 0))])
    gs = dict(zip(_SMALL + ("loss",), _unpack(gssum, piece_shapes, gs_offs)))
    loss = gs["loss"][0, 0]
    lb2 = gs["a_lb_logits"]
    small_grads = {
        "norm1_w": gs["norm1_w"], "norm2_w": gs["norm2_w"], "a_lb_logits": lb2[0] + lb2[1], "a_norm_w": gs["a_norm_w"],
        "b_dt_bias": gs["b_dt_bias"][:, :2 * SSM_HEADS].reshape(1, 2, SSM_HEADS),
        "b_a_log": gs["b_a_log"][:, :2 * SSM_HEADS].reshape(1, 2, SSM_HEADS),
        "b_d_skip": gs["b_d_skip"].reshape(1, SSM_HEADS, SSM_HD).sum(axis=-1),
        "ffn_conv_b": gs["ffn_conv_b"], "final_norm_w": gs["final_norm_w"][0],
        "b_conv_w": gs["b_conv_w"][None], "b_conv_b": gs["b_conv_b"], "b_norm_w": gs["b_norm_w"], "ffn_conv_w": gs["ffn_conv_w"],
    }
    for n in _SMALL_SPLIT:
        width = w[n].shape[-1]
        small_grads[n] = lax.dynamic_slice_in_dim(small_grads[n], chip * width, width, axis=small_grads[n].ndim - 1)
    grads.update(small_grads)

    delta, new_m, new_v = {}, {}, {}
    for n in _BIG:
        shp = w[n].shape
        two_d = (shp[0] * shp[1], shp[2])
        d_, m_, v_ = _adam(f"adam_{n}", w[n].reshape(two_d), grads[n].reshape(two_d), mom[n].reshape(two_d), var[n].reshape(two_d))
        delta[n], new_m[n], new_v[n] = d_.reshape(shp), m_.reshape(shp), v_.reshape(shp)
    s_shapes = [w[n].shape for n in _SMALL]
    packs = [_pack([src[n] for n in _SMALL], 128, 8, F32) for src in (w, grads, mom, var)]
    outs = _adam("adam_small", *[pk[0] for pk in packs])
    for res, dst in zip(outs, (delta, new_m, new_v)):
        dst.update(dict(zip(_SMALL, _unpack(res, s_shapes, packs[0][1]))))

    return (loss, dx[None], *[grads[n] for n in _ORDER], *[delta[n] for n in _ORDER],
            *[new_m[n] for n in _ORDER], *[new_v[n] for n in _ORDER])
```

```python
import functools

import jax
import jax.numpy as jnp
from jax import lax
from jax.experimental import pallas as pl
from jax.experimental.pallas import tpu as pltpu

F32, BF = jnp.float32, jnp.bfloat16
HI = lax.Precision.HIGHEST

D = 1024
EPS = 1e-6
HG_HEADS, HG_HD, HG_CHUNK, HG_SUB = 8, 128, 64, 8
HG_HB = 8
SSM_GB = 4
D_INNER, SSM_HEADS, SSM_HD, SSM_GROUPS, SSM_HPG, SSM_N, SSD_CHUNK = 2048, 32, 64, 8, 4, 128, 128
CONV_DIM = D_INNER + 2 * SSM_GROUPS * SSM_N
B_PROJ = 2 * D_INNER + 2 * SSM_GROUPS * SSM_N + 2 * SSM_HEADS
B_PROJ_PAD = 6272
D_FF = 2816
NEG = -1e30
ROWS = 256
GATE_ROWS = 1024
VMEM_LIMIT = 56 * 1024 * 1024

ADAM_LR, ADAM_B1, ADAM_B2, ADAM_EPS, ADAM_WD, ADAM_STEP = 0.001, 0.9, 0.999, 1e-08, 0.01, 10

MESH = pl.DeviceIdType.MESH


def _pick(n, cands):
    for c in cands:
        if n % c == 0:
            return c
    return n


class _Side:
    def __init__(self, ins, outs, plan, n_remote, n_local, alias=None):
        self.ins, self.outs, self.plan, self.n_remote, self.n_local = list(ins), list(outs), plan, n_remote, n_local
        self.alias = alias or {}

    def copies(self, in_refs, out_refs, send_sems, recv_sems, local_sems):
        remote, local = self.plan(in_refs, out_refs)
        cps = [pltpu.make_async_copy(s, d, local_sems.at[i]) for i, (s, d) in enumerate(local)]
        cps += [pltpu.make_async_remote_copy(src_ref=s, dst_ref=d, send_sem=send_sems.at[i], recv_sem=recv_sems.at[i],
                                             device_id=dev, device_id_type=MESH)
                for i, (s, d, dev) in enumerate(remote)]
        return cps

    def sems(self):
        return [pltpu.SemaphoreType.DMA((self.n_remote,)), pltpu.SemaphoreType.DMA((self.n_remote,)),
                pltpu.SemaphoreType.DMA((max(self.n_local, 1),))]


def _vcall(name, fn, grid, ins, outs, acc=None, scratch=(), place=None, side=None):
    acc = acc or {}
    n_in, n_out, nd = len(ins), len(outs), len(grid)
    n_pre = 0 if place is None else 1
    n_sin = len(side.ins) if side else 0
    n_sout = len(side.outs) if side else 0
    n_scr = len(scratch)

    def body(*refs):
        refs = refs[n_pre:]
        in_refs, refs = refs[:n_in], refs[n_in:]
        sin_refs, refs = refs[:n_sin], refs[n_sin:]
        out_refs, refs = refs[:n_out], refs[n_out:]
        sout_refs, refs = refs[:n_sout], refs[n_sout:]
        scr, sems = refs[:n_scr], refs[n_scr:]
        if side:
            at_first, at_last = None, None
            for ax in range(nd):
                f, l = pl.program_id(ax) == 0, pl.program_id(ax) == grid[ax] - 1
                at_first = f if at_first is None else jnp.logical_and(at_first, f)
                at_last = l if at_last is None else jnp.logical_and(at_last, l)

            @pl.when(at_first)
            def _():
                for cp in side.copies(sin_refs, sout_refs, *sems):
                    cp.start()

        res = fn(*[r[...] for r in in_refs], *scr)
        if not isinstance(res, (tuple, list)):
            res = (res,)
        if side:
            @pl.when(at_last)
            def _():
                for cp in side.copies(sin_refs, sout_refs, *sems):
                    cp.wait()
        for j, (o_ref, r) in enumerate(zip(out_refs, res)):
            mode = acc.get(j)
            if mode is None:
                o_ref[...] = r.astype(o_ref.dtype)
                continue
            first = pl.program_id(nd - 1) == 0
            if mode == "all":
                for ax in range(nd - 1):
                    first = jnp.logical_and(first, pl.program_id(ax) == 0)

            @pl.when(first)
            def _():
                o_ref[...] = r.astype(o_ref.dtype)

            @pl.when(jnp.logical_not(first))
            def _():
                o_ref[...] += r.astype(o_ref.dtype)

    hbm = pl.BlockSpec(memory_space=pl.ANY)
    in_specs = [pl.BlockSpec(bs, im) for _, bs, im in ins] + [hbm] * n_sin
    out_specs = [pl.BlockSpec(bs, im) for _, _, bs, im in outs] + [hbm] * n_sout
    params = pltpu.CompilerParams(dimension_semantics=("arbitrary",) * nd, vmem_limit_bytes=VMEM_LIMIT)
    out_shape = [jax.ShapeDtypeStruct(s, dt) for s, dt, _, _ in outs]
    operands = [a for a, _, _ in ins]
    scratch = list(scratch)
    aliases = {}
    if side:
        out_shape += [jax.ShapeDtypeStruct(s, dt) for s, dt in side.outs]
        operands += side.ins
        scratch += side.sems()
        aliases = {n_pre + n_in + i: n_out + o for i, o in side.alias.items()}
    if place is None:
        out = pl.pallas_call(body, name=name, grid=grid, in_specs=in_specs, out_specs=out_specs, out_shape=out_shape,
                             scratch_shapes=scratch, compiler_params=params, input_output_aliases=aliases)(*operands)
    else:
        spec = pltpu.PrefetchScalarGridSpec(num_scalar_prefetch=1, grid=grid, in_specs=in_specs, out_specs=out_specs,
                                            scratch_shapes=scratch)
        out = pl.pallas_call(body, name=name, grid_spec=spec, out_shape=out_shape, compiler_params=params,
                             input_output_aliases=aliases)(place, *operands)
    if side:
        return tuple(out[:n_out]), tuple(out[n_out:])
    return out[0] if n_out == 1 else out


def _mm(name, a, b, kind, out_dtype=F32, add=None, side=None):
    if kind == "tn":
        m, k = a.shape
        _, n = b.shape
        tm = _pick(m, (1024, 512, 256))
        tk = _pick(k, (1024, 1408, 896, 512, 256, 128))
        tn = _pick(n, (1024, 1408, 896, 512, 256, 128))

        def fn(av, bv):
            return lax.dot_general(av.astype(BF), bv.astype(BF), (((0,), (0,)), ((), ())),
                                   preferred_element_type=F32)

        return _vcall(name, fn, (k // tk, n // tn, m // tm),
                      [(a, (tm, tk), lambda i, j, s: (s, i)), (b, (tm, tn), lambda i, j, s: (s, j))],
                      [((k, n), F32, (tk, tn), lambda i, j, s: (i, j))], acc={0: "last"})
    m, k = a.shape
    n = b.shape[1] if kind == "nn" else b.shape[0]
    long_k = k > 4096
    tm = _pick(m, (512, 256)) if long_k else _pick(m, (1024, 512, 256))
    tn = _pick(n, (512, 896, 256, 128)) if long_k else _pick(n, (1024, 1408, 896, 512, 256, 128))
    dims =(((1,), (0,)), ((), ())) if kind == "nn" else (((1,), (1,)), ((), ()))

    def fn(av, bv, *rest):
        r = lax.dot_general(av.astype(BF), bv.astype(BF), dims, preferred_element_type=F32)
        return r + rest[0] if rest else r

    ins = [(a, (tm, k), lambda i, j: (i, 0)),
           (b, (k, tn), lambda i, j: (0, j)) if kind == "nn" else (b, (tn, k), lambda i, j: (j, 0))]
    if add is not None:
        ins.append((add, (tm, tn), lambda i, j: (i, j)))
    return _vcall(name, fn, (m // tm, n // tn), ins, [((m, n), out_dtype, (tm, tn), lambda i, j: (i, j))], side=side)


def _rms(h, w):
    return h * lax.rsqrt(jnp.mean(h * h, axis=-1, keepdims=True) + EPS) * w


def _rms_fwd(name, h, w):
    L = h.shape[0]
    tb = _pick(L, (ROWS,))
    return _vcall(name, _rms, (L // tb,),
                  [(h, (tb, D), lambda i: (i, 0)), (w.reshape(1, D), (1, D), lambda i: (0, 0))],
                  [((L, D), BF, (tb, D), lambda i: (i, 0))])


def _rms_bwd(name, du, h, w, dh_next, side=None):
    L = h.shape[0]
    tb = _pick(L, (ROWS,))

    def fn(duv, hv, wv, dnv):
        _, vjp = jax.vjp(_rms, hv, wv)
        dh, dw = vjp(duv)
        return dh + dnv, dw

    row = lambda i: (i, 0)
    return _vcall(name, fn, (L // tb,),
                  [(du, (tb, D), row), (h, (tb, D), row), (w.reshape(1, D), (1, D), lambda i: (0, 0)),
                   (dh_next, (tb, D), row)],
                  [((L, D), F32, (tb, D), row), ((1, D), F32, (1, D), lambda i: (0, 0))], acc={1: "all"}, side=side)


def _loss_head(name, h, tgt, w):
    L = h.shape[0]
    tb = _pick(L, (ROWS,))

    def lossf(hv, wv, tv):
        err = _rms(hv, wv) - tv
        return 0.5 * jnp.sum(err * err) * (1.0 / D)

    def fn(hv, wv, tv):
        val, vjp = jax.vjp(lambda a, b: lossf(a, b, tv), hv, wv)
        dh, dw = vjp(jnp.ones((), F32))
        return jnp.full((1, 128), val, F32), dh, dw

    row = lambda i: (i, 0)
    zero = lambda i: (0, 0)
    return _vcall(name, fn, (L // tb,),
                  [(h, (tb, D), row), (w.reshape(1, D), (1, D), zero), (tgt, (tb, D), row)],
                  [((1, 128), F32, (1, 128), zero), ((L, D), F32, (tb, D), row), ((1, D), F32, (1, D), zero)],
                  acc={0: "all", 2: "all"})


def _bf(x):
    return x.astype(BF)


def _dot(a, b, dims, precision=None):
    return lax.dot_general(a, b, (dims, ((), ())), preferred_element_type=F32, precision=precision)


def _tri(n, reverse):
    r = lax.broadcasted_iota(jnp.int32, (n, n), 0)
    c = lax.broadcasted_iota(jnp.int32, (n, n), 1)
    return (r <= c) if reverse else (r >= c)


def _tri_matmul(n, reverse, x):
    hi = x.astype(BF)
    r1 = x - hi.astype(F32)
    mid = r1.astype(BF)
    lo = (r1 - mid.astype(F32)).astype(BF)
    y = _dot(_tri(n, reverse).astype(BF), jnp.concatenate([hi, mid, lo], axis=1), ((1,), (0,)))
    w = x.shape[1]
    return (y[:, :w] + y[:, w:2 * w]) + y[:, 2 * w:]


@functools.partial(jax.custom_vjp, nondiff_argnums=(0, 1))
def _running_sum(n, reverse, x):
    return _tri_matmul(n, reverse, x)


def _running_sum_fwd(n, reverse, x):
    return _tri_matmul(n, reverse, x), None


def _running_sum_bwd(n, reverse, _, ct):
    return (_tri_matmul(n, not reverse, ct),)


_running_sum.defvjp(_running_sum_fwd, _running_sum_bwd)


def _gla_chunk(q_raw, f_raw, v, lb3, S, reverse):
    C, SB, HD = HG_CHUNK, HG_SUB, HG_HD
    H = S.shape[0]
    heads = [slice(HD * h, HD * (h + 1)) for h in range(H)]
    row3 = lax.broadcasted_iota(jnp.int32, (3, 1), 0)
    e = jnp.exp(lb3 - jnp.max(lb3, axis=0, keepdims=True))
    lb = jnp.sum(jnp.where(row3 == 0, e, 0.0), axis=0, keepdims=True) / jnp.sum(e, axis=0, keepdims=True)
    q = q_raw * jax.nn.sigmoid(q_raw)
    f = lb + (1.0 - lb) * jax.nn.sigmoid(f_raw)
    g = jnp.log(f)
    k = 1.0 - f
    b = _running_sum(C, reverse, g)
    row = lax.broadcasted_iota(jnp.int32, (C, 1), 0)
    vb = _bf(v)

    def rowof(x, t):
        return jnp.sum(jnp.where(row == t, x, 0.0), axis=0, keepdims=True)

    qe = _bf(q * jnp.exp(b))
    o = [_dot(qe[:, hs], _bf(S[h]), ((1,), (0,))) for h, hs in enumerate(heads)]
    att = [[] for _ in range(H)]
    for i in range(C // SB):
        lo = SB * i
        if (not reverse and i == 0) or (reverse and i == C // SB - 1):
            for h in range(H):
                att[h].append(jnp.zeros((SB, C), F32))
            continue
        first = lo + SB - 1 if reverse else lo
        r = rowof(b, first) - rowof(g, first)
        before = (row >= lo + SB) if reverse else (row < lo)
        qi = q[lo:lo + SB] * jnp.exp(b[lo:lo + SB] - r)
        kk = _bf(k * jnp.exp(jnp.where(before, r - b, NEG)))
        for h, hs in enumerate(heads):
            att[h].append(_dot(_bf(qi[:, hs]), kk[:, hs], ((1,), (1,))))
    o = [o[h] + _dot(_bf(jnp.concatenate(att[h], axis=0)), vb[:, hs], ((1,), (0,))) for h, hs in enumerate(heads)]
    s_i = lax.broadcasted_iota(jnp.int32, (SB, SB, HD), 0)
    t_i = lax.broadcasted_iota(jnp.int32, (SB, SB, HD), 1)
    pair = (t_i <= s_i) if reverse else (t_i >= s_i)
    shp = (SB, SB, HD)
    diag = [[] for _ in range(H)]
    for i in range(C // SB):
        rows = slice(SB * i, SB * (i + 1))
        for h, hs in enumerate(heads):
            qb, kb, bb = q[rows, hs], k[rows, hs], b[rows, hs]
            dif = lax.broadcast_in_dim(bb, shp, (1, 2)) - lax.broadcast_in_dim(bb, shp, (0, 2))
            w = lax.broadcast_in_dim(qb, shp, (1, 2)) * jnp.exp(jnp.where(pair, dif, NEG)) * lax.broadcast_in_dim(kb, shp, (0, 2))
            d = jnp.sum(w, axis=2, keepdims=True)
            diag[h].append(jnp.sum(d * lax.broadcast_in_dim(v[rows, hs], shp, (0, 2)), axis=0))
    o = jnp.concatenate([o[h] + jnp.concatenate(diag[h], axis=0) for h in range(H)], axis=1)
    btot = rowof(b, 0 if reverse else C - 1)
    kd = _bf(k * jnp.exp(btot - b))
    eye = lax.broadcasted_iota(jnp.int32, (HD, HD), 0) == lax.broadcasted_iota(jnp.int32, (HD, HD), 1)
    s_new = []
    for h, hs in enumerate(heads):
        btot_col = jnp.sum(jnp.where(eye, btot[:, hs], 0.0), axis=1, keepdims=True)
        s_new.append((jnp.exp(btot_col) * S[h] + _dot(kd[:, hs], vb[:, hs], ((0,), (0,))))[None])
    return o, jnp.concatenate(s_new, axis=0)


def _gla_fwd(name, pa, lbl, reverse, side=None):
    L = pa.shape[0]
    C = HG_CHUNK
    nc = L // C
    cidx = (lambda i: nc - 1 - i) if reverse else (lambda i: i)
    sec = 2 if reverse else 1
    hb_n, nh = HG_HB, HG_HEADS // HG_HB

    def fn(qr, fr, v, lb3, s_ref):
        @pl.when(pl.program_id(1) == 0)
        def _():
            s_ref[...] = jnp.zeros_like(s_ref)

        s_all = s_ref[...]
        o, s_new = _gla_chunk(qr, fr, v, lb3, s_all, reverse)
        s_ref[...] = s_new
        return o, s_all[None]

    blk = (C, HG_HD * hb_n)
    return _vcall(name, fn, (nh, nc),
                  [(pa, blk, lambda h, i: (cidx(i), h)), (pa, blk, lambda h, i: (cidx(i), sec * nh + h)),
                   (pa, blk, lambda h, i: (cidx(i), 3 * nh + h)), (lbl, (3, HG_HD * hb_n), lambda h, i: (0, h))],
                  [((L, D), F32, blk, lambda h, i: (cidx(i), h)),
                   ((nc, HG_HEADS, HG_HD, HG_HD), F32, (1, hb_n, HG_HD, HG_HD), lambda h, i: (cidx(i), h, 0, 0))],
                  scratch=[pltpu.VMEM((hb_n, HG_HD, HG_HD), F32)], side=side)


def _gla_bwd(name, pa, lbl, s_in, do, reverse, prev=None, side=None):
    L = pa.shape[0]
    C = HG_CHUNK
    nc = L // C
    cidx = (lambda i: i) if reverse else (lambda i: nc - 1 - i)
    sec = 2 if reverse else 1
    n_prev = 0 if prev is None else 2
    hb_n, nh = HG_HB, HG_HEADS // HG_HB

    def fn(qr, fr, v, lb3, s, dov, *rest):
        ds_ref = rest[n_prev]

        @pl.when(pl.program_id(1) == 0)
        def _():
            ds_ref[...] = jnp.zeros_like(ds_ref)

        _, vjp = jax.vjp(lambda *a: _gla_chunk(*a, reverse), qr, fr, v, lb3, s[0])
        dq, df, dv, dlb, ds = vjp((dov, ds_ref[...]))
        ds_ref[...] = ds
        if n_prev:
            dq, dv = dq + rest[0], dv + rest[1]
        return dq, df, dv, dlb

    blk = (C, HG_HD * hb_n)
    at = lambda h, i: (cidx(i), h)
    ins = [(pa, blk, at), (pa, blk, lambda h, i: (cidx(i), sec * nh + h)), (pa, blk, lambda h, i: (cidx(i), 3 * nh + h)),
           (lbl, (3, HG_HD * hb_n), lambda h, i: (0, h)),
           (s_in, (1, hb_n, HG_HD, HG_HD), lambda h, i: (cidx(i), h, 0, 0)), (do, blk, at)]
    if prev is not None:
        ins += [(prev[0], blk, at), (prev[1], blk, at)]
    sum_dt = F32 if prev is None else BF
    return _vcall(name, fn, (nh, nc), ins,
                  [((L, D), sum_dt, blk, at), ((L, D), BF, blk, at), ((L, D), sum_dt, blk, at),
                   ((3, D), F32, (3, HG_HD * hb_n), lambda h, i: (0, h))],
                  acc={3: "last"}, scratch=[pltpu.VMEM((hb_n, HG_HD, HG_HD), F32)], side=side)


def _hgout(o_f, o_b, g, nw):
    o = o_f + o_b
    return _rms(o, nw) * (g * jax.nn.sigmoid(g))


def _hgout_fwd(name, o_f, o_b, pa, nw, side=None):
    L = o_f.shape[0]
    tb = _pick(L, (GATE_ROWS, ROWS))
    blk = (tb, HG_HD)
    at = lambda h, i: (i, h)
    return _vcall(name, _hgout, (HG_HEADS, L // tb),
                  [(o_f, blk, at), (o_b, blk, at), (pa, blk, lambda h, i: (i, 32 + h)),
                   (nw.reshape(1, HG_HD), (1, HG_HD), lambda h, i: (0, 0))],
                  [((L, D), BF, blk, at)], side=side)


def _hgout_bwd(name, o_f, o_b, pa, nw, dy, side=None):
    L = o_f.shape[0]
    tb = _pick(L, (GATE_ROWS, ROWS))

    def fn(ofv, obv, gv, nwv, dyv):
        _, vjp = jax.vjp(_hgout, ofv, obv, gv, nwv)
        do, _, dg, dnw = vjp(dyv)
        return do, dg, dnw

    blk = (tb, HG_HD)
    at = lambda h, i: (i, h)
    zero = lambda h, i: (0, 0)
    return _vcall(name, fn, (HG_HEADS, L // tb),
                  [(o_f, blk, at), (o_b, blk, at), (pa, blk, lambda h, i: (i, 32 + h)),
                   (nw.reshape(1, HG_HD), (1, HG_HD), zero), (dy, blk, at)],
                  [((L, D), F32, blk, at), ((L, D), BF, blk, at), ((1, HG_HD), F32, (1, HG_HD), zero)],
                  acc={2: "all"}, side=side)


def _shift(x, s):
    if s == 0:
        return x
    n = x.shape[0]
    t = lax.broadcasted_iota(jnp.int32, (n, 1), 0)
    if s > 0:
        return jnp.where(t >= s, pltpu.roll(x, s, 0), 0.0)
    return jnp.where(t < n + s, pltpu.roll(x, n + s, 0), 0.0)


def _conv(x, w, b):
    kk = w.shape[0]
    p = (kk - 1) // 2
    y = b
    for j in range(kk):
        y = y + w[j:j + 1] * _shift(x, p - j)
    return y


def _conv_bwd(x, w, dc):
    kk = w.shape[0]
    p = (kk - 1) // 2
    dx = None
    dws = []
    for j in range(kk):
        t = w[j:j + 1] * _shift(dc, j - p)
        dx = t if dx is None else dx + t
        dws.append(jnp.sum(dc * _shift(x, p - j), axis=0, keepdims=True))
    rows = lax.broadcasted_iota(jnp.int32, (kk, 1), 0)
    dw = None
    for j in range(kk):
        t = jnp.where(rows == j, dws[j], 0.0)
        dw = t if dw is None else dw + t
    return dx, dw, jnp.sum(dc, axis=0, keepdims=True)


def _silu(c):
    return c * jax.nn.sigmoid(c)


def _silu_grad(c):
    s = jax.nn.sigmoid(c)
    return s * (1.0 + c * (1.0 - s))


def _glu_fwd(name, pf, cw, cb):
    L = pf.shape[0]
    tc = 128
    nt = D_FF // tc
    return _vcall(name, lambda gate, val, w, b: _silu(_conv(gate, w, b)) * val, (nt,),
                  [(pf, (L, tc), lambda j: (0, j)), (pf, (L, tc), lambda j: (0, nt + j)),
                   (cw, (3, tc), lambda j: (0, j)), (cb.reshape(1, D_FF), (1, tc), lambda j: (0, j))],
                  [((L, D_FF), BF, (L, tc), lambda j: (0, j))])


def _glu_bwd(name, pf, cw, cb, dy, side=None):
    L = pf.shape[0]
    tc = 128
    nt = D_FF // tc

    def fn(gate, val, w, b, dyv):
        c = _conv(gate, w, b)
        dgate, dw, db = _conv_bwd(gate, w, dyv * val * _silu_grad(c))
        return dgate, dyv * _silu(c), dw, db

    col = lambda j: (0, j)
    return _vcall(name, fn, (nt,),
                  [(pf, (L, tc), col), (pf, (L, tc), lambda j: (0, nt + j)), (cw, (3, tc), col),
                   (cb.reshape(1, D_FF), (1, tc), col), (dy, (L, tc), col)],
                  [((L, D_FF), BF, (L, tc), col), ((L, D_FF), BF, (L, tc), col),
                   ((3, D_FF), F32, (3, tc), col), ((1, D_FF), F32, (1, tc), col)], side=side)


def _perm_tile(j):
    return jnp.where(j < 16, 4 * (j // 2) + j % 2, jnp.where(j < 24, 4 * (j - 16) + 2, 4 * (j - 24) + 3))


def _mpre_fwd(name, pb, cw, cb):
    L = pb.shape[0]
    tc = 128
    return _vcall(name, lambda x, w, b: _silu(_conv(x, w, b)), (CONV_DIM // tc,),
                  [(pb, (L, tc), lambda j: (0, 16 + j)), (cw, (5, tc), lambda j: (0, j)),
                   (cb.reshape(1, CONV_DIM), (1, tc), lambda j: (0, j))],
                  [((L, CONV_DIM), F32, (L, tc), lambda j: (0, _perm_tile(j)))])


def _mpre_bwd(name, pb, cw, cb, dact):
    L = pb.shape[0]
    tc = 128
    col = lambda j: (0, j)
    return _vcall(name, lambda x, w, b, da: _conv_bwd(x, w, da * _silu_grad(_conv(x, w, b))), (CONV_DIM // tc,),
                  [(pb, (L, tc), lambda j: (0, 16 + j)), (cw, (5, tc), col), (cb.reshape(1, CONV_DIM), (1, tc), col),
                   (dact, (L, tc), lambda j: (0, _perm_tile(j)))],
                  [((L, CONV_DIM), BF, (L, tc), col), ((5, CONV_DIM), F32, (5, tc), col),
                   ((1, CONV_DIM), F32, (1, tc), col)])


def _softplus(x):
    return jnp.maximum(x, 0.0) + jnp.log(1.0 + jnp.exp(-jnp.abs(x)))


def _dt_fwd(name, pb, dtb, alog):
    L = pb.shape[0]
    tb = _pick(L, (1024, ROWS))

    def fn(x, bias, al):
        dt = _softplus(x + bias)
        return dt, dt * (-jnp.exp(al))

    row = lambda i: (i, 0)
    zero = lambda i: (0, 0)
    return _vcall(name, fn, (L // tb,),
                  [(pb, (tb, 128), lambda i: (i, 48)), (dtb, (1, 128), zero), (alog, (1, 128), zero)],
                  [((L, 128), F32, (tb, 128), row), ((L, 128), F32, (tb, 128), row)])


def _dt_bwd(name, pb, dtb, alog, ddt_f, dla_f, ddt_b, dla_b):
    L = pb.shape[0]
    tb = _pick(L, (1024, ROWS))

    def fn(x, bias, al, a1, b1, a2, b2):
        ddt = jnp.sum(a1, axis=0) + jnp.sum(a2, axis=0)
        dla = jnp.sum(b1, axis=0) + jnp.sum(b2, axis=0)
        z = x + bias
        dt = _softplus(z)
        a = -jnp.exp(al)
        dz = (ddt + dla * a) * jax.nn.sigmoid(z)
        return dz, jnp.sum(dz, axis=0, keepdims=True), jnp.sum(dla * dt, axis=0, keepdims=True) * a

    zero = lambda i: (0, 0)
    g3 = (ddt_f.shape[0], tb, 128)
    at3 = lambda i: (0, i, 0)
    return _vcall(name, fn, (L // tb,),
                  [(pb, (tb, 128), lambda i: (i, 48)), (dtb, (1, 128), zero), (alog, (1, 128), zero),
                   (ddt_f, g3, at3), (dla_f, g3, at3), (ddt_b, g3, at3), (dla_b, g3, at3)],
                  [((L, 128), BF, (tb, 128), lambda i: (i, 0)), ((1, 128), F32, (1, 128), zero),
                   ((1, 128), F32, (1, 128), zero)], acc={1: "all", 2: "all"})


def _split_dot(x, e, dims, pieces):
    hi = x.astype(BF)
    r1 = x - hi.astype(F32)
    mid = r1.astype(BF)
    y = _dot(hi, e, dims) + _dot(mid, e, dims)
    if pieces == 3:
        y = y + _dot((r1 - mid.astype(F32)).astype(BF), e, dims)
    return y


@functools.partial(jax.custom_vjp, nondiff_argnums=(2,))
def _spread(x, e, pieces):
    return _split_dot(x, e, ((1,), (0,)), pieces)


def _spread_fwd(x, e, pieces):
    return _split_dot(x, e, ((1,), (0,)), pieces), e


def _spread_bwd(pieces, e, ct):
    return _split_dot(ct, e, ((1,), (1,)), pieces), jnp.zeros_like(e)


_spread.defvjp(_spread_fwd, _spread_bwd)


def _ssd_chunk(xa, dt, la, hs, head0, reverse):
    C, P4, HD, N = SSD_CHUNK, SSM_HPG * SSM_HD, SSM_HD, SSM_N
    G = hs.shape[0]
    nh = SSM_HPG * G
    row = lax.broadcasted_iota(jnp.int32, (C, 1), 0)
    lane = lax.broadcasted_iota(jnp.int32, (1, 128), 1)
    eye = lax.broadcasted_iota(jnp.int32, (C, C), 0) == lax.broadcasted_iota(jnp.int32, (C, C), 1)
    tri = _tri(C, reverse)
    last = 0 if reverse else C - 1
    acum = _running_sum(C, reverse, la)
    atot = jnp.sum(jnp.where(row == last, acum, 0.0), axis=0, keepdims=True)
    src = lax.broadcasted_iota(jnp.int32, (128, 1), 0) - head0
    to_x = (src == lax.broadcasted_iota(jnp.int32, (1, nh * HD), 1) // HD).astype(BF)
    dt_x = _spread(dt, to_x, 2)
    ea_x = _spread(jnp.exp(acum), to_x, 2)
    dec_x = _spread(jnp.exp(atot - acum), to_x, 2)
    col_head = lax.broadcasted_iota(jnp.int32, (1, P4), 1) // HD
    row_head = lax.broadcasted_iota(jnp.int32, (P4, 1), 0) // HD
    ys, news = [], []
    for gi in range(G):
        xs = xa[:, 512 * gi:512 * gi + P4]
        bm = _bf(xa[:, 512 * gi + P4:512 * gi + P4 + N])
        cm = _bf(xa[:, 512 * gi + P4 + N:512 * (gi + 1)])
        gx = slice(P4 * gi, P4 * (gi + 1))
        cb = _dot(cm, bm, ((1,), (1,)))
        xd = xs * dt_x[:, gx]
        ms, xds, scale = [], [], 0.0
        for j in range(SSM_HPG):
            i = SSM_HPG * gi + j
            ac = jnp.sum(jnp.where(lane == head0 + i, acum, 0.0), axis=1, keepdims=True)
            ac_row = jnp.sum(jnp.where(eye, ac, 0.0), axis=0, keepdims=True)
            ms.append(_bf(cb * jnp.exp(jnp.where(tri, ac - ac_row, NEG))))
            xds.append(_bf(jnp.where(col_head == j, xd, 0.0)))
            a_i = jnp.sum(jnp.where(lane == head0 + i, atot, 0.0), axis=1, keepdims=True)
            scale = scale + jnp.where(row_head == j, jnp.exp(a_i), 0.0)
        y = _dot(jnp.concatenate(ms, axis=1), jnp.concatenate(xds, axis=0), ((1,), (0,)))
        y = y + _dot(cm, _bf(hs[gi]), ((1,), (1,))) * ea_x[:, gx]
        ys.append(y)
        news.append((scale * hs[gi] + _dot(_bf(xd * dec_x[:, gx]), bm, ((0,), (0,))))[None])
    return jnp.concatenate(ys, axis=1), jnp.concatenate(news, axis=0)


def _ssd_fwd(name, xact, dt, la, reverse):
    L = xact.shape[0]
    C = SSD_CHUNK
    nc = L // C
    cidx = (lambda i: nc - 1 - i) if reverse else (lambda i: i)
    base = SSM_HEADS if reverse else 0
    P4 = SSM_HPG * SSM_HD

    gb_n = SSM_GB

    def fn(xa, dtv, lav, h_ref):
        @pl.when(pl.program_id(1) == 0)
        def _():
            h_ref[...] = jnp.zeros_like(h_ref)

        h_all = h_ref[...]
        y, h_new = _ssd_chunk(xa, dtv, lav, h_all, base + SSM_HPG * gb_n * pl.program_id(0), reverse)
        h_ref[...] = h_new
        return y, h_all[None]

    return _vcall(name, fn, (SSM_GROUPS // gb_n, nc),
                  [(xact, (C, 512 * gb_n), lambda g, i: (cidx(i), g)), (dt, (C, 128), lambda g, i: (cidx(i), 0)),
                   (la, (C, 128), lambda g, i: (cidx(i), 0))],
                  [((L, D_INNER), F32, (C, P4 * gb_n), lambda g, i: (cidx(i), g)),
                   ((nc, SSM_GROUPS, P4, SSM_N), F32, (1, gb_n, P4, SSM_N), lambda g, i: (cidx(i), g, 0, 0))],
                  scratch=[pltpu.VMEM((gb_n, P4, SSM_N), F32)])


def _ssd_bwd(name, xact, dt, la, h_in, dy, reverse, prev_xs=None, prev_all=None, side=None):
    L = xact.shape[0]
    C = SSD_CHUNK
    nc = L // C
    cidx = (lambda i: i) if reverse else (lambda i: nc - 1 - i)
    base = SSM_HEADS if reverse else 0
    P4 = SSM_HPG * SSM_HD

    gb_n = SSM_GB

    def fn(xa, dtv, lav, hs, dyv, pv, dh_ref):
        @pl.when(pl.program_id(1) == 0)
        def _():
            dh_ref[...] = jnp.zeros_like(dh_ref)

        head0 = base + SSM_HPG * gb_n * pl.program_id(0)
        _, vjp = jax.vjp(lambda a, b, c, d: _ssd_chunk(a, b, c, d, head0, reverse), xa, dtv, lav, hs[0])
        dxa, ddt, dla, dh = vjp((dyv, dh_ref[...]))
        dh_ref[...] = dh
        if prev_all is not None:
            dxa = dxa + pv
        else:
            zeros = jnp.zeros((C, 2 * SSM_N), F32)
            dxa = dxa + jnp.concatenate([t for gb in range(gb_n) for t in (pv[:, P4 * gb:P4 * (gb + 1)], zeros)], axis=1)
        return dxa, ddt[None], dla[None]

    at = lambda g, i: (cidx(i), g)
    at0 = lambda g, i: (cidx(i), 0)
    pv = (prev_all, (C, 512 * gb_n), at) if prev_all is not None else (prev_xs, (C, P4 * gb_n), at)
    steps = SSM_GROUPS // gb_n
    return _vcall(name, fn, (steps, nc),
                  [(xact, (C, 512 * gb_n), at), (dt, (C, 128), at0), (la, (C, 128), at0),
                   (h_in, (1, gb_n, P4, SSM_N), lambda g, i: (cidx(i), g, 0, 0)), (dy, (C, P4 * gb_n), at), pv],
                  [((L, CONV_DIM), F32, (C, 512 * gb_n), at),
                   ((steps, L, 128), F32, (1, C, 128), lambda g, i: (g, cidx(i), 0)),
                   ((steps, L, 128), F32, (1, C, 128), lambda g, i: (g, cidx(i), 0))],
                  scratch=[pltpu.VMEM((gb_n, P4, SSM_N), F32)], side=side)


def _mpost(y_f, y_b, xs, z, dsk, nw):
    y = (y_f + y_b + xs * dsk) * (z * jax.nn.sigmoid(z))
    return _rms(y, nw)


def _mpost_fwd(name, y_f, y_b, xact, pb, dsk, nw):
    L = y_f.shape[0]
    tb = _pick(L, (GATE_ROWS, ROWS))
    blk = (tb, 256)
    at = lambda g, i: (i, g)
    par = lambda g, i: (0, g)
    return _vcall(name, _mpost, (SSM_GROUPS, L // tb),
                  [(y_f, blk, at), (y_b, blk, at), (xact, blk, lambda g, i: (i, 2 * g)), (pb, blk, at),
                   (dsk, (1, 256), par), (nw, (1, 256), par)],
                  [((L, D_INNER), BF, blk, at)])


def _mpost_bwd(name, y_f, y_b, xact, pb, dsk, nw, dy):
    L = y_f.shape[0]
    tb = _pick(L, (GATE_ROWS, ROWS))

    def fn(yf, yb, xs, z, dskv, nwv, dyv):
        _, vjp = jax.vjp(_mpost, yf, yb, xs, z, dskv, nwv)
        dyf, _, dxs, dz, ddsk, dnw = vjp(dyv)
        return dyf, dxs, dz, ddsk, dnw

    blk = (tb, 256)
    at = lambda g, i: (i, g)
    par = lambda g, i: (0, g)
    return _vcall(name, fn, (SSM_GROUPS, L // tb),
                  [(y_f, blk, at), (y_b, blk, at), (xact, blk, lambda g, i: (i, 2 * g)), (pb, blk, at),
                   (dsk, (1, 256), par), (nw, (1, 256), par), (dy, blk, at)],
                  [((L, D_INNER), F32, blk, at), ((L, D_INNER), F32, blk, at), ((L, D_INNER), BF, blk, at),
                   ((1, D_INNER), F32, (1, 256), par), ((1, D_INNER), F32, (1, 256), par)],
                  acc={3: "last", 4: "last"})


def _ffn_fwd(tag, h, nw, w_in, cw, cb, w_out):
    u = _rms_fwd(f"{tag}_norm", h, nw)
    pf = _mm(f"{tag}_in", u, w_in, "nn")
    yf = _glu_fwd(f"{tag}_glu", pf, cw, cb)
    return _mm(f"{tag}_out", yf, w_out, "nn", add=h), (u, pf, yf)


def _ffn_bwd(tag, h, nw, w_in, cw, cb, w_out, saved, dh, side=None):
    u, pf, yf = saved
    d_w_out = _mm(f"{tag}_dwout", yf, dh, "tn")
    dyf = _mm(f"{tag}_dy", dh, w_out, "nt")
    dgate, dval, dcw, dcb = _glu_bwd(f"{tag}_dglu", pf, cw, cb, dyf)
    dpf = jnp.concatenate([dgate, dval], axis=1)
    d_w_in = _mm(f"{tag}_dwin", u, dpf, "tn")
    got = ()
    if side is None:
        du = _mm(f"{tag}_du", dpf, w_in, "nt")
    else:
        (du,), got = _mm(f"{tag}_du", dpf, w_in, "nt", side=side)
    dh_in, dnw = _rms_bwd(f"{tag}_dnorm", du, h, nw, dh)
    return dh_in, dnw, d_w_in, dcw, dcb, d_w_out, got


def _sequence_grads(x, tgt, p, sh, place):
    g = {}
    lbl = p["a_lb_logits"]
    first, mid, last = ("a_in", "a_out"), ("f0_in", "b_in", "b_out"), ("f1_in", "f1_out", "f0_out")
    W = {}
    got = _exchange("w_first", _gather_side(first, (), sh, W))
    W.update(zip(first, got))
    got = _exchange("w_first_pass", _gather_side((), first, sh, W))
    W.update(zip(first, got))
    u1 = _rms_fwd("a_norm", x, p["norm1_w"][0])
    pa = _mm("a_in", u1, W["a_in"], "nn")
    (o_f, s_f), got = _gla_fwd("a_scan_f", pa, lbl, False, side=_gather_side(mid, (), sh, W))
    W.update(zip(mid, got))
    (o_b, s_b), got = _gla_fwd("a_scan_b", pa, lbl, True, side=_gather_side(last, mid, sh, W))
    W.update(zip(last + mid, got))
    (ya,), got = _hgout_fwd("a_gate", o_f, o_b, pa, p["a_norm_w"], side=_gather_side((), last, sh, W))
    W.update(zip(last, got))
    wb4 = W["b_in"].reshape(4, D, B_PROJ // 4)
    p = dict(p, a_w_in=W["a_in"], a_w_out=W["a_out"], b_w_out=W["b_out"], ffn_w_in=(W["f0_in"], W["f1_in"]),
             ffn_w_out=(W["f0_out"], W["f1_out"]),
             b_w_in=jnp.pad(jnp.concatenate([wb4[j] for j in range(4)], axis=1), ((0, 0), (0, B_PROJ_PAD - B_PROJ))))
    h1 = _mm("a_out", ya, p["a_w_out"], "nn", add=x)
    h2, ffn0 = _ffn_fwd("f0", h1, p["norm2_w"][0], p["ffn_w_in"][0], p["ffn_conv_w"][0], p["ffn_conv_b"][0], p["ffn_w_out"][0])
    u3 = _rms_fwd("b_norm", h2, p["norm1_w"][1])
    pb = _mm("b_in", u3, p["b_w_in"], "nn")
    xact = _mpre_fwd("b_conv", pb, p["b_conv_w"], p["b_conv_b"])
    dt, la = _dt_fwd("b_dt", pb, p["b_dt_bias"], p["b_a_log"])
    y_f, hs_f = _ssd_fwd("b_scan_f", xact, dt, la, False)
    y_b, hs_b = _ssd_fwd("b_scan_b", xact, dt, la, True)
    yb = _mpost_fwd("b_gate", y_f, y_b, xact, pb, p["b_d_skip"], p["b_norm_w"])
    h3 = _mm("b_out", yb, p["b_w_out"], "nn", add=h2)
    h4, ffn1 = _ffn_fwd("f1", h3, p["norm2_w"][1], p["ffn_w_in"][1], p["ffn_conv_w"][1], p["ffn_conv_b"][1], p["ffn_w_out"][1])
    loss, dh4, g["final_norm_w"] = _loss_head("head", h4, tgt, p["final_norm_w"])
    dh3, dn2_1, dwin1, dcw1, dcb1, dwout1, _ = _ffn_bwd("f1", h3, p["norm2_w"][1], p["ffn_w_in"][1], p["ffn_conv_w"][1],
                                                        p["ffn_conv_b"][1], p["ffn_w_out"][1], ffn1, dh4)
    G = {"f1_in": dwin1, "f1_out": dwout1}
    G["b_out"] = _mm("b_dwout", yb, dh3, "tn")
    dyb = _mm("b_dy", dh3, p["b_w_out"], "nt")
    dys, dxs, dz, g["b_d_skip"], g["b_norm_w"] = _mpost_bwd("b_dgate", y_f, y_b, xact, pb, p["b_d_skip"], p["b_norm_w"], dyb)
    wave1 = ("f1_in", "f1_out", "b_out")
    (dxa1, ddt_f, dla_f), got = _ssd_bwd("b_dscan_f", xact, dt, la, hs_f, dys, False, prev_xs=dxs, side=_pair_side(wave1, G))
    chip_sums = _pair_sums(wave1, G, got)
    dxa, ddt_b, dla_b = _ssd_bwd("b_dscan_b", xact, dt, la, hs_b, dys, True, prev_all=dxa1)
    dxbc, g["b_conv_w"], g["b_conv_b"] = _mpre_bwd("b_dconv", pb, p["b_conv_w"], p["b_conv_b"], dxa)
    ddtr, g["b_dt_bias"], g["b_a_log"] = _dt_bwd("b_ddt", pb, p["b_dt_bias"], p["b_a_log"], ddt_f, dla_f, ddt_b, dla_b)
    dpb = jnp.concatenate([dz, dxbc, ddtr], axis=1)
    G["b_in"] = _mm("b_dwin", dpb, u3, "tn")
    du3 = _mm("b_du", dpb, p["b_w_in"], "nt")
    dh2, dn1_1 = _rms_bwd("b_dnorm", du3, h2, p["norm1_w"][1], dh3)
    dh1, dn2_0, G["f0_in"], dcw0, dcb0, G["f0_out"], got = _ffn_bwd("f0", h1, p["norm2_w"][0], p["ffn_w_in"][0], p["ffn_conv_w"][0],
                                                                   p["ffn_conv_b"][0], p["ffn_w_out"][0], ffn0, dh2,
                                                                   side=_pair_side(("b_in",), G))
    chip_sums.update(_pair_sums(("b_in",), G, got))
    wave3 = ("f0_in", "f0_out")
    G["a_out"] = _mm("a_dwout", ya, dh1, "tn")
    dya = _mm("a_dy", dh1, p["a_w_out"], "nt")
    (do, dg, g["a_norm_w"]), got = _hgout_bwd("a_dgate", o_f, o_b, pa, p["a_norm_w"], dya, side=_pair_side(wave3, G))
    chip_sums.update(_pair_sums(wave3, G, got))
    late = last + mid
    (dq1, df1, dv1, dl1), got = _gla_bwd("a_dscan_f", pa, lbl, s_f, do, False, side=_chips_side(late, chip_sums))
    shards = {u: _chip_sum(f"gl_sum_{u}", _GGEO[u], chip_sums[u], r, place) for u, r in zip(late, got)}
    (dq, df2, dv, dl2), got = _gla_bwd("a_dscan_b", pa, lbl, s_b, do, True, prev=(dq1, dv1), side=_halves_side(late, shards))
    shards = dict(zip(late, got))
    dpa = jnp.concatenate([dq, df1, df2, dv, dg], axis=1)
    G["a_in"] = _mm("a_dwin", u1, dpa, "tn")
    chip_sums = _pair_sums(first, G, _exchange("ga_pair", _pair_side(first, G)))
    (du1,), got = _mm("a_du", dpa, p["a_w_in"], "nt", side=_chips_side(first, chip_sums))
    mine = {u: _chip_sum(f"ga_sum_{u}", _GGEO[u], chip_sums[u], r, place) for u, r in zip(first, got)}
    (dx, dn1_0), got = _rms_bwd("a_dnorm", du1, x, p["norm1_w"][0], dh1, side=_halves_side(first, mine))
    shards.update(zip(first, got))
    g["a_lb_logits"] = (dl1, dl2)
    g["norm1_w"] = (dn1_0, dn1_1)
    g["norm2_w"] = (dn2_0, dn2_1)
    g["ffn_conv_w"] = (dcw0, dcw1)
    g["ffn_conv_b"] = (dcb0, dcb1)
    return loss, dx, g, shards


def _here():
    return lax.axis_index("x"), lax.axis_index("y"), lax.axis_index("c")


def _allgather8(name, src, by_core=False):
    blk = src.shape[1:] if by_core else src.shape

    def body(x_ref, out_ref, send_sems, recv_sems, local_sem):
        x, y, c = _here()
        me, sibling = (x, y, c), (x, y, 1 - c)
        chips = [(1 - x, y), (x, 1 - y), (1 - x, 1 - y)]
        own = x_ref.at[c] if by_core else x_ref

        def slot(px, py, pc):
            return out_ref.at[4 * px + 2 * py + pc]

        def copy(k, block, to, from_own=False):
            return pltpu.make_async_remote_copy(
                src_ref=own if from_own else slot(*block), dst_ref=slot(*block),
                send_sem=send_sems.at[k], recv_sem=recv_sems.at[k], device_id=to, device_id_type=MESH)

        mine = pltpu.make_async_copy(own, slot(*me), local_sem)
        mine.start()
        first = [copy(0, me, sibling, from_own=True)]
        first += [copy(1 + j, me, (*chip, c), from_own=True) for j, chip in enumerate(chips)]
        for cp in first:
            cp.start()
        passed = [copy(4 + j, (*chip, c), sibling) for j, chip in enumerate(chips)]
        for j, chip in enumerate(chips):
            copy(1 + j, (*chip, c), me).wait_recv()
            passed[j].start()
        copy(0, sibling, me).wait_recv()
        for j, chip in enumerate(chips):
            copy(4 + j, (*chip, 1 - c), me).wait_recv()
        for cp in first + passed:
            cp.wait_send()
        mine.wait()

    return pl.pallas_call(
        body, name=name,
        out_shape=jax.ShapeDtypeStruct((8,) + tuple(blk), src.dtype),
        in_specs=[pl.BlockSpec(memory_space=pl.ANY)],
        out_specs=pl.BlockSpec(memory_space=pl.ANY),
        scratch_shapes=[pltpu.SemaphoreType.DMA((7,)), pltpu.SemaphoreType.DMA((7,)), pltpu.SemaphoreType.DMA],
    )(src)


def _exchange(name, side):
    n_i, n_o = len(side.ins), len(side.outs)

    def body(*refs):
        copies = side.copies(refs[:n_i], refs[n_i:n_i + n_o], *refs[n_i + n_o:])
        for cp in copies:
            cp.start()
        for cp in copies:
            cp.wait()

    return pl.pallas_call(
        body, name=name,
        out_shape=[jax.ShapeDtypeStruct(s, dt) for s, dt in side.outs],
        in_specs=[pl.BlockSpec(memory_space=pl.ANY)] * n_i,
        out_specs=[pl.BlockSpec(memory_space=pl.ANY)] * n_o,
        scratch_shapes=side.sems(),
        input_output_aliases=dict(side.alias),
    )(*side.ins)


_WGEO = {"a_in": ("col", 1024, 1280), "a_out": ("row", 256, 1024), "b_in": ("row", 1024, 1552), "b_out": ("row", 512, 1024),
         "f0_in": ("col", 1024, 1408), "f1_in": ("col", 1024, 1408), "f0_out": ("row", 704, 1024), "f1_out": ("row", 704, 1024)}
_GGEO = dict(_WGEO, b_in=("row", 1552, 1024))


def _full_shape(geo):
    kind, r, cw = geo
    return (r, 4 * cw) if kind == "col" else (4 * r, cw)


def _times(i, step):
    return i * step if isinstance(i, int) else pl.multiple_of(i * step, step & -step)


def _win(ref, geo, j, h):
    kind, r, cw = geo
    hr = r // 2
    if kind == "col":
        return ref.at[pl.ds(_times(h, hr), hr), pl.ds(_times(j, cw), cw)]
    return ref.at[pl.ds(_times(2 * j + h, hr), hr), :]


def _half(ref, geo, h):
    hr = geo[1] // 2
    return ref.at[pl.ds(_times(h, hr), hr), :]


def _gather_side(first, second, sh, full):
    n1 = len(first)

    def plan(ins, outs):
        x, y, c = _here()
        m = 2 * x + y
        remote, local = [], []
        for u, src, dst_full in zip(first, ins[:n1], outs[:n1]):
            mine, dst = _half(src, _WGEO[u], c), _win(dst_full, _WGEO[u], m, c)
            local.append((mine, dst))
            remote.append((mine, dst, (x, y, 1 - c)))
            for k in (1, 2, 3):
                t = (m + k) % 4
                remote.append((mine, dst, (t // 2, t % 2, c)))
        for u, buf in zip(second, outs[n1:]):
            for k in (1, 2, 3):
                w_ = _win(buf, _WGEO[u], (m + k) % 4, c)
                remote.append((w_, w_, (x, y, 1 - c)))
        return remote, local

    return _Side([sh[u] for u in first] + [full[u] for u in second],
                 [(_full_shape(_WGEO[u]), BF) for u in first + second], plan, 4 * n1 + 3 * len(second), n1,
                 alias={n1 + i: n1 + i for i in range(len(second))})


def _pair_side(units, G):
    def plan(ins, outs):
        x, y, c = _here()
        return [(_win(gr, _GGEO[u], j, 1 - c), got.at[j], (x, y, 1 - c))
                for u, gr, got in zip(units, ins, outs) for j in range(4)], []

    return _Side([G[u] for u in units], [((4, _GGEO[u][1] // 2, _GGEO[u][2]), F32) for u in units], plan, 4 * len(units), 0)


def _pair_sums(units, G, gots):
    out = {}
    for u, got in zip(units, gots):
        blk = got.shape[1:]
        at = (lambda j: (lax.axis_index("c"), j)) if _GGEO[u][0] == "col" else (lambda j: (2 * j + lax.axis_index("c"), 0))
        slab = lambda j: (j, 0, 0)
        out[u] = _vcall(f"g_pair_sum_{u}", lambda a, b: (a + b[0])[None], (4,),
                        [(G[u], blk, at), (got, (1,) + blk, slab)], [(got.shape, BF, (1,) + blk, slab)])
    return out


def _chips_side(units, chip_sums):
    def plan(ins, outs):
        x, y, c = _here()
        m = 2 * x + y
        remote = []
        for s, got in zip(ins, outs):
            for k in (1, 2, 3):
                t = (m + k) % 4
                remote.append((s.at[t], got.at[k - 1], (t // 2, t % 2, c)))
        return remote, []

    return _Side([chip_sums[u] for u in units], [((3,) + chip_sums[u].shape[1:], BF) for u in units], plan,
                 3 * len(units), 0)


def _chip_sum(name, geo, chip_sums, got, place):
    _, r, cw = geo
    blk = (r // 2, cw)
    return _vcall(name, lambda a, b: ((a[0].astype(F32) + b[0].astype(F32)) + b[1].astype(F32)) + b[2].astype(F32), (1,),
                  [(chip_sums, (1,) + blk, lambda i: (2 * lax.axis_index("x") + lax.axis_index("y"), 0, 0)),
                   (got, (3,) + blk, lambda i: (0, 0, 0))],
                  [((r, cw), F32, blk, lambda i: (lax.axis_index("c"), 0))])


def _halves_side(units, shards):
    def plan(ins, outs):
        x, y, c = _here()
        return [(_half(o, _GGEO[u], c), _half(o, _GGEO[u], c), (x, y, 1 - c)) for u, o in zip(units, outs)], []

    return _Side([shards[u] for u in units], [(shards[u].shape, F32) for u in units], plan, len(units), 0,
                 alias={i: i for i in range(len(units))})


def _adam(name, w, g, m, v):
    rows, cols = w.shape
    tb = _pick(rows, (256, 128, 64, 8))

    def fn(wv, gv, mv, vv):
        m2 = ADAM_B1 * mv + (1.0 - ADAM_B1) * gv
        v2 = ADAM_B2 * vv + (1.0 - ADAM_B2) * jnp.square(gv)
        m_hat = m2 / (1.0 - ADAM_B1 ** ADAM_STEP)
        v_hat = v2 / (1.0 - ADAM_B2 ** ADAM_STEP)
        return -ADAM_LR * (m_hat / (jnp.sqrt(v_hat) + ADAM_EPS) + ADAM_WD * wv), m2, v2

    at = lambda i: (i, 0)
    return _vcall(name, fn, (rows // tb,), [(a, (tb, cols), at) for a in (w, g, m, v)],
                  [((rows, cols), F32, (tb, cols), at)] * 3)


def _pack(arrays, width, row_multiple, dtype):
    parts, offs, at = [], [], 0
    for a in arrays:
        flat = a.reshape(-1).astype(dtype)
        rows = -(-flat.shape[0] // (width * row_multiple)) * row_multiple
        parts.append(jnp.pad(flat, (0, rows * width - flat.shape[0])).reshape(rows, width))
        offs.append(at)
        at += rows
    return jnp.concatenate(parts, axis=0), offs


def _unpack(flat, shapes, offs):
    out = []
    for shp, at in zip(shapes, offs):
        n = 1
        for s in shp:
            n *= s
        rows = -(-n // flat.shape[1])
        out.append(flat[at:at + rows].reshape(-1)[:n].reshape(shp))
    return out


_BIG = ("a_w_in", "a_w_out", "b_w_in", "b_w_out", "ffn_w_in", "ffn_w_out")
_BIG_AXIS = {"a_w_in": 2, "a_w_out": 1, "b_w_in": 2, "b_w_out": 1, "ffn_w_in": 2, "ffn_w_out": 1}
_SMALL_SPLIT = ("b_conv_w", "b_conv_b", "b_norm_w", "ffn_conv_w")
_SMALL = ("norm1_w", "norm2_w", "a_lb_logits", "a_norm_w", "b_conv_w", "b_conv_b", "b_dt_bias", "b_a_log", "b_d_skip",
          "b_norm_w", "ffn_conv_w", "ffn_conv_b", "final_norm_w")
_ORDER = ("norm1_w", "norm2_w", "a_w_in", "a_lb_logits", "a_norm_w", "a_w_out", "b_w_in", "b_conv_w", "b_conv_b", "b_dt_bias",
          "b_a_log", "b_d_skip", "b_norm_w", "b_w_out", "ffn_w_in", "ffn_conv_w", "ffn_conv_b", "ffn_w_out", "final_norm_w")


def kernel(x, norm1_w, norm2_w, a_w_in, a_lb_logits, a_norm_w, a_w_out, b_w_in, b_conv_w, b_conv_b, b_dt_bias, b_a_log, b_d_skip, b_norm_w, b_w_out, ffn_w_in, ffn_conv_w, ffn_conv_b, ffn_w_out, final_norm_w, loss_target, m_norm1_w, m_norm2_w, m_a_w_in, m_a_lb_logits, m_a_norm_w, m_a_w_out, m_b_w_in, m_b_conv_w, m_b_conv_b, m_b_dt_bias, m_b_a_log, m_b_d_skip, m_b_norm_w, m_b_w_out, m_ffn_w_in, m_ffn_conv_w, m_ffn_conv_b, m_ffn_w_out, m_final_norm_w, v_norm1_w, v_norm2_w, v_a_w_in, v_a_lb_logits, v_a_norm_w, v_a_w_out, v_b_w_in, v_b_conv_w, v_b_conv_b, v_b_dt_bias, v_b_a_log, v_b_d_skip, v_b_norm_w, v_b_w_out, v_ffn_w_in, v_ffn_conv_w, v_ffn_conv_b, v_ffn_w_out, v_final_norm_w):
    w = dict(norm1_w=norm1_w, norm2_w=norm2_w, a_w_in=a_w_in, a_lb_logits=a_lb_logits, a_norm_w=a_norm_w, a_w_out=a_w_out,
             b_w_in=b_w_in, b_conv_w=b_conv_w, b_conv_b=b_conv_b, b_dt_bias=b_dt_bias, b_a_log=b_a_log, b_d_skip=b_d_skip,
             b_norm_w=b_norm_w, b_w_out=b_w_out, ffn_w_in=ffn_w_in, ffn_conv_w=ffn_conv_w, ffn_conv_b=ffn_conv_b,
             ffn_w_out=ffn_w_out, final_norm_w=final_norm_w)
    mom = dict(norm1_w=m_norm1_w, norm2_w=m_norm2_w, a_w_in=m_a_w_in, a_lb_logits=m_a_lb_logits, a_norm_w=m_a_norm_w,
               a_w_out=m_a_w_out, b_w_in=m_b_w_in, b_conv_w=m_b_conv_w, b_conv_b=m_b_conv_b, b_dt_bias=m_b_dt_bias,
               b_a_log=m_b_a_log, b_d_skip=m_b_d_skip, b_norm_w=m_b_norm_w, b_w_out=m_b_w_out, ffn_w_in=m_ffn_w_in,
               ffn_conv_w=m_ffn_conv_w, ffn_conv_b=m_ffn_conv_b, ffn_w_out=m_ffn_w_out, final_norm_w=m_final_norm_w)
    var = dict(norm1_w=v_norm1_w, norm2_w=v_norm2_w, a_w_in=v_a_w_in, a_lb_logits=v_a_lb_logits, a_norm_w=v_a_norm_w,
               a_w_out=v_a_w_out, b_w_in=v_b_w_in, b_conv_w=v_b_conv_w, b_conv_b=v_b_conv_b, b_dt_bias=v_b_dt_bias,
               b_a_log=v_b_a_log, b_d_skip=v_b_d_skip, b_norm_w=v_b_norm_w, b_w_out=v_b_w_out, ffn_w_in=v_ffn_w_in,
               ffn_conv_w=v_ffn_conv_w, ffn_conv_b=v_ffn_conv_b, ffn_w_out=v_ffn_w_out, final_norm_w=v_final_norm_w)
    chip = 2 * lax.axis_index("x") + lax.axis_index("y")
    place = jnp.stack([chip, lax.axis_index("c")]).astype(jnp.int32)

    sh = {"a_in": a_w_in[0], "a_out": a_w_out[0], "b_in": b_w_in[0], "b_out": b_w_out[0], "f0_in": ffn_w_in[0],
          "f1_in": ffn_w_in[1], "f0_out": ffn_w_out[0], "f1_out": ffn_w_out[1]}
    sh = {u: a.astype(BF) for u, a in sh.items()}
    small_shapes = [w[n].shape for n in _SMALL_SPLIT]
    spack, small_offs = _pack([w[n] for n in _SMALL_SPLIT], 128, 8, F32)
    sall = _allgather8("s_gather", spack)
    sshards = [_unpack(sall[2 * j], small_shapes, small_offs) for j in range(4)]
    sfull = {n: jnp.concatenate([sshards[j][i] for j in range(4)], axis=-1) for i, n in enumerate(_SMALL_SPLIT)}

    p = dict(
        norm1_w=norm1_w, norm2_w=norm2_w, a_lb_logits=a_lb_logits, a_norm_w=a_norm_w[0], final_norm_w=final_norm_w,
        b_conv_w=sfull["b_conv_w"][0], b_conv_b=sfull["b_conv_b"][0], b_norm_w=sfull["b_norm_w"],
        ffn_conv_w=sfull["ffn_conv_w"], ffn_conv_b=ffn_conv_b,
        b_dt_bias=jnp.pad(b_dt_bias.reshape(1, 2 * SSM_HEADS), ((0, 0), (0, 128 - 2 * SSM_HEADS))),
        b_a_log=jnp.pad(b_a_log.reshape(1, 2 * SSM_HEADS), ((0, 0), (0, 128 - 2 * SSM_HEADS))),
        b_d_skip=jnp.repeat(b_d_skip[0], SSM_HD)[None],
    )

    loss_row, dx, g, gs_ = _sequence_grads(x[0], loss_target[0], p, sh, place)
    grads = {"a_w_in": gs_["a_in"][None], "a_w_out": gs_["a_out"][None], "b_w_in": gs_["b_in"].T[None],
             "b_w_out": gs_["b_out"][None], "ffn_w_in": jnp.stack([gs_["f0_in"], gs_["f1_in"]]),
             "ffn_w_out": jnp.stack([gs_["f0_out"], gs_["f1_out"]])}

    gsmall = {
        "norm1_w": jnp.concatenate(g["norm1_w"], axis=0), "norm2_w": jnp.concatenate(g["norm2_w"], axis=0),
        "a_lb_logits": jnp.stack(g["a_lb_logits"]), "a_norm_w": g["a_norm_w"], "b_conv_w": g["b_conv_w"],
        "b_conv_b": g["b_conv_b"], "b_dt_bias": g["b_dt_bias"], "b_a_log": g["b_a_log"], "b_d_skip": g["b_d_skip"],
        "b_norm_w": g["b_norm_w"], "ffn_conv_w": jnp.stack(g["ffn_conv_w"]),
        "ffn_conv_b": jnp.concatenate(g["ffn_conv_b"], axis=0), "final_norm_w": g["final_norm_w"],
    }
    pieces = [gsmall[n] for n in _SMALL] + [loss_row]
    piece_shapes = [a.shape for a in pieces]
    gspack, gs_offs = _pack(pieces, 128, 8, F32)
    rows = gspack.shape[0]
    gsall = _allgather8("gs_gather", gspack)

    def sum8(a):
        r = a[0]
        for i in range(1, 8):
            r = r + a[i]
        return r

    gssum = _vcall("gs_sum", sum8, (1,), [(gsall, (8, rows, 128), lambda i: (0, 0, 0))],
                   [((rows, 128), F32, (rows, 128), lambda i: (0, 0))])
    gs = dict(zip(_SMALL + ("loss",), _unpack(gssum, piece_shapes, gs_offs)))
    loss = gs["loss"][0, 0]
    lb2 = gs["a_lb_logits"]
    small_grads = {
        "norm1_w": gs["norm1_w"], "norm2_w": gs["norm2_w"], "a_lb_logits": lb2[0] + lb2[1], "a_norm_w": gs["a_norm_w"],
        "b_dt_bias": gs["b_dt_bias"][:, :2 * SSM_HEADS].reshape(1, 2, SSM_HEADS),
        "b_a_log": gs["b_a_log"][:, :2 * SSM_HEADS].reshape(1, 2, SSM_HEADS),
        "b_d_skip": gs["b_d_skip"].reshape(1, SSM_HEADS, SSM_HD).sum(axis=-1),
        "ffn_conv_b": gs["ffn_conv_b"], "final_norm_w": gs["final_norm_w"][0],
        "b_conv_w": gs["b_conv_w"][None], "b_conv_b": gs["b_conv_b"], "b_norm_w": gs["b_norm_w"], "ffn_conv_w": gs["ffn_conv_w"],
    }
    for n in _SMALL_SPLIT:
        width = w[n].shape[-1]
        small_grads[n] = lax.dynamic_slice_in_dim(small_grads[n], chip * width, width, axis=small_grads[n].ndim - 1)
    grads.update(small_grads)

    delta, new_m, new_v = {}, {}, {}
    for n in _BIG:
        shp = w[n].shape
        two_d = (shp[0] * shp[1], shp[2])
        d_, m_, v_ = _adam(f"adam_{n}", w[n].reshape(two_d), grads[n].reshape(two_d), mom[n].reshape(two_d), var[n].reshape(two_d))
        delta[n], new_m[n], new_v[n] = d_.reshape(shp), m_.reshape(shp), v_.reshape(shp)
    s_shapes = [w[n].shape for n in _SMALL]
    packs = [_pack([src[n] for n in _SMALL], 128, 8, F32) for src in (w, grads, mom, var)]
    outs = _adam("adam_small", *[pk[0] for pk in packs])
    for res, dst in zip(outs, (delta, new_m, new_v)):
        dst.update(dict(zip(_SMALL, _unpack(res, s_shapes, packs[0][1]))))

    return (loss, dx[None], *[grads[n] for n in _ORDER], *[delta[n] for n in _ORDER],
            *[new_m[n] for n in _ORDER], *[new_v[n] for n in _ORDER])
```

```python
import functools

import jax
import jax.numpy as jnp
from jax import lax
from jax.experimental import pallas as pl
from jax.experimental.pallas import tpu as pltpu

F32, BF = jnp.float32, jnp.bfloat16
HI = lax.Precision.HIGHEST

D = 1024
EPS = 1e-6
HG_HEADS, HG_HD, HG_CHUNK, HG_SUB = 8, 128, 64, 8
HG_HB = 8
SSM_GB = 4
D_INNER, SSM_HEADS, SSM_HD, SSM_GROUPS, SSM_HPG, SSM_N, SSD_CHUNK = 2048, 32, 64, 8, 4, 128, 128
CONV_DIM = D_INNER + 2 * SSM_GROUPS * SSM_N
B_PROJ = 2 * D_INNER + 2 * SSM_GROUPS * SSM_N + 2 * SSM_HEADS
B_PROJ_PAD = 6272
D_FF = 2816
NEG = -1e30
ROWS = 256
GATE_ROWS = 1024
VMEM_LIMIT = 56 * 1024 * 1024

ADAM_LR, ADAM_B1, ADAM_B2, ADAM_EPS, ADAM_WD, ADAM_STEP = 0.001, 0.9, 0.999, 1e-08, 0.01, 10

MESH = pl.DeviceIdType.MESH


def _pick(n, cands):
    for c in cands:
        if n % c == 0:
            return c
    return n


class _Side:
    def __init__(self, ins, outs, plan, n_remote, n_local, alias=None):
        self.ins, self.outs, self.plan, self.n_remote, self.n_local = list(ins), list(outs), plan, n_remote, n_local
        self.alias = alias or {}

    def copies(self, in_refs, out_refs, send_sems, recv_sems, local_sems):
        remote, local = self.plan(in_refs, out_refs)
        cps = [pltpu.make_async_copy(s, d, local_sems.at[i]) for i, (s, d) in enumerate(local)]
        cps += [pltpu.make_async_remote_copy(src_ref=s, dst_ref=d, send_sem=send_sems.at[i], recv_sem=recv_sems.at[i],
                                             device_id=dev, device_id_type=MESH)
                for i, (s, d, dev) in enumerate(remote)]
        return cps

    def sems(self):
        return [pltpu.SemaphoreType.DMA((self.n_remote,)), pltpu.SemaphoreType.DMA((self.n_remote,)),
                pltpu.SemaphoreType.DMA((max(self.n_local, 1),))]


def _vcall(name, fn, grid, ins, outs, acc=None, scratch=(), place=None, side=None):
    acc = acc or {}
    n_in, n_out, nd = len(ins), len(outs), len(grid)
    n_pre = 0 if place is None else 1
    n_sin = len(side.ins) if side else 0
    n_sout = len(side.outs) if side else 0
    n_scr = len(scratch)

    def body(*refs):
        refs = refs[n_pre:]
        in_refs, refs = refs[:n_in], refs[n_in:]
        sin_refs, refs = refs[:n_sin], refs[n_sin:]
        out_refs, refs = refs[:n_out], refs[n_out:]
        sout_refs, refs = refs[:n_sout], refs[n_sout:]
        scr, sems = refs[:n_scr], refs[n_scr:]
        if side:
            at_first, at_last = None, None
            for ax in range(nd):
                f, l = pl.program_id(ax) == 0, pl.program_id(ax) == grid[ax] - 1
                at_first = f if at_first is None else jnp.logical_and(at_first, f)
                at_last = l if at_last is None else jnp.logical_and(at_last, l)

            @pl.when(at_first)
            def _():
                for cp in side.copies(sin_refs, sout_refs, *sems):
                    cp.start()

        res = fn(*[r[...] for r in in_refs], *scr)
        if not isinstance(res, (tuple, list)):
            res = (res,)
        if side:
            @pl.when(at_last)
            def _():
                for cp in side.copies(sin_refs, sout_refs, *sems):
                    cp.wait()
        for j, (o_ref, r) in enumerate(zip(out_refs, res)):
            mode = acc.get(j)
            if mode is None:
                o_ref[...] = r.astype(o_ref.dtype)
                continue
            first = pl.program_id(nd - 1) == 0
            if mode == "all":
                for ax in range(nd - 1):
                    first = jnp.logical_and(first, pl.program_id(ax) == 0)

            @pl.when(first)
            def _():
                o_ref[...] = r.astype(o_ref.dtype)

            @pl.when(jnp.logical_not(first))
            def _():
                o_ref[...] += r.astype(o_ref.dtype)

    hbm = pl.BlockSpec(memory_space=pl.ANY)
    in_specs = [pl.BlockSpec(bs, im) for _, bs, im in ins] + [hbm] * n_sin
    out_specs = [pl.BlockSpec(bs, im) for _, _, bs, im in outs] + [hbm] * n_sout
    params = pltpu.CompilerParams(dimension_semantics=("arbitrary",) * nd, vmem_limit_bytes=VMEM_LIMIT)
    out_shape = [jax.ShapeDtypeStruct(s, dt) for s, dt, _, _ in outs]
    operands = [a for a, _, _ in ins]
    scratch = list(scratch)
    aliases = {}
    if side:
        out_shape += [jax.ShapeDtypeStruct(s, dt) for s, dt in side.outs]
        operands += side.ins
        scratch += side.sems()
        aliases = {n_pre + n_in + i: n_out + o for i, o in side.alias.items()}
    if place is None:
        out = pl.pallas_call(body, name=name, grid=grid, in_specs=in_specs, out_specs=out_specs, out_shape=out_shape,
                             scratch_shapes=scratch, compiler_params=params, input_output_aliases=aliases)(*operands)
    else:
        spec = pltpu.PrefetchScalarGridSpec(num_scalar_prefetch=1, grid=grid, in_specs=in_specs, out_specs=out_specs,
                                            scratch_shapes=scratch)
        out = pl.pallas_call(body, name=name, grid_spec=spec, out_shape=out_shape, compiler_params=params,
                             input_output_aliases=aliases)(place, *operands)
    if side:
        return tuple(out[:n_out]), tuple(out[n_out:])
    return out[0] if n_out == 1 else out


def _mm(name, a, b, kind, out_dtype=F32, add=None, side=None):
    if kind == "tn":
        m, k = a.shape
        _, n = b.shape
        tm = _pick(m, (1024, 512, 256))
        tk = _pick(k, (1024, 1408, 896, 512, 256, 128))
        tn = _pick(n, (1024, 1408, 896, 512, 256, 128))

        def fn(av, bv):
            return lax.dot_general(av.astype(BF), bv.astype(BF), (((0,), (0,)), ((), ())),
                                   preferred_element_type=F32)

        return _vcall(name, fn, (k // tk, n // tn, m // tm),
                      [(a, (tm, tk), lambda i, j, s: (s, i)), (b, (tm, tn), lambda i, j, s: (s, j))],
                      [((k, n), F32, (tk, tn), lambda i, j, s: (i, j))], acc={0: "last"}, side=side)
    m, k = a.shape
    n = b.shape[1] if kind == "nn" else b.shape[0]
    long_k = k > 4096
    tm = _pick(m, (512, 256)) if long_k else _pick(m, (1024, 512, 256))
    tn = _pick(n, (512, 896, 256, 128)) if long_k else _pick(n, (1024, 1408, 896, 512, 256, 128))
    dims = (((1,), (0,)), ((), ())) if kind == "nn" else (((1,), (1,)), ((), ()))

    def fn(av, bv, *rest):
        r = lax.dot_general(av.astype(BF), bv.astype(BF), dims, preferred_element_type=F32)
        return r + rest[0] if rest else r

    ins = [(a, (tm, k), lambda i, j: (i, 0)),
           (b, (k, tn), lambda i, j: (0, j)) if kind == "nn" else (b, (tn, k), lambda i, j: (j, 0))]
    if add is not None:
        ins.append((add, (tm, tn), lambda i, j: (i, j)))
    return _vcall(name, fn, (m // tm, n // tn), ins, [((m, n), out_dtype, (tm, tn), lambda i, j: (i, j))], side=side)


def _rms(h, w):
    return h * lax.rsqrt(jnp.mean(h * h, axis=-1, keepdims=True) + EPS) * w


def _rms_fwd(name, h, w):
    L = h.shape[0]
    tb = _pick(L, (ROWS,))
    return _vcall(name, _rms, (L // tb,),
                  [(h, (tb, D), lambda i: (i, 0)), (w.reshape(1, D), (1, D), lambda i: (0, 0))],
                  [((L, D), BF, (tb, D), lambda i: (i, 0))])


def _rms_bwd(name, du, h, w, dh_next, side=None):
    L = h.shape[0]
    tb = _pick(L, (ROWS,))

    def fn(duv, hv, wv, dnv):
        _, vjp = jax.vjp(_rms, hv, wv)
        dh, dw = vjp(duv)
        return dh + dnv, dw

    row = lambda i: (i, 0)
    return _vcall(name, fn, (L // tb,),
                  [(du, (tb, D), row), (h, (tb, D), row), (w.reshape(1, D), (1, D), lambda i: (0, 0)),
                   (dh_next, (tb, D), row)],
                  [((L, D), F32, (tb, D), row), ((1, D), F32, (1, D), lambda i: (0, 0))], acc={1: "all"}, side=side)


def _loss_head(name, h, tgt, w):
    L = h.shape[0]
    tb = _pick(L, (ROWS,))

    def lossf(hv, wv, tv):
        err = _rms(hv, wv) - tv
        return 0.5 * jnp.sum(err * err) * (1.0 / D)

    def fn(hv, wv, tv):
        val, vjp = jax.vjp(lambda a, b: lossf(a, b, tv), hv, wv)
        dh, dw = vjp(jnp.ones((), F32))
        return jnp.full((1, 128), val, F32), dh, dw

    row = lambda i: (i, 0)
    zero = lambda i: (0, 0)
    return _vcall(name, fn, (L // tb,),
                  [(h, (tb, D), row), (w.reshape(1, D), (1, D), zero), (tgt, (tb, D), row)],
                  [((1, 128), F32, (1, 128), zero), ((L, D), F32, (tb, D), row), ((1, D), F32, (1, D), zero)],
                  acc={0: "all", 2: "all"})


def _bf(x):
    return x.astype(BF)


def _dot(a, b, dims, precision=None):
    return lax.dot_general(a, b, (dims, ((), ())), preferred_element_type=F32, precision=precision)


def _tri(n, reverse):
    r = lax.broadcasted_iota(jnp.int32, (n, n), 0)
    c = lax.broadcasted_iota(jnp.int32, (n, n), 1)
    return (r <= c) if reverse else (r >= c)


def _tri_matmul(n, reverse, x):
    hi = x.astype(BF)
    r1 = x - hi.astype(F32)
    mid = r1.astype(BF)
    lo = (r1 - mid.astype(F32)).astype(BF)
    y = _dot(_tri(n, reverse).astype(BF), jnp.concatenate([hi, mid, lo], axis=1), ((1,), (0,)))
    w = x.shape[1]
    return (y[:, :w] + y[:, w:2 * w]) + y[:, 2 * w:]


@functools.partial(jax.custom_vjp, nondiff_argnums=(0, 1))
def _running_sum(n, reverse, x):
    return _tri_matmul(n, reverse, x)


def _running_sum_fwd(n, reverse, x):
    return _tri_matmul(n, reverse, x), None


def _running_sum_bwd(n, reverse, _, ct):
    return (_tri_matmul(n, not reverse, ct),)


_running_sum.defvjp(_running_sum_fwd, _running_sum_bwd)


def _gla_chunk(q_raw, f_raw, v, lb3, S, reverse):
    C, SB, HD = HG_CHUNK, HG_SUB, HG_HD
    H = S.shape[0]
    heads = [slice(HD * h, HD * (h + 1)) for h in range(H)]
    row3 = lax.broadcasted_iota(jnp.int32, (3, 1), 0)
    e = jnp.exp(lb3 - jnp.max(lb3, axis=0, keepdims=True))
    lb = jnp.sum(jnp.where(row3 == 0, e, 0.0), axis=0, keepdims=True) / jnp.sum(e, axis=0, keepdims=True)
    q = q_raw * jax.nn.sigmoid(q_raw)
    f = lb + (1.0 - lb) * jax.nn.sigmoid(f_raw)
    g = jnp.log(f)
    k = 1.0 - f
    b = _running_sum(C, reverse, g)
    row = lax.broadcasted_iota(jnp.int32, (C, 1), 0)
    vb = _bf(v)

    def rowof(x, t):
        return jnp.sum(jnp.where(row == t, x, 0.0), axis=0, keepdims=True)

    qe = _bf(q * jnp.exp(b))
    o = [_dot(qe[:, hs], _bf(S[h]), ((1,), (0,))) for h, hs in enumerate(heads)]
    att = [[] for _ in range(H)]
    for i in range(C // SB):
        lo = SB * i
        if (not reverse and i == 0) or (reverse and i == C // SB - 1):
            for h in range(H):
                att[h].append(jnp.zeros((SB, C), F32))
            continue
        first = lo + SB - 1 if reverse else lo
        r = rowof(b, first) - rowof(g, first)
        before = (row >= lo + SB) if reverse else (row < lo)
        qi = q[lo:lo + SB] * jnp.exp(b[lo:lo + SB] - r)
        kk = _bf(k * jnp.exp(jnp.where(before, r - b, NEG)))
        for h, hs in enumerate(heads):
            att[h].append(_dot(_bf(qi[:, hs]), kk[:, hs], ((1,), (1,))))
    o = [o[h] + _dot(_bf(jnp.concatenate(att[h], axis=0)), vb[:, hs], ((1,), (0,))) for h, hs in enumerate(heads)]
    s_i = lax.broadcasted_iota(jnp.int32, (SB, SB, HD), 0)
    t_i = lax.broadcasted_iota(jnp.int32, (SB, SB, HD), 1)
    pair = (t_i <= s_i) if reverse else (t_i >= s_i)
    shp = (SB, SB, HD)
    diag = [[] for _ in range(H)]
    for i in range(C // SB):
        rows = slice(SB * i, SB * (i + 1))
        for h, hs in enumerate(heads):
            qb, kb, bb = q[rows, hs], k[rows, hs], b[rows, hs]
            dif = lax.broadcast_in_dim(bb, shp, (1, 2)) - lax.broadcast_in_dim(bb, shp, (0, 2))
            w = lax.broadcast_in_dim(qb, shp, (1, 2)) * jnp.exp(jnp.where(pair, dif, NEG)) * lax.broadcast_in_dim(kb, shp, (0, 2))
            d = jnp.sum(w, axis=2, keepdims=True)
            diag[h].append(jnp.sum(d * lax.broadcast_in_dim(v[rows, hs], shp, (0, 2)), axis=0))
    o = jnp.concatenate([o[h] + jnp.concatenate(diag[h], axis=0) for h in range(H)], axis=1)
    btot = rowof(b, 0 if reverse else C - 1)
    kd = _bf(k * jnp.exp(btot - b))
    eye = lax.broadcasted_iota(jnp.int32, (HD, HD), 0) == lax.broadcasted_iota(jnp.int32, (HD, HD), 1)
    s_new = []
    for h, hs in enumerate(heads):
        btot_col = jnp.sum(jnp.where(eye, btot[:, hs], 0.0), axis=1, keepdims=True)
        s_new.append((jnp.exp(btot_col) * S[h] + _dot(kd[:, hs], vb[:, hs], ((0,), (0,))))[None])
    return o, jnp.concatenate(s_new, axis=0)


def _gla_fwd(name, pa, lbl, reverse, side=None):
    L = pa.shape[0]
    C = HG_CHUNK
    nc = L // C
    cidx = (lambda i: nc - 1 - i) if reverse else (lambda i: i)
    sec = 2 if reverse else 1
    hb_n, nh = HG_HB, HG_HEADS // HG_HB

    def fn(qr, fr, v, lb3, s_ref):
        @pl.when(pl.program_id(1) == 0)
        def _():
            s_ref[...] = jnp.zeros_like(s_ref)

        s_all = s_ref[...]
        o, s_new = _gla_chunk(qr, fr, v, lb3, s_all, reverse)
        s_ref[...] = s_new
        return o, s_all[None]

    blk = (C, HG_HD * hb_n)
    return _vcall(name, fn, (nh, nc),
                  [(pa, blk, lambda h, i: (cidx(i), h)), (pa, blk, lambda h, i: (cidx(i), sec * nh + h)),
                   (pa, blk, lambda h, i: (cidx(i), 3 * nh + h)), (lbl, (3, HG_HD * hb_n), lambda h, i: (0, h))],
                  [((L, D), F32, blk, lambda h, i: (cidx(i), h)),
                   ((nc, HG_HEADS, HG_HD, HG_HD), F32, (1, hb_n, HG_HD, HG_HD), lambda h, i: (cidx(i), h, 0, 0))],
                  scratch=[pltpu.VMEM((hb_n, HG_HD, HG_HD), F32)], side=side)


def _gla_bwd(name, pa, lbl, s_in, do, reverse, prev=None, side=None):
    L = pa.shape[0]
    C = HG_CHUNK
    nc = L // C
    cidx = (lambda i: i) if reverse else (lambda i: nc - 1 - i)
    sec = 2 if reverse else 1
    n_prev = 0 if prev is None else 2
    hb_n, nh = HG_HB, HG_HEADS // HG_HB

    def fn(qr, fr, v, lb3, s, dov, *rest):
        ds_ref = rest[n_prev]

        @pl.when(pl.program_id(1) == 0)
        def _():
            ds_ref[...] = jnp.zeros_like(ds_ref)

        _, vjp = jax.vjp(lambda *a: _gla_chunk(*a, reverse), qr, fr, v, lb3, s[0])
        dq, df, dv, dlb, ds = vjp((dov, ds_ref[...]))
        ds_ref[...] = ds
        if n_prev:
            dq, dv = dq + rest[0], dv + rest[1]
        return dq, df, dv, dlb

    blk = (C, HG_HD * hb_n)
    at = lambda h, i: (cidx(i), h)
    ins = [(pa, blk, at), (pa, blk, lambda h, i: (cidx(i), sec * nh + h)), (pa, blk, lambda h, i: (cidx(i), 3 * nh + h)),
           (lbl, (3, HG_HD * hb_n), lambda h, i: (0, h)),
           (s_in, (1, hb_n, HG_HD, HG_HD), lambda h, i: (cidx(i), h, 0, 0)), (do, blk, at)]
    if prev is not None:
        ins += [(prev[0], blk, at), (prev[1], blk, at)]
    sum_dt = F32 if prev is None else BF
    return _vcall(name, fn, (nh, nc), ins,
                  [((L, D), sum_dt, blk, at), ((L, D), BF, blk, at), ((L, D), sum_dt, blk, at),
                   ((3, D), F32, (3, HG_HD * hb_n), lambda h, i: (0, h))],
                  acc={3: "last"}, scratch=[pltpu.VMEM((hb_n, HG_HD, HG_HD), F32)], side=side)


def _hgout(o_f, o_b, g, nw):
    o = o_f + o_b
    return _rms(o, nw) * (g * jax.nn.sigmoid(g))


def _hgout_fwd(name, o_f, o_b, pa, nw, side=None):
    L = o_f.shape[0]
    tb = _pick(L, (GATE_ROWS, ROWS))
    blk = (tb, HG_HD)
    at = lambda h, i: (i, h)
    return _vcall(name, _hgout, (HG_HEADS, L // tb),
                  [(o_f, blk, at), (o_b, blk, at), (pa, blk, lambda h, i: (i, 32 + h)),
                   (nw.reshape(1, HG_HD), (1, HG_HD), lambda h, i: (0, 0))],
                  [((L, D), BF, blk, at)], side=side)


def _hgout_bwd(name, o_f, o_b, pa, nw, dy, side=None):
    L = o_f.shape[0]
    tb = _pick(L, (GATE_ROWS, ROWS))

    def fn(ofv, obv, gv, nwv, dyv):
        _, vjp = jax.vjp(_hgout, ofv, obv, gv, nwv)
        do, _, dg, dnw = vjp(dyv)
        return do, dg, dnw

    blk = (tb, HG_HD)
    at = lambda h, i: (i, h)
    zero = lambda h, i: (0, 0)
    return _vcall(name, fn, (HG_HEADS, L // tb),
                  [(o_f, blk, at), (o_b, blk, at), (pa, blk, lambda h, i: (i, 32 + h)),
                   (nw.reshape(1, HG_HD), (1, HG_HD), zero), (dy, blk, at)],
                  [((L, D), F32, blk, at), ((L, D), BF, blk, at), ((1, HG_HD), F32, (1, HG_HD), zero)],
                  acc={2: "all"}, side=side)


def _shift(x, s):
    if s == 0:
        return x
    n = x.shape[0]
    t = lax.broadcasted_iota(jnp.int32, (n, 1), 0)
    if s > 0:
        return jnp.where(t >= s, pltpu.roll(x, s, 0), 0.0)
    return jnp.where(t < n + s, pltpu.roll(x, n + s, 0), 0.0)


def _conv(x, w, b):
    kk = w.shape[0]
    p = (kk - 1) // 2
    y = b
    for j in range(kk):
        y = y + w[j:j + 1] * _shift(x, p - j)
    return y


def _conv_bwd(x, w, dc):
    kk = w.shape[0]
    p = (kk - 1) // 2
    dx = None
    dws = []
    for j in range(kk):
        t = w[j:j + 1] * _shift(dc, j - p)
        dx = t if dx is None else dx + t
        dws.append(jnp.sum(dc * _shift(x, p - j), axis=0, keepdims=True))
    rows = lax.broadcasted_iota(jnp.int32, (kk, 1), 0)
    dw = None
    for j in range(kk):
        t = jnp.where(rows == j, dws[j], 0.0)
        dw = t if dw is None else dw + t
    return dx, dw, jnp.sum(dc, axis=0, keepdims=True)


def _silu(c):
    return c * jax.nn.sigmoid(c)


def _silu_grad(c):
    s = jax.nn.sigmoid(c)
    return s * (1.0 + c * (1.0 - s))


def _glu_fwd(name, pf, cw, cb):
    L = pf.shape[0]
    tc = 128
    nt = D_FF // tc
    return _vcall(name, lambda gate, val, w, b: _silu(_conv(gate, w, b)) * val, (nt,),
                  [(pf, (L, tc), lambda j: (0, j)), (pf, (L, tc), lambda j: (0, nt + j)),
                   (cw, (3, tc), lambda j: (0, j)), (cb.reshape(1, D_FF), (1, tc), lambda j: (0, j))],
                  [((L, D_FF), BF, (L, tc), lambda j: (0, j))])


def _glu_bwd(name, pf, cw, cb, dy, side=None):
    L = pf.shape[0]
    tc = 128
    nt = D_FF // tc

    def fn(gate, val, w, b, dyv):
        c = _conv(gate, w, b)
        dgate, dw, db = _conv_bwd(gate, w, dyv * val * _silu_grad(c))
        return dgate, dyv * _silu(c), dw, db

    col = lambda j: (0, j)
    return _vcall(name, fn, (nt,),
                  [(pf, (L, tc), col), (pf, (L, tc), lambda j: (0, nt + j)), (cw, (3, tc), col),
                   (cb.reshape(1, D_FF), (1, tc), col), (dy, (L, tc), col)],
                  [((L, D_FF), BF, (L, tc), col), ((L, D_FF), BF, (L, tc), col),
                   ((3, D_FF), F32, (3, tc), col), ((1, D_FF), F32, (1, tc), col)], side=side)


def _perm_tile(j):
    return jnp.where(j < 16, 4 * (j // 2) + j % 2, jnp.where(j < 24, 4 * (j - 16) + 2, 4 * (j - 24) + 3))


def _mpre_fwd(name, pb, cw, cb):
    L = pb.shape[0]
    tc = 128
    return _vcall(name, lambda x, w, b: _silu(_conv(x, w, b)), (CONV_DIM // tc,),
                  [(pb, (L, tc), lambda j: (0, 16 + j)), (cw, (5, tc), lambda j: (0, j)),
                   (cb.reshape(1, CONV_DIM), (1, tc), lambda j: (0, j))],
                  [((L, CONV_DIM), F32, (L, tc), lambda j: (0, _perm_tile(j)))])


def _mpre_bwd(name, pb, cw, cb, dact):
    L = pb.shape[0]
    tc = 128
    col = lambda j: (0, j)
    return _vcall(name, lambda x, w, b, da: _conv_bwd(x, w, da * _silu_grad(_conv(x, w, b))), (CONV_DIM // tc,),
                  [(pb, (L, tc), lambda j: (0, 16 + j)), (cw, (5, tc), col), (cb.reshape(1, CONV_DIM), (1, tc), col),
                   (dact, (L, tc), lambda j: (0, _perm_tile(j)))],
                  [((L, CONV_DIM), BF, (L, tc), col), ((5, CONV_DIM), F32, (5, tc), col),
                   ((1, CONV_DIM), F32, (1, tc), col)])


def _softplus(x):
    return jnp.maximum(x, 0.0) + jnp.log(1.0 + jnp.exp(-jnp.abs(x)))


def _dt_fwd(name, pb, dtb, alog):
    L = pb.shape[0]
    tb = _pick(L, (1024, ROWS))

    def fn(x, bias, al):
        dt = _softplus(x + bias)
        return dt, dt * (-jnp.exp(al))

    row = lambda i: (i, 0)
    zero = lambda i: (0, 0)
    return _vcall(name, fn, (L // tb,),
                  [(pb, (tb, 128), lambda i: (i, 48)), (dtb, (1, 128), zero), (alog, (1, 128), zero)],
                  [((L, 128), F32, (tb, 128), row), ((L, 128), F32, (tb, 128), row)])


def _dt_bwd(name, pb, dtb, alog, ddt_f, dla_f, ddt_b, dla_b):
    L = pb.shape[0]
    tb = _pick(L, (1024, ROWS))

    def fn(x, bias, al, a1, b1, a2, b2):
        ddt = jnp.sum(a1, axis=0) + jnp.sum(a2, axis=0)
        dla = jnp.sum(b1, axis=0) + jnp.sum(b2, axis=0)
        z = x + bias
        dt = _softplus(z)
        a = -jnp.exp(al)
        dz = (ddt + dla * a) * jax.nn.sigmoid(z)
        return dz, jnp.sum(dz, axis=0, keepdims=True), jnp.sum(dla * dt, axis=0, keepdims=True) * a

    zero = lambda i: (0, 0)
    g3 = (ddt_f.shape[0], tb, 128)
    at3 = lambda i: (0, i, 0)
    return _vcall(name, fn, (L // tb,),
                  [(pb, (tb, 128), lambda i: (i, 48)), (dtb, (1, 128), zero), (alog, (1, 128), zero),
                   (ddt_f, g3, at3), (dla_f, g3, at3), (ddt_b, g3, at3), (dla_b, g3, at3)],
                  [((L, 128), BF, (tb, 128), lambda i: (i, 0)), ((1, 128), F32, (1, 128), zero),
                   ((1, 128), F32, (1, 128), zero)], acc={1: "all", 2: "all"})


def _split_dot(x, e, dims, pieces):
    hi = x.astype(BF)
    r1 = x - hi.astype(F32)
    mid = r1.astype(BF)
    y = _dot(hi, e, dims) + _dot(mid, e, dims)
    if pieces == 3:
        y = y + _dot((r1 - mid.astype(F32)).astype(BF), e, dims)
    return y


@functools.partial(jax.custom_vjp, nondiff_argnums=(2,))
def _spread(x, e, pieces):
    return _split_dot(x, e, ((1,), (0,)), pieces)


def _spread_fwd(x, e, pieces):
    return _split_dot(x, e, ((1,), (0,)), pieces), e


def _spread_bwd(pieces, e, ct):
    return _split_dot(ct, e, ((1,), (1,)), pieces), jnp.zeros_like(e)


_spread.defvjp(_spread_fwd, _spread_bwd)


def _ssd_chunk(xa, dt, la, hs, head0, reverse):
    C, P4, HD, N = SSD_CHUNK, SSM_HPG * SSM_HD, SSM_HD, SSM_N
    G = hs.shape[0]
    nh = SSM_HPG * G
    row = lax.broadcasted_iota(jnp.int32, (C, 1), 0)
    lane = lax.broadcasted_iota(jnp.int32, (1, 128), 1)
    eye = lax.broadcasted_iota(jnp.int32, (C, C), 0) == lax.broadcasted_iota(jnp.int32, (C, C), 1)
    tri = _tri(C, reverse)
    last = 0 if reverse else C - 1
    acum = _running_sum(C, reverse, la)
    atot = jnp.sum(jnp.where(row == last, acum, 0.0), axis=0, keepdims=True)
    src = lax.broadcasted_iota(jnp.int32, (128, 1), 0) - head0
    to_x = (src == lax.broadcasted_iota(jnp.int32, (1, nh * HD), 1) // HD).astype(BF)
    dt_x = _spread(dt, to_x, 2)
    ea_x = _spread(jnp.exp(acum), to_x, 2)
    dec_x = _spread(jnp.exp(atot - acum), to_x, 2)
    col_head = lax.broadcasted_iota(jnp.int32, (1, P4), 1) // HD
    row_head = lax.broadcasted_iota(jnp.int32, (P4, 1), 0) // HD
    ys, news = [], []
    for gi in range(G):
        xs = xa[:, 512 * gi:512 * gi + P4]
        bm = _bf(xa[:, 512 * gi + P4:512 * gi + P4 + N])
        cm = _bf(xa[:, 512 * gi + P4 + N:512 * (gi + 1)])
        gx = slice(P4 * gi, P4 * (gi + 1))
        cb = _dot(cm, bm, ((1,), (1,)))
        xd = xs * dt_x[:, gx]
        ms, xds, scale = [], [], 0.0
        for j in range(SSM_HPG):
            i = SSM_HPG * gi + j
            ac = jnp.sum(jnp.where(lane == head0 + i, acum, 0.0), axis=1, keepdims=True)
            ac_row = jnp.sum(jnp.where(eye, ac, 0.0), axis=0, keepdims=True)
            ms.append(_bf(cb * jnp.exp(jnp.where(tri, ac - ac_row, NEG))))
            xds.append(_bf(jnp.where(col_head == j, xd, 0.0)))
            a_i = jnp.sum(jnp.where(lane == head0 + i, atot, 0.0), axis=1, keepdims=True)
            scale = scale + jnp.where(row_head == j, jnp.exp(a_i), 0.0)
        y = _dot(jnp.concatenate(ms, axis=1), jnp.concatenate(xds, axis=0), ((1,), (0,)))
        y = y + _dot(cm, _bf(hs[gi]), ((1,), (1,))) * ea_x[:, gx]
        ys.append(y)
        news.append((scale * hs[gi] + _dot(_bf(xd * dec_x[:, gx]), bm, ((0,), (0,))))[None])
    return jnp.concatenate(ys, axis=1), jnp.concatenate(news, axis=0)


def _ssd_fwd(name, xact, dt, la, reverse):
    L = xact.shape[0]
    C = SSD_CHUNK
    nc = L // C
    cidx = (lambda i: nc - 1 - i) if reverse else (lambda i: i)
    base = SSM_HEADS if reverse else 0
    P4 = SSM_HPG * SSM_HD

    gb_n = SSM_GB

    def fn(xa, dtv, lav, h_ref):
        @pl.when(pl.program_id(1) == 0)
        def _():
            h_ref[...] = jnp.zeros_like(h_ref)

        h_all = h_ref[...]
        y, h_new = _ssd_chunk(xa, dtv, lav, h_all, base + SSM_HPG * gb_n * pl.program_id(0), reverse)
        h_ref[...] = h_new
        return y, h_all[None]

    return _vcall(name, fn, (SSM_GROUPS // gb_n, nc),
                  [(xact, (C, 512 * gb_n), lambda g, i: (cidx(i), g)), (dt, (C, 128), lambda g, i: (cidx(i), 0)),
                   (la, (C, 128), lambda g, i: (cidx(i), 0))],
                  [((L, D_INNER), F32, (C, P4 * gb_n), lambda g, i: (cidx(i), g)),
                   ((nc, SSM_GROUPS, P4, SSM_N), F32, (1, gb_n, P4, SSM_N), lambda g, i: (cidx(i), g, 0, 0))],
                  scratch=[pltpu.VMEM((gb_n, P4, SSM_N), F32)])


def _ssd_bwd(name, xact, dt, la, h_in, dy, reverse, prev_xs=None, prev_all=None, side=None):
    L = xact.shape[0]
    C = SSD_CHUNK
    nc = L // C
    cidx = (lambda i: i) if reverse else (lambda i: nc - 1 - i)
    base = SSM_HEADS if reverse else 0
    P4 = SSM_HPG * SSM_HD

    gb_n = SSM_GB

    def fn(xa, dtv, lav, hs, dyv, pv, dh_ref):
        @pl.when(pl.program_id(1) == 0)
        def _():
            dh_ref[...] = jnp.zeros_like(dh_ref)

        head0 = base + SSM_HPG * gb_n * pl.program_id(0)
        _, vjp = jax.vjp(lambda a, b, c, d: _ssd_chunk(a, b, c, d, head0, reverse), xa, dtv, lav, hs[0])
        dxa, ddt, dla, dh = vjp((dyv, dh_ref[...]))
        dh_ref[...] = dh
        if prev_all is not None:
            dxa = dxa + pv
        else:
            zeros = jnp.zeros((C, 2 * SSM_N), F32)
            dxa = dxa + jnp.concatenate([t for gb in range(gb_n) for t in (pv[:, P4 * gb:P4 * (gb + 1)], zeros)], axis=1)
        return dxa, ddt[None], dla[None]

    at = lambda g, i: (cidx(i), g)
    at0 = lambda g, i: (cidx(i), 0)
    pv = (prev_all, (C, 512 * gb_n), at) if prev_all is not None else (prev_xs, (C, P4 * gb_n), at)
    steps = SSM_GROUPS // gb_n
    return _vcall(name, fn, (steps, nc),
                  [(xact, (C, 512 * gb_n), at), (dt, (C, 128), at0), (la, (C, 128), at0),
                   (h_in, (1, gb_n, P4, SSM_N), lambda g, i: (cidx(i), g, 0, 0)), (dy, (C, P4 * gb_n), at), pv],
                  [((L, CONV_DIM), F32, (C, 512 * gb_n), at),
                   ((steps, L, 128), F32, (1, C, 128), lambda g, i: (g, cidx(i), 0)),
                   ((steps, L, 128), F32, (1, C, 128), lambda g, i: (g, cidx(i), 0))],
                  scratch=[pltpu.VMEM((gb_n, P4, SSM_N), F32)], side=side)


def _mpost(y_f, y_b, xs, z, dsk, nw):
    y = (y_f + y_b + xs * dsk) * (z * jax.nn.sigmoid(z))
    return _rms(y, nw)


def _mpost_fwd(name, y_f, y_b, xact, pb, dsk, nw):
    L = y_f.shape[0]
    tb = _pick(L, (GATE_ROWS, ROWS))
    blk = (tb, 256)
    at = lambda g, i: (i, g)
    par = lambda g, i: (0, g)
    return _vcall(name, _mpost, (SSM_GROUPS, L // tb),
                  [(y_f, blk, at), (y_b, blk, at), (xact, blk, lambda g, i: (i, 2 * g)), (pb, blk, at),
                   (dsk, (1, 256), par), (nw, (1, 256), par)],
                  [((L, D_INNER), BF, blk, at)])


def _mpost_bwd(name, y_f, y_b, xact, pb, dsk, nw, dy):
    L = y_f.shape[0]
    tb = _pick(L, (GATE_ROWS, ROWS))

    def fn(yf, yb, xs, z, dskv, nwv, dyv):
        _, vjp = jax.vjp(_mpost, yf, yb, xs, z, dskv, nwv)
        dyf, _, dxs, dz, ddsk, dnw = vjp(dyv)
        return dyf, dxs, dz, ddsk, dnw

    blk = (tb, 256)
    at = lambda g, i: (i, g)
    par = lambda g, i: (0, g)
    return _vcall(name, fn, (SSM_GROUPS, L // tb),
                  [(y_f, blk, at), (y_b, blk, at), (xact, blk, lambda g, i: (i, 2 * g)), (pb, blk, at),
                   (dsk, (1, 256), par), (nw, (1, 256), par), (dy, blk, at)],
                  [((L, D_INNER), F32, blk, at), ((L, D_INNER), F32, blk, at), ((L, D_INNER), BF, blk, at),
                   ((1, D_INNER), F32, (1, 256), par), ((1, D_INNER), F32, (1, 256), par)],
                  acc={3: "last", 4: "last"})


def _ffn_fwd(tag, h, nw, w_in, cw, cb, w_out):
    u = _rms_fwd(f"{tag}_norm", h, nw)
    pf = _mm(f"{tag}_in", u, w_in, "nn")
    yf = _glu_fwd(f"{tag}_glu", pf, cw, cb)
    return _mm(f"{tag}_out", yf, w_out, "nn", add=h), (u, pf, yf)


def _ffn_bwd(tag, h, nw, w_in, cw, cb, w_out, saved, dh, side=None):
    u, pf, yf = saved
    d_w_out = _mm(f"{tag}_dwout", yf, dh, "tn")
    dyf = _mm(f"{tag}_dy", dh, w_out, "nt")
    dgate, dval, dcw, dcb = _glu_bwd(f"{tag}_dglu", pf, cw, cb, dyf)
    dpf = jnp.concatenate([dgate, dval], axis=1)
    d_w_in = _mm(f"{tag}_dwin", u, dpf, "tn")
    got = ()
    if side is None:
        du = _mm(f"{tag}_du", dpf, w_in, "nt")
    else:
        (du,), got = _mm(f"{tag}_du", dpf, w_in, "nt", side=side)
    dh_in, dnw = _rms_bwd(f"{tag}_dnorm", du, h, nw, dh)
    return dh_in, dnw, d_w_in, dcw, dcb, d_w_out, got


def _sequence_grads(x, tgt, p, sh, place):
    g = {}
    lbl = p["a_lb_logits"]
    first, early, mid, last = ("a_in", "a_out"), ("f0_in",), ("b_in", "b_out"), ("f1_in", "f1_out", "f0_out")
    W = {}
    got = _exchange("w_first", _gather_side(first, (), sh, W))
    W.update(zip(first, got))
    got = _exchange("w_first_pass", _gather_side((), first, sh, W))
    W.update(zip(first, got))
    u1 = _rms_fwd("a_norm", x, p["norm1_w"][0])
    (pa,), got = _mm("a_in", u1, W["a_in"], "nn", side=_gather_side(early, (), sh, W))
    W.update(zip(early, got))
    (o_f, s_f), got = _gla_fwd("a_scan_f", pa, lbl, False, side=_gather_side(mid, early, sh, W))
    W.update(zip(mid + early, got))
    (o_b, s_b), got = _gla_fwd("a_scan_b", pa, lbl, True, side=_gather_side(last, mid, sh, W))
    W.update(zip(last + mid, got))
    (ya,), got = _hgout_fwd("a_gate", o_f, o_b, pa, p["a_norm_w"], side=_gather_side((), last, sh, W))
    W.update(zip(last, got))
    wb4 = W["b_in"].reshape(4, D, B_PROJ // 4)
    p = dict(p, a_w_in=W["a_in"], a_w_out=W["a_out"], b_w_out=W["b_out"], ffn_w_in=(W["f0_in"], W["f1_in"]),
             ffn_w_out=(W["f0_out"], W["f1_out"]),
             b_w_in=jnp.pad(jnp.concatenate([wb4[j] for j in range(4)], axis=1), ((0, 0), (0, B_PROJ_PAD - B_PROJ))))
    h1 = _mm("a_out", ya, p["a_w_out"], "nn", add=x)
    h2, ffn0 = _ffn_fwd("f0", h1, p["norm2_w"][0], p["ffn_w_in"][0], p["ffn_conv_w"][0], p["ffn_conv_b"][0], p["ffn_w_out"][0])
    u3 = _rms_fwd("b_norm", h2, p["norm1_w"][1])
    pb = _mm("b_in", u3, p["b_w_in"], "nn")
    xact = _mpre_fwd("b_conv", pb, p["b_conv_w"], p["b_conv_b"])
    dt, la = _dt_fwd("b_dt", pb, p["b_dt_bias"], p["b_a_log"])
    y_f, hs_f = _ssd_fwd("b_scan_f", xact, dt, la, False)
    y_b, hs_b = _ssd_fwd("b_scan_b", xact, dt, la, True)
    yb = _mpost_fwd("b_gate", y_f, y_b, xact, pb, p["b_d_skip"], p["b_norm_w"])
    h3 = _mm("b_out", yb, p["b_w_out"], "nn", add=h2)
    h4, ffn1 = _ffn_fwd("f1", h3, p["norm2_w"][1], p["ffn_w_in"][1], p["ffn_conv_w"][1], p["ffn_conv_b"][1], p["ffn_w_out"][1])
    loss, dh4, g["final_norm_w"] = _loss_head("head", h4, tgt, p["final_norm_w"])
    dh3, dn2_1, dwin1, dcw1, dcb1, dwout1, _ = _ffn_bwd("f1", h3, p["norm2_w"][1], p["ffn_w_in"][1], p["ffn_conv_w"][1],
                                                        p["ffn_conv_b"][1], p["ffn_w_out"][1], ffn1, dh4)
    G = {"f1_in": dwin1, "f1_out": dwout1}
    G["b_out"] = _mm("b_dwout", yb, dh3, "tn")
    wave1 = ("f1_in", "f1_out", "b_out")
    (dyb,), got = _mm("b_dy", dh3, p["b_w_out"], "nt", side=_pair_side(wave1, G))
    chip_sums = _pair_sums(wave1, G, got)
    dys, dxs, dz, g["b_d_skip"], g["b_norm_w"] = _mpost_bwd("b_dgate", y_f, y_b, xact, pb, p["b_d_skip"], p["b_norm_w"], dyb)
    dxa1, ddt_f, dla_f = _ssd_bwd("b_dscan_f", xact, dt, la, hs_f, dys, False, prev_xs=dxs)
    dxa, ddt_b, dla_b = _ssd_bwd("b_dscan_b", xact, dt, la, hs_b, dys, True, prev_all=dxa1)
    dxbc, g["b_conv_w"], g["b_conv_b"] = _mpre_bwd("b_dconv", pb, p["b_conv_w"], p["b_conv_b"], dxa)
    ddtr, g["b_dt_bias"], g["b_a_log"] = _dt_bwd("b_ddt", pb, p["b_dt_bias"], p["b_a_log"], ddt_f, dla_f, ddt_b, dla_b)
    dpb = jnp.concatenate([dz, dxbc, ddtr], axis=1)
    G["b_in"] = _mm("b_dwin", dpb, u3, "tn")
    du3 = _mm("b_du", dpb, p["b_w_in"], "nt")
    dh2, dn1_1 = _rms_bwd("b_dnorm", du3, h2, p["norm1_w"][1], dh3)
    dh1, dn2_0, G["f0_in"], dcw0, dcb0, G["f0_out"], got = _ffn_bwd("f0", h1, p["norm2_w"][0], p["ffn_w_in"][0], p["ffn_conv_w"][0],
                                                                   p["ffn_conv_b"][0], p["ffn_w_out"][0], ffn0, dh2,
                                                                   side=_pair_side(("b_in",), G))
    chip_sums.update(_pair_sums(("b_in",), G, got))
    wave3 = ("f0_in", "f0_out")
    (G["a_out"],), got = _mm("a_dwout", ya, dh1, "tn", side=_pair_side(wave3, G))
    chip_sums.update(_pair_sums(wave3, G, got))
    dya = _mm("a_dy", dh1, p["a_w_out"], "nt")
    do, dg, g["a_norm_w"] = _hgout_bwd("a_dgate", o_f, o_b, pa, p["a_norm_w"], dya)
    late = last + mid + early
    (dq1, df1, dv1, dl1), got = _gla_bwd("a_dscan_f", pa, lbl, s_f, do, False, side=_chips_side(late, chip_sums))
    shards = {u: _chip_sum(f"gl_sum_{u}", _GGEO[u], chip_sums[u], r, place) for u, r in zip(late, got)}
    (dq, df2, dv, dl2), got = _gla_bwd("a_dscan_b", pa, lbl, s_b, do, True, prev=(dq1, dv1), side=_halves_side(late, shards))
    shards = dict(zip(late, got))
    dpa = jnp.concatenate([dq, df1, df2, dv, dg], axis=1)
    G["a_in"] = _mm("a_dwin", u1, dpa, "tn")
    chip_sums = _pair_sums(first, G, _exchange("ga_pair", _pair_side(first, G)))
    (du1,), got = _mm("a_du", dpa, p["a_w_in"], "nt", side=_chips_side(first, chip_sums))
    mine = {u: _chip_sum(f"ga_sum_{u}", _GGEO[u], chip_sums[u], r, place) for u, r in zip(first, got)}
    (dx, dn1_0), got = _rms_bwd("a_dnorm", du1, x, p["norm1_w"][0], dh1, side=_halves_side(first, mine))
    shards.update(zip(first, got))
    g["a_lb_logits"] = (dl1, dl2)
    g["norm1_w"] = (dn1_0, dn1_1)
    g["norm2_w"] = (dn2_0, dn2_1)
    g["ffn_conv_w"] = (dcw0, dcw1)
    g["ffn_conv_b"] = (dcb0, dcb1)
    return loss, dx, g, shards


def _here():
    return lax.axis_index("x"), lax.axis_index("y"), lax.axis_index("c")


def _allgather8(name, src, by_core=False):
    blk = src.shape[1:] if by_core else src.shape

    def body(x_ref, out_ref, send_sems, recv_sems, local_sem):
        x, y, c = _here()
        me, sibling = (x, y, c), (x, y, 1 - c)
        chips = [(1 - x, y), (x, 1 - y), (1 - x, 1 - y)]
        own = x_ref.at[c] if by_core else x_ref

        def slot(px, py, pc):
            return out_ref.at[4 * px + 2 * py + pc]

        def copy(k, block, to, from_own=False):
            return pltpu.make_async_remote_copy(
                src_ref=own if from_own else slot(*block), dst_ref=slot(*block),
                send_sem=send_sems.at[k], recv_sem=recv_sems.at[k], device_id=to, device_id_type=MESH)

        mine = pltpu.make_async_copy(own, slot(*me), local_sem)
        mine.start()
        first = [copy(0, me, sibling, from_own=True)]
        first += [copy(1 + j, me, (*chip, c), from_own=True) for j, chip in enumerate(chips)]
        for cp in first:
            cp.start()
        passed = [copy(4 + j, (*chip, c), sibling) for j, chip in enumerate(chips)]
        for j, chip in enumerate(chips):
            copy(1 + j, (*chip, c), me).wait_recv()
            passed[j].start()
        copy(0, sibling, me).wait_recv()
        for j, chip in enumerate(chips):
            copy(4 + j, (*chip, 1 - c), me).wait_recv()
        for cp in first + passed:
            cp.wait_send()
        mine.wait()

    return pl.pallas_call(
        body, name=name,
        out_shape=jax.ShapeDtypeStruct((8,) + tuple(blk), src.dtype),
        in_specs=[pl.BlockSpec(memory_space=pl.ANY)],
        out_specs=pl.BlockSpec(memory_space=pl.ANY),
        scratch_shapes=[pltpu.SemaphoreType.DMA((7,)), pltpu.SemaphoreType.DMA((7,)), pltpu.SemaphoreType.DMA],
    )(src)


def _exchange(name, side):
    n_i, n_o = len(side.ins), len(side.outs)

    def body(*refs):
        copies = side.copies(refs[:n_i], refs[n_i:n_i + n_o], *refs[n_i + n_o:])
        for cp in copies:
            cp.start()
        for cp in copies:
            cp.wait()

    return pl.pallas_call(
        body, name=name,
        out_shape=[jax.ShapeDtypeStruct(s, dt) for s, dt in side.outs],
        in_specs=[pl.BlockSpec(memory_space=pl.ANY)] * n_i,
        out_specs=[pl.BlockSpec(memory_space=pl.ANY)] * n_o,
        scratch_shapes=side.sems(),
        input_output_aliases=dict(side.alias),
    )(*side.ins)


_WGEO = {"a_in": ("col", 1024, 1280), "a_out": ("row", 256, 1024), "b_in": ("row", 1024, 1552), "b_out": ("row", 512, 1024),
         "f0_in": ("col", 1024, 1408), "f1_in": ("col", 1024, 1408), "f0_out": ("row", 704, 1024), "f1_out": ("row", 704, 1024)}
_GGEO = dict(_WGEO, b_in=("row", 1552, 1024))


def _full_shape(geo):
    kind, r, cw = geo
    return (r, 4 * cw) if kind == "col" else (4 * r, cw)


def _times(i, step):
    return i * step if isinstance(i, int) else pl.multiple_of(i * step, step & -step)


def _win(ref, geo, j, h):
    kind, r, cw = geo
    hr = r // 2
    if kind == "col":
        return ref.at[pl.ds(_times(h, hr), hr), pl.ds(_times(j, cw), cw)]
    return ref.at[pl.ds(_times(2 * j + h, hr), hr), :]


def _half(ref, geo, h):
    hr = geo[1] // 2
    return ref.at[pl.ds(_times(h, hr), hr), :]


def _gather_side(first, second, sh, full):
    n1 = len(first)

    def plan(ins, outs):
        x, y, c = _here()
        m = 2 * x + y
        remote, local = [], []
        for u, src, dst_full in zip(first, ins[:n1], outs[:n1]):
            mine, dst = _half(src, _WGEO[u], c), _win(dst_full, _WGEO[u], m, c)
            local.append((mine, dst))
            remote.append((mine, dst, (x, y, 1 - c)))
            for k in (1, 2, 3):
                t = (m + k) % 4
                remote.append((mine, dst, (t // 2, t % 2, c)))
        for u, buf in zip(second, outs[n1:]):
            for k in (1, 2, 3):
                w_ = _win(buf, _WGEO[u], (m + k) % 4, c)
                remote.append((w_, w_, (x, y, 1 - c)))
        return remote, local

    return _Side([sh[u] for u in first] + [full[u] for u in second],
                 [(_full_shape(_WGEO[u]), BF) for u in first + second], plan, 4 * n1 + 3 * len(second), n1,
                 alias={n1 + i: n1 + i for i in range(len(second))})


def _pair_side(units, G):
    def plan(ins, outs):
        x, y, c = _here()
        return [(_win(gr, _GGEO[u], j, 1 - c), got.at[j], (x, y, 1 - c))
                for u, gr, got in zip(units, ins, outs) for j in range(4)], []

    return _Side([G[u] for u in units], [((4, _GGEO[u][1] // 2, _GGEO[u][2]), F32) for u in units], plan, 4 * len(units), 0)


def _pair_sums(units, G, gots):
    out = {}
    for u, got in zip(units, gots):
        blk = got.shape[1:]
        at = (lambda j: (lax.axis_index("c"), j)) if _GGEO[u][0] == "col" else (lambda j: (2 * j + lax.axis_index("c"), 0))
        slab = lambda j: (j, 0, 0)
        out[u] = _vcall(f"g_pair_sum_{u}", lambda a, b: (a + b[0])[None], (4,),
                        [(G[u], blk, at), (got, (1,) + blk, slab)], [(got.shape, BF, (1,) + blk, slab)])
    return out


def _chips_side(units, chip_sums):
    def plan(ins, outs):
        x, y, c = _here()
        m = 2 * x + y
        remote = []
        for s, got in zip(ins, outs):
            for k in (1, 2, 3):
                t = (m + k) % 4
                remote.append((s.at[t], got.at[k - 1], (t // 2, t % 2, c)))
        return remote, []

    return _Side([chip_sums[u] for u in units], [((3,) + chip_sums[u].shape[1:], BF) for u in units], plan,
                 3 * len(units), 0)


def _chip_sum(name, geo, chip_sums, got, place):
    _, r, cw = geo
    blk = (r // 2, cw)
    return _vcall(name, lambda a, b: ((a[0].astype(F32) + b[0].astype(F32)) + b[1].astype(F32)) + b[2].astype(F32), (1,),
                  [(chip_sums, (1,) + blk, lambda i: (2 * lax.axis_index("x") + lax.axis_index("y"), 0, 0)),
                   (got, (3,) + blk, lambda i: (0, 0, 0))],
                  [((r, cw), F32, blk, lambda i: (lax.axis_index("c"), 0))])


def _halves_side(units, shards):
    def plan(ins, outs):
        x, y, c = _here()
        return [(_half(o, _GGEO[u], c), _half(o, _GGEO[u], c), (x, y, 1 - c)) for u, o in zip(units, outs)], []

    return _Side([shards[u] for u in units], [(shards[u].shape, F32) for u in units], plan, len(units), 0,
                 alias={i: i for i in range(len(units))})


def _adam(name, w, g, m, v):
    rows, cols = w.shape
    tb = _pick(rows, (256, 128, 64, 8))

    def fn(wv, gv, mv, vv):
        m2 = ADAM_B1 * mv + (1.0 - ADAM_B1) * gv
        v2 = ADAM_B2 * vv + (1.0 - ADAM_B2) * jnp.square(gv)
        m_hat = m2 / (1.0 - ADAM_B1 ** ADAM_STEP)
        v_hat = v2 / (1.0 - ADAM_B2 ** ADAM_STEP)
        return -ADAM_LR * (m_hat / (jnp.sqrt(v_hat) + ADAM_EPS) + ADAM_WD * wv), m2, v2

    at = lambda i: (i, 0)
    return _vcall(name, fn, (rows // tb,), [(a, (tb, cols), at) for a in (w, g, m, v)],
                  [((rows, cols), F32, (tb, cols), at)] * 3)


def _pack(arrays, width, row_multiple, dtype):
    parts, offs, at = [], [], 0
    for a in arrays:
        flat = a.reshape(-1).astype(dtype)
        rows = -(-flat.shape[0] // (width * row_multiple)) * row_multiple
        parts.append(jnp.pad(flat, (0, rows * width - flat.shape[0])).reshape(rows, width))
        offs.append(at)
        at += rows
    return jnp.concatenate(parts, axis=0), offs


def _unpack(flat, shapes, offs):
    out = []
    for shp, at in zip(shapes, offs):
        n = 1
        for s in shp:
            n *= s
        rows = -(-n // flat.shape[1])
        out.append(flat[at:at + rows].reshape(-1)[:n].reshape(shp))
    return out


_BIG = ("a_w_in", "a_w_out", "b_w_in", "b_w_out", "ffn_w_in", "ffn_w_out")
_BIG_AXIS = {"a_w_in": 2, "a_w_out": 1, "b_w_in": 2, "b_w_out": 1, "ffn_w_in": 2, "ffn_w_out": 1}
_SMALL_SPLIT = ("b_conv_w", "b_conv_b", "b_norm_w", "ffn_conv_w")
_SMALL = ("norm1_w", "norm2_w", "a_lb_logits", "a_norm_w", "b_conv_w", "b_conv_b", "b_dt_bias", "b_a_log", "b_d_skip",
          "b_norm_w", "ffn_conv_w", "ffn_conv_b", "final_norm_w")
_ORDER = ("norm1_w", "norm2_w", "a_w_in", "a_lb_logits", "a_norm_w", "a_w_out", "b_w_in", "b_conv_w", "b_conv_b", "b_dt_bias",
          "b_a_log", "b_d_skip", "b_norm_w", "b_w_out", "ffn_w_in", "ffn_conv_w", "ffn_conv_b", "ffn_w_out", "final_norm_w")


def kernel(x, norm1_w, norm2_w, a_w_in, a_lb_logits, a_norm_w, a_w_out, b_w_in, b_conv_w, b_conv_b, b_dt_bias, b_a_log, b_d_skip, b_norm_w, b_w_out, ffn_w_in, ffn_conv_w, ffn_conv_b, ffn_w_out, final_norm_w, loss_target, m_norm1_w, m_norm2_w, m_a_w_in, m_a_lb_logits, m_a_norm_w, m_a_w_out, m_b_w_in, m_b_conv_w, m_b_conv_b, m_b_dt_bias, m_b_a_log, m_b_d_skip, m_b_norm_w, m_b_w_out, m_ffn_w_in, m_ffn_conv_w, m_ffn_conv_b, m_ffn_w_out, m_final_norm_w, v_norm1_w, v_norm2_w, v_a_w_in, v_a_lb_logits, v_a_norm_w, v_a_w_out, v_b_w_in, v_b_conv_w, v_b_conv_b, v_b_dt_bias, v_b_a_log, v_b_d_skip, v_b_norm_w, v_b_w_out, v_ffn_w_in, v_ffn_conv_w, v_ffn_conv_b, v_ffn_w_out, v_final_norm_w):
    w = dict(norm1_w=norm1_w, norm2_w=norm2_w, a_w_in=a_w_in, a_lb_logits=a_lb_logits, a_norm_w=a_norm_w, a_w_out=a_w_out,
             b_w_in=b_w_in, b_conv_w=b_conv_w, b_conv_b=b_conv_b, b_dt_bias=b_dt_bias, b_a_log=b_a_log, b_d_skip=b_d_skip,
             b_norm_w=b_norm_w, b_w_out=b_w_out, ffn_w_in=ffn_w_in, ffn_conv_w=ffn_conv_w, ffn_conv_b=ffn_conv_b,
             ffn_w_out=ffn_w_out, final_norm_w=final_norm_w)
    mom = dict(norm1_w=m_norm1_w, norm2_w=m_norm2_w, a_w_in=m_a_w_in, a_lb_logits=m_a_lb_logits, a_norm_w=m_a_norm_w,
               a_w_out=m_a_w_out, b_w_in=m_b_w_in, b_conv_w=m_b_conv_w, b_conv_b=m_b_conv_b, b_dt_bias=m_b_dt_bias,
               b_a_log=m_b_a_log, b_d_skip=m_b_d_skip, b_norm_w=m_b_norm_w, b_w_out=m_b_w_out, ffn_w_in=m_ffn_w_in,
               ffn_conv_w=m_ffn_conv_w, ffn_conv_b=m_ffn_conv_b, ffn_w_out=m_ffn_w_out, final_norm_w=m_final_norm_w)
    var = dict(norm1_w=v_norm1_w, norm2_w=v_norm2_w, a_w_in=v_a_w_in, a_lb_logits=v_a_lb_logits, a_norm_w=v_a_norm_w,
               a_w_out=v_a_w_out, b_w_in=v_b_w_in, b_conv_w=v_b_conv_w, b_conv_b=v_b_conv_b, b_dt_bias=v_b_dt_bias,
               b_a_log=v_b_a_log, b_d_skip=v_b_d_skip, b_norm_w=v_b_norm_w, b_w_out=v_b_w_out, ffn_w_in=v_ffn_w_in,
               ffn_conv_w=v_ffn_conv_w, ffn_conv_b=v_ffn_conv_b, ffn_w_out=v_ffn_w_out, final_norm_w=v_final_norm_w)
    chip = 2 * lax.axis_index("x") + lax.axis_index("y")
    place = jnp.stack([chip, lax.axis_index("c")]).astype(jnp.int32)

    sh = {"a_in": a_w_in[0], "a_out": a_w_out[0], "b_in": b_w_in[0], "b_out": b_w_out[0], "f0_in": ffn_w_in[0],
          "f1_in": ffn_w_in[1], "f0_out": ffn_w_out[0], "f1_out": ffn_w_out[1]}
    sh = {u: a.astype(BF) for u, a in sh.items()}
    small_shapes = [w[n].shape for n in _SMALL_SPLIT]
    spack, small_offs = _pack([w[n] for n in _SMALL_SPLIT], 128, 8, F32)
    sall = _allgather8("s_gather", spack)
    sshards = [_unpack(sall[2 * j], small_shapes, small_offs) for j in range(4)]
    sfull = {n: jnp.concatenate([sshards[j][i] for j in range(4)], axis=-1) for i, n in enumerate(_SMALL_SPLIT)}

    p = dict(
        norm1_w=norm1_w, norm2_w=norm2_w, a_lb_logits=a_lb_logits, a_norm_w=a_norm_w[0], final_norm_w=final_norm_w,
        b_conv_w=sfull["b_conv_w"][0], b_conv_b=sfull["b_conv_b"][0], b_norm_w=sfull["b_norm_w"],
        ffn_conv_w=sfull["ffn_conv_w"], ffn_conv_b=ffn_conv_b,
        b_dt_bias=jnp.pad(b_dt_bias.reshape(1, 2 * SSM_HEADS), ((0, 0), (0, 128 - 2 * SSM_HEADS))),
        b_a_log=jnp.pad(b_a_log.reshape(1, 2 * SSM_HEADS), ((0, 0), (0, 128 - 2 * SSM_HEADS))),
        b_d_skip=jnp.repeat(b_d_skip[0], SSM_HD)[None],
    )

    loss_row, dx, g, gs_ = _sequence_grads(x[0], loss_target[0], p, sh, place)
    grads = {"a_w_in": gs_["a_in"][None], "a_w_out": gs_["a_out"][None], "b_w_in": gs_["b_in"].T[None],
             "b_w_out": gs_["b_out"][None], "ffn_w_in": jnp.stack([gs_["f0_in"], gs_["f1_in"]]),
             "ffn_w_out": jnp.stack([gs_["f0_out"], gs_["f1_out"]])}

    gsmall = {
        "norm1_w": jnp.concatenate(g["norm1_w"], axis=0), "norm2_w": jnp.concatenate(g["norm2_w"], axis=0),
        "a_lb_logits": jnp.stack(g["a_lb_logits"]), "a_norm_w": g["a_norm_w"], "b_conv_w": g["b_conv_w"],
        "b_conv_b": g["b_conv_b"], "b_dt_bias": g["b_dt_bias"], "b_a_log": g["b_a_log"], "b_d_skip": g["b_d_skip"],
        "b_norm_w": g["b_norm_w"], "ffn_conv_w": jnp.stack(g["ffn_conv_w"]),
        "ffn_conv_b": jnp.concatenate(g["ffn_conv_b"], axis=0), "final_norm_w": g["final_norm_w"],
    }
    pieces = [gsmall[n] for n in _SMALL] + [loss_row]
    piece_shapes = [a.shape for a in pieces]
    gspack, gs_offs = _pack(pieces, 128, 8, F32)
    rows = gspack.shape[0]
    gsall = _allgather8("gs_gather", gspack)

    def sum8(a):
        r = a[0]
        for i in range(1, 8):
            r = r + a[i]
        return r

    gssum = _vcall("gs_sum", sum8, (1,), [(gsall, (8, rows, 128), lambda i: (0, 0, 0))],
                   [((rows, 128), F32, (rows, 128), lambda i: (0, 0))])
    gs = dict(zip(_SMALL + ("loss",), _unpack(gssum, piece_shapes, gs_offs)))
    loss = gs["loss"][0, 0]
    lb2 = gs["a_lb_logits"]
    small_grads = {
        "norm1_w": gs["norm1_w"], "norm2_w": gs["norm2_w"], "a_lb_logits": lb2[0] + lb2[1], "a_norm_w": gs["a_norm_w"],
        "b_dt_bias": gs["b_dt_bias"][:, :2 * SSM_HEADS].reshape(1, 2, SSM_HEADS),
        "b_a_log": gs["b_a_log"][:, :2 * SSM_HEADS].reshape(1, 2, SSM_HEADS),
        "b_d_skip": gs["b_d_skip"].reshape(1, SSM_HEADS, SSM_HD).sum(axis=-1),
        "ffn_conv_b": gs["ffn_conv_b"], "final_norm_w": gs["final_norm_w"][0],
        "b_conv_w": gs["b_conv_w"][None], "b_conv_b": gs["b_conv_b"], "b_norm_w": gs["b_norm_w"], "ffn_conv_w": gs["ffn_conv_w"],
    }
    for n in _SMALL_SPLIT:
        width = w[n].shape[-1]
        small_grads[n] = lax.dynamic_slice_in_dim(small_grads[n], chip * width, width, axis=small_grads[n].ndim - 1)
    grads.update(small_grads)

    delta, new_m, new_v = {}, {}, {}
    for n in _BIG:
        shp = w[n].shape
        two_d = (shp[0] * shp[1], shp[2])
        d_, m_, v_ = _adam(f"adam_{n}", w[n].reshape(two_d), grads[n].reshape(two_d), mom[n].reshape(two_d), var[n].reshape(two_d))
        delta[n], new_m[n], new_v[n] = d_.reshape(shp), m_.reshape(shp), v_.reshape(shp)
    s_shapes = [w[n].shape for n in _SMALL]
    packs = [_pack([src[n] for n in _SMALL], 128, 8, F32) for src in (w, grads, mom, var)]
    outs = _adam("adam_small", *[pk[0] for pk in packs])
    for res, dst in zip(outs, (delta, new_m, new_v)):
        dst.update(dict(zip(_SMALL, _unpack(res, s_shapes, packs[0][1]))))

    return (loss, dx[None], *[grads[n] for n in _ORDER], *[delta[n] for n in _ORDER],
            *[new_m[n] for n in _ORDER], *[new_v[n] for n in _ORDER])
```

```python
import functools

import jax
import jax.numpy as jnp
from jax import lax
from jax.experimental import pallas as pl
from jax.experimental.pallas import tpu as pltpu

F32, BF = jnp.float32, jnp.bfloat16

D = 1024
EPS = 1e-6
HG_HEADS, HG_HD, HG_CHUNK, HG_SUB = 8, 128, 64, 8
HG_HB = 8
SSM_GB = 4
D_INNER, SSM_HEADS, SSM_HD, SSM_GROUPS, SSM_HPG, SSM_N, SSD_CHUNK = 2048, 32, 64, 8, 4, 128, 128
CONV_DIM = D_INNER + 2 * SSM_GROUPS * SSM_N
B_PROJ = 2 * D_INNER + 2 * SSM_GROUPS * SSM_N + 2 * SSM_HEADS
B_PROJ_PAD = 6272
D_FF = 2816
NEG = -1e30
ROWS = 512
GATE_ROWS = 1024
VMEM_LIMIT = 56 * 1024 * 1024

ADAM_LR, ADAM_B1, ADAM_B2, ADAM_EPS, ADAM_WD, ADAM_STEP = 0.001, 0.9, 0.999, 1e-08, 0.01, 10

MESH = pl.DeviceIdType.MESH


def _pick(n, cands):
    for c in cands:
        if n % c == 0:
            return c
    return n


class _Side:
    def __init__(self, ins, outs, plan, n_remote, n_local, alias=None):
        self.ins, self.outs, self.plan, self.n_remote, self.n_local = list(ins), list(outs), plan, n_remote, n_local
        self.alias = alias or {}

    def copies(self, in_refs, out_refs, send_sems, recv_sems, local_sems):
        remote, local = self.plan(in_refs, out_refs)
        cps = [pltpu.make_async_copy(s, d, local_sems.at[i]) for i, (s, d) in enumerate(local)]
        cps += [pltpu.make_async_remote_copy(src_ref=s, dst_ref=d, send_sem=send_sems.at[i], recv_sem=recv_sems.at[i],
                                             device_id=dev, device_id_type=MESH)
                for i, (s, d, dev) in enumerate(remote)]
        return cps

    def sems(self):
        return [pltpu.SemaphoreType.DMA((self.n_remote,)), pltpu.SemaphoreType.DMA((self.n_remote,)),
                pltpu.SemaphoreType.DMA((max(self.n_local, 1),))]


def _vcall(name, fn, grid, ins, outs, acc=None, scratch=(), side=None):
    acc = acc or {}
    n_in, n_out, nd = len(ins), len(outs), len(grid)
    n_sin = len(side.ins) if side else 0
    n_sout = len(side.outs) if side else 0
    n_scr = len(scratch)

    def body(*refs):
        in_refs, refs = refs[:n_in], refs[n_in:]
        sin_refs, refs = refs[:n_sin], refs[n_sin:]
        out_refs, refs = refs[:n_out], refs[n_out:]
        sout_refs, refs = refs[:n_sout], refs[n_sout:]
        scr, sems = refs[:n_scr], refs[n_scr:]
        if side:
            at_first, at_last = None, None
            for ax in range(nd):
                f, l = pl.program_id(ax) == 0, pl.program_id(ax) == grid[ax] - 1
                at_first = f if at_first is None else jnp.logical_and(at_first, f)
                at_last = l if at_last is None else jnp.logical_and(at_last, l)

            @pl.when(at_first)
            def _():
                for cp in side.copies(sin_refs, sout_refs, *sems):
                    cp.start()

        res = fn(*[r[...] for r in in_refs], *scr)
        if not isinstance(res, (tuple, list)):
            res = (res,)
        if side:
            @pl.when(at_last)
            def _():
                for cp in side.copies(sin_refs, sout_refs, *sems):
                    cp.wait()
        for j, (o_ref, r) in enumerate(zip(out_refs, res)):
            mode = acc.get(j)
            if mode is None:
                o_ref[...] = r.astype(o_ref.dtype)
                continue
            first = pl.program_id(nd - 1) == 0
            if mode == "all":
                for ax in range(nd - 1):
                    first = jnp.logical_and(first, pl.program_id(ax) == 0)

            @pl.when(first)
            def _():
                o_ref[...] = r.astype(o_ref.dtype)

            @pl.when(jnp.logical_not(first))
            def _():
                o_ref[...] += r.astype(o_ref.dtype)

    hbm = pl.BlockSpec(memory_space=pl.ANY)
    in_specs = [pl.BlockSpec(bs, im) for _, bs, im in ins] + [hbm] * n_sin
    out_specs = [pl.BlockSpec(bs, im) for _, _, bs, im in outs] + [hbm] * n_sout
    params = pltpu.CompilerParams(dimension_semantics=("arbitrary",) * nd, vmem_limit_bytes=VMEM_LIMIT)
    out_shape = [jax.ShapeDtypeStruct(s, dt) for s, dt, _, _ in outs]
    operands = [a for a, _, _ in ins]
    scratch = list(scratch)
    aliases = {}
    if side:
        out_shape += [jax.ShapeDtypeStruct(s, dt) for s, dt in side.outs]
        operands += side.ins
        scratch += side.sems()
        aliases = {n_in + i: n_out + o for i, o in side.alias.items()}
    out = pl.pallas_call(body, name=name, grid=grid, in_specs=in_specs, out_specs=out_specs, out_shape=out_shape,
                         scratch_shapes=scratch, compiler_params=params, input_output_aliases=aliases)(*operands)
    if side:
        return tuple(out[:n_out]), tuple(out[n_out:])
    return out[0] if n_out == 1 else out


def _mm(name, a, b, kind, out_dtype=F32, add=None, side=None):
    if kind == "tn":
        m, k = a.shape
        _, n = b.shape
        tm = _pick(m, (1024, 512, 256))
        tk = _pick(k, (1024, 1408, 896, 512, 256, 128))
        tn = _pick(n, (1024, 1408, 896, 512, 256, 128))

        def fn(av, bv):
            return lax.dot_general(av.astype(BF), bv.astype(BF), (((0,), (0,)), ((), ())),
                                   preferred_element_type=F32)

        return _vcall(name, fn, (k // tk, n // tn, m // tm),
                      [(a, (tm, tk), lambda i, j, s: (s, i)), (b, (tm, tn), lambda i, j, s: (s, j))],
                      [((k, n), F32, (tk, tn), lambda i, j, s: (i, j))], acc={0: "last"}, side=side)
    m, k = a.shape
    n = b.shape[1] if kind == "nn" else b.shape[0]
    long_k = k > 4096
    tm = _pick(m, (1024, 512, 256))
    tn = _pick(n, (512, 896, 256, 128)) if long_k else _pick(n, (1024, 1408, 896, 512, 256, 128))
    dims = (((1,), (0,)), ((), ())) if kind == "nn" else (((1,), (1,)), ((), ()))

    def fn(av, bv, *rest):
        r = lax.dot_general(av.astype(BF), bv.astype(BF), dims, preferred_element_type=F32)
        return r + rest[0] if rest else r

    ins = [(a, (tm, k), lambda i, j: (i, 0)),
           (b, (k, tn), lambda i, j: (0, j)) if kind == "nn" else (b, (tn, k), lambda i, j: (j, 0))]
    if add is not None:
        ins.append((add, (tm, tn), lambda i, j: (i, j)))
    return _vcall(name, fn, (m // tm, n // tn), ins, [((m, n), out_dtype, (tm, tn), lambda i, j: (i, j))], side=side)


def _rms(h, w):
    return h * lax.rsqrt(jnp.mean(h * h, axis=-1, keepdims=True) + EPS) * w


def _rms_fwd(name, h, w):
    L = h.shape[0]
    tb = _pick(L, (ROWS,))
    return _vcall(name, _rms, (L // tb,),
                  [(h, (tb, D), lambda i: (i, 0)), (w.reshape(1, D), (1, D), lambda i: (0, 0))],
                  [((L, D), BF, (tb, D), lambda i: (i, 0))])


def _rms_bwd(name, du, h, w, dh_next, side=None):
    L = h.shape[0]
    tb = _pick(L, (ROWS,))

    def fn(duv, hv, wv, dnv):
        _, vjp = jax.vjp(_rms, hv, wv)
        dh, dw = vjp(duv)
        return dh + dnv, dw

    row = lambda i: (i, 0)
    return _vcall(name, fn, (L // tb,),
                  [(du, (tb, D), row), (h, (tb, D), row), (w.reshape(1, D), (1, D), lambda i: (0, 0)),
                   (dh_next, (tb, D), row)],
                  [((L, D), F32, (tb, D), row), ((1, D), F32, (1, D), lambda i: (0, 0))], acc={1: "all"}, side=side)


def _loss_head(name, h, tgt, w):
    L = h.shape[0]
    tb = _pick(L, (ROWS,))

    def lossf(hv, wv, tv):
        err = _rms(hv, wv) - tv
        return 0.5 * jnp.sum(err * err) * (1.0 / D)

    def fn(hv, wv, tv):
        val, vjp = jax.vjp(lambda a, b: lossf(a, b, tv), hv, wv)
        dh, dw = vjp(jnp.ones((), F32))
        return jnp.full((1, 128), val, F32), dh, dw

    row = lambda i: (i, 0)
    zero = lambda i: (0, 0)
    return _vcall(name, fn, (L // tb,),
                  [(h, (tb, D), row), (w.reshape(1, D), (1, D), zero), (tgt, (tb, D), row)],
                  [((1, 128), F32, (1, 128), zero), ((L, D), F32, (tb, D), row), ((1, D), F32, (1, D), zero)],
                  acc={0: "all", 2: "all"})


def _bf(x):
    return x.astype(BF)


def _dot(a, b, dims):
    return lax.dot_general(a, b, (dims, ((), ())), preferred_element_type=F32)


def _tri(n, reverse):
    r = lax.broadcasted_iota(jnp.int32, (n, n), 0)
    c = lax.broadcasted_iota(jnp.int32, (n, n), 1)
    return (r <= c) if reverse else (r >= c)


def _tri_matmul(n, reverse, x):
    hi = x.astype(BF)
    r1 = x - hi.astype(F32)
    mid = r1.astype(BF)
    lo = (r1 - mid.astype(F32)).astype(BF)
    y = _dot(_tri(n, reverse).astype(BF), jnp.concatenate([hi, mid, lo], axis=1), ((1,), (0,)))
    w = x.shape[1]
    return (y[:, :w] + y[:, w:2 * w]) + y[:, 2 * w:]


@functools.partial(jax.custom_vjp, nondiff_argnums=(0, 1))
def _running_sum(n, reverse, x):
    return _tri_matmul(n, reverse, x)


def _running_sum_fwd(n, reverse, x):
    return _tri_matmul(n, reverse, x), None


def _running_sum_bwd(n, reverse, _, ct):
    return (_tri_matmul(n, not reverse, ct),)


_running_sum.defvjp(_running_sum_fwd, _running_sum_bwd)


def _gla_chunk(q_raw, f_raw, v, lb3, S, reverse):
    C, SB, HD = HG_CHUNK, HG_SUB, HG_HD
    H = S.shape[0]
    heads = [slice(HD * h, HD * (h + 1)) for h in range(H)]
    row3 = lax.broadcasted_iota(jnp.int32, (3, 1), 0)
    e = jnp.exp(lb3 - jnp.max(lb3, axis=0, keepdims=True))
    lb = jnp.sum(jnp.where(row3 == 0, e, 0.0), axis=0, keepdims=True) / jnp.sum(e, axis=0, keepdims=True)
    q = q_raw * jax.nn.sigmoid(q_raw)
    f = lb + (1.0 - lb) * jax.nn.sigmoid(f_raw)
    g = jnp.log(f)
    k = 1.0 - f
    b = _running_sum(C, reverse, g)
    row = lax.broadcasted_iota(jnp.int32, (C, 1), 0)
    vb = _bf(v)

    def rowof(x, t):
        return jnp.sum(jnp.where(row == t, x, 0.0), axis=0, keepdims=True)

    qe = _bf(q * jnp.exp(b))
    o = [_dot(qe[:, hs], _bf(S[h]), ((1,), (0,))) for h, hs in enumerate(heads)]
    att = [[] for _ in range(H)]
    for i in range(C // SB):
        lo = SB * i
        if (not reverse and i == 0) or (reverse and i == C // SB - 1):
            for h in range(H):
                att[h].append(jnp.zeros((SB, C), F32))
            continue
        first = lo + SB - 1 if reverse else lo
        r = rowof(b, first) - rowof(g, first)
        before = (row >= lo + SB) if reverse else (row < lo)
        qi = q[lo:lo + SB] * jnp.exp(b[lo:lo + SB] - r)
        kk = _bf(k * jnp.exp(jnp.where(before, r - b, NEG)))
        for h, hs in enumerate(heads):
            att[h].append(_dot(_bf(qi[:, hs]), kk[:, hs], ((1,), (1,))))
    o = [o[h] + _dot(_bf(jnp.concatenate(att[h], axis=0)), vb[:, hs], ((1,), (0,))) for h, hs in enumerate(heads)]
    s_i = lax.broadcasted_iota(jnp.int32, (SB, SB, HD), 0)
    t_i = lax.broadcasted_iota(jnp.int32, (SB, SB, HD), 1)
    pair = (t_i <= s_i) if reverse else (t_i >= s_i)
    shp = (SB, SB, HD)
    diag = [[] for _ in range(H)]
    for i in range(C // SB):
        rows = slice(SB * i, SB * (i + 1))
        for h, hs in enumerate(heads):
            qb, kb, bb = q[rows, hs], k[rows, hs], b[rows, hs]
            dif = lax.broadcast_in_dim(bb, shp, (1, 2)) - lax.broadcast_in_dim(bb, shp, (0, 2))
            w = lax.broadcast_in_dim(qb, shp, (1, 2)) * jnp.exp(jnp.where(pair, dif, NEG)) * lax.broadcast_in_dim(kb, shp, (0, 2))
            d = jnp.sum(w, axis=2, keepdims=True)
            diag[h].append(jnp.sum(d * lax.broadcast_in_dim(v[rows, hs], shp, (0, 2)), axis=0))
    o = jnp.concatenate([o[h] + jnp.concatenate(diag[h], axis=0) for h in range(H)], axis=1)
    btot = rowof(b, 0 if reverse else C - 1)
    kd = _bf(k * jnp.exp(btot - b))
    eye = lax.broadcasted_iota(jnp.int32, (HD, HD), 0) == lax.broadcasted_iota(jnp.int32, (HD, HD), 1)
    s_new = []
    for h, hs in enumerate(heads):
        btot_col = jnp.sum(jnp.where(eye, btot[:, hs], 0.0), axis=1, keepdims=True)
        s_new.append((jnp.exp(btot_col) * S[h] + _dot(kd[:, hs], vb[:, hs], ((0,), (0,))))[None])
    return o, jnp.concatenate(s_new, axis=0)


def _gla_fwd(name, pa, lbl, reverse, side=None):
    L = pa.shape[0]
    C = HG_CHUNK
    nc = L // C
    cidx = (lambda i: nc - 1 - i) if reverse else (lambda i: i)
    sec = 2 if reverse else 1
    hb_n, nh = HG_HB, HG_HEADS // HG_HB

    def fn(qr, fr, v, lb3, s_ref):
        @pl.when(pl.program_id(1) == 0)
        def _():
            s_ref[...] = jnp.zeros_like(s_ref)

        s_all = s_ref[...]
        o, s_new = _gla_chunk(qr, fr, v, lb3, s_all, reverse)
        s_ref[...] = s_new
        return o, s_all[None]

    blk = (C, HG_HD * hb_n)
    return _vcall(name, fn, (nh, nc),
                  [(pa, blk, lambda h, i: (cidx(i), h)), (pa, blk, lambda h, i: (cidx(i), sec * nh + h)),
                   (pa, blk, lambda h, i: (cidx(i), 3 * nh + h)), (lbl, (3, HG_HD * hb_n), lambda h, i: (0, h))],
                  [((L, D), F32, blk, lambda h, i: (cidx(i), h)),
                   ((nc, HG_HEADS, HG_HD, HG_HD), F32, (1, hb_n, HG_HD, HG_HD), lambda h, i: (cidx(i), h, 0, 0))],
                  scratch=[pltpu.VMEM((hb_n, HG_HD, HG_HD), F32)], side=side)


def _gla_bwd(name, pa, lbl, s_in, do, reverse, prev=None, side=None):
    L = pa.shape[0]
    C = HG_CHUNK
    nc = L // C
    cidx = (lambda i: i) if reverse else (lambda i: nc - 1 - i)
    sec = 2 if reverse else 1
    n_prev = 0 if prev is None else 2
    hb_n, nh = HG_HB, HG_HEADS // HG_HB

    def fn(qr, fr, v, lb3, s, dov, *rest):
        ds_ref = rest[n_prev]

        @pl.when(pl.program_id(1) == 0)
        def _():
            ds_ref[...] = jnp.zeros_like(ds_ref)

        _, vjp = jax.vjp(lambda *a: _gla_chunk(*a, reverse), qr, fr, v, lb3, s[0])
        dq, df, dv, dlb, ds = vjp((dov, ds_ref[...]))
        ds_ref[...] = ds
        if n_prev:
            dq, dv = dq + rest[0], dv + rest[1]
        return dq, df, dv, dlb

    blk = (C, HG_HD * hb_n)
    at = lambda h, i: (cidx(i), h)
    ins = [(pa, blk, at), (pa, blk, lambda h, i: (cidx(i), sec * nh + h)), (pa, blk, lambda h, i: (cidx(i), 3 * nh + h)),
           (lbl, (3, HG_HD * hb_n), lambda h, i: (0, h)),
           (s_in, (1, hb_n, HG_HD, HG_HD), lambda h, i: (cidx(i), h, 0, 0)), (do, blk, at)]
    if prev is not None:
        ins += [(prev[0], blk, at), (prev[1], blk, at)]
    sum_dt = F32 if prev is None else BF
    return _vcall(name, fn, (nh, nc), ins,
                  [((L, D), sum_dt, blk, at), ((L, D), BF, blk, at), ((L, D), sum_dt, blk, at),
                   ((3, D), F32, (3, HG_HD * hb_n), lambda h, i: (0, h))],
                  acc={3: "last"}, scratch=[pltpu.VMEM((hb_n, HG_HD, HG_HD), F32)], side=side)


def _hgout(o_f, o_b, g, nw):
    o = o_f + o_b
    return _rms(o, nw) * (g * jax.nn.sigmoid(g))


def _hgout_fwd(name, o_f, o_b, pa, nw, side=None):
    L = o_f.shape[0]
    tb = _pick(L, (GATE_ROWS, ROWS))
    blk = (tb, HG_HD)
    at = lambda h, i: (i, h)
    return _vcall(name, _hgout, (HG_HEADS, L // tb),
                  [(o_f, blk, at), (o_b, blk, at), (pa, blk, lambda h, i: (i, 32 + h)),
                   (nw.reshape(1, HG_HD), (1, HG_HD), lambda h, i: (0, 0))],
                  [((L, D), BF, blk, at)], side=side)


def _hgout_bwd(name, o_f, o_b, pa, nw, dy, side=None):
    L = o_f.shape[0]
    tb = _pick(L, (GATE_ROWS, ROWS))

    def fn(ofv, obv, gv, nwv, dyv):
        _, vjp = jax.vjp(_hgout, ofv, obv, gv, nwv)
        do, _, dg, dnw = vjp(dyv)
        return do, dg, dnw

    blk = (tb, HG_HD)
    at = lambda h, i: (i, h)
    zero = lambda h, i: (0, 0)
    return _vcall(name, fn, (HG_HEADS, L // tb),
                  [(o_f, blk, at), (o_b, blk, at), (pa, blk, lambda h, i: (i, 32 + h)),
                   (nw.reshape(1, HG_HD), (1, HG_HD), zero), (dy, blk, at)],
                  [((L, D), F32, blk, at), ((L, D), BF, blk, at), ((1, HG_HD), F32, (1, HG_HD), zero)],
                  acc={2: "all"}, side=side)


def _shift(x, s):
    if s == 0:
        return x
    n = x.shape[0]
    t = lax.broadcasted_iota(jnp.int32, (n, 1), 0)
    if s > 0:
        return jnp.where(t >= s, pltpu.roll(x, s, 0), 0.0)
    return jnp.where(t < n + s, pltpu.roll(x, n + s, 0), 0.0)


def _conv(x, w, b):
    kk = w.shape[0]
    p = (kk - 1) // 2
    y = b
    for j in range(kk):
        y = y + w[j:j + 1] * _shift(x, p - j)
    return y


def _conv_bwd(x, w, dc):
    kk = w.shape[0]
    p = (kk - 1) // 2
    dx = None
    dws = []
    for j in range(kk):
        t = w[j:j + 1] * _shift(dc, j - p)
        dx = t if dx is None else dx + t
        dws.append(jnp.sum(dc * _shift(x, p - j), axis=0, keepdims=True))
    rows = lax.broadcasted_iota(jnp.int32, (kk, 1), 0)
    dw = None
    for j in range(kk):
        t = jnp.where(rows == j, dws[j], 0.0)
        dw = t if dw is None else dw + t
    return dx, dw, jnp.sum(dc, axis=0, keepdims=True)


def _silu(c):
    return c * jax.nn.sigmoid(c)


def _silu_grad(c):
    s = jax.nn.sigmoid(c)
    return s * (1.0 + c * (1.0 - s))


def _glu_fwd(name, pf, cw, cb):
    L = pf.shape[0]
    tc = 128
    nt = D_FF // tc
    return _vcall(name, lambda gate, val, w, b: _silu(_conv(gate, w, b)) * val, (nt,),
                  [(pf, (L, tc), lambda j: (0, j)), (pf, (L, tc), lambda j: (0, nt + j)),
                   (cw, (3, tc), lambda j: (0, j)), (cb.reshape(1, D_FF), (1, tc), lambda j: (0, j))],
                  [((L, D_FF), BF, (L, tc), lambda j: (0, j))])


def _glu_bwd(name, pf, cw, cb, dy, side=None):
    L = pf.shape[0]
    tc = 128
    nt = D_FF // tc

    def fn(gate, val, w, b, dyv):
        c = _conv(gate, w, b)
        dgate, dw, db = _conv_bwd(gate, w, dyv * val * _silu_grad(c))
        return dgate, dyv * _silu(c), dw, db

    col = lambda j: (0, j)
    return _vcall(name, fn, (nt,),
                  [(pf, (L, tc), col), (pf, (L, tc), lambda j: (0, nt + j)), (cw, (3, tc), col),
                   (cb.reshape(1, D_FF), (1, tc), col), (dy, (L, tc), col)],
                  [((L, D_FF), BF, (L, tc), col), ((L, D_FF), BF, (L, tc), col),
                   ((3, D_FF), F32, (3, tc), col), ((1, D_FF), F32, (1, tc), col)], side=side)


def _perm_tile(j):
    return jnp.where(j < 16, 4 * (j // 2) + j % 2, jnp.where(j < 24, 4 * (j - 16) + 2, 4 * (j - 24) + 3))


def _mpre_fwd(name, pb, cw, cb):
    L = pb.shape[0]
    tc = 128
    return _vcall(name, lambda x, w, b: _silu(_conv(x, w, b)), (CONV_DIM // tc,),
                  [(pb, (L, tc), lambda j: (0, 16 + j)), (cw, (5, tc), lambda j: (0, j)),
                   (cb.reshape(1, CONV_DIM), (1, tc), lambda j: (0, j))],
                  [((L, CONV_DIM), F32, (L, tc), lambda j: (0, _perm_tile(j)))])


def _mpre_bwd(name, pb, cw, cb, dact):
    L = pb.shape[0]
    tc = 128
    col = lambda j: (0, j)
    return _vcall(name, lambda x, w, b, da: _conv_bwd(x, w, da * _silu_grad(_conv(x, w, b))), (CONV_DIM // tc,),
                  [(pb, (L, tc), lambda j: (0, 16 + j)), (cw, (5, tc), col), (cb.reshape(1, CONV_DIM), (1, tc), col),
                   (dact, (L, tc), lambda j: (0, _perm_tile(j)))],
                  [((L, CONV_DIM), BF, (L, tc), col), ((5, CONV_DIM), F32, (5, tc), col),
                   ((1, CONV_DIM), F32, (1, tc), col)])


def _softplus(x):
    return jnp.maximum(x, 0.0) + jnp.log(1.0 + jnp.exp(-jnp.abs(x)))


def _dt_fwd(name, pb, dtb, alog):
    L = pb.shape[0]
    tb = _pick(L, (1024, ROWS))

    def fn(x, bias, al):
        dt = _softplus(x + bias)
        return dt, dt * (-jnp.exp(al))

    row = lambda i: (i, 0)
    zero = lambda i: (0, 0)
    return _vcall(name, fn, (L // tb,),
                  [(pb, (tb, 128), lambda i: (i, 48)), (dtb, (1, 128), zero), (alog, (1, 128), zero)],
                  [((L, 128), F32, (tb, 128), row), ((L, 128), F32, (tb, 128), row)])


def _dt_bwd(name, pb, dtb, alog, ddt_f, dla_f, ddt_b, dla_b):
    L = pb.shape[0]
    tb = _pick(L, (1024, ROWS))

    def fn(x, bias, al, a1, b1, a2, b2):
        ddt = jnp.sum(a1, axis=0) + jnp.sum(a2, axis=0)
        dla = jnp.sum(b1, axis=0) + jnp.sum(b2, axis=0)
        z = x + bias
        dt = _softplus(z)
        a = -jnp.exp(al)
        dz = (ddt + dla * a) * jax.nn.sigmoid(z)
        return dz, jnp.sum(dz, axis=0, keepdims=True), jnp.sum(dla * dt, axis=0, keepdims=True) * a

    zero = lambda i: (0, 0)
    g3 = (ddt_f.shape[0], tb, 128)
    at3 = lambda i: (0, i, 0)
    return _vcall(name, fn, (L // tb,),
                  [(pb, (tb, 128), lambda i: (i, 48)), (dtb, (1, 128), zero), (alog, (1, 128), zero),
                   (ddt_f, g3, at3), (dla_f, g3, at3), (ddt_b, g3, at3), (dla_b, g3, at3)],
                  [((L, 128), BF, (tb, 128), lambda i: (i, 0)), ((1, 128), F32, (1, 128), zero),
                   ((1, 128), F32, (1, 128), zero)], acc={1: "all", 2: "all"})


def _split_dot(x, e, dims, pieces):
    hi = x.astype(BF)
    r1 = x - hi.astype(F32)
    mid = r1.astype(BF)
    y = _dot(hi, e, dims) + _dot(mid, e, dims)
    if pieces == 3:
        y = y + _dot((r1 - mid.astype(F32)).astype(BF), e, dims)
    return y


@functools.partial(jax.custom_vjp, nondiff_argnums=(2,))
def _spread(x, e, pieces):
    return _split_dot(x, e, ((1,), (0,)), pieces)


def _spread_fwd(x, e, pieces):
    return _split_dot(x, e, ((1,), (0,)), pieces), e


def _spread_bwd(pieces, e, ct):
    return _split_dot(ct, e, ((1,), (1,)), pieces), jnp.zeros_like(e)


_spread.defvjp(_spread_fwd, _spread_bwd)


def _ssd_chunk(xa, dt, la, hs, head0, reverse):
    C, P4, HD, N = SSD_CHUNK, SSM_HPG * SSM_HD, SSM_HD, SSM_N
    G = hs.shape[0]
    nh = SSM_HPG * G
    row = lax.broadcasted_iota(jnp.int32, (C, 1), 0)
    lane = lax.broadcasted_iota(jnp.int32, (1, 128), 1)
    eye = lax.broadcasted_iota(jnp.int32, (C, C), 0) == lax.broadcasted_iota(jnp.int32, (C, C), 1)
    tri = _tri(C, reverse)
    last = 0 if reverse else C - 1
    acum = _running_sum(C, reverse, la)
    atot = jnp.sum(jnp.where(row == last, acum, 0.0), axis=0, keepdims=True)
    src = lax.broadcasted_iota(jnp.int32, (128, 1), 0) - head0
    to_x = (src == lax.broadcasted_iota(jnp.int32, (1, nh * HD), 1) // HD).astype(BF)
    dt_x = _spread(dt, to_x, 2)
    ea_x = _spread(jnp.exp(acum), to_x, 2)
    dec_x = _spread(jnp.exp(atot - acum), to_x, 2)
    col_head = lax.broadcasted_iota(jnp.int32, (1, P4), 1) // HD
    row_head = lax.broadcasted_iota(jnp.int32, (P4, 1), 0) // HD
    ys, news = [], []
    for gi in range(G):
        xs = xa[:, 512 * gi:512 * gi + P4]
        bm = _bf(xa[:, 512 * gi + P4:512 * gi + P4 + N])
        cm = _bf(xa[:, 512 * gi + P4 + N:512 * (gi + 1)])
        gx = slice(P4 * gi, P4 * (gi + 1))
        cb = _dot(cm, bm, ((1,), (1,)))
        xd = xs * dt_x[:, gx]
        ms, xds, scale = [], [], 0.0
        for j in range(SSM_HPG):
            i = SSM_HPG * gi + j
            ac = jnp.sum(jnp.where(lane == head0 + i, acum, 0.0), axis=1, keepdims=True)
            ac_row = jnp.sum(jnp.where(eye, ac, 0.0), axis=0, keepdims=True)
            ms.append(_bf(cb * jnp.exp(jnp.where(tri, ac - ac_row, NEG))))
            xds.append(_bf(jnp.where(col_head == j, xd, 0.0)))
            a_i = jnp.sum(jnp.where(lane == head0 + i, atot, 0.0), axis=1, keepdims=True)
            scale = scale + jnp.where(row_head == j, jnp.exp(a_i), 0.0)
        y = _dot(jnp.concatenate(ms, axis=1), jnp.concatenate(xds, axis=0), ((1,), (0,)))
        y = y + _dot(cm, _bf(hs[gi]), ((1,), (1,))) * ea_x[:, gx]
        ys.append(y)
        news.append((scale * hs[gi] + _dot(_bf(xd * dec_x[:, gx]), bm, ((0,), (0,))))[None])
    return jnp.concatenate(ys, axis=1), jnp.concatenate(news, axis=0)


def _ssd_fwd(name, xact, dt, la, reverse):
    L = xact.shape[0]
    C = SSD_CHUNK
    nc = L // C
    cidx = (lambda i: nc - 1 - i) if reverse else (lambda i: i)
    base = SSM_HEADS if reverse else 0
    P4 = SSM_HPG * SSM_HD

    gb_n = SSM_GB

    def fn(xa, dtv, lav, h_ref):
        @pl.when(pl.program_id(1) == 0)
        def _():
            h_ref[...] = jnp.zeros_like(h_ref)

        h_all = h_ref[...]
        y, h_new = _ssd_chunk(xa, dtv, lav, h_all, base + SSM_HPG * gb_n * pl.program_id(0), reverse)
        h_ref[...] = h_new
        return y, h_all[None]

    return _vcall(name, fn, (SSM_GROUPS // gb_n, nc),
                  [(xact, (C, 512 * gb_n), lambda g, i: (cidx(i), g)), (dt, (C, 128), lambda g, i: (cidx(i), 0)),
                   (la, (C, 128), lambda g, i: (cidx(i), 0))],
                  [((L, D_INNER), F32, (C, P4 * gb_n), lambda g, i: (cidx(i), g)),
                   ((nc, SSM_GROUPS, P4, SSM_N), F32, (1, gb_n, P4, SSM_N), lambda g, i: (cidx(i), g, 0, 0))],
                  scratch=[pltpu.VMEM((gb_n, P4, SSM_N), F32)])


def _ssd_bwd(name, xact, dt, la, h_in, dy, reverse, prev_xs=None, prev_all=None, side=None):
    L = xact.shape[0]
    C = SSD_CHUNK
    nc = L // C
    cidx = (lambda i: i) if reverse else (lambda i: nc - 1 - i)
    base = SSM_HEADS if reverse else 0
    P4 = SSM_HPG * SSM_HD

    gb_n = SSM_GB

    def fn(xa, dtv, lav, hs, dyv, pv, dh_ref):
        @pl.when(pl.program_id(1) == 0)
        def _():
            dh_ref[...] = jnp.zeros_like(dh_ref)

        head0 = base + SSM_HPG * gb_n * pl.program_id(0)
        _, vjp = jax.vjp(lambda a, b, c, d: _ssd_chunk(a, b, c, d, head0, reverse), xa, dtv, lav, hs[0])
        dxa, ddt, dla, dh = vjp((dyv, dh_ref[...]))
        dh_ref[...] = dh
        if prev_all is not None:
            dxa = dxa + pv
        else:
            zeros = jnp.zeros((C, 2 * SSM_N), F32)
            dxa = dxa + jnp.concatenate([t for gb in range(gb_n) for t in (pv[:, P4 * gb:P4 * (gb + 1)], zeros)], axis=1)
        return dxa, ddt[None], dla[None]

    at = lambda g, i: (cidx(i), g)
    at0 = lambda g, i: (cidx(i), 0)
    pv = (prev_all, (C, 512 * gb_n), at) if prev_all is not None else (prev_xs, (C, P4 * gb_n), at)
    steps = SSM_GROUPS // gb_n
    return _vcall(name, fn, (steps, nc),
                  [(xact, (C, 512 * gb_n), at), (dt, (C, 128), at0), (la, (C, 128), at0),
                   (h_in, (1, gb_n, P4, SSM_N), lambda g, i: (cidx(i), g, 0, 0)), (dy, (C, P4 * gb_n), at), pv],
                  [((L, CONV_DIM), F32, (C, 512 * gb_n), at),
                   ((steps, L, 128), F32, (1, C, 128), lambda g, i: (g, cidx(i), 0)),
                   ((steps, L, 128), F32, (1, C, 128), lambda g, i: (g, cidx(i), 0))],
                  scratch=[pltpu.VMEM((gb_n, P4, SSM_N), F32)], side=side)


def _mpost(y_f, y_b, xs, z, dsk, nw):
    y = (y_f + y_b + xs * dsk) * (z * jax.nn.sigmoid(z))
    return _rms(y, nw)


def _mpost_fwd(name, y_f, y_b, xact, pb, dsk, nw):
    L = y_f.shape[0]
    tb = _pick(L, (GATE_ROWS, ROWS))
    blk = (tb, 256)
    at = lambda g, i: (i, g)
    par = lambda g, i: (0, g)
    return _vcall(name, _mpost, (SSM_GROUPS, L // tb),
                  [(y_f, blk, at), (y_b, blk, at), (xact, blk, lambda g, i: (i, 2 * g)), (pb, blk, at),
                   (dsk, (1, 256), par), (nw, (1, 256), par)],
                  [((L, D_INNER), BF, blk, at)])


def _mpost_bwd(name, y_f, y_b, xact, pb, dsk, nw, dy):
    L = y_f.shape[0]
    tb = _pick(L, (GATE_ROWS, ROWS))

    def fn(yf, yb, xs, z, dskv, nwv, dyv):
        _, vjp = jax.vjp(_mpost, yf, yb, xs, z, dskv, nwv)
        dyf, _, dxs, dz, ddsk, dnw = vjp(dyv)
        return dyf, dxs, dz, ddsk, dnw

    blk = (tb, 256)
    at = lambda g, i: (i, g)
    par = lambda g, i: (0, g)
    return _vcall(name, fn, (SSM_GROUPS, L // tb),
                  [(y_f, blk, at), (y_b, blk, at), (xact, blk, lambda g, i: (i, 2 * g)), (pb, blk, at),
                   (dsk, (1, 256), par), (nw, (1, 256), par), (dy, blk, at)],
                  [((L, D_INNER), F32, blk, at), ((L, D_INNER), F32, blk, at), ((L, D_INNER), BF, blk, at),
                   ((1, D_INNER), F32, (1, 256), par), ((1, D_INNER), F32, (1, 256), par)],
                  acc={3: "last", 4: "last"})


def _ffn_fwd(tag, h, nw, w_in, cw, cb, w_out):
    u = _rms_fwd(f"{tag}_norm", h, nw)
    pf = _mm(f"{tag}_in", u, w_in, "nn")
    yf = _glu_fwd(f"{tag}_glu", pf, cw, cb)
    return _mm(f"{tag}_out", yf, w_out, "nn", add=h), (u, pf, yf)


def _ffn_bwd(tag, h, nw, w_in, cw, cb, w_out, saved, dh, side=None):
    u, pf, yf = saved
    d_w_out = _mm(f"{tag}_dwout", yf, dh, "tn")
    dyf = _mm(f"{tag}_dy", dh, w_out, "nt")
    dgate, dval, dcw, dcb = _glu_bwd(f"{tag}_dglu", pf, cw, cb, dyf)
    dpf = jnp.concatenate([dgate, dval], axis=1)
    d_w_in = _mm(f"{tag}_dwin", u, dpf, "tn")
    got = ()
    if side is None:
        du = _mm(f"{tag}_du", dpf, w_in, "nt")
    else:
        (du,), got = _mm(f"{tag}_du", dpf, w_in, "nt", side=side)
    dh_in, dnw = _rms_bwd(f"{tag}_dnorm", du, h, nw, dh)
    return dh_in, dnw, d_w_in, dcw, dcb, d_w_out, got


def _sequence_grads(x, tgt, p, sh):
    g = {}
    lbl = p["a_lb_logits"]
    first, early, mid, last = ("a_in", "a_out"), ("f0_in",), ("b_in", "b_out"), ("f1_in", "f1_out", "f0_out")
    W = {}
    got = _exchange("w_first", _gather_side(first, (), sh, W))
    W.update(zip(first, got))
    got = _exchange("w_first_pass", _gather_side((), first, sh, W))
    W.update(zip(first, got))
    u1 = _rms_fwd("a_norm", x, p["norm1_w"][0])
    (pa,), got = _mm("a_in", u1, W["a_in"], "nn", side=_gather_side(early, (), sh, W))
    W.update(zip(early, got))
    (o_f, s_f), got = _gla_fwd("a_scan_f", pa, lbl, False, side=_gather_side(mid, early, sh, W))
    W.update(zip(mid + early, got))
    (o_b, s_b), got = _gla_fwd("a_scan_b", pa, lbl, True, side=_gather_side(last, mid, sh, W))
    W.update(zip(last + mid, got))
    (ya,), got = _hgout_fwd("a_gate", o_f, o_b, pa, p["a_norm_w"], side=_gather_side((), last, sh, W))
    W.update(zip(last, got))
    wb4 = W["b_in"].reshape(4, D, B_PROJ // 4)
    p = dict(p, a_w_in=W["a_in"], a_w_out=W["a_out"], b_w_out=W["b_out"], ffn_w_in=(W["f0_in"], W["f1_in"]),
             ffn_w_out=(W["f0_out"], W["f1_out"]),
             b_w_in=jnp.pad(jnp.concatenate([wb4[j] for j in range(4)], axis=1), ((0, 0), (0, B_PROJ_PAD - B_PROJ))))
    h1 = _mm("a_out", ya, p["a_w_out"], "nn", add=x)
    h2, ffn0 = _ffn_fwd("f0", h1, p["norm2_w"][0], p["ffn_w_in"][0], p["ffn_conv_w"][0], p["ffn_conv_b"][0], p["ffn_w_out"][0])
    u3 = _rms_fwd("b_norm", h2, p["norm1_w"][1])
    pb = _mm("b_in", u3, p["b_w_in"], "nn")
    xact = _mpre_fwd("b_conv", pb, p["b_conv_w"], p["b_conv_b"])
    dt, la = _dt_fwd("b_dt", pb, p["b_dt_bias"], p["b_a_log"])
    y_f, hs_f = _ssd_fwd("b_scan_f", xact, dt, la, False)
    y_b, hs_b = _ssd_fwd("b_scan_b", xact, dt, la, True)
    yb = _mpost_fwd("b_gate", y_f, y_b, xact, pb, p["b_d_skip"], p["b_norm_w"])
    h3 = _mm("b_out", yb, p["b_w_out"], "nn", add=h2)
    h4, ffn1 = _ffn_fwd("f1", h3, p["norm2_w"][1], p["ffn_w_in"][1], p["ffn_conv_w"][1], p["ffn_conv_b"][1], p["ffn_w_out"][1])
    loss, dh4, g["final_norm_w"] = _loss_head("head", h4, tgt, p["final_norm_w"])
    dh3, dn2_1, dwin1, dcw1, dcb1, dwout1, _ = _ffn_bwd("f1", h3, p["norm2_w"][1], p["ffn_w_in"][1], p["ffn_conv_w"][1],
                                                        p["ffn_conv_b"][1], p["ffn_w_out"][1], ffn1, dh4)
    G = {"f1_in": dwin1, "f1_out": dwout1}
    G["b_out"] = _mm("b_dwout", yb, dh3, "tn")
    wave1 = ("f1_in", "f1_out", "b_out")
    (dyb,), got = _mm("b_dy", dh3, p["b_w_out"], "nt", side=_pair_side(wave1, G))
    chip_sums = _pair_sums(wave1, G, got)
    dys, dxs, dz, g["b_d_skip"], g["b_norm_w"] = _mpost_bwd("b_dgate", y_f, y_b, xact, pb, p["b_d_skip"], p["b_norm_w"], dyb)
    dxa1, ddt_f, dla_f = _ssd_bwd("b_dscan_f", xact, dt, la, hs_f, dys, False, prev_xs=dxs)
    dxa, ddt_b, dla_b = _ssd_bwd("b_dscan_b", xact, dt, la, hs_b, dys, True, prev_all=dxa1)
    dxbc, g["b_conv_w"], g["b_conv_b"] = _mpre_bwd("b_dconv", pb, p["b_conv_w"], p["b_conv_b"], dxa)
    ddtr, g["b_dt_bias"], g["b_a_log"] = _dt_bwd("b_ddt", pb, p["b_dt_bias"], p["b_a_log"], ddt_f, dla_f, ddt_b, dla_b)
    dpb = jnp.concatenate([dz, dxbc, ddtr], axis=1)
    G["b_in"] = _mm("b_dwin", dpb, u3, "tn")
    du3 = _mm("b_du", dpb, p["b_w_in"], "nt")
    dh2, dn1_1 = _rms_bwd("b_dnorm", du3, h2, p["norm1_w"][1], dh3)
    dh1, dn2_0, G["f0_in"], dcw0, dcb0, G["f0_out"], got = _ffn_bwd("f0", h1, p["norm2_w"][0], p["ffn_w_in"][0], p["ffn_conv_w"][0],
                                                                   p["ffn_conv_b"][0], p["ffn_w_out"][0], ffn0, dh2,
                                                                   side=_pair_side(("b_in",), G))
    chip_sums.update(_pair_sums(("b_in",), G, got))
    wave3 = ("f0_in", "f0_out")
    (G["a_out"],), got = _mm("a_dwout", ya, dh1, "tn", side=_pair_side(wave3, G))
    chip_sums.update(_pair_sums(wave3, G, got))
    dya = _mm("a_dy", dh1, p["a_w_out"], "nt")
    do, dg, g["a_norm_w"] = _hgout_bwd("a_dgate", o_f, o_b, pa, p["a_norm_w"], dya)
    late = last + mid + early
    (dq1, df1, dv1, dl1), got = _gla_bwd("a_dscan_f", pa, lbl, s_f, do, False, side=_chips_side(late, chip_sums))
    shards = {u: _chip_sum(f"gl_sum_{u}", _GGEO[u], chip_sums[u], r) for u, r in zip(late, got)}
    (dq, df2, dv, dl2), got = _gla_bwd("a_dscan_b", pa, lbl, s_b, do, True, prev=(dq1, dv1), side=_halves_side(late, shards))
    shards = dict(zip(late, got))
    dpa = jnp.concatenate([dq, df1, df2, dv, dg], axis=1)
    G["a_in"] = _mm("a_dwin", u1, dpa, "tn")
    chip_sums = _pair_sums(first, G, _exchange("ga_pair", _pair_side(first, G)))
    (du1,), got = _mm("a_du", dpa, p["a_w_in"], "nt", side=_chips_side(first, chip_sums))
    mine = {u: _chip_sum(f"ga_sum_{u}", _GGEO[u], chip_sums[u], r) for u, r in zip(first, got)}
    (dx, dn1_0), got = _rms_bwd("a_dnorm", du1, x, p["norm1_w"][0], dh1, side=_halves_side(first, mine))
    shards.update(zip(first, got))
    g["a_lb_logits"] = (dl1, dl2)
    g["norm1_w"] = (dn1_0, dn1_1)
    g["norm2_w"] = (dn2_0, dn2_1)
    g["ffn_conv_w"] = (dcw0, dcw1)
    g["ffn_conv_b"] = (dcb0, dcb1)
    return loss, dx, g, shards


def _here():
    return lax.axis_index("x"), lax.axis_index("y"), lax.axis_index("c")


def _allgather8(name, src):
    blk = src.shape

    def body(x_ref, out_ref, send_sems, recv_sems, local_sem):
        x, y, c = _here()
        me, sibling = (x, y, c), (x, y, 1 - c)
        chips = [(1 - x, y), (x, 1 - y), (1 - x, 1 - y)]
        own = x_ref

        def slot(px, py, pc):
            return out_ref.at[4 * px + 2 * py + pc]

        def copy(k, block, to, from_own=False):
            return pltpu.make_async_remote_copy(
                src_ref=own if from_own else slot(*block), dst_ref=slot(*block),
                send_sem=send_sems.at[k], recv_sem=recv_sems.at[k], device_id=to, device_id_type=MESH)

        mine = pltpu.make_async_copy(own, slot(*me), local_sem)
        mine.start()
        first = [copy(0, me, sibling, from_own=True)]
        first += [copy(1 + j, me, (*chip, c), from_own=True) for j, chip in enumerate(chips)]
        for cp in first:
            cp.start()
        passed = [copy(4 + j, (*chip, c), sibling) for j, chip in enumerate(chips)]
        for j, chip in enumerate(chips):
            copy(1 + j, (*chip, c), me).wait_recv()
            passed[j].start()
        copy(0, sibling, me).wait_recv()
        for j, chip in enumerate(chips):
            copy(4 + j, (*chip, 1 - c), me).wait_recv()
        for cp in first + passed:
            cp.wait_send()
        mine.wait()

    return pl.pallas_call(
        body, name=name,
        out_shape=jax.ShapeDtypeStruct((8,) + tuple(blk), src.dtype),
        in_specs=[pl.BlockSpec(memory_space=pl.ANY)],
        out_specs=pl.BlockSpec(memory_space=pl.ANY),
        scratch_shapes=[pltpu.SemaphoreType.DMA((7,)), pltpu.SemaphoreType.DMA((7,)), pltpu.SemaphoreType.DMA],
    )(src)


def _exchange(name, side):
    n_i, n_o = len(side.ins), len(side.outs)

    def body(*refs):
        copies = side.copies(refs[:n_i], refs[n_i:n_i + n_o], *refs[n_i + n_o:])
        for cp in copies:
            cp.start()
        for cp in copies:
            cp.wait()

    return pl.pallas_call(
        body, name=name,
        out_shape=[jax.ShapeDtypeStruct(s, dt) for s, dt in side.outs],
        in_specs=[pl.BlockSpec(memory_space=pl.ANY)] * n_i,
        out_specs=[pl.BlockSpec(memory_space=pl.ANY)] * n_o,
        scratch_shapes=side.sems(),
        input_output_aliases=dict(side.alias),
    )(*side.ins)


_WGEO = {"a_in": ("col", 1024, 1280), "a_out": ("row", 256, 1024), "b_in": ("row", 1024, 1552), "b_out": ("row", 512, 1024),
         "f0_in": ("col", 1024, 1408), "f1_in": ("col", 1024, 1408), "f0_out": ("row", 704, 1024), "f1_out": ("row", 704, 1024)}
_GGEO = dict(_WGEO, b_in=("row", 1552, 1024))


def _full_shape(geo):
    kind, r, cw = geo
    return (r, 4 * cw) if kind == "col" else (4 * r, cw)


def _times(i, step):
    return i * step if isinstance(i, int) else pl.multiple_of(i * step, step & -step)


def _win(ref, geo, j, h):
    kind, r, cw = geo
    hr = r // 2
    if kind == "col":
        return ref.at[pl.ds(_times(h, hr), hr), pl.ds(_times(j, cw), cw)]
    return ref.at[pl.ds(_times(2 * j + h, hr), hr), :]


def _half(ref, geo, h):
    hr = geo[1] // 2
    return ref.at[pl.ds(_times(h, hr), hr), :]


def _gather_side(first, second, sh, full):
    n1 = len(first)

    def plan(ins, outs):
        x, y, c = _here()
        m = 2 * x + y
        remote, local = [], []
        for u, src, dst_full in zip(first, ins[:n1], outs[:n1]):
            mine, dst = _half(src, _WGEO[u], c), _win(dst_full, _WGEO[u], m, c)
            local.append((mine, dst))
            remote.append((mine, dst, (x, y, 1 - c)))
            for k in (1, 2, 3):
                t = (m + k) % 4
                remote.append((mine, dst, (t // 2, t % 2, c)))
        for u, buf in zip(second, outs[n1:]):
            for k in (1, 2, 3):
                w_ = _win(buf, _WGEO[u], (m + k) % 4, c)
                remote.append((w_, w_, (x, y, 1 - c)))
        return remote, local

    return _Side([sh[u] for u in first] + [full[u] for u in second],
                 [(_full_shape(_WGEO[u]), BF) for u in first + second], plan, 4 * n1 + 3 * len(second), n1,
                 alias={n1 + i: n1 + i for i in range(len(second))})


def _pair_side(units, G):
    def plan(ins, outs):
        x, y, c = _here()
        return [(_win(gr, _GGEO[u], j, 1 - c), got.at[j], (x, y, 1 - c))
                for u, gr, got in zip(units, ins, outs) for j in range(4)], []

    return _Side([G[u] for u in units], [((4, _GGEO[u][1] // 2, _GGEO[u][2]), F32) for u in units], plan, 4 * len(units), 0)


def _pair_sums(units, G, gots):
    out = {}
    for u, got in zip(units, gots):
        blk = got.shape[1:]
        at = (lambda j: (lax.axis_index("c"), j)) if _GGEO[u][0] == "col" else (lambda j: (2 * j + lax.axis_index("c"), 0))
        slab = lambda j: (j, 0, 0)
        out[u] = _vcall(f"g_pair_sum_{u}", lambda a, b: (a + b[0])[None], (4,),
                        [(G[u], blk, at), (got, (1,) + blk, slab)], [(got.shape, BF, (1,) + blk, slab)])
    return out


def _chips_side(units, chip_sums):
    def plan(ins, outs):
        x, y, c = _here()
        m = 2 * x + y
        remote = []
        for s, got in zip(ins, outs):
            for k in (1, 2, 3):
                t = (m + k) % 4
                remote.append((s.at[t], got.at[k - 1], (t // 2, t % 2, c)))
        return remote, []

    return _Side([chip_sums[u] for u in units], [((3,) + chip_sums[u].shape[1:], BF) for u in units], plan,
                 3 * len(units), 0)


def _chip_sum(name, geo, chip_sums, got):
    _, r, cw = geo
    blk = (r // 2, cw)
    return _vcall(name, lambda a, b: ((a[0].astype(F32) + b[0].astype(F32)) + b[1].astype(F32)) + b[2].astype(F32), (1,),
                  [(chip_sums, (1,) + blk, lambda i: (2 * lax.axis_index("x") + lax.axis_index("y"), 0, 0)),
                   (got, (3,) + blk, lambda i: (0, 0, 0))],
                  [((r, cw), F32, blk, lambda i: (lax.axis_index("c"), 0))])


def _halves_side(units, shards):
    def plan(ins, outs):
        x, y, c = _here()
        return [(_half(o, _GGEO[u], c), _half(o, _GGEO[u], c), (x, y, 1 - c)) for u, o in zip(units, outs)], []

    return _Side([shards[u] for u in units], [(shards[u].shape, F32) for u in units], plan, len(units), 0,
                 alias={i: i for i in range(len(units))})


def _adam(name, w, g, m, v):
    rows, cols = w.shape
    tb = _pick(rows, (256, 128, 64, 8))

    def fn(wv, gv, mv, vv):
        m2 = ADAM_B1 * mv + (1.0 - ADAM_B1) * gv
        v2 = ADAM_B2 * vv + (1.0 - ADAM_B2) * jnp.square(gv)
        m_hat = m2 / (1.0 - ADAM_B1 ** ADAM_STEP)
        v_hat = v2 / (1.0 - ADAM_B2 ** ADAM_STEP)
        return -ADAM_LR * (m_hat / (jnp.sqrt(v_hat) + ADAM_EPS) + ADAM_WD * wv), m2, v2

    at = lambda i: (i, 0)
    return _vcall(name, fn, (rows // tb,), [(a, (tb, cols), at) for a in (w, g, m, v)],
                  [((rows, cols), F32, (tb, cols), at)] * 3)


def _pack(arrays, width, row_multiple, dtype):
    parts, offs, at = [], [], 0
    for a in arrays:
        flat = a.reshape(-1).astype(dtype)
        rows = -(-flat.shape[0] // (width * row_multiple)) * row_multiple
        parts.append(jnp.pad(flat, (0, rows * width - flat.shape[0])).reshape(rows, width))
        offs.append(at)
        at += rows
    return jnp.concatenate(parts, axis=0), offs


def _unpack(flat, shapes, offs):
    out = []
    for shp, at in zip(shapes, offs):
        n = 1
        for s in shp:
            n *= s
        rows = -(-n // flat.shape[1])
        out.append(flat[at:at + rows].reshape(-1)[:n].reshape(shp))
    return out


_BIG = ("a_w_in", "a_w_out", "b_w_in", "b_w_out", "ffn_w_in", "ffn_w_out")
_SMALL_SPLIT = ("b_conv_w", "b_conv_b", "b_norm_w", "ffn_conv_w")
_SMALL = ("norm1_w", "norm2_w", "a_lb_logits", "a_norm_w", "b_conv_w", "b_conv_b", "b_dt_bias", "b_a_log", "b_d_skip",
          "b_norm_w", "ffn_conv_w", "ffn_conv_b", "final_norm_w")
_ORDER = ("norm1_w", "norm2_w", "a_w_in", "a_lb_logits", "a_norm_w", "a_w_out", "b_w_in", "b_conv_w", "b_conv_b", "b_dt_bias",
          "b_a_log", "b_d_skip", "b_norm_w", "b_w_out", "ffn_w_in", "ffn_conv_w", "ffn_conv_b", "ffn_w_out", "final_norm_w")


def kernel(x, norm1_w, norm2_w, a_w_in, a_lb_logits, a_norm_w, a_w_out, b_w_in, b_conv_w, b_conv_b, b_dt_bias, b_a_log, b_d_skip, b_norm_w, b_w_out, ffn_w_in, ffn_conv_w, ffn_conv_b, ffn_w_out, final_norm_w, loss_target, m_norm1_w, m_norm2_w, m_a_w_in, m_a_lb_logits, m_a_norm_w, m_a_w_out, m_b_w_in, m_b_conv_w, m_b_conv_b, m_b_dt_bias, m_b_a_log, m_b_d_skip, m_b_norm_w, m_b_w_out, m_ffn_w_in, m_ffn_conv_w, m_ffn_conv_b, m_ffn_w_out, m_final_norm_w, v_norm1_w, v_norm2_w, v_a_w_in, v_a_lb_logits, v_a_norm_w, v_a_w_out, v_b_w_in, v_b_conv_w, v_b_conv_b, v_b_dt_bias, v_b_a_log, v_b_d_skip, v_b_norm_w, v_b_w_out, v_ffn_w_in, v_ffn_conv_w, v_ffn_conv_b, v_ffn_w_out, v_final_norm_w):
    w = dict(norm1_w=norm1_w, norm2_w=norm2_w, a_w_in=a_w_in, a_lb_logits=a_lb_logits, a_norm_w=a_norm_w, a_w_out=a_w_out,
             b_w_in=b_w_in, b_conv_w=b_conv_w, b_conv_b=b_conv_b, b_dt_bias=b_dt_bias, b_a_log=b_a_log, b_d_skip=b_d_skip,
             b_norm_w=b_norm_w, b_w_out=b_w_out, ffn_w_in=ffn_w_in, ffn_conv_w=ffn_conv_w, ffn_conv_b=ffn_conv_b,
             ffn_w_out=ffn_w_out, final_norm_w=final_norm_w)
    mom = dict(norm1_w=m_norm1_w, norm2_w=m_norm2_w, a_w_in=m_a_w_in, a_lb_logits=m_a_lb_logits, a_norm_w=m_a_norm_w,
               a_w_out=m_a_w_out, b_w_in=m_b_w_in, b_conv_w=m_b_conv_w, b_conv_b=m_b_conv_b, b_dt_bias=m_b_dt_bias,
               b_a_log=m_b_a_log, b_d_skip=m_b_d_skip, b_norm_w=m_b_norm_w, b_w_out=m_b_w_out, ffn_w_in=m_ffn_w_in,
               ffn_conv_w=m_ffn_conv_w, ffn_conv_b=m_ffn_conv_b, ffn_w_out=m_ffn_w_out, final_norm_w=m_final_norm_w)
    var = dict(norm1_w=v_norm1_w, norm2_w=v_norm2_w, a_w_in=v_a_w_in, a_lb_logits=v_a_lb_logits, a_norm_w=v_a_norm_w,
               a_w_out=v_a_w_out, b_w_in=v_b_w_in, b_conv_w=v_b_conv_w, b_conv_b=v_b_conv_b, b_dt_bias=v_b_dt_bias,
               b_a_log=v_b_a_log, b_d_skip=v_b_d_skip, b_norm_w=v_b_norm_w, b_w_out=v_b_w_out, ffn_w_in=v_ffn_w_in,
               ffn_conv_w=v_ffn_conv_w, ffn_conv_b=v_ffn_conv_b, ffn_w_out=v_ffn_w_out, final_norm_w=v_final_norm_w)
    chip = 2 * lax.axis_index("x") + lax.axis_index("y")

    sh = {"a_in": a_w_in[0], "a_out": a_w_out[0], "b_in": b_w_in[0], "b_out": b_w_out[0], "f0_in": ffn_w_in[0],
          "f1_in": ffn_w_in[1], "f0_out": ffn_w_out[0], "f1_out": ffn_w_out[1]}
    sh = {u: a.astype(BF) for u, a in sh.items()}
    small_shapes = [w[n].shape for n in _SMALL_SPLIT]
    spack, small_offs = _pack([w[n] for n in _SMALL_SPLIT], 128, 8, F32)
    sall = _allgather8("s_gather", spack)
    sshards = [_unpack(sall[2 * j], small_shapes, small_offs) for j in range(4)]
    sfull = {n: jnp.concatenate([sshards[j][i] for j in range(4)], axis=-1) for i, n in enumerate(_SMALL_SPLIT)}

    p = dict(
        norm1_w=norm1_w, norm2_w=norm2_w, a_lb_logits=a_lb_logits, a_norm_w=a_norm_w[0], final_norm_w=final_norm_w,
        b_conv_w=sfull["b_conv_w"][0], b_conv_b=sfull["b_conv_b"][0], b_norm_w=sfull["b_norm_w"],
        ffn_conv_w=sfull["ffn_conv_w"], ffn_conv_b=ffn_conv_b,
        b_dt_bias=jnp.pad(b_dt_bias.reshape(1, 2 * SSM_HEADS), ((0, 0), (0, 128 - 2 * SSM_HEADS))),
        b_a_log=jnp.pad(b_a_log.reshape(1, 2 * SSM_HEADS), ((0, 0), (0, 128 - 2 * SSM_HEADS))),
        b_d_skip=jnp.repeat(b_d_skip[0], SSM_HD)[None],
    )

    loss_row, dx, g, gs_ = _sequence_grads(x[0], loss_target[0], p, sh)
    grads = {"a_w_in": gs_["a_in"][None], "a_w_out": gs_["a_out"][None], "b_w_in": gs_["b_in"].T[None],
             "b_w_out": gs_["b_out"][None], "ffn_w_in": jnp.stack([gs_["f0_in"], gs_["f1_in"]]),
             "ffn_w_out": jnp.stack([gs_["f0_out"], gs_["f1_out"]])}

    gsmall = {
        "norm1_w": jnp.concatenate(g["norm1_w"], axis=0), "norm2_w": jnp.concatenate(g["norm2_w"], axis=0),
        "a_lb_logits": jnp.stack(g["a_lb_logits"]), "a_norm_w": g["a_norm_w"], "b_conv_w": g["b_conv_w"],
        "b_conv_b": g["b_conv_b"], "b_dt_bias": g["b_dt_bias"], "b_a_log": g["b_a_log"], "b_d_skip": g["b_d_skip"],
        "b_norm_w": g["b_norm_w"], "ffn_conv_w": jnp.stack(g["ffn_conv_w"]),
        "ffn_conv_b": jnp.concatenate(g["ffn_conv_b"], axis=0), "final_norm_w": g["final_norm_w"],
    }
    pieces = [gsmall[n] for n in _SMALL] + [loss_row]
    piece_shapes = [a.shape for a in pieces]
    gspack, gs_offs = _pack(pieces, 128, 8, F32)
    rows = gspack.shape[0]
    gsall = _allgather8("gs_gather", gspack)

    def sum8(a):
        r = a[0]
        for i in range(1, 8):
            r = r + a[i]
        return r

    gssum = _vcall("gs_sum", sum8, (1,), [(gsall, (8, rows, 128), lambda i: (0, 0, 0))],
                   [((rows, 128), F32, (rows, 128), lambda i: (0, 0))])
    gs = dict(zip(_SMALL + ("loss",), _unpack(gssum, piece_shapes, gs_offs)))
    loss = gs["loss"][0, 0]
    lb2 = gs["a_lb_logits"]
    small_grads = {
        "norm1_w": gs["norm1_w"], "norm2_w": gs["norm2_w"], "a_lb_logits": lb2[0] + lb2[1], "a_norm_w": gs["a_norm_w"],
        "b_dt_bias": gs["b_dt_bias"][:, :2 * SSM_HEADS].reshape(1, 2, SSM_HEADS),
        "b_a_log": gs["b_a_log"][:, :2 * SSM_HEADS].reshape(1, 2, SSM_HEADS),
        "b_d_skip": gs["b_d_skip"].reshape(1, SSM_HEADS, SSM_HD).sum(axis=-1),
        "ffn_conv_b": gs["ffn_conv_b"], "final_norm_w": gs["final_norm_w"][0],
        "b_conv_w": gs["b_conv_w"][None], "b_conv_b": gs["b_conv_b"], "b_norm_w": gs["b_norm_w"], "ffn_conv_w": gs["ffn_conv_w"],
    }
    for n in _SMALL_SPLIT:
        width = w[n].shape[-1]
        small_grads[n] = lax.dynamic_slice_in_dim(small_grads[n], chip * width, width, axis=small_grads[n].ndim - 1)
    grads.update(small_grads)

    delta, new_m, new_v = {}, {}, {}
    for n in _BIG:
        shp = w[n].shape
        two_d = (shp[0] * shp[1], shp[2])
        d_, m_, v_ = _adam(f"adam_{n}", w[n].reshape(two_d), grads[n].reshape(two_d), mom[n].reshape(two_d), var[n].reshape(two_d))
        delta[n], new_m[n], new_v[n] = d_.reshape(shp), m_.reshape(shp), v_.reshape(shp)
    s_shapes = [w[n].shape for n in _SMALL]
    packs = [_pack([src[n] for n in _SMALL], 128, 8, F32) for src in (w, grads, mom, var)]
    outs = _adam("adam_small", *[pk[0] for pk in packs])
    for res, dst in zip(outs, (delta, new_m, new_v)):
        dst.update(dict(zip(_SMALL, _unpack(res, s_shapes, packs[0][1]))))

    return (loss, dx[None], *[grads[n] for n in _ORDER], *[delta[n] for n in _ORDER],
            *[new_m[n] for n in _ORDER], *[new_v[n] for n in _ORDER])
```

```python
import functools

import jax
import jax.numpy as jnp
from jax import lax
from jax.experimental import pallas as pl
from jax.experimental.pallas import tpu as pltpu

F32, BF = jnp.float32, jnp.bfloat16

D = 1024
EPS = 1e-6
HG_HEADS, HG_HD, HG_CHUNK, HG_SUB = 8, 128, 64, 8
HG_HB = 8
HG_CPS = 2
SSM_GB = 4
D_INNER, SSM_HEADS, SSM_HD, SSM_GROUPS, SSM_HPG, SSM_N, SSD_CHUNK = 2048, 32, 64, 8, 4, 128, 128
CONV_DIM = D_INNER + 2 * SSM_GROUPS * SSM_N
B_PROJ = 2 * D_INNER + 2 * SSM_GROUPS * SSM_N + 2 * SSM_HEADS
B_PROJ_PAD = 6272
D_FF = 2816
NEG = -1e30
ROWS = 512
GATE_ROWS = 1024
VMEM_LIMIT = 56 * 1024 * 1024

ADAM_LR, ADAM_B1, ADAM_B2, ADAM_EPS, ADAM_WD, ADAM_STEP = 0.001, 0.9, 0.999, 1e-08, 0.01, 10

MESH = pl.DeviceIdType.MESH


def _pick(n, cands):
    for c in cands:
        if n % c == 0:
            return c
    return n


class _Side:
    def __init__(self, ins, outs, plan, n_remote, n_local, alias=None):
        self.ins, self.outs, self.plan, self.n_remote, self.n_local = list(ins), list(outs), plan, n_remote, n_local
        self.alias = alias or {}

    def copies(self, in_refs, out_refs, send_sems, recv_sems, local_sems):
        remote, local = self.plan(in_refs, out_refs)
        cps = [pltpu.make_async_copy(s, d, local_sems.at[i]) for i, (s, d) in enumerate(local)]
        cps += [pltpu.make_async_remote_copy(src_ref=s, dst_ref=d, send_sem=send_sems.at[i], recv_sem=recv_sems.at[i],
                                             device_id=dev, device_id_type=MESH)
                for i, (s, d, dev) in enumerate(remote)]
        return cps

    def sems(self):
        return [pltpu.SemaphoreType.DMA((self.n_remote,)), pltpu.SemaphoreType.DMA((self.n_remote,)),
                pltpu.SemaphoreType.DMA((max(self.n_local, 1),))]


def _vcall(name, fn, grid, ins, outs, acc=None, scratch=(), side=None):
    acc = acc or {}
    n_in, n_out, nd = len(ins), len(outs), len(grid)
    n_sin = len(side.ins) if side else 0
    n_sout = len(side.outs) if side else 0
    n_scr = len(scratch)

    def body(*refs):
        in_refs, refs = refs[:n_in], refs[n_in:]
        sin_refs, refs = refs[:n_sin], refs[n_sin:]
        out_refs, refs = refs[:n_out], refs[n_out:]
        sout_refs, refs = refs[:n_sout], refs[n_sout:]
        scr, sems = refs[:n_scr], refs[n_scr:]
        if side:
            at_first, at_last = None, None
            for ax in range(nd):
                f, l = pl.program_id(ax) == 0, pl.program_id(ax) == grid[ax] - 1
                at_first = f if at_first is None else jnp.logical_and(at_first, f)
                at_last = l if at_last is None else jnp.logical_and(at_last, l)

            @pl.when(at_first)
            def _():
                for cp in side.copies(sin_refs, sout_refs, *sems):
                    cp.start()

        res = fn(*[r[...] for r in in_refs], *scr)
        if not isinstance(res, (tuple, list)):
            res = (res,)
        if side:
            @pl.when(at_last)
            def _():
                for cp in side.copies(sin_refs, sout_refs, *sems):
                    cp.wait()
        for j, (o_ref, r) in enumerate(zip(out_refs, res)):
            mode = acc.get(j)
            if mode is None:
                o_ref[...] = r.astype(o_ref.dtype)
                continue
            first = pl.program_id(nd - 1) == 0
            if mode == "all":
                for ax in range(nd - 1):
                    first = jnp.logical_and(first, pl.program_id(ax) == 0)

            @pl.when(first)
            def _():
                o_ref[...] = r.astype(o_ref.dtype)

            @pl.when(jnp.logical_not(first))
            def _():
                o_ref[...] += r.astype(o_ref.dtype)

    hbm = pl.BlockSpec(memory_space=pl.ANY)
    in_specs = [pl.BlockSpec(bs, im) for _, bs, im in ins] + [hbm] * n_sin
    out_specs = [pl.BlockSpec(bs, im) for _, _, bs, im in outs] + [hbm] * n_sout
    params = pltpu.CompilerParams(dimension_semantics=("arbitrary",) * nd, vmem_limit_bytes=VMEM_LIMIT)
    out_shape = [jax.ShapeDtypeStruct(s, dt) for s, dt, _, _ in outs]
    operands = [a for a, _, _ in ins]
    scratch = list(scratch)
    aliases = {}
    if side:
        out_shape += [jax.ShapeDtypeStruct(s, dt) for s, dt in side.outs]
        operands += side.ins
        scratch += side.sems()
        aliases = {n_in + i: n_out + o for i, o in side.alias.items()}
    out = pl.pallas_call(body, name=name, grid=grid, in_specs=in_specs, out_specs=out_specs, out_shape=out_shape,
                         scratch_shapes=scratch, compiler_params=params, input_output_aliases=aliases)(*operands)
    if side:
        return tuple(out[:n_out]), tuple(out[n_out:])
    return out[0] if n_out == 1 else out


def _mm(name, a, b, kind, out_dtype=F32, add=None, side=None):
    if kind == "tn":
        m, k = a.shape
        _, n = b.shape
        tm = _pick(m, (1024, 512, 256))
        tk = _pick(k, (1024, 1408, 896, 512, 256, 128))
        tn = _pick(n, (1024, 1408, 896, 512, 256, 128))

        def fn(av, bv):
            return lax.dot_general(av.astype(BF), bv.astype(BF), (((0,), (0,)), ((), ())),
                                   preferred_element_type=F32)

        return _vcall(name, fn, (k // tk, n // tn, m // tm),
                      [(a, (tm, tk), lambda i, j, s: (s, i)), (b, (tm, tn), lambda i, j, s: (s, j))],
                      [((k, n), F32, (tk, tn), lambda i, j, s: (i, j))], acc={0: "last"}, side=side)
    m, k = a.shape
    n = b.shape[1] if kind == "nn" else b.shape[0]
    long_k = k > 4096
    tm = _pick(m, (1024, 512, 256))
    tn = _pick(n, (512, 896, 256, 128)) if long_k else _pick(n, (1024, 1408, 896, 512, 256, 128))
    dims = (((1,), (0,)), ((), ())) if kind == "nn" else (((1,), (1,)), ((), ()))

    def fn(av, bv, *rest):
        r = lax.dot_general(av.astype(BF), bv.astype(BF), dims, preferred_element_type=F32)
        return r + rest[0] if rest else r

    ins = [(a, (tm, k), lambda i, j: (i, 0)),
           (b, (k, tn), lambda i, j: (0, j)) if kind == "nn" else (b, (tn, k), lambda i, j: (j, 0))]
    if add is not None:
        ins.append((add, (tm, tn), lambda i, j: (i, j)))
    return _vcall(name, fn, (m // tm, n // tn), ins, [((m, n), out_dtype, (tm, tn), lambda i, j: (i, j))], side=side)


def _rms(h, w):
    return h * lax.rsqrt(jnp.mean(h * h, axis=-1, keepdims=True) + EPS) * w


def _rms_fwd(name, h, w):
    L = h.shape[0]
    tb = _pick(L, (ROWS,))
    return _vcall(name, _rms, (L // tb,),
                  [(h, (tb, D), lambda i: (i, 0)), (w.reshape(1, D), (1, D), lambda i: (0, 0))],
                  [((L, D), BF, (tb, D), lambda i: (i, 0))])


def _rms_bwd(name, du, h, w, dh_next, side=None):
    L = h.shape[0]
    tb = _pick(L, (ROWS,))

    def fn(duv, hv, wv, dnv):
        _, vjp = jax.vjp(_rms, hv, wv)
        dh, dw = vjp(duv)
        return dh + dnv, dw

    row = lambda i: (i, 0)
    return _vcall(name, fn, (L // tb,),
                  [(du, (tb, D), row), (h, (tb, D), row), (w.reshape(1, D), (1, D), lambda i: (0, 0)),
                   (dh_next, (tb, D), row)],
                  [((L, D), F32, (tb, D), row), ((1, D), F32, (1, D), lambda i: (0, 0))], acc={1: "all"}, side=side)


def _loss_head(name, h, tgt, w):
    L = h.shape[0]
    tb = _pick(L, (ROWS,))

    def lossf(hv, wv, tv):
        err = _rms(hv, wv) - tv
        return 0.5 * jnp.sum(err * err) * (1.0 / D)

    def fn(hv, wv, tv):
        val, vjp = jax.vjp(lambda a, b: lossf(a, b, tv), hv, wv)
        dh, dw = vjp(jnp.ones((), F32))
        return jnp.full((1, 128), val, F32), dh, dw

    row = lambda i: (i, 0)
    zero = lambda i: (0, 0)
    return _vcall(name, fn, (L // tb,),
                  [(h, (tb, D), row), (w.reshape(1, D), (1, D), zero), (tgt, (tb, D), row)],
                  [((1, 128), F32, (1, 128), zero), ((L, D), F32, (tb, D), row), ((1, D), F32, (1, D), zero)],
                  acc={0: "all", 2: "all"})


def _bf(x):
    return x.astype(BF)


def _dot(a, b, dims):
    return lax.dot_general(a, b, (dims, ((), ())), preferred_element_type=F32)


def _tri(n, reverse):
    r = lax.broadcasted_iota(jnp.int32, (n, n), 0)
    c = lax.broadcasted_iota(jnp.int32, (n, n), 1)
    return (r <= c) if reverse else (r >= c)


def _tri_matmul(n, reverse, x):
    hi = x.astype(BF)
    r1 = x - hi.astype(F32)
    mid = r1.astype(BF)
    lo = (r1 - mid.astype(F32)).astype(BF)
    y = _dot(_tri(n, reverse).astype(BF), jnp.concatenate([hi, mid, lo], axis=1), ((1,), (0,)))
    w = x.shape[1]
    return (y[:, :w] + y[:, w:2 * w]) + y[:, 2 * w:]


@functools.partial(jax.custom_vjp, nondiff_argnums=(0, 1))
def _running_sum(n, reverse, x):
    return _tri_matmul(n, reverse, x)


def _running_sum_fwd(n, reverse, x):
    return _tri_matmul(n, reverse, x), None


def _running_sum_bwd(n, reverse, _, ct):
    return (_tri_matmul(n, not reverse, ct),)


_running_sum.defvjp(_running_sum_fwd, _running_sum_bwd)


def _gla_chunk(q_raw, f_raw, v, lb3, S, reverse):
    C, SB, HD = HG_CHUNK, HG_SUB, HG_HD
    H = S.shape[0]
    heads = [slice(HD * h, HD * (h + 1)) for h in range(H)]
    row3 = lax.broadcasted_iota(jnp.int32, (3, 1), 0)
    e = jnp.exp(lb3 - jnp.max(lb3, axis=0, keepdims=True))
    lb = jnp.sum(jnp.where(row3 == 0, e, 0.0), axis=0, keepdims=True) / jnp.sum(e, axis=0, keepdims=True)
    q = q_raw * jax.nn.sigmoid(q_raw)
    f = lb + (1.0 - lb) * jax.nn.sigmoid(f_raw)
    g = jnp.log(f)
    k = 1.0 - f
    b = _running_sum(C, reverse, g)
    row = lax.broadcasted_iota(jnp.int32, (C, 1), 0)
    vb = _bf(v)

    def rowof(x, t):
        return jnp.sum(jnp.where(row == t, x, 0.0), axis=0, keepdims=True)

    qe = _bf(q * jnp.exp(b))
    o = [_dot(qe[:, hs], _bf(S[h]), ((1,), (0,))) for h, hs in enumerate(heads)]
    att = [[] for _ in range(H)]
    for i in range(C // SB):
        lo = SB * i
        if (not reverse and i == 0) or (reverse and i == C // SB - 1):
            for h in range(H):
                att[h].append(jnp.zeros((SB, C), F32))
            continue
        first = lo + SB - 1 if reverse else lo
        r = rowof(b, first) - rowof(g, first)
        before = (row >= lo + SB) if reverse else (row < lo)
        qi = q[lo:lo + SB] * jnp.exp(b[lo:lo + SB] - r)
        kk = _bf(k * jnp.exp(jnp.where(before, r - b, NEG)))
        for h, hs in enumerate(heads):
            att[h].append(_dot(_bf(qi[:, hs]), kk[:, hs], ((1,), (1,))))
    o = [o[h] + _dot(_bf(jnp.concatenate(att[h], axis=0)), vb[:, hs], ((1,), (0,))) for h, hs in enumerate(heads)]
    s_i = lax.broadcasted_iota(jnp.int32, (SB, SB, HD), 0)
    t_i = lax.broadcasted_iota(jnp.int32, (SB, SB, HD), 1)
    pair = (t_i <= s_i) if reverse else (t_i >= s_i)
    shp = (SB, SB, HD)
    diag = [[] for _ in range(H)]
    for i in range(C // SB):
        rows = slice(SB * i, SB * (i + 1))
        for h, hs in enumerate(heads):
            qb, kb, bb = q[rows, hs], k[rows, hs], b[rows, hs]
            dif = lax.broadcast_in_dim(bb, shp, (1, 2)) - lax.broadcast_in_dim(bb, shp, (0, 2))
            w = lax.broadcast_in_dim(qb, shp, (1, 2)) * jnp.exp(jnp.where(pair, dif, NEG)) * lax.broadcast_in_dim(kb, shp, (0, 2))
            d = jnp.sum(w, axis=2, keepdims=True)
            diag[h].append(jnp.sum(d * lax.broadcast_in_dim(v[rows, hs], shp, (0, 2)), axis=0))
    o = jnp.concatenate([o[h] + jnp.concatenate(diag[h], axis=0) for h in range(H)], axis=1)
    btot = rowof(b, 0 if reverse else C - 1)
    kd = _bf(k * jnp.exp(btot - b))
    eye = lax.broadcasted_iota(jnp.int32, (HD, HD), 0) == lax.broadcasted_iota(jnp.int32, (HD, HD), 1)
    s_new = []
    for h, hs in enumerate(heads):
        btot_col = jnp.sum(jnp.where(eye, btot[:, hs], 0.0), axis=1, keepdims=True)
        s_new.append((jnp.exp(btot_col) * S[h] + _dot(kd[:, hs], vb[:, hs], ((0,), (0,))))[None])
    return o, jnp.concatenate(s_new, axis=0)


def _gla_fwd(name, pa, lbl, reverse, side=None):
    L = pa.shape[0]
    C, cs = HG_CHUNK, HG_CPS
    nc, ns = L // C, L // (C * cs)
    cidx = (lambda i: ns - 1 - i) if reverse else (lambda i: i)
    sec = 2 if reverse else 1
    hb_n, nh = HG_HB, HG_HEADS // HG_HB

    def fn(qr, fr, v, lb3, s_ref):
        @pl.when(pl.program_id(1) == 0)
        def _():
            s_ref[...] = jnp.zeros_like(s_ref)

        s = s_ref[...]
        outs, olds = [None] * cs, [None] * cs
        for c in (reversed(range(cs)) if reverse else range(cs)):
            rows = slice(C * c, C * (c + 1))
            olds[c] = s[None]
            outs[c], s = _gla_chunk(qr[rows], fr[rows], v[rows], lb3, s, reverse)
        s_ref[...] = s
        return jnp.concatenate(outs, axis=0), jnp.concatenate(olds, axis=0)

    blk = (C * cs, HG_HD * hb_n)
    return _vcall(name, fn, (nh, ns),
                  [(pa, blk, lambda h, i: (cidx(i), h)), (pa, blk, lambda h, i: (cidx(i), sec * nh + h)),
                   (pa, blk, lambda h, i: (cidx(i), 3 * nh + h)), (lbl, (3, HG_HD * hb_n), lambda h, i: (0, h))],
                  [((L, D), F32, blk, lambda h, i: (cidx(i), h)),
                   ((nc, HG_HEADS, HG_HD, HG_HD), F32, (cs, hb_n, HG_HD, HG_HD), lambda h, i: (cidx(i), h, 0, 0))],
                  scratch=[pltpu.VMEM((hb_n, HG_HD, HG_HD), F32)], side=side)


def _gla_bwd(name, pa, lbl, s_in, do, reverse, prev=None, side=None):
    L = pa.shape[0]
    C, cs = HG_CHUNK, HG_CPS
    ns = L // (C * cs)
    cidx = (lambda i: i) if reverse else (lambda i: ns - 1 - i)
    sec = 2 if reverse else 1
    n_prev = 0 if prev is None else 2
    hb_n, nh = HG_HB, HG_HEADS // HG_HB

    def fn(qr, fr, v, lb3, s, dov, *rest):
        ds_ref = rest[n_prev]

        @pl.when(pl.program_id(1) == 0)
        def _():
            ds_ref[...] = jnp.zeros_like(ds_ref)

        ds = ds_ref[...]
        parts, dlb = [None] * cs, None
        for c in (range(cs) if reverse else reversed(range(cs))):
            rows = slice(C * c, C * (c + 1))
            _, vjp = jax.vjp(lambda *a: _gla_chunk(*a, reverse), qr[rows], fr[rows], v[rows], lb3, s[c])
            dq, df, dv, dl, ds = vjp((dov[rows], ds))
            if n_prev:
                dq, dv = dq + rest[0][rows], dv + rest[1][rows]
            parts[c] = (dq, df, dv)
            dlb = dl if dlb is None else dlb + dl
        ds_ref[...] = ds
        return tuple(jnp.concatenate([p_[j] for p_ in parts], axis=0) for j in range(3)) + (dlb,)

    blk = (C * cs, HG_HD * hb_n)
    at = lambda h, i: (cidx(i), h)
    ins = [(pa, blk, at), (pa, blk, lambda h, i: (cidx(i), sec * nh + h)), (pa, blk, lambda h, i: (cidx(i), 3 * nh + h)),
           (lbl, (3, HG_HD * hb_n), lambda h, i: (0, h)),
           (s_in, (cs, hb_n, HG_HD, HG_HD), lambda h, i: (cidx(i), h, 0, 0)), (do, blk, at)]
    if prev is not None:
        ins += [(prev[0], blk, at), (prev[1], blk, at)]
    sum_dt = F32 if prev is None else BF
    return _vcall(name, fn, (nh, ns), ins,
                  [((L, D), sum_dt, blk, at), ((L, D), BF, blk, at), ((L, D), sum_dt, blk, at),
                   ((3, D), F32, (3, HG_HD * hb_n), lambda h, i: (0, h))],
                  acc={3: "last"}, scratch=[pltpu.VMEM((hb_n, HG_HD, HG_HD), F32)], side=side)


def _hgout(o_f, o_b, g, nw):
    o = o_f + o_b
    return _rms(o, nw) * (g * jax.nn.sigmoid(g))


def _hgout_fwd(name, o_f, o_b, pa, nw, side=None):
    L = o_f.shape[0]
    tb = _pick(L, (GATE_ROWS, ROWS))
    blk = (tb, HG_HD)
    at = lambda h, i: (i, h)
    return _vcall(name, _hgout, (HG_HEADS, L // tb),
                  [(o_f, blk, at), (o_b, blk, at), (pa, blk, lambda h, i: (i, 32 + h)),
                   (nw.reshape(1, HG_HD), (1, HG_HD), lambda h, i: (0, 0))],
                  [((L, D), BF, blk, at)], side=side)


def _hgout_bwd(name, o_f, o_b, pa, nw, dy, side=None):
    L = o_f.shape[0]
    tb = _pick(L, (GATE_ROWS, ROWS))

    def fn(ofv, obv, gv, nwv, dyv):
        _, vjp = jax.vjp(_hgout, ofv, obv, gv, nwv)
        do, _, dg, dnw = vjp(dyv)
        return do, dg, dnw

    blk = (tb, HG_HD)
    at = lambda h, i: (i, h)
    zero = lambda h, i: (0, 0)
    return _vcall(name, fn, (HG_HEADS, L // tb),
                  [(o_f, blk, at), (o_b, blk, at), (pa, blk, lambda h, i: (i, 32 + h)),
                   (nw.reshape(1, HG_HD), (1, HG_HD), zero), (dy, blk, at)],
                  [((L, D), F32, blk, at), ((L, D), BF, blk, at), ((1, HG_HD), F32, (1, HG_HD), zero)],
                  acc={2: "all"}, side=side)


def _shift(x, s):
    if s == 0:
        return x
    n = x.shape[0]
    t = lax.broadcasted_iota(jnp.int32, (n, 1), 0)
    if s > 0:
        return jnp.where(t >= s, pltpu.roll(x, s, 0), 0.0)
    return jnp.where(t < n + s, pltpu.roll(x, n + s, 0), 0.0)


def _conv(x, w, b):
    kk = w.shape[0]
    p = (kk - 1) // 2
    y = b
    for j in range(kk):
        y = y + w[j:j + 1] * _shift(x, p - j)
    return y


def _conv_bwd(x, w, dc):
    kk = w.shape[0]
    p = (kk - 1) // 2
    dx = None
    dws = []
    for j in range(kk):
        t = w[j:j + 1] * _shift(dc, j - p)
        dx = t if dx is None else dx + t
        dws.append(jnp.sum(dc * _shift(x, p - j), axis=0, keepdims=True))
    rows = lax.broadcasted_iota(jnp.int32, (kk, 1), 0)
    dw = None
    for j in range(kk):
        t = jnp.where(rows == j, dws[j], 0.0)
        dw = t if dw is None else dw + t
    return dx, dw, jnp.sum(dc, axis=0, keepdims=True)


def _silu(c):
    return c * jax.nn.sigmoid(c)


def _silu_grad(c):
    s = jax.nn.sigmoid(c)
    return s * (1.0 + c * (1.0 - s))


def _glu_fwd(name, pf, cw, cb):
    L = pf.shape[0]
    tc = 128
    nt = D_FF // tc
    return _vcall(name, lambda gate, val, w, b: _silu(_conv(gate, w, b)) * val, (nt,),
                  [(pf, (L, tc), lambda j: (0, j)), (pf, (L, tc), lambda j: (0, nt + j)),
                   (cw, (3, tc), lambda j: (0, j)), (cb.reshape(1, D_FF), (1, tc), lambda j: (0, j))],
                  [((L, D_FF), BF, (L, tc), lambda j: (0, j))])


def _glu_bwd(name, pf, cw, cb, dy, side=None):
    L = pf.shape[0]
    tc = 128
    nt = D_FF // tc

    def fn(gate, val, w, b, dyv):
        c = _conv(gate, w, b)
        dgate, dw, db = _conv_bwd(gate, w, dyv * val * _silu_grad(c))
        return dgate, dyv * _silu(c), dw, db

    col = lambda j: (0, j)
    return _vcall(name, fn, (nt,),
                  [(pf, (L, tc), col), (pf, (L, tc), lambda j: (0, nt + j)), (cw, (3, tc), col),
                   (cb.reshape(1, D_FF), (1, tc), col), (dy, (L, tc), col)],
                  [((L, D_FF), BF, (L, tc), col), ((L, D_FF), BF, (L, tc), col),
                   ((3, D_FF), F32, (3, tc), col), ((1, D_FF), F32, (1, tc), col)], side=side)


def _perm_tile(j):
    return jnp.where(j < 16, 4 * (j // 2) + j % 2, jnp.where(j < 24, 4 * (j - 16) + 2, 4 * (j - 24) + 3))


def _mpre_fwd(name, pb, cw, cb):
    L = pb.shape[0]
    tc = 128
    return _vcall(name, lambda x, w, b: _silu(_conv(x, w, b)), (CONV_DIM // tc,),
                  [(pb, (L, tc), lambda j: (0, 16 + j)), (cw, (5, tc), lambda j: (0, j)),
                   (cb.reshape(1, CONV_DIM), (1, tc), lambda j: (0, j))],
                  [((L, CONV_DIM), F32, (L, tc), lambda j: (0, _perm_tile(j)))])


def _mpre_bwd(name, pb, cw, cb, dact):
    L = pb.shape[0]
    tc = 128
    col = lambda j: (0, j)
    return _vcall(name, lambda x, w, b, da: _conv_bwd(x, w, da * _silu_grad(_conv(x, w, b))), (CONV_DIM // tc,),
                  [(pb, (L, tc), lambda j: (0, 16 + j)), (cw, (5, tc), col), (cb.reshape(1, CONV_DIM), (1, tc), col),
                   (dact, (L, tc), lambda j: (0, _perm_tile(j)))],
                  [((L, CONV_DIM), BF, (L, tc), col), ((5, CONV_DIM), F32, (5, tc), col),
                   ((1, CONV_DIM), F32, (1, tc), col)])


def _softplus(x):
    return jnp.maximum(x, 0.0) + jnp.log(1.0 + jnp.exp(-jnp.abs(x)))


def _dt_fwd(name, pb, dtb, alog):
    L = pb.shape[0]
    tb = _pick(L, (1024, ROWS))

    def fn(x, bias, al):
        dt = _softplus(x + bias)
        return dt, dt * (-jnp.exp(al))

    row = lambda i: (i, 0)
    zero = lambda i: (0, 0)
    return _vcall(name, fn, (L // tb,),
                  [(pb, (tb, 128), lambda i: (i, 48)), (dtb, (1, 128), zero), (alog, (1, 128), zero)],
                  [((L, 128), F32, (tb, 128), row), ((L, 128), F32, (tb, 128), row)])


def _dt_bwd(name, pb, dtb, alog, ddt_f, dla_f, ddt_b, dla_b):
    L = pb.shape[0]
    tb = _pick(L, (1024, ROWS))

    def fn(x, bias, al, a1, b1, a2, b2):
        ddt = jnp.sum(a1, axis=0) + jnp.sum(a2, axis=0)
        dla = jnp.sum(b1, axis=0) + jnp.sum(b2, axis=0)
        z = x + bias
        dt = _softplus(z)
        a = -jnp.exp(al)
        dz = (ddt + dla * a) * jax.nn.sigmoid(z)
        return dz, jnp.sum(dz, axis=0, keepdims=True), jnp.sum(dla * dt, axis=0, keepdims=True) * a

    zero = lambda i: (0, 0)
    g3 = (ddt_f.shape[0], tb, 128)
    at3 = lambda i: (0, i, 0)
    return _vcall(name, fn, (L // tb,),
                  [(pb, (tb, 128), lambda i: (i, 48)), (dtb, (1, 128), zero), (alog, (1, 128), zero),
                   (ddt_f, g3, at3), (dla_f, g3, at3), (ddt_b, g3, at3), (dla_b, g3, at3)],
                  [((L, 128), BF, (tb, 128), lambda i: (i, 0)), ((1, 128), F32, (1, 128), zero),
                   ((1, 128), F32, (1, 128), zero)], acc={1: "all", 2: "all"})


def _split_dot(x, e, dims, pieces):
    hi = x.astype(BF)
    r1 = x - hi.astype(F32)
    mid = r1.astype(BF)
    y = _dot(hi, e, dims) + _dot(mid, e, dims)
    if pieces == 3:
        y = y + _dot((r1 - mid.astype(F32)).astype(BF), e, dims)
    return y


@functools.partial(jax.custom_vjp, nondiff_argnums=(2,))
def _spread(x, e, pieces):
    return _split_dot(x, e, ((1,), (0,)), pieces)


def _spread_fwd(x, e, pieces):
    return _split_dot(x, e, ((1,), (0,)), pieces), e


def _spread_bwd(pieces, e, ct):
    return _split_dot(ct, e, ((1,), (1,)), pieces), jnp.zeros_like(e)


_spread.defvjp(_spread_fwd, _spread_bwd)


def _ssd_chunk(xa, dt, la, hs, head0, reverse):
    C, P4, HD, N = SSD_CHUNK, SSM_HPG * SSM_HD, SSM_HD, SSM_N
    G = hs.shape[0]
    nh = SSM_HPG * G
    row = lax.broadcasted_iota(jnp.int32, (C, 1), 0)
    lane = lax.broadcasted_iota(jnp.int32, (1, 128), 1)
    eye = lax.broadcasted_iota(jnp.int32, (C, C), 0) == lax.broadcasted_iota(jnp.int32, (C, C), 1)
    tri = _tri(C, reverse)
    last = 0 if reverse else C - 1
    acum = _running_sum(C, reverse, la)
    atot = jnp.sum(jnp.where(row == last, acum, 0.0), axis=0, keepdims=True)
    src = lax.broadcasted_iota(jnp.int32, (128, 1), 0) - head0
    to_x = (src == lax.broadcasted_iota(jnp.int32, (1, nh * HD), 1) // HD).astype(BF)
    dt_x = _spread(dt, to_x, 2)
    ea_x = _spread(jnp.exp(acum), to_x, 2)
    dec_x = _spread(jnp.exp(atot - acum), to_x, 2)
    col_head = lax.broadcasted_iota(jnp.int32, (1, P4), 1) // HD
    row_head = lax.broadcasted_iota(jnp.int32, (P4, 1), 0) // HD
    ys, news = [], []
    for gi in range(G):
        xs = xa[:, 512 * gi:512 * gi + P4]
        bm = _bf(xa[:, 512 * gi + P4:512 * gi + P4 + N])
        cm = _bf(xa[:, 512 * gi + P4 + N:512 * (gi + 1)])
        gx = slice(P4 * gi, P4 * (gi + 1))
        cb = _dot(cm, bm, ((1,), (1,)))
        xd = xs * dt_x[:, gx]
        ms, xds, scale = [], [], 0.0
        for j in range(SSM_HPG):
            i = SSM_HPG * gi + j
            ac = jnp.sum(jnp.where(lane == head0 + i, acum, 0.0), axis=1, keepdims=True)
            ac_row = jnp.sum(jnp.where(eye, ac, 0.0), axis=0, keepdims=True)
            ms.append(_bf(cb * jnp.exp(jnp.where(tri, ac - ac_row, NEG))))
            xds.append(_bf(jnp.where(col_head == j, xd, 0.0)))
            a_i = jnp.sum(jnp.where(lane == head0 + i, atot, 0.0), axis=1, keepdims=True)
            scale = scale + jnp.where(row_head == j, jnp.exp(a_i), 0.0)
        y = _dot(jnp.concatenate(ms, axis=1), jnp.concatenate(xds, axis=0), ((1,), (0,)))
        y = y + _dot(cm, _bf(hs[gi]), ((1,), (1,))) * ea_x[:, gx]
        ys.append(y)
        news.append((scale * hs[gi] + _dot(_bf(xd * dec_x[:, gx]), bm, ((0,), (0,))))[None])
    return jnp.concatenate(ys, axis=1), jnp.concatenate(news, axis=0)


def _ssd_fwd(name, xact, dt, la, reverse):
    L = xact.shape[0]
    C = SSD_CHUNK
    nc = L // C
    cidx = (lambda i: nc - 1 - i) if reverse else (lambda i: i)
    base = SSM_HEADS if reverse else 0
    P4 = SSM_HPG * SSM_HD

    gb_n = SSM_GB

    def fn(xa, dtv, lav, h_ref):
        @pl.when(pl.program_id(1) == 0)
        def _():
            h_ref[...] = jnp.zeros_like(h_ref)

        h_all = h_ref[...]
        y, h_new = _ssd_chunk(xa, dtv, lav, h_all, base + SSM_HPG * gb_n * pl.program_id(0), reverse)
        h_ref[...] = h_new
        return y, h_all[None]

    return _vcall(name, fn, (SSM_GROUPS // gb_n, nc),
                  [(xact, (C, 512 * gb_n), lambda g, i: (cidx(i), g)), (dt, (C, 128), lambda g, i: (cidx(i), 0)),
                   (la, (C, 128), lambda g, i: (cidx(i), 0))],
                  [((L, D_INNER), F32, (C, P4 * gb_n), lambda g, i: (cidx(i), g)),
                   ((nc, SSM_GROUPS, P4, SSM_N), F32, (1, gb_n, P4, SSM_N), lambda g, i: (cidx(i), g, 0, 0))],
                  scratch=[pltpu.VMEM((gb_n, P4, SSM_N), F32)])


def _ssd_bwd(name, xact, dt, la, h_in, dy, reverse, prev_xs=None, prev_all=None, side=None):
    L = xact.shape[0]
    C = SSD_CHUNK
    nc = L // C
    cidx = (lambda i: i) if reverse else (lambda i: nc - 1 - i)
    base = SSM_HEADS if reverse else 0
    P4 = SSM_HPG * SSM_HD

    gb_n = SSM_GB

    def fn(xa, dtv, lav, hs, dyv, pv, dh_ref):
        @pl.when(pl.program_id(1) == 0)
        def _():
            dh_ref[...] = jnp.zeros_like(dh_ref)

        head0 = base + SSM_HPG * gb_n * pl.program_id(0)
        _, vjp = jax.vjp(lambda a, b, c, d: _ssd_chunk(a, b, c, d, head0, reverse), xa, dtv, lav, hs[0])
        dxa, ddt, dla, dh = vjp((dyv, dh_ref[...]))
        dh_ref[...] = dh
        if prev_all is not None:
            dxa = dxa + pv
        else:
            zeros = jnp.zeros((C, 2 * SSM_N), F32)
            dxa = dxa + jnp.concatenate([t for gb in range(gb_n) for t in (pv[:, P4 * gb:P4 * (gb + 1)], zeros)], axis=1)
        return dxa, ddt[None], dla[None]

    at = lambda g, i: (cidx(i), g)
    at0 = lambda g, i: (cidx(i), 0)
    pv = (prev_all, (C, 512 * gb_n), at) if prev_all is not None else (prev_xs, (C, P4 * gb_n), at)
    steps = SSM_GROUPS // gb_n
    return _vcall(name, fn, (steps, nc),
                  [(xact, (C, 512 * gb_n), at), (dt, (C, 128), at0), (la, (C, 128), at0),
                   (h_in, (1, gb_n, P4, SSM_N), lambda g, i: (cidx(i), g, 0, 0)), (dy, (C, P4 * gb_n), at), pv],
                  [((L, CONV_DIM), F32, (C, 512 * gb_n), at),
                   ((steps, L, 128), F32, (1, C, 128), lambda g, i: (g, cidx(i), 0)),
                   ((steps, L, 128), F32, (1, C, 128), lambda g, i: (g, cidx(i), 0))],
                  scratch=[pltpu.VMEM((gb_n, P4, SSM_N), F32)], side=side)


def _mpost(y_f, y_b, xs, z, dsk, nw):
    y = (y_f + y_b + xs * dsk) * (z * jax.nn.sigmoid(z))
    return _rms(y, nw)


def _mpost_fwd(name, y_f, y_b, xact, pb, dsk, nw):
    L = y_f.shape[0]
    tb = _pick(L, (GATE_ROWS, ROWS))
    blk = (tb, 256)
    at = lambda g, i: (i, g)
    par = lambda g, i: (0, g)
    return _vcall(name, _mpost, (SSM_GROUPS, L // tb),
                  [(y_f, blk, at), (y_b, blk, at), (xact, blk, lambda g, i: (i, 2 * g)), (pb, blk, at),
                   (dsk, (1, 256), par), (nw, (1, 256), par)],
                  [((L, D_INNER), BF, blk, at)])


def _mpost_bwd(name, y_f, y_b, xact, pb, dsk, nw, dy):
    L = y_f.shape[0]
    tb = _pick(L, (GATE_ROWS, ROWS))

    def fn(yf, yb, xs, z, dskv, nwv, dyv):
        _, vjp = jax.vjp(_mpost, yf, yb, xs, z, dskv, nwv)
        dyf, _, dxs, dz, ddsk, dnw = vjp(dyv)
        return dyf, dxs, dz, ddsk, dnw

    blk = (tb, 256)
    at = lambda g, i: (i, g)
    par = lambda g, i: (0, g)
    return _vcall(name, fn, (SSM_GROUPS, L // tb),
                  [(y_f, blk, at), (y_b, blk, at), (xact, blk, lambda g, i: (i, 2 * g)), (pb, blk, at),
                   (dsk, (1, 256), par), (nw, (1, 256), par), (dy, blk, at)],
                  [((L, D_INNER), F32, blk, at), ((L, D_INNER), F32, blk, at), ((L, D_INNER), BF, blk, at),
                   ((1, D_INNER), F32, (1, 256), par), ((1, D_INNER), F32, (1, 256), par)],
                  acc={3: "last", 4: "last"})


def _ffn_fwd(tag, h, nw, w_in, cw, cb, w_out):
    u = _rms_fwd(f"{tag}_norm", h, nw)
    pf = _mm(f"{tag}_in", u, w_in, "nn")
    yf = _glu_fwd(f"{tag}_glu", pf, cw, cb)
    return _mm(f"{tag}_out", yf, w_out, "nn", add=h), (u, pf, yf)


def _ffn_bwd(tag, h, nw, w_in, cw, cb, w_out, saved, dh, side=None):
    u, pf, yf = saved
    d_w_out = _mm(f"{tag}_dwout", yf, dh, "tn")
    dyf = _mm(f"{tag}_dy", dh, w_out, "nt")
    dgate, dval, dcw, dcb = _glu_bwd(f"{tag}_dglu", pf, cw, cb, dyf)
    dpf = jnp.concatenate([dgate, dval], axis=1)
    d_w_in = _mm(f"{tag}_dwin", u, dpf, "tn")
    got = ()
    if side is None:
        du = _mm(f"{tag}_du", dpf, w_in, "nt")
    else:
        (du,), got = _mm(f"{tag}_du", dpf, w_in, "nt", side=side)
    dh_in, dnw = _rms_bwd(f"{tag}_dnorm", du, h, nw, dh)
    return dh_in, dnw, d_w_in, dcw, dcb, d_w_out, got


def _sequence_grads(x, tgt, p, sh):
    g = {}
    lbl = p["a_lb_logits"]
    first, early, mid, last = ("a_in", "a_out"), ("f0_in",), ("b_in", "b_out"), ("f1_in", "f1_out", "f0_out")
    W = {}
    got = _exchange("w_first", _gather_side(first, (), sh, W))
    W.update(zip(first, got))
    got = _exchange("w_first_pass", _gather_side((), first, sh, W))
    W.update(zip(first, got))
    u1 = _rms_fwd("a_norm", x, p["norm1_w"][0])
    (pa,), got = _mm("a_in", u1, W["a_in"], "nn", side=_gather_side(early, (), sh, W))
    W.update(zip(early, got))
    (o_f, s_f), got = _gla_fwd("a_scan_f", pa, lbl, False, side=_gather_side(mid, early, sh, W))
    W.update(zip(mid + early, got))
    (o_b, s_b), got = _gla_fwd("a_scan_b", pa, lbl, True, side=_gather_side(last, mid, sh, W))
    W.update(zip(last + mid, got))
    (ya,), got = _hgout_fwd("a_gate", o_f, o_b, pa, p["a_norm_w"], side=_gather_side((), last, sh, W))
    W.update(zip(last, got))
    wb4 = W["b_in"].reshape(4, D, B_PROJ // 4)
    p = dict(p, a_w_in=W["a_in"], a_w_out=W["a_out"], b_w_out=W["b_out"], ffn_w_in=(W["f0_in"], W["f1_in"]),
             ffn_w_out=(W["f0_out"], W["f1_out"]),
             b_w_in=jnp.pad(jnp.concatenate([wb4[j] for j in range(4)], axis=1), ((0, 0), (0, B_PROJ_PAD - B_PROJ))))
    h1 = _mm("a_out", ya, p["a_w_out"], "nn", add=x)
    h2, ffn0 = _ffn_fwd("f0", h1, p["norm2_w"][0], p["ffn_w_in"][0], p["ffn_conv_w"][0], p["ffn_conv_b"][0], p["ffn_w_out"][0])
    u3 = _rms_fwd("b_norm", h2, p["norm1_w"][1])
    pb = _mm("b_in", u3, p["b_w_in"], "nn")
    xact = _mpre_fwd("b_conv", pb, p["b_conv_w"], p["b_conv_b"])
    dt, la = _dt_fwd("b_dt", pb, p["b_dt_bias"], p["b_a_log"])
    y_f, hs_f = _ssd_fwd("b_scan_f", xact, dt, la, False)
    y_b, hs_b = _ssd_fwd("b_scan_b", xact, dt, la, True)
    yb = _mpost_fwd("b_gate", y_f, y_b, xact, pb, p["b_d_skip"], p["b_norm_w"])
    h3 = _mm("b_out", yb, p["b_w_out"], "nn", add=h2)
    h4, ffn1 = _ffn_fwd("f1", h3, p["norm2_w"][1], p["ffn_w_in"][1], p["ffn_conv_w"][1], p["ffn_conv_b"][1], p["ffn_w_out"][1])
    loss, dh4, g["final_norm_w"] = _loss_head("head", h4, tgt, p["final_norm_w"])
    dh3, dn2_1, dwin1, dcw1, dcb1, dwout1, _ = _ffn_bwd("f1", h3, p["norm2_w"][1], p["ffn_w_in"][1], p["ffn_conv_w"][1],
                                                        p["ffn_conv_b"][1], p["ffn_w_out"][1], ffn1, dh4)
    G = {"f1_in": dwin1, "f1_out": dwout1}
    G["b_out"] = _mm("b_dwout", yb, dh3, "tn")
    wave1 = ("f1_in", "f1_out", "b_out")
    (dyb,), got = _mm("b_dy", dh3, p["b_w_out"], "nt", side=_pair_side(wave1, G))
    chip_sums = _pair_sums(wave1, G, got)
    dys, dxs, dz, g["b_d_skip"], g["b_norm_w"] = _mpost_bwd("b_dgate", y_f, y_b, xact, pb, p["b_d_skip"], p["b_norm_w"], dyb)
    dxa1, ddt_f, dla_f = _ssd_bwd("b_dscan_f", xact, dt, la, hs_f, dys, False, prev_xs=dxs)
    dxa, ddt_b, dla_b = _ssd_bwd("b_dscan_b", xact, dt, la, hs_b, dys, True, prev_all=dxa1)
    dxbc, g["b_conv_w"], g["b_conv_b"] = _mpre_bwd("b_dconv", pb, p["b_conv_w"], p["b_conv_b"], dxa)
    ddtr, g["b_dt_bias"], g["b_a_log"] = _dt_bwd("b_ddt", pb, p["b_dt_bias"], p["b_a_log"], ddt_f, dla_f, ddt_b, dla_b)
    dpb = jnp.concatenate([dz, dxbc, ddtr], axis=1)
    G["b_in"] = _mm("b_dwin", dpb, u3, "tn")
    du3 = _mm("b_du", dpb, p["b_w_in"], "nt")
    dh2, dn1_1 = _rms_bwd("b_dnorm", du3, h2, p["norm1_w"][1], dh3)
    dh1, dn2_0, G["f0_in"], dcw0, dcb0, G["f0_out"], got = _ffn_bwd("f0", h1, p["norm2_w"][0], p["ffn_w_in"][0], p["ffn_conv_w"][0],
                                                                   p["ffn_conv_b"][0], p["ffn_w_out"][0], ffn0, dh2,
                                                                   side=_pair_side(("b_in",), G))
    chip_sums.update(_pair_sums(("b_in",), G, got))
    wave3 = ("f0_in", "f0_out")
    (G["a_out"],), got = _mm("a_dwout", ya, dh1, "tn", side=_pair_side(wave3, G))
    chip_sums.update(_pair_sums(wave3, G, got))
    dya = _mm("a_dy", dh1, p["a_w_out"], "nt")
    do, dg, g["a_norm_w"] = _hgout_bwd("a_dgate", o_f, o_b, pa, p["a_norm_w"], dya)
    late = last + mid + early
    (dq1, df1, dv1, dl1), got = _gla_bwd("a_dscan_f", pa, lbl, s_f, do, False, side=_chips_side(late, chip_sums))
    shards = {u: _chip_sum(f"gl_sum_{u}", _GGEO[u], chip_sums[u], r) for u, r in zip(late, got)}
    (dq, df2, dv, dl2), got = _gla_bwd("a_dscan_b", pa, lbl, s_b, do, True, prev=(dq1, dv1), side=_halves_side(late, shards))
    shards = dict(zip(late, got))
    dpa = jnp.concatenate([dq, df1, df2, dv, dg], axis=1)
    G["a_in"] = _mm("a_dwin", u1, dpa, "tn")
    chip_sums = _pair_sums(first, G, _exchange("ga_pair", _pair_side(first, G)))
    (du1,), got = _mm("a_du", dpa, p["a_w_in"], "nt", side=_chips_side(first, chip_sums))
    mine = {u: _chip_sum(f"ga_sum_{u}", _GGEO[u], chip_sums[u], r) for u, r in zip(first, got)}
    (dx, dn1_0), got = _rms_bwd("a_dnorm", du1, x, p["norm1_w"][0], dh1, side=_halves_side(first, mine))
    shards.update(zip(first, got))
    g["a_lb_logits"] = (dl1, dl2)
    g["norm1_w"] = (dn1_0, dn1_1)
    g["norm2_w"] = (dn2_0, dn2_1)
    g["ffn_conv_w"] = (dcw0, dcw1)
    g["ffn_conv_b"] = (dcb0, dcb1)
    return loss, dx, g, shards


def _here():
    return lax.axis_index("x"), lax.axis_index("y"), lax.axis_index("c")


def _allgather8(name, src):
    blk = src.shape

    def body(x_ref, out_ref, send_sems, recv_sems, local_sem):
        x, y, c = _here()
        me, sibling = (x, y, c), (x, y, 1 - c)
        chips = [(1 - x, y), (x, 1 - y), (1 - x, 1 - y)]
        own = x_ref

        def slot(px, py, pc):
            return out_ref.at[4 * px + 2 * py + pc]

        def copy(k, block, to, from_own=False):
            return pltpu.make_async_remote_copy(
                src_ref=own if from_own else slot(*block), dst_ref=slot(*block),
                send_sem=send_sems.at[k], recv_sem=recv_sems.at[k], device_id=to, device_id_type=MESH)

        mine = pltpu.make_async_copy(own, slot(*me), local_sem)
        mine.start()
        first = [copy(0, me, sibling, from_own=True)]
        first += [copy(1 + j, me, (*chip, c), from_own=True) for j, chip in enumerate(chips)]
        for cp in first:
            cp.start()
        passed = [copy(4 + j, (*chip, c), sibling) for j, chip in enumerate(chips)]
        for j, chip in enumerate(chips):
            copy(1 + j, (*chip, c), me).wait_recv()
            passed[j].start()
        copy(0, sibling, me).wait_recv()
        for j, chip in enumerate(chips):
            copy(4 + j, (*chip, 1 - c), me).wait_recv()
        for cp in first + passed:
            cp.wait_send()
        mine.wait()

    return pl.pallas_call(
        body, name=name,
        out_shape=jax.ShapeDtypeStruct((8,) + tuple(blk), src.dtype),
        in_specs=[pl.BlockSpec(memory_space=pl.ANY)],
        out_specs=pl.BlockSpec(memory_space=pl.ANY),
        scratch_shapes=[pltpu.SemaphoreType.DMA((7,)), pltpu.SemaphoreType.DMA((7,)), pltpu.SemaphoreType.DMA],
    )(src)


def _exchange(name, side):
    n_i, n_o = len(side.ins), len(side.outs)

    def body(*refs):
        copies = side.copies(refs[:n_i], refs[n_i:n_i + n_o], *refs[n_i + n_o:])
        for cp in copies:
            cp.start()
        for cp in copies:
            cp.wait()

    return pl.pallas_call(
        body, name=name,
        out_shape=[jax.ShapeDtypeStruct(s, dt) for s, dt in side.outs],
        in_specs=[pl.BlockSpec(memory_space=pl.ANY)] * n_i,
        out_specs=[pl.BlockSpec(memory_space=pl.ANY)] * n_o,
        scratch_shapes=side.sems(),
        input_output_aliases=dict(side.alias),
    )(*side.ins)


_WGEO = {"a_in": ("col", 1024, 1280), "a_out": ("row", 256, 1024), "b_in": ("row", 1024, 1552), "b_out": ("row", 512, 1024),
         "f0_in": ("col", 1024, 1408), "f1_in": ("col", 1024, 1408), "f0_out": ("row", 704, 1024), "f1_out": ("row", 704, 1024)}
_GGEO = dict(_WGEO, b_in=("row", 1552, 1024))


def _full_shape(geo):
    kind, r, cw = geo
    return (r, 4 * cw) if kind == "col" else (4 * r, cw)


def _times(i, step):
    return i * step if isinstance(i, int) else pl.multiple_of(i * step, step & -step)


def _win(ref, geo, j, h):
    kind, r, cw = geo
    hr = r // 2
    if kind == "col":
        return ref.at[pl.ds(_times(h, hr), hr), pl.ds(_times(j, cw), cw)]
    return ref.at[pl.ds(_times(2 * j + h, hr), hr), :]


def _half(ref, geo, h):
    hr = geo[1] // 2
    return ref.at[pl.ds(_times(h, hr), hr), :]


def _gather_side(first, second, sh, full):
    n1 = len(first)

    def plan(ins, outs):
        x, y, c = _here()
        m = 2 * x + y
        remote, local = [], []
        for u, src, dst_full in zip(first, ins[:n1], outs[:n1]):
            mine, dst = _half(src, _WGEO[u], c), _win(dst_full, _WGEO[u], m, c)
            local.append((mine, dst))
            remote.append((mine, dst, (x, y, 1 - c)))
            for k in (1, 2, 3):
                t = (m + k) % 4
                remote.append((mine, dst, (t // 2, t % 2, c)))
        for u, buf in zip(second, outs[n1:]):
            for k in (1, 2, 3):
                w_ = _win(buf, _WGEO[u], (m + k) % 4, c)
                remote.append((w_, w_, (x, y, 1 - c)))
        return remote, local

    return _Side([sh[u] for u in first] + [full[u] for u in second],
                 [(_full_shape(_WGEO[u]), BF) for u in first + second], plan, 4 * n1 + 3 * len(second), n1,
                 alias={n1 + i: n1 + i for i in range(len(second))})


def _pair_side(units, G):
    def plan(ins, outs):
        x, y, c = _here()
        return [(_win(gr, _GGEO[u], j, 1 - c), got.at[j], (x, y, 1 - c))
                for u, gr, got in zip(units, ins, outs) for j in range(4)], []

    return _Side([G[u] for u in units], [((4, _GGEO[u][1] // 2, _GGEO[u][2]), F32) for u in units], plan, 4 * len(units), 0)


def _pair_sums(units, G, gots):
    out = {}
    for u, got in zip(units, gots):
        blk = got.shape[1:]
        at = (lambda j: (lax.axis_index("c"), j)) if _GGEO[u][0] == "col" else (lambda j: (2 * j + lax.axis_index("c"), 0))
        slab = lambda j: (j, 0, 0)
        out[u] = _vcall(f"g_pair_sum_{u}", lambda a, b: (a + b[0])[None], (4,),
                        [(G[u], blk, at), (got, (1,) + blk, slab)], [(got.shape, BF, (1,) + blk, slab)])
    return out


def _chips_side(units, chip_sums):
    def plan(ins, outs):
        x, y, c = _here()
        m = 2 * x + y
        remote = []
        for s, got in zip(ins, outs):
            for k in (1, 2, 3):
                t = (m + k) % 4
                remote.append((s.at[t], got.at[k - 1], (t // 2, t % 2, c)))
        return remote, []

    return _Side([chip_sums[u] for u in units], [((3,) + chip_sums[u].shape[1:], BF) for u in units], plan,
                 3 * len(units), 0)


def _chip_sum(name, geo, chip_sums, got):
    _, r, cw = geo
    blk = (r // 2, cw)
    return _vcall(name, lambda a, b: ((a[0].astype(F32) + b[0].astype(F32)) + b[1].astype(F32)) + b[2].astype(F32), (1,),
                  [(chip_sums, (1,) + blk, lambda i: (2 * lax.axis_index("x") + lax.axis_index("y"), 0, 0)),
                   (got, (3,) + blk, lambda i: (0, 0, 0))],
                  [((r, cw), F32, blk, lambda i: (lax.axis_index("c"), 0))])


def _halves_side(units, shards):
    def plan(ins, outs):
        x, y, c = _here()
        return [(_half(o, _GGEO[u], c), _half(o, _GGEO[u], c), (x, y, 1 - c)) for u, o in zip(units, outs)], []

    return _Side([shards[u] for u in units], [(shards[u].shape, F32) for u in units], plan, len(units), 0,
                 alias={i: i for i in range(len(units))})


def _adam(name, w, g, m, v):
    rows, cols = w.shape
    tb = _pick(rows, (256, 128, 64, 8))

    def fn(wv, gv, mv, vv):
        m2 = ADAM_B1 * mv + (1.0 - ADAM_B1) * gv
        v2 = ADAM_B2 * vv + (1.0 - ADAM_B2) * jnp.square(gv)
        m_hat = m2 / (1.0 - ADAM_B1 ** ADAM_STEP)
        v_hat = v2 / (1.0 - ADAM_B2 ** ADAM_STEP)
        return -ADAM_LR * (m_hat / (jnp.sqrt(v_hat) + ADAM_EPS) + ADAM_WD * wv), m2, v2

    at = lambda i: (i, 0)
    return _vcall(name, fn, (rows // tb,), [(a, (tb, cols), at) for a in (w, g, m, v)],
                  [((rows, cols), F32, (tb, cols), at)] * 3)


def _pack(arrays, width, row_multiple, dtype):
    parts, offs, at = [], [], 0
    for a in arrays:
        flat = a.reshape(-1).astype(dtype)
        rows = -(-flat.shape[0] // (width * row_multiple)) * row_multiple
        parts.append(jnp.pad(flat, (0, rows * width - flat.shape[0])).reshape(rows, width))
        offs.append(at)
        at += rows
    return jnp.concatenate(parts, axis=0), offs


def _unpack(flat, shapes, offs):
    out = []
    for shp, at in zip(shapes, offs):
        n = 1
        for s in shp:
            n *= s
        rows = -(-n // flat.shape[1])
        out.append(flat[at:at + rows].reshape(-1)[:n].reshape(shp))
    return out


_BIG = ("a_w_in", "a_w_out", "b_w_in", "b_w_out", "ffn_w_in", "ffn_w_out")
_SMALL_SPLIT = ("b_conv_w", "b_conv_b", "b_norm_w", "ffn_conv_w")
_SMALL = ("norm1_w", "norm2_w", "a_lb_logits", "a_norm_w", "b_conv_w", "b_conv_b", "b_dt_bias", "b_a_log", "b_d_skip",
          "b_norm_w", "ffn_conv_w", "ffn_conv_b", "final_norm_w")
_ORDER = ("norm1_w", "norm2_w", "a_w_in", "a_lb_logits", "a_norm_w", "a_w_out", "b_w_in", "b_conv_w", "b_conv_b", "b_dt_bias",
          "b_a_log", "b_d_skip", "b_norm_w", "b_w_out", "ffn_w_in", "ffn_conv_w", "ffn_conv_b", "ffn_w_out", "final_norm_w")


def kernel(x, norm1_w, norm2_w, a_w_in, a_lb_logits, a_norm_w, a_w_out, b_w_in, b_conv_w, b_conv_b, b_dt_bias, b_a_log, b_d_skip, b_norm_w, b_w_out, ffn_w_in, ffn_conv_w, ffn_conv_b, ffn_w_out, final_norm_w, loss_target, m_norm1_w, m_norm2_w, m_a_w_in, m_a_lb_logits, m_a_norm_w, m_a_w_out, m_b_w_in, m_b_conv_w, m_b_conv_b, m_b_dt_bias, m_b_a_log, m_b_d_skip, m_b_norm_w, m_b_w_out, m_ffn_w_in, m_ffn_conv_w, m_ffn_conv_b, m_ffn_w_out, m_final_norm_w, v_norm1_w, v_norm2_w, v_a_w_in, v_a_lb_logits, v_a_norm_w, v_a_w_out, v_b_w_in, v_b_conv_w, v_b_conv_b, v_b_dt_bias, v_b_a_log, v_b_d_skip, v_b_norm_w, v_b_w_out, v_ffn_w_in, v_ffn_conv_w, v_ffn_conv_b, v_ffn_w_out, v_final_norm_w):
    w = dict(norm1_w=norm1_w, norm2_w=norm2_w, a_w_in=a_w_in, a_lb_logits=a_lb_logits, a_norm_w=a_norm_w, a_w_out=a_w_out,
             b_w_in=b_w_in, b_conv_w=b_conv_w, b_conv_b=b_conv_b, b_dt_bias=b_dt_bias, b_a_log=b_a_log, b_d_skip=b_d_skip,
             b_norm_w=b_norm_w, b_w_out=b_w_out, ffn_w_in=ffn_w_in, ffn_conv_w=ffn_conv_w, ffn_conv_b=ffn_conv_b,
             ffn_w_out=ffn_w_out, final_norm_w=final_norm_w)
    mom = dict(norm1_w=m_norm1_w, norm2_w=m_norm2_w, a_w_in=m_a_w_in, a_lb_logits=m_a_lb_logits, a_norm_w=m_a_norm_w,
               a_w_out=m_a_w_out, b_w_in=m_b_w_in, b_conv_w=m_b_conv_w, b_conv_b=m_b_conv_b, b_dt_bias=m_b_dt_bias,
               b_a_log=m_b_a_log, b_d_skip=m_b_d_skip, b_norm_w=m_b_norm_w, b_w_out=m_b_w_out, ffn_w_in=m_ffn_w_in,
               ffn_conv_w=m_ffn_conv_w, ffn_conv_b=m_ffn_conv_b, ffn_w_out=m_ffn_w_out, final_norm_w=m_final_norm_w)
    var = dict(norm1_w=v_norm1_w, norm2_w=v_norm2_w, a_w_in=v_a_w_in, a_lb_logits=v_a_lb_logits, a_norm_w=v_a_norm_w,
               a_w_out=v_a_w_out, b_w_in=v_b_w_in, b_conv_w=v_b_conv_w, b_conv_b=v_b_conv_b, b_dt_bias=v_b_dt_bias,
               b_a_log=v_b_a_log, b_d_skip=v_b_d_skip, b_norm_w=v_b_norm_w, b_w_out=v_b_w_out, ffn_w_in=v_ffn_w_in,
               ffn_conv_w=v_ffn_conv_w, ffn_conv_b=v_ffn_conv_b, ffn_w_out=v_ffn_w_out, final_norm_w=v_final_norm_w)
    chip = 2 * lax.axis_index("x") + lax.axis_index("y")

    sh = {"a_in": a_w_in[0], "a_out": a_w_out[0], "b_in": b_w_in[0], "b_out": b_w_out[0], "f0_in": ffn_w_in[0],
          "f1_in": ffn_w_in[1], "f0_out": ffn_w_out[0], "f1_out": ffn_w_out[1]}
    sh = {u: a.astype(BF) for u, a in sh.items()}
    small_shapes = [w[n].shape for n in _SMALL_SPLIT]
    spack, small_offs = _pack([w[n] for n in _SMALL_SPLIT], 128, 8, F32)
    sall = _allgather8("s_gather", spack)
    sshards = [_unpack(sall[2 * j], small_shapes, small_offs) for j in range(4)]
    sfull = {n: jnp.concatenate([sshards[j][i] for j in range(4)], axis=-1) for i, n in enumerate(_SMALL_SPLIT)}

    p = dict(
        norm1_w=norm1_w, norm2_w=norm2_w, a_lb_logits=a_lb_logits, a_norm_w=a_norm_w[0], final_norm_w=final_norm_w,
        b_conv_w=sfull["b_conv_w"][0], b_conv_b=sfull["b_conv_b"][0], b_norm_w=sfull["b_norm_w"],
        ffn_conv_w=sfull["ffn_conv_w"], ffn_conv_b=ffn_conv_b,
        b_dt_bias=jnp.pad(b_dt_bias.reshape(1, 2 * SSM_HEADS), ((0, 0), (0, 128 - 2 * SSM_HEADS))),
        b_a_log=jnp.pad(b_a_log.reshape(1, 2 * SSM_HEADS), ((0, 0), (0, 128 - 2 * SSM_HEADS))),
        b_d_skip=jnp.repeat(b_d_skip[0], SSM_HD)[None],
    )

    loss_row, dx, g, gs_ = _sequence_grads(x[0], loss_target[0], p, sh)
    grads = {"a_w_in": gs_["a_in"][None], "a_w_out": gs_["a_out"][None], "b_w_in": gs_["b_in"].T[None],
             "b_w_out": gs_["b_out"][None], "ffn_w_in": jnp.stack([gs_["f0_in"], gs_["f1_in"]]),
             "ffn_w_out": jnp.stack([gs_["f0_out"], gs_["f1_out"]])}

    gsmall = {
        "norm1_w": jnp.concatenate(g["norm1_w"], axis=0), "norm2_w": jnp.concatenate(g["norm2_w"], axis=0),
        "a_lb_logits": jnp.stack(g["a_lb_logits"]), "a_norm_w": g["a_norm_w"], "b_conv_w": g["b_conv_w"],
        "b_conv_b": g["b_conv_b"], "b_dt_bias": g["b_dt_bias"], "b_a_log": g["b_a_log"], "b_d_skip": g["b_d_skip"],
        "b_norm_w": g["b_norm_w"], "ffn_conv_w": jnp.stack(g["ffn_conv_w"]),
        "ffn_conv_b": jnp.concatenate(g["ffn_conv_b"], axis=0), "final_norm_w": g["final_norm_w"],
    }
    pieces = [gsmall[n] for n in _SMALL] + [loss_row]
    piece_shapes = [a.shape for a in pieces]
    gspack, gs_offs = _pack(pieces, 128, 8, F32)
    rows = gspack.shape[0]
    gsall = _allgather8("gs_gather", gspack)

    def sum8(a):
        r = a[0]
        for i in range(1, 8):
            r = r + a[i]
        return r

    gssum = _vcall("gs_sum", sum8, (1,), [(gsall, (8, rows, 128), lambda i: (0, 0, 0))],
                   [((rows, 128), F32, (rows, 128), lambda i: (0, 0))])
    gs = dict(zip(_SMALL + ("loss",), _unpack(gssum, piece_shapes, gs_offs)))
    loss = gs["loss"][0, 0]
    lb2 = gs["a_lb_logits"]
    small_grads = {
        "norm1_w": gs["norm1_w"], "norm2_w": gs["norm2_w"], "a_lb_logits": lb2[0] + lb2[1], "a_norm_w": gs["a_norm_w"],
        "b_dt_bias": gs["b_dt_bias"][:, :2 * SSM_HEADS].reshape(1, 2, SSM_HEADS),
        "b_a_log": gs["b_a_log"][:, :2 * SSM_HEADS].reshape(1, 2, SSM_HEADS),
        "b_d_skip": gs["b_d_skip"].reshape(1, SSM_HEADS, SSM_HD).sum(axis=-1),
        "ffn_conv_b": gs["ffn_conv_b"], "final_norm_w": gs["final_norm_w"][0],
        "b_conv_w": gs["b_conv_w"][None], "b_conv_b": gs["b_conv_b"], "b_norm_w": gs["b_norm_w"], "ffn_conv_w": gs["ffn_conv_w"],
    }
    for n in _SMALL_SPLIT:
        width = w[n].shape[-1]
        small_grads[n] = lax.dynamic_slice_in_dim(small_grads[n], chip * width, width, axis=small_grads[n].ndim - 1)
    grads.update(small_grads)

    delta, new_m, new_v = {}, {}, {}
    for n in _BIG:
        shp = w[n].shape
        two_d = (shp[0] * shp[1], shp[2])
        d_, m_, v_ = _adam(f"adam_{n}", w[n].reshape(two_d), grads[n].reshape(two_d), mom[n].reshape(two_d), var[n].reshape(two_d))
        delta[n], new_m[n], new_v[n] = d_.reshape(shp), m_.reshape(shp), v_.reshape(shp)
    s_shapes = [w[n].shape for n in _SMALL]
    packs = [_pack([src[n] for n in _SMALL], 128, 8, F32) for src in (w, grads, mom, var)]
    outs = _adam("adam_small", *[pk[0] for pk in packs])
    for res, dst in zip(outs, (delta, new_m, new_v)):
        dst.update(dict(zip(_SMALL, _unpack(res, s_shapes, packs[0][1]))))

    return (loss, dx[None], *[grads[n] for n in _ORDER], *[delta[n] for n in _ORDER],
            *[new_m[n] for n in _ORDER], *[new_v[n] for n in _ORDER])
```

```python
import functools

import jax
import jax.numpy as jnp
from jax import lax
from jax.experimental import pallas as pl
from jax.experimental.pallas import tpu as pltpu

F32, BF = jnp.float32, jnp.bfloat16

D = 1024
EPS = 1e-6
HG_HEADS, HG_HD, HG_CHUNK, HG_SUB = 8, 128, 64, 8
HG_HB = 8
HG_CPS = 1
SSM_GB = 8
D_INNER, SSM_HEADS, SSM_HD, SSM_GROUPS, SSM_HPG, SSM_N, SSD_CHUNK = 2048, 32, 64, 8, 4, 128, 128
CONV_DIM = D_INNER + 2 * SSM_GROUPS * SSM_N
B_PROJ = 2 * D_INNER + 2 * SSM_GROUPS * SSM_N + 2 * SSM_HEADS
B_PROJ_PAD = 6272
D_FF = 2816
NEG = -1e30
ROWS = 512
GATE_ROWS = 1024
VMEM_LIMIT = 56 * 1024 * 1024

ADAM_LR, ADAM_B1, ADAM_B2, ADAM_EPS, ADAM_WD, ADAM_STEP = 0.001, 0.9, 0.999, 1e-08, 0.01, 10

MESH = pl.DeviceIdType.MESH


def _pick(n, cands):
    for c in cands:
        if n % c == 0:
            return c
    return n


class _Side:
    def __init__(self, ins, outs, plan, n_remote, n_local, alias=None):
        self.ins, self.outs, self.plan, self.n_remote, self.n_local = list(ins), list(outs), plan, n_remote, n_local
        self.alias = alias or {}

    def copies(self, in_refs, out_refs, send_sems, recv_sems, local_sems):
        remote, local = self.plan(in_refs, out_refs)
        cps = [pltpu.make_async_copy(s, d, local_sems.at[i]) for i, (s, d) in enumerate(local)]
        cps += [pltpu.make_async_remote_copy(src_ref=s, dst_ref=d, send_sem=send_sems.at[i], recv_sem=recv_sems.at[i],
                                             device_id=dev, device_id_type=MESH)
                for i, (s, d, dev) in enumerate(remote)]
        return cps

    def sems(self):
        return [pltpu.SemaphoreType.DMA((self.n_remote,)), pltpu.SemaphoreType.DMA((self.n_remote,)),
                pltpu.SemaphoreType.DMA((max(self.n_local, 1),))]


def _vcall(name, fn, grid, ins, outs, acc=None, scratch=(), side=None):
    acc = acc or {}
    n_in, n_out, nd = len(ins), len(outs), len(grid)
    n_sin = len(side.ins) if side else 0
    n_sout = len(side.outs) if side else 0
    n_scr = len(scratch)

    def body(*refs):
        in_refs, refs = refs[:n_in], refs[n_in:]
        sin_refs, refs = refs[:n_sin], refs[n_sin:]
        out_refs, refs = refs[:n_out], refs[n_out:]
        sout_refs, refs = refs[:n_sout], refs[n_sout:]
        scr, sems = refs[:n_scr], refs[n_scr:]
        if side:
            at_first, at_last = None, None
            for ax in range(nd):
                f, l = pl.program_id(ax) == 0, pl.program_id(ax) == grid[ax] - 1
                at_first = f if at_first is None else jnp.logical_and(at_first, f)
                at_last = l if at_last is None else jnp.logical_and(at_last, l)

            @pl.when(at_first)
            def _():
                for cp in side.copies(sin_refs, sout_refs, *sems):
                    cp.start()

        res = fn(*[r[...] for r in in_refs], *scr)
        if not isinstance(res, (tuple, list)):
            res = (res,)
        if side:
            @pl.when(at_last)
            def _():
                for cp in side.copies(sin_refs, sout_refs, *sems):
                    cp.wait()
        for j, (o_ref, r) in enumerate(zip(out_refs, res)):
            mode = acc.get(j)
            if mode is None:
                o_ref[...] = r.astype(o_ref.dtype)
                continue
            first = pl.program_id(nd - 1) == 0
            if mode == "all":
                for ax in range(nd - 1):
                    first = jnp.logical_and(first, pl.program_id(ax) == 0)

            @pl.when(first)
            def _():
                o_ref[...] = r.astype(o_ref.dtype)

            @pl.when(jnp.logical_not(first))
            def _():
                o_ref[...] += r.astype(o_ref.dtype)

    hbm = pl.BlockSpec(memory_space=pl.ANY)
    in_specs = [pl.BlockSpec(bs, im) for _, bs, im in ins] + [hbm] * n_sin
    out_specs = [pl.BlockSpec(bs, im) for _, _, bs, im in outs] + [hbm] * n_sout
    params = pltpu.CompilerParams(dimension_semantics=("arbitrary",) * nd, vmem_limit_bytes=VMEM_LIMIT)
    out_shape = [jax.ShapeDtypeStruct(s, dt) for s, dt, _, _ in outs]
    operands = [a for a, _, _ in ins]
    scratch = list(scratch)
    aliases = {}
    if side:
        out_shape += [jax.ShapeDtypeStruct(s, dt) for s, dt in side.outs]
        operands += side.ins
        scratch += side.sems()
        aliases = {n_in + i: n_out + o for i, o in side.alias.items()}
    out = pl.pallas_call(body, name=name, grid=grid, in_specs=in_specs, out_specs=out_specs, out_shape=out_shape,
                         scratch_shapes=scratch, compiler_params=params, input_output_aliases=aliases)(*operands)
    if side:
        return tuple(out[:n_out]), tuple(out[n_out:])
    return out[0] if n_out == 1 else out


def _mm(name, a, b, kind, out_dtype=F32, add=None, side=None):
    if kind == "tn":
        m, k = a.shape
        _, n = b.shape
        tm = _pick(m, (1024, 512, 256))
        tk = _pick(k, (1024, 1408, 896, 512, 256, 128))
        tn = _pick(n, (1024, 1408, 896, 512, 256, 128))

        def fn(av, bv):
            return lax.dot_general(av.astype(BF), bv.astype(BF), (((0,), (0,)), ((), ())),
                                   preferred_element_type=F32)

        return _vcall(name, fn, (k // tk, n // tn, m // tm),
                      [(a, (tm, tk), lambda i, j, s: (s, i)), (b, (tm, tn), lambda i, j, s: (s, j))],
                      [((k, n), F32, (tk, tn), lambda i, j, s: (i, j))], acc={0: "last"}, side=side)
    m, k = a.shape
    n = b.shape[1] if kind == "nn" else b.shape[0]
    long_k = k > 4096
    tm = _pick(m, (1024, 512, 256))
    tn = _pick(n, (512, 896, 256, 128)) if long_k else _pick(n, (1024, 1408, 896, 512, 256, 128))
    dims = (((1,), (0,)), ((), ())) if kind == "nn" else (((1,), (1,)), ((), ()))

    def fn(av, bv, *rest):
        r = lax.dot_general(av.astype(BF), bv.astype(BF), dims, preferred_element_type=F32)
        return r + rest[0] if rest else r

    ins = [(a, (tm, k), lambda i, j: (i, 0)),
           (b, (k, tn), lambda i, j: (0, j)) if kind == "nn" else (b, (tn, k), lambda i, j: (j, 0))]
    if add is not None:
        ins.append((add, (tm, tn), lambda i, j: (i, j)))
    return _vcall(name, fn, (m // tm, n // tn), ins, [((m, n), out_dtype, (tm, tn), lambda i, j: (i, j))], side=side)


def _rms(h, w):
    return h * lax.rsqrt(jnp.mean(h * h, axis=-1, keepdims=True) + EPS) * w


def _rms_fwd(name, h, w):
    L = h.shape[0]
    tb = _pick(L, (ROWS,))
    return _vcall(name, _rms, (L // tb,),
                  [(h, (tb, D), lambda i: (i, 0)), (w.reshape(1, D), (1, D), lambda i: (0, 0))],
                  [((L, D), BF, (tb, D), lambda i: (i, 0))])


def _rms_bwd(name, du, h, w, dh_next, side=None):
    L = h.shape[0]
    tb = _pick(L, (ROWS,))

    def fn(duv, hv, wv, dnv):
        _, vjp = jax.vjp(_rms, hv, wv)
        dh, dw = vjp(duv)
        return dh + dnv, dw

    row = lambda i: (i, 0)
    return _vcall(name, fn, (L // tb,),
                  [(du, (tb, D), row), (h, (tb, D), row), (w.reshape(1, D), (1, D), lambda i: (0, 0)),
                   (dh_next, (tb, D), row)],
                  [((L, D), F32, (tb, D), row), ((1, D), F32, (1, D), lambda i: (0, 0))], acc={1: "all"}, side=side)


def _loss_head(name, h, tgt, w):
    L = h.shape[0]
    tb = _pick(L, (ROWS,))

    def lossf(hv, wv, tv):
        err = _rms(hv, wv) - tv
        return 0.5 * jnp.sum(err * err) * (1.0 / D)

    def fn(hv, wv, tv):
        val, vjp = jax.vjp(lambda a, b: lossf(a, b, tv), hv, wv)
        dh, dw = vjp(jnp.ones((), F32))
        return jnp.full((1, 128), val, F32), dh, dw

    row = lambda i: (i, 0)
    zero = lambda i: (0, 0)
    return _vcall(name, fn, (L // tb,),
                  [(h, (tb, D), row), (w.reshape(1, D), (1, D), zero), (tgt, (tb, D), row)],
                  [((1, 128), F32, (1, 128), zero), ((L, D), F32, (tb, D), row), ((1, D), F32, (1, D), zero)],
                  acc={0: "all", 2: "all"})


def _bf(x):
    return x.astype(BF)


def _dot(a, b, dims):
    return lax.dot_general(a, b, (dims, ((), ())), preferred_element_type=F32)


def _tri(n, reverse):
    r = lax.broadcasted_iota(jnp.int32, (n, n), 0)
    c = lax.broadcasted_iota(jnp.int32, (n, n), 1)
    return (r <= c) if reverse else (r >= c)


def _tri_matmul(n, reverse, x):
    hi = x.astype(BF)
    r1 = x - hi.astype(F32)
    mid = r1.astype(BF)
    lo = (r1 - mid.astype(F32)).astype(BF)
    y = _dot(_tri(n, reverse).astype(BF), jnp.concatenate([hi, mid, lo], axis=1), ((1,), (0,)))
    w = x.shape[1]
    return (y[:, :w] + y[:, w:2 * w]) + y[:, 2 * w:]


@functools.partial(jax.custom_vjp, nondiff_argnums=(0, 1))
def _running_sum(n, reverse, x):
    return _tri_matmul(n, reverse, x)


def _running_sum_fwd(n, reverse, x):
    return _tri_matmul(n, reverse, x), None


def _running_sum_bwd(n, reverse, _, ct):
    return (_tri_matmul(n, not reverse, ct),)


_running_sum.defvjp(_running_sum_fwd, _running_sum_bwd)


def _gla_chunk(q_raw, f_raw, v, lb3, S, reverse):
    C, SB, HD = HG_CHUNK, HG_SUB, HG_HD
    H = S.shape[0]
    heads = [slice(HD * h, HD * (h + 1)) for h in range(H)]
    row3 = lax.broadcasted_iota(jnp.int32, (3, 1), 0)
    e = jnp.exp(lb3 - jnp.max(lb3, axis=0, keepdims=True))
    lb = jnp.sum(jnp.where(row3 == 0, e, 0.0), axis=0, keepdims=True) / jnp.sum(e, axis=0, keepdims=True)
    q = q_raw * jax.nn.sigmoid(q_raw)
    f = lb + (1.0 - lb) * jax.nn.sigmoid(f_raw)
    g = jnp.log(f)
    k = 1.0 - f
    b = _running_sum(C, reverse, g)
    row = lax.broadcasted_iota(jnp.int32, (C, 1), 0)
    vb = _bf(v)

    def rowof(x, t):
        return jnp.sum(jnp.where(row == t, x, 0.0), axis=0, keepdims=True)

    qe = _bf(q * jnp.exp(b))
    o = [_dot(qe[:, hs], _bf(S[h]), ((1,), (0,))) for h, hs in enumerate(heads)]
    att = [[] for _ in range(H)]
    for i in range(C // SB):
        lo = SB * i
        if (not reverse and i == 0) or (reverse and i == C // SB - 1):
            for h in range(H):
                att[h].append(jnp.zeros((SB, C), F32))
            continue
        first = lo + SB - 1 if reverse else lo
        r = rowof(b, first) - rowof(g, first)
        before = (row >= lo + SB) if reverse else (row < lo)
        qi = q[lo:lo + SB] * jnp.exp(b[lo:lo + SB] - r)
        kk = _bf(k * jnp.exp(jnp.where(before, r - b, NEG)))
        for h, hs in enumerate(heads):
            att[h].append(_dot(_bf(qi[:, hs]), kk[:, hs], ((1,), (1,))))
    o = [o[h] + _dot(_bf(jnp.concatenate(att[h], axis=0)), vb[:, hs], ((1,), (0,))) for h, hs in enumerate(heads)]
    s_i = lax.broadcasted_iota(jnp.int32, (SB, SB, HD), 0)
    t_i = lax.broadcasted_iota(jnp.int32, (SB, SB, HD), 1)
    pair = (t_i <= s_i) if reverse else (t_i >= s_i)
    shp = (SB, SB, HD)
    diag = [[] for _ in range(H)]
    for i in range(C // SB):
        rows = slice(SB * i, SB * (i + 1))
        for h, hs in enumerate(heads):
            qb, kb, bb = q[rows, hs], k[rows, hs], b[rows, hs]
            dif = lax.broadcast_in_dim(bb, shp, (1, 2)) - lax.broadcast_in_dim(bb, shp, (0, 2))
            w = lax.broadcast_in_dim(qb, shp, (1, 2)) * jnp.exp(jnp.where(pair, dif, NEG)) * lax.broadcast_in_dim(kb, shp, (0, 2))
            d = jnp.sum(w, axis=2, keepdims=True)
            diag[h].append(jnp.sum(d * lax.broadcast_in_dim(v[rows, hs], shp, (0, 2)), axis=0))
    o = jnp.concatenate([o[h] + jnp.concatenate(diag[h], axis=0) for h in range(H)], axis=1)
    btot = rowof(b, 0 if reverse else C - 1)
    kd = _bf(k * jnp.exp(btot - b))
    eye = lax.broadcasted_iota(jnp.int32, (HD, HD), 0) == lax.broadcasted_iota(jnp.int32, (HD, HD), 1)
    s_new = []
    for h, hs in enumerate(heads):
        btot_col = jnp.sum(jnp.where(eye, btot[:, hs], 0.0), axis=1, keepdims=True)
        s_new.append((jnp.exp(btot_col) * S[h] + _dot(kd[:, hs], vb[:, hs], ((0,), (0,))))[None])
    return o, jnp.concatenate(s_new, axis=0)


def _gla_fwd(name, pa, lbl, reverse, side=None):
    L = pa.shape[0]
    C, cs = HG_CHUNK, HG_CPS
    nc, ns = L // C, L // (C * cs)
    cidx = (lambda i: ns - 1 - i) if reverse else (lambda i: i)
    sec = 2 if reverse else 1
    hb_n, nh = HG_HB, HG_HEADS // HG_HB

    def fn(qr, fr, v, lb3, s_ref):
        @pl.when(pl.program_id(1) == 0)
        def _():
            s_ref[...] = jnp.zeros_like(s_ref)

        s = s_ref[...]
        outs, olds = [None] * cs, [None] * cs
        for c in (reversed(range(cs)) if reverse else range(cs)):
            rows = slice(C * c, C * (c + 1))
            olds[c] = s[None]
            outs[c], s = _gla_chunk(qr[rows], fr[rows], v[rows], lb3, s, reverse)
        s_ref[...] = s
        return jnp.concatenate(outs, axis=0), jnp.concatenate(olds, axis=0)

    blk = (C * cs, HG_HD * hb_n)
    return _vcall(name, fn, (nh, ns),
                  [(pa, blk, lambda h, i: (cidx(i), h)), (pa, blk, lambda h, i: (cidx(i), sec * nh + h)),
                   (pa, blk, lambda h, i: (cidx(i), 3 * nh + h)), (lbl, (3, HG_HD * hb_n), lambda h, i: (0, h))],
                  [((L, D), F32, blk, lambda h, i: (cidx(i), h)),
                   ((nc, HG_HEADS, HG_HD, HG_HD), F32, (cs, hb_n, HG_HD, HG_HD), lambda h, i: (cidx(i), h, 0, 0))],
                  scratch=[pltpu.VMEM((hb_n, HG_HD, HG_HD), F32)], side=side)


def _gla_bwd(name, pa, lbl, s_in, do, reverse, prev=None, side=None):
    L = pa.shape[0]
    C, cs = HG_CHUNK, HG_CPS
    ns = L // (C * cs)
    cidx = (lambda i: i) if reverse else (lambda i: ns - 1 - i)
    sec = 2 if reverse else 1
    n_prev = 0 if prev is None else 2
    hb_n, nh = HG_HB, HG_HEADS // HG_HB

    def fn(qr, fr, v, lb3, s, dov, *rest):
        ds_ref = rest[n_prev]

        @pl.when(pl.program_id(1) == 0)
        def _():
            ds_ref[...] = jnp.zeros_like(ds_ref)

        ds = ds_ref[...]
        parts, dlb = [None] * cs, None
        for c in (range(cs) if reverse else reversed(range(cs))):
            rows = slice(C * c, C * (c + 1))
            _, vjp = jax.vjp(lambda *a: _gla_chunk(*a, reverse), qr[rows], fr[rows], v[rows], lb3, s[c])
            dq, df, dv, dl, ds = vjp((dov[rows], ds))
            if n_prev:
                dq, dv = dq + rest[0][rows], dv + rest[1][rows]
            parts[c] = (dq, df, dv)
            dlb = dl if dlb is None else dlb + dl
        ds_ref[...] = ds
        return tuple(jnp.concatenate([p_[j] for p_ in parts], axis=0) for j in range(3)) + (dlb,)

    blk = (C * cs, HG_HD * hb_n)
    at = lambda h, i: (cidx(i), h)
    ins = [(pa, blk, at), (pa, blk, lambda h, i: (cidx(i), sec * nh + h)), (pa, blk, lambda h, i: (cidx(i), 3 * nh + h)),
           (lbl, (3, HG_HD * hb_n), lambda h, i: (0, h)),
           (s_in, (cs, hb_n, HG_HD, HG_HD), lambda h, i: (cidx(i), h, 0, 0)), (do, blk, at)]
    if prev is not None:
        ins += [(prev[0], blk, at), (prev[1], blk, at)]
    sum_dt = F32 if prev is None else BF
    return _vcall(name, fn, (nh, ns), ins,
                  [((L, D), sum_dt, blk, at), ((L, D), BF, blk, at), ((L, D), sum_dt, blk, at),
                   ((3, D), F32, (3, HG_HD * hb_n), lambda h, i: (0, h))],
                  acc={3: "last"}, scratch=[pltpu.VMEM((hb_n, HG_HD, HG_HD), F32)], side=side)


def _hgout(o_f, o_b, g, nw):
    o = o_f + o_b
    return _rms(o, nw) * (g * jax.nn.sigmoid(g))


def _hgout_fwd(name, o_f, o_b, pa, nw, side=None):
    L = o_f.shape[0]
    tb = _pick(L, (GATE_ROWS, ROWS))
    blk = (tb, HG_HD)
    at = lambda h, i: (i, h)
    return _vcall(name, _hgout, (HG_HEADS, L // tb),
                  [(o_f, blk, at), (o_b, blk, at), (pa, blk, lambda h, i: (i, 32 + h)),
                   (nw.reshape(1, HG_HD), (1, HG_HD), lambda h, i: (0, 0))],
                  [((L, D), BF, blk, at)], side=side)


def _hgout_bwd(name, o_f, o_b, pa, nw, dy, side=None):
    L = o_f.shape[0]
    tb = _pick(L, (GATE_ROWS, ROWS))

    def fn(ofv, obv, gv, nwv, dyv):
        _, vjp = jax.vjp(_hgout, ofv, obv, gv, nwv)
        do, _, dg, dnw = vjp(dyv)
        return do, dg, dnw

    blk = (tb, HG_HD)
    at = lambda h, i: (i, h)
    zero = lambda h, i: (0, 0)
    return _vcall(name, fn, (HG_HEADS, L // tb),
                  [(o_f, blk, at), (o_b, blk, at), (pa, blk, lambda h, i: (i, 32 + h)),
                   (nw.reshape(1, HG_HD), (1, HG_HD), zero), (dy, blk, at)],
                  [((L, D), F32, blk, at), ((L, D), BF, blk, at), ((1, HG_HD), F32, (1, HG_HD), zero)],
                  acc={2: "all"}, side=side)


def _shift(x, s):
    if s == 0:
        return x
    n = x.shape[0]
    t = lax.broadcasted_iota(jnp.int32, (n, 1), 0)
    if s > 0:
        return jnp.where(t >= s, pltpu.roll(x, s, 0), 0.0)
    return jnp.where(t < n + s, pltpu.roll(x, n + s, 0), 0.0)


def _conv(x, w, b):
    kk = w.shape[0]
    p = (kk - 1) // 2
    y = b
    for j in range(kk):
        y = y + w[j:j + 1] * _shift(x, p - j)
    return y


def _conv_bwd(x, w, dc):
    kk = w.shape[0]
    p = (kk - 1) // 2
    dx = None
    dws = []
    for j in range(kk):
        t = w[j:j + 1] * _shift(dc, j - p)
        dx = t if dx is None else dx + t
        dws.append(jnp.sum(dc * _shift(x, p - j), axis=0, keepdims=True))
    rows = lax.broadcasted_iota(jnp.int32, (kk, 1), 0)
    dw = None
    for j in range(kk):
        t = jnp.where(rows == j, dws[j], 0.0)
        dw = t if dw is None else dw + t
    return dx, dw, jnp.sum(dc, axis=0, keepdims=True)


def _silu(c):
    return c * jax.nn.sigmoid(c)


def _silu_grad(c):
    s = jax.nn.sigmoid(c)
    return s * (1.0 + c * (1.0 - s))


def _glu_fwd(name, pf, cw, cb):
    L = pf.shape[0]
    tc = 128
    nt = D_FF // tc
    return _vcall(name, lambda gate, val, w, b: _silu(_conv(gate, w, b)) * val, (nt,),
                  [(pf, (L, tc), lambda j: (0, j)), (pf, (L, tc), lambda j: (0, nt + j)),
                   (cw, (3, tc), lambda j: (0, j)), (cb.reshape(1, D_FF), (1, tc), lambda j: (0, j))],
                  [((L, D_FF), BF, (L, tc), lambda j: (0, j))])


def _glu_bwd(name, pf, cw, cb, dy, side=None):
    L = pf.shape[0]
    tc = 128
    nt = D_FF // tc

    def fn(gate, val, w, b, dyv):
        c = _conv(gate, w, b)
        dgate, dw, db = _conv_bwd(gate, w, dyv * val * _silu_grad(c))
        return dgate, dyv * _silu(c), dw, db

    col = lambda j: (0, j)
    return _vcall(name, fn, (nt,),
                  [(pf, (L, tc), col), (pf, (L, tc), lambda j: (0, nt + j)), (cw, (3, tc), col),
                   (cb.reshape(1, D_FF), (1, tc), col), (dy, (L, tc), col)],
                  [((L, D_FF), BF, (L, tc), col), ((L, D_FF), BF, (L, tc), col),
                   ((3, D_FF), F32, (3, tc), col), ((1, D_FF), F32, (1, tc), col)], side=side)


def _perm_tile(j):
    return jnp.where(j < 16, 4 * (j // 2) + j % 2, jnp.where(j < 24, 4 * (j - 16) + 2, 4 * (j - 24) + 3))


def _mpre_fwd(name, pb, cw, cb):
    L = pb.shape[0]
    tc = 128
    return _vcall(name, lambda x, w, b: _silu(_conv(x, w, b)), (CONV_DIM // tc,),
                  [(pb, (L, tc), lambda j: (0, 16 + j)), (cw, (5, tc), lambda j: (0, j)),
                   (cb.reshape(1, CONV_DIM), (1, tc), lambda j: (0, j))],
                  [((L, CONV_DIM), F32, (L, tc), lambda j: (0, _perm_tile(j)))])


def _mpre_bwd(name, pb, cw, cb, dact):
    L = pb.shape[0]
    tc = 128
    col = lambda j: (0, j)
    return _vcall(name, lambda x, w, b, da: _conv_bwd(x, w, da * _silu_grad(_conv(x, w, b))), (CONV_DIM // tc,),
                  [(pb, (L, tc), lambda j: (0, 16 + j)), (cw, (5, tc), col), (cb.reshape(1, CONV_DIM), (1, tc), col),
                   (dact, (L, tc), lambda j: (0, _perm_tile(j)))],
                  [((L, CONV_DIM), BF, (L, tc), col), ((5, CONV_DIM), F32, (5, tc), col),
                   ((1, CONV_DIM), F32, (1, tc), col)])


def _softplus(x):
    return jnp.maximum(x, 0.0) + jnp.log(1.0 + jnp.exp(-jnp.abs(x)))


def _dt_fwd(name, pb, dtb, alog):
    L = pb.shape[0]
    tb = _pick(L, (1024, ROWS))

    def fn(x, bias, al):
        dt = _softplus(x + bias)
        return dt, dt * (-jnp.exp(al))

    row = lambda i: (i, 0)
    zero = lambda i: (0, 0)
    return _vcall(name, fn, (L // tb,),
                  [(pb, (tb, 128), lambda i: (i, 48)), (dtb, (1, 128), zero), (alog, (1, 128), zero)],
                  [((L, 128), F32, (tb, 128), row), ((L, 128), F32, (tb, 128), row)])


def _dt_bwd(name, pb, dtb, alog, ddt_f, dla_f, ddt_b, dla_b):
    L = pb.shape[0]
    tb = _pick(L, (1024, ROWS))

    def fn(x, bias, al, a1, b1, a2, b2):
        ddt = jnp.sum(a1, axis=0) + jnp.sum(a2, axis=0)
        dla = jnp.sum(b1, axis=0) + jnp.sum(b2, axis=0)
        z = x + bias
        dt = _softplus(z)
        a = -jnp.exp(al)
        dz = (ddt + dla * a) * jax.nn.sigmoid(z)
        return dz, jnp.sum(dz, axis=0, keepdims=True), jnp.sum(dla * dt, axis=0, keepdims=True) * a

    zero = lambda i: (0, 0)
    g3 = (ddt_f.shape[0], tb, 128)
    at3 = lambda i: (0, i, 0)
    return _vcall(name, fn, (L // tb,),
                  [(pb, (tb, 128), lambda i: (i, 48)), (dtb, (1, 128), zero), (alog, (1, 128), zero),
                   (ddt_f, g3, at3), (dla_f, g3, at3), (ddt_b, g3, at3), (dla_b, g3, at3)],
                  [((L, 128), BF, (tb, 128), lambda i: (i, 0)), ((1, 128), F32, (1, 128), zero),
                   ((1, 128), F32, (1, 128), zero)], acc={1: "all", 2: "all"})


def _split_dot(x, e, dims, pieces):
    hi = x.astype(BF)
    r1 = x - hi.astype(F32)
    mid = r1.astype(BF)
    y = _dot(hi, e, dims) + _dot(mid, e, dims)
    if pieces == 3:
        y = y + _dot((r1 - mid.astype(F32)).astype(BF), e, dims)
    return y


@functools.partial(jax.custom_vjp, nondiff_argnums=(2,))
def _spread(x, e, pieces):
    return _split_dot(x, e, ((1,), (0,)), pieces)


def _spread_fwd(x, e, pieces):
    return _split_dot(x, e, ((1,), (0,)), pieces), e


def _spread_bwd(pieces, e, ct):
    return _split_dot(ct, e, ((1,), (1,)), pieces), jnp.zeros_like(e)


_spread.defvjp(_spread_fwd, _spread_bwd)


def _ssd_chunk(xa, dt, la, hs, head0, reverse):
    C, P4, HD, N = SSD_CHUNK, SSM_HPG * SSM_HD, SSM_HD, SSM_N
    G = hs.shape[0]
    nh = SSM_HPG * G
    row = lax.broadcasted_iota(jnp.int32, (C, 1), 0)
    lane = lax.broadcasted_iota(jnp.int32, (1, 128), 1)
    eye = lax.broadcasted_iota(jnp.int32, (C, C), 0) == lax.broadcasted_iota(jnp.int32, (C, C), 1)
    tri = _tri(C, reverse)
    last = 0 if reverse else C - 1
    acum = _running_sum(C, reverse, la)
    atot = jnp.sum(jnp.where(row == last, acum, 0.0), axis=0, keepdims=True)
    src = lax.broadcasted_iota(jnp.int32, (128, 1), 0) - head0
    to_x = (src == lax.broadcasted_iota(jnp.int32, (1, nh * HD), 1) // HD).astype(BF)
    dt_x = _spread(dt, to_x, 2)
    ea_x = _spread(jnp.exp(acum), to_x, 2)
    dec_x = _spread(jnp.exp(atot - acum), to_x, 2)
    col_head = lax.broadcasted_iota(jnp.int32, (1, P4), 1) // HD
    row_head = lax.broadcasted_iota(jnp.int32, (P4, 1), 0) // HD
    ys, news = [], []
    for gi in range(G):
        xs = xa[:, 512 * gi:512 * gi + P4]
        bm = _bf(xa[:, 512 * gi + P4:512 * gi + P4 + N])
        cm = _bf(xa[:, 512 * gi + P4 + N:512 * (gi + 1)])
        gx = slice(P4 * gi, P4 * (gi + 1))
        cb = _dot(cm, bm, ((1,), (1,)))
        xd = xs * dt_x[:, gx]
        ms, xds, scale = [], [], 0.0
        for j in range(SSM_HPG):
            i = SSM_HPG * gi + j
            ac = jnp.sum(jnp.where(lane == head0 + i, acum, 0.0), axis=1, keepdims=True)
            ac_row = jnp.sum(jnp.where(eye, ac, 0.0), axis=0, keepdims=True)
            ms.append(_bf(cb * jnp.exp(jnp.where(tri, ac - ac_row, NEG))))
            xds.append(_bf(jnp.where(col_head == j, xd, 0.0)))
            a_i = jnp.sum(jnp.where(lane == head0 + i, atot, 0.0), axis=1, keepdims=True)
            scale = scale + jnp.where(row_head == j, jnp.exp(a_i), 0.0)
        y = _dot(jnp.concatenate(ms, axis=1), jnp.concatenate(xds, axis=0), ((1,), (0,)))
        y = y + _dot(cm, _bf(hs[gi]), ((1,), (1,))) * ea_x[:, gx]
        ys.append(y)
        news.append((scale * hs[gi] + _dot(_bf(xd * dec_x[:, gx]), bm, ((0,), (0,))))[None])
    return jnp.concatenate(ys, axis=1), jnp.concatenate(news, axis=0)


def _ssd_fwd(name, xact, dt, la, reverse):
    L = xact.shape[0]
    C = SSD_CHUNK
    nc = L // C
    cidx = (lambda i: nc - 1 - i) if reverse else (lambda i: i)
    base = SSM_HEADS if reverse else 0
    P4 = SSM_HPG * SSM_HD

    gb_n = SSM_GB

    def fn(xa, dtv, lav, h_ref):
        @pl.when(pl.program_id(1) == 0)
        def _():
            h_ref[...] = jnp.zeros_like(h_ref)

        h_all = h_ref[...]
        y, h_new = _ssd_chunk(xa, dtv, lav, h_all, base + SSM_HPG * gb_n * pl.program_id(0), reverse)
        h_ref[...] = h_new
        return y, h_all[None]

    return _vcall(name, fn, (SSM_GROUPS // gb_n, nc),
                  [(xact, (C, 512 * gb_n), lambda g, i: (cidx(i), g)), (dt, (C, 128), lambda g, i: (cidx(i), 0)),
                   (la, (C, 128), lambda g, i: (cidx(i), 0))],
                  [((L, D_INNER), F32, (C, P4 * gb_n), lambda g, i: (cidx(i), g)),
                   ((nc, SSM_GROUPS, P4, SSM_N), F32, (1, gb_n, P4, SSM_N), lambda g, i: (cidx(i), g, 0, 0))],
                  scratch=[pltpu.VMEM((gb_n, P4, SSM_N), F32)])


def _ssd_bwd(name, xact, dt, la, h_in, dy, reverse, prev_xs=None, prev_all=None, side=None):
    L = xact.shape[0]
    C = SSD_CHUNK
    nc = L // C
    cidx = (lambda i: i) if reverse else (lambda i: nc - 1 - i)
    base = SSM_HEADS if reverse else 0
    P4 = SSM_HPG * SSM_HD

    gb_n = SSM_GB

    def fn(xa, dtv, lav, hs, dyv, pv, dh_ref):
        @pl.when(pl.program_id(1) == 0)
        def _():
            dh_ref[...] = jnp.zeros_like(dh_ref)

        head0 = base + SSM_HPG * gb_n * pl.program_id(0)
        _, vjp = jax.vjp(lambda a, b, c, d: _ssd_chunk(a, b, c, d, head0, reverse), xa, dtv, lav, hs[0])
        dxa, ddt, dla, dh = vjp((dyv, dh_ref[...]))
        dh_ref[...] = dh
        if prev_all is not None:
            dxa = dxa + pv
        else:
            zeros = jnp.zeros((C, 2 * SSM_N), F32)
            dxa = dxa + jnp.concatenate([t for gb in range(gb_n) for t in (pv[:, P4 * gb:P4 * (gb + 1)], zeros)], axis=1)
        return dxa, ddt[None], dla[None]

    at = lambda g, i: (cidx(i), g)
    at0 = lambda g, i: (cidx(i), 0)
    pv = (prev_all, (C, 512 * gb_n), at) if prev_all is not None else (prev_xs, (C, P4 * gb_n), at)
    steps = SSM_GROUPS // gb_n
    return _vcall(name, fn, (steps, nc),
                  [(xact, (C, 512 * gb_n), at), (dt, (C, 128), at0), (la, (C, 128), at0),
                   (h_in, (1, gb_n, P4, SSM_N), lambda g, i: (cidx(i), g, 0, 0)), (dy, (C, P4 * gb_n), at), pv],
                  [((L, CONV_DIM), F32, (C, 512 * gb_n), at),
                   ((steps, L, 128), F32, (1, C, 128), lambda g, i: (g, cidx(i), 0)),
                   ((steps, L, 128), F32, (1, C, 128), lambda g, i: (g, cidx(i), 0))],
                  scratch=[pltpu.VMEM((gb_n, P4, SSM_N), F32)], side=side)


def _mpost(y_f, y_b, xs, z, dsk, nw):
    y = (y_f + y_b + xs * dsk) * (z * jax.nn.sigmoid(z))
    return _rms(y, nw)


def _mpost_fwd(name, y_f, y_b, xact, pb, dsk, nw):
    L = y_f.shape[0]
    tb = _pick(L, (GATE_ROWS, ROWS))
    blk = (tb, 256)
    at = lambda g, i: (i, g)
    par = lambda g, i: (0, g)
    return _vcall(name, _mpost, (SSM_GROUPS, L // tb),
                  [(y_f, blk, at), (y_b, blk, at), (xact, blk, lambda g, i: (i, 2 * g)), (pb, blk, at),
                   (dsk, (1, 256), par), (nw, (1, 256), par)],
                  [((L, D_INNER), BF, blk, at)])


def _mpost_bwd(name, y_f, y_b, xact, pb, dsk, nw, dy):
    L = y_f.shape[0]
    tb = _pick(L, (GATE_ROWS, ROWS))

    def fn(yf, yb, xs, z, dskv, nwv, dyv):
        _, vjp = jax.vjp(_mpost, yf, yb, xs, z, dskv, nwv)
        dyf, _, dxs, dz, ddsk, dnw = vjp(dyv)
        return dyf, dxs, dz, ddsk, dnw

    blk = (tb, 256)
    at = lambda g, i: (i, g)
    par = lambda g, i: (0, g)
    return _vcall(name, fn, (SSM_GROUPS, L // tb),
                  [(y_f, blk, at), (y_b, blk, at), (xact, blk, lambda g, i: (i, 2 * g)), (pb, blk, at),
                   (dsk, (1, 256), par), (nw, (1, 256), par), (dy, blk, at)],
                  [((L, D_INNER), F32, blk, at), ((L, D_INNER), F32, blk, at), ((L, D_INNER), BF, blk, at),
                   ((1, D_INNER), F32, (1, 256), par), ((1, D_INNER), F32, (1, 256), par)],
                  acc={3: "last", 4: "last"})


def _ffn_fwd(tag, h, nw, w_in, cw, cb, w_out):
    u = _rms_fwd(f"{tag}_norm", h, nw)
    pf = _mm(f"{tag}_in", u, w_in, "nn")
    yf = _glu_fwd(f"{tag}_glu", pf, cw, cb)
    return _mm(f"{tag}_out", yf, w_out, "nn", add=h), (u, pf, yf)


def _ffn_bwd(tag, h, nw, w_in, cw, cb, w_out, saved, dh, side=None):
    u, pf, yf = saved
    d_w_out = _mm(f"{tag}_dwout", yf, dh, "tn")
    dyf = _mm(f"{tag}_dy", dh, w_out, "nt")
    dgate, dval, dcw, dcb = _glu_bwd(f"{tag}_dglu", pf, cw, cb, dyf)
    dpf = jnp.concatenate([dgate, dval], axis=1)
    d_w_in = _mm(f"{tag}_dwin", u, dpf, "tn")
    got = ()
    if side is None:
        du = _mm(f"{tag}_du", dpf, w_in, "nt")
    else:
        (du,), got = _mm(f"{tag}_du", dpf, w_in, "nt", side=side)
    dh_in, dnw = _rms_bwd(f"{tag}_dnorm", du, h, nw, dh)
    return dh_in, dnw, d_w_in, dcw, dcb, d_w_out, got


def _sequence_grads(x, tgt, p, sh):
    g = {}
    lbl = p["a_lb_logits"]
    first, early, mid, last = ("a_in", "a_out"), ("f0_in",), ("b_in", "b_out"), ("f1_in", "f1_out", "f0_out")
    W = {}
    got = _exchange("w_first", _gather_side(first, (), sh, W))
    W.update(zip(first, got))
    got = _exchange("w_first_pass", _gather_side((), first, sh, W))
    W.update(zip(first, got))
    u1 = _rms_fwd("a_norm", x, p["norm1_w"][0])
    (pa,), got = _mm("a_in", u1, W["a_in"], "nn", side=_gather_side(early, (), sh, W))
    W.update(zip(early, got))
    (o_f, s_f), got = _gla_fwd("a_scan_f", pa, lbl, False, side=_gather_side(mid, early, sh, W))
    W.update(zip(mid + early, got))
    (o_b, s_b), got = _gla_fwd("a_scan_b", pa, lbl, True, side=_gather_side(last, mid, sh, W))
    W.update(zip(last + mid, got))
    (ya,), got = _hgout_fwd("a_gate", o_f, o_b, pa, p["a_norm_w"], side=_gather_side((), last, sh, W))
    W.update(zip(last, got))
    wb4 = W["b_in"].reshape(4, D, B_PROJ // 4)
    p = dict(p, a_w_in=W["a_in"], a_w_out=W["a_out"], b_w_out=W["b_out"], ffn_w_in=(W["f0_in"], W["f1_in"]),
             ffn_w_out=(W["f0_out"], W["f1_out"]),
             b_w_in=jnp.pad(jnp.concatenate([wb4[j] for j in range(4)], axis=1), ((0, 0), (0, B_PROJ_PAD - B_PROJ))))
    h1 = _mm("a_out", ya, p["a_w_out"], "nn", add=x)
    h2, ffn0 = _ffn_fwd("f0", h1, p["norm2_w"][0], p["ffn_w_in"][0], p["ffn_conv_w"][0], p["ffn_conv_b"][0], p["ffn_w_out"][0])
    u3 = _rms_fwd("b_norm", h2, p["norm1_w"][1])
    pb = _mm("b_in", u3, p["b_w_in"], "nn")
    xact = _mpre_fwd("b_conv", pb, p["b_conv_w"], p["b_conv_b"])
    dt, la = _dt_fwd("b_dt", pb, p["b_dt_bias"], p["b_a_log"])
    y_f, hs_f = _ssd_fwd("b_scan_f", xact, dt, la, False)
    y_b, hs_b = _ssd_fwd("b_scan_b", xact, dt, la, True)
    yb = _mpost_fwd("b_gate", y_f, y_b, xact, pb, p["b_d_skip"], p["b_norm_w"])
    h3 = _mm("b_out", yb, p["b_w_out"], "nn", add=h2)
    h4, ffn1 = _ffn_fwd("f1", h3, p["norm2_w"][1], p["ffn_w_in"][1], p["ffn_conv_w"][1], p["ffn_conv_b"][1], p["ffn_w_out"][1])
    loss, dh4, g["final_norm_w"] = _loss_head("head", h4, tgt, p["final_norm_w"])
    dh3, dn2_1, dwin1, dcw1, dcb1, dwout1, _ = _ffn_bwd("f1", h3, p["norm2_w"][1], p["ffn_w_in"][1], p["ffn_conv_w"][1],
                                                        p["ffn_conv_b"][1], p["ffn_w_out"][1], ffn1, dh4)
    G = {"f1_in": dwin1, "f1_out": dwout1}
    G["b_out"] = _mm("b_dwout", yb, dh3, "tn")
    wave1 = ("f1_in", "f1_out", "b_out")
    (dyb,), got = _mm("b_dy", dh3, p["b_w_out"], "nt", side=_pair_side(wave1, G))
    chip_sums = _pair_sums(wave1, G, got)
    dys, dxs, dz, g["b_d_skip"], g["b_norm_w"] = _mpost_bwd("b_dgate", y_f, y_b, xact, pb, p["b_d_skip"], p["b_norm_w"], dyb)
    dxa1, ddt_f, dla_f = _ssd_bwd("b_dscan_f", xact, dt, la, hs_f, dys, False, prev_xs=dxs)
    dxa, ddt_b, dla_b = _ssd_bwd("b_dscan_b", xact, dt, la, hs_b, dys, True, prev_all=dxa1)
    dxbc, g["b_conv_w"], g["b_conv_b"] = _mpre_bwd("b_dconv", pb, p["b_conv_w"], p["b_conv_b"], dxa)
    ddtr, g["b_dt_bias"], g["b_a_log"] = _dt_bwd("b_ddt", pb, p["b_dt_bias"], p["b_a_log"], ddt_f, dla_f, ddt_b, dla_b)
    dpb = jnp.concatenate([dz, dxbc, ddtr], axis=1)
    G["b_in"] = _mm("b_dwin", dpb, u3, "tn")
    du3 = _mm("b_du", dpb, p["b_w_in"], "nt")
    dh2, dn1_1 = _rms_bwd("b_dnorm", du3, h2, p["norm1_w"][1], dh3)
    dh1, dn2_0, G["f0_in"], dcw0, dcb0, G["f0_out"], got = _ffn_bwd("f0", h1, p["norm2_w"][0], p["ffn_w_in"][0], p["ffn_conv_w"][0],
                                                                   p["ffn_conv_b"][0], p["ffn_w_out"][0], ffn0, dh2,
                                                                   side=_pair_side(("b_in",), G))
    chip_sums.update(_pair_sums(("b_in",), G, got))
    wave3 = ("f0_in", "f0_out")
    (G["a_out"],), got = _mm("a_dwout", ya, dh1, "tn", side=_pair_side(wave3, G))
    chip_sums.update(_pair_sums(wave3, G, got))
    dya = _mm("a_dy", dh1, p["a_w_out"], "nt")
    do, dg, g["a_norm_w"] = _hgout_bwd("a_dgate", o_f, o_b, pa, p["a_norm_w"], dya)
    late = last + mid + early
    (dq1, df1, dv1, dl1), got = _gla_bwd("a_dscan_f", pa, lbl, s_f, do, False, side=_chips_side(late, chip_sums))
    shards = {u: _chip_sum(f"gl_sum_{u}", _GGEO[u], chip_sums[u], r) for u, r in zip(late, got)}
    (dq, df2, dv, dl2), got = _gla_bwd("a_dscan_b", pa, lbl, s_b, do, True, prev=(dq1, dv1), side=_halves_side(late, shards))
    shards = dict(zip(late, got))
    dpa = jnp.concatenate([dq, df1, df2, dv, dg], axis=1)
    G["a_in"] = _mm("a_dwin", u1, dpa, "tn")
    chip_sums = _pair_sums(first, G, _exchange("ga_pair", _pair_side(first, G)))
    (du1,), got = _mm("a_du", dpa, p["a_w_in"], "nt", side=_chips_side(first, chip_sums))
    mine = {u: _chip_sum(f"ga_sum_{u}", _GGEO[u], chip_sums[u], r) for u, r in zip(first, got)}
    (dx, dn1_0), got = _rms_bwd("a_dnorm", du1, x, p["norm1_w"][0], dh1, side=_halves_side(first, mine))
    shards.update(zip(first, got))
    g["a_lb_logits"] = (dl1, dl2)
    g["norm1_w"] = (dn1_0, dn1_1)
    g["norm2_w"] = (dn2_0, dn2_1)
    g["ffn_conv_w"] = (dcw0, dcw1)
    g["ffn_conv_b"] = (dcb0, dcb1)
    return loss, dx, g, shards


def _here():
    return lax.axis_index("x"), lax.axis_index("y"), lax.axis_index("c")


def _allgather8(name, src):
    blk = src.shape

    def body(x_ref, out_ref, send_sems, recv_sems, local_sem):
        x, y, c = _here()
        me, sibling = (x, y, c), (x, y, 1 - c)
        chips = [(1 - x, y), (x, 1 - y), (1 - x, 1 - y)]
        own = x_ref

        def slot(px, py, pc):
            return out_ref.at[4 * px + 2 * py + pc]

        def copy(k, block, to, from_own=False):
            return pltpu.make_async_remote_copy(
                src_ref=own if from_own else slot(*block), dst_ref=slot(*block),
                send_sem=send_sems.at[k], recv_sem=recv_sems.at[k], device_id=to, device_id_type=MESH)

        mine = pltpu.make_async_copy(own, slot(*me), local_sem)
        mine.start()
        first = [copy(0, me, sibling, from_own=True)]
        first += [copy(1 + j, me, (*chip, c), from_own=True) for j, chip in enumerate(chips)]
        for cp in first:
            cp.start()
        passed = [copy(4 + j, (*chip, c), sibling) for j, chip in enumerate(chips)]
        for j, chip in enumerate(chips):
            copy(1 + j, (*chip, c), me).wait_recv()
            passed[j].start()
        copy(0, sibling, me).wait_recv()
        for j, chip in enumerate(chips):
            copy(4 + j, (*chip, 1 - c), me).wait_recv()
        for cp in first + passed:
            cp.wait_send()
        mine.wait()

    return pl.pallas_call(
        body, name=name,
        out_shape=jax.ShapeDtypeStruct((8,) + tuple(blk), src.dtype),
        in_specs=[pl.BlockSpec(memory_space=pl.ANY)],
        out_specs=pl.BlockSpec(memory_space=pl.ANY),
        scratch_shapes=[pltpu.SemaphoreType.DMA((7,)), pltpu.SemaphoreType.DMA((7,)), pltpu.SemaphoreType.DMA],
    )(src)


def _exchange(name, side):
    n_i, n_o = len(side.ins), len(side.outs)

    def body(*refs):
        copies = side.copies(refs[:n_i], refs[n_i:n_i + n_o], *refs[n_i + n_o:])
        for cp in copies:
            cp.start()
        for cp in copies:
            cp.wait()

    return pl.pallas_call(
        body, name=name,
        out_shape=[jax.ShapeDtypeStruct(s, dt) for s, dt in side.outs],
        in_specs=[pl.BlockSpec(memory_space=pl.ANY)] * n_i,
        out_specs=[pl.BlockSpec(memory_space=pl.ANY)] * n_o,
        scratch_shapes=side.sems(),
        input_output_aliases=dict(side.alias),
    )(*side.ins)


_WGEO = {"a_in": ("col", 1024, 1280), "a_out": ("row", 256, 1024), "b_in": ("row", 1024, 1552), "b_out": ("row", 512, 1024),
         "f0_in": ("col", 1024, 1408), "f1_in": ("col", 1024, 1408), "f0_out": ("row", 704, 1024), "f1_out": ("row", 704, 1024)}
_GGEO = dict(_WGEO, b_in=("row", 1552, 1024))


def _full_shape(geo):
    kind, r, cw = geo
    return (r, 4 * cw) if kind == "col" else (4 * r, cw)


def _times(i, step):
    return i * step if isinstance(i, int) else pl.multiple_of(i * step, step & -step)


def _win(ref, geo, j, h):
    kind, r, cw = geo
    hr = r // 2
    if kind == "col":
        return ref.at[pl.ds(_times(h, hr), hr), pl.ds(_times(j, cw), cw)]
    return ref.at[pl.ds(_times(2 * j + h, hr), hr), :]


def _half(ref, geo, h):
    hr = geo[1] // 2
    return ref.at[pl.ds(_times(h, hr), hr), :]


def _gather_side(first, second, sh, full):
    n1 = len(first)

    def plan(ins, outs):
        x, y, c = _here()
        m = 2 * x + y
        remote, local = [], []
        for u, src, dst_full in zip(first, ins[:n1], outs[:n1]):
            mine, dst = _half(src, _WGEO[u], c), _win(dst_full, _WGEO[u], m, c)
            local.append((mine, dst))
            remote.append((mine, dst, (x, y, 1 - c)))
            for k in (1, 2, 3):
                t = (m + k) % 4
                remote.append((mine, dst, (t // 2, t % 2, c)))
        for u, buf in zip(second, outs[n1:]):
            for k in (1, 2, 3):
                w_ = _win(buf, _WGEO[u], (m + k) % 4, c)
                remote.append((w_, w_, (x, y, 1 - c)))
        return remote, local

    return _Side([sh[u] for u in first] + [full[u] for u in second],
                 [(_full_shape(_WGEO[u]), BF) for u in first + second], plan, 4 * n1 + 3 * len(second), n1,
                 alias={n1 + i: n1 + i for i in range(len(second))})


def _pair_side(units, G):
    def plan(ins, outs):
        x, y, c = _here()
        return [(_win(gr, _GGEO[u], j, 1 - c), got.at[j], (x, y, 1 - c))
                for u, gr, got in zip(units, ins, outs) for j in range(4)], []

    return _Side([G[u] for u in units], [((4, _GGEO[u][1] // 2, _GGEO[u][2]), F32) for u in units], plan, 4 * len(units), 0)


def _pair_sums(units, G, gots):
    out = {}
    for u, got in zip(units, gots):
        blk = got.shape[1:]
        at = (lambda j: (lax.axis_index("c"), j)) if _GGEO[u][0] == "col" else (lambda j: (2 * j + lax.axis_index("c"), 0))
        slab = lambda j: (j, 0, 0)
        out[u] = _vcall(f"g_pair_sum_{u}", lambda a, b: (a + b[0])[None], (4,),
                        [(G[u], blk, at), (got, (1,) + blk, slab)], [(got.shape, BF, (1,) + blk, slab)])
    return out


def _chips_side(units, chip_sums):
    def plan(ins, outs):
        x, y, c = _here()
        m = 2 * x + y
        remote = []
        for s, got in zip(ins, outs):
            for k in (1, 2, 3):
                t = (m + k) % 4
                remote.append((s.at[t], got.at[k - 1], (t // 2, t % 2, c)))
        return remote, []

    return _Side([chip_sums[u] for u in units], [((3,) + chip_sums[u].shape[1:], BF) for u in units], plan,
                 3 * len(units), 0)


def _chip_sum(name, geo, chip_sums, got):
    _, r, cw = geo
    blk = (r // 2, cw)
    return _vcall(name, lambda a, b: ((a[0].astype(F32) + b[0].astype(F32)) + b[1].astype(F32)) + b[2].astype(F32), (1,),
                  [(chip_sums, (1,) + blk, lambda i: (2 * lax.axis_index("x") + lax.axis_index("y"), 0, 0)),
                   (got, (3,) + blk, lambda i: (0, 0, 0))],
                  [((r, cw), F32, blk, lambda i: (lax.axis_index("c"), 0))])


def _halves_side(units, shards):
    def plan(ins, outs):
        x, y, c = _here()
        return [(_half(o, _GGEO[u], c), _half(o, _GGEO[u], c), (x, y, 1 - c)) for u, o in zip(units, outs)], []

    return _Side([shards[u] for u in units], [(shards[u].shape, F32) for u in units], plan, len(units), 0,
                 alias={i: i for i in range(len(units))})


def _adam(name, w, g, m, v):
    rows, cols = w.shape
    tb = _pick(rows, (256, 128, 64, 8))

    def fn(wv, gv, mv, vv):
        m2 = ADAM_B1 * mv + (1.0 - ADAM_B1) * gv
        v2 = ADAM_B2 * vv + (1.0 - ADAM_B2) * jnp.square(gv)
        m_hat = m2 / (1.0 - ADAM_B1 ** ADAM_STEP)
        v_hat = v2 / (1.0 - ADAM_B2 ** ADAM_STEP)
        return -ADAM_LR * (m_hat / (jnp.sqrt(v_hat) + ADAM_EPS) + ADAM_WD * wv), m2, v2

    at = lambda i: (i, 0)
    return _vcall(name, fn, (rows // tb,), [(a, (tb, cols), at) for a in (w, g, m, v)],
                  [((rows, cols), F32, (tb, cols), at)] * 3)


def _pack(arrays, width, row_multiple, dtype):
    parts, offs, at = [], [], 0
    for a in arrays:
        flat = a.reshape(-1).astype(dtype)
        rows = -(-flat.shape[0] // (width * row_multiple)) * row_multiple
        parts.append(jnp.pad(flat, (0, rows * width - flat.shape[0])).reshape(rows, width))
        offs.append(at)
        at += rows
    return jnp.concatenate(parts, axis=0), offs


def _unpack(flat, shapes, offs):
    out = []
    for shp, at in zip(shapes, offs):
        n = 1
        for s in shp:
            n *= s
        rows = -(-n // flat.shape[1])
        out.append(flat[at:at + rows].reshape(-1)[:n].reshape(shp))
    return out


_BIG = ("a_w_in", "a_w_out", "b_w_in", "b_w_out", "ffn_w_in", "ffn_w_out")
_SMALL_SPLIT = ("b_conv_w", "b_conv_b", "b_norm_w", "ffn_conv_w")
_SMALL = ("norm1_w", "norm2_w", "a_lb_logits", "a_norm_w", "b_conv_w", "b_conv_b", "b_dt_bias", "b_a_log", "b_d_skip",
          "b_norm_w", "ffn_conv_w", "ffn_conv_b", "final_norm_w")
_ORDER = ("norm1_w", "norm2_w", "a_w_in", "a_lb_logits", "a_norm_w", "a_w_out", "b_w_in", "b_conv_w", "b_conv_b", "b_dt_bias",
          "b_a_log", "b_d_skip", "b_norm_w", "b_w_out", "ffn_w_in", "ffn_conv_w", "ffn_conv_b", "ffn_w_out", "final_norm_w")


def kernel(x, norm1_w, norm2_w, a_w_in, a_lb_logits, a_norm_w, a_w_out, b_w_in, b_conv_w, b_conv_b, b_dt_bias, b_a_log, b_d_skip, b_norm_w, b_w_out, ffn_w_in, ffn_conv_w, ffn_conv_b, ffn_w_out, final_norm_w, loss_target, m_norm1_w, m_norm2_w, m_a_w_in, m_a_lb_logits, m_a_norm_w, m_a_w_out, m_b_w_in, m_b_conv_w, m_b_conv_b, m_b_dt_bias, m_b_a_log, m_b_d_skip, m_b_norm_w, m_b_w_out, m_ffn_w_in, m_ffn_conv_w, m_ffn_conv_b, m_ffn_w_out, m_final_norm_w, v_norm1_w, v_norm2_w, v_a_w_in, v_a_lb_logits, v_a_norm_w, v_a_w_out, v_b_w_in, v_b_conv_w, v_b_conv_b, v_b_dt_bias, v_b_a_log, v_b_d_skip, v_b_norm_w, v_b_w_out, v_ffn_w_in, v_ffn_conv_w, v_ffn_conv_b, v_ffn_w_out, v_final_norm_w):
    w = dict(norm1_w=norm1_w, norm2_w=norm2_w, a_w_in=a_w_in, a_lb_logits=a_lb_logits, a_norm_w=a_norm_w, a_w_out=a_w_out,
             b_w_in=b_w_in, b_conv_w=b_conv_w, b_conv_b=b_conv_b, b_dt_bias=b_dt_bias, b_a_log=b_a_log, b_d_skip=b_d_skip,
             b_norm_w=b_norm_w, b_w_out=b_w_out, ffn_w_in=ffn_w_in, ffn_conv_w=ffn_conv_w, ffn_conv_b=ffn_conv_b,
             ffn_w_out=ffn_w_out, final_norm_w=final_norm_w)
    mom = dict(norm1_w=m_norm1_w, norm2_w=m_norm2_w, a_w_in=m_a_w_in, a_lb_logits=m_a_lb_logits, a_norm_w=m_a_norm_w,
               a_w_out=m_a_w_out, b_w_in=m_b_w_in, b_conv_w=m_b_conv_w, b_conv_b=m_b_conv_b, b_dt_bias=m_b_dt_bias,
               b_a_log=m_b_a_log, b_d_skip=m_b_d_skip, b_norm_w=m_b_norm_w, b_w_out=m_b_w_out, ffn_w_in=m_ffn_w_in,
               ffn_conv_w=m_ffn_conv_w, ffn_conv_b=m_ffn_conv_b, ffn_w_out=m_ffn_w_out, final_norm_w=m_final_norm_w)
    var = dict(norm1_w=v_norm1_w, norm2_w=v_norm2_w, a_w_in=v_a_w_in, a_lb_logits=v_a_lb_logits, a_norm_w=v_a_norm_w,
               a_w_out=v_a_w_out, b_w_in=v_b_w_in, b_conv_w=v_b_conv_w, b_conv_b=v_b_conv_b, b_dt_bias=v_b_dt_bias,
               b_a_log=v_b_a_log, b_d_skip=v_b_d_skip, b_norm_w=v_b_norm_w, b_w_out=v_b_w_out, ffn_w_in=v_ffn_w_in,
               ffn_conv_w=v_ffn_conv_w, ffn_conv_b=v_ffn_conv_b, ffn_w_out=v_ffn_w_out, final_norm_w=v_final_norm_w)
    chip = 2 * lax.axis_index("x") + lax.axis_index("y")

    sh = {"a_in": a_w_in[0], "a_out": a_w_out[0], "b_in": b_w_in[0], "b_out": b_w_out[0], "f0_in": ffn_w_in[0],
          "f1_in": ffn_w_in[1], "f0_out": ffn_w_out[0], "f1_out": ffn_w_out[1]}
    sh = {u: a.astype(BF) for u, a in sh.items()}
    small_shapes = [w[n].shape for n in _SMALL_SPLIT]
    spack, small_offs = _pack([w[n] for n in _SMALL_SPLIT], 128, 8, F32)
    sall = _allgather8("s_gather", spack)
    sshards = [_unpack(sall[2 * j], small_shapes, small_offs) for j in range(4)]
    sfull = {n: jnp.concatenate([sshards[j][i] for j in range(4)], axis=-1) for i, n in enumerate(_SMALL_SPLIT)}

    p = dict(
        norm1_w=norm1_w, norm2_w=norm2_w, a_lb_logits=a_lb_logits, a_norm_w=a_norm_w[0], final_norm_w=final_norm_w,
        b_conv_w=sfull["b_conv_w"][0], b_conv_b=sfull["b_conv_b"][0], b_norm_w=sfull["b_norm_w"],
        ffn_conv_w=sfull["ffn_conv_w"], ffn_conv_b=ffn_conv_b,
        b_dt_bias=jnp.pad(b_dt_bias.reshape(1, 2 * SSM_HEADS), ((0, 0), (0, 128 - 2 * SSM_HEADS))),
        b_a_log=jnp.pad(b_a_log.reshape(1, 2 * SSM_HEADS), ((0, 0), (0, 128 - 2 * SSM_HEADS))),
        b_d_skip=jnp.repeat(b_d_skip[0], SSM_HD)[None],
    )

    loss_row, dx, g, gs_ = _sequence_grads(x[0], loss_target[0], p, sh)
    grads = {"a_w_in": gs_["a_in"][None], "a_w_out": gs_["a_out"][None], "b_w_in": gs_["b_in"].T[None],
             "b_w_out": gs_["b_out"][None], "ffn_w_in": jnp.stack([gs_["f0_in"], gs_["f1_in"]]),
             "ffn_w_out": jnp.stack([gs_["f0_out"], gs_["f1_out"]])}

    gsmall = {
        "norm1_w": jnp.concatenate(g["norm1_w"], axis=0), "norm2_w": jnp.concatenate(g["norm2_w"], axis=0),
        "a_lb_logits": jnp.stack(g["a_lb_logits"]), "a_norm_w": g["a_norm_w"], "b_conv_w": g["b_conv_w"],
        "b_conv_b": g["b_conv_b"], "b_dt_bias": g["b_dt_bias"], "b_a_log": g["b_a_log"], "b_d_skip": g["b_d_skip"],
        "b_norm_w": g["b_norm_w"], "ffn_conv_w": jnp.stack(g["ffn_conv_w"]),
        "ffn_conv_b": jnp.concatenate(g["ffn_conv_b"], axis=0), "final_norm_w": g["final_norm_w"],
    }
    pieces = [gsmall[n] for n in _SMALL] + [loss_row]
    piece_shapes = [a.shape for a in pieces]
    gspack, gs_offs = _pack(pieces, 128, 8, F32)
    rows = gspack.shape[0]
    gsall = _allgather8("gs_gather", gspack)

    def sum8(a):
        r = a[0]
        for i in range(1, 8):
            r = r + a[i]
        return r

    gssum = _vcall("gs_sum", sum8, (1,), [(gsall, (8, rows, 128), lambda i: (0, 0, 0))],
                   [((rows, 128), F32, (rows, 128), lambda i: (0, 0))])
    gs = dict(zip(_SMALL + ("loss",), _unpack(gssum, piece_shapes, gs_offs)))
    loss = gs["loss"][0, 0]
    lb2 = gs["a_lb_logits"]
    small_grads = {
        "norm1_w": gs["norm1_w"], "norm2_w": gs["norm2_w"], "a_lb_logits": lb2[0] + lb2[1], "a_norm_w": gs["a_norm_w"],
        "b_dt_bias": gs["b_dt_bias"][:, :2 * SSM_HEADS].reshape(1, 2, SSM_HEADS),
        "b_a_log": gs["b_a_log"][:, :2 * SSM_HEADS].reshape(1, 2, SSM_HEADS),
        "b_d_skip": gs["b_d_skip"].reshape(1, SSM_HEADS, SSM_HD).sum(axis=-1),
        "ffn_conv_b": gs["ffn_conv_b"], "final_norm_w": gs["final_norm_w"][0],
        "b_conv_w": gs["b_conv_w"][None], "b_conv_b": gs["b_conv_b"], "b_norm_w": gs["b_norm_w"], "ffn_conv_w": gs["ffn_conv_w"],
    }
    for n in _SMALL_SPLIT:
        width = w[n].shape[-1]
        small_grads[n] = lax.dynamic_slice_in_dim(small_grads[n], chip * width, width, axis=small_grads[n].ndim - 1)
    grads.update(small_grads)

    delta, new_m, new_v = {}, {}, {}
    for n in _BIG:
        shp = w[n].shape
        two_d = (shp[0] * shp[1], shp[2])
        d_, m_, v_ = _adam(f"adam_{n}", w[n].reshape(two_d), grads[n].reshape(two_d), mom[n].reshape(two_d), var[n].reshape(two_d))
        delta[n], new_m[n], new_v[n] = d_.reshape(shp), m_.reshape(shp), v_.reshape(shp)
    s_shapes = [w[n].shape for n in _SMALL]
    packs = [_pack([src[n] for n in _SMALL], 128, 8, F32) for src in (w, grads, mom, var)]
    outs = _adam("adam_small", *[pk[0] for pk in packs])
    for res, dst in zip(outs, (delta, new_m, new_v)):
        dst.update(dict(zip(_SMALL, _unpack(res, s_shapes, packs[0][1]))))

    return (loss, dx[None], *[grads[n] for n in _ORDER], *[delta[n] for n in _ORDER],
            *[new_m[n] for n in _ORDER], *[new_v[n] for n in _ORDER])
```

```python
import functools

import jax
import jax.numpy as jnp
from jax import lax
from jax.experimental import pallas as pl
from jax.experimental.pallas import tpu as pltpu

F32, BF = jnp.float32, jnp.bfloat16

D = 1024
EPS = 1e-6
HG_HEADS, HG_HD, HG_CHUNK, HG_SUB = 8, 128, 64, 8
HG_HB = 8
HG_CPS = 1
SSM_GB = 8
D_INNER, SSM_HEADS, SSM_HD, SSM_GROUPS, SSM_HPG, SSM_N, SSD_CHUNK = 2048, 32, 64, 8, 4, 128, 128
CONV_DIM = D_INNER + 2 * SSM_GROUPS * SSM_N
B_PROJ = 2 * D_INNER + 2 * SSM_GROUPS * SSM_N + 2 * SSM_HEADS
B_PROJ_PAD = 6272
D_FF = 2816
NEG = -1e30
ROWS = 512
GATE_ROWS = 1024
VMEM_LIMIT = 56 * 1024 * 1024

ADAM_LR, ADAM_B1, ADAM_B2, ADAM_EPS, ADAM_WD, ADAM_STEP = 0.001, 0.9, 0.999, 1e-08, 0.01, 10

MESH = pl.DeviceIdType.MESH


def _pick(n, cands):
    for c in cands:
        if n % c == 0:
            return c
    return n


class _Side:
    def __init__(self, ins, outs, plan, n_remote, n_local, alias=None):
        self.ins, self.outs, self.plan, self.n_remote, self.n_local = list(ins), list(outs), plan, n_remote, n_local
        self.alias = alias or {}

    def copies(self, in_refs, out_refs, send_sems, recv_sems, local_sems):
        remote, local = self.plan(in_refs, out_refs)
        cps = [pltpu.make_async_copy(s, d, local_sems.at[i]) for i, (s, d) in enumerate(local)]
        cps += [pltpu.make_async_remote_copy(src_ref=s, dst_ref=d, send_sem=send_sems.at[i], recv_sem=recv_sems.at[i],
                                             device_id=dev, device_id_type=MESH)
                for i, (s, d, dev) in enumerate(remote)]
        return cps

    def sems(self):
        return [pltpu.SemaphoreType.DMA((self.n_remote,)), pltpu.SemaphoreType.DMA((self.n_remote,)),
                pltpu.SemaphoreType.DMA((max(self.n_local, 1),))]


def _vcall(name, fn, grid, ins, outs, acc=None, scratch=(), side=None):
    acc = acc or {}
    n_in, n_out, nd = len(ins), len(outs), len(grid)
    n_sin = len(side.ins) if side else 0
    n_sout = len(side.outs) if side else 0
    n_scr = len(scratch)

    def body(*refs):
        in_refs, refs = refs[:n_in], refs[n_in:]
        sin_refs, refs = refs[:n_sin], refs[n_sin:]
        out_refs, refs = refs[:n_out], refs[n_out:]
        sout_refs, refs = refs[:n_sout], refs[n_sout:]
        scr, sems = refs[:n_scr], refs[n_scr:]
        if side:
            at_first, at_last = None, None
            for ax in range(nd):
                f, l = pl.program_id(ax) == 0, pl.program_id(ax) == grid[ax] - 1
                at_first = f if at_first is None else jnp.logical_and(at_first, f)
                at_last = l if at_last is None else jnp.logical_and(at_last, l)

            @pl.when(at_first)
            def _():
                for cp in side.copies(sin_refs, sout_refs, *sems):
                    cp.start()

        res = fn(*[r[...] for r in in_refs], *scr)
        if not isinstance(res, (tuple, list)):
            res = (res,)
        if side:
            @pl.when(at_last)
            def _():
                for cp in side.copies(sin_refs, sout_refs, *sems):
                    cp.wait()
        for j, (o_ref, r) in enumerate(zip(out_refs, res)):
            mode = acc.get(j)
            if mode is None:
                o_ref[...] = r.astype(o_ref.dtype)
                continue
            first = pl.program_id(nd - 1) == 0
            if mode == "all":
                for ax in range(nd - 1):
                    first = jnp.logical_and(first, pl.program_id(ax) == 0)

            @pl.when(first)
            def _():
                o_ref[...] = r.astype(o_ref.dtype)

            @pl.when(jnp.logical_not(first))
            def _():
                o_ref[...] += r.astype(o_ref.dtype)

    hbm = pl.BlockSpec(memory_space=pl.ANY)
    in_specs = [pl.BlockSpec(bs, im) for _, bs, im in ins] + [hbm] * n_sin
    out_specs = [pl.BlockSpec(bs, im) for _, _, bs, im in outs] + [hbm] * n_sout
    params = pltpu.CompilerParams(dimension_semantics=("arbitrary",) * nd, vmem_limit_bytes=VMEM_LIMIT)
    out_shape = [jax.ShapeDtypeStruct(s, dt) for s, dt, _, _ in outs]
    operands = [a for a, _, _ in ins]
    scratch = list(scratch)
    aliases = {}
    if side:
        out_shape += [jax.ShapeDtypeStruct(s, dt) for s, dt in side.outs]
        operands += side.ins
        scratch += side.sems()
        aliases = {n_in + i: n_out + o for i, o in side.alias.items()}
    out = pl.pallas_call(body, name=name, grid=grid, in_specs=in_specs, out_specs=out_specs, out_shape=out_shape,
                         scratch_shapes=scratch, compiler_params=params, input_output_aliases=aliases)(*operands)
    if side:
        return tuple(out[:n_out]), tuple(out[n_out:])
    return out[0] if n_out == 1 else out


def _mm(name, a, b, kind, out_dtype=F32, add=None, side=None):
    if kind == "tn":
        m, k = a.shape
        _, n = b.shape
        tm = _pick(m, (2048, 1024, 512, 256))
        tk = _pick(k, (1024, 1408, 896, 512, 256, 128))
        tn = _pick(n, (1024, 1408, 896, 512, 256, 128))

        def fn(av, bv):
            return lax.dot_general(av.astype(BF), bv.astype(BF), (((0,), (0,)), ((), ())),
                                   preferred_element_type=F32)

        return _vcall(name, fn, (k // tk, n // tn, m // tm),
                      [(a, (tm, tk), lambda i, j, s: (s, i)), (b, (tm, tn), lambda i, j, s: (s, j))],
                      [((k, n), F32, (tk, tn), lambda i, j, s: (i, j))], acc={0: "last"}, side=side)
    m, k = a.shape
    n = b.shape[1] if kind == "nn" else b.shape[0]
    long_k = k > 4096
    tm = _pick(m, (1024, 512, 256))
    tn = _pick(n, (512, 896, 256, 128)) if long_k else _pick(n, (1024, 1408, 896, 512, 256, 128))
    dims = (((1,), (0,)), ((), ())) if kind == "nn" else (((1,), (1,)), ((), ()))

    def fn(av, bv, *rest):
        r = lax.dot_general(av.astype(BF), bv.astype(BF), dims, preferred_element_type=F32)
        return r + rest[0] if rest else r

    ins = [(a, (tm, k), lambda i, j: (i, 0)),
           (b, (k, tn), lambda i, j: (0, j)) if kind == "nn" else (b, (tn, k), lambda i, j: (j, 0))]
    if add is not None:
        ins.append((add, (tm, tn), lambda i, j: (i, j)))
    return _vcall(name, fn, (m // tm, n // tn), ins, [((m, n), out_dtype, (tm, tn), lambda i, j: (i, j))], side=side)


def _rms(h, w):
    return h * lax.rsqrt(jnp.mean(h * h, axis=-1, keepdims=True) + EPS) * w


def _rms_fwd(name, h, w):
    L = h.shape[0]
    tb = _pick(L, (ROWS,))
    return _vcall(name, _rms, (L // tb,),
                  [(h, (tb, D), lambda i: (i, 0)), (w.reshape(1, D), (1, D), lambda i: (0, 0))],
                  [((L, D), BF, (tb, D), lambda i: (i, 0))])


def _rms_bwd(name, du, h, w, dh_next, side=None):
    L = h.shape[0]
    tb = _pick(L, (ROWS,))

    def fn(duv, hv, wv, dnv):
        _, vjp = jax.vjp(_rms, hv, wv)
        dh, dw = vjp(duv)
        return dh + dnv, dw

    row = lambda i: (i, 0)
    return _vcall(name, fn, (L // tb,),
                  [(du, (tb, D), row), (h, (tb, D), row), (w.reshape(1, D), (1, D), lambda i: (0, 0)),
                   (dh_next, (tb, D), row)],
                  [((L, D), F32, (tb, D), row), ((1, D), F32, (1, D), lambda i: (0, 0))], acc={1: "all"}, side=side)


def _loss_head(name, h, tgt, w):
    L = h.shape[0]
    tb = _pick(L, (ROWS,))

    def lossf(hv, wv, tv):
        err = _rms(hv, wv) - tv
        return 0.5 * jnp.sum(err * err) * (1.0 / D)

    def fn(hv, wv, tv):
        val, vjp = jax.vjp(lambda a, b: lossf(a, b, tv), hv, wv)
        dh, dw = vjp(jnp.ones((), F32))
        return jnp.full((1, 128), val, F32), dh, dw

    row = lambda i: (i, 0)
    zero = lambda i: (0, 0)
    return _vcall(name, fn, (L // tb,),
                  [(h, (tb, D), row), (w.reshape(1, D), (1, D), zero), (tgt, (tb, D), row)],
                  [((1, 128), F32, (1, 128), zero), ((L, D), F32, (tb, D), row), ((1, D), F32, (1, D), zero)],
                  acc={0: "all", 2: "all"})


def _bf(x):
    return x.astype(BF)


def _dot(a, b, dims):
    return lax.dot_general(a, b, (dims, ((), ())), preferred_element_type=F32)


def _tri(n, reverse):
    r = lax.broadcasted_iota(jnp.int32, (n, n), 0)
    c = lax.broadcasted_iota(jnp.int32, (n, n), 1)
    return (r <= c) if reverse else (r >= c)


def _tri_matmul(n, reverse, x):
    hi = x.astype(BF)
    r1 = x - hi.astype(F32)
    mid = r1.astype(BF)
    lo = (r1 - mid.astype(F32)).astype(BF)
    y = _dot(_tri(n, reverse).astype(BF), jnp.concatenate([hi, mid, lo], axis=1), ((1,), (0,)))
    w = x.shape[1]
    return (y[:, :w] + y[:, w:2 * w]) + y[:, 2 * w:]


@functools.partial(jax.custom_vjp, nondiff_argnums=(0, 1))
def _running_sum(n, reverse, x):
    return _tri_matmul(n, reverse, x)


def _running_sum_fwd(n, reverse, x):
    return _tri_matmul(n, reverse, x), None


def _running_sum_bwd(n, reverse, _, ct):
    return (_tri_matmul(n, not reverse, ct),)


_running_sum.defvjp(_running_sum_fwd, _running_sum_bwd)


def _gla_chunk(q_raw, f_raw, v, lb3, S, reverse):
    C, SB, HD = HG_CHUNK, HG_SUB, HG_HD
    H = S.shape[0]
    heads = [slice(HD * h, HD * (h + 1)) for h in range(H)]
    row3 = lax.broadcasted_iota(jnp.int32, (3, 1), 0)
    e = jnp.exp(lb3 - jnp.max(lb3, axis=0, keepdims=True))
    lb = jnp.sum(jnp.where(row3 == 0, e, 0.0), axis=0, keepdims=True) / jnp.sum(e, axis=0, keepdims=True)
    q = q_raw * jax.nn.sigmoid(q_raw)
    f = lb + (1.0 - lb) * jax.nn.sigmoid(f_raw)
    g = jnp.log(f)
    k = 1.0 - f
    b = _running_sum(C, reverse, g)
    row = lax.broadcasted_iota(jnp.int32, (C, 1), 0)
    vb = _bf(v)

    def rowof(x, t):
        return jnp.sum(jnp.where(row == t, x, 0.0), axis=0, keepdims=True)

    qe = _bf(q * jnp.exp(b))
    o = [_dot(qe[:, hs], _bf(S[h]), ((1,), (0,))) for h, hs in enumerate(heads)]
    att = [[] for _ in range(H)]
    for i in range(C // SB):
        lo = SB * i
        if (not reverse and i == 0) or (reverse and i == C // SB - 1):
            for h in range(H):
                att[h].append(jnp.zeros((SB, C), F32))
            continue
        first = lo + SB - 1 if reverse else lo
        r = rowof(b, first) - rowof(g, first)
        before = (row >= lo + SB) if reverse else (row < lo)
        qi = q[lo:lo + SB] * jnp.exp(b[lo:lo + SB] - r)
        kk = _bf(k * jnp.exp(jnp.where(before, r - b, NEG)))
        for h, hs in enumerate(heads):
            att[h].append(_dot(_bf(qi[:, hs]), kk[:, hs], ((1,), (1,))))
    o = [o[h] + _dot(_bf(jnp.concatenate(att[h], axis=0)), vb[:, hs], ((1,), (0,))) for h, hs in enumerate(heads)]
    s_i = lax.broadcasted_iota(jnp.int32, (SB, SB, HD), 0)
    t_i = lax.broadcasted_iota(jnp.int32, (SB, SB, HD), 1)
    pair = (t_i <= s_i) if reverse else (t_i >= s_i)
    shp = (SB, SB, HD)
    diag = [[] for _ in range(H)]
    for i in range(C // SB):
        rows = slice(SB * i, SB * (i + 1))
        for h, hs in enumerate(heads):
            qb, kb, bb = q[rows, hs], k[rows, hs], b[rows, hs]
            dif = lax.broadcast_in_dim(bb, shp, (1, 2)) - lax.broadcast_in_dim(bb, shp, (0, 2))
            w = lax.broadcast_in_dim(qb, shp, (1, 2)) * jnp.exp(jnp.where(pair, dif, NEG)) * lax.broadcast_in_dim(kb, shp, (0, 2))
            d = jnp.sum(w, axis=2, keepdims=True)
            diag[h].append(jnp.sum(d * lax.broadcast_in_dim(v[rows, hs], shp, (0, 2)), axis=0))
    o = jnp.concatenate([o[h] + jnp.concatenate(diag[h], axis=0) for h in range(H)], axis=1)
    btot = rowof(b, 0 if reverse else C - 1)
    kd = _bf(k * jnp.exp(btot - b))
    eye = lax.broadcasted_iota(jnp.int32, (HD, HD), 0) == lax.broadcasted_iota(jnp.int32, (HD, HD), 1)
    s_new = []
    for h, hs in enumerate(heads):
        btot_col = jnp.sum(jnp.where(eye, btot[:, hs], 0.0), axis=1, keepdims=True)
        s_new.append((jnp.exp(btot_col) * S[h] + _dot(kd[:, hs], vb[:, hs], ((0,), (0,))))[None])
    return o, jnp.concatenate(s_new, axis=0)


def _gla_fwd(name, pa, lbl, reverse, side=None):
    L = pa.shape[0]
    C, cs = HG_CHUNK, HG_CPS
    nc, ns = L // C, L // (C * cs)
    cidx = (lambda i: ns - 1 - i) if reverse else (lambda i: i)
    sec = 2 if reverse else 1
    hb_n, nh = HG_HB, HG_HEADS // HG_HB

    def fn(qr, fr, v, lb3, s_ref):
        @pl.when(pl.program_id(1) == 0)
        def _():
            s_ref[...] = jnp.zeros_like(s_ref)

        s = s_ref[...]
        outs, olds = [None] * cs, [None] * cs
        for c in (reversed(range(cs)) if reverse else range(cs)):
            rows = slice(C * c, C * (c + 1))
            olds[c] = s[None]
            outs[c], s = _gla_chunk(qr[rows], fr[rows], v[rows], lb3, s, reverse)
        s_ref[...] = s
        return jnp.concatenate(outs, axis=0), jnp.concatenate(olds, axis=0)

    blk = (C * cs, HG_HD * hb_n)
    return _vcall(name, fn, (nh, ns),
                  [(pa, blk, lambda h, i: (cidx(i), h)), (pa, blk, lambda h, i: (cidx(i), sec * nh + h)),
                   (pa, blk, lambda h, i: (cidx(i), 3 * nh + h)), (lbl, (3, HG_HD * hb_n), lambda h, i: (0, h))],
                  [((L, D), F32, blk, lambda h, i: (cidx(i), h)),
                   ((nc, HG_HEADS, HG_HD, HG_HD), F32, (cs, hb_n, HG_HD, HG_HD), lambda h, i: (cidx(i), h, 0, 0))],
                  scratch=[pltpu.VMEM((hb_n, HG_HD, HG_HD), F32)], side=side)


def _gla_bwd(name, pa, lbl, s_in, do, reverse, prev=None, side=None):
    L = pa.shape[0]
    C, cs = HG_CHUNK, HG_CPS
    ns = L // (C * cs)
    cidx = (lambda i: i) if reverse else (lambda i: ns - 1 - i)
    sec = 2 if reverse else 1
    n_prev = 0 if prev is None else 2
    hb_n, nh = HG_HB, HG_HEADS // HG_HB

    def fn(qr, fr, v, lb3, s, dov, *rest):
        ds_ref = rest[n_prev]

        @pl.when(pl.program_id(1) == 0)
        def _():
            ds_ref[...] = jnp.zeros_like(ds_ref)

        ds = ds_ref[...]
        parts, dlb = [None] * cs, None
        for c in (range(cs) if reverse else reversed(range(cs))):
            rows = slice(C * c, C * (c + 1))
            _, vjp = jax.vjp(lambda *a: _gla_chunk(*a, reverse), qr[rows], fr[rows], v[rows], lb3, s[c])
            dq, df, dv, dl, ds = vjp((dov[rows], ds))
            if n_prev:
                dq, dv = dq + rest[0][rows], dv + rest[1][rows]
            parts[c] = (dq, df, dv)
            dlb = dl if dlb is None else dlb + dl
        ds_ref[...] = ds
        return tuple(jnp.concatenate([p_[j] for p_ in parts], axis=0) for j in range(3)) + (dlb,)

    blk = (C * cs, HG_HD * hb_n)
    at = lambda h, i: (cidx(i), h)
    ins = [(pa, blk, at), (pa, blk, lambda h, i: (cidx(i), sec * nh + h)), (pa, blk, lambda h, i: (cidx(i), 3 * nh + h)),
           (lbl, (3, HG_HD * hb_n), lambda h, i: (0, h)),
           (s_in, (cs, hb_n, HG_HD, HG_HD), lambda h, i: (cidx(i), h, 0, 0)), (do, blk, at)]
    if prev is not None:
        ins += [(prev[0], blk, at), (prev[1], blk, at)]
    sum_dt = F32 if prev is None else BF
    return _vcall(name, fn, (nh, ns), ins,
                  [((L, D), sum_dt, blk, at), ((L, D), BF, blk, at), ((L, D), sum_dt, blk, at),
                   ((3, D), F32, (3, HG_HD * hb_n), lambda h, i: (0, h))],
                  acc={3: "last"}, scratch=[pltpu.VMEM((hb_n, HG_HD, HG_HD), F32)], side=side)


def _hgout(o_f, o_b, g, nw):
    o = o_f + o_b
    return _rms(o, nw) * (g * jax.nn.sigmoid(g))


def _hgout_fwd(name, o_f, o_b, pa, nw, side=None):
    L = o_f.shape[0]
    tb = _pick(L, (GATE_ROWS, ROWS))
    blk = (tb, HG_HD)
    at = lambda h, i: (i, h)
    return _vcall(name, _hgout, (HG_HEADS, L // tb),
                  [(o_f, blk, at), (o_b, blk, at), (pa, blk, lambda h, i: (i, 32 + h)),
                   (nw.reshape(1, HG_HD), (1, HG_HD), lambda h, i: (0, 0))],
                  [((L, D), BF, blk, at)], side=side)


def _hgout_bwd(name, o_f, o_b, pa, nw, dy, side=None):
    L = o_f.shape[0]
    tb = _pick(L, (GATE_ROWS, ROWS))

    def fn(ofv, obv, gv, nwv, dyv):
        _, vjp = jax.vjp(_hgout, ofv, obv, gv, nwv)
        do, _, dg, dnw = vjp(dyv)
        return do, dg, dnw

    blk = (tb, HG_HD)
    at = lambda h, i: (i, h)
    zero = lambda h, i: (0, 0)
    return _vcall(name, fn, (HG_HEADS, L // tb),
                  [(o_f, blk, at), (o_b, blk, at), (pa, blk, lambda h, i: (i, 32 + h)),
                   (nw.reshape(1, HG_HD), (1, HG_HD), zero), (dy, blk, at)],
                  [((L, D), F32, blk, at), ((L, D), BF, blk, at), ((1, HG_HD), F32, (1, HG_HD), zero)],
                  acc={2: "all"}, side=side)


def _shift(x, s):
    if s == 0:
        return x
    n = x.shape[0]
    t = lax.broadcasted_iota(jnp.int32, (n, 1), 0)
    if s > 0:
        return jnp.where(t >= s, pltpu.roll(x, s, 0), 0.0)
    return jnp.where(t < n + s, pltpu.roll(x, n + s, 0), 0.0)


def _conv(x, w, b):
    kk = w.shape[0]
    p = (kk - 1) // 2
    y = b
    for j in range(kk):
        y = y + w[j:j + 1] * _shift(x, p - j)
    return y


def _conv_bwd(x, w, dc):
    kk = w.shape[0]
    p = (kk - 1) // 2
    dx = None
    dws = []
    for j in range(kk):
        t = w[j:j + 1] * _shift(dc, j - p)
        dx = t if dx is None else dx + t
        dws.append(jnp.sum(dc * _shift(x, p - j), axis=0, keepdims=True))
    rows = lax.broadcasted_iota(jnp.int32, (kk, 1), 0)
    dw = None
    for j in range(kk):
        t = jnp.where(rows == j, dws[j], 0.0)
        dw = t if dw is None else dw + t
    return dx, dw, jnp.sum(dc, axis=0, keepdims=True)


def _silu(c):
    return c * jax.nn.sigmoid(c)


def _silu_grad(c):
    s = jax.nn.sigmoid(c)
    return s * (1.0 + c * (1.0 - s))


def _glu_fwd(name, pf, cw, cb):
    L = pf.shape[0]
    tc = 128
    nt = D_FF // tc
    return _vcall(name, lambda gate, val, w, b: _silu(_conv(gate, w, b)) * val, (nt,),
                  [(pf, (L, tc), lambda j: (0, j)), (pf, (L, tc), lambda j: (0, nt + j)),
                   (cw, (3, tc), lambda j: (0, j)), (cb.reshape(1, D_FF), (1, tc), lambda j: (0, j))],
                  [((L, D_FF), BF, (L, tc), lambda j: (0, j))])


def _glu_bwd(name, pf, cw, cb, dy, side=None):
    L = pf.shape[0]
    tc = 128
    nt = D_FF // tc

    def fn(gate, val, w, b, dyv):
        c = _conv(gate, w, b)
        dgate, dw, db = _conv_bwd(gate, w, dyv * val * _silu_grad(c))
        return dgate, dyv * _silu(c), dw, db

    col = lambda j: (0, j)
    return _vcall(name, fn, (nt,),
                  [(pf, (L, tc), col), (pf, (L, tc), lambda j: (0, nt + j)), (cw, (3, tc), col),
                   (cb.reshape(1, D_FF), (1, tc), col), (dy, (L, tc), col)],
                  [((L, D_FF), BF, (L, tc), col), ((L, D_FF), BF, (L, tc), col),
                   ((3, D_FF), F32, (3, tc), col), ((1, D_FF), F32, (1, tc), col)], side=side)


def _perm_tile(j):
    return jnp.where(j < 16, 4 * (j // 2) + j % 2, jnp.where(j < 24, 4 * (j - 16) + 2, 4 * (j - 24) + 3))


def _mpre_fwd(name, pb, cw, cb):
    L = pb.shape[0]
    tc = 128
    return _vcall(name, lambda x, w, b: _silu(_conv(x, w, b)), (CONV_DIM // tc,),
                  [(pb, (L, tc), lambda j: (0, 16 + j)), (cw, (5, tc), lambda j: (0, j)),
                   (cb.reshape(1, CONV_DIM), (1, tc), lambda j: (0, j))],
                  [((L, CONV_DIM), F32, (L, tc), lambda j: (0, _perm_tile(j)))])


def _mpre_bwd(name, pb, cw, cb, dact):
    L = pb.shape[0]
    tc = 128
    col = lambda j: (0, j)
    return _vcall(name, lambda x, w, b, da: _conv_bwd(x, w, da * _silu_grad(_conv(x, w, b))), (CONV_DIM // tc,),
                  [(pb, (L, tc), lambda j: (0, 16 + j)), (cw, (5, tc), col), (cb.reshape(1, CONV_DIM), (1, tc), col),
                   (dact, (L, tc), lambda j: (0, _perm_tile(j)))],
                  [((L, CONV_DIM), BF, (L, tc), col), ((5, CONV_DIM), F32, (5, tc), col),
                   ((1, CONV_DIM), F32, (1, tc), col)])


def _softplus(x):
    return jnp.maximum(x, 0.0) + jnp.log(1.0 + jnp.exp(-jnp.abs(x)))


def _dt_fwd(name, pb, dtb, alog):
    L = pb.shape[0]
    tb = _pick(L, (1024, ROWS))

    def fn(x, bias, al):
        dt = _softplus(x + bias)
        return dt, dt * (-jnp.exp(al))

    row = lambda i: (i, 0)
    zero = lambda i: (0, 0)
    return _vcall(name, fn, (L // tb,),
                  [(pb, (tb, 128), lambda i: (i, 48)), (dtb, (1, 128), zero), (alog, (1, 128), zero)],
                  [((L, 128), F32, (tb, 128), row), ((L, 128), F32, (tb, 128), row)])


def _dt_bwd(name, pb, dtb, alog, ddt_f, dla_f, ddt_b, dla_b):
    L = pb.shape[0]
    tb = _pick(L, (1024, ROWS))

    def fn(x, bias, al, a1, b1, a2, b2):
        ddt = jnp.sum(a1, axis=0) + jnp.sum(a2, axis=0)
        dla = jnp.sum(b1, axis=0) + jnp.sum(b2, axis=0)
        z = x + bias
        dt = _softplus(z)
        a = -jnp.exp(al)
        dz = (ddt + dla * a) * jax.nn.sigmoid(z)
        return dz, jnp.sum(dz, axis=0, keepdims=True), jnp.sum(dla * dt, axis=0, keepdims=True) * a

    zero = lambda i: (0, 0)
    g3 = (ddt_f.shape[0], tb, 128)
    at3 = lambda i: (0, i, 0)
    return _vcall(name, fn, (L // tb,),
                  [(pb, (tb, 128), lambda i: (i, 48)), (dtb, (1, 128), zero), (alog, (1, 128), zero),
                   (ddt_f, g3, at3), (dla_f, g3, at3), (ddt_b, g3, at3), (dla_b, g3, at3)],
                  [((L, 128), BF, (tb, 128), lambda i: (i, 0)), ((1, 128), F32, (1, 128), zero),
                   ((1, 128), F32, (1, 128), zero)], acc={1: "all", 2: "all"})


def _split_dot(x, e, dims, pieces):
    hi = x.astype(BF)
    r1 = x - hi.astype(F32)
    mid = r1.astype(BF)
    y = _dot(hi, e, dims) + _dot(mid, e, dims)
    if pieces == 3:
        y = y + _dot((r1 - mid.astype(F32)).astype(BF), e, dims)
    return y


@functools.partial(jax.custom_vjp, nondiff_argnums=(2,))
def _spread(x, e, pieces):
    return _split_dot(x, e, ((1,), (0,)), pieces)


def _spread_fwd(x, e, pieces):
    return _split_dot(x, e, ((1,), (0,)), pieces), e


def _spread_bwd(pieces, e, ct):
    return _split_dot(ct, e, ((1,), (1,)), pieces), jnp.zeros_like(e)


_spread.defvjp(_spread_fwd, _spread_bwd)


def _ssd_chunk(xa, dt, la, hs, head0, reverse):
    C, P4, HD, N = SSD_CHUNK, SSM_HPG * SSM_HD, SSM_HD, SSM_N
    G = hs.shape[0]
    nh = SSM_HPG * G
    row = lax.broadcasted_iota(jnp.int32, (C, 1), 0)
    lane = lax.broadcasted_iota(jnp.int32, (1, 128), 1)
    eye = lax.broadcasted_iota(jnp.int32, (C, C), 0) == lax.broadcasted_iota(jnp.int32, (C, C), 1)
    tri = _tri(C, reverse)
    last = 0 if reverse else C - 1
    acum = _running_sum(C, reverse, la)
    atot = jnp.sum(jnp.where(row == last, acum, 0.0), axis=0, keepdims=True)
    src = lax.broadcasted_iota(jnp.int32, (128, 1), 0) - head0
    to_x = (src == lax.broadcasted_iota(jnp.int32, (1, nh * HD), 1) // HD).astype(BF)
    dt_x = _spread(dt, to_x, 2)
    ea_x = _spread(jnp.exp(acum), to_x, 2)
    dec_x = _spread(jnp.exp(atot - acum), to_x, 2)
    col_head = lax.broadcasted_iota(jnp.int32, (1, P4), 1) // HD
    row_head = lax.broadcasted_iota(jnp.int32, (P4, 1), 0) // HD
    ys, news = [], []
    for gi in range(G):
        xs = xa[:, 512 * gi:512 * gi + P4]
        bm = _bf(xa[:, 512 * gi + P4:512 * gi + P4 + N])
        cm = _bf(xa[:, 512 * gi + P4 + N:512 * (gi + 1)])
        gx = slice(P4 * gi, P4 * (gi + 1))
        cb = _dot(cm, bm, ((1,), (1,)))
        xd = xs * dt_x[:, gx]
        ms, xds, scale = [], [], 0.0
        for j in range(SSM_HPG):
            i = SSM_HPG * gi + j
            ac = jnp.sum(jnp.where(lane == head0 + i, acum, 0.0), axis=1, keepdims=True)
            ac_row = jnp.sum(jnp.where(eye, ac, 0.0), axis=0, keepdims=True)
            ms.append(_bf(cb * jnp.exp(jnp.where(tri, ac - ac_row, NEG))))
            xds.append(_bf(jnp.where(col_head == j, xd, 0.0)))
            a_i = jnp.sum(jnp.where(lane == head0 + i, atot, 0.0), axis=1, keepdims=True)
            scale = scale + jnp.where(row_head == j, jnp.exp(a_i), 0.0)
        y = _dot(jnp.concatenate(ms, axis=1), jnp.concatenate(xds, axis=0), ((1,), (0,)))
        y = y + _dot(cm, _bf(hs[gi]), ((1,), (1,))) * ea_x[:, gx]
        ys.append(y)
        news.append((scale * hs[gi] + _dot(_bf(xd * dec_x[:, gx]), bm, ((0,), (0,))))[None])
    return jnp.concatenate(ys, axis=1), jnp.concatenate(news, axis=0)


def _ssd_fwd(name, xact, dt, la, reverse):
    L = xact.shape[0]
    C = SSD_CHUNK
    nc = L // C
    cidx = (lambda i: nc - 1 - i) if reverse else (lambda i: i)
    base = SSM_HEADS if reverse else 0
    P4 = SSM_HPG * SSM_HD

    gb_n = SSM_GB

    def fn(xa, dtv, lav, h_ref):
        @pl.when(pl.program_id(1) == 0)
        def _():
            h_ref[...] = jnp.zeros_like(h_ref)

        h_all = h_ref[...]
        y, h_new = _ssd_chunk(xa, dtv, lav, h_all, base + SSM_HPG * gb_n * pl.program_id(0), reverse)
        h_ref[...] = h_new
        return y, h_all[None]

    return _vcall(name, fn, (SSM_GROUPS // gb_n, nc),
                  [(xact, (C, 512 * gb_n), lambda g, i: (cidx(i), g)), (dt, (C, 128), lambda g, i: (cidx(i), 0)),
                   (la, (C, 128), lambda g, i: (cidx(i), 0))],
                  [((L, D_INNER), F32, (C, P4 * gb_n), lambda g, i: (cidx(i), g)),
                   ((nc, SSM_GROUPS, P4, SSM_N), F32, (1, gb_n, P4, SSM_N), lambda g, i: (cidx(i), g, 0, 0))],
                  scratch=[pltpu.VMEM((gb_n, P4, SSM_N), F32)])


def _ssd_bwd(name, xact, dt, la, h_in, dy, reverse, prev_xs=None, prev_all=None, side=None):
    L = xact.shape[0]
    C = SSD_CHUNK
    nc = L // C
    cidx = (lambda i: i) if reverse else (lambda i: nc - 1 - i)
    base = SSM_HEADS if reverse else 0
    P4 = SSM_HPG * SSM_HD

    gb_n = SSM_GB

    def fn(xa, dtv, lav, hs, dyv, pv, dh_ref):
        @pl.when(pl.program_id(1) == 0)
        def _():
            dh_ref[...] = jnp.zeros_like(dh_ref)

        head0 = base + SSM_HPG * gb_n * pl.program_id(0)
        _, vjp = jax.vjp(lambda a, b, c, d: _ssd_chunk(a, b, c, d, head0, reverse), xa, dtv, lav, hs[0])
        dxa, ddt, dla, dh = vjp((dyv, dh_ref[...]))
        dh_ref[...] = dh
        if prev_all is not None:
            dxa = dxa + pv
        else:
            zeros = jnp.zeros((C, 2 * SSM_N), F32)
            dxa = dxa + jnp.concatenate([t for gb in range(gb_n) for t in (pv[:, P4 * gb:P4 * (gb + 1)], zeros)], axis=1)
        return dxa, ddt[None], dla[None]

    at = lambda g, i: (cidx(i), g)
    at0 = lambda g, i: (cidx(i), 0)
    pv = (prev_all, (C, 512 * gb_n), at) if prev_all is not None else (prev_xs, (C, P4 * gb_n), at)
    steps = SSM_GROUPS // gb_n
    return _vcall(name, fn, (steps, nc),
                  [(xact, (C, 512 * gb_n), at), (dt, (C, 128), at0), (la, (C, 128), at0),
                   (h_in, (1, gb_n, P4, SSM_N), lambda g, i: (cidx(i), g, 0, 0)), (dy, (C, P4 * gb_n), at), pv],
                  [((L, CONV_DIM), F32, (C, 512 * gb_n), at),
                   ((steps, L, 128), F32, (1, C, 128), lambda g, i: (g, cidx(i), 0)),
                   ((steps, L, 128), F32, (1, C, 128), lambda g, i: (g, cidx(i), 0))],
                  scratch=[pltpu.VMEM((gb_n, P4, SSM_N), F32)], side=side)


def _mpost(y_f, y_b, xs, z, dsk, nw):
    y = (y_f + y_b + xs * dsk) * (z * jax.nn.sigmoid(z))
    return _rms(y, nw)


def _mpost_fwd(name, y_f, y_b, xact, pb, dsk, nw):
    L = y_f.shape[0]
    tb = _pick(L, (GATE_ROWS, ROWS))
    blk = (tb, 256)
    at = lambda g, i: (i, g)
    par = lambda g, i: (0, g)
    return _vcall(name, _mpost, (SSM_GROUPS, L // tb),
                  [(y_f, blk, at), (y_b, blk, at), (xact, blk, lambda g, i: (i, 2 * g)), (pb, blk, at),
                   (dsk, (1, 256), par), (nw, (1, 256), par)],
                  [((L, D_INNER), BF, blk, at)])


def _mpost_bwd(name, y_f, y_b, xact, pb, dsk, nw, dy):
    L = y_f.shape[0]
    tb = _pick(L, (GATE_ROWS, ROWS))

    def fn(yf, yb, xs, z, dskv, nwv, dyv):
        _, vjp = jax.vjp(_mpost, yf, yb, xs, z, dskv, nwv)
        dyf, _, dxs, dz, ddsk, dnw = vjp(dyv)
        return dyf, dxs, dz, ddsk, dnw

    blk = (tb, 256)
    at = lambda g, i: (i, g)
    par = lambda g, i: (0, g)
    return _vcall(name, fn, (SSM_GROUPS, L // tb),
                  [(y_f, blk, at), (y_b, blk, at), (xact, blk, lambda g, i: (i, 2 * g)), (pb, blk, at),
                   (dsk, (1, 256), par), (nw, (1, 256), par), (dy, blk, at)],
                  [((L, D_INNER), F32, blk, at), ((L, D_INNER), F32, blk, at), ((L, D_INNER), BF, blk, at),
                   ((1, D_INNER), F32, (1, 256), par), ((1, D_INNER), F32, (1, 256), par)],
                  acc={3: "last", 4: "last"})


def _ffn_fwd(tag, h, nw, w_in, cw, cb, w_out):
    u = _rms_fwd(f"{tag}_norm", h, nw)
    pf = _mm(f"{tag}_in", u, w_in, "nn")
    yf = _glu_fwd(f"{tag}_glu", pf, cw, cb)
    return _mm(f"{tag}_out", yf, w_out, "nn", add=h), (u, pf, yf)


def _ffn_bwd(tag, h, nw, w_in, cw, cb, w_out, saved, dh, side=None):
    u, pf, yf = saved
    d_w_out = _mm(f"{tag}_dwout", yf, dh, "tn")
    dyf = _mm(f"{tag}_dy", dh, w_out, "nt")
    dgate, dval, dcw, dcb = _glu_bwd(f"{tag}_dglu", pf, cw, cb, dyf)
    dpf = jnp.concatenate([dgate, dval], axis=1)
    d_w_in = _mm(f"{tag}_dwin", u, dpf, "tn")
    got = ()
    if side is None:
        du = _mm(f"{tag}_du", dpf, w_in, "nt")
    else:
        (du,), got = _mm(f"{tag}_du", dpf, w_in, "nt", side=side)
    dh_in, dnw = _rms_bwd(f"{tag}_dnorm", du, h, nw, dh)
    return dh_in, dnw, d_w_in, dcw, dcb, d_w_out, got


def _sequence_grads(x, tgt, p, sh):
    g = {}
    lbl = p["a_lb_logits"]
    first, early, mid, last = ("a_in", "a_out"), ("f0_in",), ("b_in", "b_out"), ("f1_in", "f1_out", "f0_out")
    W = {}
    got = _exchange("w_first", _gather_side(first, (), sh, W))
    W.update(zip(first, got))
    got = _exchange("w_first_pass", _gather_side((), first, sh, W))
    W.update(zip(first, got))
    u1 = _rms_fwd("a_norm", x, p["norm1_w"][0])
    (pa,), got = _mm("a_in", u1, W["a_in"], "nn", side=_gather_side(early, (), sh, W))
    W.update(zip(early, got))
    (o_f, s_f), got = _gla_fwd("a_scan_f", pa, lbl, False, side=_gather_side(mid, early, sh, W))
    W.update(zip(mid + early, got))
    (o_b, s_b), got = _gla_fwd("a_scan_b", pa, lbl, True, side=_gather_side(last, mid, sh, W))
    W.update(zip(last + mid, got))
    (ya,), got = _hgout_fwd("a_gate", o_f, o_b, pa, p["a_norm_w"], side=_gather_side((), last, sh, W))
    W.update(zip(last, got))
    wb4 = W["b_in"].reshape(4, D, B_PROJ // 4)
    p = dict(p, a_w_in=W["a_in"], a_w_out=W["a_out"], b_w_out=W["b_out"], ffn_w_in=(W["f0_in"], W["f1_in"]),
             ffn_w_out=(W["f0_out"], W["f1_out"]),
             b_w_in=jnp.pad(jnp.concatenate([wb4[j] for j in range(4)], axis=1), ((0, 0), (0, B_PROJ_PAD - B_PROJ))))
    h1 = _mm("a_out", ya, p["a_w_out"], "nn", add=x)
    h2, ffn0 = _ffn_fwd("f0", h1, p["norm2_w"][0], p["ffn_w_in"][0], p["ffn_conv_w"][0], p["ffn_conv_b"][0], p["ffn_w_out"][0])
    u3 = _rms_fwd("b_norm", h2, p["norm1_w"][1])
    pb = _mm("b_in", u3, p["b_w_in"], "nn")
    xact = _mpre_fwd("b_conv", pb, p["b_conv_w"], p["b_conv_b"])
    dt, la = _dt_fwd("b_dt", pb, p["b_dt_bias"], p["b_a_log"])
    y_f, hs_f = _ssd_fwd("b_scan_f", xact, dt, la, False)
    y_b, hs_b = _ssd_fwd("b_scan_b", xact, dt, la, True)
    yb = _mpost_fwd("b_gate", y_f, y_b, xact, pb, p["b_d_skip"], p["b_norm_w"])
    h3 = _mm("b_out", yb, p["b_w_out"], "nn", add=h2)
    h4, ffn1 = _ffn_fwd("f1", h3, p["norm2_w"][1], p["ffn_w_in"][1], p["ffn_conv_w"][1], p["ffn_conv_b"][1], p["ffn_w_out"][1])
    loss, dh4, g["final_norm_w"] = _loss_head("head", h4, tgt, p["final_norm_w"])
    dh3, dn2_1, dwin1, dcw1, dcb1, dwout1, _ = _ffn_bwd("f1", h3, p["norm2_w"][1], p["ffn_w_in"][1], p["ffn_conv_w"][1],
                                                        p["ffn_conv_b"][1], p["ffn_w_out"][1], ffn1, dh4)
    G = {"f1_in": dwin1, "f1_out": dwout1}
    G["b_out"] = _mm("b_dwout", yb, dh3, "tn")
    wave1 = ("f1_in", "f1_out", "b_out")
    (dyb,), got = _mm("b_dy", dh3, p["b_w_out"], "nt", side=_pair_side(wave1, G))
    chip_sums = _pair_sums(wave1, G, got)
    dys, dxs, dz, g["b_d_skip"], g["b_norm_w"] = _mpost_bwd("b_dgate", y_f, y_b, xact, pb, p["b_d_skip"], p["b_norm_w"], dyb)
    dxa1, ddt_f, dla_f = _ssd_bwd("b_dscan_f", xact, dt, la, hs_f, dys, False, prev_xs=dxs)
    dxa, ddt_b, dla_b = _ssd_bwd("b_dscan_b", xact, dt, la, hs_b, dys, True, prev_all=dxa1)
    dxbc, g["b_conv_w"], g["b_conv_b"] = _mpre_bwd("b_dconv", pb, p["b_conv_w"], p["b_conv_b"], dxa)
    ddtr, g["b_dt_bias"], g["b_a_log"] = _dt_bwd("b_ddt", pb, p["b_dt_bias"], p["b_a_log"], ddt_f, dla_f, ddt_b, dla_b)
    dpb = jnp.concatenate([dz, dxbc, ddtr], axis=1)
    G["b_in"] = _mm("b_dwin", dpb, u3, "tn")
    du3 = _mm("b_du", dpb, p["b_w_in"], "nt")
    dh2, dn1_1 = _rms_bwd("b_dnorm", du3, h2, p["norm1_w"][1], dh3)
    dh1, dn2_0, G["f0_in"], dcw0, dcb0, G["f0_out"], got = _ffn_bwd("f0", h1, p["norm2_w"][0], p["ffn_w_in"][0], p["ffn_conv_w"][0],
                                                                   p["ffn_conv_b"][0], p["ffn_w_out"][0], ffn0, dh2,
                                                                   side=_pair_side(("b_in",), G))
    chip_sums.update(_pair_sums(("b_in",), G, got))
    wave3 = ("f0_in", "f0_out")
    (G["a_out"],), got = _mm("a_dwout", ya, dh1, "tn", side=_pair_side(wave3, G))
    chip_sums.update(_pair_sums(wave3, G, got))
    dya = _mm("a_dy", dh1, p["a_w_out"], "nt")
    do, dg, g["a_norm_w"] = _hgout_bwd("a_dgate", o_f, o_b, pa, p["a_norm_w"], dya)
    late = last + mid + early
    (dq1, df1, dv1, dl1), got = _gla_bwd("a_dscan_f", pa, lbl, s_f, do, False, side=_chips_side(late, chip_sums))
    shards = {u: _chip_sum(f"gl_sum_{u}", _GGEO[u], chip_sums[u], r) for u, r in zip(late, got)}
    (dq, df2, dv, dl2), got = _gla_bwd("a_dscan_b", pa, lbl, s_b, do, True, prev=(dq1, dv1), side=_halves_side(late, shards))
    shards = dict(zip(late, got))
    dpa = jnp.concatenate([dq, df1, df2, dv, dg], axis=1)
    G["a_in"] = _mm("a_dwin", u1, dpa, "tn")
    chip_sums = _pair_sums(first, G, _exchange("ga_pair", _pair_side(first, G)))
    (du1,), got = _mm("a_du", dpa, p["a_w_in"], "nt", side=_chips_side(first, chip_sums))
    mine = {u: _chip_sum(f"ga_sum_{u}", _GGEO[u], chip_sums[u], r) for u, r in zip(first, got)}
    (dx, dn1_0), got = _rms_bwd("a_dnorm", du1, x, p["norm1_w"][0], dh1, side=_halves_side(first, mine))
    shards.update(zip(first, got))
    g["a_lb_logits"] = (dl1, dl2)
    g["norm1_w"] = (dn1_0, dn1_1)
    g["norm2_w"] = (dn2_0, dn2_1)
    g["ffn_conv_w"] = (dcw0, dcw1)
    g["ffn_conv_b"] = (dcb0, dcb1)
    return loss, dx, g, shards


def _here():
    return lax.axis_index("x"), lax.axis_index("y"), lax.axis_index("c")


def _allgather8(name, src):
    blk = src.shape

    def body(x_ref, out_ref, send_sems, recv_sems, local_sem):
        x, y, c = _here()
        me, sibling = (x, y, c), (x, y, 1 - c)
        chips = [(1 - x, y), (x, 1 - y), (1 - x, 1 - y)]
        own = x_ref

        def slot(px, py, pc):
            return out_ref.at[4 * px + 2 * py + pc]

        def copy(k, block, to, from_own=False):
            return pltpu.make_async_remote_copy(
                src_ref=own if from_own else slot(*block), dst_ref=slot(*block),
                send_sem=send_sems.at[k], recv_sem=recv_sems.at[k], device_id=to, device_id_type=MESH)

        mine = pltpu.make_async_copy(own, slot(*me), local_sem)
        mine.start()
        first = [copy(0, me, sibling, from_own=True)]
        first += [copy(1 + j, me, (*chip, c), from_own=True) for j, chip in enumerate(chips)]
        for cp in first:
            cp.start()
        passed = [copy(4 + j, (*chip, c), sibling) for j, chip in enumerate(chips)]
        for j, chip in enumerate(chips):
            copy(1 + j, (*chip, c), me).wait_recv()
            passed[j].start()
        copy(0, sibling, me).wait_recv()
        for j, chip in enumerate(chips):
            copy(4 + j, (*chip, 1 - c), me).wait_recv()
        for cp in first + passed:
            cp.wait_send()
        mine.wait()

    return pl.pallas_call(
        body, name=name,
        out_shape=jax.ShapeDtypeStruct((8,) + tuple(blk), src.dtype),
        in_specs=[pl.BlockSpec(memory_space=pl.ANY)],
        out_specs=pl.BlockSpec(memory_space=pl.ANY),
        scratch_shapes=[pltpu.SemaphoreType.DMA((7,)), pltpu.SemaphoreType.DMA((7,)), pltpu.SemaphoreType.DMA],
    )(src)


def _exchange(name, side):
    n_i, n_o = len(side.ins), len(side.outs)

    def body(*refs):
        copies = side.copies(refs[:n_i], refs[n_i:n_i + n_o], *refs[n_i + n_o:])
        for cp in copies:
            cp.start()
        for cp in copies:
            cp.wait()

    return pl.pallas_call(
        body, name=name,
        out_shape=[jax.ShapeDtypeStruct(s, dt) for s, dt in side.outs],
        in_specs=[pl.BlockSpec(memory_space=pl.ANY)] * n_i,
        out_specs=[pl.BlockSpec(memory_space=pl.ANY)] * n_o,
        scratch_shapes=side.sems(),
        input_output_aliases=dict(side.alias),
    )(*side.ins)


_WGEO = {"a_in": ("col", 1024, 1280), "a_out": ("row", 256, 1024), "b_in": ("row", 1024, 1552), "b_out": ("row", 512, 1024),
         "f0_in": ("col", 1024, 1408), "f1_in": ("col", 1024, 1408), "f0_out": ("row", 704, 1024), "f1_out": ("row", 704, 1024)}
_GGEO = dict(_WGEO, b_in=("row", 1552, 1024))


def _full_shape(geo):
    kind, r, cw = geo
    return (r, 4 * cw) if kind == "col" else (4 * r, cw)


def _times(i, step):
    return i * step if isinstance(i, int) else pl.multiple_of(i * step, step & -step)


def _win(ref, geo, j, h):
    kind, r, cw = geo
    hr = r // 2
    if kind == "col":
        return ref.at[pl.ds(_times(h, hr), hr), pl.ds(_times(j, cw), cw)]
    return ref.at[pl.ds(_times(2 * j + h, hr), hr), :]


def _half(ref, geo, h):
    hr = geo[1] // 2
    return ref.at[pl.ds(_times(h, hr), hr), :]


def _gather_side(first, second, sh, full):
    n1 = len(first)

    def plan(ins, outs):
        x, y, c = _here()
        m = 2 * x + y
        remote, local = [], []
        for u, src, dst_full in zip(first, ins[:n1], outs[:n1]):
            mine, dst = _half(src, _WGEO[u], c), _win(dst_full, _WGEO[u], m, c)
            local.append((mine, dst))
            remote.append((mine, dst, (x, y, 1 - c)))
            for k in (1, 2, 3):
                t = (m + k) % 4
                remote.append((mine, dst, (t // 2, t % 2, c)))
        for u, buf in zip(second, outs[n1:]):
            for k in (1, 2, 3):
                w_ = _win(buf, _WGEO[u], (m + k) % 4, c)
                remote.append((w_, w_, (x, y, 1 - c)))
        return remote, local

    return _Side([sh[u] for u in first] + [full[u] for u in second],
                 [(_full_shape(_WGEO[u]), BF) for u in first + second], plan, 4 * n1 + 3 * len(second), n1,
                 alias={n1 + i: n1 + i for i in range(len(second))})


def _pair_side(units, G):
    def plan(ins, outs):
        x, y, c = _here()
        return [(_win(gr, _GGEO[u], j, 1 - c), got.at[j], (x, y, 1 - c))
                for u, gr, got in zip(units, ins, outs) for j in range(4)], []

    return _Side([G[u] for u in units], [((4, _GGEO[u][1] // 2, _GGEO[u][2]), F32) for u in units], plan, 4 * len(units), 0)


def _pair_sums(units, G, gots):
    out = {}
    for u, got in zip(units, gots):
        blk = got.shape[1:]
        at = (lambda j: (lax.axis_index("c"), j)) if _GGEO[u][0] == "col" else (lambda j: (2 * j + lax.axis_index("c"), 0))
        slab = lambda j: (j, 0, 0)
        out[u] = _vcall(f"g_pair_sum_{u}", lambda a, b: (a + b[0])[None], (4,),
                        [(G[u], blk, at), (got, (1,) + blk, slab)], [(got.shape, BF, (1,) + blk, slab)])
    return out


def _chips_side(units, chip_sums):
    def plan(ins, outs):
        x, y, c = _here()
        m = 2 * x + y
        remote = []
        for s, got in zip(ins, outs):
            for k in (1, 2, 3):
                t = (m + k) % 4
                remote.append((s.at[t], got.at[k - 1], (t // 2, t % 2, c)))
        return remote, []

    return _Side([chip_sums[u] for u in units], [((3,) + chip_sums[u].shape[1:], BF) for u in units], plan,
                 3 * len(units), 0)


def _chip_sum(name, geo, chip_sums, got):
    _, r, cw = geo
    blk = (r // 2, cw)
    return _vcall(name, lambda a, b: ((a[0].astype(F32) + b[0].astype(F32)) + b[1].astype(F32)) + b[2].astype(F32), (1,),
                  [(chip_sums, (1,) + blk, lambda i: (2 * lax.axis_index("x") + lax.axis_index("y"), 0, 0)),
                   (got, (3,) + blk, lambda i: (0, 0, 0))],
                  [((r, cw), F32, blk, lambda i: (lax.axis_index("c"), 0))])


def _halves_side(units, shards):
    def plan(ins, outs):
        x, y, c = _here()
        return [(_half(o, _GGEO[u], c), _half(o, _GGEO[u], c), (x, y, 1 - c)) for u, o in zip(units, outs)], []

    return _Side([shards[u] for u in units], [(shards[u].shape, F32) for u in units], plan, len(units), 0,
                 alias={i: i for i in range(len(units))})


def _adam(name, w, g, m, v):
    rows, cols = w.shape
    tb = _pick(rows, (256, 128, 64, 8))

    def fn(wv, gv, mv, vv):
        m2 = ADAM_B1 * mv + (1.0 - ADAM_B1) * gv
        v2 = ADAM_B2 * vv + (1.0 - ADAM_B2) * jnp.square(gv)
        m_hat = m2 / (1.0 - ADAM_B1 ** ADAM_STEP)
        v_hat = v2 / (1.0 - ADAM_B2 ** ADAM_STEP)
        return -ADAM_LR * (m_hat / (jnp.sqrt(v_hat) + ADAM_EPS) + ADAM_WD * wv), m2, v2

    at = lambda i: (i, 0)
    return _vcall(name, fn, (rows // tb,), [(a, (tb, cols), at) for a in (w, g, m, v)],
                  [((rows, cols), F32, (tb, cols), at)] * 3)


def _pack(arrays, width, row_multiple, dtype):
    parts, offs, at = [], [], 0
    for a in arrays:
        flat = a.reshape(-1).astype(dtype)
        rows = -(-flat.shape[0] // (width * row_multiple)) * row_multiple
        parts.append(jnp.pad(flat, (0, rows * width - flat.shape[0])).reshape(rows, width))
        offs.append(at)
        at += rows
    return jnp.concatenate(parts, axis=0), offs


def _unpack(flat, shapes, offs):
    out = []
    for shp, at in zip(shapes, offs):
        n = 1
        for s in shp:
            n *= s
        rows = -(-n // flat.shape[1])
        out.append(flat[at:at + rows].reshape(-1)[:n].reshape(shp))
    return out


_BIG = ("a_w_in", "a_w_out", "b_w_in", "b_w_out", "ffn_w_in", "ffn_w_out")
_SMALL_SPLIT = ("b_conv_w", "b_conv_b", "b_norm_w", "ffn_conv_w")
_SMALL = ("norm1_w", "norm2_w", "a_lb_logits", "a_norm_w", "b_conv_w", "b_conv_b", "b_dt_bias", "b_a_log", "b_d_skip",
          "b_norm_w", "ffn_conv_w", "ffn_conv_b", "final_norm_w")
_ORDER = ("norm1_w", "norm2_w", "a_w_in", "a_lb_logits", "a_norm_w", "a_w_out", "b_w_in", "b_conv_w", "b_conv_b", "b_dt_bias",
          "b_a_log", "b_d_skip", "b_norm_w", "b_w_out", "ffn_w_in", "ffn_conv_w", "ffn_conv_b", "ffn_w_out", "final_norm_w")


def kernel(x, norm1_w, norm2_w, a_w_in, a_lb_logits, a_norm_w, a_w_out, b_w_in, b_conv_w, b_conv_b, b_dt_bias, b_a_log, b_d_skip, b_norm_w, b_w_out, ffn_w_in, ffn_conv_w, ffn_conv_b, ffn_w_out, final_norm_w, loss_target, m_norm1_w, m_norm2_w, m_a_w_in, m_a_lb_logits, m_a_norm_w, m_a_w_out, m_b_w_in, m_b_conv_w, m_b_conv_b, m_b_dt_bias, m_b_a_log, m_b_d_skip, m_b_norm_w, m_b_w_out, m_ffn_w_in, m_ffn_conv_w, m_ffn_conv_b, m_ffn_w_out, m_final_norm_w, v_norm1_w, v_norm2_w, v_a_w_in, v_a_lb_logits, v_a_norm_w, v_a_w_out, v_b_w_in, v_b_conv_w, v_b_conv_b, v_b_dt_bias, v_b_a_log, v_b_d_skip, v_b_norm_w, v_b_w_out, v_ffn_w_in, v_ffn_conv_w, v_ffn_conv_b, v_ffn_w_out, v_final_norm_w):
    w = dict(norm1_w=norm1_w, norm2_w=norm2_w, a_w_in=a_w_in, a_lb_logits=a_lb_logits, a_norm_w=a_norm_w, a_w_out=a_w_out,
             b_w_in=b_w_in, b_conv_w=b_conv_w, b_conv_b=b_conv_b, b_dt_bias=b_dt_bias, b_a_log=b_a_log, b_d_skip=b_d_skip,
             b_norm_w=b_norm_w, b_w_out=b_w_out, ffn_w_in=ffn_w_in, ffn_conv_w=ffn_conv_w, ffn_conv_b=ffn_conv_b,
             ffn_w_out=ffn_w_out, final_norm_w=final_norm_w)
    mom = dict(norm1_w=m_norm1_w, norm2_w=m_norm2_w, a_w_in=m_a_w_in, a_lb_logits=m_a_lb_logits, a_norm_w=m_a_norm_w,
               a_w_out=m_a_w_out, b_w_in=m_b_w_in, b_conv_w=m_b_conv_w, b_conv_b=m_b_conv_b, b_dt_bias=m_b_dt_bias,
               b_a_log=m_b_a_log, b_d_skip=m_b_d_skip, b_norm_w=m_b_norm_w, b_w_out=m_b_w_out, ffn_w_in=m_ffn_w_in,
               ffn_conv_w=m_ffn_conv_w, ffn_conv_b=m_ffn_conv_b, ffn_w_out=m_ffn_w_out, final_norm_w=m_final_norm_w)
    var = dict(norm1_w=v_norm1_w, norm2_w=v_norm2_w, a_w_in=v_a_w_in, a_lb_logits=v_a_lb_logits, a_norm_w=v_a_norm_w,
               a_w_out=v_a_w_out, b_w_in=v_b_w_in, b_conv_w=v_b_conv_w, b_conv_b=v_b_conv_b, b_dt_bias=v_b_dt_bias,
               b_a_log=v_b_a_log, b_d_skip=v_b_d_skip, b_norm_w=v_b_norm_w, b_w_out=v_b_w_out, ffn_w_in=v_ffn_w_in,
               ffn_conv_w=v_ffn_conv_w, ffn_conv_b=v_ffn_conv_b, ffn_w_out=v_ffn_w_out, final_norm_w=v_final_norm_w)
    chip = 2 * lax.axis_index("x") + lax.axis_index("y")

    sh = {"a_in": a_w_in[0], "a_out": a_w_out[0], "b_in": b_w_in[0], "b_out": b_w_out[0], "f0_in": ffn_w_in[0],
          "f1_in": ffn_w_in[1], "f0_out": ffn_w_out[0], "f1_out": ffn_w_out[1]}
    sh = {u: a.astype(BF) for u, a in sh.items()}
    small_shapes = [w[n].shape for n in _SMALL_SPLIT]
    spack, small_offs = _pack([w[n] for n in _SMALL_SPLIT], 128, 8, F32)
    sall = _allgather8("s_gather", spack)
    sshards = [_unpack(sall[2 * j], small_shapes, small_offs) for j in range(4)]
    sfull = {n: jnp.concatenate([sshards[j][i] for j in range(4)], axis=-1) for i, n in enumerate(_SMALL_SPLIT)}

    p = dict(
        norm1_w=norm1_w, norm2_w=norm2_w, a_lb_logits=a_lb_logits, a_norm_w=a_norm_w[0], final_norm_w=final_norm_w,
        b_conv_w=sfull["b_conv_w"][0], b_conv_b=sfull["b_conv_b"][0], b_norm_w=sfull["b_norm_w"],
        ffn_conv_w=sfull["ffn_conv_w"], ffn_conv_b=ffn_conv_b,
        b_dt_bias=jnp.pad(b_dt_bias.reshape(1, 2 * SSM_HEADS), ((0, 0), (0, 128 - 2 * SSM_HEADS))),
        b_a_log=jnp.pad(b_a_log.reshape(1, 2 * SSM_HEADS), ((0, 0), (0, 128 - 2 * SSM_HEADS))),
        b_d_skip=jnp.repeat(b_d_skip[0], SSM_HD)[None],
    )

    loss_row, dx, g, gs_ = _sequence_grads(x[0], loss_target[0], p, sh)
    grads = {"a_w_in": gs_["a_in"][None], "a_w_out": gs_["a_out"][None], "b_w_in": gs_["b_in"].T[None],
             "b_w_out": gs_["b_out"][None], "ffn_w_in": jnp.stack([gs_["f0_in"], gs_["f1_in"]]),
             "ffn_w_out": jnp.stack([gs_["f0_out"], gs_["f1_out"]])}

    gsmall = {
        "norm1_w": jnp.concatenate(g["norm1_w"], axis=0), "norm2_w": jnp.concatenate(g["norm2_w"], axis=0),
        "a_lb_logits": jnp.stack(g["a_lb_logits"]), "a_norm_w": g["a_norm_w"], "b_conv_w": g["b_conv_w"],
        "b_conv_b": g["b_conv_b"], "b_dt_bias": g["b_dt_bias"], "b_a_log": g["b_a_log"], "b_d_skip": g["b_d_skip"],
        "b_norm_w": g["b_norm_w"], "ffn_conv_w": jnp.stack(g["ffn_conv_w"]),
        "ffn_conv_b": jnp.concatenate(g["ffn_conv_b"], axis=0), "final_norm_w": g["final_norm_w"],
    }
    pieces = [gsmall[n] for n in _SMALL] + [loss_row]
    piece_shapes = [a.shape for a in pieces]
    gspack, gs_offs = _pack(pieces, 128, 8, F32)
    rows = gspack.shape[0]
    gsall = _allgather8("gs_gather", gspack)

    def sum8(a):
        r = a[0]
        for i in range(1, 8):
            r = r + a[i]
        return r

    gssum = _vcall("gs_sum", sum8, (1,), [(gsall, (8, rows, 128), lambda i: (0, 0, 0))],
                   [((rows, 128), F32, (rows, 128), lambda i: (0, 0))])
    gs = dict(zip(_SMALL + ("loss",), _unpack(gssum, piece_shapes, gs_offs)))
    loss = gs["loss"][0, 0]
    lb2 = gs["a_lb_logits"]
    small_grads = {
        "norm1_w": gs["norm1_w"], "norm2_w": gs["norm2_w"], "a_lb_logits": lb2[0] + lb2[1], "a_norm_w": gs["a_norm_w"],
        "b_dt_bias": gs["b_dt_bias"][:, :2 * SSM_HEADS].reshape(1, 2, SSM_HEADS),
        "b_a_log": gs["b_a_log"][:, :2 * SSM_HEADS].reshape(1, 2, SSM_HEADS),
        "b_d_skip": gs["b_d_skip"].reshape(1, SSM_HEADS, SSM_HD).sum(axis=-1),
        "ffn_conv_b": gs["ffn_conv_b"], "final_norm_w": gs["final_norm_w"][0],
        "b_conv_w": gs["b_conv_w"][None], "b_conv_b": gs["b_conv_b"], "b_norm_w": gs["b_norm_w"], "ffn_conv_w": gs["ffn_conv_w"],
    }
    for n in _SMALL_SPLIT:
        width = w[n].shape[-1]
        small_grads[n] = lax.dynamic_slice_in_dim(small_grads[n], chip * width, width, axis=small_grads[n].ndim - 1)
    grads.update(small_grads)

    delta, new_m, new_v = {}, {}, {}
    for n in _BIG:
        shp = w[n].shape
        two_d = (shp[0] * shp[1], shp[2])
        d_, m_, v_ = _adam(f"adam_{n}", w[n].reshape(two_d), grads[n].reshape(two_d), mom[n].reshape(two_d), var[n].reshape(two_d))
        delta[n], new_m[n], new_v[n] = d_.reshape(shp), m_.reshape(shp), v_.reshape(shp)
    s_shapes = [w[n].shape for n in _SMALL]
    packs = [_pack([src[n] for n in _SMALL], 128, 8, F32) for src in (w, grads, mom, var)]
    outs = _adam("adam_small", *[pk[0] for pk in packs])
    for res, dst in zip(outs, (delta, new_m, new_v)):
        dst.update(dict(zip(_SMALL, _unpack(res, s_shapes, packs[0][1]))))

    return (loss, dx[None], *[grads[n] for n in _ORDER], *[delta[n] for n in _ORDER],
            *[new_m[n] for n in _ORDER], *[new_v[n] for n in _ORDER])
```

```python
import functools

import jax
import jax.numpy as jnp
from jax import lax
from jax.experimental import pallas as pl
from jax.experimental.pallas import tpu as pltpu

F32, BF = jnp.float32, jnp.bfloat16

D = 1024
EPS = 1e-6
HG_HEADS, HG_HD, HG_CHUNK, HG_SUB = 8, 128, 64, 8
HG_HB = 8
HG_CPS = 1
SSM_GB = 8
D_INNER, SSM_HEADS, SSM_HD, SSM_GROUPS, SSM_HPG, SSM_N, SSD_CHUNK = 2048, 32, 64, 8, 4, 128, 128
CONV_DIM = D_INNER + 2 * SSM_GROUPS * SSM_N
B_PROJ = 2 * D_INNER + 2 * SSM_GROUPS * SSM_N + 2 * SSM_HEADS
B_PROJ_PAD = 6272
D_FF = 2816
NEG = -1e30
ROWS = 512
GATE_ROWS = 1024
VMEM_LIMIT = 56 * 1024 * 1024

ADAM_LR, ADAM_B1, ADAM_B2, ADAM_EPS, ADAM_WD, ADAM_STEP = 0.001, 0.9, 0.999, 1e-08, 0.01, 10

MESH = pl.DeviceIdType.MESH


def _pick(n, cands):
    for c in cands:
        if n % c == 0:
            return c
    return n


class _Side:
    def __init__(self, ins, outs, plan, n_remote, n_local, alias=None):
        self.ins, self.outs, self.plan, self.n_remote, self.n_local = list(ins), list(outs), plan, n_remote, n_local
        self.alias = alias or {}

    def copies(self, in_refs, out_refs, send_sems, recv_sems, local_sems):
        remote, local = self.plan(in_refs, out_refs)
        cps = [pltpu.make_async_copy(s, d, local_sems.at[i]) for i, (s, d) in enumerate(local)]
        cps += [pltpu.make_async_remote_copy(src_ref=s, dst_ref=d, send_sem=send_sems.at[i], recv_sem=recv_sems.at[i],
                                             device_id=dev, device_id_type=MESH)
                for i, (s, d, dev) in enumerate(remote)]
        return cps

    def sems(self):
        return [pltpu.SemaphoreType.DMA((self.n_remote,)), pltpu.SemaphoreType.DMA((self.n_remote,)),
                pltpu.SemaphoreType.DMA((max(self.n_local, 1),))]


def _vcall(name, fn, grid, ins, outs, acc=None, scratch=(), side=None):
    acc = acc or {}
    n_in, n_out, nd = len(ins), len(outs), len(grid)
    n_sin = len(side.ins) if side else 0
    n_sout = len(side.outs) if side else 0
    n_scr = len(scratch)

    def body(*refs):
        in_refs, refs = refs[:n_in], refs[n_in:]
        sin_refs, refs = refs[:n_sin], refs[n_sin:]
        out_refs, refs = refs[:n_out], refs[n_out:]
        sout_refs, refs = refs[:n_sout], refs[n_sout:]
        scr, sems = refs[:n_scr], refs[n_scr:]
        if side:
            at_first, at_last = None, None
            for ax in range(nd):
                f, l = pl.program_id(ax) == 0, pl.program_id(ax) == grid[ax] - 1
                at_first = f if at_first is None else jnp.logical_and(at_first, f)
                at_last = l if at_last is None else jnp.logical_and(at_last, l)

            @pl.when(at_first)
            def _():
                for cp in side.copies(sin_refs, sout_refs, *sems):
                    cp.start()

        res = fn(*[r[...] for r in in_refs], *scr)
        if not isinstance(res, (tuple, list)):
            res = (res,)
        if side:
            @pl.when(at_last)
            def _():
                for cp in side.copies(sin_refs, sout_refs, *sems):
                    cp.wait()
        for j, (o_ref, r) in enumerate(zip(out_refs, res)):
            mode = acc.get(j)
            if mode is None:
                o_ref[...] = r.astype(o_ref.dtype)
                continue
            first = pl.program_id(nd - 1) == 0
            if mode == "all":
                for ax in range(nd - 1):
                    first = jnp.logical_and(first, pl.program_id(ax) == 0)

            @pl.when(first)
            def _():
                o_ref[...] = r.astype(o_ref.dtype)

            @pl.when(jnp.logical_not(first))
            def _():
                o_ref[...] += r.astype(o_ref.dtype)

    hbm = pl.BlockSpec(memory_space=pl.ANY)
    in_specs = [pl.BlockSpec(bs, im) for _, bs, im in ins] + [hbm] * n_sin
    out_specs = [pl.BlockSpec(bs, im) for _, _, bs, im in outs] + [hbm] * n_sout
    params = pltpu.CompilerParams(dimension_semantics=("arbitrary",) * nd, vmem_limit_bytes=VMEM_LIMIT)
    out_shape = [jax.ShapeDtypeStruct(s, dt) for s, dt, _, _ in outs]
    operands = [a for a, _, _ in ins]
    scratch = list(scratch)
    aliases = {}
    if side:
        out_shape += [jax.ShapeDtypeStruct(s, dt) for s, dt in side.outs]
        operands += side.ins
        scratch += side.sems()
        aliases = {n_in + i: n_out + o for i, o in side.alias.items()}
    out = pl.pallas_call(body, name=name, grid=grid, in_specs=in_specs, out_specs=out_specs, out_shape=out_shape,
                         scratch_shapes=scratch, compiler_params=params, input_output_aliases=aliases)(*operands)
    if side:
        return tuple(out[:n_out]), tuple(out[n_out:])
    return out[0] if n_out == 1 else out


def _mm(name, a, b, kind, out_dtype=F32, add=None, side=None):
    if kind == "tn":
        m, k = a.shape
        _, n = b.shape
        tm = _pick(m, (2048, 1024, 512, 256))
        tk = _pick(k, (1024, 1408, 896, 512, 256, 128))
        tn = _pick(n, (1024, 1408, 896, 512, 256, 128))

        def fn(av, bv):
            return lax.dot_general(av.astype(BF), bv.astype(BF), (((0,), (0,)), ((), ())),
                                   preferred_element_type=F32)

        return _vcall(name, fn, (k // tk, n // tn, m // tm),
                      [(a, (tm, tk), lambda i, j, s: (s, i)), (b, (tm, tn), lambda i, j, s: (s, j))],
                      [((k, n), F32, (tk, tn), lambda i, j, s: (i, j))], acc={0: "last"}, side=side)
    m, k = a.shape
    n = b.shape[1] if kind == "nn" else b.shape[0]
    long_k = k > 4096
    tm = _pick(m, (1024, 512, 256))
    tn = _pick(n, (512, 896, 256, 128)) if long_k else _pick(n, (1024, 1408, 896, 512, 256, 128))
    dims = (((1,), (0,)), ((), ())) if kind == "nn" else (((1,), (1,)), ((), ()))

    def fn(av, bv, *rest):
        r = lax.dot_general(av.astype(BF), bv.astype(BF), dims, preferred_element_type=F32)
        return r + rest[0] if rest else r

    ins = [(a, (tm, k), lambda i, j: (i, 0)),
           (b, (k, tn), lambda i, j: (0, j)) if kind == "nn" else (b, (tn, k), lambda i, j: (j, 0))]
    if add is not None:
        ins.append((add, (tm, tn), lambda i, j: (i, j)))
    return _vcall(name, fn, (m // tm, n // tn), ins, [((m, n), out_dtype, (tm, tn), lambda i, j: (i, j))], side=side)


def _rms(h, w):
    return h * lax.rsqrt(jnp.mean(h * h, axis=-1, keepdims=True) + EPS) * w


def _rms_fwd(name, h, w):
    L = h.shape[0]
    tb = _pick(L, (ROWS,))
    return _vcall(name, _rms, (L // tb,),
                  [(h, (tb, D), lambda i: (i, 0)), (w.reshape(1, D), (1, D), lambda i: (0, 0))],
                  [((L, D), BF, (tb, D), lambda i: (i, 0))])


def _rms_bwd(name, du, h, w, dh_next, side=None):
    L = h.shape[0]
    tb = _pick(L, (ROWS,))

    def fn(duv, hv, wv, dnv):
        _, vjp = jax.vjp(_rms, hv, wv)
        dh, dw = vjp(duv)
        return dh + dnv, dw, dh + dnv

    row = lambda i: (i, 0)
    return _vcall(name, fn, (L // tb,),
                  [(du, (tb, D), row), (h, (tb, D), row), (w.reshape(1, D), (1, D), lambda i: (0, 0)),
                   (dh_next, (tb, D), row)],
                  [((L, D), F32, (tb, D), row), ((1, D), F32, (1, D), lambda i: (0, 0)), ((L, D), BF, (tb, D), row)],
                  acc={1: "all"}, side=side)


def _loss_head(name, h, tgt, w):
    L = h.shape[0]
    tb = _pick(L, (ROWS,))

    def lossf(hv, wv, tv):
        err = _rms(hv, wv) - tv
        return 0.5 * jnp.sum(err * err) * (1.0 / D)

    def fn(hv, wv, tv):
        val, vjp = jax.vjp(lambda a, b: lossf(a, b, tv), hv, wv)
        dh, dw = vjp(jnp.ones((), F32))
        return jnp.full((1, 128), val, F32), dh, dw, dh

    row = lambda i: (i, 0)
    zero = lambda i: (0, 0)
    return _vcall(name, fn, (L // tb,),
                  [(h, (tb, D), row), (w.reshape(1, D), (1, D), zero), (tgt, (tb, D), row)],
                  [((1, 128), F32, (1, 128), zero), ((L, D), F32, (tb, D), row), ((1, D), F32, (1, D), zero),
                   ((L, D), BF, (tb, D), row)],
                  acc={0: "all", 2: "all"})


def _bf(x):
    return x.astype(BF)


def _dot(a, b, dims):
    return lax.dot_general(a, b, (dims, ((), ())), preferred_element_type=F32)


def _tri(n, reverse):
    r = lax.broadcasted_iota(jnp.int32, (n, n), 0)
    c = lax.broadcasted_iota(jnp.int32, (n, n), 1)
    return (r <= c) if reverse else (r >= c)


def _tri_matmul(n, reverse, x):
    hi = x.astype(BF)
    r1 = x - hi.astype(F32)
    mid = r1.astype(BF)
    lo = (r1 - mid.astype(F32)).astype(BF)
    y = _dot(_tri(n, reverse).astype(BF), jnp.concatenate([hi, mid, lo], axis=1), ((1,), (0,)))
    w = x.shape[1]
    return (y[:, :w] + y[:, w:2 * w]) + y[:, 2 * w:]


@functools.partial(jax.custom_vjp, nondiff_argnums=(0, 1))
def _running_sum(n, reverse, x):
    return _tri_matmul(n, reverse, x)


def _running_sum_fwd(n, reverse, x):
    return _tri_matmul(n, reverse, x), None


def _running_sum_bwd(n, reverse, _, ct):
    return (_tri_matmul(n, not reverse, ct),)


_running_sum.defvjp(_running_sum_fwd, _running_sum_bwd)


def _gla_chunk(q_raw, f_raw, v, lb3, S, reverse):
    C, SB, HD = HG_CHUNK, HG_SUB, HG_HD
    H = S.shape[0]
    heads = [slice(HD * h, HD * (h + 1)) for h in range(H)]
    row3 = lax.broadcasted_iota(jnp.int32, (3, 1), 0)
    e = jnp.exp(lb3 - jnp.max(lb3, axis=0, keepdims=True))
    lb = jnp.sum(jnp.where(row3 == 0, e, 0.0), axis=0, keepdims=True) / jnp.sum(e, axis=0, keepdims=True)
    q = q_raw * jax.nn.sigmoid(q_raw)
    f = lb + (1.0 - lb) * jax.nn.sigmoid(f_raw)
    g = jnp.log(f)
    k = 1.0 - f
    b = _running_sum(C, reverse, g)
    row = lax.broadcasted_iota(jnp.int32, (C, 1), 0)
    vb = _bf(v)

    def rowof(x, t):
        return jnp.sum(jnp.where(row == t, x, 0.0), axis=0, keepdims=True)

    qe = _bf(q * jnp.exp(b))
    o = [_dot(qe[:, hs], _bf(S[h]), ((1,), (0,))) for h, hs in enumerate(heads)]
    att = [[] for _ in range(H)]
    for i in range(C // SB):
        lo = SB * i
        if (not reverse and i == 0) or (reverse and i == C // SB - 1):
            for h in range(H):
                att[h].append(jnp.zeros((SB, C), F32))
            continue
        first = lo + SB - 1 if reverse else lo
        r = rowof(b, first) - rowof(g, first)
        before = (row >= lo + SB) if reverse else (row < lo)
        qi = q[lo:lo + SB] * jnp.exp(b[lo:lo + SB] - r)
        kk = _bf(k * jnp.exp(jnp.where(before, r - b, NEG)))
        for h, hs in enumerate(heads):
            att[h].append(_dot(_bf(qi[:, hs]), kk[:, hs], ((1,), (1,))))
    o = [o[h] + _dot(_bf(jnp.concatenate(att[h], axis=0)), vb[:, hs], ((1,), (0,))) for h, hs in enumerate(heads)]
    s_i = lax.broadcasted_iota(jnp.int32, (SB, SB, HD), 0)
    t_i = lax.broadcasted_iota(jnp.int32, (SB, SB, HD), 1)
    pair = (t_i <= s_i) if reverse else (t_i >= s_i)
    shp = (SB, SB, HD)
    diag = [[] for _ in range(H)]
    for i in range(C // SB):
        rows = slice(SB * i, SB * (i + 1))
        for h, hs in enumerate(heads):
            qb, kb, bb = q[rows, hs], k[rows, hs], b[rows, hs]
            dif = lax.broadcast_in_dim(bb, shp, (1, 2)) - lax.broadcast_in_dim(bb, shp, (0, 2))
            w = lax.broadcast_in_dim(qb, shp, (1, 2)) * jnp.exp(jnp.where(pair, dif, NEG)) * lax.broadcast_in_dim(kb, shp, (0, 2))
            d = jnp.sum(w, axis=2, keepdims=True)
            diag[h].append(jnp.sum(d * lax.broadcast_in_dim(v[rows, hs], shp, (0, 2)), axis=0))
    o = jnp.concatenate([o[h] + jnp.concatenate(diag[h], axis=0) for h in range(H)], axis=1)
    btot = rowof(b, 0 if reverse else C - 1)
    kd = _bf(k * jnp.exp(btot - b))
    eye = lax.broadcasted_iota(jnp.int32, (HD, HD), 0) == lax.broadcasted_iota(jnp.int32, (HD, HD), 1)
    s_new = []
    for h, hs in enumerate(heads):
        btot_col = jnp.sum(jnp.where(eye, btot[:, hs], 0.0), axis=1, keepdims=True)
        s_new.append((jnp.exp(btot_col) * S[h] + _dot(kd[:, hs], vb[:, hs], ((0,), (0,))))[None])
    return o, jnp.concatenate(s_new, axis=0)


def _gla_fwd(name, pa, lbl, reverse, side=None):
    L = pa.shape[0]
    C, cs = HG_CHUNK, HG_CPS
    nc, ns = L // C, L // (C * cs)
    cidx = (lambda i: ns - 1 - i) if reverse else (lambda i: i)
    sec = 2 if reverse else 1
    hb_n, nh = HG_HB, HG_HEADS // HG_HB

    def fn(qr, fr, v, lb3, s_ref):
        @pl.when(pl.program_id(1) == 0)
        def _():
            s_ref[...] = jnp.zeros_like(s_ref)

        s = s_ref[...]
        outs, olds = [None] * cs, [None] * cs
        for c in (reversed(range(cs)) if reverse else range(cs)):
            rows = slice(C * c, C * (c + 1))
            olds[c] = s[None]
            outs[c], s = _gla_chunk(qr[rows], fr[rows], v[rows], lb3, s, reverse)
        s_ref[...] = s
        return jnp.concatenate(outs, axis=0), jnp.concatenate(olds, axis=0)

    blk = (C * cs, HG_HD * hb_n)
    return _vcall(name, fn, (nh, ns),
                  [(pa, blk, lambda h, i: (cidx(i), h)), (pa, blk, lambda h, i: (cidx(i), sec * nh + h)),
                   (pa, blk, lambda h, i: (cidx(i), 3 * nh + h)), (lbl, (3, HG_HD * hb_n), lambda h, i: (0, h))],
                  [((L, D), F32, blk, lambda h, i: (cidx(i), h)),
                   ((nc, HG_HEADS, HG_HD, HG_HD), F32, (cs, hb_n, HG_HD, HG_HD), lambda h, i: (cidx(i), h, 0, 0))],
                  scratch=[pltpu.VMEM((hb_n, HG_HD, HG_HD), F32)], side=side)


def _gla_bwd(name, pa, lbl, s_in, do, reverse, prev=None, side=None):
    L = pa.shape[0]
    C, cs = HG_CHUNK, HG_CPS
    ns = L // (C * cs)
    cidx = (lambda i: i) if reverse else (lambda i: ns - 1 - i)
    sec = 2 if reverse else 1
    n_prev = 0 if prev is None else 2
    hb_n, nh = HG_HB, HG_HEADS // HG_HB

    def fn(qr, fr, v, lb3, s, dov, *rest):
        ds_ref = rest[n_prev]

        @pl.when(pl.program_id(1) == 0)
        def _():
            ds_ref[...] = jnp.zeros_like(ds_ref)

        ds = ds_ref[...]
        parts, dlb = [None] * cs, None
        for c in (range(cs) if reverse else reversed(range(cs))):
            rows = slice(C * c, C * (c + 1))
            _, vjp = jax.vjp(lambda *a: _gla_chunk(*a, reverse), qr[rows], fr[rows], v[rows], lb3, s[c])
            dq, df, dv, dl, ds = vjp((dov[rows], ds))
            if n_prev:
                dq, dv = dq + rest[0][rows], dv + rest[1][rows]
            parts[c] = (dq, df, dv)
            dlb = dl if dlb is None else dlb + dl
        ds_ref[...] = ds
        return tuple(jnp.concatenate([p_[j] for p_ in parts], axis=0) for j in range(3)) + (dlb,)

    blk = (C * cs, HG_HD * hb_n)
    at = lambda h, i: (cidx(i), h)
    ins = [(pa, blk, at), (pa, blk, lambda h, i: (cidx(i), sec * nh + h)), (pa, blk, lambda h, i: (cidx(i), 3 * nh + h)),
           (lbl, (3, HG_HD * hb_n), lambda h, i: (0, h)),
           (s_in, (cs, hb_n, HG_HD, HG_HD), lambda h, i: (cidx(i), h, 0, 0)), (do, blk, at)]
    if prev is not None:
        ins += [(prev[0], blk, at), (prev[1], blk, at)]
    sum_dt = F32 if prev is None else BF
    return _vcall(name, fn, (nh, ns), ins,
                  [((L, D), sum_dt, blk, at), ((L, D), BF, blk, at), ((L, D), sum_dt, blk, at),
                   ((3, D), F32, (3, HG_HD * hb_n), lambda h, i: (0, h))],
                  acc={3: "last"}, scratch=[pltpu.VMEM((hb_n, HG_HD, HG_HD), F32)], side=side)


def _hgout(o_f, o_b, g, nw):
    o = o_f + o_b
    return _rms(o, nw) * (g * jax.nn.sigmoid(g))


def _hgout_fwd(name, o_f, o_b, pa, nw, side=None):
    L = o_f.shape[0]
    tb = _pick(L, (GATE_ROWS, ROWS))
    blk = (tb, HG_HD)
    at = lambda h, i: (i, h)
    return _vcall(name, _hgout, (HG_HEADS, L // tb),
                  [(o_f, blk, at), (o_b, blk, at), (pa, blk, lambda h, i: (i, 32 + h)),
                   (nw.reshape(1, HG_HD), (1, HG_HD), lambda h, i: (0, 0))],
                  [((L, D), BF, blk, at)], side=side)


def _hgout_bwd(name, o_f, o_b, pa, nw, dy, side=None):
    L = o_f.shape[0]
    tb = _pick(L, (GATE_ROWS, ROWS))

    def fn(ofv, obv, gv, nwv, dyv):
        _, vjp = jax.vjp(_hgout, ofv, obv, gv, nwv)
        do, _, dg, dnw = vjp(dyv)
        return do, dg, dnw

    blk = (tb, HG_HD)
    at = lambda h, i: (i, h)
    zero = lambda h, i: (0, 0)
    return _vcall(name, fn, (HG_HEADS, L // tb),
                  [(o_f, blk, at), (o_b, blk, at), (pa, blk, lambda h, i: (i, 32 + h)),
                   (nw.reshape(1, HG_HD), (1, HG_HD), zero), (dy, blk, at)],
                  [((L, D), F32, blk, at), ((L, D), BF, blk, at), ((1, HG_HD), F32, (1, HG_HD), zero)],
                  acc={2: "all"}, side=side)


def _shift(x, s):
    if s == 0:
        return x
    n = x.shape[0]
    t = lax.broadcasted_iota(jnp.int32, (n, 1), 0)
    if s > 0:
        return jnp.where(t >= s, pltpu.roll(x, s, 0), 0.0)
    return jnp.where(t < n + s, pltpu.roll(x, n + s, 0), 0.0)


def _conv(x, w, b):
    kk = w.shape[0]
    p = (kk - 1) // 2
    y = b
    for j in range(kk):
        y = y + w[j:j + 1] * _shift(x, p - j)
    return y


def _conv_bwd(x, w, dc):
    kk = w.shape[0]
    p = (kk - 1) // 2
    dx = None
    dws = []
    for j in range(kk):
        t = w[j:j + 1] * _shift(dc, j - p)
        dx = t if dx is None else dx + t
        dws.append(jnp.sum(dc * _shift(x, p - j), axis=0, keepdims=True))
    rows = lax.broadcasted_iota(jnp.int32, (kk, 1), 0)
    dw = None
    for j in range(kk):
        t = jnp.where(rows == j, dws[j], 0.0)
        dw = t if dw is None else dw + t
    return dx, dw, jnp.sum(dc, axis=0, keepdims=True)


def _silu(c):
    return c * jax.nn.sigmoid(c)


def _silu_grad(c):
    s = jax.nn.sigmoid(c)
    return s * (1.0 + c * (1.0 - s))


def _glu_fwd(name, pf, cw, cb):
    L = pf.shape[0]
    tc = 128
    nt = D_FF // tc
    return _vcall(name, lambda gate, val, w, b: _silu(_conv(gate, w, b)) * val, (nt,),
                  [(pf, (L, tc), lambda j: (0, j)), (pf, (L, tc), lambda j: (0, nt + j)),
                   (cw, (3, tc), lambda j: (0, j)), (cb.reshape(1, D_FF), (1, tc), lambda j: (0, j))],
                  [((L, D_FF), BF, (L, tc), lambda j: (0, j))])


def _glu_bwd(name, pf, cw, cb, dy, side=None):
    L = pf.shape[0]
    tc = 128
    nt = D_FF // tc

    def fn(gate, val, w, b, dyv):
        c = _conv(gate, w, b)
        dgate, dw, db = _conv_bwd(gate, w, dyv * val * _silu_grad(c))
        return dgate, dyv * _silu(c), dw, db

    col = lambda j: (0, j)
    return _vcall(name, fn, (nt,),
                  [(pf, (L, tc), col), (pf, (L, tc), lambda j: (0, nt + j)), (cw, (3, tc), col),
                   (cb.reshape(1, D_FF), (1, tc), col), (dy, (L, tc), col)],
                  [((L, D_FF), BF, (L, tc), col), ((L, D_FF), BF, (L, tc), col),
                   ((3, D_FF), F32, (3, tc), col), ((1, D_FF), F32, (1, tc), col)], side=side)


def _perm_tile(j):
    return jnp.where(j < 16, 4 * (j // 2) + j % 2, jnp.where(j < 24, 4 * (j - 16) + 2, 4 * (j - 24) + 3))


def _mpre_fwd(name, pb, cw, cb):
    L = pb.shape[0]
    tc = 128
    return _vcall(name, lambda x, w, b: _silu(_conv(x, w, b)), (CONV_DIM // tc,),
                  [(pb, (L, tc), lambda j: (0, 16 + j)), (cw, (5, tc), lambda j: (0, j)),
                   (cb.reshape(1, CONV_DIM), (1, tc), lambda j: (0, j))],
                  [((L, CONV_DIM), F32, (L, tc), lambda j: (0, _perm_tile(j)))])


def _mpre_bwd(name, pb, cw, cb, dact):
    L = pb.shape[0]
    tc = 128
    col = lambda j: (0, j)
    return _vcall(name, lambda x, w, b, da: _conv_bwd(x, w, da * _silu_grad(_conv(x, w, b))), (CONV_DIM // tc,),
                  [(pb, (L, tc), lambda j: (0, 16 + j)), (cw, (5, tc), col), (cb.reshape(1, CONV_DIM), (1, tc), col),
                   (dact, (L, tc), lambda j: (0, _perm_tile(j)))],
                  [((L, CONV_DIM), BF, (L, tc), col), ((5, CONV_DIM), F32, (5, tc), col),
                   ((1, CONV_DIM), F32, (1, tc), col)])


def _softplus(x):
    return jnp.maximum(x, 0.0) + jnp.log(1.0 + jnp.exp(-jnp.abs(x)))


def _dt_fwd(name, pb, dtb, alog):
    L = pb.shape[0]
    tb = _pick(L, (1024, ROWS))

    def fn(x, bias, al):
        dt = _softplus(x + bias)
        return dt, dt * (-jnp.exp(al))

    row = lambda i: (i, 0)
    zero = lambda i: (0, 0)
    return _vcall(name, fn, (L // tb,),
                  [(pb, (tb, 128), lambda i: (i, 48)), (dtb, (1, 128), zero), (alog, (1, 128), zero)],
                  [((L, 128), F32, (tb, 128), row), ((L, 128), F32, (tb, 128), row)])


def _dt_bwd(name, pb, dtb, alog, ddt_f, dla_f, ddt_b, dla_b):
    L = pb.shape[0]
    tb = _pick(L, (1024, ROWS))

    def fn(x, bias, al, a1, b1, a2, b2):
        ddt = jnp.sum(a1, axis=0) + jnp.sum(a2, axis=0)
        dla = jnp.sum(b1, axis=0) + jnp.sum(b2, axis=0)
        z = x + bias
        dt = _softplus(z)
        a = -jnp.exp(al)
        dz = (ddt + dla * a) * jax.nn.sigmoid(z)
        return dz, jnp.sum(dz, axis=0, keepdims=True), jnp.sum(dla * dt, axis=0, keepdims=True) * a

    zero = lambda i: (0, 0)
    g3 = (ddt_f.shape[0], tb, 128)
    at3 = lambda i: (0, i, 0)
    return _vcall(name, fn, (L // tb,),
                  [(pb, (tb, 128), lambda i: (i, 48)), (dtb, (1, 128), zero), (alog, (1, 128), zero),
                   (ddt_f, g3, at3), (dla_f, g3, at3), (ddt_b, g3, at3), (dla_b, g3, at3)],
                  [((L, 128), BF, (tb, 128), lambda i: (i, 0)), ((1, 128), F32, (1, 128), zero),
                   ((1, 128), F32, (1, 128), zero)], acc={1: "all", 2: "all"})


def _split_dot(x, e, dims, pieces):
    hi = x.astype(BF)
    r1 = x - hi.astype(F32)
    mid = r1.astype(BF)
    y = _dot(hi, e, dims) + _dot(mid, e, dims)
    if pieces == 3:
        y = y + _dot((r1 - mid.astype(F32)).astype(BF), e, dims)
    return y


@functools.partial(jax.custom_vjp, nondiff_argnums=(2,))
def _spread(x, e, pieces):
    return _split_dot(x, e, ((1,), (0,)), pieces)


def _spread_fwd(x, e, pieces):
    return _split_dot(x, e, ((1,), (0,)), pieces), e


def _spread_bwd(pieces, e, ct):
    return _split_dot(ct, e, ((1,), (1,)), pieces), jnp.zeros_like(e)


_spread.defvjp(_spread_fwd, _spread_bwd)


def _ssd_chunk(xa, dt, la, hs, head0, reverse):
    C, P4, HD, N = SSD_CHUNK, SSM_HPG * SSM_HD, SSM_HD, SSM_N
    G = hs.shape[0]
    nh = SSM_HPG * G
    row = lax.broadcasted_iota(jnp.int32, (C, 1), 0)
    lane = lax.broadcasted_iota(jnp.int32, (1, 128), 1)
    eye = lax.broadcasted_iota(jnp.int32, (C, C), 0) == lax.broadcasted_iota(jnp.int32, (C, C), 1)
    tri = _tri(C, reverse)
    last = 0 if reverse else C - 1
    acum = _running_sum(C, reverse, la)
    atot = jnp.sum(jnp.where(row == last, acum, 0.0), axis=0, keepdims=True)
    src = lax.broadcasted_iota(jnp.int32, (128, 1), 0) - head0
    to_x = (src == lax.broadcasted_iota(jnp.int32, (1, nh * HD), 1) // HD).astype(BF)
    dt_x = _spread(dt, to_x, 2)
    ea_x = _spread(jnp.exp(acum), to_x, 2)
    dec_x = _spread(jnp.exp(atot - acum), to_x, 2)
    col_head = lax.broadcasted_iota(jnp.int32, (1, P4), 1) // HD
    row_head = lax.broadcasted_iota(jnp.int32, (P4, 1), 0) // HD
    ys, news = [], []
    for gi in range(G):
        xs = xa[:, 512 * gi:512 * gi + P4]
        bm = _bf(xa[:, 512 * gi + P4:512 * gi + P4 + N])
        cm = _bf(xa[:, 512 * gi + P4 + N:512 * (gi + 1)])
        gx = slice(P4 * gi, P4 * (gi + 1))
        cb = _dot(cm, bm, ((1,), (1,)))
        xd = xs * dt_x[:, gx]
        ms, xds, scale = [], [], 0.0
        for j in range(SSM_HPG):
            i = SSM_HPG * gi + j
            ac = jnp.sum(jnp.where(lane == head0 + i, acum, 0.0), axis=1, keepdims=True)
            ac_row = jnp.sum(jnp.where(eye, ac, 0.0), axis=0, keepdims=True)
            ms.append(_bf(cb * jnp.exp(jnp.where(tri, ac - ac_row, NEG))))
            xds.append(_bf(jnp.where(col_head == j, xd, 0.0)))
            a_i = jnp.sum(jnp.where(lane == head0 + i, atot, 0.0), axis=1, keepdims=True)
            scale = scale + jnp.where(row_head == j, jnp.exp(a_i), 0.0)
        y = _dot(jnp.concatenate(ms, axis=1), jnp.concatenate(xds, axis=0), ((1,), (0,)))
        y = y + _dot(cm, _bf(hs[gi]), ((1,), (1,))) * ea_x[:, gx]
        ys.append(y)
        news.append((scale * hs[gi] + _dot(_bf(xd * dec_x[:, gx]), bm, ((0,), (0,))))[None])
    return jnp.concatenate(ys, axis=1), jnp.concatenate(news, axis=0)


def _ssd_fwd(name, xact, dt, la, reverse):
    L = xact.shape[0]
    C = SSD_CHUNK
    nc = L // C
    cidx = (lambda i: nc - 1 - i) if reverse else (lambda i: i)
    base = SSM_HEADS if reverse else 0
    P4 = SSM_HPG * SSM_HD

    gb_n = SSM_GB

    def fn(xa, dtv, lav, h_ref):
        @pl.when(pl.program_id(1) == 0)
        def _():
            h_ref[...] = jnp.zeros_like(h_ref)

        h_all = h_ref[...]
        y, h_new = _ssd_chunk(xa, dtv, lav, h_all, base + SSM_HPG * gb_n * pl.program_id(0), reverse)
        h_ref[...] = h_new
        return y, h_all[None]

    return _vcall(name, fn, (SSM_GROUPS // gb_n, nc),
                  [(xact, (C, 512 * gb_n), lambda g, i: (cidx(i), g)), (dt, (C, 128), lambda g, i: (cidx(i), 0)),
                   (la, (C, 128), lambda g, i: (cidx(i), 0))],
                  [((L, D_INNER), F32, (C, P4 * gb_n), lambda g, i: (cidx(i), g)),
                   ((nc, SSM_GROUPS, P4, SSM_N), F32, (1, gb_n, P4, SSM_N), lambda g, i: (cidx(i), g, 0, 0))],
                  scratch=[pltpu.VMEM((gb_n, P4, SSM_N), F32)])


def _ssd_bwd(name, xact, dt, la, h_in, dy, reverse, prev_xs=None, prev_all=None, side=None):
    L = xact.shape[0]
    C = SSD_CHUNK
    nc = L // C
    cidx = (lambda i: i) if reverse else (lambda i: nc - 1 - i)
    base = SSM_HEADS if reverse else 0
    P4 = SSM_HPG * SSM_HD

    gb_n = SSM_GB

    def fn(xa, dtv, lav, hs, dyv, pv, dh_ref):
        @pl.when(pl.program_id(1) == 0)
        def _():
            dh_ref[...] = jnp.zeros_like(dh_ref)

        head0 = base + SSM_HPG * gb_n * pl.program_id(0)
        _, vjp = jax.vjp(lambda a, b, c, d: _ssd_chunk(a, b, c, d, head0, reverse), xa, dtv, lav, hs[0])
        dxa, ddt, dla, dh = vjp((dyv, dh_ref[...]))
        dh_ref[...] = dh
        if prev_all is not None:
            dxa = dxa + pv
        else:
            zeros = jnp.zeros((C, 2 * SSM_N), F32)
            dxa = dxa + jnp.concatenate([t for gb in range(gb_n) for t in (pv[:, P4 * gb:P4 * (gb + 1)], zeros)], axis=1)
        return dxa, ddt[None], dla[None]

    at = lambda g, i: (cidx(i), g)
    at0 = lambda g, i: (cidx(i), 0)
    pv = (prev_all, (C, 512 * gb_n), at) if prev_all is not None else (prev_xs, (C, P4 * gb_n), at)
    steps = SSM_GROUPS // gb_n
    return _vcall(name, fn, (steps, nc),
                  [(xact, (C, 512 * gb_n), at), (dt, (C, 128), at0), (la, (C, 128), at0),
                   (h_in, (1, gb_n, P4, SSM_N), lambda g, i: (cidx(i), g, 0, 0)), (dy, (C, P4 * gb_n), at), pv],
                  [((L, CONV_DIM), F32, (C, 512 * gb_n), at),
                   ((steps, L, 128), F32, (1, C, 128), lambda g, i: (g, cidx(i), 0)),
                   ((steps, L, 128), F32, (1, C, 128), lambda g, i: (g, cidx(i), 0))],
                  scratch=[pltpu.VMEM((gb_n, P4, SSM_N), F32)], side=side)


def _mpost(y_f, y_b, xs, z, dsk, nw):
    y = (y_f + y_b + xs * dsk) * (z * jax.nn.sigmoid(z))
    return _rms(y, nw)


def _mpost_fwd(name, y_f, y_b, xact, pb, dsk, nw):
    L = y_f.shape[0]
    tb = _pick(L, (GATE_ROWS, ROWS))
    blk = (tb, 256)
    at = lambda g, i: (i, g)
    par = lambda g, i: (0, g)
    return _vcall(name, _mpost, (SSM_GROUPS, L // tb),
                  [(y_f, blk, at), (y_b, blk, at), (xact, blk, lambda g, i: (i, 2 * g)), (pb, blk, at),
                   (dsk, (1, 256), par), (nw, (1, 256), par)],
                  [((L, D_INNER), BF, blk, at)])


def _mpost_bwd(name, y_f, y_b, xact, pb, dsk, nw, dy):
    L = y_f.shape[0]
    tb = _pick(L, (GATE_ROWS, ROWS))

    def fn(yf, yb, xs, z, dskv, nwv, dyv):
        _, vjp = jax.vjp(_mpost, yf, yb, xs, z, dskv, nwv)
        dyf, _, dxs, dz, ddsk, dnw = vjp(dyv)
        return dyf, dxs, dz, ddsk, dnw

    blk = (tb, 256)
    at = lambda g, i: (i, g)
    par = lambda g, i: (0, g)
    return _vcall(name, fn, (SSM_GROUPS, L // tb),
                  [(y_f, blk, at), (y_b, blk, at), (xact, blk, lambda g, i: (i, 2 * g)), (pb, blk, at),
                   (dsk, (1, 256), par), (nw, (1, 256), par), (dy, blk, at)],
                  [((L, D_INNER), F32, blk, at), ((L, D_INNER), F32, blk, at), ((L, D_INNER), BF, blk, at),
                   ((1, D_INNER), F32, (1, 256), par), ((1, D_INNER), F32, (1, 256), par)],
                  acc={3: "last", 4: "last"})


def _ffn_fwd(tag, h, nw, w_in, cw, cb, w_out):
    u = _rms_fwd(f"{tag}_norm", h, nw)
    pf = _mm(f"{tag}_in", u, w_in, "nn")
    yf = _glu_fwd(f"{tag}_glu", pf, cw, cb)
    return _mm(f"{tag}_out", yf, w_out, "nn", add=h), (u, pf, yf)


def _ffn_bwd(tag, h, nw, w_in, cw, cb, w_out, saved, dh, dhb, side=None):
    u, pf, yf = saved
    d_w_out = _mm(f"{tag}_dwout", yf, dhb, "tn")
    dyf = _mm(f"{tag}_dy", dhb, w_out, "nt")
    dgate, dval, dcw, dcb = _glu_bwd(f"{tag}_dglu", pf, cw, cb, dyf)
    dpf = jnp.concatenate([dgate, dval], axis=1)
    d_w_in = _mm(f"{tag}_dwin", u, dpf, "tn")
    got = ()
    if side is None:
        du = _mm(f"{tag}_du", dpf, w_in, "nt")
    else:
        (du,), got = _mm(f"{tag}_du", dpf, w_in, "nt", side=side)
    dh_in, dnw, dh_in_b = _rms_bwd(f"{tag}_dnorm", du, h, nw, dh)
    return dh_in, dnw, d_w_in, dcw, dcb, d_w_out, got, dh_in_b


def _sequence_grads(x, tgt, p, sh):
    g = {}
    lbl = p["a_lb_logits"]
    first, early, mid, last = ("a_in", "a_out"), ("f0_in",), ("b_in", "b_out"), ("f1_in", "f1_out", "f0_out")
    W = {}
    got = _exchange("w_first", _gather_side(first, (), sh, W))
    W.update(zip(first, got))
    got = _exchange("w_first_pass", _gather_side((), first, sh, W))
    W.update(zip(first, got))
    u1 = _rms_fwd("a_norm", x, p["norm1_w"][0])
    (pa,), got = _mm("a_in", u1, W["a_in"], "nn", side=_gather_side(early, (), sh, W))
    W.update(zip(early, got))
    (o_f, s_f), got = _gla_fwd("a_scan_f", pa, lbl, False, side=_gather_side(mid, early, sh, W))
    W.update(zip(mid + early, got))
    (o_b, s_b), got = _gla_fwd("a_scan_b", pa, lbl, True, side=_gather_side(last, mid, sh, W))
    W.update(zip(last + mid, got))
    (ya,), got = _hgout_fwd("a_gate", o_f, o_b, pa, p["a_norm_w"], side=_gather_side((), last, sh, W))
    W.update(zip(last, got))
    wb4 = W["b_in"].reshape(4, D, B_PROJ // 4)
    p = dict(p, a_w_in=W["a_in"], a_w_out=W["a_out"], b_w_out=W["b_out"], ffn_w_in=(W["f0_in"], W["f1_in"]),
             ffn_w_out=(W["f0_out"], W["f1_out"]),
             b_w_in=jnp.pad(jnp.concatenate([wb4[j] for j in range(4)], axis=1), ((0, 0), (0, B_PROJ_PAD - B_PROJ))))
    h1 = _mm("a_out", ya, p["a_w_out"], "nn", add=x)
    h2, ffn0 = _ffn_fwd("f0", h1, p["norm2_w"][0], p["ffn_w_in"][0], p["ffn_conv_w"][0], p["ffn_conv_b"][0], p["ffn_w_out"][0])
    u3 = _rms_fwd("b_norm", h2, p["norm1_w"][1])
    pb = _mm("b_in", u3, p["b_w_in"], "nn")
    xact = _mpre_fwd("b_conv", pb, p["b_conv_w"], p["b_conv_b"])
    dt, la = _dt_fwd("b_dt", pb, p["b_dt_bias"], p["b_a_log"])
    y_f, hs_f = _ssd_fwd("b_scan_f", xact, dt, la, False)
    y_b, hs_b = _ssd_fwd("b_scan_b", xact, dt, la, True)
    yb = _mpost_fwd("b_gate", y_f, y_b, xact, pb, p["b_d_skip"], p["b_norm_w"])
    h3 = _mm("b_out", yb, p["b_w_out"], "nn", add=h2)
    h4, ffn1 = _ffn_fwd("f1", h3, p["norm2_w"][1], p["ffn_w_in"][1], p["ffn_conv_w"][1], p["ffn_conv_b"][1], p["ffn_w_out"][1])
    loss, dh4, g["final_norm_w"], dh4b = _loss_head("head", h4, tgt, p["final_norm_w"])
    dh3, dn2_1, dwin1, dcw1, dcb1, dwout1, _, dh3b = _ffn_bwd("f1", h3, p["norm2_w"][1], p["ffn_w_in"][1], p["ffn_conv_w"][1],
                                                              p["ffn_conv_b"][1], p["ffn_w_out"][1], ffn1, dh4, dh4b)
    G = {"f1_in": dwin1, "f1_out": dwout1}
    G["b_out"] = _mm("b_dwout", yb, dh3b, "tn")
    wave1 = ("f1_in", "f1_out", "b_out")
    (dyb,), got = _mm("b_dy", dh3b, p["b_w_out"], "nt", side=_pair_side(wave1, G))
    chip_sums = _pair_sums(wave1, G, got)
    dys, dxs, dz, g["b_d_skip"], g["b_norm_w"] = _mpost_bwd("b_dgate", y_f, y_b, xact, pb, p["b_d_skip"], p["b_norm_w"], dyb)
    dxa1, ddt_f, dla_f = _ssd_bwd("b_dscan_f", xact, dt, la, hs_f, dys, False, prev_xs=dxs)
    dxa, ddt_b, dla_b = _ssd_bwd("b_dscan_b", xact, dt, la, hs_b, dys, True, prev_all=dxa1)
    dxbc, g["b_conv_w"], g["b_conv_b"] = _mpre_bwd("b_dconv", pb, p["b_conv_w"], p["b_conv_b"], dxa)
    ddtr, g["b_dt_bias"], g["b_a_log"] = _dt_bwd("b_ddt", pb, p["b_dt_bias"], p["b_a_log"], ddt_f, dla_f, ddt_b, dla_b)
    dpb = jnp.concatenate([dz, dxbc, ddtr], axis=1)
    G["b_in"] = _mm("b_dwin", dpb, u3, "tn")
    du3 = _mm("b_du", dpb, p["b_w_in"], "nt")
    dh2, dn1_1, dh2b = _rms_bwd("b_dnorm", du3, h2, p["norm1_w"][1], dh3)
    dh1, dn2_0, G["f0_in"], dcw0, dcb0, G["f0_out"], got, dh1b = _ffn_bwd(
        "f0", h1, p["norm2_w"][0], p["ffn_w_in"][0], p["ffn_conv_w"][0], p["ffn_conv_b"][0], p["ffn_w_out"][0], ffn0,
        dh2, dh2b, side=_pair_side(("b_in",), G))
    chip_sums.update(_pair_sums(("b_in",), G, got))
    wave3 = ("f0_in", "f0_out")
    (G["a_out"],), got = _mm("a_dwout", ya, dh1b, "tn", side=_pair_side(wave3, G))
    chip_sums.update(_pair_sums(wave3, G, got))
    dya = _mm("a_dy", dh1b, p["a_w_out"], "nt")
    do, dg, g["a_norm_w"] = _hgout_bwd("a_dgate", o_f, o_b, pa, p["a_norm_w"], dya)
    late = last + mid + early
    (dq1, df1, dv1, dl1), got = _gla_bwd("a_dscan_f", pa, lbl, s_f, do, False, side=_chips_side(late, chip_sums))
    shards = {u: _chip_sum(f"gl_sum_{u}", _GGEO[u], chip_sums[u], r) for u, r in zip(late, got)}
    (dq, df2, dv, dl2), got = _gla_bwd("a_dscan_b", pa, lbl, s_b, do, True, prev=(dq1, dv1), side=_halves_side(late, shards))
    shards = dict(zip(late, got))
    dpa = jnp.concatenate([dq, df1, df2, dv, dg], axis=1)
    G["a_in"] = _mm("a_dwin", u1, dpa, "tn")
    chip_sums = _pair_sums(first, G, _exchange("ga_pair", _pair_side(first, G)))
    (du1,), got = _mm("a_du", dpa, p["a_w_in"], "nt", side=_chips_side(first, chip_sums))
    mine = {u: _chip_sum(f"ga_sum_{u}", _GGEO[u], chip_sums[u], r) for u, r in zip(first, got)}
    (dx, dn1_0, _), got = _rms_bwd("a_dnorm", du1, x, p["norm1_w"][0], dh1, side=_halves_side(first, mine))
    shards.update(zip(first, got))
    g["a_lb_logits"] = (dl1, dl2)
    g["norm1_w"] = (dn1_0, dn1_1)
    g["norm2_w"] = (dn2_0, dn2_1)
    g["ffn_conv_w"] = (dcw0, dcw1)
    g["ffn_conv_b"] = (dcb0, dcb1)
    return loss, dx, g, shards


def _here():
    return lax.axis_index("x"), lax.axis_index("y"), lax.axis_index("c")


def _allgather8(name, src):
    blk = src.shape

    def body(x_ref, out_ref, send_sems, recv_sems, local_sem):
        x, y, c = _here()
        me, sibling = (x, y, c), (x, y, 1 - c)
        chips = [(1 - x, y), (x, 1 - y), (1 - x, 1 - y)]
        own = x_ref

        def slot(px, py, pc):
            return out_ref.at[4 * px + 2 * py + pc]

        def copy(k, block, to, from_own=False):
            return pltpu.make_async_remote_copy(
                src_ref=own if from_own else slot(*block), dst_ref=slot(*block),
                send_sem=send_sems.at[k], recv_sem=recv_sems.at[k], device_id=to, device_id_type=MESH)

        mine = pltpu.make_async_copy(own, slot(*me), local_sem)
        mine.start()
        first = [copy(0, me, sibling, from_own=True)]
        first += [copy(1 + j, me, (*chip, c), from_own=True) for j, chip in enumerate(chips)]
        for cp in first:
            cp.start()
        passed = [copy(4 + j, (*chip, c), sibling) for j, chip in enumerate(chips)]
        for j, chip in enumerate(chips):
            copy(1 + j, (*chip, c), me).wait_recv()
            passed[j].start()
        copy(0, sibling, me).wait_recv()
        for j, chip in enumerate(chips):
            copy(4 + j, (*chip, 1 - c), me).wait_recv()
        for cp in first + passed:
            cp.wait_send()
        mine.wait()

    return pl.pallas_call(
        body, name=name,
        out_shape=jax.ShapeDtypeStruct((8,) + tuple(blk), src.dtype),
        in_specs=[pl.BlockSpec(memory_space=pl.ANY)],
        out_specs=pl.BlockSpec(memory_space=pl.ANY),
        scratch_shapes=[pltpu.SemaphoreType.DMA((7,)), pltpu.SemaphoreType.DMA((7,)), pltpu.SemaphoreType.DMA],
    )(src)


def _exchange(name, side):
    n_i, n_o = len(side.ins), len(side.outs)

    def body(*refs):
        copies = side.copies(refs[:n_i], refs[n_i:n_i + n_o], *refs[n_i + n_o:])
        for cp in copies:
            cp.start()
        for cp in copies:
            cp.wait()

    return pl.pallas_call(
        body, name=name,
        out_shape=[jax.ShapeDtypeStruct(s, dt) for s, dt in side.outs],
        in_specs=[pl.BlockSpec(memory_space=pl.ANY)] * n_i,
        out_specs=[pl.BlockSpec(memory_space=pl.ANY)] * n_o,
        scratch_shapes=side.sems(),
        input_output_aliases=dict(side.alias),
    )(*side.ins)


_WGEO = {"a_in": ("col", 1024, 1280), "a_out": ("row", 256, 1024), "b_in": ("row", 1024, 1552), "b_out": ("row", 512, 1024),
         "f0_in": ("col", 1024, 1408), "f1_in": ("col", 1024, 1408), "f0_out": ("row", 704, 1024), "f1_out": ("row", 704, 1024)}
_GGEO = dict(_WGEO, b_in=("row", 1552, 1024))


def _full_shape(geo):
    kind, r, cw = geo
    return (r, 4 * cw) if kind == "col" else (4 * r, cw)


def _times(i, step):
    return i * step if isinstance(i, int) else pl.multiple_of(i * step, step & -step)


def _win(ref, geo, j, h):
    kind, r, cw = geo
    hr = r // 2
    if kind == "col":
        return ref.at[pl.ds(_times(h, hr), hr), pl.ds(_times(j, cw), cw)]
    return ref.at[pl.ds(_times(2 * j + h, hr), hr), :]


def _half(ref, geo, h):
    hr = geo[1] // 2
    return ref.at[pl.ds(_times(h, hr), hr), :]


def _gather_side(first, second, sh, full):
    n1 = len(first)

    def plan(ins, outs):
        x, y, c = _here()
        m = 2 * x + y
        remote, local = [], []
        for u, src, dst_full in zip(first, ins[:n1], outs[:n1]):
            mine, dst = _half(src, _WGEO[u], c), _win(dst_full, _WGEO[u], m, c)
            local.append((mine, dst))
            remote.append((mine, dst, (x, y, 1 - c)))
            for k in (1, 2, 3):
                t = (m + k) % 4
                remote.append((mine, dst, (t // 2, t % 2, c)))
        for u, buf in zip(second, outs[n1:]):
            for k in (1, 2, 3):
                w_ = _win(buf, _WGEO[u], (m + k) % 4, c)
                remote.append((w_, w_, (x, y, 1 - c)))
        return remote, local

    return _Side([sh[u] for u in first] + [full[u] for u in second],
                 [(_full_shape(_WGEO[u]), BF) for u in first + second], plan, 4 * n1 + 3 * len(second), n1,
                 alias={n1 + i: n1 + i for i in range(len(second))})


def _pair_side(units, G):
    def plan(ins, outs):
        x, y, c = _here()
        return [(_win(gr, _GGEO[u], j, 1 - c), got.at[j], (x, y, 1 - c))
                for u, gr, got in zip(units, ins, outs) for j in range(4)], []

    return _Side([G[u] for u in units], [((4, _GGEO[u][1] // 2, _GGEO[u][2]), F32) for u in units], plan, 4 * len(units), 0)


def _pair_sums(units, G, gots):
    out = {}
    for u, got in zip(units, gots):
        blk = got.shape[1:]
        at = (lambda j: (lax.axis_index("c"), j)) if _GGEO[u][0] == "col" else (lambda j: (2 * j + lax.axis_index("c"), 0))
        slab = lambda j: (j, 0, 0)
        out[u] = _vcall(f"g_pair_sum_{u}", lambda a, b: (a + b[0])[None], (4,),
                        [(G[u], blk, at), (got, (1,) + blk, slab)], [(got.shape, BF, (1,) + blk, slab)])
    return out


def _chips_side(units, chip_sums):
    def plan(ins, outs):
        x, y, c = _here()
        m = 2 * x + y
        remote = []
        for s, got in zip(ins, outs):
            for k in (1, 2, 3):
                t = (m + k) % 4
                remote.append((s.at[t], got.at[k - 1], (t // 2, t % 2, c)))
        return remote, []

    return _Side([chip_sums[u] for u in units], [((3,) + chip_sums[u].shape[1:], BF) for u in units], plan,
                 3 * len(units), 0)


def _chip_sum(name, geo, chip_sums, got):
    _, r, cw = geo
    blk = (r // 2, cw)
    return _vcall(name, lambda a, b: ((a[0].astype(F32) + b[0].astype(F32)) + b[1].astype(F32)) + b[2].astype(F32), (1,),
                  [(chip_sums, (1,) + blk, lambda i: (2 * lax.axis_index("x") + lax.axis_index("y"), 0, 0)),
                   (got, (3,) + blk, lambda i: (0, 0, 0))],
                  [((r, cw), F32, blk, lambda i: (lax.axis_index("c"), 0))])


def _halves_side(units, shards):
    def plan(ins, outs):
        x, y, c = _here()
        return [(_half(o, _GGEO[u], c), _half(o, _GGEO[u], c), (x, y, 1 - c)) for u, o in zip(units, outs)], []

    return _Side([shards[u] for u in units], [(shards[u].shape, F32) for u in units], plan, len(units), 0,
                 alias={i: i for i in range(len(units))})


def _adam(name, w, g, m, v):
    rows, cols = w.shape
    tb = _pick(rows, (256, 128, 64, 8))

    def fn(wv, gv, mv, vv):
        m2 = ADAM_B1 * mv + (1.0 - ADAM_B1) * gv
        v2 = ADAM_B2 * vv + (1.0 - ADAM_B2) * jnp.square(gv)
        m_hat = m2 / (1.0 - ADAM_B1 ** ADAM_STEP)
        v_hat = v2 / (1.0 - ADAM_B2 ** ADAM_STEP)
        return -ADAM_LR * (m_hat / (jnp.sqrt(v_hat) + ADAM_EPS) + ADAM_WD * wv), m2, v2

    at = lambda i: (i, 0)
    return _vcall(name, fn, (rows // tb,), [(a, (tb, cols), at) for a in (w, g, m, v)],
                  [((rows, cols), F32, (tb, cols), at)] * 3)


def _pack(arrays, width, row_multiple, dtype):
    parts, offs, at = [], [], 0
    for a in arrays:
        flat = a.reshape(-1).astype(dtype)
        rows = -(-flat.shape[0] // (width * row_multiple)) * row_multiple
        parts.append(jnp.pad(flat, (0, rows * width - flat.shape[0])).reshape(rows, width))
        offs.append(at)
        at += rows
    return jnp.concatenate(parts, axis=0), offs


def _unpack(flat, shapes, offs):
    out = []
    for shp, at in zip(shapes, offs):
        n = 1
        for s in shp:
            n *= s
        rows = -(-n // flat.shape[1])
        out.append(flat[at:at + rows].reshape(-1)[:n].reshape(shp))
    return out


_BIG = ("a_w_in", "a_w_out", "b_w_in", "b_w_out", "ffn_w_in", "ffn_w_out")
_SMALL_SPLIT = ("b_conv_w", "b_conv_b", "b_norm_w", "ffn_conv_w")
_SMALL = ("norm1_w", "norm2_w", "a_lb_logits", "a_norm_w", "b_conv_w", "b_conv_b", "b_dt_bias", "b_a_log", "b_d_skip",
          "b_norm_w", "ffn_conv_w", "ffn_conv_b", "final_norm_w")
_ORDER = ("norm1_w", "norm2_w", "a_w_in", "a_lb_logits", "a_norm_w", "a_w_out", "b_w_in", "b_conv_w", "b_conv_b", "b_dt_bias",
          "b_a_log", "b_d_skip", "b_norm_w", "b_w_out", "ffn_w_in", "ffn_conv_w", "ffn_conv_b", "ffn_w_out", "final_norm_w")


def kernel(x, norm1_w, norm2_w, a_w_in, a_lb_logits, a_norm_w, a_w_out, b_w_in, b_conv_w, b_conv_b, b_dt_bias, b_a_log, b_d_skip, b_norm_w, b_w_out, ffn_w_in, ffn_conv_w, ffn_conv_b, ffn_w_out, final_norm_w, loss_target, m_norm1_w, m_norm2_w, m_a_w_in, m_a_lb_logits, m_a_norm_w, m_a_w_out, m_b_w_in, m_b_conv_w, m_b_conv_b, m_b_dt_bias, m_b_a_log, m_b_d_skip, m_b_norm_w, m_b_w_out, m_ffn_w_in, m_ffn_conv_w, m_ffn_conv_b, m_ffn_w_out, m_final_norm_w, v_norm1_w, v_norm2_w, v_a_w_in, v_a_lb_logits, v_a_norm_w, v_a_w_out, v_b_w_in, v_b_conv_w, v_b_conv_b, v_b_dt_bias, v_b_a_log, v_b_d_skip, v_b_norm_w, v_b_w_out, v_ffn_w_in, v_ffn_conv_w, v_ffn_conv_b, v_ffn_w_out, v_final_norm_w):
    w = dict(norm1_w=norm1_w, norm2_w=norm2_w, a_w_in=a_w_in, a_lb_logits=a_lb_logits, a_norm_w=a_norm_w, a_w_out=a_w_out,
             b_w_in=b_w_in, b_conv_w=b_conv_w, b_conv_b=b_conv_b, b_dt_bias=b_dt_bias, b_a_log=b_a_log, b_d_skip=b_d_skip,
             b_norm_w=b_norm_w, b_w_out=b_w_out, ffn_w_in=ffn_w_in, ffn_conv_w=ffn_conv_w, ffn_conv_b=ffn_conv_b,
             ffn_w_out=ffn_w_out, final_norm_w=final_norm_w)
    mom = dict(norm1_w=m_norm1_w, norm2_w=m_norm2_w, a_w_in=m_a_w_in, a_lb_logits=m_a_lb_logits, a_norm_w=m_a_norm_w,
               a_w_out=m_a_w_out, b_w_in=m_b_w_in, b_conv_w=m_b_conv_w, b_conv_b=m_b_conv_b, b_dt_bias=m_b_dt_bias,
               b_a_log=m_b_a_log, b_d_skip=m_b_d_skip, b_norm_w=m_b_norm_w, b_w_out=m_b_w_out, ffn_w_in=m_ffn_w_in,
               ffn_conv_w=m_ffn_conv_w, ffn_conv_b=m_ffn_conv_b, ffn_w_out=m_ffn_w_out, final_norm_w=m_final_norm_w)
    var = dict(norm1_w=v_norm1_w, norm2_w=v_norm2_w, a_w_in=v_a_w_in, a_lb_logits=v_a_lb_logits, a_norm_w=v_a_norm_w,
               a_w_out=v_a_w_out, b_w_in=v_b_w_in, b_conv_w=v_b_conv_w, b_conv_b=v_b_conv_b, b_dt_bias=v_b_dt_bias,
               b_a_log=v_b_a_log, b_d_skip=v_b_d_skip, b_norm_w=v_b_norm_w, b_w_out=v_b_w_out, ffn_w_in=v_ffn_w_in,
               ffn_conv_w=v_ffn_conv_w, ffn_conv_b=v_ffn_conv_b, ffn_w_out=v_ffn_w_out, final_norm_w=v_final_norm_w)
    chip = 2 * lax.axis_index("x") + lax.axis_index("y")

    sh = {"a_in": a_w_in[0], "a_out": a_w_out[0], "b_in": b_w_in[0], "b_out": b_w_out[0], "f0_in": ffn_w_in[0],
          "f1_in": ffn_w_in[1], "f0_out": ffn_w_out[0], "f1_out": ffn_w_out[1]}
    sh = {u: a.astype(BF) for u, a in sh.items()}
    small_shapes = [w[n].shape for n in _SMALL_SPLIT]
    spack, small_offs = _pack([w[n] for n in _SMALL_SPLIT], 128, 8, F32)
    sall = _allgather8("s_gather", spack)
    sshards = [_unpack(sall[2 * j], small_shapes, small_offs) for j in range(4)]
    sfull = {n: jnp.concatenate([sshards[j][i] for j in range(4)], axis=-1) for i, n in enumerate(_SMALL_SPLIT)}

    p = dict(
        norm1_w=norm1_w, norm2_w=norm2_w, a_lb_logits=a_lb_logits, a_norm_w=a_norm_w[0], final_norm_w=final_norm_w,
        b_conv_w=sfull["b_conv_w"][0], b_conv_b=sfull["b_conv_b"][0], b_norm_w=sfull["b_norm_w"],
        ffn_conv_w=sfull["ffn_conv_w"], ffn_conv_b=ffn_conv_b,
        b_dt_bias=jnp.pad(b_dt_bias.reshape(1, 2 * SSM_HEADS), ((0, 0), (0, 128 - 2 * SSM_HEADS))),
        b_a_log=jnp.pad(b_a_log.reshape(1, 2 * SSM_HEADS), ((0, 0), (0, 128 - 2 * SSM_HEADS))),
        b_d_skip=jnp.repeat(b_d_skip[0], SSM_HD)[None],
    )

    loss_row, dx, g, gs_ = _sequence_grads(x[0], loss_target[0], p, sh)
    grads = {"a_w_in": gs_["a_in"][None], "a_w_out": gs_["a_out"][None], "b_w_in": gs_["b_in"].T[None],
             "b_w_out": gs_["b_out"][None], "ffn_w_in": jnp.stack([gs_["f0_in"], gs_["f1_in"]]),
             "ffn_w_out": jnp.stack([gs_["f0_out"], gs_["f1_out"]])}

    gsmall = {
        "norm1_w": jnp.concatenate(g["norm1_w"], axis=0), "norm2_w": jnp.concatenate(g["norm2_w"], axis=0),
        "a_lb_logits": jnp.stack(g["a_lb_logits"]), "a_norm_w": g["a_norm_w"], "b_conv_w": g["b_conv_w"],
        "b_conv_b": g["b_conv_b"], "b_dt_bias": g["b_dt_bias"], "b_a_log": g["b_a_log"], "b_d_skip": g["b_d_skip"],
        "b_norm_w": g["b_norm_w"], "ffn_conv_w": jnp.stack(g["ffn_conv_w"]),
        "ffn_conv_b": jnp.concatenate(g["ffn_conv_b"], axis=0), "final_norm_w": g["final_norm_w"],
    }
    pieces = [gsmall[n] for n in _SMALL] + [loss_row]
    piece_shapes = [a.shape for a in pieces]
    gspack, gs_offs = _pack(pieces, 128, 8, F32)
    rows = gspack.shape[0]
    gsall = _allgather8("gs_gather", gspack)

    def sum8(a):
        r = a[0]
        for i in range(1, 8):
            r = r + a[i]
        return r

    gssum = _vcall("gs_sum", sum8, (1,), [(gsall, (8, rows, 128), lambda i: (0, 0, 0))],
                   [((rows, 128), F32, (rows, 128), lambda i: (0, 0))])
    gs = dict(zip(_SMALL + ("loss",), _unpack(gssum, piece_shapes, gs_offs)))
    loss = gs["loss"][0, 0]
    lb2 = gs["a_lb_logits"]
    small_grads = {
        "norm1_w": gs["norm1_w"], "norm2_w": gs["norm2_w"], "a_lb_logits": lb2[0] + lb2[1], "a_norm_w": gs["a_norm_w"],
        "b_dt_bias": gs["b_dt_bias"][:, :2 * SSM_HEADS].reshape(1, 2, SSM_HEADS),
        "b_a_log": gs["b_a_log"][:, :2 * SSM_HEADS].reshape(1, 2, SSM_HEADS),
        "b_d_skip": gs["b_d_skip"].reshape(1, SSM_HEADS, SSM_HD).sum(axis=-1),
        "ffn_conv_b": gs["ffn_conv_b"], "final_norm_w": gs["final_norm_w"][0],
        "b_conv_w": gs["b_conv_w"][None], "b_conv_b": gs["b_conv_b"], "b_norm_w": gs["b_norm_w"], "ffn_conv_w": gs["ffn_conv_w"],
    }
    for n in _SMALL_SPLIT:
        width = w[n].shape[-1]
        small_grads[n] = lax.dynamic_slice_in_dim(small_grads[n], chip * width, width, axis=small_grads[n].ndim - 1)
    grads.update(small_grads)

    delta, new_m, new_v = {}, {}, {}
    for n in _BIG:
        shp = w[n].shape
        two_d = (shp[0] * shp[1], shp[2])
        d_, m_, v_ = _adam(f"adam_{n}", w[n].reshape(two_d), grads[n].reshape(two_d), mom[n].reshape(two_d), var[n].reshape(two_d))
        delta[n], new_m[n], new_v[n] = d_.reshape(shp), m_.reshape(shp), v_.reshape(shp)
    s_shapes = [w[n].shape for n in _SMALL]
    packs = [_pack([src[n] for n in _SMALL], 128, 8, F32) for src in (w, grads, mom, var)]
    outs = _adam("adam_small", *[pk[0] for pk in packs])
    for res, dst in zip(outs, (delta, new_m, new_v)):
        dst.update(dict(zip(_SMALL, _unpack(res, s_shapes, packs[0][1]))))

    return (loss, dx[None], *[grads[n] for n in _ORDER], *[delta[n] for n in _ORDER],
            *[new_m[n] for n in _ORDER], *[new_v[n] for n in _ORDER])
```
